```python
import math, functools
import jax, jax.numpy as jnp
from jax import lax
import numpy as np


D_MODEL = 1024
BATCH = 16
SEQ = 2048
DEPTH = 4

GRID_W = 64
CTX_LEN = 256
MIX_W = D_MODEL
N_MIXERS = 4
BR_W = MIX_W // N_MIXERS
HEAD_DIM = 64
RET_HEADS = BR_W // HEAD_DIM
SG_HEADS = BR_W // HEAD_DIM
GDN_HEADS = BR_W // HEAD_DIM
RET_CHUNK = 128
SG_CHUNK = 128
GDN_CHUNK = 64
CONV_W = 3
ROPE_BASE = 10000.0
EPS = 1e-6
SEGMENTS = (BR_W,) * 15 + (GDN_HEADS,) * 4
IN_W = 15 * BR_W + 4 * GDN_HEADS

kernel_name = 'hybrid_parallel_mixer_dit_block'


def rms_norm(x, g):
    xf = x.astype(jnp.float32)
    y = xf * lax.rsqrt(jnp.mean(xf * xf, axis=-1, keepdims=True) + EPS)
    return (y * g.astype(jnp.float32)).astype(x.dtype)


def layer_norm_plain(x):
    xf = x.astype(jnp.float32)
    mu = jnp.mean(xf, axis=-1, keepdims=True)
    var = jnp.mean(jnp.square(xf - mu), axis=-1, keepdims=True)
    return ((xf - mu) * lax.rsqrt(var + EPS)).astype(x.dtype)


def l2_normalize(x):
    return x * lax.rsqrt(jnp.sum(x * x, axis=-1, keepdims=True) + EPS)


def dwconv_centred(x, w):
    k_w = w.shape[0]
    t = x.shape[1]
    xp = jnp.pad(x, ((0, 0), (k_w // 2, k_w // 2), (0, 0)))
    out = xp[:, 0:t] * w[0]
    for i in range(1, k_w):
        out = out + xp[:, i:i + t] * w[i]
    return out


def grid_rotary(x, row, col):
    half = x.shape[-1] // 2
    nf = half // 2
    inv = ROPE_BASE ** (-jnp.arange(nf, dtype=jnp.float32) / nf)

    def rot(xs, pos):
        ang = pos.astype(jnp.float32)[:, None] * inv
        cos, sin = jnp.cos(ang)[None, :, None, :], jnp.sin(ang)[None, :, None, :]
        x1, x2 = xs[..., :nf], xs[..., nf:]
        return jnp.concatenate([x1 * cos - x2 * sin, x1 * sin + x2 * cos], axis=-1)

    return jnp.concatenate([rot(x[..., :half], row), rot(x[..., half:], col)], axis=-1)


def split_proj(p):
    cuts = [int(i) for i in np.cumsum(SEGMENTS)[:-1]]
    return jnp.split(p, cuts, axis=-1)


def identity(a):
    return a


flip_time = functools.partial(jnp.flip, axis=1)


def retention_scan(q, k, v, log_gamma, s0, with_output):
    bsz, t, h, dk = k.shape
    n = t // RET_CHUNK

    def chunks(a):
        return a.reshape(bsz, n, RET_CHUNK, h, a.shape[-1]).transpose(1, 0, 3, 2, 4)

    pos = jnp.arange(RET_CHUNK, dtype=jnp.float32)
    lg = log_gamma[:, None]
    k_dec = jnp.exp((RET_CHUNK - 1.0 - pos) * lg)[..., None]
    c_dec = jnp.exp(RET_CHUNK * log_gamma)[:, None, None]
    kc, vc = chunks(k * dk ** -0.5), chunks(v)

    def update(s, ki, vi):
        return s * c_dec + jnp.einsum('bhcd,bhce->bhde', ki * k_dec, vi)

    if not with_output:
        s_fin, _ = lax.scan(lambda s, kv: (update(s, kv[0], kv[1]), None), s0, (kc, vc))
        return None, s_fin
    diff = pos[:, None] - pos[None, :]
    intra = jnp.exp(jnp.where(diff >= 0, diff * lg[..., None], -jnp.inf))
    q_dec = jnp.exp((pos + 1.0) * lg)[..., None]

    def step(s, inp):
        qi, ki, vi = inp
        scores = jnp.einsum('bhid,bhjd->bhij', qi, ki) * intra
        o = jnp.einsum('bhij,bhje->bhie', scores, vi) + jnp.einsum('bhid,bhde->bhie', qi, s) * q_dec
        return update(s, ki, vi), o

    s_fin, o = lax.scan(step, s0, (chunks(q), kc, vc))
    return o.transpose(1, 0, 3, 2, 4).reshape(bsz, t, h, -1), s_fin


def gated_delta_scan(q, k, v, g, beta, s0, with_output):
    bsz, t, h, dk = k.shape
    dv = v.shape[-1]
    n = t // GDN_CHUNK

    def chunks(a):
        return a.reshape(bsz, n, GDN_CHUNK, h, -1).transpose(1, 0, 3, 2, 4)

    kc, vc = chunks(k), chunks(v)
    bc = chunks(beta[..., None])
    gc = jnp.cumsum(chunks(g[..., None])[..., 0], axis=-1)
    idx = jnp.arange(GDN_CHUNK)
    diff = gc[..., :, None] - gc[..., None, :]
    decay = jnp.exp(jnp.where(idx[:, None] >= idx[None, :], diff, -jnp.inf))
    kb = kc * bc
    lower = jnp.where(idx[:, None] > idx[None, :],
                      jnp.einsum('nbhid,nbhjd->nbhij', kb, kc) * decay, 0.0)
    rhs = jnp.concatenate([vc * bc, kb * jnp.exp(gc)[..., None]], axis=-1)
    sol = lax.linalg.triangular_solve(lower, rhs, left_side=True, lower=True, unit_diagonal=True)
    uc, wc = sol[..., :dv], sol[..., dv:]
    g_last = gc[..., -1:]
    k_tail = kc * jnp.exp(g_last - gc)[..., None]
    c_dec = jnp.exp(g_last)[..., None]

    def update(s, ui, wi, kti, cdi):
        v_new = ui - jnp.einsum('bhck,bhkv->bhcv', wi, s)
        return s * cdi + jnp.einsum('bhck,bhcv->bhkv', kti, v_new), v_new

    if not with_output:
        s_fin, _ = lax.scan(lambda s, xs: (update(s, *xs)[0], None), s0, (uc, wc, k_tail, c_dec))
        return None, s_fin
    qc = chunks(q * dk ** -0.5)
    qd = qc * jnp.exp(gc)[..., None]

    def step(s, inp):
        qi, qdi, ki, di, ui, wi, kti, cdi = inp
        s_new, v_new = update(s, ui, wi, kti, cdi)
        intra = jnp.einsum('bhik,bhjk->bhij', qi, ki) * di
        o = jnp.einsum('bhck,bhkv->bhcv', qdi, s) + jnp.einsum('bhij,bhjv->bhiv', intra, v_new)
        return s_new, o

    s_fin, o = lax.scan(step, s0, (qc, qd, kc, decay, uc, wc, k_tail, c_dec))
    return o.transpose(1, 0, 3, 2, 4).reshape(bsz, t, h, dv), s_fin


def retention_mixer(lat, ctx, row, col, norm_g, ctx_out):
    def heads(a):
        return a.astype(jnp.float32).reshape(a.shape[0], a.shape[1], RET_HEADS, HEAD_DIM)

    ql, kl, vl = grid_rotary(heads(lat[0]), row, col), grid_rotary(heads(lat[1]), row, col), heads(lat[2])
    qc, kc, vc = heads(ctx[0]), heads(ctx[1]), heads(ctx[2])
    log_gamma = jnp.log(1.0 - 2.0 ** (-5.0 - jnp.arange(RET_HEADS, dtype=jnp.float32)))
    s0 = jnp.zeros((ql.shape[0], RET_HEADS, HEAD_DIM, HEAD_DIM), jnp.float32)
    o_lat, o_ctx = 0.0, 0.0
    for d in (identity, flip_time):
        oc, sc = retention_scan(d(qc), d(kc), d(vc), log_gamma, s0, ctx_out)
        ol, _ = retention_scan(d(ql), d(kl), d(vl), log_gamma, sc, True)
        o_lat = o_lat + d(ol)
        if ctx_out:
            o_ctx = o_ctx + d(oc)

    def finish(o, z):
        mu = jnp.mean(o, axis=-1, keepdims=True)
        var = jnp.mean(jnp.square(o - mu), axis=-1, keepdims=True)
        y = ((o - mu) * lax.rsqrt(var + EPS)).reshape(o.shape[0], o.shape[1], BR_W) * norm_g
        return (y * jax.nn.silu(z.astype(jnp.float32))).astype(z.dtype)

    return finish(o_lat, lat[3]), (finish(o_ctx, ctx[3]) if ctx_out else None)


def spatial_gating(u, v, z, w_s, b_s):
    bsz, t, _ = u.shape
    n = t // SG_CHUNK
    u = jax.nn.gelu(u)
    v = layer_norm_plain(jax.nn.gelu(v)).reshape(bsz, n, SG_CHUNK, SG_HEADS, HEAD_DIM)
    s = jnp.einsum('hij,bnjhd->bnihd', w_s, v) + b_s.T[:, :, None]
    return u * s.reshape(bsz, t, BR_W) * jax.nn.silu(z)


def short_conv(b, c, h, z, w):
    return b * dwconv_centred(c * h, w) * jax.nn.silu(z)


def gdn_mixer(lat, ctx, conv_w, a_log, dt_bias, norm_g, ctx_out):
    def prep(parts):
        q, k, v, _, a_f, a_b, b_f, b_b = parts
        qkv = jax.nn.silu(dwconv_centred(jnp.concatenate([q, k, v], axis=-1), conv_w)).astype(jnp.float32)
        bsz, t, _ = qkv.shape
        q, k, v = [a.reshape(bsz, t, GDN_HEADS, HEAD_DIM) for a in jnp.split(qkv, 3, axis=-1)]
        g = [-jnp.exp(a_log[i]) * jax.nn.softplus(a.astype(jnp.float32) + dt_bias[i])
             for i, a in enumerate((a_f, a_b))]
        beta = [jax.nn.sigmoid(b.astype(jnp.float32)) for b in (b_f, b_b)]
        return l2_normalize(q), l2_normalize(k), v, g, beta

    ql, kl, vl, gl, bl = prep(lat)
    qc, kc, vc, gcx, bcx = prep(ctx)
    s0 = jnp.zeros((ql.shape[0], GDN_HEADS, HEAD_DIM, HEAD_DIM), jnp.float32)
    o_lat, o_ctx = 0.0, 0.0
    for i, d in enumerate((identity, flip_time)):
        oc, sc = gated_delta_scan(d(qc), d(kc), d(vc), d(gcx[i]), d(bcx[i]), s0, ctx_out)
        ol, _ = gated_delta_scan(d(ql), d(kl), d(vl), d(gl[i]), d(bl[i]), sc, True)
        o_lat = o_lat + d(ol)
        if ctx_out:
            o_ctx = o_ctx + d(oc)

    def finish(o, z):
        y = o * lax.rsqrt(jnp.mean(o * o, axis=-1, keepdims=True) + EPS) * norm_g
        y = y.reshape(o.shape[0], o.shape[1], BR_W)
        return (y * jax.nn.silu(z.astype(jnp.float32))).astype(z.dtype)

    return finish(o_lat, lat[3]), (finish(o_ctx, ctx[3]) if ctx_out else None)


def token_mixers(p_lat, p_ctx, row, col, ret_norm_g, sg_w, sg_b, sc_conv_w,
                 gdn_conv_w, gdn_a_log, gdn_dt_bias, gdn_norm_g, ctx_out):
    lat, cx = split_proj(p_lat), split_proj(p_ctx)
    a_lat, a_ctx = retention_mixer(lat[0:4], cx[0:4], row, col, ret_norm_g, ctx_out)
    d_lat, d_ctx = gdn_mixer(lat[11:19], cx[11:19], gdn_conv_w, gdn_a_log, gdn_dt_bias, gdn_norm_g, ctx_out)
    y_lat = jnp.concatenate([a_lat, spatial_gating(*lat[4:7], sg_w, sg_b),
                             short_conv(*lat[7:11], sc_conv_w), d_lat], axis=-1).astype(p_lat.dtype)
    if not ctx_out:
        return y_lat, None
    y_ctx = jnp.concatenate([a_ctx, spatial_gating(*cx[4:7], sg_w, sg_b),
                             short_conv(*cx[7:11], sc_conv_w), d_ctx], axis=-1).astype(p_ctx.dtype)
    return y_lat, y_ctx


def _fwd_setup_inputs(seed: int = 0) -> dict:
    key = jax.random.key(seed)
    ks = jax.random.split(key, 18)

    def nrm(k, shape, s):
        return jax.random.normal(k, shape, jnp.float32) * s

    dt = jnp.exp(jax.random.uniform(ks[16], (DEPTH, 2, GDN_HEADS), jnp.float32,
                                    minval=math.log(1e-3), maxval=math.log(1e-1)))
    return {
        'x': nrm(ks[0], (BATCH, SEQ, D_MODEL), 1.0),
        'c': nrm(ks[1], (BATCH, D_MODEL), 1.0),
        'ctx': nrm(ks[2], (BATCH, CTX_LEN, D_MODEL), 1.0),
        'c_ctx': nrm(ks[3], (D_MODEL,), 1.0),
        'w_mod': nrm(ks[4], (DEPTH, D_MODEL, 3 * D_MODEL), 0.5 * D_MODEL ** -0.5),
        'b_mod': nrm(ks[5], (DEPTH, 3 * D_MODEL), 0.02),
        'g_pre': 1.0 + nrm(ks[6], (DEPTH, D_MODEL), 0.02),
        'g_post': 1.0 + nrm(ks[7], (DEPTH, D_MODEL), 0.02),
        'w_in': nrm(ks[8], (DEPTH, D_MODEL, IN_W), D_MODEL ** -0.5),
        'w_out': nrm(ks[9], (DEPTH, MIX_W, D_MODEL), MIX_W ** -0.5),
        'ret_norm_g': 1.0 + nrm(ks[10], (DEPTH, BR_W), 0.02),
        'sg_w': nrm(ks[11], (DEPTH, SG_HEADS, SG_CHUNK, SG_CHUNK), SG_CHUNK ** -0.5),
        'sg_b': 1.0 + nrm(ks[12], (DEPTH, SG_HEADS, SG_CHUNK), 0.1),
        'sc_conv_w': nrm(ks[13], (DEPTH, CONV_W, BR_W), CONV_W ** -0.5),
        'gdn_conv_w': nrm(ks[14], (DEPTH, CONV_W, 3 * BR_W), CONV_W ** -0.5),
        'gdn_a_log': jnp.log(jax.random.uniform(ks[15], (DEPTH, 2, GDN_HEADS), jnp.float32, minval=1.0, maxval=16.0)),
        'gdn_dt_bias': dt + jnp.log(-jnp.expm1(-dt)),
        'gdn_norm_g': 1.0 + nrm(ks[17], (DEPTH, HEAD_DIM), 0.02),
    }


def _fwd_reference(x, c, ctx, c_ctx, w_mod, b_mod, g_pre, g_post, w_in, w_out, ret_norm_g, sg_w, sg_b,
              sc_conv_w, gdn_conv_w, gdn_a_log, gdn_dt_bias, gdn_norm_g):
    rows = x.shape[1] // GRID_W
    row = jnp.repeat(jnp.arange(rows), GRID_W)
    col = jnp.tile(jnp.arange(GRID_W), rows)
    silu_c = jax.nn.silu(c)
    silu_cc = jax.nn.silu(c_ctx)
    for l in range(DEPTH):
        ctx_out = l < DEPTH - 1
        shift, scale, gate = jnp.split(silu_c @ w_mod[l] + b_mod[l], 3, axis=-1)
        shift_c, scale_c, gate_c = jnp.split(silu_cc @ w_mod[l] + b_mod[l], 3, axis=-1)
        h = rms_norm(x, g_pre[l]) * (1.0 + scale[:, None]) + shift[:, None]
        hc = rms_norm(ctx, g_pre[l]) * (1.0 + scale_c) + shift_c
        y, yc = token_mixers(h @ w_in[l], hc @ w_in[l], row, col, ret_norm_g[l], sg_w[l], sg_b[l],
                             sc_conv_w[l], gdn_conv_w[l], gdn_a_log[l], gdn_dt_bias[l], gdn_norm_g[l], ctx_out)
        x = x + gate[:, None] * rms_norm(y @ w_out[l], g_post[l])
        if ctx_out:
            ctx = ctx + gate_c * rms_norm(yc @ w_out[l], g_post[l])
    return x


import jax as _jax
import jax.numpy as _jnp

TWIN_FORMAT = 'train_step'
FWD_PARAMS = ['x', 'c', 'ctx', 'c_ctx', 'w_mod', 'b_mod', 'g_pre', 'g_post', 'w_in', 'w_out', 'ret_norm_g', 'sg_w', 'sg_b', 'sc_conv_w', 'gdn_conv_w', 'gdn_a_log', 'gdn_dt_bias', 'gdn_norm_g']
TWIN_WEIGHTS = ['c_ctx', 'w_mod', 'b_mod', 'g_pre', 'g_post', 'w_in', 'w_out', 'ret_norm_g', 'sg_w', 'sg_b', 'sc_conv_w', 'gdn_conv_w', 'gdn_a_log', 'gdn_dt_bias', 'gdn_norm_g']
TWIN_DIFF_INPUT = 'x'
TWIN_INPUTS = ['x', 'c', 'ctx', 'c_ctx', 'w_mod', 'b_mod', 'g_pre', 'g_post', 'w_in', 'w_out', 'ret_norm_g', 'sg_w', 'sg_b', 'sc_conv_w', 'gdn_conv_w', 'gdn_a_log', 'gdn_dt_bias', 'gdn_norm_g', 'loss_target', 'm_c_ctx', 'm_w_mod', 'm_b_mod', 'm_g_pre', 'm_g_post', 'm_w_in', 'm_w_out', 'm_ret_norm_g', 'm_sg_w', 'm_sg_b', 'm_sc_conv_w', 'm_gdn_conv_w', 'm_gdn_a_log', 'm_gdn_dt_bias', 'm_gdn_norm_g', 'v_c_ctx', 'v_w_mod', 'v_b_mod', 'v_g_pre', 'v_g_post', 'v_w_in', 'v_w_out', 'v_ret_norm_g', 'v_sg_w', 'v_sg_b', 'v_sc_conv_w', 'v_gdn_conv_w', 'v_gdn_a_log', 'v_gdn_dt_bias', 'v_gdn_norm_g']
TWIN_OUTPUTS = ['loss', 'grad_x', 'grad_c_ctx', 'grad_w_mod', 'grad_b_mod', 'grad_g_pre', 'grad_g_post', 'grad_w_in', 'grad_w_out', 'grad_ret_norm_g', 'grad_sg_w', 'grad_sg_b', 'grad_sc_conv_w', 'grad_gdn_conv_w', 'grad_gdn_a_log', 'grad_gdn_dt_bias', 'grad_gdn_norm_g', 'delta_c_ctx', 'delta_w_mod', 'delta_b_mod', 'delta_g_pre', 'delta_g_post', 'delta_w_in', 'delta_w_out', 'delta_ret_norm_g', 'delta_sg_w', 'delta_sg_b', 'delta_sc_conv_w', 'delta_gdn_conv_w', 'delta_gdn_a_log', 'delta_gdn_dt_bias', 'delta_gdn_norm_g', 'new_m_c_ctx', 'new_m_w_mod', 'new_m_b_mod', 'new_m_g_pre', 'new_m_g_post', 'new_m_w_in', 'new_m_w_out', 'new_m_ret_norm_g', 'new_m_sg_w', 'new_m_sg_b', 'new_m_sc_conv_w', 'new_m_gdn_conv_w', 'new_m_gdn_a_log', 'new_m_gdn_dt_bias', 'new_m_gdn_norm_g', 'new_v_c_ctx', 'new_v_w_mod', 'new_v_b_mod', 'new_v_g_pre', 'new_v_g_post', 'new_v_w_in', 'new_v_w_out', 'new_v_ret_norm_g', 'new_v_sg_w', 'new_v_sg_b', 'new_v_sc_conv_w', 'new_v_gdn_conv_w', 'new_v_gdn_a_log', 'new_v_gdn_dt_bias', 'new_v_gdn_norm_g']
TWIN_LEAF_KINDS = {'loss': 'loss', 'grad_x': 'grad_x', 'grad_c_ctx': 'grad_w', 'grad_w_mod': 'grad_w', 'grad_b_mod': 'grad_w', 'grad_g_pre': 'grad_w', 'grad_g_post': 'grad_w', 'grad_w_in': 'grad_w', 'grad_w_out': 'grad_w', 'grad_ret_norm_g': 'grad_w', 'grad_sg_w': 'grad_w', 'grad_sg_b': 'grad_w', 'grad_sc_conv_w': 'grad_w', 'grad_gdn_conv_w': 'grad_w', 'grad_gdn_a_log': 'grad_w', 'grad_gdn_dt_bias': 'grad_w', 'grad_gdn_norm_g': 'grad_w', 'delta_c_ctx': 'delta_w', 'delta_w_mod': 'delta_w', 'delta_b_mod': 'delta_w', 'delta_g_pre': 'delta_w', 'delta_g_post': 'delta_w', 'delta_w_in': 'delta_w', 'delta_w_out': 'delta_w', 'delta_ret_norm_g': 'delta_w', 'delta_sg_w': 'delta_w', 'delta_sg_b': 'delta_w', 'delta_sc_conv_w': 'delta_w', 'delta_gdn_conv_w': 'delta_w', 'delta_gdn_a_log': 'delta_w', 'delta_gdn_dt_bias': 'delta_w', 'delta_gdn_norm_g': 'delta_w', 'new_m_c_ctx': 'new_m', 'new_m_w_mod': 'new_m', 'new_m_b_mod': 'new_m', 'new_m_g_pre': 'new_m', 'new_m_g_post': 'new_m', 'new_m_w_in': 'new_m', 'new_m_w_out': 'new_m', 'new_m_ret_norm_g': 'new_m', 'new_m_sg_w': 'new_m', 'new_m_sg_b': 'new_m', 'new_m_sc_conv_w': 'new_m', 'new_m_gdn_conv_w': 'new_m', 'new_m_gdn_a_log': 'new_m', 'new_m_gdn_dt_bias': 'new_m', 'new_m_gdn_norm_g': 'new_m', 'new_v_c_ctx': 'new_v', 'new_v_w_mod': 'new_v', 'new_v_b_mod': 'new_v', 'new_v_g_pre': 'new_v', 'new_v_g_post': 'new_v', 'new_v_w_in': 'new_v', 'new_v_w_out': 'new_v', 'new_v_ret_norm_g': 'new_v', 'new_v_sg_w': 'new_v', 'new_v_sg_b': 'new_v', 'new_v_sc_conv_w': 'new_v', 'new_v_gdn_conv_w': 'new_v', 'new_v_gdn_a_log': 'new_v', 'new_v_gdn_dt_bias': 'new_v', 'new_v_gdn_norm_g': 'new_v'}


def _forward(args):
    return _fwd_reference(*[args[k] for k in FWD_PARAMS])


def _output_shape():
    out = _jax.eval_shape(lambda: _forward(_fwd_setup_inputs(0)))
    return out.shape, out.dtype

N_MICROBATCH = 1
ADAM_LR = 0.001
ADAM_B1 = 0.9
ADAM_B2 = 0.999
ADAM_EPS = 1e-08
ADAM_WD = 0.01
ADAM_STEP = 10
PER_EXAMPLE_BATCH_AXIS = {'x': 0, 'c': 0, 'ctx': 0, 'loss_target': 0}
SHARED_INPUTS = []
_WEIGHT_DTYPES = {'c_ctx': _jnp.float32, 'w_mod': _jnp.float32, 'b_mod': _jnp.float32, 'g_pre': _jnp.float32, 'g_post': _jnp.float32, 'w_in': _jnp.float32, 'w_out': _jnp.float32, 'ret_norm_g': _jnp.float32, 'sg_w': _jnp.float32, 'sg_b': _jnp.float32, 'sc_conv_w': _jnp.float32, 'gdn_conv_w': _jnp.float32, 'gdn_a_log': _jnp.float32, 'gdn_dt_bias': _jnp.float32, 'gdn_norm_g': _jnp.float32}
MOMENT_SCALE = {'c_ctx': 6.099523e-02, 'w_mod': 1.670417e+00, 'b_mod': 3.155659e+00, 'g_pre': 2.047268e-01, 'g_post': 3.734847e+00, 'w_in': 1.135718e-01, 'w_out': 1.310111e-01, 'ret_norm_g': 1.243157e-01, 'sg_w': 4.820657e-02, 'sg_b': 4.863113e-02, 'sc_conv_w': 1.220990e-01, 'gdn_conv_w': 1.035188e-01, 'gdn_a_log': 1.931139e-01, 'gdn_dt_bias': 1.928374e-01, 'gdn_norm_g': 3.095813e-01}


def _to_microbatches(a, axis):
    t = _jnp.moveaxis(a, axis, 0)
    t = t.reshape((N_MICROBATCH, t.shape[0] // N_MICROBATCH) + t.shape[1:])
    return _jnp.moveaxis(t, 1, axis + 1)


def setup_inputs(seed: int = 0) -> dict:
    inp = _fwd_setup_inputs(seed)
    key = _jax.random.fold_in(_jax.random.key(seed), 7919)
    shape, _ = _output_shape()
    out = dict(inp)
    out["loss_target"] = _jax.random.normal(_jax.random.fold_in(key, 0), shape, _jnp.float32)
    for i, name in enumerate(TWIN_WEIGHTS):
        w = inp[name].astype(_jnp.float32)
        if MOMENT_SCALE is None:
            s = _jnp.sqrt(_jnp.mean(_jnp.square(w)) + 1e-30)
        else:
            s = MOMENT_SCALE[name]
        km, kv = _jax.random.split(_jax.random.fold_in(key, i + 1))
        out[name] = w
        out["m_" + name] = s * _jax.random.normal(km, w.shape, _jnp.float32)
        out["v_" + name] = (s * s) * _jax.random.uniform(kv, w.shape, _jnp.float32, 0.5, 1.5)
    if N_MICROBATCH > 1:
        for name, axis in PER_EXAMPLE_BATCH_AXIS.items():
            out[name] = _to_microbatches(out[name], axis)
    return {'x': out['x'], 'c': out['c'], 'ctx': out['ctx'], 'c_ctx': out['c_ctx'], 'w_mod': out['w_mod'], 'b_mod': out['b_mod'], 'g_pre': out['g_pre'], 'g_post': out['g_post'], 'w_in': out['w_in'], 'w_out': out['w_out'], 'ret_norm_g': out['ret_norm_g'], 'sg_w': out['sg_w'], 'sg_b': out['sg_b'], 'sc_conv_w': out['sc_conv_w'], 'gdn_conv_w': out['gdn_conv_w'], 'gdn_a_log': out['gdn_a_log'], 'gdn_dt_bias': out['gdn_dt_bias'], 'gdn_norm_g': out['gdn_norm_g'], 'loss_target': out['loss_target'], 'm_c_ctx': out['m_c_ctx'], 'm_w_mod': out['m_w_mod'], 'm_b_mod': out['m_b_mod'], 'm_g_pre': out['m_g_pre'], 'm_g_post': out['m_g_post'], 'm_w_in': out['m_w_in'], 'm_w_out': out['m_w_out'], 'm_ret_norm_g': out['m_ret_norm_g'], 'm_sg_w': out['m_sg_w'], 'm_sg_b': out['m_sg_b'], 'm_sc_conv_w': out['m_sc_conv_w'], 'm_gdn_conv_w': out['m_gdn_conv_w'], 'm_gdn_a_log': out['m_gdn_a_log'], 'm_gdn_dt_bias': out['m_gdn_dt_bias'], 'm_gdn_norm_g': out['m_gdn_norm_g'], 'v_c_ctx': out['v_c_ctx'], 'v_w_mod': out['v_w_mod'], 'v_b_mod': out['v_b_mod'], 'v_g_pre': out['v_g_pre'], 'v_g_post': out['v_g_post'], 'v_w_in': out['v_w_in'], 'v_w_out': out['v_w_out'], 'v_ret_norm_g': out['v_ret_norm_g'], 'v_sg_w': out['v_sg_w'], 'v_sg_b': out['v_sg_b'], 'v_sc_conv_w': out['v_sc_conv_w'], 'v_gdn_conv_w': out['v_gdn_conv_w'], 'v_gdn_a_log': out['v_gdn_a_log'], 'v_gdn_dt_bias': out['v_gdn_dt_bias'], 'v_gdn_norm_g': out['v_gdn_norm_g']}


def _loss(weights, diff, rest, loss_target):
    with _jax.named_scope("forward"):
        args = {**rest, TWIN_DIFF_INPUT: diff, **{k: w.astype(_WEIGHT_DTYPES[k]) for k, w in weights.items()}}
        y = _forward(args)
    with _jax.named_scope("loss_head"):
        err = _jnp.square(y.astype(_jnp.float32) - loss_target)
        return 0.5 * _jnp.sum(_jnp.mean(err, axis=-1)) if err.ndim else 0.5 * err


def _adamw(w, g, m, v):
    m = ADAM_B1 * m + (1.0 - ADAM_B1) * g
    v = ADAM_B2 * v + (1.0 - ADAM_B2) * _jnp.square(g)
    m_hat = m / (1.0 - ADAM_B1 ** ADAM_STEP)
    v_hat = v / (1.0 - ADAM_B2 ** ADAM_STEP)
    delta = -ADAM_LR * (m_hat / (_jnp.sqrt(v_hat) + ADAM_EPS) + ADAM_WD * w)
    return delta, m, v


def reference(x, c, ctx, c_ctx, w_mod, b_mod, g_pre, g_post, w_in, w_out, ret_norm_g, sg_w, sg_b, sc_conv_w, gdn_conv_w, gdn_a_log, gdn_dt_bias, gdn_norm_g, loss_target, m_c_ctx, m_w_mod, m_b_mod, m_g_pre, m_g_post, m_w_in, m_w_out, m_ret_norm_g, m_sg_w, m_sg_b, m_sc_conv_w, m_gdn_conv_w, m_gdn_a_log, m_gdn_dt_bias, m_gdn_norm_g, v_c_ctx, v_w_mod, v_b_mod, v_g_pre, v_g_post, v_w_in, v_w_out, v_ret_norm_g, v_sg_w, v_sg_b, v_sc_conv_w, v_gdn_conv_w, v_gdn_a_log, v_gdn_dt_bias, v_gdn_norm_g):
    given = dict(x=x, c=c, ctx=ctx, c_ctx=c_ctx, w_mod=w_mod, b_mod=b_mod, g_pre=g_pre, g_post=g_post, w_in=w_in, w_out=w_out, ret_norm_g=ret_norm_g, sg_w=sg_w, sg_b=sg_b, sc_conv_w=sc_conv_w, gdn_conv_w=gdn_conv_w, gdn_a_log=gdn_a_log, gdn_dt_bias=gdn_dt_bias, gdn_norm_g=gdn_norm_g, loss_target=loss_target, m_c_ctx=m_c_ctx, m_w_mod=m_w_mod, m_b_mod=m_b_mod, m_g_pre=m_g_pre, m_g_post=m_g_post, m_w_in=m_w_in, m_w_out=m_w_out, m_ret_norm_g=m_ret_norm_g, m_sg_w=m_sg_w, m_sg_b=m_sg_b, m_sc_conv_w=m_sc_conv_w, m_gdn_conv_w=m_gdn_conv_w, m_gdn_a_log=m_gdn_a_log, m_gdn_dt_bias=m_gdn_dt_bias, m_gdn_norm_g=m_gdn_norm_g, v_c_ctx=v_c_ctx, v_w_mod=v_w_mod, v_b_mod=v_b_mod, v_g_pre=v_g_pre, v_g_post=v_g_post, v_w_in=v_w_in, v_w_out=v_w_out, v_ret_norm_g=v_ret_norm_g, v_sg_w=v_sg_w, v_sg_b=v_sg_b, v_sc_conv_w=v_sc_conv_w, v_gdn_conv_w=v_gdn_conv_w, v_gdn_a_log=v_gdn_a_log, v_gdn_dt_bias=v_gdn_dt_bias, v_gdn_norm_g=v_gdn_norm_g)
    weights = {n: given[n] for n in TWIN_WEIGHTS}
    shared = {n: given[n] for n in SHARED_INPUTS}
    per_example = {n: given[n] for n in ['x', 'c', 'ctx']}
    grad_fn = _jax.value_and_grad(_loss, argnums=(0, 1))

    def one_microbatch(ex, loss_target):
        ex = dict(ex)
        diff = ex.pop(TWIN_DIFF_INPUT)
        return grad_fn(weights, diff, {**shared, **ex}, loss_target)

    if N_MICROBATCH == 1:
        loss, (grad_w, grad_x) = one_microbatch(per_example, given["loss_target"])
    else:
        def body(carry, xs):
            loss_sum, grad_sum = carry
            l_k, (gw_k, gx_k) = one_microbatch(xs[0], xs[1])
            with _jax.named_scope("update"):
                return (loss_sum + l_k, _jax.tree.map(_jnp.add, grad_sum, gw_k)), gx_k

        init = (_jnp.zeros((), _jnp.float32), _jax.tree.map(_jnp.zeros_like, weights))
        (loss, grad_w), grad_x = _jax.lax.scan(body, init, (per_example, given["loss_target"]))
    with _jax.named_scope("update"):
        delta_w, new_m, new_v = {}, {}, {}
        for n in TWIN_WEIGHTS:
            delta_w[n], new_m[n], new_v[n] = _adamw(weights[n], grad_w[n], given["m_" + n], given["v_" + n])
    return (loss, grad_x, *[grad_w[n] for n in TWIN_WEIGHTS], *[delta_w[n] for n in TWIN_WEIGHTS],
            *[new_m[n] for n in TWIN_WEIGHTS], *[new_v[n] for n in TWIN_WEIGHTS])
```

```python
import functools

import jax
import jax.numpy as jnp
import numpy as np
from jax import lax
from jax.experimental import pallas as pl
from jax.experimental.pallas import tpu as pltpu

f32 = jnp.float32
bf16 = jnp.bfloat16
HI = lax.Precision.HIGHEST
MESH = pl.DeviceIdType.MESH

EPS = 1e-6
D = 1024
NH = 4
HD = 64
BW = NH * HD
RC = 128
GC = 64
GRID_W = 64
ROPE_BASE = 10000.0
IN_W = 15 * BW + 16
PW = 4096
GATE_COL = 15 * BW
N_CHIPS = 4
N_DEV = 8
TM = 256
ADAM_LR, ADAM_B1, ADAM_B2, ADAM_EPS, ADAM_WD, ADAM_STEP = 0.001, 0.9, 0.999, 1e-08, 0.01, 10
LANE_HEAD = np.arange(BW) // HD
VMEM_BIG = 56 * 1024 * 1024


def _dot(a, b, precision=None):
    return jnp.dot(a, b, precision=precision, preferred_element_type=f32)


def _dot_nt(a, b, precision=None):
    return lax.dot_general(a, b, (((1,), (1,)), ((), ())), precision=precision, preferred_element_type=f32)


def _dot_tn(a, b, precision=None):
    return lax.dot_general(a, b, (((0,), (0,)), ((), ())), precision=precision, preferred_element_type=f32)


def _sds(shape, dtype=f32):
    return jax.ShapeDtypeStruct(shape, dtype)


def _cparams(sem=None, vmem=None):
    kw = {}
    if sem is not None:
        kw["dimension_semantics"] = sem
    if vmem is not None:
        kw["vmem_limit_bytes"] = vmem
    return pltpu.CompilerParams(**kw)


def _full(shape):
    n = len(shape)
    return pl.BlockSpec(shape, lambda *_: (0,) * n)


def _head_masks():
    return np.stack([(LANE_HEAD == h).astype(np.float32)[None, :] for h in range(NH)])


def _block_diag():
    return (LANE_HEAD[:, None] == LANE_HEAD[None, :]).astype(np.float32)


def _tau(c, d):
    return np.arange(c) if d == 0 else c - 1 - np.arange(c)


def _ret_consts():
    lg = np.log(1.0 - 2.0 ** (-5.0 - np.arange(NH)))
    intra = np.zeros((2, NH, RC, RC)); qdec = np.zeros((2, RC, BW)); kdec = np.zeros((2, RC, BW))
    for d in range(2):
        t = _tau(RC, d)
        diff = t[:, None] - t[None, :]
        for h in range(NH):
            intra[d, h] = np.where(diff >= 0, np.exp(np.maximum(diff, 0) * lg[h]), 0.0)
        qdec[d] = np.exp((t[:, None] + 1.0) * lg[LANE_HEAD][None, :])
        kdec[d] = np.exp((RC - 1.0 - t[:, None]) * lg[LANE_HEAD][None, :])
    cd = np.exp(RC * lg[LANE_HEAD])[:, None] * np.ones((1, BW))
    return [jnp.asarray(a, f32) for a in (intra, qdec, kdec, cd)]


def _rope_tables(t_lat, t_ctx):
    nf = HD // 4
    inv = ROPE_BASE ** (-np.arange(nf) / nf)
    pos = np.arange(t_lat)
    ang_r = (pos // GRID_W)[:, None] * inv[None, :]
    ang_c = (pos % GRID_W)[:, None] * inv[None, :]
    ang = np.concatenate([ang_r, ang_r, ang_c, ang_c], axis=1)
    sign = np.concatenate([-np.ones(nf), np.ones(nf), -np.ones(nf), np.ones(nf)])
    cos = np.tile(np.cos(ang), (1, NH)); sins = np.tile(np.sin(ang) * sign, (1, NH))
    cos = np.concatenate([np.ones((t_ctx, BW)), cos]); sins = np.concatenate([np.zeros((t_ctx, BW)), sins])
    return jnp.asarray(cos, f32), jnp.asarray(sins, f32)


def _gdn_consts():
    tmask = np.zeros((2, GC, GC)); strict = np.zeros((2, GC, GC))
    exp_g = np.zeros((2, 128, BW)); exp_b = np.zeros((2, 128, BW)); onehot = np.zeros((2, NH, 1, 128))
    for d in range(2):
        t = _tau(GC, d)
        tmask[d] = (t[:, None] >= t[None, :])
        strict[d] = (t[:, None] > t[None, :])
        for h in range(NH):
            exp_g[d, 4 * d + h, LANE_HEAD == h] = 1.0
            exp_b[d, 8 + 4 * d + h, LANE_HEAD == h] = 1.0
            onehot[d, h, 0, 4 * d + h] = 1.0
    exp_gt = np.transpose(exp_g, (0, 2, 1))
    return [jnp.asarray(a, f32) for a in (tmask, strict, exp_g, exp_b, exp_gt, onehot)]


def _swap16(x):
    lane = lax.broadcasted_iota(jnp.int32, x.shape, x.ndim - 1)
    n = x.shape[-1]
    return jnp.where(lane % 32 < 16, pltpu.roll(x, n - 16, axis=x.ndim - 1), pltpu.roll(x, 16, axis=x.ndim - 1))


@jax.custom_vjp
def _rot(x, cos, sins):
    return x * cos + _swap16(x) * sins


def _rot_fwd(x, cos, sins):
    return _rot(x, cos, sins), (cos, sins)


def _rot_bwd(res, g):
    cos, sins = res
    return g * cos + _swap16(g * sins), jnp.zeros_like(cos), jnp.zeros_like(sins)


_rot.defvjp(_rot_fwd, _rot_bwd)


def _silu(z):
    return z * jax.nn.sigmoid(z)


def _head_sum(x, bd):
    return _dot(x, bd, precision=HI)


def _ret_step(s, q, k, v, cos, sins, intra, qdec, kdec, cd, bd, hm):
    qr = _rot(q, cos, sins)
    kr = _rot(k, cos, sins) * (HD ** -0.5)
    o = _dot(qr * qdec, s)
    for h in range(NH):
        sc = _dot_nt(qr * hm[h], kr) * intra[h]
        o = o + _dot(sc, v) * hm[h]
    s_new = s * cd + bd * _dot_tn(kr * kdec, v)
    return s_new, o


def _ret_finish(o_f, o_b, z, norm_g, bd):
    o = o_f + o_b
    mu = _head_sum(o, bd) * (1.0 / HD)
    xc = o - mu
    var = _head_sum(xc * xc, bd) * (1.0 / HD)
    return xc * lax.rsqrt(var + EPS) * norm_g * _silu(z)


def _softplus(x):
    return jnp.maximum(x, 0.0) + jnp.log(1.0 + jnp.exp(-jnp.abs(x)))


def _gdn_step(s, q, k, v, gate, alog, dtb, tmask, strict, exp_g, exp_b, exp_gt, onehot, bd, hm):
    c = q.shape[0]
    qn = q * lax.rsqrt(_head_sum(q * q, bd) + EPS)
    kn = k * lax.rsqrt(_head_sum(k * k, bd) + EPS)
    g = -jnp.exp(alog) * _softplus(gate + dtb)
    beta = jax.nn.sigmoid(gate)
    gc = _dot(tmask, g, precision=HI)
    glast = jnp.sum(g, axis=0, keepdims=True)
    gc_l = _dot(gc, exp_g, precision=HI)
    glast_l = _dot(glast, exp_g, precision=HI)
    beta_l = _dot(beta, exp_b, precision=HI)
    eg = jnp.exp(gc_l)
    kb = kn * beta_l
    vb = v * beta_l
    kbg = kb * eg
    qs = qn * (HD ** -0.5)
    eye = (lax.broadcasted_iota(jnp.int32, (c, c), 0) == lax.broadcasted_iota(jnp.int32, (c, c), 1)).astype(f32)
    u = jnp.zeros_like(v)
    w = jnp.zeros_like(v)
    decs = []
    for h in range(NH):
        col = jnp.sum(gc * onehot[h], axis=-1, keepdims=True)
        rowm = _dot_nt(jnp.broadcast_to(onehot[h], (c, 128)), gc, precision=HI)
        dec = jnp.exp(jnp.where(tmask > 0, col - rowm, -1e30))
        decs.append(dec)
        m = -(_dot_nt(kb * hm[h], kn) * dec * strict)
        inv = eye + m
        p = m
        for _ in range(5):
            p = _dot(p, p, precision=HI)
            inv = inv + _dot(inv, p, precision=HI)
        u = u + _dot(inv, vb, precision=HI) * hm[h]
        w = w + _dot(inv, kbg, precision=HI) * hm[h]
    v_new = u - _dot(w, s)
    k_tail = kn * jnp.exp(glast_l - gc_l)
    cdec = jnp.sum(exp_gt * jnp.exp(glast), axis=-1, keepdims=True)
    s_new = s * cdec + bd * _dot_tn(k_tail, v_new)
    o = _dot(qs * eg, s)
    for h in range(NH):
        o = o + _dot(_dot_nt(qs * hm[h], kn) * decs[h], v_new) * hm[h]
    return s_new, o


def _gdn_finish(o_f, o_b, z, norm_g, bd):
    o = o_f + o_b
    ms = _head_sum(o * o, bd) * (1.0 / HD)
    return o * lax.rsqrt(ms + EPS) * norm_g * _silu(z)


def _gelu(x):
    return 0.5 * x * (1.0 + jnp.tanh(0.7978845608028654 * (x + 0.044715 * (x * x * x))))


def _sg_chunk(u, v, z, w, b, hm4):
    u = _gelu(u)
    gv = _gelu(v)
    mu = jnp.mean(gv, axis=-1, keepdims=True)
    xc = gv - mu
    var = jnp.mean(xc * xc, axis=-1, keepdims=True)
    vn = xc * lax.rsqrt(var + EPS)
    s = _dot_tn(b, hm4, precision=HI)
    for h in range(NH):
        s = s + _dot(w[h], vn) * hm4[h:h + 1]
    return u * s * _silu(z)


def _make_shifts(t_ctx, n):
    def dn(x):
        t = lax.broadcasted_iota(jnp.int32, x.shape, 0)
        return jnp.where((t != 0) & (t != t_ctx), pltpu.roll(x, 1, axis=0), 0.0)

    def up(x):
        t = lax.broadcasted_iota(jnp.int32, x.shape, 0)
        return jnp.where((t != t_ctx - 1) & (t != n - 1), pltpu.roll(x, n - 1, axis=0), 0.0)

    @jax.custom_vjp
    def shift_dn(x):
        return dn(x)
    shift_dn.defvjp(lambda x: (dn(x), None), lambda _, g: (up(g),))

    @jax.custom_vjp
    def shift_up(x):
        return up(x)
    shift_up.defvjp(lambda x: (up(x), None), lambda _, g: (dn(g),))
    return shift_dn, shift_up


def _conv3(x, w, shift_dn, shift_up):
    return shift_dn(x) * w[0:1] + x * w[1:2] + shift_up(x) * w[2:3]


def inproj_fwd(x, shift_t, scale_t, g_pre, w_in, n_batch, sb):
    n = x.shape[0]

    def sel(i):
        return jnp.where(i % sb == 0, n_batch, i // sb)

    def body(x_ref, sh_ref, sc_ref, g_ref, w_ref, p_ref, h_ref):
        xv = x_ref[...]
        r = xv * lax.rsqrt(jnp.mean(xv * xv, axis=-1, keepdims=True) + EPS)
        h = (r * g_ref[...]) * (1.0 + sc_ref[0]) + sh_ref[0]
        hb = h.astype(bf16)
        h_ref[...] = hb
        p_ref[...] = _dot(hb, w_ref[...])

    return pl.pallas_call(
        body, name="inproj_fwd", grid=(n // TM,),
        in_specs=[pl.BlockSpec((TM, D), lambda i: (i, 0)),
                  pl.BlockSpec((1, 1, D), lambda i: (sel(i), 0, 0)),
                  pl.BlockSpec((1, 1, D), lambda i: (sel(i), 0, 0)),
                  _full((1, D)), _full((D, PW))],
        out_specs=[pl.BlockSpec((TM, PW), lambda i: (i, 0)), pl.BlockSpec((TM, D), lambda i: (i, 0))],
        out_shape=[_sds((n, PW)), _sds((n, D), bf16)],
        compiler_params=_cparams(("arbitrary",), VMEM_BIG),
    )(x, shift_t, scale_t, g_pre, w_in)


def outproj_fwd(ys, w_out, x, gate_t, g_post, n_batch, sb):
    n = x.shape[0]

    def sel(i):
        return jnp.where(i % sb == 0, n_batch, i // sb)

    def body(y0, y1, y2, y3, w_ref, x_ref, gt_ref, g_ref, xn_ref, o_ref):
        y = jnp.concatenate([y0[...], y1[...], y2[...], y3[...]], axis=1)
        o = _dot(y, w_ref[...])
        o_ref[...] = o
        nrm = o * lax.rsqrt(jnp.mean(o * o, axis=-1, keepdims=True) + EPS) * g_ref[...]
        xn_ref[...] = x_ref[...] + gt_ref[0] * nrm

    yspec = pl.BlockSpec((TM, BW), lambda i: (i, 0))
    return pl.pallas_call(
        body, name="outproj_fwd", grid=(n // TM,),
        in_specs=[yspec, yspec, yspec, yspec, _full((D, D)), pl.BlockSpec((TM, D), lambda i: (i, 0)),
                  pl.BlockSpec((1, 1, D), lambda i: (sel(i), 0, 0)), _full((1, D))],
        out_specs=[pl.BlockSpec((TM, D), lambda i: (i, 0)), pl.BlockSpec((TM, D), lambda i: (i, 0))],
        out_shape=[_sds((n, D)), _sds((n, D))],
        compiler_params=_cparams(("arbitrary",)),
    )(*ys, w_out, x, gate_t, g_post)


def _row_onehot(r):
    return lax.broadcasted_iota(jnp.int32, (8, 1), 0) == r


def outproj_bwd(dxn, o, gate_t, g_post, ys, w_out_t, n_batch, sb):
    n = dxn.shape[0]

    def sel(i):
        return jnp.where(i % sb == 0, n_batch, i // sb)

    def body(dxn_ref, o_ref, gt_ref, g_ref, y0, y1, y2, y3, wt_ref, dy_ref, dw_ref, dg_ref, dgate_ref):
        i = pl.program_id(0)

        @pl.when(i == 0)
        def _():
            dw_ref[...] = jnp.zeros_like(dw_ref)
            dg_ref[...] = jnp.zeros_like(dg_ref)
            dgate_ref[...] = jnp.zeros_like(dgate_ref)

        ov = o_ref[...]
        rstd = lax.rsqrt(jnp.mean(ov * ov, axis=-1, keepdims=True) + EPS)
        r = ov * rstd
        g = g_ref[...]
        dx = dxn_ref[...]
        dgate_ref[...] += jnp.where(_row_onehot(sel(i)), jnp.sum(dx * (r * g), axis=0, keepdims=True), 0.0)
        dn = dx * gt_ref[0]
        dg_ref[...] += jnp.sum(dn * r, axis=0, keepdims=True)
        dr = dn * g
        do = rstd * (dr - r * jnp.mean(dr * r, axis=-1, keepdims=True))
        dob = do.astype(bf16)
        dy_ref[...] = _dot(dob, wt_ref[...])
        y = jnp.concatenate([y0[...], y1[...], y2[...], y3[...]], axis=1)
        dw_ref[...] += _dot_tn(y, dob)

    yspec = pl.BlockSpec((TM, BW), lambda i: (i, 0))
    row = pl.BlockSpec((TM, D), lambda i: (i, 0))
    return pl.pallas_call(
        body, name="outproj_bwd", grid=(n // TM,),
        in_specs=[row, row, pl.BlockSpec((1, 1, D), lambda i: (sel(i), 0, 0)), _full((1, D)),
                  yspec, yspec, yspec, yspec, _full((D, D))],
        out_specs=[row, _full((D, D)), _full((1, D)), _full((8, D))],
        out_shape=[_sds((n, D)), _sds((D, D)), _sds((1, D)), _sds((8, D))],
        compiler_params=_cparams(("arbitrary",), VMEM_BIG),
    )(dxn, o, gate_t, g_post, *ys, w_out_t)


def inproj_bwd_x(dp, w_in_t, x, shift_t, scale_t, g_pre, dxn, n_batch, sb):
    n = x.shape[0]

    def sel(i):
        return jnp.where(i % sb == 0, n_batch, i // sb)

    def body(dp_ref, wt_ref, x_ref, sc_ref, g_ref, dxn_ref, dx_ref, dg_ref, dsh_ref, dsc_ref):
        i = pl.program_id(0)

        @pl.when(i == 0)
        def _():
            dg_ref[...] = jnp.zeros_like(dg_ref)
            dsh_ref[...] = jnp.zeros_like(dsh_ref)
            dsc_ref[...] = jnp.zeros_like(dsc_ref)

        dh = _dot(dp_ref[...], wt_ref[...])
        xv = x_ref[...]
        rstd = lax.rsqrt(jnp.mean(xv * xv, axis=-1, keepdims=True) + EPS)
        r = xv * rstd
        g = g_ref[...]
        hot = _row_onehot(sel(i))
        dsh_ref[...] += jnp.where(hot, jnp.sum(dh, axis=0, keepdims=True), 0.0)
        dsc_ref[...] += jnp.where(hot, jnp.sum(dh * (r * g), axis=0, keepdims=True), 0.0)
        t = dh * (1.0 + sc_ref[0])
        dg_ref[...] += jnp.sum(t * r, axis=0, keepdims=True)
        dr = t * g
        dx_ref[...] = dxn_ref[...] + rstd * (dr - r * jnp.mean(dr * r, axis=-1, keepdims=True))

    row = pl.BlockSpec((TM, D), lambda i: (i, 0))
    return pl.pallas_call(
        body, name="inproj_bwd_x", grid=(n // TM,),
        in_specs=[pl.BlockSpec((TM, PW), lambda i: (i, 0)), _full((PW, D)), row,
                  pl.BlockSpec((1, 1, D), lambda i: (sel(i), 0, 0)), _full((1, D)), row],
        out_specs=[row, _full((1, D)), _full((8, D)), _full((8, D))],
        out_shape=[_sds((n, D)), _sds((1, D)), _sds((8, D)), _sds((8, D))],
        compiler_params=_cparams(("arbitrary",), VMEM_BIG),
    )(dp, w_in_t, x, scale_t, g_pre, dxn)


def dw_in(h, dp):
    n = h.shape[0]
    tk, tn = 512, 1024

    def body(h_ref, dp_ref, o_ref):
        @pl.when(pl.program_id(1) == 0)
        def _():
            o_ref[...] = jnp.zeros_like(o_ref)
        o_ref[...] += _dot_tn(h_ref[...], dp_ref[...])

    return pl.pallas_call(
        body, name="dw_in", grid=(PW // tn, n // tk),
        in_specs=[pl.BlockSpec((tk, D), lambda j, k: (k, 0)), pl.BlockSpec((tk, tn), lambda j, k: (k, j))],
        out_specs=pl.BlockSpec((D, tn), lambda j, k: (0, j)),
        out_shape=_sds((D, PW)),
        compiler_params=_cparams(("parallel", "arbitrary"), VMEM_BIG),
    )(h, dp)


def loss_head(xf, target, t_ctx):
    nb, s, _ = xf.shape
    jc = t_ctx // TM

    def body(x_ref, t_ref, dx_ref, l_ref):
        b, j = pl.program_id(0), pl.program_id(1)

        @pl.when((b == 0) & (j == 0))
        def _():
            l_ref[...] = jnp.zeros_like(l_ref)

        @pl.when(j < jc)
        def _():
            dx_ref[...] = jnp.zeros_like(dx_ref)

        @pl.when(j >= jc)
        def _():
            diff = x_ref[0] - t_ref[0]
            dx_ref[0] = diff * (1.0 / D)
            l_ref[...] += 0.5 * jnp.sum(diff * diff) * (1.0 / D)

    return pl.pallas_call(
        body, name="loss_head", grid=(nb, s // TM),
        in_specs=[pl.BlockSpec((1, TM, D), lambda b, j: (b, j, 0)),
                  pl.BlockSpec((1, TM, D), lambda b, j: (b, jnp.maximum(j - jc, 0), 0))],
        out_specs=[pl.BlockSpec((1, TM, D), lambda b, j: (b, j, 0)), _full((1, 128))],
        out_shape=[_sds((nb, s, D)), _sds((1, 128))],
        compiler_params=_cparams(("arbitrary", "arbitrary")),
    )(xf, target)


def _chunk_maps(n_ctx, n_lat):
    n = n_ctx + n_lat

    def cf(t):
        return t

    def cb(t):
        return jnp.where(t < n_ctx, n_ctx - 1 - t, n - 1 - t + n_ctx)
    return n, cf, cb


def ret_scan_fwd(p3, cos, sins, consts, t_ctx):
    nb, s, _ = p3.shape
    n, cf, cb = _chunk_maps(t_ctx // RC, (s - t_ctx) // RC)
    intra, qdec, kdec, cd, bd, hm = consts
    cmaps = (cf, cb)

    def body(qf, kf, vf, qb, kb, vb, cosf, sinf, cosb, sinb, intra_r, qdec_r, kdec_r, cd_r, bd_r, hm_r,
             of_ref, ob_ref, sall_ref, s_sc):
        @pl.when(pl.program_id(1) == 0)
        def _():
            s_sc[...] = jnp.zeros_like(s_sc)
        ins = ((qf, kf, vf, cosf, sinf, of_ref), (qb, kb, vb, cosb, sinb, ob_ref))
        for d, (q, k, v, c_, s_, o_ref) in enumerate(ins):
            st = s_sc[d]
            sall_ref[0, d, 0] = st
            s_new, o = _ret_step(st, q[0], k[0], v[0], c_[...], s_[...], intra_r[d], qdec_r[d], kdec_r[d],
                                 cd_r[...], bd_r[...], hm_r[...])
            s_sc[d] = s_new
            o_ref[0] = o

    def pspec(m, seg):
        return pl.BlockSpec((1, RC, BW), lambda b, t: (b, m(t), seg))

    def tspec(m):
        return pl.BlockSpec((RC, BW), lambda b, t: (m(t), 0))

    return pl.pallas_call(
        body, name="ret_scan_fwd", grid=(nb, n),
        in_specs=[pspec(cf, 0), pspec(cf, 1), pspec(cf, 2), pspec(cb, 0), pspec(cb, 1), pspec(cb, 2),
                  tspec(cf), tspec(cf), tspec(cb), tspec(cb),
                  _full(intra.shape), _full(qdec.shape), _full(kdec.shape), _full(cd.shape), _full(bd.shape),
                  _full(hm.shape)],
        out_specs=[pl.BlockSpec((1, RC, BW), lambda b, t: (b, cf(t), 0)),
                   pl.BlockSpec((1, RC, BW), lambda b, t: (b, cb(t), 0)),
                   pl.BlockSpec((1, 2, 1, BW, BW), lambda b, t: (b, 0, t, 0, 0))],
        out_shape=[_sds((nb, s, BW)), _sds((nb, s, BW)), _sds((nb, 2, n, BW, BW))],
        scratch_shapes=[pltpu.VMEM((2, BW, BW), f32)],
        compiler_params=_cparams(("arbitrary", "arbitrary")),
    )(p3, p3, p3, p3, p3, p3, cos, sins, cos, sins, intra, qdec, kdec, cd, bd, hm)


def ret_scan_bwd(p3, cos, sins, consts, s_all, do, t_ctx):
    nb, s, _ = p3.shape
    n, cf, cb = _chunk_maps(t_ctx // RC, (s - t_ctx) // RC)
    intra, qdec, kdec, cd, bd, hm = consts

    def rf(t):
        return cf(n - 1 - t)

    def rb(t):
        return cb(n - 1 - t)

    def body(qf, kf, vf, qb, kb, vb, cosf, sinf, cosb, sinb, intra_r, qdec_r, kdec_r, cd_r, bd_r, hm_r,
             sall_ref, dof, dob, dqf, dkf, dvf, dqb, dkb, dvb, ds_sc):
        @pl.when(pl.program_id(1) == 0)
        def _():
            ds_sc[...] = jnp.zeros_like(ds_sc)
        ins = ((qf, kf, vf, cosf, sinf, dof, (dqf, dkf, dvf)), (qb, kb, vb, cosb, sinb, dob, (dqb, dkb, dvb)))
        for d, (q, k, v, c_, s_, do_ref, outs) in enumerate(ins):
            step = functools.partial(_ret_step, cos=c_[...], sins=s_[...], intra=intra_r[d], qdec=qdec_r[d],
                                     kdec=kdec_r[d], cd=cd_r[...], bd=bd_r[...], hm=hm_r[...])
            _, vjp = jax.vjp(step, sall_ref[0, d, 0], q[0], k[0], v[0])
            ds, dq, dk, dv = vjp((ds_sc[d], do_ref[0]))
            ds_sc[d] = ds
            outs[0][0] = dq
            outs[1][0] = dk
            outs[2][0] = dv

    def pspec(m, seg):
        return pl.BlockSpec((1, RC, BW), lambda b, t: (b, m(t), seg))

    def tspec(m):
        return pl.BlockSpec((RC, BW), lambda b, t: (m(t), 0))

    def ospec(m):
        return pl.BlockSpec((1, RC, BW), lambda b, t: (b, m(t), 0))

    return pl.pallas_call(
        body, name="ret_scan_bwd", grid=(nb, n),
        in_specs=[pspec(rf, 0), pspec(rf, 1), pspec(rf, 2), pspec(rb, 0), pspec(rb, 1), pspec(rb, 2),
                  tspec(rf), tspec(rf), tspec(rb), tspec(rb),
                  _full(intra.shape), _full(qdec.shape), _full(kdec.shape), _full(cd.shape), _full(bd.shape),
                  _full(hm.shape),
                  pl.BlockSpec((1, 2, 1, BW, BW), lambda b, t: (b, 0, n - 1 - t, 0, 0)), ospec(rf), ospec(rb)],
        out_specs=[ospec(rf), ospec(rf), ospec(rf), ospec(rb), ospec(rb), ospec(rb)],
        out_shape=[_sds((nb, s, BW))] * 6,
        scratch_shapes=[pltpu.VMEM((2, BW, BW), f32)],
        compiler_params=_cparams(("arbitrary", "arbitrary")),
    )(p3, p3, p3, p3, p3, p3, cos, sins, cos, sins, intra, qdec, kdec, cd, bd, hm, s_all, do, do)


def mix_finish_fwd(fn, name, o_f, o_b, p3, zseg, norm_g, bd):
    nb, s, _ = p3.shape

    def body(of_ref, ob_ref, z_ref, g_ref, bd_ref, y_ref):
        y_ref[0] = fn(of_ref[0], ob_ref[0], z_ref[0], g_ref[...], bd_ref[...]).astype(bf16)

    blk = pl.BlockSpec((1, TM, BW), lambda b, j: (b, j, 0))
    return pl.pallas_call(
        body, name=name, grid=(nb, s // TM),
        in_specs=[blk, blk, pl.BlockSpec((1, TM, BW), lambda b, j: (b, j, zseg)), _full((1, BW)), _full((BW, BW))],
        out_specs=blk, out_shape=_sds((nb, s, BW), bf16),
        compiler_params=_cparams(("arbitrary", "arbitrary")),
    )(o_f, o_b, p3, norm_g, bd)


def mix_finish_bwd(fn, name, o_f, o_b, p3, zseg, norm_g, bd, dy3, yseg):
    nb, s, _ = p3.shape

    def body(of_ref, ob_ref, z_ref, g_ref, bd_ref, dy_ref, do_ref, dz_ref, dg_ref):
        @pl.when((pl.program_id(0) == 0) & (pl.program_id(1) == 0))
        def _():
            dg_ref[...] = jnp.zeros_like(dg_ref)
        bdv = bd_ref[...]
        _, vjp = jax.vjp(lambda a, b, z, g: fn(a, b, z, g, bdv), of_ref[0], ob_ref[0], z_ref[0], g_ref[...])
        do, _, dz, dg = vjp(dy_ref[0])
        do_ref[0] = do
        dz_ref[0] = dz
        dg_ref[...] += dg

    blk = pl.BlockSpec((1, TM, BW), lambda b, j: (b, j, 0))
    return pl.pallas_call(
        body, name=name, grid=(nb, s // TM),
        in_specs=[blk, blk, pl.BlockSpec((1, TM, BW), lambda b, j: (b, j, zseg)), _full((1, BW)), _full((BW, BW)),
                  pl.BlockSpec((1, TM, BW), lambda b, j: (b, j, yseg))],
        out_specs=[blk, blk, _full((1, BW))],
        out_shape=[_sds((nb, s, BW)), _sds((nb, s, BW)), _sds((1, BW))],
        compiler_params=_cparams(("arbitrary", "arbitrary")),
    )(o_f, o_b, p3, norm_g, bd, dy3)


def gdn_conv_fwd(p3, w, seg, t_ctx):
    nb, s, _ = p3.shape
    sd, su = _make_shifts(t_ctx, s)

    def body(x_ref, w_ref, o_ref):
        o_ref[0] = _silu(_conv3(x_ref[0], w_ref[...], sd, su))

    return pl.pallas_call(
        body, name="gdn_conv_fwd", grid=(nb, 2),
        in_specs=[pl.BlockSpec((1, s, 128), lambda b, j: (b, 0, 2 * seg + j)), pl.BlockSpec((3, 128), lambda b, j: (0, j))],
        out_specs=pl.BlockSpec((1, s, 128), lambda b, j: (b, 0, j)),
        out_shape=_sds((nb, s, BW)),
        compiler_params=_cparams(("arbitrary", "arbitrary")),
    )(p3, w)


def gdn_conv_bwd(p3, w, seg, d_f, d_b, t_ctx):
    nb, s, _ = p3.shape
    sd, su = _make_shifts(t_ctx, s)

    def body(x_ref, w_ref, df_ref, db_ref, dx_ref, dw_ref):
        @pl.when(pl.program_id(1) == 0)
        def _():
            dw_ref[...] = jnp.zeros_like(dw_ref)
        _, vjp = jax.vjp(lambda x, w_: _silu(_conv3(x, w_, sd, su)), x_ref[0], w_ref[...])
        dx, dw = vjp(df_ref[0] + db_ref[0])
        dx_ref[0] = dx
        dw_ref[...] += dw

    blk = pl.BlockSpec((1, s, 128), lambda j, b: (b, 0, j))
    return pl.pallas_call(
        body, name="gdn_conv_bwd", grid=(2, nb),
        in_specs=[pl.BlockSpec((1, s, 128), lambda j, b: (b, 0, 2 * seg + j)), pl.BlockSpec((3, 128), lambda j, b: (0, j)),
                  blk, blk],
        out_specs=[blk, pl.BlockSpec((3, 128), lambda j, b: (0, j))],
        out_shape=[_sds((nb, s, BW)), _sds((3, BW))],
        compiler_params=_cparams(("arbitrary", "arbitrary"), VMEM_BIG),
    )(p3, w, d_f, d_b)


def gdn_scan_fwd(cq, ck, cv, p3, alog, dtb, consts, t_ctx):
    nb, s, _ = p3.shape
    n, cf, cb = _chunk_maps(t_ctx // GC, (s - t_ctx) // GC)
    gblk = GATE_COL // 128

    def body(qf, kf, vf, gf, qb, kb, vb, gb, al_ref, dt_ref, tm_r, st_r, eg_r, eb_r, egt_r, oh_r, bd_r, hm_r,
             of_ref, ob_ref, sall_ref, s_sc):
        @pl.when(pl.program_id(1) == 0)
        def _():
            s_sc[...] = jnp.zeros_like(s_sc)
        ins = ((qf, kf, vf, gf, of_ref), (qb, kb, vb, gb, ob_ref))
        for d, (q, k, v, g, o_ref) in enumerate(ins):
            st = s_sc[d]
            sall_ref[0, d, 0] = st
            s_new, o = _gdn_step(st, q[0], k[0], v[0], g[0], al_ref[...], dt_ref[...], tm_r[d], st_r[d], eg_r[d],
                                 eb_r[d], egt_r[d], oh_r[d], bd_r[...], hm_r[...])
            s_sc[d] = s_new
            o_ref[0] = o

    def cspec(m):
        return pl.BlockSpec((1, GC, BW), lambda b, t: (b, m(t), 0))

    def gspec(m):
        return pl.BlockSpec((1, GC, 128), lambda b, t: (b, m(t), gblk))

    return pl.pallas_call(
        body, name="gdn_scan_fwd", grid=(nb, n),
        in_specs=[cspec(cf), cspec(cf), cspec(cf), gspec(cf), cspec(cb), cspec(cb), cspec(cb), gspec(cb),
                  _full((1, 128)), _full((1, 128))] + [_full(c.shape) for c in consts],
        out_specs=[cspec(cf), cspec(cb), pl.BlockSpec((1, 2, 1, BW, BW), lambda b, t: (b, 0, t, 0, 0))],
        out_shape=[_sds((nb, s, BW)), _sds((nb, s, BW)), _sds((nb, 2, n, BW, BW))],
        scratch_shapes=[pltpu.VMEM((2, BW, BW), f32)],
        compiler_params=_cparams(("arbitrary", "arbitrary")),
    )(cq, ck, cv, p3, cq, ck, cv, p3, alog, dtb, *consts)


def gdn_scan_bwd(cq, ck, cv, p3, alog, dtb, consts, s_all, do, t_ctx):
    nb, s, _ = p3.shape
    n, cf, cb = _chunk_maps(t_ctx // GC, (s - t_ctx) // GC)
    gblk = GATE_COL // 128

    def rf(t):
        return cf(n - 1 - t)

    def rb(t):
        return cb(n - 1 - t)

    def body(qf, kf, vf, gf, qb, kb, vb, gb, al_ref, dt_ref, tm_r, st_r, eg_r, eb_r, egt_r, oh_r, bd_r, hm_r,
             sall_ref, dof, dob, dqf, dkf, dvf, dgf, dqb, dkb, dvb, dgb, dal_ref, ddt_ref, ds_sc):
        @pl.when((pl.program_id(0) == 0) & (pl.program_id(1) == 0))
        def _():
            dal_ref[...] = jnp.zeros_like(dal_ref)
            ddt_ref[...] = jnp.zeros_like(ddt_ref)

        @pl.when(pl.program_id(1) == 0)
        def _():
            ds_sc[...] = jnp.zeros_like(ds_sc)
        ins = ((qf, kf, vf, gf, dof, (dqf, dkf, dvf, dgf)), (qb, kb, vb, gb, dob, (dqb, dkb, dvb, dgb)))
        for d, (q, k, v, g, do_ref, outs) in enumerate(ins):
            step = functools.partial(_gdn_step, tmask=tm_r[d], strict=st_r[d], exp_g=eg_r[d], exp_b=eb_r[d],
                                     exp_gt=egt_r[d], onehot=oh_r[d], bd=bd_r[...], hm=hm_r[...])
            _, vjp = jax.vjp(step, sall_ref[0, d, 0], q[0], k[0], v[0], g[0], al_ref[...], dt_ref[...])
            ds, dq, dk, dv, dg, dal, ddt = vjp((ds_sc[d], do_ref[0]))
            ds_sc[d] = ds
            outs[0][0] = dq
            outs[1][0] = dk
            outs[2][0] = dv
            outs[3][0] = dg
            dal_ref[...] += dal
            ddt_ref[...] += ddt

    def cspec(m):
        return pl.BlockSpec((1, GC, BW), lambda b, t: (b, m(t), 0))

    def gspec(m):
        return pl.BlockSpec((1, GC, 128), lambda b, t: (b, m(t), gblk))

    def gout(m):
        return pl.BlockSpec((1, GC, 128), lambda b, t: (b, m(t), 0))

    return pl.pallas_call(
        body, name="gdn_scan_bwd", grid=(nb, n),
        in_specs=[cspec(rf), cspec(rf), cspec(rf), gspec(rf), cspec(rb), cspec(rb), cspec(rb), gspec(rb),
                  _full((1, 128)), _full((1, 128))] + [_full(c.shape) for c in consts]
                 + [pl.BlockSpec((1, 2, 1, BW, BW), lambda b, t: (b, 0, n - 1 - t, 0, 0)), cspec(rf), cspec(rb)],
        out_specs=[cspec(rf), cspec(rf), cspec(rf), gout(rf), cspec(rb), cspec(rb), cspec(rb), gout(rb),
                   _full((1, 128)), _full((1, 128))],
        out_shape=[_sds((nb, s, BW))] * 3 + [_sds((nb, s, 128))] + [_sds((nb, s, BW))] * 3 + [_sds((nb, s, 128))]
                  + [_sds((1, 128)), _sds((1, 128))],
        scratch_shapes=[pltpu.VMEM((2, BW, BW), f32)],
        compiler_params=_cparams(("arbitrary", "arbitrary"), VMEM_BIG),
    )(cq, ck, cv, p3, cq, ck, cv, p3, alog, dtb, *consts, s_all, do, do)


def sg_fwd(p3, w, b, hm4):
    nb, s, _ = p3.shape

    def body(u_ref, v_ref, z_ref, w_ref, b_ref, hm_ref, y_ref):
        y_ref[0] = _sg_chunk(u_ref[0], v_ref[0], z_ref[0], w_ref[...], b_ref[...], hm_ref[...]).astype(bf16)

    def seg(k):
        return pl.BlockSpec((1, RC, BW), lambda bi, i: (bi, i, k))

    return pl.pallas_call(
        body, name="sg_fwd", grid=(nb, s // RC),
        in_specs=[seg(4), seg(5), seg(6), _full((NH, RC, RC)), _full((NH, RC)), _full((NH, BW))],
        out_specs=pl.BlockSpec((1, RC, BW), lambda bi, i: (bi, i, 0)),
        out_shape=_sds((nb, s, BW), bf16),
        compiler_params=_cparams(("arbitrary", "arbitrary")),
    )(p3, p3, p3, w, b, hm4)


def sg_bwd(p3, w, b, hm4, dy3):
    nb, s, _ = p3.shape

    def body(u_ref, v_ref, z_ref, w_ref, b_ref, hm_ref, dy_ref, du_ref, dv_ref, dz_ref, dw_ref, db_ref):
        @pl.when((pl.program_id(0) == 0) & (pl.program_id(1) == 0))
        def _():
            dw_ref[...] = jnp.zeros_like(dw_ref)
            db_ref[...] = jnp.zeros_like(db_ref)
        hm = hm_ref[...]
        _, vjp = jax.vjp(lambda u, v, z, w_, b_: _sg_chunk(u, v, z, w_, b_, hm),
                         u_ref[0], v_ref[0], z_ref[0], w_ref[...], b_ref[...])
        du, dv, dz, dw, db = vjp(dy_ref[0])
        du_ref[0] = du
        dv_ref[0] = dv
        dz_ref[0] = dz
        dw_ref[...] += dw
        db_ref[...] += db

    def seg(k):
        return pl.BlockSpec((1, RC, BW), lambda bi, i: (bi, i, k))

    blk = pl.BlockSpec((1, RC, BW), lambda bi, i: (bi, i, 0))
    return pl.pallas_call(
        body, name="sg_bwd", grid=(nb, s // RC),
        in_specs=[seg(4), seg(5), seg(6), _full((NH, RC, RC)), _full((NH, RC)), _full((NH, BW)), seg(1)],
        out_specs=[blk, blk, blk, _full((NH, RC, RC)), _full((NH, RC))],
        out_shape=[_sds((nb, s, BW))] * 3 + [_sds((NH, RC, RC)), _sds((NH, RC))],
        compiler_params=_cparams(("arbitrary", "arbitrary")),
    )(p3, p3, p3, w, b, hm4, dy3)


def _sc_fn(b, c, h, z, w, sd, su):
    return b * _conv3(c * h, w, sd, su) * _silu(z)


def sc_fwd(p3, w, t_ctx):
    nb, s, _ = p3.shape
    sd, su = _make_shifts(t_ctx, s)

    def body(b_ref, c_ref, h_ref, z_ref, w_ref, y_ref):
        y_ref[0] = _sc_fn(b_ref[0], c_ref[0], h_ref[0], z_ref[0], w_ref[...], sd, su).astype(bf16)

    def seg(k):
        return pl.BlockSpec((1, s, 128), lambda bi, j: (bi, 0, 2 * k + j))

    return pl.pallas_call(
        body, name="sc_fwd", grid=(nb, 2),
        in_specs=[seg(7), seg(8), seg(9), seg(10), pl.BlockSpec((3, 128), lambda bi, j: (0, j))],
        out_specs=pl.BlockSpec((1, s, 128), lambda bi, j: (bi, 0, j)),
        out_shape=_sds((nb, s, BW), bf16),
        compiler_params=_cparams(("arbitrary", "arbitrary"), VMEM_BIG),
    )(p3, p3, p3, p3, w)


def sc_bwd(p3, w, dy3, t_ctx):
    nb, s, _ = p3.shape
    sd, su = _make_shifts(t_ctx, s)

    def body(b_ref, c_ref, h_ref, z_ref, w_ref, dy_ref, db_ref, dc_ref, dh_ref, dz_ref, dw_ref):
        @pl.when(pl.program_id(1) == 0)
        def _():
            dw_ref[...] = jnp.zeros_like(dw_ref)
        _, vjp = jax.vjp(lambda b, c, h, z, w_: _sc_fn(b, c, h, z, w_, sd, su),
                         b_ref[0], c_ref[0], h_ref[0], z_ref[0], w_ref[...])
        db, dc, dh, dz, dw = vjp(dy_ref[0])
        db_ref[0] = db
        dc_ref[0] = dc
        dh_ref[0] = dh
        dz_ref[0] = dz
        dw_ref[...] += dw

    def seg(k):
        return pl.BlockSpec((1, s, 128), lambda j, bi: (bi, 0, 2 * k + j))

    blk = pl.BlockSpec((1, s, 128), lambda j, bi: (bi, 0, j))
    wspec = pl.BlockSpec((3, 128), lambda j, bi: (0, j))
    return pl.pallas_call(
        body, name="sc_bwd", grid=(2, nb),
        in_specs=[seg(7), seg(8), seg(9), seg(10), wspec, seg(2)],
        out_specs=[blk, blk, blk, blk, wspec],
        out_shape=[_sds((nb, s, BW))] * 4 + [_sds((3, BW))],
        compiler_params=_cparams(("arbitrary", "arbitrary"), VMEM_BIG),
    )(p3, p3, p3, p3, w, dy3)


def assemble_dp(pairs, singles_a, gdn_x, singles_b, gates):
    nb, s, _ = singles_a[0].shape
    flat = [a for pr in pairs for a in pr] + list(singles_a) + list(gdn_x) + list(singles_b) + list(gates)
    n_pairs, n_a, n_x, n_b = len(pairs), len(singles_a), len(gdn_x), len(singles_b)

    def body(*refs):
        out = refs[-1]
        ins = refs[:-1]
        col = 0
        for p in range(n_pairs):
            out[0, :, col:col + BW] = (ins[2 * p][0] + ins[2 * p + 1][0]).astype(bf16)
            col += BW
        k = 2 * n_pairs
        for _ in range(n_a + n_x + n_b):
            out[0, :, col:col + BW] = ins[k][0].astype(bf16)
            col += BW
            k += 1
        out[0, :, col:col + 128] = (ins[k][0] + ins[k + 1][0]).astype(bf16)
        out[0, :, col + 128:] = jnp.zeros((TM, PW - col - 128), bf16)

    def spec(a):
        return pl.BlockSpec((1, TM, a.shape[-1]), lambda b, j: (b, j, 0))

    return pl.pallas_call(
        body, name="assemble_dp", grid=(nb, s // TM),
        in_specs=[spec(a) for a in flat],
        out_specs=pl.BlockSpec((1, TM, PW), lambda b, j: (b, j, 0)),
        out_shape=_sds((nb, s, PW), bf16),
        compiler_params=_cparams(("arbitrary", "arbitrary")),
    )(*flat)


def mod_fwd(c_rows, w_mod, b_cols):
    nl, _, wc = w_mod.shape
    nr = c_rows.shape[0]

    def body(c_ref, w_ref, b_ref, o_ref):
        o_ref[0] = _dot(_silu(c_ref[...]), w_ref[0], precision=HI) + b_ref[0]

    return pl.pallas_call(
        body, name="mod_fwd", grid=(nl,),
        in_specs=[_full((nr, D)), pl.BlockSpec((1, D, wc), lambda l: (l, 0, 0)), pl.BlockSpec((1, 1, wc), lambda l: (l, 0, 0))],
        out_specs=pl.BlockSpec((1, nr, wc), lambda l: (l, 0, 0)),
        out_shape=_sds((nl, nr, wc)),
        compiler_params=_cparams(("arbitrary",)),
    )(c_rows, w_mod, b_cols)


def mod_bwd(c_rows, w_mod, dm_cols, dm_full):
    nl, _, wc = w_mod.shape
    nr = c_rows.shape[0]

    def body(c_ref, w_ref, dmc_ref, dmf_ref, gw_ref, gb_ref, dcc_ref):
        @pl.when(pl.program_id(0) == 0)
        def _():
            dcc_ref[...] = jnp.zeros_like(dcc_ref)
        a = _silu(c_ref[...])
        dmc = dmc_ref[0]
        gw_ref[0] = _dot_tn(a, dmc, precision=HI)
        gb_ref[0] = jnp.sum(dmf_ref[0], axis=0, keepdims=True)
        dcc_ref[...] += _dot_nt(dmc[nr - 8:nr], w_ref[0], precision=HI)

    return pl.pallas_call(
        body, name="mod_bwd", grid=(nl,),
        in_specs=[_full((nr, D)), pl.BlockSpec((1, D, wc), lambda l: (l, 0, 0)),
                  pl.BlockSpec((1, nr, wc), lambda l: (l, 0, 0)), pl.BlockSpec((1, nr, 3 * D), lambda l: (l, 0, 0))],
        out_specs=[pl.BlockSpec((1, D, wc), lambda l: (l, 0, 0)), pl.BlockSpec((1, 1, 3 * D), lambda l: (l, 0, 0)),
                   _full((8, D))],
        out_shape=[_sds((nl, D, wc)), _sds((nl, 1, 3 * D)), _sds((8, D))],
        compiler_params=_cparams(("arbitrary",)),
    )(c_rows, w_mod, dm_cols, dm_full)


def cctx_grad(parts, c_ctx):
    def body(p_ref, c_ref, o_ref):
        tot = p_ref[0, 0:1, :]
        for k in (2, 4, 6):
            tot = tot + p_ref[k, 0:1, :]
        c = c_ref[...]
        sg = jax.nn.sigmoid(c)
        o_ref[...] = tot * (sg * (1.0 + c * (1.0 - sg)))

    return pl.pallas_call(body, name="cctx_grad", out_shape=_sds((1, D)))(parts, c_ctx)


def sum_lead(x, out_dtype=f32, tr=256):
    k, r, c = x.shape
    tr = min(tr, r)
    assert r % tr == 0

    def body(x_ref, o_ref):
        tot = x_ref[0].astype(f32)
        for i in range(1, k):
            tot = tot + x_ref[i].astype(f32)
        o_ref[...] = tot.astype(out_dtype)

    return pl.pallas_call(
        body, name="sum_lead", grid=(r // tr,),
        in_specs=[pl.BlockSpec((k, tr, c), lambda i: (0, i, 0))],
        out_specs=pl.BlockSpec((tr, c), lambda i: (i, 0)),
        out_shape=_sds((r, c), out_dtype),
        compiler_params=_cparams(("arbitrary",)),
    )(x)


def adamw(w, m, v, g1, g2=None, tr=256):
    r, c = w.shape
    tr = min(tr, r)
    assert r % tr == 0
    two = g2 is not None
    c1 = 1.0 / (1.0 - ADAM_B1 ** ADAM_STEP)
    c2 = 1.0 / (1.0 - ADAM_B2 ** ADAM_STEP)

    def body(*refs):
        w_ref, m_ref, v_ref, g_ref = refs[:4]
        g = g_ref[...]
        if two:
            g = g + refs[4][...]
        go_ref, d_ref, mo_ref, vo_ref = refs[-4:]
        mn = ADAM_B1 * m_ref[...] + (1.0 - ADAM_B1) * g
        vn = ADAM_B2 * v_ref[...] + (1.0 - ADAM_B2) * (g * g)
        go_ref[...] = g
        mo_ref[...] = mn
        vo_ref[...] = vn
        d_ref[...] = -ADAM_LR * ((mn * c1) / (jnp.sqrt(vn * c2) + ADAM_EPS) + ADAM_WD * w_ref[...])

    blk = pl.BlockSpec((tr, c), lambda i: (i, 0))
    args = [w, m, v, g1] + ([g2] if two else [])
    return pl.pallas_call(
        body, name="adamw", grid=(r // tr,),
        in_specs=[blk] * len(args), out_specs=[blk] * 4, out_shape=[_sds((r, c))] * 4,
        compiler_params=_cparams(("arbitrary",)),
    )(*args)


def _my_pos():
    return lax.axis_index("x"), lax.axis_index("y"), lax.axis_index("c")


def gather8(x):
    shape = x.shape

    def body(x_ref, out_ref, send_sems, recv_sems, local_sem):
        mx, my, mc = _my_pos()
        me = 4 * mx + 2 * my + mc
        mine = pltpu.make_async_copy(x_ref, out_ref.at[me], local_sem)
        mine.start()
        copies = []
        for k in range(1, N_DEV):
            peer = (mx ^ (k >> 2), my ^ ((k >> 1) & 1), mc ^ (k & 1))
            cp = pltpu.make_async_remote_copy(src_ref=x_ref, dst_ref=out_ref.at[me], send_sem=send_sems.at[k - 1],
                                              recv_sem=recv_sems.at[k - 1], device_id=peer, device_id_type=MESH)
            cp.start()
            copies.append(cp)
        for k in range(1, N_DEV):
            src = me ^ k
            pltpu.make_async_remote_copy(src_ref=x_ref, dst_ref=out_ref.at[src], send_sem=send_sems.at[k - 1],
                                         recv_sem=recv_sems.at[k - 1], device_id=(mx, my, mc),
                                         device_id_type=MESH).wait_recv()
        for cp in copies:
            cp.wait_send()
        mine.wait()

    return pl.pallas_call(
        body, name="gather8", out_shape=_sds((N_DEV,) + shape, x.dtype),
        in_specs=[pl.BlockSpec(memory_space=pl.ANY)], out_specs=pl.BlockSpec(memory_space=pl.ANY),
        scratch_shapes=[pltpu.SemaphoreType.DMA((N_DEV - 1,)), pltpu.SemaphoreType.DMA((N_DEV - 1,)),
                        pltpu.SemaphoreType.DMA(())],
    )(x)


def gather4(x):
    shape = x.shape

    def body(x_ref, out_ref, send_sems, recv_sems, local_sem):
        mx, my, mc = _my_pos()
        me = 2 * mx + my
        mine = pltpu.make_async_copy(x_ref, out_ref.at[me], local_sem)
        mine.start()
        copies = []
        for k in range(1, N_CHIPS):
            peer = (mx ^ (k >> 1), my ^ (k & 1), mc)
            cp = pltpu.make_async_remote_copy(src_ref=x_ref, dst_ref=out_ref.at[me], send_sem=send_sems.at[k - 1],
                                              recv_sem=recv_sems.at[k - 1], device_id=peer, device_id_type=MESH)
            cp.start()
            copies.append(cp)
        for k in range(1, N_CHIPS):
            src = me ^ k
            pltpu.make_async_remote_copy(src_ref=x_ref, dst_ref=out_ref.at[src], send_sem=send_sems.at[k - 1],
                                         recv_sem=recv_sems.at[k - 1], device_id=(mx, my, mc),
                                         device_id_type=MESH).wait_recv()
        for cp in copies:
            cp.wait_send()
        mine.wait()

    return pl.pallas_call(
        body, name="gather4", out_shape=_sds((N_CHIPS,) + shape, x.dtype),
        in_specs=[pl.BlockSpec(memory_space=pl.ANY)], out_specs=pl.BlockSpec(memory_space=pl.ANY),
        scratch_shapes=[pltpu.SemaphoreType.DMA((N_CHIPS - 1,)), pltpu.SemaphoreType.DMA((N_CHIPS - 1,)),
                        pltpu.SemaphoreType.DMA(())],
    )(x)


def scatter4(g):
    shape = g.shape[1:]

    def body(g_ref, out_ref, send_sems, recv_sems, local_sem):
        mx, my, mc = _my_pos()
        me = 2 * mx + my
        mine = pltpu.make_async_copy(g_ref.at[me], out_ref.at[me], local_sem)
        mine.start()
        copies = []
        for k in range(1, N_CHIPS):
            peer = (mx ^ (k >> 1), my ^ (k & 1), mc)
            cp = pltpu.make_async_remote_copy(src_ref=g_ref.at[me ^ k], dst_ref=out_ref.at[me], send_sem=send_sems.at[k - 1],
                                              recv_sem=recv_sems.at[k - 1], device_id=peer, device_id_type=MESH)
            cp.start()
            copies.append(cp)
        for k in range(1, N_CHIPS):
            src = me ^ k
            pltpu.make_async_remote_copy(src_ref=g_ref.at[src], dst_ref=out_ref.at[src], send_sem=send_sems.at[k - 1],
                                         recv_sem=recv_sems.at[k - 1], device_id=(mx, my, mc),
                                         device_id_type=MESH).wait_recv()
        for cp in copies:
            cp.wait_send()
        mine.wait()

    return pl.pallas_call(
        body, name="scatter4", out_shape=_sds((N_CHIPS,) + shape, g.dtype),
        in_specs=[pl.BlockSpec(memory_space=pl.ANY)], out_specs=pl.BlockSpec(memory_space=pl.ANY),
        scratch_shapes=[pltpu.SemaphoreType.DMA((N_CHIPS - 1,)), pltpu.SemaphoreType.DMA((N_CHIPS - 1,)),
                        pltpu.SemaphoreType.DMA(())],
    )(g)


def swap_sibling(x):
    def body(x_ref, out_ref, send_sem, recv_sem):
        mx, my, mc = _my_pos()
        cp = pltpu.make_async_remote_copy(src_ref=x_ref, dst_ref=out_ref, send_sem=send_sem, recv_sem=recv_sem,
                                          device_id=(mx, my, 1 - mc), device_id_type=MESH)
        cp.start()
        cp.wait()

    return pl.pallas_call(
        body, name="swap_sibling", out_shape=_sds(x.shape, x.dtype),
        in_specs=[pl.BlockSpec(memory_space=pl.ANY)], out_specs=pl.BlockSpec(memory_space=pl.ANY),
        scratch_shapes=[pltpu.SemaphoreType.DMA(()), pltpu.SemaphoreType.DMA(())],
    )(x)


PACK_ROWS = 64
SMALL = ("c_ctx", "b_mod", "g_pre", "g_post", "ret_norm_g", "sg_w", "sg_b", "sc_conv_w", "gdn_conv_w",
         "gdn_a_log", "gdn_dt_bias", "gdn_norm_g")


def _pack(arrs, width=D):
    rows = []
    for a in arrs:
        flat = a.reshape(-1)
        pad = (-flat.shape[0]) % width
        rows.append(jnp.pad(flat, (0, pad)).reshape(-1, width))
    out = jnp.concatenate(rows, axis=0)
    return jnp.pad(out, ((0, (-out.shape[0]) % PACK_ROWS), (0, 0)))


def _unpack(packed, shapes, width=D):
    outs, r = [], 0
    for shp in shapes:
        size = int(np.prod(shp))
        nr = -(-size // width)
        outs.append(packed[r:r + nr].reshape(-1)[:size].reshape(shp))
        r += nr
    return outs


def kernel(x, c, ctx, c_ctx, w_mod, b_mod, g_pre, g_post, w_in, w_out, ret_norm_g, sg_w, sg_b, sc_conv_w, gdn_conv_w, gdn_a_log, gdn_dt_bias, gdn_norm_g, loss_target, m_c_ctx, m_w_mod, m_b_mod, m_g_pre, m_g_post, m_w_in, m_w_out, m_ret_norm_g, m_sg_w, m_sg_b, m_sc_conv_w, m_gdn_conv_w, m_gdn_a_log, m_gdn_dt_bias, m_gdn_norm_g, v_c_ctx, v_w_mod, v_b_mod, v_g_pre, v_g_post, v_w_in, v_w_out, v_ret_norm_g, v_sg_w, v_sg_b, v_sc_conv_w, v_gdn_conv_w, v_gdn_a_log, v_gdn_dt_bias, v_gdn_norm_g):
    weights = dict(c_ctx=c_ctx, w_mod=w_mod, b_mod=b_mod, g_pre=g_pre, g_post=g_post, w_in=w_in, w_out=w_out,
                   ret_norm_g=ret_norm_g, sg_w=sg_w, sg_b=sg_b, sc_conv_w=sc_conv_w, gdn_conv_w=gdn_conv_w,
                   gdn_a_log=gdn_a_log, gdn_dt_bias=gdn_dt_bias, gdn_norm_g=gdn_norm_g)
    mom = dict(c_ctx=m_c_ctx, w_mod=m_w_mod, b_mod=m_b_mod, g_pre=m_g_pre, g_post=m_g_post, w_in=m_w_in,
               w_out=m_w_out, ret_norm_g=m_ret_norm_g, sg_w=m_sg_w, sg_b=m_sg_b, sc_conv_w=m_sc_conv_w,
               gdn_conv_w=m_gdn_conv_w, gdn_a_log=m_gdn_a_log, gdn_dt_bias=m_gdn_dt_bias, gdn_norm_g=m_gdn_norm_g)
    var = dict(c_ctx=v_c_ctx, w_mod=v_w_mod, b_mod=v_b_mod, g_pre=v_g_pre, g_post=v_g_post, w_in=v_w_in,
               w_out=v_w_out, ret_norm_g=v_ret_norm_g, sg_w=v_sg_w, sg_b=v_sg_b, sc_conv_w=v_sc_conv_w,
               gdn_conv_w=v_gdn_conv_w, gdn_a_log=v_gdn_a_log, gdn_dt_bias=v_gdn_dt_bias, gdn_norm_g=v_gdn_norm_g)

    nb, t_lat, _ = x.shape
    t_ctx = ctx.shape[1]
    s = t_ctx + t_lat
    n = nb * s
    sb = s // TM
    nl = w_in.shape[0]
    wc_in = w_in.shape[2]
    wc_mod = w_mod.shape[2]
    rows_out = w_out.shape[1]
    n_all = nb * N_DEV
    mx, my, mc = _my_pos()
    chip = 2 * mx + my
    dev = 2 * chip + mc

    hm = jnp.asarray(_head_masks())
    hm4 = hm[:, 0, :]
    bd = jnp.asarray(_block_diag())
    ret_c = _ret_consts() + [bd, hm]
    gdn_c = _gdn_consts() + [bd, hm]
    cos, sins = _rope_tables(t_lat, t_ctx)

    pre = _pack([c, sc_conv_w, gdn_conv_w])
    pre_all = gather8(pre)
    c_parts, scw_parts, gcw_parts = [], [], []
    for k in range(N_DEV):
        ck, sk, gk = _unpack(pre_all[k], [c.shape, sc_conv_w.shape, gdn_conv_w.shape])
        c_parts.append(ck)
        if k % 2 == 0:
            scw_parts.append(sk)
            gcw_parts.append(gk)
    c_all = jnp.concatenate(c_parts, axis=0)
    sc_w_full = jnp.concatenate(scw_parts, axis=-1)
    gdn_w_full = jnp.concatenate(gcw_parts, axis=-1)
    c_rows = jnp.concatenate([c_all, c_ctx[None, :], jnp.zeros((7, D), f32)], axis=0)

    b_cols = lax.dynamic_slice_in_dim(b_mod, chip * wc_mod, wc_mod, axis=1)[:, None, :]
    mod_part = mod_fwd(c_rows, w_mod, b_cols)
    mod_all = gather8(mod_part)
    mod = jnp.concatenate([mod_all[2 * k] for k in range(N_CHIPS)], axis=-1)
    my_rows = jnp.concatenate([lax.dynamic_slice_in_dim(mod, dev * nb, nb, axis=1), mod[:, n_all:n_all + 1]], axis=1)
    shift_t = my_rows[:, :, None, 0:D]
    scale_t = my_rows[:, :, None, D:2 * D]
    gate_t = my_rows[:, :, None, 2 * D:3 * D]

    w_in_all = gather4(w_in.astype(bf16))
    w_in_full = jnp.concatenate([w_in_all[k] for k in range(N_CHIPS)], axis=-1)
    w_in_full = jnp.pad(w_in_full, ((0, 0), (0, 0), (0, PW - IN_W)))
    w_in_t = jnp.transpose(w_in_full, (0, 2, 1))
    w_out_all = gather4(w_out.astype(bf16))
    w_out_full = jnp.concatenate([w_out_all[k] for k in range(N_CHIPS)], axis=1)
    w_out_t = jnp.transpose(w_out_full, (0, 2, 1))

    alog = jnp.pad(gdn_a_log.reshape(nl, 1, 8), ((0, 0), (0, 0), (0, 120)))
    dtb = jnp.pad(gdn_dt_bias.reshape(nl, 1, 8), ((0, 0), (0, 0), (0, 120)))
    gdn_ng = jnp.tile(gdn_norm_g, (1, NH))[:, None, :]
    ret_ng = ret_norm_g[:, None, :]

    xs = jnp.concatenate([ctx, x], axis=1).reshape(n, D)
    saved = []
    for l in range(nl):
        p, h = inproj_fwd(xs, shift_t[l], scale_t[l], g_pre[l][None, :], w_in_full[l], nb, sb)
        p3 = p.reshape(nb, s, PW)
        ro_f, ro_b, rs_all = ret_scan_fwd(p3, cos, sins, ret_c, t_ctx)
        y_ret = mix_finish_fwd(_ret_finish, "ret_finish_fwd", ro_f, ro_b, p3, 3, ret_ng[l], bd)
        y_sg = sg_fwd(p3, sg_w[l], sg_b[l], hm4)
        y_sc = sc_fwd(p3, sc_w_full[l], t_ctx)
        cq, ck, cv = [gdn_conv_fwd(p3, gdn_w_full[l][:, BW * i:BW * (i + 1)], 11 + i, t_ctx) for i in range(3)]
        go_f, go_b, gs_all = gdn_scan_fwd(cq, ck, cv, p3, alog[l], dtb[l], gdn_c, t_ctx)
        y_gdn = mix_finish_fwd(_gdn_finish, "gdn_finish_fwd", go_f, go_b, p3, 14, gdn_ng[l], bd)
        ys = [a.reshape(n, BW) for a in (y_ret, y_sg, y_sc, y_gdn)]
        x_new, o = outproj_fwd(ys, w_out_full[l], xs, gate_t[l], g_post[l][None, :], nb, sb)
        saved.append(dict(x=xs, h=h, p3=p3, ro=(ro_f, ro_b), rs=rs_all, c=(cq, ck, cv), go=(go_f, go_b), gs=gs_all,
                          ys=ys, o=o))
        xs = x_new

    dx3, loss_part = loss_head(xs.reshape(nb, s, D), loss_target, t_ctx)
    loss = lax.psum(loss_part[0, 0], ("x", "y", "c"))

    dxs = dx3.reshape(n, D)
    g_small = {k: [None] * nl for k in SMALL if k not in ("c_ctx", "b_mod")}
    dm_rows = [None] * nl
    gw_in = [None] * nl
    gw_out = [None] * nl
    for l in reversed(range(nl)):
        sv = saved[l]
        p3 = sv["p3"]
        dy, gw_out[l], dg_post, dgate = outproj_bwd(dxs, sv["o"], gate_t[l], g_post[l][None, :], sv["ys"], w_out_t[l], nb, sb)
        dy3 = dy.reshape(nb, s, D)
        r_do, r_dz, d_rng = mix_finish_bwd(_ret_finish, "ret_finish_bwd", *sv["ro"], p3, 3, ret_ng[l], bd, dy3, 0)
        r_d = ret_scan_bwd(p3, cos, sins, ret_c, sv["rs"], r_do, t_ctx)
        s_du, s_dv, s_dz, d_sgw, d_sgb = sg_bwd(p3, sg_w[l], sg_b[l], hm4, dy3)
        c_db, c_dc, c_dh, c_dz, d_scw = sc_bwd(p3, sc_w_full[l], dy3, t_ctx)
        g_do, g_dz, d_gng = mix_finish_bwd(_gdn_finish, "gdn_finish_bwd", *sv["go"], p3, 14, gdn_ng[l], bd, dy3, 3)
        g_d = gdn_scan_bwd(*sv["c"], p3, alog[l], dtb[l], gdn_c, sv["gs"], g_do, t_ctx)
        gx, d_gcw = [], []
        for i in range(3):
            dxi, dwi = gdn_conv_bwd(p3, gdn_w_full[l][:, BW * i:BW * (i + 1)], 11 + i, g_d[i], g_d[4 + i], t_ctx)
            gx.append(dxi)
            d_gcw.append(dwi)
        dp3 = assemble_dp([(r_d[0], r_d[3]), (r_d[1], r_d[4]), (r_d[2], r_d[5])],
                          [r_dz, s_du, s_dv, s_dz, c_db, c_dc, c_dh, c_dz], gx, [g_dz], [g_d[3], g_d[7]])
        dp = dp3.reshape(n, PW)
        dxs, dg_pre, dshift, dscale = inproj_bwd_x(dp, w_in_t[l], sv["x"], shift_t[l], scale_t[l], g_pre[l][None, :], dxs, nb, sb)
        gw_in[l] = dw_in(sv["h"], dp)
        g_small["g_pre"][l] = dg_pre[0]
        g_small["g_post"][l] = dg_post[0]
        g_small["ret_norm_g"][l] = d_rng[0]
        g_small["sg_w"][l] = d_sgw
        g_small["sg_b"][l] = d_sgb
        g_small["sc_conv_w"][l] = d_scw
        g_small["gdn_conv_w"][l] = jnp.concatenate(d_gcw, axis=-1)
        g_small["gdn_a_log"][l] = g_d[8][0, :8].reshape(2, NH)
        g_small["gdn_dt_bias"][l] = g_d[9][0, :8].reshape(2, NH)
        g_small["gdn_norm_g"][l] = d_gng[0].reshape(NH, HD)
        dm_rows[l] = jnp.concatenate([dshift, dscale, dgate], axis=-1)[:nb + 1]
    grad_x = dxs.reshape(nb, s, D)[:, t_ctx:, :]

    g_small = {k: jnp.stack(v) for k, v in g_small.items()}
    dm_rows = jnp.stack(dm_rows)
    dm_slot = jnp.zeros((nl, n_all + 8, 3 * D), f32)
    dm_slot = lax.dynamic_update_slice_in_dim(dm_slot, dm_rows[:, :nb], dev * nb, axis=1)
    dm_slot = lax.dynamic_update_slice_in_dim(dm_slot, dm_rows[:, nb:], n_all, axis=1)
    names2 = [k for k in SMALL if k not in ("c_ctx", "b_mod")]
    pack2 = _pack([g_small[k] for k in names2] + [dm_slot])
    tot2 = sum_lead(gather8(pack2), tr=PACK_ROWS)
    outs2 = _unpack(tot2, [g_small[k].shape for k in names2] + [dm_slot.shape])
    grads = dict(zip(names2, outs2[:-1]))
    dm_all = outs2[-1]
    grads["gdn_norm_g"] = sum_lead(jnp.transpose(grads["gdn_norm_g"], (1, 0, 2)), tr=nl)
    for k in ("sc_conv_w", "gdn_conv_w"):
        wc = weights[k].shape[2]
        grads[k] = lax.dynamic_slice_in_dim(grads[k], chip * wc, wc, axis=2)

    dm_cols = lax.dynamic_slice_in_dim(dm_all, chip * wc_mod, wc_mod, axis=2)
    g_w_mod, g_b_mod, dcc_part = mod_bwd(c_rows, w_mod, dm_cols, dm_all)
    grads["b_mod"] = g_b_mod[:, 0, :]
    grads["c_ctx"] = cctx_grad(gather8(dcc_part), c_ctx[None, :])[0]

    gw_in = jnp.stack(gw_in)[:, :, :IN_W].reshape(nl, D, N_CHIPS, wc_in)
    gw_in = jnp.transpose(gw_in, (2, 0, 1, 3)).astype(bf16).reshape(N_CHIPS, nl * D, wc_in)
    gin_mine = sum_lead(scatter4(gw_in))
    gin_sib = swap_sibling(gin_mine)
    gw_out = jnp.stack(gw_out).reshape(nl, N_CHIPS, rows_out, D)
    gw_out = jnp.transpose(gw_out, (1, 0, 2, 3)).astype(bf16).reshape(N_CHIPS, nl * rows_out, D)
    gout_mine = sum_lead(scatter4(gw_out))
    gout_sib = swap_sibling(gout_mine)

    res = {}
    res["w_in"] = [a.reshape(w_in.shape) for a in adamw(w_in.reshape(nl * D, wc_in), m_w_in.reshape(nl * D, wc_in),
                                                          v_w_in.reshape(nl * D, wc_in), gin_mine, gin_sib)]
    res["w_out"] = [a.reshape(w_out.shape) for a in adamw(w_out.reshape(nl * rows_out, D), m_w_out.reshape(nl * rows_out, D),
                                                            v_w_out.reshape(nl * rows_out, D), gout_mine, gout_sib)]
    res["w_mod"] = [a.reshape(w_mod.shape) for a in adamw(w_mod.reshape(nl * D, wc_mod), m_w_mod.reshape(nl * D, wc_mod),
                                                            v_w_mod.reshape(nl * D, wc_mod), g_w_mod.reshape(nl * D, wc_mod))]
    shapes = [weights[k].shape for k in SMALL]
    small = adamw(_pack([weights[k] for k in SMALL]), _pack([mom[k] for k in SMALL]), _pack([var[k] for k in SMALL]),
                  _pack([grads[k].reshape(weights[k].shape) for k in SMALL]), tr=PACK_ROWS)
    small = [_unpack(a, shapes) for a in small]
    for i, k in enumerate(SMALL):
        res[k] = [small[j][i] for j in range(4)]

    order = ["c_ctx", "w_mod", "b_mod", "g_pre", "g_post", "w_in", "w_out", "ret_norm_g", "sg_w", "sg_b", "sc_conv_w",
             "gdn_conv_w", "gdn_a_log", "gdn_dt_bias", "gdn_norm_g"]
    return (loss, grad_x, *[res[k][0] for k in order], *[res[k][1] for k in order], *[res[k][2] for k in order],
            *[res[k][3] for k in order])
```

```python
import functools

import jax
import jax.numpy as jnp
import numpy as np
from jax import lax
from jax.experimental import pallas as pl
from jax.experimental.pallas import tpu as pltpu

f32 = jnp.float32
bf16 = jnp.bfloat16
HI = lax.Precision.HIGHEST
P3 = lax.Precision.HIGH
MESH = pl.DeviceIdType.MESH

EPS = 1e-6
D = 1024
NH = 4
HD = 64
BW = NH * HD
RC = 128
GC = 64
GRID_W = 64
ROPE_BASE = 10000.0
IN_W = 15 * BW + 16
PW = 4096
GATE_COL = 15 * BW
N_CHIPS = 4
N_DEV = 8
TM = 256
ADAM_LR, ADAM_B1, ADAM_B2, ADAM_EPS, ADAM_WD, ADAM_STEP = 0.001, 0.9, 0.999, 1e-08, 0.01, 10
LANE_HEAD = np.arange(BW) // HD
VMEM_BIG = 56 * 1024 * 1024


def _dot(a, b, precision=None):
    return jnp.dot(a, b, precision=precision, preferred_element_type=f32)


def _dot_nt(a, b, precision=None):
    return lax.dot_general(a, b, (((1,), (1,)), ((), ())), precision=precision, preferred_element_type=f32)


def _dot_tn(a, b, precision=None):
    return lax.dot_general(a, b, (((0,), (0,)), ((), ())), precision=precision, preferred_element_type=f32)


def _sds(shape, dtype=f32):
    return jax.ShapeDtypeStruct(shape, dtype)


def _cparams(sem=None, vmem=None):
    kw = {}
    if sem is not None:
        kw["dimension_semantics"] = sem
    if vmem is not None:
        kw["vmem_limit_bytes"] = vmem
    return pltpu.CompilerParams(**kw)


def _full(shape):
    n = len(shape)
    return pl.BlockSpec(shape, lambda *_: (0,) * n)


def _head_masks():
    return np.stack([(LANE_HEAD == h).astype(np.float32)[None, :] for h in range(NH)])


def _block_diag():
    return (LANE_HEAD[:, None] == LANE_HEAD[None, :]).astype(np.float32)


def _tau(c, d):
    return np.arange(c) if d == 0 else c - 1 - np.arange(c)


def _ret_consts():
    lg = np.log(1.0 - 2.0 ** (-5.0 - np.arange(NH)))
    intra = np.zeros((2, NH, RC, RC)); qdec = np.zeros((2, RC, BW)); kdec = np.zeros((2, RC, BW))
    for d in range(2):
        t = _tau(RC, d)
        diff = t[:, None] - t[None, :]
        for h in range(NH):
            intra[d, h] = np.where(diff >= 0, np.exp(np.maximum(diff, 0) * lg[h]), 0.0)
        qdec[d] = np.exp((t[:, None] + 1.0) * lg[LANE_HEAD][None, :])
        kdec[d] = np.exp((RC - 1.0 - t[:, None]) * lg[LANE_HEAD][None, :])
    cd = np.exp(RC * lg[LANE_HEAD])[:, None] * np.ones((1, BW))
    return [jnp.asarray(a, f32) for a in (intra, qdec, kdec, cd)]


def _rope_tables(t_lat, t_ctx):
    nf = HD // 4
    inv = ROPE_BASE ** (-np.arange(nf) / nf)
    pos = np.arange(t_lat)
    ang_r = (pos // GRID_W)[:, None] * inv[None, :]
    ang_c = (pos % GRID_W)[:, None] * inv[None, :]
    ang = np.concatenate([ang_r, ang_r, ang_c, ang_c], axis=1)
    sign = np.concatenate([-np.ones(nf), np.ones(nf), -np.ones(nf), np.ones(nf)])
    cos = np.tile(np.cos(ang), (1, NH)); sins = np.tile(np.sin(ang) * sign, (1, NH))
    cos = np.concatenate([np.ones((t_ctx, BW)), cos]); sins = np.concatenate([np.zeros((t_ctx, BW)), sins])
    return jnp.asarray(cos, f32), jnp.asarray(sins, f32)


def _gdn_consts():
    tmask = np.zeros((2, GC, GC)); mask_bd = np.zeros((2, BW, BW)); strict_bd = np.zeros((2, BW, BW))
    exp_g = np.zeros((2, 128, BW)); exp_b = np.zeros((2, 128, BW))
    blk = (np.arange(BW)[:, None] // GC == np.arange(BW)[None, :] // GC)
    for d in range(2):
        t = _tau(GC, d)
        tmask[d] = (t[:, None] >= t[None, :])
        mask_bd[d] = blk & np.tile(t[:, None] >= t[None, :], (NH, NH))
        strict_bd[d] = blk & np.tile(t[:, None] > t[None, :], (NH, NH))
        for h in range(NH):
            exp_g[d, 4 * d + h, LANE_HEAD == h] = 1.0
            exp_b[d, 8 + 4 * d + h, LANE_HEAD == h] = 1.0
    exp_gt = np.transpose(exp_g, (0, 2, 1))
    dsel = np.tile(np.eye(GC), (1, NH))
    sel1 = (np.arange(BW)[None, :] == (np.arange(BW)[:, None] // GC) * HD)
    return [jnp.asarray(a, f32) for a in (tmask, mask_bd, strict_bd, exp_g, exp_b, exp_gt, dsel, sel1)]


def _swap16(x):
    lane = lax.broadcasted_iota(jnp.int32, x.shape, x.ndim - 1)
    n = x.shape[-1]
    return jnp.where(lane % 32 < 16, pltpu.roll(x, n - 16, axis=x.ndim - 1), pltpu.roll(x, 16, axis=x.ndim - 1))


@jax.custom_vjp
def _rot(x, cos, sins):
    return x * cos + _swap16(x) * sins


def _rot_fwd(x, cos, sins):
    return _rot(x, cos, sins), (cos, sins)


def _rot_bwd(res, g):
    cos, sins = res
    return g * cos + _swap16(g * sins), jnp.zeros_like(cos), jnp.zeros_like(sins)


_rot.defvjp(_rot_fwd, _rot_bwd)


def _silu(z):
    return z * jax.nn.sigmoid(z)


def _head_sum(x, bd):
    return _dot(x, bd, precision=HI)


def _ret_step(s, q, k, v, cos, sins, intra, qdec, kdec, cd, bd, hm):
    qr = _rot(q, cos, sins)
    kr = _rot(k, cos, sins) * (HD ** -0.5)
    o = _dot(qr * qdec, s)
    for h in range(NH):
        sc = _dot_nt(qr * hm[h], kr) * intra[h]
        o = o + _dot(sc, v) * hm[h]
    s_new = s * cd + bd * _dot_tn(kr * kdec, v)
    return s_new, o


def _ret_finish(o_f, o_b, z, norm_g, bd):
    o = o_f + o_b
    mu = _head_sum(o, bd) * (1.0 / HD)
    xc = o - mu
    var = _head_sum(xc * xc, bd) * (1.0 / HD)
    return xc * lax.rsqrt(var + EPS) * norm_g * _silu(z)


def _softplus(x):
    return jnp.maximum(x, 0.0) + jnp.log(1.0 + jnp.exp(-jnp.abs(x)))


def _stack(a):
    return jnp.concatenate([a] * NH, axis=0)


@jax.custom_vjp
def _unstack(a):
    c = a.shape[0] // NH
    return a[0:c] + a[c:2 * c] + a[2 * c:3 * c] + a[3 * c:4 * c]


_unstack.defvjp(lambda a: (_unstack(a), None), lambda _, g: (_stack(g),))


def _gdn_step(s, q, k, v, gate, alog, dtb, tmask, mask_bd, strict_bd, exp_g, exp_b, exp_gt, dsel, sel1, bd):
    qn = q * lax.rsqrt(_dot(q * q, bd, P3) + EPS)
    kn = k * lax.rsqrt(_dot(k * k, bd, P3) + EPS)
    g = -jnp.exp(alog) * _softplus(gate + dtb)
    beta = jax.nn.sigmoid(gate)
    gc = _dot(tmask, g, P3)
    glast = jnp.sum(g, axis=0, keepdims=True)
    gc_l = _dot(gc, exp_g, P3)
    glast_l = _dot(glast, exp_g, P3)
    beta_l = _dot(beta, exp_b, P3)
    eg = jnp.exp(gc_l)
    kb = kn * beta_l
    vb = v * beta_l
    kbg = kb * eg
    qs = qn * (HD ** -0.5)
    gc_r = jnp.sum(gc_l * dsel, axis=0, keepdims=True)
    gc_s = jnp.sum(_stack(gc_l) * sel1, axis=-1, keepdims=True)
    dec = jnp.exp(jnp.where(mask_bd > 0, gc_s - gc_r, -1e30))
    kns = _stack(kn)
    eye = (lax.broadcasted_iota(jnp.int32, (BW, BW), 0) == lax.broadcasted_iota(jnp.int32, (BW, BW), 1)).astype(f32)
    m = -(_dot_nt(_stack(kb) * bd, kns) * dec * strict_bd)
    inv = eye + m
    p = m
    for _ in range(5):
        p = _dot(p, p, P3)
        inv = inv + _dot(inv, p, P3)
    u = _unstack(_dot(inv, _stack(vb) * bd, P3))
    w = _unstack(_dot(inv, _stack(kbg) * bd, P3))
    v_new = u - _dot(w, s)
    k_tail = kn * jnp.exp(glast_l - gc_l)
    cdec = jnp.sum(exp_gt * jnp.exp(glast), axis=-1, keepdims=True)
    s_new = s * cdec + bd * _dot_tn(k_tail, v_new)
    a = _dot_nt(_stack(qs) * bd, kns) * dec
    o = _dot(qs * eg, s) + _unstack(_dot(a, _stack(v_new) * bd))
    return s_new, o


def _gdn_finish(o_f, o_b, z, norm_g, bd):
    o = o_f + o_b
    ms = _head_sum(o * o, bd) * (1.0 / HD)
    return o * lax.rsqrt(ms + EPS) * norm_g * _silu(z)


def _gelu(x):
    return 0.5 * x * (1.0 + jnp.tanh(0.7978845608028654 * (x + 0.044715 * (x * x * x))))


def _sg_chunk(u, v, z, w, b, hm4):
    u = _gelu(u)
    gv = _gelu(v)
    mu = jnp.mean(gv, axis=-1, keepdims=True)
    xc = gv - mu
    var = jnp.mean(xc * xc, axis=-1, keepdims=True)
    vn = xc * lax.rsqrt(var + EPS)
    s = _dot_tn(b, hm4, precision=HI)
    for h in range(NH):
        s = s + _dot(w[h], vn) * hm4[h:h + 1]
    return u * s * _silu(z)


def _make_shifts(t_ctx, n):
    def dn(x):
        t = lax.broadcasted_iota(jnp.int32, x.shape, 0)
        return jnp.where((t != 0) & (t != t_ctx), pltpu.roll(x, 1, axis=0), 0.0)

    def up(x):
        t = lax.broadcasted_iota(jnp.int32, x.shape, 0)
        return jnp.where((t != t_ctx - 1) & (t != n - 1), pltpu.roll(x, n - 1, axis=0), 0.0)

    @jax.custom_vjp
    def shift_dn(x):
        return dn(x)
    shift_dn.defvjp(lambda x: (dn(x), None), lambda _, g: (up(g),))

    @jax.custom_vjp
    def shift_up(x):
        return up(x)
    shift_up.defvjp(lambda x: (up(x), None), lambda _, g: (dn(g),))
    return shift_dn, shift_up


def _conv3(x, w, shift_dn, shift_up):
    return shift_dn(x) * w[0:1] + x * w[1:2] + shift_up(x) * w[2:3]


def inproj_fwd(x, shift_t, scale_t, g_pre, w_in, n_batch, sb):
    n = x.shape[0]

    def sel(i):
        return jnp.where(i % sb == 0, n_batch, i // sb)

    def body(x_ref, sh_ref, sc_ref, g_ref, w_ref, p_ref, h_ref):
        xv = x_ref[...]
        r = xv * lax.rsqrt(jnp.mean(xv * xv, axis=-1, keepdims=True) + EPS)
        h = (r * g_ref[...]) * (1.0 + sc_ref[0]) + sh_ref[0]
        hb = h.astype(bf16)
        h_ref[...] = hb
        p_ref[...] = _dot(hb, w_ref[...])

    return pl.pallas_call(
        body, name="inproj_fwd", grid=(n // TM,),
        in_specs=[pl.BlockSpec((TM, D), lambda i: (i, 0)),
                  pl.BlockSpec((1, 1, D), lambda i: (sel(i), 0, 0)),
                  pl.BlockSpec((1, 1, D), lambda i: (sel(i), 0, 0)),
                  _full((1, D)), _full((D, PW))],
        out_specs=[pl.BlockSpec((TM, PW), lambda i: (i, 0)), pl.BlockSpec((TM, D), lambda i: (i, 0))],
        out_shape=[_sds((n, PW)), _sds((n, D), bf16)],
        compiler_params=_cparams(("arbitrary",), VMEM_BIG),
    )(x, shift_t, scale_t, g_pre, w_in)


def outproj_fwd(ys, w_out, x, gate_t, g_post, n_batch, sb):
    n = x.shape[0]

    def sel(i):
        return jnp.where(i % sb == 0, n_batch, i // sb)

    def body(y0, y1, y2, y3, w_ref, x_ref, gt_ref, g_ref, xn_ref, o_ref):
        y = jnp.concatenate([y0[...], y1[...], y2[...], y3[...]], axis=1)
        o = _dot(y, w_ref[...])
        o_ref[...] = o
        nrm = o * lax.rsqrt(jnp.mean(o * o, axis=-1, keepdims=True) + EPS) * g_ref[...]
        xn_ref[...] = x_ref[...] + gt_ref[0] * nrm

    yspec = pl.BlockSpec((TM, BW), lambda i: (i, 0))
    return pl.pallas_call(
        body, name="outproj_fwd", grid=(n // TM,),
        in_specs=[yspec, yspec, yspec, yspec, _full((D, D)), pl.BlockSpec((TM, D), lambda i: (i, 0)),
                  pl.BlockSpec((1, 1, D), lambda i: (sel(i), 0, 0)), _full((1, D))],
        out_specs=[pl.BlockSpec((TM, D), lambda i: (i, 0)), pl.BlockSpec((TM, D), lambda i: (i, 0))],
        out_shape=[_sds((n, D)), _sds((n, D))],
        compiler_params=_cparams(("arbitrary",)),
    )(*ys, w_out, x, gate_t, g_post)


def _row_onehot(r):
    return lax.broadcasted_iota(jnp.int32, (8, 1), 0) == r


def outproj_bwd(dxn, o, gate_t, g_post, ys, w_out_t, n_batch, sb):
    n = dxn.shape[0]

    def sel(i):
        return jnp.where(i % sb == 0, n_batch, i // sb)

    def body(dxn_ref, o_ref, gt_ref, g_ref, y0, y1, y2, y3, wt_ref, dy_ref, dw_ref, dg_ref, dgate_ref):
        i = pl.program_id(0)

        @pl.when(i == 0)
        def _():
            dw_ref[...] = jnp.zeros_like(dw_ref)
            dg_ref[...] = jnp.zeros_like(dg_ref)
            dgate_ref[...] = jnp.zeros_like(dgate_ref)

        ov = o_ref[...]
        rstd = lax.rsqrt(jnp.mean(ov * ov, axis=-1, keepdims=True) + EPS)
        r = ov * rstd
        g = g_ref[...]
        dx = dxn_ref[...]
        dgate_ref[...] += jnp.where(_row_onehot(sel(i)), jnp.sum(dx * (r * g), axis=0, keepdims=True), 0.0)
        dn = dx * gt_ref[0]
        dg_ref[...] += jnp.sum(dn * r, axis=0, keepdims=True)
        dr = dn * g
        do = rstd * (dr - r * jnp.mean(dr * r, axis=-1, keepdims=True))
        dob = do.astype(bf16)
        dy_ref[...] = _dot(dob, wt_ref[...])
        y = jnp.concatenate([y0[...], y1[...], y2[...], y3[...]], axis=1)
        dw_ref[...] += _dot_tn(y, dob)

    yspec = pl.BlockSpec((TM, BW), lambda i: (i, 0))
    row = pl.BlockSpec((TM, D), lambda i: (i, 0))
    return pl.pallas_call(
        body, name="outproj_bwd", grid=(n // TM,),
        in_specs=[row, row, pl.BlockSpec((1, 1, D), lambda i: (sel(i), 0, 0)), _full((1, D)),
                  yspec, yspec, yspec, yspec, _full((D, D))],
        out_specs=[row, _full((D, D)), _full((1, D)), _full((8, D))],
        out_shape=[_sds((n, D)), _sds((D, D)), _sds((1, D)), _sds((8, D))],
        compiler_params=_cparams(("arbitrary",), VMEM_BIG),
    )(dxn, o, gate_t, g_post, *ys, w_out_t)


def inproj_bwd_x(dp, w_in_t, x, shift_t, scale_t, g_pre, dxn, n_batch, sb):
    n = x.shape[0]

    def sel(i):
        return jnp.where(i % sb == 0, n_batch, i // sb)

    def body(dp_ref, wt_ref, x_ref, sc_ref, g_ref, dxn_ref, dx_ref, dg_ref, dsh_ref, dsc_ref):
        i = pl.program_id(0)

        @pl.when(i == 0)
        def _():
            dg_ref[...] = jnp.zeros_like(dg_ref)
            dsh_ref[...] = jnp.zeros_like(dsh_ref)
            dsc_ref[...] = jnp.zeros_like(dsc_ref)

        dh = _dot(dp_ref[...], wt_ref[...])
        xv = x_ref[...]
        rstd = lax.rsqrt(jnp.mean(xv * xv, axis=-1, keepdims=True) + EPS)
        r = xv * rstd
        g = g_ref[...]
        hot = _row_onehot(sel(i))
        dsh_ref[...] += jnp.where(hot, jnp.sum(dh, axis=0, keepdims=True), 0.0)
        dsc_ref[...] += jnp.where(hot, jnp.sum(dh * (r * g), axis=0, keepdims=True), 0.0)
        t = dh * (1.0 + sc_ref[0])
        dg_ref[...] += jnp.sum(t * r, axis=0, keepdims=True)
        dr = t * g
        dx_ref[...] = dxn_ref[...] + rstd * (dr - r * jnp.mean(dr * r, axis=-1, keepdims=True))

    row = pl.BlockSpec((TM, D), lambda i: (i, 0))
    return pl.pallas_call(
        body, name="inproj_bwd_x", grid=(n // TM,),
        in_specs=[pl.BlockSpec((TM, PW), lambda i: (i, 0)), _full((PW, D)), row,
                  pl.BlockSpec((1, 1, D), lambda i: (sel(i), 0, 0)), _full((1, D)), row],
        out_specs=[row, _full((1, D)), _full((8, D)), _full((8, D))],
        out_shape=[_sds((n, D)), _sds((1, D)), _sds((8, D)), _sds((8, D))],
        compiler_params=_cparams(("arbitrary",), VMEM_BIG),
    )(dp, w_in_t, x, scale_t, g_pre, dxn)


def dw_in(h, dp):
    n = h.shape[0]
    tk, tn = 512, 1024

    def body(h_ref, dp_ref, o_ref):
        @pl.when(pl.program_id(1) == 0)
        def _():
            o_ref[...] = jnp.zeros_like(o_ref)
        o_ref[...] += _dot_tn(h_ref[...], dp_ref[...])

    return pl.pallas_call(
        body, name="dw_in", grid=(PW // tn, n // tk),
        in_specs=[pl.BlockSpec((tk, D), lambda j, k: (k, 0)), pl.BlockSpec((tk, tn), lambda j, k: (k, j))],
        out_specs=pl.BlockSpec((D, tn), lambda j, k: (0, j)),
        out_shape=_sds((D, PW)),
        compiler_params=_cparams(("parallel", "arbitrary"), VMEM_BIG),
    )(h, dp)


def loss_head(xf, target, t_ctx):
    nb, s, _ = xf.shape
    jc = t_ctx // TM

    def body(x_ref, t_ref, dx_ref, l_ref):
        b, j = pl.program_id(0), pl.program_id(1)

        @pl.when((b == 0) & (j == 0))
        def _():
            l_ref[...] = jnp.zeros_like(l_ref)

        @pl.when(j < jc)
        def _():
            dx_ref[...] = jnp.zeros_like(dx_ref)

        @pl.when(j >= jc)
        def _():
            diff = x_ref[0] - t_ref[0]
            dx_ref[0] = diff * (1.0 / D)
            l_ref[...] += 0.5 * jnp.sum(diff * diff) * (1.0 / D)

    return pl.pallas_call(
        body, name="loss_head", grid=(nb, s // TM),
        in_specs=[pl.BlockSpec((1, TM, D), lambda b, j: (b, j, 0)),
                  pl.BlockSpec((1, TM, D), lambda b, j: (b, jnp.maximum(j - jc, 0), 0))],
        out_specs=[pl.BlockSpec((1, TM, D), lambda b, j: (b, j, 0)), _full((1, 128))],
        out_shape=[_sds((nb, s, D)), _sds((1, 128))],
        compiler_params=_cparams(("arbitrary", "arbitrary")),
    )(xf, target)


def _chunk_maps(n_ctx, n_lat):
    n = n_ctx + n_lat

    def cf(t):
        return t

    def cb(t):
        return jnp.where(t < n_ctx, n_ctx - 1 - t, n - 1 - t + n_ctx)
    return n, cf, cb


def ret_scan_fwd(p3, cos, sins, consts, t_ctx):
    nb, s, _ = p3.shape
    n, cf, cb = _chunk_maps(t_ctx // RC, (s - t_ctx) // RC)
    intra, qdec, kdec, cd, bd, hm = consts
    cmaps = (cf, cb)

    def body(qf, kf, vf, qb, kb, vb, cosf, sinf, cosb, sinb, intra_r, qdec_r, kdec_r, cd_r, bd_r, hm_r,
             of_ref, ob_ref, sall_ref, s_sc):
        @pl.when(pl.program_id(1) == 0)
        def _():
            s_sc[...] = jnp.zeros_like(s_sc)
        ins = ((qf, kf, vf, cosf, sinf, of_ref), (qb, kb, vb, cosb, sinb, ob_ref))
        for d, (q, k, v, c_, s_, o_ref) in enumerate(ins):
            st = s_sc[d]
            sall_ref[0, d, 0] = st
            s_new, o = _ret_step(st, q[0], k[0], v[0], c_[...], s_[...], intra_r[d], qdec_r[d], kdec_r[d],
                                 cd_r[...], bd_r[...], hm_r[...])
            s_sc[d] = s_new
            o_ref[0] = o

    def pspec(m, seg):
        return pl.BlockSpec((1, RC, BW), lambda b, t: (b, m(t), seg))

    def tspec(m):
        return pl.BlockSpec((RC, BW), lambda b, t: (m(t), 0))

    return pl.pallas_call(
        body, name="ret_scan_fwd", grid=(nb, n),
        in_specs=[pspec(cf, 0), pspec(cf, 1), pspec(cf, 2), pspec(cb, 0), pspec(cb, 1), pspec(cb, 2),
                  tspec(cf), tspec(cf), tspec(cb), tspec(cb),
                  _full(intra.shape), _full(qdec.shape), _full(kdec.shape), _full(cd.shape), _full(bd.shape),
                  _full(hm.shape)],
        out_specs=[pl.BlockSpec((1, RC, BW), lambda b, t: (b, cf(t), 0)),
                   pl.BlockSpec((1, RC, BW), lambda b, t: (b, cb(t), 0)),
                   pl.BlockSpec((1, 2, 1, BW, BW), lambda b, t: (b, 0, t, 0, 0))],
        out_shape=[_sds((nb, s, BW)), _sds((nb, s, BW)), _sds((nb, 2, n, BW, BW))],
        scratch_shapes=[pltpu.VMEM((2, BW, BW), f32)],
        compiler_params=_cparams(("arbitrary", "arbitrary")),
    )(p3, p3, p3, p3, p3, p3, cos, sins, cos, sins, intra, qdec, kdec, cd, bd, hm)


def ret_scan_bwd(p3, cos, sins, consts, s_all, do, t_ctx):
    nb, s, _ = p3.shape
    n, cf, cb = _chunk_maps(t_ctx // RC, (s - t_ctx) // RC)
    intra, qdec, kdec, cd, bd, hm = consts

    def rf(t):
        return cf(n - 1 - t)

    def rb(t):
        return cb(n - 1 - t)

    def body(qf, kf, vf, qb, kb, vb, cosf, sinf, cosb, sinb, intra_r, qdec_r, kdec_r, cd_r, bd_r, hm_r,
             sall_ref, dof, dob, dqf, dkf, dvf, dqb, dkb, dvb, ds_sc):
        @pl.when(pl.program_id(1) == 0)
        def _():
            ds_sc[...] = jnp.zeros_like(ds_sc)
        ins = ((qf, kf, vf, cosf, sinf, dof, (dqf, dkf, dvf)), (qb, kb, vb, cosb, sinb, dob, (dqb, dkb, dvb)))
        for d, (q, k, v, c_, s_, do_ref, outs) in enumerate(ins):
            step = functools.partial(_ret_step, cos=c_[...], sins=s_[...], intra=intra_r[d], qdec=qdec_r[d],
                                     kdec=kdec_r[d], cd=cd_r[...], bd=bd_r[...], hm=hm_r[...])
            _, vjp = jax.vjp(step, sall_ref[0, d, 0], q[0], k[0], v[0])
            ds, dq, dk, dv = vjp((ds_sc[d], do_ref[0]))
            ds_sc[d] = ds
            outs[0][0] = dq
            outs[1][0] = dk
            outs[2][0] = dv

    def pspec(m, seg):
        return pl.BlockSpec((1, RC, BW), lambda b, t: (b, m(t), seg))

    def tspec(m):
        return pl.BlockSpec((RC, BW), lambda b, t: (m(t), 0))

    def ospec(m):
        return pl.BlockSpec((1, RC, BW), lambda b, t: (b, m(t), 0))

    return pl.pallas_call(
        body, name="ret_scan_bwd", grid=(nb, n),
        in_specs=[pspec(rf, 0), pspec(rf, 1), pspec(rf, 2), pspec(rb, 0), pspec(rb, 1), pspec(rb, 2),
                  tspec(rf), tspec(rf), tspec(rb), tspec(rb),
                  _full(intra.shape), _full(qdec.shape), _full(kdec.shape), _full(cd.shape), _full(bd.shape),
                  _full(hm.shape),
                  pl.BlockSpec((1, 2, 1, BW, BW), lambda b, t: (b, 0, n - 1 - t, 0, 0)), ospec(rf), ospec(rb)],
        out_specs=[ospec(rf), ospec(rf), ospec(rf), ospec(rb), ospec(rb), ospec(rb)],
        out_shape=[_sds((nb, s, BW))] * 6,
        scratch_shapes=[pltpu.VMEM((2, BW, BW), f32)],
        compiler_params=_cparams(("arbitrary", "arbitrary")),
    )(p3, p3, p3, p3, p3, p3, cos, sins, cos, sins, intra, qdec, kdec, cd, bd, hm, s_all, do, do)


def mix_finish_fwd(fn, name, o_f, o_b, p3, zseg, norm_g, bd):
    nb, s, _ = p3.shape

    def body(of_ref, ob_ref, z_ref, g_ref, bd_ref, y_ref):
        y_ref[0] = fn(of_ref[0], ob_ref[0], z_ref[0], g_ref[...], bd_ref[...]).astype(bf16)

    blk = pl.BlockSpec((1, TM, BW), lambda b, j: (b, j, 0))
    return pl.pallas_call(
        body, name=name, grid=(nb, s // TM),
        in_specs=[blk, blk, pl.BlockSpec((1, TM, BW), lambda b, j: (b, j, zseg)), _full((1, BW)), _full((BW, BW))],
        out_specs=blk, out_shape=_sds((nb, s, BW), bf16),
        compiler_params=_cparams(("arbitrary", "arbitrary")),
    )(o_f, o_b, p3, norm_g, bd)


def mix_finish_bwd(fn, name, o_f, o_b, p3, zseg, norm_g, bd, dy3, yseg):
    nb, s, _ = p3.shape

    def body(of_ref, ob_ref, z_ref, g_ref, bd_ref, dy_ref, do_ref, dz_ref, dg_ref):
        @pl.when((pl.program_id(0) == 0) & (pl.program_id(1) == 0))
        def _():
            dg_ref[...] = jnp.zeros_like(dg_ref)
        bdv = bd_ref[...]
        _, vjp = jax.vjp(lambda a, b, z, g: fn(a, b, z, g, bdv), of_ref[0], ob_ref[0], z_ref[0], g_ref[...])
        do, _, dz, dg = vjp(dy_ref[0])
        do_ref[0] = do
        dz_ref[0] = dz
        dg_ref[...] += dg

    blk = pl.BlockSpec((1, TM, BW), lambda b, j: (b, j, 0))
    return pl.pallas_call(
        body, name=name, grid=(nb, s // TM),
        in_specs=[blk, blk, pl.BlockSpec((1, TM, BW), lambda b, j: (b, j, zseg)), _full((1, BW)), _full((BW, BW)),
                  pl.BlockSpec((1, TM, BW), lambda b, j: (b, j, yseg))],
        out_specs=[blk, blk, _full((1, BW))],
        out_shape=[_sds((nb, s, BW)), _sds((nb, s, BW)), _sds((1, BW))],
        compiler_params=_cparams(("arbitrary", "arbitrary")),
    )(o_f, o_b, p3, norm_g, bd, dy3)


def gdn_conv_fwd(p3, w, seg, t_ctx):
    nb, s, _ = p3.shape
    sd, su = _make_shifts(t_ctx, s)

    def body(x_ref, w_ref, o_ref):
        o_ref[0] = _silu(_conv3(x_ref[0], w_ref[...], sd, su))

    return pl.pallas_call(
        body, name="gdn_conv_fwd", grid=(nb, 2),
        in_specs=[pl.BlockSpec((1, s, 128), lambda b, j: (b, 0, 2 * seg + j)), pl.BlockSpec((3, 128), lambda b, j: (0, j))],
        out_specs=pl.BlockSpec((1, s, 128), lambda b, j: (b, 0, j)),
        out_shape=_sds((nb, s, BW)),
        compiler_params=_cparams(("arbitrary", "arbitrary")),
    )(p3, w)


def gdn_conv_bwd(p3, w, seg, d_f, d_b, t_ctx):
    nb, s, _ = p3.shape
    sd, su = _make_shifts(t_ctx, s)

    def body(x_ref, w_ref, df_ref, db_ref, dx_ref, dw_ref):
        @pl.when(pl.program_id(1) == 0)
        def _():
            dw_ref[...] = jnp.zeros_like(dw_ref)
        _, vjp = jax.vjp(lambda x, w_: _silu(_conv3(x, w_, sd, su)), x_ref[0], w_ref[...])
        dx, dw = vjp(df_ref[0] + db_ref[0])
        dx_ref[0] = dx
        dw_ref[...] += dw

    blk = pl.BlockSpec((1, s, 128), lambda j, b: (b, 0, j))
    return pl.pallas_call(
        body, name="gdn_conv_bwd", grid=(2, nb),
        in_specs=[pl.BlockSpec((1, s, 128), lambda j, b: (b, 0, 2 * seg + j)), pl.BlockSpec((3, 128), lambda j, b: (0, j)),
                  blk, blk],
        out_specs=[blk, pl.BlockSpec((3, 128), lambda j, b: (0, j))],
        out_shape=[_sds((nb, s, BW)), _sds((3, BW))],
        compiler_params=_cparams(("arbitrary", "arbitrary"), VMEM_BIG),
    )(p3, w, d_f, d_b)


def gdn_scan_fwd(cq, ck, cv, p3, alog, dtb, consts, t_ctx):
    nb, s, _ = p3.shape
    n, cf, cb = _chunk_maps(t_ctx // GC, (s - t_ctx) // GC)
    gblk = GATE_COL // 128

    def body(qf, kf, vf, gf, qb, kb, vb, gb, al_ref, dt_ref, tm_r, mb_r, sb_r, eg_r, eb_r, egt_r, dsel_r, sel1_r, bd_r,
             of_ref, ob_ref, sall_ref, s_sc):
        @pl.when(pl.program_id(1) == 0)
        def _():
            s_sc[...] = jnp.zeros_like(s_sc)
        ins = ((qf, kf, vf, gf, of_ref), (qb, kb, vb, gb, ob_ref))
        for d, (q, k, v, g, o_ref) in enumerate(ins):
            st = s_sc[d]
            sall_ref[0, d, 0] = st
            s_new, o = _gdn_step(st, q[0], k[0], v[0], g[0], al_ref[...], dt_ref[...], tm_r[d], mb_r[d], sb_r[d], eg_r[d],
                                 eb_r[d], egt_r[d], dsel_r[...], sel1_r[...], bd_r[...])
            s_sc[d] = s_new
            o_ref[0] = o

    def cspec(m):
        return pl.BlockSpec((1, GC, BW), lambda b, t: (b, m(t), 0))

    def gspec(m):
        return pl.BlockSpec((1, GC, 128), lambda b, t: (b, m(t), gblk))

    return pl.pallas_call(
        body, name="gdn_scan_fwd", grid=(nb, n),
        in_specs=[cspec(cf), cspec(cf), cspec(cf), gspec(cf), cspec(cb), cspec(cb), cspec(cb), gspec(cb),
                  _full((1, 128)), _full((1, 128))] + [_full(c.shape) for c in consts],
        out_specs=[cspec(cf), cspec(cb), pl.BlockSpec((1, 2, 1, BW, BW), lambda b, t: (b, 0, t, 0, 0))],
        out_shape=[_sds((nb, s, BW)), _sds((nb, s, BW)), _sds((nb, 2, n, BW, BW))],
        scratch_shapes=[pltpu.VMEM((2, BW, BW), f32)],
        compiler_params=_cparams(("arbitrary", "arbitrary")),
    )(cq, ck, cv, p3, cq, ck, cv, p3, alog, dtb, *consts)


def gdn_scan_bwd(cq, ck, cv, p3, alog, dtb, consts, s_all, do, t_ctx):
    nb, s, _ = p3.shape
    n, cf, cb = _chunk_maps(t_ctx // GC, (s - t_ctx) // GC)
    gblk = GATE_COL // 128

    def rf(t):
        return cf(n - 1 - t)

    def rb(t):
        return cb(n - 1 - t)

    def body(qf, kf, vf, gf, qb, kb, vb, gb, al_ref, dt_ref, tm_r, mb_r, sb_r, eg_r, eb_r, egt_r, dsel_r, sel1_r, bd_r,
             sall_ref, dof, dob, dqf, dkf, dvf, dgf, dqb, dkb, dvb, dgb, dal_ref, ddt_ref, ds_sc):
        @pl.when((pl.program_id(0) == 0) & (pl.program_id(1) == 0))
        def _():
            dal_ref[...] = jnp.zeros_like(dal_ref)
            ddt_ref[...] = jnp.zeros_like(ddt_ref)

        @pl.when(pl.program_id(1) == 0)
        def _():
            ds_sc[...] = jnp.zeros_like(ds_sc)
        ins = ((qf, kf, vf, gf, dof, (dqf, dkf, dvf, dgf)), (qb, kb, vb, gb, dob, (dqb, dkb, dvb, dgb)))
        for d, (q, k, v, g, do_ref, outs) in enumerate(ins):
            step = functools.partial(_gdn_step, tmask=tm_r[d], mask_bd=mb_r[d], strict_bd=sb_r[d], exp_g=eg_r[d],
                                     exp_b=eb_r[d], exp_gt=egt_r[d], dsel=dsel_r[...], sel1=sel1_r[...], bd=bd_r[...])
            _, vjp = jax.vjp(step, sall_ref[0, d, 0], q[0], k[0], v[0], g[0], al_ref[...], dt_ref[...])
            ds, dq, dk, dv, dg, dal, ddt = vjp((ds_sc[d], do_ref[0]))
            ds_sc[d] = ds
            outs[0][0] = dq
            outs[1][0] = dk
            outs[2][0] = dv
            outs[3][0] = dg
            dal_ref[...] += dal
            ddt_ref[...] += ddt

    def cspec(m):
        return pl.BlockSpec((1, GC, BW), lambda b, t: (b, m(t), 0))

    def gspec(m):
        return pl.BlockSpec((1, GC, 128), lambda b, t: (b, m(t), gblk))

    def gout(m):
        return pl.BlockSpec((1, GC, 128), lambda b, t: (b, m(t), 0))

    return pl.pallas_call(
        body, name="gdn_scan_bwd", grid=(nb, n),
        in_specs=[cspec(rf), cspec(rf), cspec(rf), gspec(rf), cspec(rb), cspec(rb), cspec(rb), gspec(rb),
                  _full((1, 128)), _full((1, 128))] + [_full(c.shape) for c in consts]
                 + [pl.BlockSpec((1, 2, 1, BW, BW), lambda b, t: (b, 0, n - 1 - t, 0, 0)), cspec(rf), cspec(rb)],
        out_specs=[cspec(rf), cspec(rf), cspec(rf), gout(rf), cspec(rb), cspec(rb), cspec(rb), gout(rb),
                   _full((1, 128)), _full((1, 128))],
        out_shape=[_sds((nb, s, BW))] * 3 + [_sds((nb, s, 128))] + [_sds((nb, s, BW))] * 3 + [_sds((nb, s, 128))]
                  + [_sds((1, 128)), _sds((1, 128))],
        scratch_shapes=[pltpu.VMEM((2, BW, BW), f32)],
        compiler_params=_cparams(("arbitrary", "arbitrary"), VMEM_BIG),
    )(cq, ck, cv, p3, cq, ck, cv, p3, alog, dtb, *consts, s_all, do, do)


def sg_fwd(p3, w, b, hm4):
    nb, s, _ = p3.shape

    def body(u_ref, v_ref, z_ref, w_ref, b_ref, hm_ref, y_ref):
        y_ref[0] = _sg_chunk(u_ref[0], v_ref[0], z_ref[0], w_ref[...], b_ref[...], hm_ref[...]).astype(bf16)

    def seg(k):
        return pl.BlockSpec((1, RC, BW), lambda bi, i: (bi, i, k))

    return pl.pallas_call(
        body, name="sg_fwd", grid=(nb, s // RC),
        in_specs=[seg(4), seg(5), seg(6), _full((NH, RC, RC)), _full((NH, RC)), _full((NH, BW))],
        out_specs=pl.BlockSpec((1, RC, BW), lambda bi, i: (bi, i, 0)),
        out_shape=_sds((nb, s, BW), bf16),
        compiler_params=_cparams(("arbitrary", "arbitrary")),
    )(p3, p3, p3, w, b, hm4)


def sg_bwd(p3, w, b, hm4, dy3):
    nb, s, _ = p3.shape

    def body(u_ref, v_ref, z_ref, w_ref, b_ref, hm_ref, dy_ref, du_ref, dv_ref, dz_ref, dw_ref, db_ref):
        @pl.when((pl.program_id(0) == 0) & (pl.program_id(1) == 0))
        def _():
            dw_ref[...] = jnp.zeros_like(dw_ref)
            db_ref[...] = jnp.zeros_like(db_ref)
        hm = hm_ref[...]
        _, vjp = jax.vjp(lambda u, v, z, w_, b_: _sg_chunk(u, v, z, w_, b_, hm),
                         u_ref[0], v_ref[0], z_ref[0], w_ref[...], b_ref[...])
        du, dv, dz, dw, db = vjp(dy_ref[0])
        du_ref[0] = du
        dv_ref[0] = dv
        dz_ref[0] = dz
        dw_ref[...] += dw
        db_ref[...] += db

    def seg(k):
        return pl.BlockSpec((1, RC, BW), lambda bi, i: (bi, i, k))

    blk = pl.BlockSpec((1, RC, BW), lambda bi, i: (bi, i, 0))
    return pl.pallas_call(
        body, name="sg_bwd", grid=(nb, s // RC),
        in_specs=[seg(4), seg(5), seg(6), _full((NH, RC, RC)), _full((NH, RC)), _full((NH, BW)), seg(1)],
        out_specs=[blk, blk, blk, _full((NH, RC, RC)), _full((NH, RC))],
        out_shape=[_sds((nb, s, BW))] * 3 + [_sds((NH, RC, RC)), _sds((NH, RC))],
        compiler_params=_cparams(("arbitrary", "arbitrary")),
    )(p3, p3, p3, w, b, hm4, dy3)


def _sc_fn(b, c, h, z, w, sd, su):
    return b * _conv3(c * h, w, sd, su) * _silu(z)


def sc_fwd(p3, w, t_ctx):
    nb, s, _ = p3.shape
    sd, su = _make_shifts(t_ctx, s)

    def body(b_ref, c_ref, h_ref, z_ref, w_ref, y_ref):
        y_ref[0] = _sc_fn(b_ref[0], c_ref[0], h_ref[0], z_ref[0], w_ref[...], sd, su).astype(bf16)

    def seg(k):
        return pl.BlockSpec((1, s, 128), lambda bi, j: (bi, 0, 2 * k + j))

    return pl.pallas_call(
        body, name="sc_fwd", grid=(nb, 2),
        in_specs=[seg(7), seg(8), seg(9), seg(10), pl.BlockSpec((3, 128), lambda bi, j: (0, j))],
        out_specs=pl.BlockSpec((1, s, 128), lambda bi, j: (bi, 0, j)),
        out_shape=_sds((nb, s, BW), bf16),
        compiler_params=_cparams(("arbitrary", "arbitrary"), VMEM_BIG),
    )(p3, p3, p3, p3, w)


def sc_bwd(p3, w, dy3, t_ctx):
    nb, s, _ = p3.shape
    sd, su = _make_shifts(t_ctx, s)

    def body(b_ref, c_ref, h_ref, z_ref, w_ref, dy_ref, db_ref, dc_ref, dh_ref, dz_ref, dw_ref):
        @pl.when(pl.program_id(1) == 0)
        def _():
            dw_ref[...] = jnp.zeros_like(dw_ref)
        _, vjp = jax.vjp(lambda b, c, h, z, w_: _sc_fn(b, c, h, z, w_, sd, su),
                         b_ref[0], c_ref[0], h_ref[0], z_ref[0], w_ref[...])
        db, dc, dh, dz, dw = vjp(dy_ref[0])
        db_ref[0] = db
        dc_ref[0] = dc
        dh_ref[0] = dh
        dz_ref[0] = dz
        dw_ref[...] += dw

    def seg(k):
        return pl.BlockSpec((1, s, 128), lambda j, bi: (bi, 0, 2 * k + j))

    blk = pl.BlockSpec((1, s, 128), lambda j, bi: (bi, 0, j))
    wspec = pl.BlockSpec((3, 128), lambda j, bi: (0, j))
    return pl.pallas_call(
        body, name="sc_bwd", grid=(2, nb),
        in_specs=[seg(7), seg(8), seg(9), seg(10), wspec, seg(2)],
        out_specs=[blk, blk, blk, blk, wspec],
        out_shape=[_sds((nb, s, BW))] * 4 + [_sds((3, BW))],
        compiler_params=_cparams(("arbitrary", "arbitrary"), VMEM_BIG),
    )(p3, p3, p3, p3, w, dy3)


def assemble_dp(pairs, singles_a, gdn_x, singles_b, gates):
    nb, s, _ = singles_a[0].shape
    flat = [a for pr in pairs for a in pr] + list(singles_a) + list(gdn_x) + list(singles_b) + list(gates)
    n_pairs, n_a, n_x, n_b = len(pairs), len(singles_a), len(gdn_x), len(singles_b)

    def body(*refs):
        out = refs[-1]
        ins = refs[:-1]
        col = 0
        for p in range(n_pairs):
            out[0, :, col:col + BW] = (ins[2 * p][0] + ins[2 * p + 1][0]).astype(bf16)
            col += BW
        k = 2 * n_pairs
        for _ in range(n_a + n_x + n_b):
            out[0, :, col:col + BW] = ins[k][0].astype(bf16)
            col += BW
            k += 1
        out[0, :, col:col + 128] = (ins[k][0] + ins[k + 1][0]).astype(bf16)
        out[0, :, col + 128:] = jnp.zeros((TM, PW - col - 128), bf16)

    def spec(a):
        return pl.BlockSpec((1, TM, a.shape[-1]), lambda b, j: (b, j, 0))

    return pl.pallas_call(
        body, name="assemble_dp", grid=(nb, s // TM),
        in_specs=[spec(a) for a in flat],
        out_specs=pl.BlockSpec((1, TM, PW), lambda b, j: (b, j, 0)),
        out_shape=_sds((nb, s, PW), bf16),
        compiler_params=_cparams(("arbitrary", "arbitrary")),
    )(*flat)


def mod_fwd(c_rows, w_mod, b_cols):
    nl, _, wc = w_mod.shape
    nr = c_rows.shape[0]

    def body(c_ref, w_ref, b_ref, o_ref):
        o_ref[0] = _dot(_silu(c_ref[...]), w_ref[0], precision=HI) + b_ref[0]

    return pl.pallas_call(
        body, name="mod_fwd", grid=(nl,),
        in_specs=[_full((nr, D)), pl.BlockSpec((1, D, wc), lambda l: (l, 0, 0)), pl.BlockSpec((1, 1, wc), lambda l: (l, 0, 0))],
        out_specs=pl.BlockSpec((1, nr, wc), lambda l: (l, 0, 0)),
        out_shape=_sds((nl, nr, wc)),
        compiler_params=_cparams(("arbitrary",)),
    )(c_rows, w_mod, b_cols)


def mod_bwd(c_rows, w_mod, dm_cols, dm_full):
    nl, _, wc = w_mod.shape
    nr = c_rows.shape[0]

    def body(c_ref, w_ref, dmc_ref, dmf_ref, gw_ref, gb_ref, dcc_ref):
        @pl.when(pl.program_id(0) == 0)
        def _():
            dcc_ref[...] = jnp.zeros_like(dcc_ref)
        a = _silu(c_ref[...])
        dmc = dmc_ref[0]
        gw_ref[0] = _dot_tn(a, dmc, precision=HI)
        gb_ref[0] = jnp.sum(dmf_ref[0], axis=0, keepdims=True)
        dcc_ref[...] += _dot_nt(dmc[nr - 8:nr], w_ref[0], precision=HI)

    return pl.pallas_call(
        body, name="mod_bwd", grid=(nl,),
        in_specs=[_full((nr, D)), pl.BlockSpec((1, D, wc), lambda l: (l, 0, 0)),
                  pl.BlockSpec((1, nr, wc), lambda l: (l, 0, 0)), pl.BlockSpec((1, nr, 3 * D), lambda l: (l, 0, 0))],
        out_specs=[pl.BlockSpec((1, D, wc), lambda l: (l, 0, 0)), pl.BlockSpec((1, 1, 3 * D), lambda l: (l, 0, 0)),
                   _full((8, D))],
        out_shape=[_sds((nl, D, wc)), _sds((nl, 1, 3 * D)), _sds((8, D))],
        compiler_params=_cparams(("arbitrary",)),
    )(c_rows, w_mod, dm_cols, dm_full)


def cctx_grad(parts, c_ctx):
    def body(p_ref, c_ref, o_ref):
        tot = p_ref[0, 0:1, :]
        for k in (2, 4, 6):
            tot = tot + p_ref[k, 0:1, :]
        c = c_ref[...]
        sg = jax.nn.sigmoid(c)
        o_ref[...] = tot * (sg * (1.0 + c * (1.0 - sg)))

    return pl.pallas_call(body, name="cctx_grad", out_shape=_sds((1, D)))(parts, c_ctx)


def sum_lead(x, out_dtype=f32, tr=256):
    k, r, c = x.shape
    tr = min(tr, r)
    assert r % tr == 0

    def body(x_ref, o_ref):
        tot = x_ref[0].astype(f32)
        for i in range(1, k):
            tot = tot + x_ref[i].astype(f32)
        o_ref[...] = tot.astype(out_dtype)

    return pl.pallas_call(
        body, name="sum_lead", grid=(r // tr,),
        in_specs=[pl.BlockSpec((k, tr, c), lambda i: (0, i, 0))],
        out_specs=pl.BlockSpec((tr, c), lambda i: (i, 0)),
        out_shape=_sds((r, c), out_dtype),
        compiler_params=_cparams(("arbitrary",)),
    )(x)


def adamw(w, m, v, g1, g2=None, tr=256):
    r, c = w.shape
    tr = min(tr, r)
    assert r % tr == 0
    two = g2 is not None
    c1 = 1.0 / (1.0 - ADAM_B1 ** ADAM_STEP)
    c2 = 1.0 / (1.0 - ADAM_B2 ** ADAM_STEP)

    def body(*refs):
        w_ref, m_ref, v_ref, g_ref = refs[:4]
        g = g_ref[...]
        if two:
            g = g + refs[4][...]
        go_ref, d_ref, mo_ref, vo_ref = refs[-4:]
        mn = ADAM_B1 * m_ref[...] + (1.0 - ADAM_B1) * g
        vn = ADAM_B2 * v_ref[...] + (1.0 - ADAM_B2) * (g * g)
        go_ref[...] = g
        mo_ref[...] = mn
        vo_ref[...] = vn
        d_ref[...] = -ADAM_LR * ((mn * c1) / (jnp.sqrt(vn * c2) + ADAM_EPS) + ADAM_WD * w_ref[...])

    blk = pl.BlockSpec((tr, c), lambda i: (i, 0))
    args = [w, m, v, g1] + ([g2] if two else [])
    return pl.pallas_call(
        body, name="adamw", grid=(r // tr,),
        in_specs=[blk] * len(args), out_specs=[blk] * 4, out_shape=[_sds((r, c))] * 4,
        compiler_params=_cparams(("arbitrary",)),
    )(*args)


def _my_pos():
    return lax.axis_index("x"), lax.axis_index("y"), lax.axis_index("c")


def gather8(x):
    shape = x.shape

    def body(x_ref, out_ref, send_sems, recv_sems, local_sem):
        mx, my, mc = _my_pos()
        me = 4 * mx + 2 * my + mc
        mine = pltpu.make_async_copy(x_ref, out_ref.at[me], local_sem)
        mine.start()
        copies = []
        for k in range(1, N_DEV):
            peer = (mx ^ (k >> 2), my ^ ((k >> 1) & 1), mc ^ (k & 1))
            cp = pltpu.make_async_remote_copy(src_ref=x_ref, dst_ref=out_ref.at[me], send_sem=send_sems.at[k - 1],
                                              recv_sem=recv_sems.at[k - 1], device_id=peer, device_id_type=MESH)
            cp.start()
            copies.append(cp)
        for k in range(1, N_DEV):
            src = me ^ k
            pltpu.make_async_remote_copy(src_ref=x_ref, dst_ref=out_ref.at[src], send_sem=send_sems.at[k - 1],
                                         recv_sem=recv_sems.at[k - 1], device_id=(mx, my, mc),
                                         device_id_type=MESH).wait_recv()
        for cp in copies:
            cp.wait_send()
        mine.wait()

    return pl.pallas_call(
        body, name="gather8", out_shape=_sds((N_DEV,) + shape, x.dtype),
        in_specs=[pl.BlockSpec(memory_space=pl.ANY)], out_specs=pl.BlockSpec(memory_space=pl.ANY),
        scratch_shapes=[pltpu.SemaphoreType.DMA((N_DEV - 1,)), pltpu.SemaphoreType.DMA((N_DEV - 1,)),
                        pltpu.SemaphoreType.DMA(())],
    )(x)


def gather4(x):
    shape = x.shape

    def body(x_ref, out_ref, send_sems, recv_sems, local_sem):
        mx, my, mc = _my_pos()
        me = 2 * mx + my
        mine = pltpu.make_async_copy(x_ref, out_ref.at[me], local_sem)
        mine.start()
        copies = []
        for k in range(1, N_CHIPS):
            peer = (mx ^ (k >> 1), my ^ (k & 1), mc)
            cp = pltpu.make_async_remote_copy(src_ref=x_ref, dst_ref=out_ref.at[me], send_sem=send_sems.at[k - 1],
                                              recv_sem=recv_sems.at[k - 1], device_id=peer, device_id_type=MESH)
            cp.start()
            copies.append(cp)
        for k in range(1, N_CHIPS):
            src = me ^ k
            pltpu.make_async_remote_copy(src_ref=x_ref, dst_ref=out_ref.at[src], send_sem=send_sems.at[k - 1],
                                         recv_sem=recv_sems.at[k - 1], device_id=(mx, my, mc),
                                         device_id_type=MESH).wait_recv()
        for cp in copies:
            cp.wait_send()
        mine.wait()

    return pl.pallas_call(
        body, name="gather4", out_shape=_sds((N_CHIPS,) + shape, x.dtype),
        in_specs=[pl.BlockSpec(memory_space=pl.ANY)], out_specs=pl.BlockSpec(memory_space=pl.ANY),
        scratch_shapes=[pltpu.SemaphoreType.DMA((N_CHIPS - 1,)), pltpu.SemaphoreType.DMA((N_CHIPS - 1,)),
                        pltpu.SemaphoreType.DMA(())],
    )(x)


def scatter4(g):
    shape = g.shape[1:]

    def body(g_ref, out_ref, send_sems, recv_sems, local_sem):
        mx, my, mc = _my_pos()
        me = 2 * mx + my
        mine = pltpu.make_async_copy(g_ref.at[me], out_ref.at[me], local_sem)
        mine.start()
        copies = []
        for k in range(1, N_CHIPS):
            peer = (mx ^ (k >> 1), my ^ (k & 1), mc)
            cp = pltpu.make_async_remote_copy(src_ref=g_ref.at[me ^ k], dst_ref=out_ref.at[me], send_sem=send_sems.at[k - 1],
                                              recv_sem=recv_sems.at[k - 1], device_id=peer, device_id_type=MESH)
            cp.start()
            copies.append(cp)
        for k in range(1, N_CHIPS):
            src = me ^ k
            pltpu.make_async_remote_copy(src_ref=g_ref.at[src], dst_ref=out_ref.at[src], send_sem=send_sems.at[k - 1],
                                         recv_sem=recv_sems.at[k - 1], device_id=(mx, my, mc),
                                         device_id_type=MESH).wait_recv()
        for cp in copies:
            cp.wait_send()
        mine.wait()

    return pl.pallas_call(
        body, name="scatter4", out_shape=_sds((N_CHIPS,) + shape, g.dtype),
        in_specs=[pl.BlockSpec(memory_space=pl.ANY)], out_specs=pl.BlockSpec(memory_space=pl.ANY),
        scratch_shapes=[pltpu.SemaphoreType.DMA((N_CHIPS - 1,)), pltpu.SemaphoreType.DMA((N_CHIPS - 1,)),
                        pltpu.SemaphoreType.DMA(())],
    )(g)


def swap_sibling(x):
    def body(x_ref, out_ref, send_sem, recv_sem):
        mx, my, mc = _my_pos()
        cp = pltpu.make_async_remote_copy(src_ref=x_ref, dst_ref=out_ref, send_sem=send_sem, recv_sem=recv_sem,
                                          device_id=(mx, my, 1 - mc), device_id_type=MESH)
        cp.start()
        cp.wait()

    return pl.pallas_call(
        body, name="swap_sibling", out_shape=_sds(x.shape, x.dtype),
        in_specs=[pl.BlockSpec(memory_space=pl.ANY)], out_specs=pl.BlockSpec(memory_space=pl.ANY),
        scratch_shapes=[pltpu.SemaphoreType.DMA(()), pltpu.SemaphoreType.DMA(())],
    )(x)


PACK_ROWS = 64
SMALL = ("c_ctx", "b_mod", "g_pre", "g_post", "ret_norm_g", "sg_w", "sg_b", "sc_conv_w", "gdn_conv_w",
         "gdn_a_log", "gdn_dt_bias", "gdn_norm_g")


def _pack(arrs, width=D):
    rows = []
    for a in arrs:
        flat = a.reshape(-1)
        pad = (-flat.shape[0]) % width
        rows.append(jnp.pad(flat, (0, pad)).reshape(-1, width))
    out = jnp.concatenate(rows, axis=0)
    return jnp.pad(out, ((0, (-out.shape[0]) % PACK_ROWS), (0, 0)))


def _unpack(packed, shapes, width=D):
    outs, r = [], 0
    for shp in shapes:
        size = int(np.prod(shp))
        nr = -(-size // width)
        outs.append(packed[r:r + nr].reshape(-1)[:size].reshape(shp))
        r += nr
    return outs


def kernel(x, c, ctx, c_ctx, w_mod, b_mod, g_pre, g_post, w_in, w_out, ret_norm_g, sg_w, sg_b, sc_conv_w, gdn_conv_w, gdn_a_log, gdn_dt_bias, gdn_norm_g, loss_target, m_c_ctx, m_w_mod, m_b_mod, m_g_pre, m_g_post, m_w_in, m_w_out, m_ret_norm_g, m_sg_w, m_sg_b, m_sc_conv_w, m_gdn_conv_w, m_gdn_a_log, m_gdn_dt_bias, m_gdn_norm_g, v_c_ctx, v_w_mod, v_b_mod, v_g_pre, v_g_post, v_w_in, v_w_out, v_ret_norm_g, v_sg_w, v_sg_b, v_sc_conv_w, v_gdn_conv_w, v_gdn_a_log, v_gdn_dt_bias, v_gdn_norm_g):
    weights = dict(c_ctx=c_ctx, w_mod=w_mod, b_mod=b_mod, g_pre=g_pre, g_post=g_post, w_in=w_in, w_out=w_out,
                   ret_norm_g=ret_norm_g, sg_w=sg_w, sg_b=sg_b, sc_conv_w=sc_conv_w, gdn_conv_w=gdn_conv_w,
                   gdn_a_log=gdn_a_log, gdn_dt_bias=gdn_dt_bias, gdn_norm_g=gdn_norm_g)
    mom = dict(c_ctx=m_c_ctx, w_mod=m_w_mod, b_mod=m_b_mod, g_pre=m_g_pre, g_post=m_g_post, w_in=m_w_in,
               w_out=m_w_out, ret_norm_g=m_ret_norm_g, sg_w=m_sg_w, sg_b=m_sg_b, sc_conv_w=m_sc_conv_w,
               gdn_conv_w=m_gdn_conv_w, gdn_a_log=m_gdn_a_log, gdn_dt_bias=m_gdn_dt_bias, gdn_norm_g=m_gdn_norm_g)
    var = dict(c_ctx=v_c_ctx, w_mod=v_w_mod, b_mod=v_b_mod, g_pre=v_g_pre, g_post=v_g_post, w_in=v_w_in,
               w_out=v_w_out, ret_norm_g=v_ret_norm_g, sg_w=v_sg_w, sg_b=v_sg_b, sc_conv_w=v_sc_conv_w,
               gdn_conv_w=v_gdn_conv_w, gdn_a_log=v_gdn_a_log, gdn_dt_bias=v_gdn_dt_bias, gdn_norm_g=v_gdn_norm_g)

    nb, t_lat, _ = x.shape
    t_ctx = ctx.shape[1]
    s = t_ctx + t_lat
    n = nb * s
    sb = s // TM
    nl = w_in.shape[0]
    wc_in = w_in.shape[2]
    wc_mod = w_mod.shape[2]
    rows_out = w_out.shape[1]
    n_all = nb * N_DEV
    mx, my, mc = _my_pos()
    chip = 2 * mx + my
    dev = 2 * chip + mc

    hm = jnp.asarray(_head_masks())
    hm4 = hm[:, 0, :]
    bd = jnp.asarray(_block_diag())
    ret_c = _ret_consts() + [bd, hm]
    gdn_c = _gdn_consts() + [bd]
    cos, sins = _rope_tables(t_lat, t_ctx)

    pre = _pack([c, sc_conv_w, gdn_conv_w])
    pre_all = gather8(pre)
    c_parts, scw_parts, gcw_parts = [], [], []
    for k in range(N_DEV):
        ck, sk, gk = _unpack(pre_all[k], [c.shape, sc_conv_w.shape, gdn_conv_w.shape])
        c_parts.append(ck)
        if k % 2 == 0:
            scw_parts.append(sk)
            gcw_parts.append(gk)
    c_all = jnp.concatenate(c_parts, axis=0)
    sc_w_full = jnp.concatenate(scw_parts, axis=-1)
    gdn_w_full = jnp.concatenate(gcw_parts, axis=-1)
    c_rows = jnp.concatenate([c_all, c_ctx[None, :], jnp.zeros((7, D), f32)], axis=0)

    b_cols = lax.dynamic_slice_in_dim(b_mod, chip * wc_mod, wc_mod, axis=1)[:, None, :]
    mod_part = mod_fwd(c_rows, w_mod, b_cols)
    mod_all = gather8(mod_part)
    mod = jnp.concatenate([mod_all[2 * k] for k in range(N_CHIPS)], axis=-1)
    my_rows = jnp.concatenate([lax.dynamic_slice_in_dim(mod, dev * nb, nb, axis=1), mod[:, n_all:n_all + 1]], axis=1)
    shift_t = my_rows[:, :, None, 0:D]
    scale_t = my_rows[:, :, None, D:2 * D]
    gate_t = my_rows[:, :, None, 2 * D:3 * D]

    w_in_all = gather4(w_in.astype(bf16))
    w_in_full = jnp.concatenate([w_in_all[k] for k in range(N_CHIPS)], axis=-1)
    w_in_full = jnp.pad(w_in_full, ((0, 0), (0, 0), (0, PW - IN_W)))
    w_in_t = jnp.transpose(w_in_full, (0, 2, 1))
    w_out_all = gather4(w_out.astype(bf16))
    w_out_full = jnp.concatenate([w_out_all[k] for k in range(N_CHIPS)], axis=1)
    w_out_t = jnp.transpose(w_out_full, (0, 2, 1))

    alog = jnp.pad(gdn_a_log.reshape(nl, 1, 8), ((0, 0), (0, 0), (0, 120)))
    dtb = jnp.pad(gdn_dt_bias.reshape(nl, 1, 8), ((0, 0), (0, 0), (0, 120)))
    gdn_ng = jnp.tile(gdn_norm_g, (1, NH))[:, None, :]
    ret_ng = ret_norm_g[:, None, :]

    xs = jnp.concatenate([ctx, x], axis=1).reshape(n, D)
    saved = []
    for l in range(nl):
        p, h = inproj_fwd(xs, shift_t[l], scale_t[l], g_pre[l][None, :], w_in_full[l], nb, sb)
        p3 = p.reshape(nb, s, PW)
        ro_f, ro_b, rs_all = ret_scan_fwd(p3, cos, sins, ret_c, t_ctx)
        y_ret = mix_finish_fwd(_ret_finish, "ret_finish_fwd", ro_f, ro_b, p3, 3, ret_ng[l], bd)
        y_sg = sg_fwd(p3, sg_w[l], sg_b[l], hm4)
        y_sc = sc_fwd(p3, sc_w_full[l], t_ctx)
        cq, ck, cv = [gdn_conv_fwd(p3, gdn_w_full[l][:, BW * i:BW * (i + 1)], 11 + i, t_ctx) for i in range(3)]
        go_f, go_b, gs_all = gdn_scan_fwd(cq, ck, cv, p3, alog[l], dtb[l], gdn_c, t_ctx)
        y_gdn = mix_finish_fwd(_gdn_finish, "gdn_finish_fwd", go_f, go_b, p3, 14, gdn_ng[l], bd)
        ys = [a.reshape(n, BW) for a in (y_ret, y_sg, y_sc, y_gdn)]
        x_new, o = outproj_fwd(ys, w_out_full[l], xs, gate_t[l], g_post[l][None, :], nb, sb)
        saved.append(dict(x=xs, h=h, p3=p3, ro=(ro_f, ro_b), rs=rs_all, c=(cq, ck, cv), go=(go_f, go_b), gs=gs_all,
                          ys=ys, o=o))
        xs = x_new

    dx3, loss_part = loss_head(xs.reshape(nb, s, D), loss_target, t_ctx)
    loss = lax.psum(loss_part[0, 0], ("x", "y", "c"))

    dxs = dx3.reshape(n, D)
    g_small = {k: [None] * nl for k in SMALL if k not in ("c_ctx", "b_mod")}
    dm_rows = [None] * nl
    gw_in = [None] * nl
    gw_out = [None] * nl
    for l in reversed(range(nl)):
        sv = saved[l]
        p3 = sv["p3"]
        dy, gw_out[l], dg_post, dgate = outproj_bwd(dxs, sv["o"], gate_t[l], g_post[l][None, :], sv["ys"], w_out_t[l], nb, sb)
        dy3 = dy.reshape(nb, s, D)
        r_do, r_dz, d_rng = mix_finish_bwd(_ret_finish, "ret_finish_bwd", *sv["ro"], p3, 3, ret_ng[l], bd, dy3, 0)
        r_d = ret_scan_bwd(p3, cos, sins, ret_c, sv["rs"], r_do, t_ctx)
        s_du, s_dv, s_dz, d_sgw, d_sgb = sg_bwd(p3, sg_w[l], sg_b[l], hm4, dy3)
        c_db, c_dc, c_dh, c_dz, d_scw = sc_bwd(p3, sc_w_full[l], dy3, t_ctx)
        g_do, g_dz, d_gng = mix_finish_bwd(_gdn_finish, "gdn_finish_bwd", *sv["go"], p3, 14, gdn_ng[l], bd, dy3, 3)
        g_d = gdn_scan_bwd(*sv["c"], p3, alog[l], dtb[l], gdn_c, sv["gs"], g_do, t_ctx)
        gx, d_gcw = [], []
        for i in range(3):
            dxi, dwi = gdn_conv_bwd(p3, gdn_w_full[l][:, BW * i:BW * (i + 1)], 11 + i, g_d[i], g_d[4 + i], t_ctx)
            gx.append(dxi)
            d_gcw.append(dwi)
        dp3 = assemble_dp([(r_d[0], r_d[3]), (r_d[1], r_d[4]), (r_d[2], r_d[5])],
                          [r_dz, s_du, s_dv, s_dz, c_db, c_dc, c_dh, c_dz], gx, [g_dz], [g_d[3], g_d[7]])
        dp = dp3.reshape(n, PW)
        dxs, dg_pre, dshift, dscale = inproj_bwd_x(dp, w_in_t[l], sv["x"], shift_t[l], scale_t[l], g_pre[l][None, :], dxs, nb, sb)
        gw_in[l] = dw_in(sv["h"], dp)
        g_small["g_pre"][l] = dg_pre[0]
        g_small["g_post"][l] = dg_post[0]
        g_small["ret_norm_g"][l] = d_rng[0]
        g_small["sg_w"][l] = d_sgw
        g_small["sg_b"][l] = d_sgb
        g_small["sc_conv_w"][l] = d_scw
        g_small["gdn_conv_w"][l] = jnp.concatenate(d_gcw, axis=-1)
        g_small["gdn_a_log"][l] = g_d[8][0, :8].reshape(2, NH)
        g_small["gdn_dt_bias"][l] = g_d[9][0, :8].reshape(2, NH)
        g_small["gdn_norm_g"][l] = d_gng[0].reshape(NH, HD)
        dm_rows[l] = jnp.concatenate([dshift, dscale, dgate], axis=-1)[:nb + 1]
    grad_x = dxs.reshape(nb, s, D)[:, t_ctx:, :]

    g_small = {k: jnp.stack(v) for k, v in g_small.items()}
    dm_rows = jnp.stack(dm_rows)
    dm_slot = jnp.zeros((nl, n_all + 8, 3 * D), f32)
    dm_slot = lax.dynamic_update_slice_in_dim(dm_slot, dm_rows[:, :nb], dev * nb, axis=1)
    dm_slot = lax.dynamic_update_slice_in_dim(dm_slot, dm_rows[:, nb:], n_all, axis=1)
    names2 = [k for k in SMALL if k not in ("c_ctx", "b_mod")]
    pack2 = _pack([g_small[k] for k in names2] + [dm_slot])
    tot2 = sum_lead(gather8(pack2), tr=PACK_ROWS)
    outs2 = _unpack(tot2, [g_small[k].shape for k in names2] + [dm_slot.shape])
    grads = dict(zip(names2, outs2[:-1]))
    dm_all = outs2[-1]
    grads["gdn_norm_g"] = sum_lead(jnp.transpose(grads["gdn_norm_g"], (1, 0, 2)), tr=nl)
    for k in ("sc_conv_w", "gdn_conv_w"):
        wc = weights[k].shape[2]
        grads[k] = lax.dynamic_slice_in_dim(grads[k], chip * wc, wc, axis=2)

    dm_cols = lax.dynamic_slice_in_dim(dm_all, chip * wc_mod, wc_mod, axis=2)
    g_w_mod, g_b_mod, dcc_part = mod_bwd(c_rows, w_mod, dm_cols, dm_all)
    grads["b_mod"] = g_b_mod[:, 0, :]
    grads["c_ctx"] = cctx_grad(gather8(dcc_part), c_ctx[None, :])[0]

    gw_in = jnp.stack(gw_in)[:, :, :IN_W].reshape(nl, D, N_CHIPS, wc_in)
    gw_in = jnp.transpose(gw_in, (2, 0, 1, 3)).astype(bf16).reshape(N_CHIPS, nl * D, wc_in)
    gin_mine = sum_lead(scatter4(gw_in))
    gin_sib = swap_sibling(gin_mine)
    gw_out = jnp.stack(gw_out).reshape(nl, N_CHIPS, rows_out, D)
    gw_out = jnp.transpose(gw_out, (1, 0, 2, 3)).astype(bf16).reshape(N_CHIPS, nl * rows_out, D)
    gout_mine = sum_lead(scatter4(gw_out))
    gout_sib = swap_sibling(gout_mine)

    res = {}
    res["w_in"] = [a.reshape(w_in.shape) for a in adamw(w_in.reshape(nl * D, wc_in), m_w_in.reshape(nl * D, wc_in),
                                                          v_w_in.reshape(nl * D, wc_in), gin_mine, gin_sib)]
    res["w_out"] = [a.reshape(w_out.shape) for a in adamw(w_out.reshape(nl * rows_out, D), m_w_out.reshape(nl * rows_out, D),
                                                            v_w_out.reshape(nl * rows_out, D), gout_mine, gout_sib)]
    res["w_mod"] = [a.reshape(w_mod.shape) for a in adamw(w_mod.reshape(nl * D, wc_mod), m_w_mod.reshape(nl * D, wc_mod),
                                                            v_w_mod.reshape(nl * D, wc_mod), g_w_mod.reshape(nl * D, wc_mod))]
    shapes = [weights[k].shape for k in SMALL]
    small = adamw(_pack([weights[k] for k in SMALL]), _pack([mom[k] for k in SMALL]), _pack([var[k] for k in SMALL]),
                  _pack([grads[k].reshape(weights[k].shape) for k in SMALL]), tr=PACK_ROWS)
    small = [_unpack(a, shapes) for a in small]
    for i, k in enumerate(SMALL):
        res[k] = [small[j][i] for j in range(4)]

    order = ["c_ctx", "w_mod", "b_mod", "g_pre", "g_post", "w_in", "w_out", "ret_norm_g", "sg_w", "sg_b", "sc_conv_w",
             "gdn_conv_w", "gdn_a_log", "gdn_dt_bias", "gdn_norm_g"]
    return (loss, grad_x, *[res[k][0] for k in order], *[res[k][1] for k in order], *[res[k][2] for k in order],
            *[res[k][3] for k in order])
```

```python
import functools

import jax
import jax.numpy as jnp
import numpy as np
from jax import lax
from jax.experimental import pallas as pl
from jax.experimental.pallas import tpu as pltpu

f32 = jnp.float32
bf16 = jnp.bfloat16
HI = lax.Precision.HIGHEST
P3 = lax.Precision.HIGH
MESH = pl.DeviceIdType.MESH

EPS = 1e-6
D = 1024
NH = 4
HD = 64
BW = NH * HD
PAIR_W = 2 * HD
RC = 128
GC = 64
GRID_W = 64
ROPE_BASE = 10000.0
IN_W = 15 * BW + 16
PW = 4096
GATE_COL = 15 * BW
N_CHIPS = 4
N_DEV = 8
TM = 256
ADAM_LR, ADAM_B1, ADAM_B2, ADAM_EPS, ADAM_WD, ADAM_STEP = 0.001, 0.9, 0.999, 1e-08, 0.01, 10
LANE_HEAD = np.arange(BW) // HD
VMEM_BIG = 56 * 1024 * 1024


def _dot(a, b, precision=None):
    return jnp.dot(a, b, precision=precision, preferred_element_type=f32)


def _dot_nt(a, b, precision=None):
    return lax.dot_general(a, b, (((1,), (1,)), ((), ())), precision=precision, preferred_element_type=f32)


def _dot_tn(a, b, precision=None):
    return lax.dot_general(a, b, (((0,), (0,)), ((), ())), precision=precision, preferred_element_type=f32)


def _sds(shape, dtype=f32):
    return jax.ShapeDtypeStruct(shape, dtype)


def _cparams(sem=None, vmem=None):
    kw = {}
    if sem is not None:
        kw["dimension_semantics"] = sem
    if vmem is not None:
        kw["vmem_limit_bytes"] = vmem
    return pltpu.CompilerParams(**kw)


def _full(shape):
    n = len(shape)
    return pl.BlockSpec(shape, lambda *_: (0,) * n)


def _head_masks():
    return np.stack([(LANE_HEAD == h).astype(np.float32)[None, :] for h in range(NH)])


def _block_diag():
    return (LANE_HEAD[:, None] == LANE_HEAD[None, :]).astype(np.float32)


def _tau(c, d):
    return np.arange(c) if d == 0 else c - 1 - np.arange(c)


def _ret_consts():
    lg = np.log(1.0 - 2.0 ** (-5.0 - np.arange(NH)))
    intra = np.zeros((2, NH, RC, RC)); qdec = np.zeros((2, RC, BW)); kdec = np.zeros((2, RC, BW))
    for d in range(2):
        t = _tau(RC, d)
        diff = t[:, None] - t[None, :]
        for h in range(NH):
            intra[d, h] = np.where(diff >= 0, np.exp(np.maximum(diff, 0) * lg[h]), 0.0)
        qdec[d] = np.exp((t[:, None] + 1.0) * lg[LANE_HEAD][None, :])
        kdec[d] = np.exp((RC - 1.0 - t[:, None]) * lg[LANE_HEAD][None, :])
    cd = np.exp(RC * lg[LANE_HEAD])[:, None] * np.ones((1, BW))
    return [jnp.asarray(a, f32) for a in (intra, qdec, kdec, cd)]


def _rope_tables(t_lat, t_ctx):
    nf = HD // 4
    inv = ROPE_BASE ** (-np.arange(nf) / nf)
    pos = np.arange(t_lat)
    ang_r = (pos // GRID_W)[:, None] * inv[None, :]
    ang_c = (pos % GRID_W)[:, None] * inv[None, :]
    ang = np.concatenate([ang_r, ang_r, ang_c, ang_c], axis=1)
    sign = np.concatenate([-np.ones(nf), np.ones(nf), -np.ones(nf), np.ones(nf)])
    cos = np.tile(np.cos(ang), (1, NH)); sins = np.tile(np.sin(ang) * sign, (1, NH))
    cos = np.concatenate([np.ones((t_ctx, BW)), cos]); sins = np.concatenate([np.zeros((t_ctx, BW)), sins])
    return jnp.asarray(cos, f32), jnp.asarray(sins, f32)


def _gdn_consts(nb):
    tmask = np.zeros((2, 2, GC, GC)); tmask2 = np.zeros((2, 2, GC, PAIR_W)); strict2 = np.zeros((2, 2, GC, PAIR_W))
    exp_g = np.zeros((2, 2, 128, PAIR_W)); exp_b = np.zeros((2, 2, 128, PAIR_W))
    for d in range(2):
        t = _tau(GC, d)
        tmask[d, :] = (t[:, None] >= t[None, :])
        tmask2[d, :] = np.tile(t[:, None] >= t[None, :], (1, 2))
        strict2[d, :] = np.tile(t[:, None] > t[None, :], (1, 2))
        for h in range(NH):
            exp_g[d, h // 2, 4 * d + h, (h % 2) * HD:(h % 2 + 1) * HD] = 1.0
            exp_b[d, h // 2, 8 + 4 * d + h, (h % 2) * HD:(h % 2 + 1) * HD] = 1.0
    exp_gt = np.transpose(exp_g, (0, 1, 3, 2))
    per_z = [np.tile(a.reshape((4,) + a.shape[2:]), (nb, 1, 1)) for a in (tmask, tmask2, strict2, exp_g, exp_b, exp_gt)]
    dsel2 = np.tile(np.eye(GC), (1, 2))
    eye2 = np.tile(np.eye(GC), (1, 2))
    bd2 = (np.arange(PAIR_W)[:, None] // HD == np.arange(PAIR_W)[None, :] // HD)
    return [jnp.asarray(a, f32) for a in per_z + [dsel2, eye2, bd2]]


def _swap16(x):
    lane = lax.broadcasted_iota(jnp.int32, x.shape, x.ndim - 1)
    n = x.shape[-1]
    return jnp.where(lane % 32 < 16, pltpu.roll(x, n - 16, axis=x.ndim - 1), pltpu.roll(x, 16, axis=x.ndim - 1))


@jax.custom_vjp
def _rot(x, cos, sins):
    return x * cos + _swap16(x) * sins


def _rot_fwd(x, cos, sins):
    return _rot(x, cos, sins), (cos, sins)


def _rot_bwd(res, g):
    cos, sins = res
    return g * cos + _swap16(g * sins), jnp.zeros_like(cos), jnp.zeros_like(sins)


_rot.defvjp(_rot_fwd, _rot_bwd)


def _silu(z):
    return z * jax.nn.sigmoid(z)


def _head_sum(x, bd):
    return _dot(x, bd, precision=HI)


def _ret_step(s, q, k, v, cos, sins, intra, qdec, kdec, cd, bd, hm):
    qr = _rot(q, cos, sins)
    kr = _rot(k, cos, sins) * (HD ** -0.5)
    o = _dot(qr * qdec, s)
    for h in range(NH):
        sc = _dot_nt(qr * hm[h], kr) * intra[h]
        o = o + _dot(sc, v) * hm[h]
    s_new = s * cd + bd * _dot_tn(kr * kdec, v)
    return s_new, o


def _ret_finish(o_f, o_b, z, norm_g, bd):
    o = o_f + o_b
    mu = _head_sum(o, bd) * (1.0 / HD)
    xc = o - mu
    var = _head_sum(xc * xc, bd) * (1.0 / HD)
    return xc * lax.rsqrt(var + EPS) * norm_g * _silu(z)


def _softplus(x):
    return jnp.maximum(x, 0.0) + jnp.log(1.0 + jnp.exp(-jnp.abs(x)))


def _bmm(a, b, precision=None):
    return lax.dot_general(a, b, (((2,), (1,)), ((0,), (0,))), precision=precision, preferred_element_type=f32)


def _bmm_nt(a, b, precision=None):
    return lax.dot_general(a, b, (((2,), (2,)), ((0,), (0,))), precision=precision, preferred_element_type=f32)


def _bmm_tn(a, b, precision=None):
    return lax.dot_general(a, b, (((1,), (1,)), ((0,), (0,))), precision=precision, preferred_element_type=f32)


def _gdn_step(s, q, k, v, gate, alog, dtb, tmask, tmask2, strict2, exp_g, exp_b, exp_gt, dsel2, eye2, bd2):
    z, c, w_ = q.shape
    ne = gate.shape[0]

    def per_pair(a):
        return jnp.broadcast_to(a[:, None], (ne, z // ne) + a.shape[1:]).reshape((z,) + a.shape[1:])

    def rows(a):
        return a.reshape(z * c, w_)

    def bdiag(x):
        return jnp.concatenate([x, x], axis=1) * bd2

    g = per_pair(-jnp.exp(alog) * _softplus(gate + dtb))
    beta = per_pair(jax.nn.sigmoid(gate))
    gl = _bmm(g, exp_g, P3)
    gc_l = _bmm(tmask, gl, P3)
    glast_l = jnp.sum(gl, axis=1, keepdims=True)
    glast = jnp.sum(g, axis=1, keepdims=True)
    beta_l = _bmm(beta, exp_b, P3)
    gc_r = jnp.sum(gc_l * dsel2, axis=1, keepdims=True)
    qn = q * lax.rsqrt(_dot(rows(q * q), bd2, P3).reshape(z, c, w_) + EPS)
    kn = k * lax.rsqrt(_dot(rows(k * k), bd2, P3).reshape(z, c, w_) + EPS)
    eg = jnp.exp(gc_l)
    kb = kn * beta_l
    vb = v * beta_l
    kbg = kb * eg
    qs = qn * (HD ** -0.5)
    dec = jnp.exp(jnp.where(tmask2 > 0, gc_l - gc_r, -1e30))
    kns = bdiag(kn)
    m = -(_bmm_nt(kb, kns) * dec * strict2)
    inv = eye2 + m
    p = m
    for _ in range(5):
        p = _bmm(p, bdiag(p), P3)
        inv = inv + _bmm(inv, bdiag(p), P3)
    u = _bmm(inv, bdiag(vb), P3)
    w = _bmm(inv, bdiag(kbg), P3)
    v_new = u - _bmm(w, s)
    k_tail = kn * jnp.exp(glast_l - gc_l)
    cdec = jnp.sum(exp_gt * jnp.exp(glast), axis=-1, keepdims=True)
    s_new = s * cdec + bd2 * _bmm_tn(k_tail, v_new)
    a = _bmm_nt(qs, kns) * dec
    o = _bmm(qs * eg, s) + _bmm(a, bdiag(v_new))
    return s_new, o


def _gdn_finish(o_f, o_b, z, norm_g, bd):
    o = o_f + o_b
    ms = _head_sum(o * o, bd) * (1.0 / HD)
    return o * lax.rsqrt(ms + EPS) * norm_g * _silu(z)


def _gelu(x):
    return 0.5 * x * (1.0 + jnp.tanh(0.7978845608028654 * (x + 0.044715 * (x * x * x))))


def _sg_chunk(u, v, z, w, b, hm4):
    u = _gelu(u)
    gv = _gelu(v)
    mu = jnp.mean(gv, axis=-1, keepdims=True)
    xc = gv - mu
    var = jnp.mean(xc * xc, axis=-1, keepdims=True)
    vn = xc * lax.rsqrt(var + EPS)
    s = _dot_tn(b, hm4, precision=HI)
    for h in range(NH):
        s = s + _dot(w[h], vn) * hm4[h:h + 1]
    return u * s * _silu(z)


def _make_shifts(t_ctx, n):
    def dn(x):
        t = lax.broadcasted_iota(jnp.int32, x.shape, 0)
        return jnp.where((t != 0) & (t != t_ctx), pltpu.roll(x, 1, axis=0), 0.0)

    def up(x):
        t = lax.broadcasted_iota(jnp.int32, x.shape, 0)
        return jnp.where((t != t_ctx - 1) & (t != n - 1), pltpu.roll(x, n - 1, axis=0), 0.0)

    @jax.custom_vjp
    def shift_dn(x):
        return dn(x)
    shift_dn.defvjp(lambda x: (dn(x), None), lambda _, g: (up(g),))

    @jax.custom_vjp
    def shift_up(x):
        return up(x)
    shift_up.defvjp(lambda x: (up(x), None), lambda _, g: (dn(g),))
    return shift_dn, shift_up


def _conv3(x, w, shift_dn, shift_up):
    return shift_dn(x) * w[0:1] + x * w[1:2] + shift_up(x) * w[2:3]


def inproj_fwd(x, shift_t, scale_t, g_pre, w_in, n_batch, sb):
    n = x.shape[0]

    def sel(i):
        return jnp.where(i % sb == 0, n_batch, i // sb)

    def body(x_ref, sh_ref, sc_ref, g_ref, w_ref, p_ref, h_ref):
        xv = x_ref[...]
        r = xv * lax.rsqrt(jnp.mean(xv * xv, axis=-1, keepdims=True) + EPS)
        h = (r * g_ref[...]) * (1.0 + sc_ref[0]) + sh_ref[0]
        hb = h.astype(bf16)
        h_ref[...] = hb
        p_ref[...] = _dot(hb, w_ref[...])

    return pl.pallas_call(
        body, name="inproj_fwd", grid=(n // TM,),
        in_specs=[pl.BlockSpec((TM, D), lambda i: (i, 0)),
                  pl.BlockSpec((1, 1, D), lambda i: (sel(i), 0, 0)),
                  pl.BlockSpec((1, 1, D), lambda i: (sel(i), 0, 0)),
                  _full((1, D)), _full((D, PW))],
        out_specs=[pl.BlockSpec((TM, PW), lambda i: (i, 0)), pl.BlockSpec((TM, D), lambda i: (i, 0))],
        out_shape=[_sds((n, PW)), _sds((n, D), bf16)],
        compiler_params=_cparams(("arbitrary",), VMEM_BIG),
    )(x, shift_t, scale_t, g_pre, w_in)


def outproj_fwd(ys, w_out, x, gate_t, g_post, n_batch, sb):
    n = x.shape[0]

    def sel(i):
        return jnp.where(i % sb == 0, n_batch, i // sb)

    def body(y0, y1, y2, y3, w_ref, x_ref, gt_ref, g_ref, xn_ref, o_ref):
        y = jnp.concatenate([y0[...], y1[...], y2[...], y3[...]], axis=1)
        o = _dot(y, w_ref[...])
        o_ref[...] = o
        nrm = o * lax.rsqrt(jnp.mean(o * o, axis=-1, keepdims=True) + EPS) * g_ref[...]
        xn_ref[...] = x_ref[...] + gt_ref[0] * nrm

    yspec = pl.BlockSpec((TM, BW), lambda i: (i, 0))
    return pl.pallas_call(
        body, name="outproj_fwd", grid=(n // TM,),
        in_specs=[yspec, yspec, yspec, yspec, _full((D, D)), pl.BlockSpec((TM, D), lambda i: (i, 0)),
                  pl.BlockSpec((1, 1, D), lambda i: (sel(i), 0, 0)), _full((1, D))],
        out_specs=[pl.BlockSpec((TM, D), lambda i: (i, 0)), pl.BlockSpec((TM, D), lambda i: (i, 0))],
        out_shape=[_sds((n, D)), _sds((n, D))],
        compiler_params=_cparams(("arbitrary",)),
    )(*ys, w_out, x, gate_t, g_post)


def _row_onehot(r):
    return lax.broadcasted_iota(jnp.int32, (8, 1), 0) == r


def outproj_bwd(dxn, o, gate_t, g_post, ys, w_out_t, n_batch, sb):
    n = dxn.shape[0]

    def sel(i):
        return jnp.where(i % sb == 0, n_batch, i // sb)

    def body(dxn_ref, o_ref, gt_ref, g_ref, y0, y1, y2, y3, wt_ref, dy_ref, dw_ref, dg_ref, dgate_ref):
        i = pl.program_id(0)

        @pl.when(i == 0)
        def _():
            dw_ref[...] = jnp.zeros_like(dw_ref)
            dg_ref[...] = jnp.zeros_like(dg_ref)
            dgate_ref[...] = jnp.zeros_like(dgate_ref)

        ov = o_ref[...]
        rstd = lax.rsqrt(jnp.mean(ov * ov, axis=-1, keepdims=True) + EPS)
        r = ov * rstd
        g = g_ref[...]
        dx = dxn_ref[...]
        dgate_ref[...] += jnp.where(_row_onehot(sel(i)), jnp.sum(dx * (r * g), axis=0, keepdims=True), 0.0)
        dn = dx * gt_ref[0]
        dg_ref[...] += jnp.sum(dn * r, axis=0, keepdims=True)
        dr = dn * g
        do = rstd * (dr - r * jnp.mean(dr * r, axis=-1, keepdims=True))
        dob = do.astype(bf16)
        dy_ref[...] = _dot(dob, wt_ref[...])
        y = jnp.concatenate([y0[...], y1[...], y2[...], y3[...]], axis=1)
        dw_ref[...] += _dot_tn(y, dob)

    yspec = pl.BlockSpec((TM, BW), lambda i: (i, 0))
    row = pl.BlockSpec((TM, D), lambda i: (i, 0))
    return pl.pallas_call(
        body, name="outproj_bwd", grid=(n // TM,),
        in_specs=[row, row, pl.BlockSpec((1, 1, D), lambda i: (sel(i), 0, 0)), _full((1, D)),
                  yspec, yspec, yspec, yspec, _full((D, D))],
        out_specs=[row, _full((D, D)), _full((1, D)), _full((8, D))],
        out_shape=[_sds((n, D)), _sds((D, D)), _sds((1, D)), _sds((8, D))],
        compiler_params=_cparams(("arbitrary",), VMEM_BIG),
    )(dxn, o, gate_t, g_post, *ys, w_out_t)


def inproj_bwd_x(dp, w_in_t, x, shift_t, scale_t, g_pre, dxn, n_batch, sb):
    n = x.shape[0]

    def sel(i):
        return jnp.where(i % sb == 0, n_batch, i // sb)

    def body(dp_ref, wt_ref, x_ref, sc_ref, g_ref, dxn_ref, dx_ref, dg_ref, dsh_ref, dsc_ref):
        i = pl.program_id(0)

        @pl.when(i == 0)
        def _():
            dg_ref[...] = jnp.zeros_like(dg_ref)
            dsh_ref[...] = jnp.zeros_like(dsh_ref)
            dsc_ref[...] = jnp.zeros_like(dsc_ref)

        dh = _dot(dp_ref[...], wt_ref[...])
        xv = x_ref[...]
        rstd = lax.rsqrt(jnp.mean(xv * xv, axis=-1, keepdims=True) + EPS)
        r = xv * rstd
        g = g_ref[...]
        hot = _row_onehot(sel(i))
        dsh_ref[...] += jnp.where(hot, jnp.sum(dh, axis=0, keepdims=True), 0.0)
        dsc_ref[...] += jnp.where(hot, jnp.sum(dh * (r * g), axis=0, keepdims=True), 0.0)
        t = dh * (1.0 + sc_ref[0])
        dg_ref[...] += jnp.sum(t * r, axis=0, keepdims=True)
        dr = t * g
        dx_ref[...] = dxn_ref[...] + rstd * (dr - r * jnp.mean(dr * r, axis=-1, keepdims=True))

    row = pl.BlockSpec((TM, D), lambda i: (i, 0))
    return pl.pallas_call(
        body, name="inproj_bwd_x", grid=(n // TM,),
        in_specs=[pl.BlockSpec((TM, PW), lambda i: (i, 0)), _full((PW, D)), row,
                  pl.BlockSpec((1, 1, D), lambda i: (sel(i), 0, 0)), _full((1, D)), row],
        out_specs=[row, _full((1, D)), _full((8, D)), _full((8, D))],
        out_shape=[_sds((n, D)), _sds((1, D)), _sds((8, D)), _sds((8, D))],
        compiler_params=_cparams(("arbitrary",), VMEM_BIG),
    )(dp, w_in_t, x, scale_t, g_pre, dxn)


def dw_in(h, dp):
    n = h.shape[0]
    tk, tn = 512, 1024

    def body(h_ref, dp_ref, o_ref):
        @pl.when(pl.program_id(1) == 0)
        def _():
            o_ref[...] = jnp.zeros_like(o_ref)
        o_ref[...] += _dot_tn(h_ref[...], dp_ref[...])

    return pl.pallas_call(
        body, name="dw_in", grid=(PW // tn, n // tk),
        in_specs=[pl.BlockSpec((tk, D), lambda j, k: (k, 0)), pl.BlockSpec((tk, tn), lambda j, k: (k, j))],
        out_specs=pl.BlockSpec((D, tn), lambda j, k: (0, j)),
        out_shape=_sds((D, PW)),
        compiler_params=_cparams(("parallel", "arbitrary"), VMEM_BIG),
    )(h, dp)


def loss_head(xf, target, t_ctx):
    nb, s, _ = xf.shape
    jc = t_ctx // TM

    def body(x_ref, t_ref, dx_ref, l_ref):
        b, j = pl.program_id(0), pl.program_id(1)

        @pl.when((b == 0) & (j == 0))
        def _():
            l_ref[...] = jnp.zeros_like(l_ref)

        @pl.when(j < jc)
        def _():
            dx_ref[...] = jnp.zeros_like(dx_ref)

        @pl.when(j >= jc)
        def _():
            diff = x_ref[0] - t_ref[0]
            dx_ref[0] = diff * (1.0 / D)
            l_ref[...] += 0.5 * jnp.sum(diff * diff) * (1.0 / D)

    return pl.pallas_call(
        body, name="loss_head", grid=(nb, s // TM),
        in_specs=[pl.BlockSpec((1, TM, D), lambda b, j: (b, j, 0)),
                  pl.BlockSpec((1, TM, D), lambda b, j: (b, jnp.maximum(j - jc, 0), 0))],
        out_specs=[pl.BlockSpec((1, TM, D), lambda b, j: (b, j, 0)), _full((1, 128))],
        out_shape=[_sds((nb, s, D)), _sds((1, 128))],
        compiler_params=_cparams(("arbitrary", "arbitrary")),
    )(xf, target)


def _chunk_maps(n_ctx, n_lat):
    n = n_ctx + n_lat

    def cf(t):
        return t

    def cb(t):
        return jnp.where(t < n_ctx, n_ctx - 1 - t, n - 1 - t + n_ctx)
    return n, cf, cb


def ret_scan_fwd(p3, cos, sins, consts, t_ctx):
    nb, s, _ = p3.shape
    n, cf, cb = _chunk_maps(t_ctx // RC, (s - t_ctx) // RC)
    intra, qdec, kdec, cd, bd, hm = consts
    cmaps = (cf, cb)

    def body(qf, kf, vf, qb, kb, vb, cosf, sinf, cosb, sinb, intra_r, qdec_r, kdec_r, cd_r, bd_r, hm_r,
             of_ref, ob_ref, sall_ref, s_sc):
        @pl.when(pl.program_id(1) == 0)
        def _():
            s_sc[...] = jnp.zeros_like(s_sc)
        ins = ((qf, kf, vf, cosf, sinf, of_ref), (qb, kb, vb, cosb, sinb, ob_ref))
        for d, (q, k, v, c_, s_, o_ref) in enumerate(ins):
            st = s_sc[d]
            sall_ref[0, d, 0] = st
            s_new, o = _ret_step(st, q[0], k[0], v[0], c_[...], s_[...], intra_r[d], qdec_r[d], kdec_r[d],
                                 cd_r[...], bd_r[...], hm_r[...])
            s_sc[d] = s_new
            o_ref[0] = o

    def pspec(m, seg):
        return pl.BlockSpec((1, RC, BW), lambda b, t: (b, m(t), seg))

    def tspec(m):
        return pl.BlockSpec((RC, BW), lambda b, t: (m(t), 0))

    return pl.pallas_call(
        body, name="ret_scan_fwd", grid=(nb, n),
        in_specs=[pspec(cf, 0), pspec(cf, 1), pspec(cf, 2), pspec(cb, 0), pspec(cb, 1), pspec(cb, 2),
                  tspec(cf), tspec(cf), tspec(cb), tspec(cb),
                  _full(intra.shape), _full(qdec.shape), _full(kdec.shape), _full(cd.shape), _full(bd.shape),
                  _full(hm.shape)],
        out_specs=[pl.BlockSpec((1, RC, BW), lambda b, t: (b, cf(t), 0)),
                   pl.BlockSpec((1, RC, BW), lambda b, t: (b, cb(t), 0)),
                   pl.BlockSpec((1, 2, 1, BW, BW), lambda b, t: (b, 0, t, 0, 0))],
        out_shape=[_sds((nb, s, BW)), _sds((nb, s, BW)), _sds((nb, 2, n, BW, BW))],
        scratch_shapes=[pltpu.VMEM((2, BW, BW), f32)],
        compiler_params=_cparams(("arbitrary", "arbitrary")),
    )(p3, p3, p3, p3, p3, p3, cos, sins, cos, sins, intra, qdec, kdec, cd, bd, hm)


def ret_scan_bwd(p3, cos, sins, consts, s_all, do, t_ctx):
    nb, s, _ = p3.shape
    n, cf, cb = _chunk_maps(t_ctx // RC, (s - t_ctx) // RC)
    intra, qdec, kdec, cd, bd, hm = consts

    def rf(t):
        return cf(n - 1 - t)

    def rb(t):
        return cb(n - 1 - t)

    def body(qf, kf, vf, qb, kb, vb, cosf, sinf, cosb, sinb, intra_r, qdec_r, kdec_r, cd_r, bd_r, hm_r,
             sall_ref, dof, dob, dqf, dkf, dvf, dqb, dkb, dvb, ds_sc):
        @pl.when(pl.program_id(1) == 0)
        def _():
            ds_sc[...] = jnp.zeros_like(ds_sc)
        ins = ((qf, kf, vf, cosf, sinf, dof, (dqf, dkf, dvf)), (qb, kb, vb, cosb, sinb, dob, (dqb, dkb, dvb)))
        for d, (q, k, v, c_, s_, do_ref, outs) in enumerate(ins):
            step = functools.partial(_ret_step, cos=c_[...], sins=s_[...], intra=intra_r[d], qdec=qdec_r[d],
                                     kdec=kdec_r[d], cd=cd_r[...], bd=bd_r[...], hm=hm_r[...])
            _, vjp = jax.vjp(step, sall_ref[0, d, 0], q[0], k[0], v[0])
            ds, dq, dk, dv = vjp((ds_sc[d], do_ref[0]))
            ds_sc[d] = ds
            outs[0][0] = dq
            outs[1][0] = dk
            outs[2][0] = dv

    def pspec(m, seg):
        return pl.BlockSpec((1, RC, BW), lambda b, t: (b, m(t), seg))

    def tspec(m):
        return pl.BlockSpec((RC, BW), lambda b, t: (m(t), 0))

    def ospec(m):
        return pl.BlockSpec((1, RC, BW), lambda b, t: (b, m(t), 0))

    return pl.pallas_call(
        body, name="ret_scan_bwd", grid=(nb, n),
        in_specs=[pspec(rf, 0), pspec(rf, 1), pspec(rf, 2), pspec(rb, 0), pspec(rb, 1), pspec(rb, 2),
                  tspec(rf), tspec(rf), tspec(rb), tspec(rb),
                  _full(intra.shape), _full(qdec.shape), _full(kdec.shape), _full(cd.shape), _full(bd.shape),
                  _full(hm.shape),
                  pl.BlockSpec((1, 2, 1, BW, BW), lambda b, t: (b, 0, n - 1 - t, 0, 0)), ospec(rf), ospec(rb)],
        out_specs=[ospec(rf), ospec(rf), ospec(rf), ospec(rb), ospec(rb), ospec(rb)],
        out_shape=[_sds((nb, s, BW))] * 6,
        scratch_shapes=[pltpu.VMEM((2, BW, BW), f32)],
        compiler_params=_cparams(("arbitrary", "arbitrary")),
    )(p3, p3, p3, p3, p3, p3, cos, sins, cos, sins, intra, qdec, kdec, cd, bd, hm, s_all, do, do)


def mix_finish_fwd(fn, name, o_f, o_b, p3, zseg, norm_g, bd):
    nb, s, _ = p3.shape

    def body(of_ref, ob_ref, z_ref, g_ref, bd_ref, y_ref):
        y_ref[0] = fn(of_ref[0], ob_ref[0], z_ref[0], g_ref[...], bd_ref[...]).astype(bf16)

    blk = pl.BlockSpec((1, TM, BW), lambda b, j: (b, j, 0))
    return pl.pallas_call(
        body, name=name, grid=(nb, s // TM),
        in_specs=[blk, blk, pl.BlockSpec((1, TM, BW), lambda b, j: (b, j, zseg)), _full((1, BW)), _full((BW, BW))],
        out_specs=blk, out_shape=_sds((nb, s, BW), bf16),
        compiler_params=_cparams(("arbitrary", "arbitrary")),
    )(o_f, o_b, p3, norm_g, bd)


def mix_finish_bwd(fn, name, o_f, o_b, p3, zseg, norm_g, bd, dy3, yseg):
    nb, s, _ = p3.shape

    def body(of_ref, ob_ref, z_ref, g_ref, bd_ref, dy_ref, do_ref, dz_ref, dg_ref):
        @pl.when((pl.program_id(0) == 0) & (pl.program_id(1) == 0))
        def _():
            dg_ref[...] = jnp.zeros_like(dg_ref)
        bdv = bd_ref[...]
        _, vjp = jax.vjp(lambda a, b, z, g: fn(a, b, z, g, bdv), of_ref[0], ob_ref[0], z_ref[0], g_ref[...])
        do, _, dz, dg = vjp(dy_ref[0])
        do_ref[0] = do
        dz_ref[0] = dz
        dg_ref[...] += dg

    blk = pl.BlockSpec((1, TM, BW), lambda b, j: (b, j, 0))
    return pl.pallas_call(
        body, name=name, grid=(nb, s // TM),
        in_specs=[blk, blk, pl.BlockSpec((1, TM, BW), lambda b, j: (b, j, zseg)), _full((1, BW)), _full((BW, BW)),
                  pl.BlockSpec((1, TM, BW), lambda b, j: (b, j, yseg))],
        out_specs=[blk, blk, _full((1, BW))],
        out_shape=[_sds((nb, s, BW)), _sds((nb, s, BW)), _sds((1, BW))],
        compiler_params=_cparams(("arbitrary", "arbitrary")),
    )(o_f, o_b, p3, norm_g, bd, dy3)


def gdn_conv_fwd(p3, w, seg, t_ctx):
    nb, s, _ = p3.shape
    sd, su = _make_shifts(t_ctx, s)

    def body(x_ref, w_ref, o_ref):
        o_ref[0] = _silu(_conv3(x_ref[0], w_ref[...], sd, su))

    return pl.pallas_call(
        body, name="gdn_conv_fwd", grid=(nb, 2),
        in_specs=[pl.BlockSpec((1, s, 128), lambda b, j: (b, 0, 2 * seg + j)), pl.BlockSpec((3, 128), lambda b, j: (0, j))],
        out_specs=pl.BlockSpec((1, s, 128), lambda b, j: (b, 0, j)),
        out_shape=_sds((nb, s, BW)),
        compiler_params=_cparams(("arbitrary", "arbitrary")),
    )(p3, w)


def gdn_conv_bwd(p3, w, seg, d_f, d_b, t_ctx):
    nb, s, _ = p3.shape
    sd, su = _make_shifts(t_ctx, s)

    def body(x_ref, w_ref, df_ref, db_ref, dx_ref, dw_ref):
        @pl.when(pl.program_id(1) == 0)
        def _():
            dw_ref[...] = jnp.zeros_like(dw_ref)
        _, vjp = jax.vjp(lambda x, w_: _silu(_conv3(x, w_, sd, su)), x_ref[0], w_ref[...])
        dx, dw = vjp(df_ref[0] + db_ref[0])
        dx_ref[0] = dx
        dw_ref[...] += dw

    blk = pl.BlockSpec((1, s, 128), lambda j, b: (b, 0, j))
    return pl.pallas_call(
        body, name="gdn_conv_bwd", grid=(2, nb),
        in_specs=[pl.BlockSpec((1, s, 128), lambda j, b: (b, 0, 2 * seg + j)), pl.BlockSpec((3, 128), lambda j, b: (0, j)),
                  blk, blk],
        out_specs=[blk, pl.BlockSpec((3, 128), lambda j, b: (0, j))],
        out_shape=[_sds((nb, s, BW)), _sds((3, BW))],
        compiler_params=_cparams(("arbitrary", "arbitrary"), VMEM_BIG),
    )(p3, w, d_f, d_b)


def _pairs(f_ref, b_ref, nb):
    return jnp.stack([r[b, :, PAIR_W * p:PAIR_W * (p + 1)] for b in range(nb) for r in (f_ref, b_ref) for p in range(2)])


def _gates(f_ref, b_ref, nb):
    return jnp.stack([r[b] for b in range(nb) for r in (f_ref, b_ref)])


def _unpairs(a, f_ref, b_ref, nb):
    for b in range(nb):
        for d, r in enumerate((f_ref, b_ref)):
            for p in range(2):
                r[b, :, PAIR_W * p:PAIR_W * (p + 1)] = a[4 * b + 2 * d + p]


def gdn_scan_fwd(cq, ck, cv, p3, alog, dtb, consts, t_ctx):
    nb, s, _ = p3.shape
    n, cf, cb = _chunk_maps(t_ctx // GC, (s - t_ctx) // GC)
    gblk = GATE_COL // 128

    nz = 4 * nb

    def body(qf, kf, vf, gf, qb, kb, vb, gb, al_ref, dt_ref, tm_r, tm2_r, st2_r, eg_r, eb_r, egt_r, dsel_r, eye_r, bd_r,
             of_ref, ob_ref, sall_ref, s_sc):
        @pl.when(pl.program_id(0) == 0)
        def _():
            s_sc[...] = jnp.zeros_like(s_sc)
        st = s_sc[...]
        sall_ref[0] = st
        s_new, o = _gdn_step(st, _pairs(qf, qb, nb), _pairs(kf, kb, nb), _pairs(vf, vb, nb), _gates(gf, gb, nb),
                             al_ref[...], dt_ref[...], tm_r[...], tm2_r[...], st2_r[...], eg_r[...], eb_r[...],
                             egt_r[...], dsel_r[...], eye_r[...], bd_r[...])
        s_sc[...] = s_new
        _unpairs(o, of_ref, ob_ref, nb)

    def cspec(m):
        return pl.BlockSpec((nb, GC, BW), lambda t: (0, m(t), 0))

    def gspec(m):
        return pl.BlockSpec((nb, GC, 128), lambda t: (0, m(t), gblk))

    return pl.pallas_call(
        body, name="gdn_scan_fwd", grid=(n,),
        in_specs=[cspec(cf), cspec(cf), cspec(cf), gspec(cf), cspec(cb), cspec(cb), cspec(cb), gspec(cb),
                  _full((1, 128)), _full((1, 128))] + [_full(c.shape) for c in consts],
        out_specs=[cspec(cf), cspec(cb), pl.BlockSpec((1, nz, PAIR_W, PAIR_W), lambda t: (t, 0, 0, 0))],
        out_shape=[_sds((nb, s, BW)), _sds((nb, s, BW)), _sds((n, nz, PAIR_W, PAIR_W))],
        scratch_shapes=[pltpu.VMEM((nz, PAIR_W, PAIR_W), f32)],
        compiler_params=_cparams(("arbitrary",)),
    )(cq, ck, cv, p3, cq, ck, cv, p3, alog, dtb, *consts)


def gdn_scan_bwd(cq, ck, cv, p3, alog, dtb, consts, s_all, do, t_ctx):
    nb, s, _ = p3.shape
    n, cf, cb = _chunk_maps(t_ctx // GC, (s - t_ctx) // GC)
    gblk = GATE_COL // 128

    def rf(t):
        return cf(n - 1 - t)

    def rb(t):
        return cb(n - 1 - t)

    nz = 4 * nb

    def body(qf, kf, vf, gf, qb, kb, vb, gb, al_ref, dt_ref, tm_r, tm2_r, st2_r, eg_r, eb_r, egt_r, dsel_r, eye_r, bd_r,
             sall_ref, dof, dob, dqf, dkf, dvf, dgf, dqb, dkb, dvb, dgb, dal_ref, ddt_ref, ds_sc):
        @pl.when(pl.program_id(0) == 0)
        def _():
            dal_ref[...] = jnp.zeros_like(dal_ref)
            ddt_ref[...] = jnp.zeros_like(ddt_ref)
            ds_sc[...] = jnp.zeros_like(ds_sc)
        step = functools.partial(_gdn_step, tmask=tm_r[...], tmask2=tm2_r[...], strict2=st2_r[...], exp_g=eg_r[...],
                                 exp_b=eb_r[...], exp_gt=egt_r[...], dsel2=dsel_r[...], eye2=eye_r[...], bd2=bd_r[...])
        _, vjp = jax.vjp(step, sall_ref[0], _pairs(qf, qb, nb), _pairs(kf, kb, nb), _pairs(vf, vb, nb),
                         _gates(gf, gb, nb), al_ref[...], dt_ref[...])
        ds, dq, dk, dv, dg, dal, ddt = vjp((ds_sc[...], _pairs(dof, dob, nb)))
        ds_sc[...] = ds
        _unpairs(dq, dqf, dqb, nb)
        _unpairs(dk, dkf, dkb, nb)
        _unpairs(dv, dvf, dvb, nb)
        for b in range(nb):
            dgf[b] = dg[2 * b]
            dgb[b] = dg[2 * b + 1]
        dal_ref[...] += dal
        ddt_ref[...] += ddt

    def cspec(m):
        return pl.BlockSpec((nb, GC, BW), lambda t: (0, m(t), 0))

    def gspec(m):
        return pl.BlockSpec((nb, GC, 128), lambda t: (0, m(t), gblk))

    def gout(m):
        return pl.BlockSpec((nb, GC, 128), lambda t: (0, m(t), 0))

    return pl.pallas_call(
        body, name="gdn_scan_bwd", grid=(n,),
        in_specs=[cspec(rf), cspec(rf), cspec(rf), gspec(rf), cspec(rb), cspec(rb), cspec(rb), gspec(rb),
                  _full((1, 128)), _full((1, 128))] + [_full(c.shape) for c in consts]
                 + [pl.BlockSpec((1, nz, PAIR_W, PAIR_W), lambda t: (n - 1 - t, 0, 0, 0)), cspec(rf), cspec(rb)],
        out_specs=[cspec(rf), cspec(rf), cspec(rf), gout(rf), cspec(rb), cspec(rb), cspec(rb), gout(rb),
                   _full((1, 128)), _full((1, 128))],
        out_shape=[_sds((nb, s, BW))] * 3 + [_sds((nb, s, 128))] + [_sds((nb, s, BW))] * 3 + [_sds((nb, s, 128))]
                  + [_sds((1, 128)), _sds((1, 128))],
        scratch_shapes=[pltpu.VMEM((nz, PAIR_W, PAIR_W), f32)],
        compiler_params=_cparams(("arbitrary",), VMEM_BIG),
    )(cq, ck, cv, p3, cq, ck, cv, p3, alog, dtb, *consts, s_all, do, do)


def sg_fwd(p3, w, b, hm4):
    nb, s, _ = p3.shape

    def body(u_ref, v_ref, z_ref, w_ref, b_ref, hm_ref, y_ref):
        y_ref[0] = _sg_chunk(u_ref[0], v_ref[0], z_ref[0], w_ref[...], b_ref[...], hm_ref[...]).astype(bf16)

    def seg(k):
        return pl.BlockSpec((1, RC, BW), lambda bi, i: (bi, i, k))

    return pl.pallas_call(
        body, name="sg_fwd", grid=(nb, s // RC),
        in_specs=[seg(4), seg(5), seg(6), _full((NH, RC, RC)), _full((NH, RC)), _full((NH, BW))],
        out_specs=pl.BlockSpec((1, RC, BW), lambda bi, i: (bi, i, 0)),
        out_shape=_sds((nb, s, BW), bf16),
        compiler_params=_cparams(("arbitrary", "arbitrary")),
    )(p3, p3, p3, w, b, hm4)


def sg_bwd(p3, w, b, hm4, dy3):
    nb, s, _ = p3.shape

    def body(u_ref, v_ref, z_ref, w_ref, b_ref, hm_ref, dy_ref, du_ref, dv_ref, dz_ref, dw_ref, db_ref):
        @pl.when((pl.program_id(0) == 0) & (pl.program_id(1) == 0))
        def _():
            dw_ref[...] = jnp.zeros_like(dw_ref)
            db_ref[...] = jnp.zeros_like(db_ref)
        hm = hm_ref[...]
        _, vjp = jax.vjp(lambda u, v, z, w_, b_: _sg_chunk(u, v, z, w_, b_, hm),
                         u_ref[0], v_ref[0], z_ref[0], w_ref[...], b_ref[...])
        du, dv, dz, dw, db = vjp(dy_ref[0])
        du_ref[0] = du
        dv_ref[0] = dv
        dz_ref[0] = dz
        dw_ref[...] += dw
        db_ref[...] += db

    def seg(k):
        return pl.BlockSpec((1, RC, BW), lambda bi, i: (bi, i, k))

    blk = pl.BlockSpec((1, RC, BW), lambda bi, i: (bi, i, 0))
    return pl.pallas_call(
        body, name="sg_bwd", grid=(nb, s // RC),
        in_specs=[seg(4), seg(5), seg(6), _full((NH, RC, RC)), _full((NH, RC)), _full((NH, BW)), seg(1)],
        out_specs=[blk, blk, blk, _full((NH, RC, RC)), _full((NH, RC))],
        out_shape=[_sds((nb, s, BW))] * 3 + [_sds((NH, RC, RC)), _sds((NH, RC))],
        compiler_params=_cparams(("arbitrary", "arbitrary")),
    )(p3, p3, p3, w, b, hm4, dy3)


def _sc_fn(b, c, h, z, w, sd, su):
    return b * _conv3(c * h, w, sd, su) * _silu(z)


def sc_fwd(p3, w, t_ctx):
    nb, s, _ = p3.shape
    sd, su = _make_shifts(t_ctx, s)

    def body(b_ref, c_ref, h_ref, z_ref, w_ref, y_ref):
        y_ref[0] = _sc_fn(b_ref[0], c_ref[0], h_ref[0], z_ref[0], w_ref[...], sd, su).astype(bf16)

    def seg(k):
        return pl.BlockSpec((1, s, 128), lambda bi, j: (bi, 0, 2 * k + j))

    return pl.pallas_call(
        body, name="sc_fwd", grid=(nb, 2),
        in_specs=[seg(7), seg(8), seg(9), seg(10), pl.BlockSpec((3, 128), lambda bi, j: (0, j))],
        out_specs=pl.BlockSpec((1, s, 128), lambda bi, j: (bi, 0, j)),
        out_shape=_sds((nb, s, BW), bf16),
        compiler_params=_cparams(("arbitrary", "arbitrary"), VMEM_BIG),
    )(p3, p3, p3, p3, w)


def sc_bwd(p3, w, dy3, t_ctx):
    nb, s, _ = p3.shape
    sd, su = _make_shifts(t_ctx, s)

    def body(b_ref, c_ref, h_ref, z_ref, w_ref, dy_ref, db_ref, dc_ref, dh_ref, dz_ref, dw_ref):
        @pl.when(pl.program_id(1) == 0)
        def _():
            dw_ref[...] = jnp.zeros_like(dw_ref)
        _, vjp = jax.vjp(lambda b, c, h, z, w_: _sc_fn(b, c, h, z, w_, sd, su),
                         b_ref[0], c_ref[0], h_ref[0], z_ref[0], w_ref[...])
        db, dc, dh, dz, dw = vjp(dy_ref[0])
        db_ref[0] = db
        dc_ref[0] = dc
        dh_ref[0] = dh
        dz_ref[0] = dz
        dw_ref[...] += dw

    def seg(k):
        return pl.BlockSpec((1, s, 128), lambda j, bi: (bi, 0, 2 * k + j))

    blk = pl.BlockSpec((1, s, 128), lambda j, bi: (bi, 0, j))
    wspec = pl.BlockSpec((3, 128), lambda j, bi: (0, j))
    return pl.pallas_call(
        body, name="sc_bwd", grid=(2, nb),
        in_specs=[seg(7), seg(8), seg(9), seg(10), wspec, seg(2)],
        out_specs=[blk, blk, blk, blk, wspec],
        out_shape=[_sds((nb, s, BW))] * 4 + [_sds((3, BW))],
        compiler_params=_cparams(("arbitrary", "arbitrary"), VMEM_BIG),
    )(p3, p3, p3, p3, w, dy3)


def assemble_dp(pairs, singles_a, gdn_x, singles_b, gates):
    nb, s, _ = singles_a[0].shape
    flat = [a for pr in pairs for a in pr] + list(singles_a) + list(gdn_x) + list(singles_b) + list(gates)
    n_pairs, n_a, n_x, n_b = len(pairs), len(singles_a), len(gdn_x), len(singles_b)

    def body(*refs):
        out = refs[-1]
        ins = refs[:-1]
        col = 0
        for p in range(n_pairs):
            out[0, :, col:col + BW] = (ins[2 * p][0] + ins[2 * p + 1][0]).astype(bf16)
            col += BW
        k = 2 * n_pairs
        for _ in range(n_a + n_x + n_b):
            out[0, :, col:col + BW] = ins[k][0].astype(bf16)
            col += BW
            k += 1
        out[0, :, col:col + 128] = (ins[k][0] + ins[k + 1][0]).astype(bf16)
        out[0, :, col + 128:] = jnp.zeros((TM, PW - col - 128), bf16)

    def spec(a):
        return pl.BlockSpec((1, TM, a.shape[-1]), lambda b, j: (b, j, 0))

    return pl.pallas_call(
        body, name="assemble_dp", grid=(nb, s // TM),
        in_specs=[spec(a) for a in flat],
        out_specs=pl.BlockSpec((1, TM, PW), lambda b, j: (b, j, 0)),
        out_shape=_sds((nb, s, PW), bf16),
        compiler_params=_cparams(("arbitrary", "arbitrary")),
    )(*flat)


def mod_fwd(c_rows, w_mod, b_cols):
    nl, _, wc = w_mod.shape
    nr = c_rows.shape[0]

    def body(c_ref, w_ref, b_ref, o_ref):
        o_ref[0] = _dot(_silu(c_ref[...]), w_ref[0], precision=HI) + b_ref[0]

    return pl.pallas_call(
        body, name="mod_fwd", grid=(nl,),
        in_specs=[_full((nr, D)), pl.BlockSpec((1, D, wc), lambda l: (l, 0, 0)), pl.BlockSpec((1, 1, wc), lambda l: (l, 0, 0))],
        out_specs=pl.BlockSpec((1, nr, wc), lambda l: (l, 0, 0)),
        out_shape=_sds((nl, nr, wc)),
        compiler_params=_cparams(("arbitrary",)),
    )(c_rows, w_mod, b_cols)


def mod_bwd(c_rows, w_mod, dm_cols, dm_full):
    nl, _, wc = w_mod.shape
    nr = c_rows.shape[0]

    def body(c_ref, w_ref, dmc_ref, dmf_ref, gw_ref, gb_ref, dcc_ref):
        @pl.when(pl.program_id(0) == 0)
        def _():
            dcc_ref[...] = jnp.zeros_like(dcc_ref)
        a = _silu(c_ref[...])
        dmc = dmc_ref[0]
        gw_ref[0] = _dot_tn(a, dmc, precision=HI)
        gb_ref[0] = jnp.sum(dmf_ref[0], axis=0, keepdims=True)
        dcc_ref[...] += _dot_nt(dmc[nr - 8:nr], w_ref[0], precision=HI)

    return pl.pallas_call(
        body, name="mod_bwd", grid=(nl,),
        in_specs=[_full((nr, D)), pl.BlockSpec((1, D, wc), lambda l: (l, 0, 0)),
                  pl.BlockSpec((1, nr, wc), lambda l: (l, 0, 0)), pl.BlockSpec((1, nr, 3 * D), lambda l: (l, 0, 0))],
        out_specs=[pl.BlockSpec((1, D, wc), lambda l: (l, 0, 0)), pl.BlockSpec((1, 1, 3 * D), lambda l: (l, 0, 0)),
                   _full((8, D))],
        out_shape=[_sds((nl, D, wc)), _sds((nl, 1, 3 * D)), _sds((8, D))],
        compiler_params=_cparams(("arbitrary",)),
    )(c_rows, w_mod, dm_cols, dm_full)


def cctx_grad(parts, c_ctx):
    def body(p_ref, c_ref, o_ref):
        tot = p_ref[0, 0:1, :]
        for k in (2, 4, 6):
            tot = tot + p_ref[k, 0:1, :]
        c = c_ref[...]
        sg = jax.nn.sigmoid(c)
        o_ref[...] = tot * (sg * (1.0 + c * (1.0 - sg)))

    return pl.pallas_call(body, name="cctx_grad", out_shape=_sds((1, D)))(parts, c_ctx)


def sum_lead(x, out_dtype=f32, tr=256):
    k, r, c = x.shape
    tr = min(tr, r)
    assert r % tr == 0

    def body(x_ref, o_ref):
        tot = x_ref[0].astype(f32)
        for i in range(1, k):
            tot = tot + x_ref[i].astype(f32)
        o_ref[...] = tot.astype(out_dtype)

    return pl.pallas_call(
        body, name="sum_lead", grid=(r // tr,),
        in_specs=[pl.BlockSpec((k, tr, c), lambda i: (0, i, 0))],
        out_specs=pl.BlockSpec((tr, c), lambda i: (i, 0)),
        out_shape=_sds((r, c), out_dtype),
        compiler_params=_cparams(("arbitrary",)),
    )(x)


def adamw(w, m, v, g1, g2=None, tr=256):
    r, c = w.shape
    tr = min(tr, r)
    assert r % tr == 0
    two = g2 is not None
    c1 = 1.0 / (1.0 - ADAM_B1 ** ADAM_STEP)
    c2 = 1.0 / (1.0 - ADAM_B2 ** ADAM_STEP)

    def body(*refs):
        w_ref, m_ref, v_ref, g_ref = refs[:4]
        g = g_ref[...]
        if two:
            g = g + refs[4][...]
        go_ref, d_ref, mo_ref, vo_ref = refs[-4:]
        mn = ADAM_B1 * m_ref[...] + (1.0 - ADAM_B1) * g
        vn = ADAM_B2 * v_ref[...] + (1.0 - ADAM_B2) * (g * g)
        go_ref[...] = g
        mo_ref[...] = mn
        vo_ref[...] = vn
        d_ref[...] = -ADAM_LR * ((mn * c1) / (jnp.sqrt(vn * c2) + ADAM_EPS) + ADAM_WD * w_ref[...])

    blk = pl.BlockSpec((tr, c), lambda i: (i, 0))
    args = [w, m, v, g1] + ([g2] if two else [])
    return pl.pallas_call(
        body, name="adamw", grid=(r // tr,),
        in_specs=[blk] * len(args), out_specs=[blk] * 4, out_shape=[_sds((r, c))] * 4,
        compiler_params=_cparams(("arbitrary",)),
    )(*args)


def _my_pos():
    return lax.axis_index("x"), lax.axis_index("y"), lax.axis_index("c")


def gather8(x):
    shape = x.shape

    def body(x_ref, out_ref, send_sems, recv_sems, local_sem):
        mx, my, mc = _my_pos()
        me = 4 * mx + 2 * my + mc
        mine = pltpu.make_async_copy(x_ref, out_ref.at[me], local_sem)
        mine.start()
        copies = []
        for k in range(1, N_DEV):
            peer = (mx ^ (k >> 2), my ^ ((k >> 1) & 1), mc ^ (k & 1))
            cp = pltpu.make_async_remote_copy(src_ref=x_ref, dst_ref=out_ref.at[me], send_sem=send_sems.at[k - 1],
                                              recv_sem=recv_sems.at[k - 1], device_id=peer, device_id_type=MESH)
            cp.start()
            copies.append(cp)
        for k in range(1, N_DEV):
            src = me ^ k
            pltpu.make_async_remote_copy(src_ref=x_ref, dst_ref=out_ref.at[src], send_sem=send_sems.at[k - 1],
                                         recv_sem=recv_sems.at[k - 1], device_id=(mx, my, mc),
                                         device_id_type=MESH).wait_recv()
        for cp in copies:
            cp.wait_send()
        mine.wait()

    return pl.pallas_call(
        body, name="gather8", out_shape=_sds((N_DEV,) + shape, x.dtype),
        in_specs=[pl.BlockSpec(memory_space=pl.ANY)], out_specs=pl.BlockSpec(memory_space=pl.ANY),
        scratch_shapes=[pltpu.SemaphoreType.DMA((N_DEV - 1,)), pltpu.SemaphoreType.DMA((N_DEV - 1,)),
                        pltpu.SemaphoreType.DMA(())],
    )(x)


def gather4(x):
    shape = x.shape

    def body(x_ref, out_ref, send_sems, recv_sems, local_sem):
        mx, my, mc = _my_pos()
        me = 2 * mx + my
        mine = pltpu.make_async_copy(x_ref, out_ref.at[me], local_sem)
        mine.start()
        copies = []
        for k in range(1, N_CHIPS):
            peer = (mx ^ (k >> 1), my ^ (k & 1), mc)
            cp = pltpu.make_async_remote_copy(src_ref=x_ref, dst_ref=out_ref.at[me], send_sem=send_sems.at[k - 1],
                                              recv_sem=recv_sems.at[k - 1], device_id=peer, device_id_type=MESH)
            cp.start()
            copies.append(cp)
        for k in range(1, N_CHIPS):
            src = me ^ k
            pltpu.make_async_remote_copy(src_ref=x_ref, dst_ref=out_ref.at[src], send_sem=send_sems.at[k - 1],
                                         recv_sem=recv_sems.at[k - 1], device_id=(mx, my, mc),
                                         device_id_type=MESH).wait_recv()
        for cp in copies:
            cp.wait_send()
        mine.wait()

    return pl.pallas_call(
        body, name="gather4", out_shape=_sds((N_CHIPS,) + shape, x.dtype),
        in_specs=[pl.BlockSpec(memory_space=pl.ANY)], out_specs=pl.BlockSpec(memory_space=pl.ANY),
        scratch_shapes=[pltpu.SemaphoreType.DMA((N_CHIPS - 1,)), pltpu.SemaphoreType.DMA((N_CHIPS - 1,)),
                        pltpu.SemaphoreType.DMA(())],
    )(x)


def scatter4(g):
    shape = g.shape[1:]

    def body(g_ref, out_ref, send_sems, recv_sems, local_sem):
        mx, my, mc = _my_pos()
        me = 2 * mx + my
        mine = pltpu.make_async_copy(g_ref.at[me], out_ref.at[me], local_sem)
        mine.start()
        copies = []
        for k in range(1, N_CHIPS):
            peer = (mx ^ (k >> 1), my ^ (k & 1), mc)
            cp = pltpu.make_async_remote_copy(src_ref=g_ref.at[me ^ k], dst_ref=out_ref.at[me], send_sem=send_sems.at[k - 1],
                                              recv_sem=recv_sems.at[k - 1], device_id=peer, device_id_type=MESH)
            cp.start()
            copies.append(cp)
        for k in range(1, N_CHIPS):
            src = me ^ k
            pltpu.make_async_remote_copy(src_ref=g_ref.at[src], dst_ref=out_ref.at[src], send_sem=send_sems.at[k - 1],
                                         recv_sem=recv_sems.at[k - 1], device_id=(mx, my, mc),
                                         device_id_type=MESH).wait_recv()
        for cp in copies:
            cp.wait_send()
        mine.wait()

    return pl.pallas_call(
        body, name="scatter4", out_shape=_sds((N_CHIPS,) + shape, g.dtype),
        in_specs=[pl.BlockSpec(memory_space=pl.ANY)], out_specs=pl.BlockSpec(memory_space=pl.ANY),
        scratch_shapes=[pltpu.SemaphoreType.DMA((N_CHIPS - 1,)), pltpu.SemaphoreType.DMA((N_CHIPS - 1,)),
                        pltpu.SemaphoreType.DMA(())],
    )(g)


def swap_sibling(x):
    def body(x_ref, out_ref, send_sem, recv_sem):
        mx, my, mc = _my_pos()
        cp = pltpu.make_async_remote_copy(src_ref=x_ref, dst_ref=out_ref, send_sem=send_sem, recv_sem=recv_sem,
                                          device_id=(mx, my, 1 - mc), device_id_type=MESH)
        cp.start()
        cp.wait()

    return pl.pallas_call(
        body, name="swap_sibling", out_shape=_sds(x.shape, x.dtype),
        in_specs=[pl.BlockSpec(memory_space=pl.ANY)], out_specs=pl.BlockSpec(memory_space=pl.ANY),
        scratch_shapes=[pltpu.SemaphoreType.DMA(()), pltpu.SemaphoreType.DMA(())],
    )(x)


PACK_ROWS = 64
SMALL = ("c_ctx", "b_mod", "g_pre", "g_post", "ret_norm_g", "sg_w", "sg_b", "sc_conv_w", "gdn_conv_w",
         "gdn_a_log", "gdn_dt_bias", "gdn_norm_g")


def _pack(arrs, width=D):
    rows = []
    for a in arrs:
        flat = a.reshape(-1)
        pad = (-flat.shape[0]) % width
        rows.append(jnp.pad(flat, (0, pad)).reshape(-1, width))
    out = jnp.concatenate(rows, axis=0)
    return jnp.pad(out, ((0, (-out.shape[0]) % PACK_ROWS), (0, 0)))


def _unpack(packed, shapes, width=D):
    outs, r = [], 0
    for shp in shapes:
        size = int(np.prod(shp))
        nr = -(-size // width)
        outs.append(packed[r:r + nr].reshape(-1)[:size].reshape(shp))
        r += nr
    return outs


def kernel(x, c, ctx, c_ctx, w_mod, b_mod, g_pre, g_post, w_in, w_out, ret_norm_g, sg_w, sg_b, sc_conv_w, gdn_conv_w, gdn_a_log, gdn_dt_bias, gdn_norm_g, loss_target, m_c_ctx, m_w_mod, m_b_mod, m_g_pre, m_g_post, m_w_in, m_w_out, m_ret_norm_g, m_sg_w, m_sg_b, m_sc_conv_w, m_gdn_conv_w, m_gdn_a_log, m_gdn_dt_bias, m_gdn_norm_g, v_c_ctx, v_w_mod, v_b_mod, v_g_pre, v_g_post, v_w_in, v_w_out, v_ret_norm_g, v_sg_w, v_sg_b, v_sc_conv_w, v_gdn_conv_w, v_gdn_a_log, v_gdn_dt_bias, v_gdn_norm_g):
    weights = dict(c_ctx=c_ctx, w_mod=w_mod, b_mod=b_mod, g_pre=g_pre, g_post=g_post, w_in=w_in, w_out=w_out,
                   ret_norm_g=ret_norm_g, sg_w=sg_w, sg_b=sg_b, sc_conv_w=sc_conv_w, gdn_conv_w=gdn_conv_w,
                   gdn_a_log=gdn_a_log, gdn_dt_bias=gdn_dt_bias, gdn_norm_g=gdn_norm_g)
    mom = dict(c_ctx=m_c_ctx, w_mod=m_w_mod, b_mod=m_b_mod, g_pre=m_g_pre, g_post=m_g_post, w_in=m_w_in,
               w_out=m_w_out, ret_norm_g=m_ret_norm_g, sg_w=m_sg_w, sg_b=m_sg_b, sc_conv_w=m_sc_conv_w,
               gdn_conv_w=m_gdn_conv_w, gdn_a_log=m_gdn_a_log, gdn_dt_bias=m_gdn_dt_bias, gdn_norm_g=m_gdn_norm_g)
    var = dict(c_ctx=v_c_ctx, w_mod=v_w_mod, b_mod=v_b_mod, g_pre=v_g_pre, g_post=v_g_post, w_in=v_w_in,
               w_out=v_w_out, ret_norm_g=v_ret_norm_g, sg_w=v_sg_w, sg_b=v_sg_b, sc_conv_w=v_sc_conv_w,
               gdn_conv_w=v_gdn_conv_w, gdn_a_log=v_gdn_a_log, gdn_dt_bias=v_gdn_dt_bias, gdn_norm_g=v_gdn_norm_g)

    nb, t_lat, _ = x.shape
    t_ctx = ctx.shape[1]
    s = t_ctx + t_lat
    n = nb * s
    sb = s // TM
    nl = w_in.shape[0]
    wc_in = w_in.shape[2]
    wc_mod = w_mod.shape[2]
    rows_out = w_out.shape[1]
    n_all = nb * N_DEV
    mx, my, mc = _my_pos()
    chip = 2 * mx + my
    dev = 2 * chip + mc

    hm = jnp.asarray(_head_masks())
    hm4 = hm[:, 0, :]
    bd = jnp.asarray(_block_diag())
    ret_c = _ret_consts() + [bd, hm]
    gdn_c = _gdn_consts(nb)
    cos, sins = _rope_tables(t_lat, t_ctx)

    pre = _pack([c, sc_conv_w, gdn_conv_w])
    pre_all = gather8(pre)
    c_parts, scw_parts, gcw_parts = [], [], []
    for k in range(N_DEV):
        ck, sk, gk = _unpack(pre_all[k], [c.shape, sc_conv_w.shape, gdn_conv_w.shape])
        c_parts.append(ck)
        if k % 2 == 0:
            scw_parts.append(sk)
            gcw_parts.append(gk)
    c_all = jnp.concatenate(c_parts, axis=0)
    sc_w_full = jnp.concatenate(scw_parts, axis=-1)
    gdn_w_full = jnp.concatenate(gcw_parts, axis=-1)
    c_rows = jnp.concatenate([c_all, c_ctx[None, :], jnp.zeros((7, D), f32)], axis=0)

    b_cols = lax.dynamic_slice_in_dim(b_mod, chip * wc_mod, wc_mod, axis=1)[:, None, :]
    mod_part = mod_fwd(c_rows, w_mod, b_cols)
    mod_all = gather8(mod_part)
    mod = jnp.concatenate([mod_all[2 * k] for k in range(N_CHIPS)], axis=-1)
    my_rows = jnp.concatenate([lax.dynamic_slice_in_dim(mod, dev * nb, nb, axis=1), mod[:, n_all:n_all + 1]], axis=1)
    shift_t = my_rows[:, :, None, 0:D]
    scale_t = my_rows[:, :, None, D:2 * D]
    gate_t = my_rows[:, :, None, 2 * D:3 * D]

    w_in_all = gather4(w_in.astype(bf16))
    w_in_full = jnp.concatenate([w_in_all[k] for k in range(N_CHIPS)], axis=-1)
    w_in_full = jnp.pad(w_in_full, ((0, 0), (0, 0), (0, PW - IN_W)))
    w_in_t = jnp.transpose(w_in_full, (0, 2, 1))
    w_out_all = gather4(w_out.astype(bf16))
    w_out_full = jnp.concatenate([w_out_all[k] for k in range(N_CHIPS)], axis=1)
    w_out_t = jnp.transpose(w_out_full, (0, 2, 1))

    alog = jnp.pad(gdn_a_log.reshape(nl, 1, 8), ((0, 0), (0, 0), (0, 120)))
    dtb = jnp.pad(gdn_dt_bias.reshape(nl, 1, 8), ((0, 0), (0, 0), (0, 120)))
    gdn_ng = jnp.tile(gdn_norm_g, (1, NH))[:, None, :]
    ret_ng = ret_norm_g[:, None, :]

    xs = jnp.concatenate([ctx, x], axis=1).reshape(n, D)
    saved = []
    for l in range(nl):
        p, h = inproj_fwd(xs, shift_t[l], scale_t[l], g_pre[l][None, :], w_in_full[l], nb, sb)
        p3 = p.reshape(nb, s, PW)
        ro_f, ro_b, rs_all = ret_scan_fwd(p3, cos, sins, ret_c, t_ctx)
        y_ret = mix_finish_fwd(_ret_finish, "ret_finish_fwd", ro_f, ro_b, p3, 3, ret_ng[l], bd)
        y_sg = sg_fwd(p3, sg_w[l], sg_b[l], hm4)
        y_sc = sc_fwd(p3, sc_w_full[l], t_ctx)
        cq, ck, cv = [gdn_conv_fwd(p3, gdn_w_full[l][:, BW * i:BW * (i + 1)], 11 + i, t_ctx) for i in range(3)]
        go_f, go_b, gs_all = gdn_scan_fwd(cq, ck, cv, p3, alog[l], dtb[l], gdn_c, t_ctx)
        y_gdn = mix_finish_fwd(_gdn_finish, "gdn_finish_fwd", go_f, go_b, p3, 14, gdn_ng[l], bd)
        ys = [a.reshape(n, BW) for a in (y_ret, y_sg, y_sc, y_gdn)]
        x_new, o = outproj_fwd(ys, w_out_full[l], xs, gate_t[l], g_post[l][None, :], nb, sb)
        saved.append(dict(x=xs, h=h, p3=p3, ro=(ro_f, ro_b), rs=rs_all, c=(cq, ck, cv), go=(go_f, go_b), gs=gs_all,
                          ys=ys, o=o))
        xs = x_new

    dx3, loss_part = loss_head(xs.reshape(nb, s, D), loss_target, t_ctx)
    loss = lax.psum(loss_part[0, 0], ("x", "y", "c"))

    dxs = dx3.reshape(n, D)
    g_small = {k: [None] * nl for k in SMALL if k not in ("c_ctx", "b_mod")}
    dm_rows = [None] * nl
    gw_in = [None] * nl
    gw_out = [None] * nl
    for l in reversed(range(nl)):
        sv = saved[l]
        p3 = sv["p3"]
        dy, gw_out[l], dg_post, dgate = outproj_bwd(dxs, sv["o"], gate_t[l], g_post[l][None, :], sv["ys"], w_out_t[l], nb, sb)
        dy3 = dy.reshape(nb, s, D)
        r_do, r_dz, d_rng = mix_finish_bwd(_ret_finish, "ret_finish_bwd", *sv["ro"], p3, 3, ret_ng[l], bd, dy3, 0)
        r_d = ret_scan_bwd(p3, cos, sins, ret_c, sv["rs"], r_do, t_ctx)
        s_du, s_dv, s_dz, d_sgw, d_sgb = sg_bwd(p3, sg_w[l], sg_b[l], hm4, dy3)
        c_db, c_dc, c_dh, c_dz, d_scw = sc_bwd(p3, sc_w_full[l], dy3, t_ctx)
        g_do, g_dz, d_gng = mix_finish_bwd(_gdn_finish, "gdn_finish_bwd", *sv["go"], p3, 14, gdn_ng[l], bd, dy3, 3)
        g_d = gdn_scan_bwd(*sv["c"], p3, alog[l], dtb[l], gdn_c, sv["gs"], g_do, t_ctx)
        gx, d_gcw = [], []
        for i in range(3):
            dxi, dwi = gdn_conv_bwd(p3, gdn_w_full[l][:, BW * i:BW * (i + 1)], 11 + i, g_d[i], g_d[4 + i], t_ctx)
            gx.append(dxi)
            d_gcw.append(dwi)
        dp3 = assemble_dp([(r_d[0], r_d[3]), (r_d[1], r_d[4]), (r_d[2], r_d[5])],
                          [r_dz, s_du, s_dv, s_dz, c_db, c_dc, c_dh, c_dz], gx, [g_dz], [g_d[3], g_d[7]])
        dp = dp3.reshape(n, PW)
        dxs, dg_pre, dshift, dscale = inproj_bwd_x(dp, w_in_t[l], sv["x"], shift_t[l], scale_t[l], g_pre[l][None, :], dxs, nb, sb)
        gw_in[l] = dw_in(sv["h"], dp)
        g_small["g_pre"][l] = dg_pre[0]
        g_small["g_post"][l] = dg_post[0]
        g_small["ret_norm_g"][l] = d_rng[0]
        g_small["sg_w"][l] = d_sgw
        g_small["sg_b"][l] = d_sgb
        g_small["sc_conv_w"][l] = d_scw
        g_small["gdn_conv_w"][l] = jnp.concatenate(d_gcw, axis=-1)
        g_small["gdn_a_log"][l] = g_d[8][0, :8].reshape(2, NH)
        g_small["gdn_dt_bias"][l] = g_d[9][0, :8].reshape(2, NH)
        g_small["gdn_norm_g"][l] = d_gng[0].reshape(NH, HD)
        dm_rows[l] = jnp.concatenate([dshift, dscale, dgate], axis=-1)[:nb + 1]
    grad_x = dxs.reshape(nb, s, D)[:, t_ctx:, :]

    g_small = {k: jnp.stack(v) for k, v in g_small.items()}
    dm_rows = jnp.stack(dm_rows)
    dm_slot = jnp.zeros((nl, n_all + 8, 3 * D), f32)
    dm_slot = lax.dynamic_update_slice_in_dim(dm_slot, dm_rows[:, :nb], dev * nb, axis=1)
    dm_slot = lax.dynamic_update_slice_in_dim(dm_slot, dm_rows[:, nb:], n_all, axis=1)
    names2 = [k for k in SMALL if k not in ("c_ctx", "b_mod")]
    pack2 = _pack([g_small[k] for k in names2] + [dm_slot])
    tot2 = sum_lead(gather8(pack2), tr=PACK_ROWS)
    outs2 = _unpack(tot2, [g_small[k].shape for k in names2] + [dm_slot.shape])
    grads = dict(zip(names2, outs2[:-1]))
    dm_all = outs2[-1]
    grads["gdn_norm_g"] = sum_lead(jnp.transpose(grads["gdn_norm_g"], (1, 0, 2)), tr=nl)
    for k in ("sc_conv_w", "gdn_conv_w"):
        wc = weights[k].shape[2]
        grads[k] = lax.dynamic_slice_in_dim(grads[k], chip * wc, wc, axis=2)

    dm_cols = lax.dynamic_slice_in_dim(dm_all, chip * wc_mod, wc_mod, axis=2)
    g_w_mod, g_b_mod, dcc_part = mod_bwd(c_rows, w_mod, dm_cols, dm_all)
    grads["b_mod"] = g_b_mod[:, 0, :]
    grads["c_ctx"] = cctx_grad(gather8(dcc_part), c_ctx[None, :])[0]

    gw_in = jnp.stack(gw_in)[:, :, :IN_W].reshape(nl, D, N_CHIPS, wc_in)
    gw_in = jnp.transpose(gw_in, (2, 0, 1, 3)).astype(bf16).reshape(N_CHIPS, nl * D, wc_in)
    gin_mine = sum_lead(scatter4(gw_in))
    gin_sib = swap_sibling(gin_mine)
    gw_out = jnp.stack(gw_out).reshape(nl, N_CHIPS, rows_out, D)
    gw_out = jnp.transpose(gw_out, (1, 0, 2, 3)).astype(bf16).reshape(N_CHIPS, nl * rows_out, D)
    gout_mine = sum_lead(scatter4(gw_out))
    gout_sib = swap_sibling(gout_mine)

    res = {}
    res["w_in"] = [a.reshape(w_in.shape) for a in adamw(w_in.reshape(nl * D, wc_in), m_w_in.reshape(nl * D, wc_in),
                                                          v_w_in.reshape(nl * D, wc_in), gin_mine, gin_sib)]
    res["w_out"] = [a.reshape(w_out.shape) for a in adamw(w_out.reshape(nl * rows_out, D), m_w_out.reshape(nl * rows_out, D),
                                                            v_w_out.reshape(nl * rows_out, D), gout_mine, gout_sib)]
    res["w_mod"] = [a.reshape(w_mod.shape) for a in adamw(w_mod.reshape(nl * D, wc_mod), m_w_mod.reshape(nl * D, wc_mod),
                                                            v_w_mod.reshape(nl * D, wc_mod), g_w_mod.reshape(nl * D, wc_mod))]
    shapes = [weights[k].shape for k in SMALL]
    small = adamw(_pack([weights[k] for k in SMALL]), _pack([mom[k] for k in SMALL]), _pack([var[k] for k in SMALL]),
                  _pack([grads[k].reshape(weights[k].shape) for k in SMALL]), tr=PACK_ROWS)
    small = [_unpack(a, shapes) for a in small]
    for i, k in enumerate(SMALL):
        res[k] = [small[j][i] for j in range(4)]

    order = ["c_ctx", "w_mod", "b_mod", "g_pre", "g_post", "w_in", "w_out", "ret_norm_g", "sg_w", "sg_b", "sc_conv_w",
             "gdn_conv_w", "gdn_a_log", "gdn_dt_bias", "gdn_norm_g"]
    return (loss, grad_x, *[res[k][0] for k in order], *[res[k][1] for k in order], *[res[k][2] for k in order],
            *[res[k][3] for k in order])
```

```python
import functools

import jax
import jax.numpy as jnp
import numpy as np
from jax import lax
from jax.experimental import pallas as pl
from jax.experimental.pallas import tpu as pltpu

f32 = jnp.float32
bf16 = jnp.bfloat16
HI = lax.Precision.HIGHEST
P3 = lax.Precision.HIGH
MESH = pl.DeviceIdType.MESH

EPS = 1e-6
D = 1024
NH = 4
HD = 64
BW = NH * HD
PAIR_W = 2 * HD
RC = 128
GC = 64
GRID_W = 64
ROPE_BASE = 10000.0
IN_W = 15 * BW + 16
PW = 4096
GATE_COL = 15 * BW
N_CHIPS = 4
N_DEV = 8
TM = 256
ADAM_LR, ADAM_B1, ADAM_B2, ADAM_EPS, ADAM_WD, ADAM_STEP = 0.001, 0.9, 0.999, 1e-08, 0.01, 10
LANE_HEAD = np.arange(BW) // HD
VMEM_BIG = 56 * 1024 * 1024


def _dot(a, b, precision=None):
    return jnp.dot(a, b, precision=precision, preferred_element_type=f32)


def _dot_nt(a, b, precision=None):
    return lax.dot_general(a, b, (((1,), (1,)), ((), ())), precision=precision, preferred_element_type=f32)


def _dot_tn(a, b, precision=None):
    return lax.dot_general(a, b, (((0,), (0,)), ((), ())), precision=precision, preferred_element_type=f32)


def _sds(shape, dtype=f32):
    return jax.ShapeDtypeStruct(shape, dtype)


def _cparams(sem=None, vmem=None):
    kw = {}
    if sem is not None:
        kw["dimension_semantics"] = sem
    if vmem is not None:
        kw["vmem_limit_bytes"] = vmem
    return pltpu.CompilerParams(**kw)


def _full(shape):
    n = len(shape)
    return pl.BlockSpec(shape, lambda *_: (0,) * n)


def _head_masks():
    return np.stack([(LANE_HEAD == h).astype(np.float32)[None, :] for h in range(NH)])


def _block_diag():
    return (LANE_HEAD[:, None] == LANE_HEAD[None, :]).astype(np.float32)


def _tau(c, d):
    return np.arange(c) if d == 0 else c - 1 - np.arange(c)


def _ret_consts():
    lg = np.log(1.0 - 2.0 ** (-5.0 - np.arange(NH)))
    intra = np.zeros((2, NH, RC, RC)); qdec = np.zeros((2, RC, BW)); kdec = np.zeros((2, RC, BW))
    for d in range(2):
        t = _tau(RC, d)
        diff = t[:, None] - t[None, :]
        for h in range(NH):
            intra[d, h] = np.where(diff >= 0, np.exp(np.maximum(diff, 0) * lg[h]), 0.0)
        qdec[d] = np.exp((t[:, None] + 1.0) * lg[LANE_HEAD][None, :])
        kdec[d] = np.exp((RC - 1.0 - t[:, None]) * lg[LANE_HEAD][None, :])
    cd = np.exp(RC * lg[LANE_HEAD])[:, None] * np.ones((1, BW))
    return [jnp.asarray(a, f32) for a in (intra, qdec, kdec, cd)]


def _rope_tables(t_lat, t_ctx):
    nf = HD // 4
    inv = ROPE_BASE ** (-np.arange(nf) / nf)
    pos = np.arange(t_lat)
    ang_r = (pos // GRID_W)[:, None] * inv[None, :]
    ang_c = (pos % GRID_W)[:, None] * inv[None, :]
    ang = np.concatenate([ang_r, ang_r, ang_c, ang_c], axis=1)
    sign = np.concatenate([-np.ones(nf), np.ones(nf), -np.ones(nf), np.ones(nf)])
    cos = np.tile(np.cos(ang), (1, NH)); sins = np.tile(np.sin(ang) * sign, (1, NH))
    cos = np.concatenate([np.ones((t_ctx, BW)), cos]); sins = np.concatenate([np.zeros((t_ctx, BW)), sins])
    return jnp.asarray(cos, f32), jnp.asarray(sins, f32)


def _gdn_consts(nb):
    tmask = np.zeros((2, 2, GC, GC)); tmask2 = np.zeros((2, 2, GC, PAIR_W)); strict2 = np.zeros((2, 2, GC, PAIR_W))
    exp_g = np.zeros((2, 2, 128, PAIR_W)); exp_b = np.zeros((2, 2, 128, PAIR_W))
    for d in range(2):
        t = _tau(GC, d)
        tmask[d, :] = (t[:, None] >= t[None, :])
        tmask2[d, :] = np.tile(t[:, None] >= t[None, :], (1, 2))
        strict2[d, :] = np.tile(t[:, None] > t[None, :], (1, 2))
        for h in range(NH):
            exp_g[d, h // 2, 4 * d + h, (h % 2) * HD:(h % 2 + 1) * HD] = 1.0
            exp_b[d, h // 2, 8 + 4 * d + h, (h % 2) * HD:(h % 2 + 1) * HD] = 1.0
    exp_gt = np.transpose(exp_g, (0, 1, 3, 2))
    per_z = [np.tile(a.reshape((4,) + a.shape[2:]), (nb, 1, 1)) for a in (tmask, tmask2, strict2, exp_g, exp_b, exp_gt)]
    dsel2 = np.tile(np.eye(GC), (1, 2))
    eye2 = np.tile(np.eye(GC), (1, 2))
    bd2 = (np.arange(PAIR_W)[:, None] // HD == np.arange(PAIR_W)[None, :] // HD)
    return [jnp.asarray(a, f32) for a in per_z + [dsel2, eye2, bd2]]


def _swap16(x):
    lane = lax.broadcasted_iota(jnp.int32, x.shape, x.ndim - 1)
    n = x.shape[-1]
    return jnp.where(lane % 32 < 16, pltpu.roll(x, n - 16, axis=x.ndim - 1), pltpu.roll(x, 16, axis=x.ndim - 1))


@jax.custom_vjp
def _rot(x, cos, sins):
    return x * cos + _swap16(x) * sins


def _rot_fwd(x, cos, sins):
    return _rot(x, cos, sins), (cos, sins)


def _rot_bwd(res, g):
    cos, sins = res
    return g * cos + _swap16(g * sins), jnp.zeros_like(cos), jnp.zeros_like(sins)


_rot.defvjp(_rot_fwd, _rot_bwd)


def _silu(z):
    return z * jax.nn.sigmoid(z)


def _head_sum(x, bd):
    return _dot(x, bd, precision=HI)


def _ret_step(s, q, k, v, cos, sins, intra, qdec, kdec, cd, bd, hm):
    qr = _rot(q, cos, sins)
    kr = _rot(k, cos, sins) * (HD ** -0.5)
    o = _dot(qr * qdec, s)
    for h in range(NH):
        sc = _dot_nt(qr * hm[h], kr) * intra[h]
        o = o + _dot(sc, v) * hm[h]
    s_new = s * cd + bd * _dot_tn(kr * kdec, v)
    return s_new, o


def _ret_finish(o_f, o_b, z, norm_g, bd):
    o = o_f + o_b
    mu = _head_sum(o, bd) * (1.0 / HD)
    xc = o - mu
    var = _head_sum(xc * xc, bd) * (1.0 / HD)
    return xc * lax.rsqrt(var + EPS) * norm_g * _silu(z)


def _softplus(x):
    return jnp.maximum(x, 0.0) + jnp.log(1.0 + jnp.exp(-jnp.abs(x)))


def _bmm(a, b, precision=None):
    return lax.dot_general(a, b, (((2,), (1,)), ((0,), (0,))), precision=precision, preferred_element_type=f32)


def _bmm_nt(a, b, precision=None):
    return lax.dot_general(a, b, (((2,), (2,)), ((0,), (0,))), precision=precision, preferred_element_type=f32)


def _bmm_tn(a, b, precision=None):
    return lax.dot_general(a, b, (((1,), (1,)), ((0,), (0,))), precision=precision, preferred_element_type=f32)


def _bdiag(x, bd2):
    return jnp.concatenate([x, x], axis=1) * bd2


@jax.custom_vjp
def _solve_given_inv(m, vb, kbg, inv, bd2):
    return _bmm(inv, _bdiag(vb, bd2), P3), _bmm(inv, _bdiag(kbg, bd2), P3)


def _solve_fwd(m, vb, kbg, inv, bd2):
    u, w = _solve_given_inv(m, vb, kbg, inv, bd2)
    return (u, w), (inv, u, w, bd2)


def _solve_bwd(res, cts):
    inv, u, w, bd2 = res
    du, dw = cts
    c = inv.shape[1]
    t = jnp.swapaxes(_bdiag(inv, bd2), 1, 2)
    inv_t = t[:, :c] + t[:, c:]
    dvb = _bmm(inv_t, _bdiag(du, bd2), P3)
    dkbg = _bmm(inv_t, _bdiag(dw, bd2), P3)
    dm = _bmm_nt(dvb, _bdiag(u, bd2), P3) + _bmm_nt(dkbg, _bdiag(w, bd2), P3)
    return dm, dvb, dkbg, jnp.zeros_like(inv), jnp.zeros_like(bd2)


_solve_given_inv.defvjp(_solve_fwd, _solve_bwd)


def _gdn_step(s, q, k, v, gate, alog, dtb, tmask, tmask2, strict2, exp_g, exp_b, exp_gt, dsel2, eye2, bd2, inv=None):
    z, c, w_ = q.shape
    ne = gate.shape[0]

    def per_pair(a):
        return jnp.broadcast_to(a[:, None], (ne, z // ne) + a.shape[1:]).reshape((z,) + a.shape[1:])

    def rows(a):
        return a.reshape(z * c, w_)

    def bdiag(x):
        return _bdiag(x, bd2)

    g = per_pair(-jnp.exp(alog) * _softplus(gate + dtb))
    beta = per_pair(jax.nn.sigmoid(gate))
    gl = _bmm(g, exp_g, P3)
    gc_l = _bmm(tmask, gl, P3)
    glast_l = jnp.sum(gl, axis=1, keepdims=True)
    glast = jnp.sum(g, axis=1, keepdims=True)
    beta_l = _bmm(beta, exp_b, P3)
    gc_r = jnp.sum(gc_l * dsel2, axis=1, keepdims=True)
    qn = q * lax.rsqrt(_dot(rows(q * q), bd2, P3).reshape(z, c, w_) + EPS)
    kn = k * lax.rsqrt(_dot(rows(k * k), bd2, P3).reshape(z, c, w_) + EPS)
    eg = jnp.exp(gc_l)
    kb = kn * beta_l
    vb = v * beta_l
    kbg = kb * eg
    qs = qn * (HD ** -0.5)
    dec = jnp.exp(jnp.where(tmask2 > 0, gc_l - gc_r, -1e30))
    kns = bdiag(kn)
    m = -(_bmm_nt(kb, kns) * dec * strict2)
    if inv is None:
        inv = eye2 + m
        p = m
        for _ in range(5):
            p = _bmm(p, bdiag(p), P3)
            inv = inv + _bmm(inv, bdiag(p), P3)
        u = _bmm(inv, bdiag(vb), P3)
        w = _bmm(inv, bdiag(kbg), P3)
    else:
        u, w = _solve_given_inv(m, vb, kbg, inv, bd2)
    v_new = u - _bmm(w, s)
    k_tail = kn * jnp.exp(glast_l - gc_l)
    cdec = jnp.sum(exp_gt * jnp.exp(glast), axis=-1, keepdims=True)
    s_new = s * cdec + bd2 * _bmm_tn(k_tail, v_new)
    a = _bmm_nt(qs, kns) * dec
    o = _bmm(qs * eg, s) + _bmm(a, bdiag(v_new))
    return s_new, o, inv


def _gdn_finish(o_f, o_b, z, norm_g, bd):
    o = o_f + o_b
    ms = _head_sum(o * o, bd) * (1.0 / HD)
    return o * lax.rsqrt(ms + EPS) * norm_g * _silu(z)


def _gelu(x):
    return 0.5 * x * (1.0 + jnp.tanh(0.7978845608028654 * (x + 0.044715 * (x * x * x))))


def _sg_chunk(u, v, z, w, b, hm4):
    u = _gelu(u)
    gv = _gelu(v)
    mu = jnp.mean(gv, axis=-1, keepdims=True)
    xc = gv - mu
    var = jnp.mean(xc * xc, axis=-1, keepdims=True)
    vn = xc * lax.rsqrt(var + EPS)
    s = _dot_tn(b, hm4, precision=HI)
    for h in range(NH):
        s = s + _dot(w[h], vn) * hm4[h:h + 1]
    return u * s * _silu(z)


def _make_shifts(t_ctx, n):
    def dn(x):
        t = lax.broadcasted_iota(jnp.int32, x.shape, 0)
        return jnp.where((t != 0) & (t != t_ctx), pltpu.roll(x, 1, axis=0), 0.0)

    def up(x):
        t = lax.broadcasted_iota(jnp.int32, x.shape, 0)
        return jnp.where((t != t_ctx - 1) & (t != n - 1), pltpu.roll(x, n - 1, axis=0), 0.0)

    @jax.custom_vjp
    def shift_dn(x):
        return dn(x)
    shift_dn.defvjp(lambda x: (dn(x), None), lambda _, g: (up(g),))

    @jax.custom_vjp
    def shift_up(x):
        return up(x)
    shift_up.defvjp(lambda x: (up(x), None), lambda _, g: (dn(g),))
    return shift_dn, shift_up


def _conv3(x, w, shift_dn, shift_up):
    return shift_dn(x) * w[0:1] + x * w[1:2] + shift_up(x) * w[2:3]


def inproj_fwd(x, shift_t, scale_t, g_pre, w_in, n_batch, sb):
    n = x.shape[0]

    def sel(i):
        return jnp.where(i % sb == 0, n_batch, i // sb)

    def body(x_ref, sh_ref, sc_ref, g_ref, w_ref, p_ref, h_ref):
        xv = x_ref[...]
        r = xv * lax.rsqrt(jnp.mean(xv * xv, axis=-1, keepdims=True) + EPS)
        h = (r * g_ref[...]) * (1.0 + sc_ref[0]) + sh_ref[0]
        hb = h.astype(bf16)
        h_ref[...] = hb
        p_ref[...] = _dot(hb, w_ref[...])

    return pl.pallas_call(
        body, name="inproj_fwd", grid=(n // TM,),
        in_specs=[pl.BlockSpec((TM, D), lambda i: (i, 0)),
                  pl.BlockSpec((1, 1, D), lambda i: (sel(i), 0, 0)),
                  pl.BlockSpec((1, 1, D), lambda i: (sel(i), 0, 0)),
                  _full((1, D)), _full((D, PW))],
        out_specs=[pl.BlockSpec((TM, PW), lambda i: (i, 0)), pl.BlockSpec((TM, D), lambda i: (i, 0))],
        out_shape=[_sds((n, PW)), _sds((n, D), bf16)],
        compiler_params=_cparams(("arbitrary",), VMEM_BIG),
    )(x, shift_t, scale_t, g_pre, w_in)


def outproj_fwd(ys, w_out, x, gate_t, g_post, n_batch, sb):
    n = x.shape[0]

    def sel(i):
        return jnp.where(i % sb == 0, n_batch, i // sb)

    def body(y0, y1, y2, y3, w_ref, x_ref, gt_ref, g_ref, xn_ref, o_ref):
        y = jnp.concatenate([y0[...], y1[...], y2[...], y3[...]], axis=1)
        o = _dot(y, w_ref[...])
        o_ref[...] = o
        nrm = o * lax.rsqrt(jnp.mean(o * o, axis=-1, keepdims=True) + EPS) * g_ref[...]
        xn_ref[...] = x_ref[...] + gt_ref[0] * nrm

    yspec = pl.BlockSpec((TM, BW), lambda i: (i, 0))
    return pl.pallas_call(
        body, name="outproj_fwd", grid=(n // TM,),
        in_specs=[yspec, yspec, yspec, yspec, _full((D, D)), pl.BlockSpec((TM, D), lambda i: (i, 0)),
                  pl.BlockSpec((1, 1, D), lambda i: (sel(i), 0, 0)), _full((1, D))],
        out_specs=[pl.BlockSpec((TM, D), lambda i: (i, 0)), pl.BlockSpec((TM, D), lambda i: (i, 0))],
        out_shape=[_sds((n, D)), _sds((n, D))],
        compiler_params=_cparams(("arbitrary",)),
    )(*ys, w_out, x, gate_t, g_post)


def _row_onehot(r):
    return lax.broadcasted_iota(jnp.int32, (8, 1), 0) == r


def outproj_bwd(dxn, o, gate_t, g_post, ys, w_out_t, n_batch, sb):
    n = dxn.shape[0]

    def sel(i):
        return jnp.where(i % sb == 0, n_batch, i // sb)

    def body(dxn_ref, o_ref, gt_ref, g_ref, y0, y1, y2, y3, wt_ref, dy_ref, dw_ref, dg_ref, dgate_ref):
        i = pl.program_id(0)

        @pl.when(i == 0)
        def _():
            dw_ref[...] = jnp.zeros_like(dw_ref)
            dg_ref[...] = jnp.zeros_like(dg_ref)
            dgate_ref[...] = jnp.zeros_like(dgate_ref)

        ov = o_ref[...]
        rstd = lax.rsqrt(jnp.mean(ov * ov, axis=-1, keepdims=True) + EPS)
        r = ov * rstd
        g = g_ref[...]
        dx = dxn_ref[...]
        dgate_ref[...] += jnp.where(_row_onehot(sel(i)), jnp.sum(dx * (r * g), axis=0, keepdims=True), 0.0)
        dn = dx * gt_ref[0]
        dg_ref[...] += jnp.sum(dn * r, axis=0, keepdims=True)
        dr = dn * g
        do = rstd * (dr - r * jnp.mean(dr * r, axis=-1, keepdims=True))
        dob = do.astype(bf16)
        dy_ref[...] = _dot(dob, wt_ref[...])
        y = jnp.concatenate([y0[...], y1[...], y2[...], y3[...]], axis=1)
        dw_ref[...] += _dot_tn(y, dob)

    yspec = pl.BlockSpec((TM, BW), lambda i: (i, 0))
    row = pl.BlockSpec((TM, D), lambda i: (i, 0))
    return pl.pallas_call(
        body, name="outproj_bwd", grid=(n // TM,),
        in_specs=[row, row, pl.BlockSpec((1, 1, D), lambda i: (sel(i), 0, 0)), _full((1, D)),
                  yspec, yspec, yspec, yspec, _full((D, D))],
        out_specs=[row, _full((D, D)), _full((1, D)), _full((8, D))],
        out_shape=[_sds((n, D)), _sds((D, D)), _sds((1, D)), _sds((8, D))],
        compiler_params=_cparams(("arbitrary",), VMEM_BIG),
    )(dxn, o, gate_t, g_post, *ys, w_out_t)


def inproj_bwd_x(dp, w_in_t, x, shift_t, scale_t, g_pre, dxn, n_batch, sb):
    n = x.shape[0]

    def sel(i):
        return jnp.where(i % sb == 0, n_batch, i // sb)

    def body(dp_ref, wt_ref, x_ref, sc_ref, g_ref, dxn_ref, dx_ref, dg_ref, dsh_ref, dsc_ref):
        i = pl.program_id(0)

        @pl.when(i == 0)
        def _():
            dg_ref[...] = jnp.zeros_like(dg_ref)
            dsh_ref[...] = jnp.zeros_like(dsh_ref)
            dsc_ref[...] = jnp.zeros_like(dsc_ref)

        dh = _dot(dp_ref[...], wt_ref[...])
        xv = x_ref[...]
        rstd = lax.rsqrt(jnp.mean(xv * xv, axis=-1, keepdims=True) + EPS)
        r = xv * rstd
        g = g_ref[...]
        hot = _row_onehot(sel(i))
        dsh_ref[...] += jnp.where(hot, jnp.sum(dh, axis=0, keepdims=True), 0.0)
        dsc_ref[...] += jnp.where(hot, jnp.sum(dh * (r * g), axis=0, keepdims=True), 0.0)
        t = dh * (1.0 + sc_ref[0])
        dg_ref[...] += jnp.sum(t * r, axis=0, keepdims=True)
        dr = t * g
        dx_ref[...] = dxn_ref[...] + rstd * (dr - r * jnp.mean(dr * r, axis=-1, keepdims=True))

    row = pl.BlockSpec((TM, D), lambda i: (i, 0))
    return pl.pallas_call(
        body, name="inproj_bwd_x", grid=(n // TM,),
        in_specs=[pl.BlockSpec((TM, PW), lambda i: (i, 0)), _full((PW, D)), row,
                  pl.BlockSpec((1, 1, D), lambda i: (sel(i), 0, 0)), _full((1, D)), row],
        out_specs=[row, _full((1, D)), _full((8, D)), _full((8, D))],
        out_shape=[_sds((n, D)), _sds((1, D)), _sds((8, D)), _sds((8, D))],
        compiler_params=_cparams(("arbitrary",), VMEM_BIG),
    )(dp, w_in_t, x, scale_t, g_pre, dxn)


def dw_in(h, dp):
    n = h.shape[0]
    tk, tn = 512, 1024

    def body(h_ref, dp_ref, o_ref):
        @pl.when(pl.program_id(1) == 0)
        def _():
            o_ref[...] = jnp.zeros_like(o_ref)
        o_ref[...] += _dot_tn(h_ref[...], dp_ref[...])

    return pl.pallas_call(
        body, name="dw_in", grid=(PW // tn, n // tk),
        in_specs=[pl.BlockSpec((tk, D), lambda j, k: (k, 0)), pl.BlockSpec((tk, tn), lambda j, k: (k, j))],
        out_specs=pl.BlockSpec((D, tn), lambda j, k: (0, j)),
        out_shape=_sds((D, PW)),
        compiler_params=_cparams(("parallel", "arbitrary"), VMEM_BIG),
    )(h, dp)


def loss_head(xf, target, t_ctx):
    nb, s, _ = xf.shape
    jc = t_ctx // TM

    def body(x_ref, t_ref, dx_ref, l_ref):
        b, j = pl.program_id(0), pl.program_id(1)

        @pl.when((b == 0) & (j == 0))
        def _():
            l_ref[...] = jnp.zeros_like(l_ref)

        @pl.when(j < jc)
        def _():
            dx_ref[...] = jnp.zeros_like(dx_ref)

        @pl.when(j >= jc)
        def _():
            diff = x_ref[0] - t_ref[0]
            dx_ref[0] = diff * (1.0 / D)
            l_ref[...] += 0.5 * jnp.sum(diff * diff) * (1.0 / D)

    return pl.pallas_call(
        body, name="loss_head", grid=(nb, s // TM),
        in_specs=[pl.BlockSpec((1, TM, D), lambda b, j: (b, j, 0)),
                  pl.BlockSpec((1, TM, D), lambda b, j: (b, jnp.maximum(j - jc, 0), 0))],
        out_specs=[pl.BlockSpec((1, TM, D), lambda b, j: (b, j, 0)), _full((1, 128))],
        out_shape=[_sds((nb, s, D)), _sds((1, 128))],
        compiler_params=_cparams(("arbitrary", "arbitrary")),
    )(xf, target)


def _chunk_maps(n_ctx, n_lat):
    n = n_ctx + n_lat

    def cf(t):
        return t

    def cb(t):
        return jnp.where(t < n_ctx, n_ctx - 1 - t, n - 1 - t + n_ctx)
    return n, cf, cb


def ret_scan_fwd(p3, cos, sins, consts, t_ctx):
    nb, s, _ = p3.shape
    n, cf, cb = _chunk_maps(t_ctx // RC, (s - t_ctx) // RC)
    intra, qdec, kdec, cd, bd, hm = consts
    cmaps = (cf, cb)

    def body(qf, kf, vf, qb, kb, vb, cosf, sinf, cosb, sinb, intra_r, qdec_r, kdec_r, cd_r, bd_r, hm_r,
             of_ref, ob_ref, sall_ref, s_sc):
        @pl.when(pl.program_id(1) == 0)
        def _():
            s_sc[...] = jnp.zeros_like(s_sc)
        ins = ((qf, kf, vf, cosf, sinf, of_ref), (qb, kb, vb, cosb, sinb, ob_ref))
        for d, (q, k, v, c_, s_, o_ref) in enumerate(ins):
            st = s_sc[d]
            sall_ref[0, d, 0] = st
            s_new, o = _ret_step(st, q[0], k[0], v[0], c_[...], s_[...], intra_r[d], qdec_r[d], kdec_r[d],
                                 cd_r[...], bd_r[...], hm_r[...])
            s_sc[d] = s_new
            o_ref[0] = o

    def pspec(m, seg):
        return pl.BlockSpec((1, RC, BW), lambda b, t: (b, m(t), seg))

    def tspec(m):
        return pl.BlockSpec((RC, BW), lambda b, t: (m(t), 0))

    return pl.pallas_call(
        body, name="ret_scan_fwd", grid=(nb, n),
        in_specs=[pspec(cf, 0), pspec(cf, 1), pspec(cf, 2), pspec(cb, 0), pspec(cb, 1), pspec(cb, 2),
                  tspec(cf), tspec(cf), tspec(cb), tspec(cb),
                  _full(intra.shape), _full(qdec.shape), _full(kdec.shape), _full(cd.shape), _full(bd.shape),
                  _full(hm.shape)],
        out_specs=[pl.BlockSpec((1, RC, BW), lambda b, t: (b, cf(t), 0)),
                   pl.BlockSpec((1, RC, BW), lambda b, t: (b, cb(t), 0)),
                   pl.BlockSpec((1, 2, 1, BW, BW), lambda b, t: (b, 0, t, 0, 0))],
        out_shape=[_sds((nb, s, BW)), _sds((nb, s, BW)), _sds((nb, 2, n, BW, BW))],
        scratch_shapes=[pltpu.VMEM((2, BW, BW), f32)],
        compiler_params=_cparams(("arbitrary", "arbitrary")),
    )(p3, p3, p3, p3, p3, p3, cos, sins, cos, sins, intra, qdec, kdec, cd, bd, hm)


def ret_scan_bwd(p3, cos, sins, consts, s_all, do, t_ctx):
    nb, s, _ = p3.shape
    n, cf, cb = _chunk_maps(t_ctx // RC, (s - t_ctx) // RC)
    intra, qdec, kdec, cd, bd, hm = consts

    def rf(t):
        return cf(n - 1 - t)

    def rb(t):
        return cb(n - 1 - t)

    def body(qf, kf, vf, qb, kb, vb, cosf, sinf, cosb, sinb, intra_r, qdec_r, kdec_r, cd_r, bd_r, hm_r,
             sall_ref, dof, dob, dqf, dkf, dvf, dqb, dkb, dvb, ds_sc):
        @pl.when(pl.program_id(1) == 0)
        def _():
            ds_sc[...] = jnp.zeros_like(ds_sc)
        ins = ((qf, kf, vf, cosf, sinf, dof, (dqf, dkf, dvf)), (qb, kb, vb, cosb, sinb, dob, (dqb, dkb, dvb)))
        for d, (q, k, v, c_, s_, do_ref, outs) in enumerate(ins):
            step = functools.partial(_ret_step, cos=c_[...], sins=s_[...], intra=intra_r[d], qdec=qdec_r[d],
                                     kdec=kdec_r[d], cd=cd_r[...], bd=bd_r[...], hm=hm_r[...])
            _, vjp = jax.vjp(step, sall_ref[0, d, 0], q[0], k[0], v[0])
            ds, dq, dk, dv = vjp((ds_sc[d], do_ref[0]))
            ds_sc[d] = ds
            outs[0][0] = dq
            outs[1][0] = dk
            outs[2][0] = dv

    def pspec(m, seg):
        return pl.BlockSpec((1, RC, BW), lambda b, t: (b, m(t), seg))

    def tspec(m):
        return pl.BlockSpec((RC, BW), lambda b, t: (m(t), 0))

    def ospec(m):
        return pl.BlockSpec((1, RC, BW), lambda b, t: (b, m(t), 0))

    return pl.pallas_call(
        body, name="ret_scan_bwd", grid=(nb, n),
        in_specs=[pspec(rf, 0), pspec(rf, 1), pspec(rf, 2), pspec(rb, 0), pspec(rb, 1), pspec(rb, 2),
                  tspec(rf), tspec(rf), tspec(rb), tspec(rb),
                  _full(intra.shape), _full(qdec.shape), _full(kdec.shape), _full(cd.shape), _full(bd.shape),
                  _full(hm.shape),
                  pl.BlockSpec((1, 2, 1, BW, BW), lambda b, t: (b, 0, n - 1 - t, 0, 0)), ospec(rf), ospec(rb)],
        out_specs=[ospec(rf), ospec(rf), ospec(rf), ospec(rb), ospec(rb), ospec(rb)],
        out_shape=[_sds((nb, s, BW))] * 6,
        scratch_shapes=[pltpu.VMEM((2, BW, BW), f32)],
        compiler_params=_cparams(("arbitrary", "arbitrary")),
    )(p3, p3, p3, p3, p3, p3, cos, sins, cos, sins, intra, qdec, kdec, cd, bd, hm, s_all, do, do)


def mix_finish_fwd(fn, name, o_f, o_b, p3, zseg, norm_g, bd):
    nb, s, _ = p3.shape

    def body(of_ref, ob_ref, z_ref, g_ref, bd_ref, y_ref):
        y_ref[0] = fn(of_ref[0], ob_ref[0], z_ref[0], g_ref[...], bd_ref[...]).astype(bf16)

    blk = pl.BlockSpec((1, TM, BW), lambda b, j: (b, j, 0))
    return pl.pallas_call(
        body, name=name, grid=(nb, s // TM),
        in_specs=[blk, blk, pl.BlockSpec((1, TM, BW), lambda b, j: (b, j, zseg)), _full((1, BW)), _full((BW, BW))],
        out_specs=blk, out_shape=_sds((nb, s, BW), bf16),
        compiler_params=_cparams(("arbitrary", "arbitrary")),
    )(o_f, o_b, p3, norm_g, bd)


def mix_finish_bwd(fn, name, o_f, o_b, p3, zseg, norm_g, bd, dy3, yseg):
    nb, s, _ = p3.shape

    def body(of_ref, ob_ref, z_ref, g_ref, bd_ref, dy_ref, do_ref, dz_ref, dg_ref):
        @pl.when((pl.program_id(0) == 0) & (pl.program_id(1) == 0))
        def _():
            dg_ref[...] = jnp.zeros_like(dg_ref)
        bdv = bd_ref[...]
        _, vjp = jax.vjp(lambda a, b, z, g: fn(a, b, z, g, bdv), of_ref[0], ob_ref[0], z_ref[0], g_ref[...])
        do, _, dz, dg = vjp(dy_ref[0])
        do_ref[0] = do
        dz_ref[0] = dz
        dg_ref[...] += dg

    blk = pl.BlockSpec((1, TM, BW), lambda b, j: (b, j, 0))
    return pl.pallas_call(
        body, name=name, grid=(nb, s // TM),
        in_specs=[blk, blk, pl.BlockSpec((1, TM, BW), lambda b, j: (b, j, zseg)), _full((1, BW)), _full((BW, BW)),
                  pl.BlockSpec((1, TM, BW), lambda b, j: (b, j, yseg))],
        out_specs=[blk, blk, _full((1, BW))],
        out_shape=[_sds((nb, s, BW)), _sds((nb, s, BW)), _sds((1, BW))],
        compiler_params=_cparams(("arbitrary", "arbitrary")),
    )(o_f, o_b, p3, norm_g, bd, dy3)


def gdn_conv_fwd(p3, w, seg, t_ctx):
    nb, s, _ = p3.shape
    sd, su = _make_shifts(t_ctx, s)

    def body(x_ref, w_ref, o_ref):
        o_ref[0] = _silu(_conv3(x_ref[0], w_ref[...], sd, su))

    return pl.pallas_call(
        body, name="gdn_conv_fwd", grid=(nb, 2),
        in_specs=[pl.BlockSpec((1, s, 128), lambda b, j: (b, 0, 2 * seg + j)), pl.BlockSpec((3, 128), lambda b, j: (0, j))],
        out_specs=pl.BlockSpec((1, s, 128), lambda b, j: (b, 0, j)),
        out_shape=_sds((nb, s, BW)),
        compiler_params=_cparams(("arbitrary", "arbitrary")),
    )(p3, w)


def gdn_conv_bwd(p3, w, seg, d_f, d_b, t_ctx):
    nb, s, _ = p3.shape
    sd, su = _make_shifts(t_ctx, s)

    def body(x_ref, w_ref, df_ref, db_ref, dx_ref, dw_ref):
        @pl.when(pl.program_id(1) == 0)
        def _():
            dw_ref[...] = jnp.zeros_like(dw_ref)
        _, vjp = jax.vjp(lambda x, w_: _silu(_conv3(x, w_, sd, su)), x_ref[0], w_ref[...])
        dx, dw = vjp(df_ref[0] + db_ref[0])
        dx_ref[0] = dx
        dw_ref[...] += dw

    blk = pl.BlockSpec((1, s, 128), lambda j, b: (b, 0, j))
    return pl.pallas_call(
        body, name="gdn_conv_bwd", grid=(2, nb),
        in_specs=[pl.BlockSpec((1, s, 128), lambda j, b: (b, 0, 2 * seg + j)), pl.BlockSpec((3, 128), lambda j, b: (0, j)),
                  blk, blk],
        out_specs=[blk, pl.BlockSpec((3, 128), lambda j, b: (0, j))],
        out_shape=[_sds((nb, s, BW)), _sds((3, BW))],
        compiler_params=_cparams(("arbitrary", "arbitrary"), VMEM_BIG),
    )(p3, w, d_f, d_b)


def _pairs(f_ref, b_ref, nb):
    return jnp.stack([r[b, :, PAIR_W * p:PAIR_W * (p + 1)] for b in range(nb) for r in (f_ref, b_ref) for p in range(2)])


def _gates(f_ref, b_ref, nb):
    return jnp.stack([r[b] for b in range(nb) for r in (f_ref, b_ref)])


def _unpairs(a, f_ref, b_ref, nb):
    for b in range(nb):
        for d, r in enumerate((f_ref, b_ref)):
            for p in range(2):
                r[b, :, PAIR_W * p:PAIR_W * (p + 1)] = a[4 * b + 2 * d + p]


def gdn_scan_fwd(cq, ck, cv, p3, alog, dtb, consts, t_ctx):
    nb, s, _ = p3.shape
    n, cf, cb = _chunk_maps(t_ctx // GC, (s - t_ctx) // GC)
    gblk = GATE_COL // 128

    nz = 4 * nb

    def body(qf, kf, vf, gf, qb, kb, vb, gb, al_ref, dt_ref, tm_r, tm2_r, st2_r, eg_r, eb_r, egt_r, dsel_r, eye_r, bd_r,
             of_ref, ob_ref, sall_ref, inv_ref, s_sc):
        @pl.when(pl.program_id(0) == 0)
        def _():
            s_sc[...] = jnp.zeros_like(s_sc)
        st = s_sc[...]
        sall_ref[0] = st
        s_new, o, inv = _gdn_step(st, _pairs(qf, qb, nb), _pairs(kf, kb, nb), _pairs(vf, vb, nb), _gates(gf, gb, nb),
                                  al_ref[...], dt_ref[...], tm_r[...], tm2_r[...], st2_r[...], eg_r[...], eb_r[...],
                                  egt_r[...], dsel_r[...], eye_r[...], bd_r[...])
        s_sc[...] = s_new
        inv_ref[0] = inv
        _unpairs(o, of_ref, ob_ref, nb)

    def cspec(m):
        return pl.BlockSpec((nb, GC, BW), lambda t: (0, m(t), 0))

    def gspec(m):
        return pl.BlockSpec((nb, GC, 128), lambda t: (0, m(t), gblk))

    return pl.pallas_call(
        body, name="gdn_scan_fwd", grid=(n,),
        in_specs=[cspec(cf), cspec(cf), cspec(cf), gspec(cf), cspec(cb), cspec(cb), cspec(cb), gspec(cb),
                  _full((1, 128)), _full((1, 128))] + [_full(c.shape) for c in consts],
        out_specs=[cspec(cf), cspec(cb), pl.BlockSpec((1, nz, PAIR_W, PAIR_W), lambda t: (t, 0, 0, 0)),
                   pl.BlockSpec((1, nz, GC, PAIR_W), lambda t: (t, 0, 0, 0))],
        out_shape=[_sds((nb, s, BW)), _sds((nb, s, BW)), _sds((n, nz, PAIR_W, PAIR_W)), _sds((n, nz, GC, PAIR_W))],
        scratch_shapes=[pltpu.VMEM((nz, PAIR_W, PAIR_W), f32)],
        compiler_params=_cparams(("arbitrary",)),
    )(cq, ck, cv, p3, cq, ck, cv, p3, alog, dtb, *consts)


def gdn_scan_bwd(cq, ck, cv, p3, alog, dtb, consts, s_all, inv_all, do, t_ctx):
    nb, s, _ = p3.shape
    n, cf, cb = _chunk_maps(t_ctx // GC, (s - t_ctx) // GC)
    gblk = GATE_COL // 128

    def rf(t):
        return cf(n - 1 - t)

    def rb(t):
        return cb(n - 1 - t)

    nz = 4 * nb

    def body(qf, kf, vf, gf, qb, kb, vb, gb, al_ref, dt_ref, tm_r, tm2_r, st2_r, eg_r, eb_r, egt_r, dsel_r, eye_r, bd_r,
             sall_ref, inv_ref, dof, dob, dqf, dkf, dvf, dgf, dqb, dkb, dvb, dgb, dal_ref, ddt_ref, ds_sc):
        @pl.when(pl.program_id(0) == 0)
        def _():
            dal_ref[...] = jnp.zeros_like(dal_ref)
            ddt_ref[...] = jnp.zeros_like(ddt_ref)
            ds_sc[...] = jnp.zeros_like(ds_sc)
        consts = dict(tmask=tm_r[...], tmask2=tm2_r[...], strict2=st2_r[...], exp_g=eg_r[...], exp_b=eb_r[...],
                      exp_gt=egt_r[...], dsel2=dsel_r[...], eye2=eye_r[...], bd2=bd_r[...], inv=inv_ref[0])

        def step(*a):
            return _gdn_step(*a, **consts)[:2]

        _, vjp = jax.vjp(step, sall_ref[0], _pairs(qf, qb, nb), _pairs(kf, kb, nb), _pairs(vf, vb, nb),
                         _gates(gf, gb, nb), al_ref[...], dt_ref[...])
        ds, dq, dk, dv, dg, dal, ddt = vjp((ds_sc[...], _pairs(dof, dob, nb)))
        ds_sc[...] = ds
        _unpairs(dq, dqf, dqb, nb)
        _unpairs(dk, dkf, dkb, nb)
        _unpairs(dv, dvf, dvb, nb)
        for b in range(nb):
            dgf[b] = dg[2 * b]
            dgb[b] = dg[2 * b + 1]
        dal_ref[...] += dal
        ddt_ref[...] += ddt

    def cspec(m):
        return pl.BlockSpec((nb, GC, BW), lambda t: (0, m(t), 0))

    def gspec(m):
        return pl.BlockSpec((nb, GC, 128), lambda t: (0, m(t), gblk))

    def gout(m):
        return pl.BlockSpec((nb, GC, 128), lambda t: (0, m(t), 0))

    return pl.pallas_call(
        body, name="gdn_scan_bwd", grid=(n,),
        in_specs=[cspec(rf), cspec(rf), cspec(rf), gspec(rf), cspec(rb), cspec(rb), cspec(rb), gspec(rb),
                  _full((1, 128)), _full((1, 128))] + [_full(c.shape) for c in consts]
                 + [pl.BlockSpec((1, nz, PAIR_W, PAIR_W), lambda t: (n - 1 - t, 0, 0, 0)),
                    pl.BlockSpec((1, nz, GC, PAIR_W), lambda t: (n - 1 - t, 0, 0, 0)), cspec(rf), cspec(rb)],
        out_specs=[cspec(rf), cspec(rf), cspec(rf), gout(rf), cspec(rb), cspec(rb), cspec(rb), gout(rb),
                   _full((1, 128)), _full((1, 128))],
        out_shape=[_sds((nb, s, BW))] * 3 + [_sds((nb, s, 128))] + [_sds((nb, s, BW))] * 3 + [_sds((nb, s, 128))]
                  + [_sds((1, 128)), _sds((1, 128))],
        scratch_shapes=[pltpu.VMEM((nz, PAIR_W, PAIR_W), f32)],
        compiler_params=_cparams(("arbitrary",), VMEM_BIG),
    )(cq, ck, cv, p3, cq, ck, cv, p3, alog, dtb, *consts, s_all, inv_all, do, do)


def sg_fwd(p3, w, b, hm4):
    nb, s, _ = p3.shape

    def body(u_ref, v_ref, z_ref, w_ref, b_ref, hm_ref, y_ref):
        y_ref[0] = _sg_chunk(u_ref[0], v_ref[0], z_ref[0], w_ref[...], b_ref[...], hm_ref[...]).astype(bf16)

    def seg(k):
        return pl.BlockSpec((1, RC, BW), lambda bi, i: (bi, i, k))

    return pl.pallas_call(
        body, name="sg_fwd", grid=(nb, s // RC),
        in_specs=[seg(4), seg(5), seg(6), _full((NH, RC, RC)), _full((NH, RC)), _full((NH, BW))],
        out_specs=pl.BlockSpec((1, RC, BW), lambda bi, i: (bi, i, 0)),
        out_shape=_sds((nb, s, BW), bf16),
        compiler_params=_cparams(("arbitrary", "arbitrary")),
    )(p3, p3, p3, w, b, hm4)


def sg_bwd(p3, w, b, hm4, dy3):
    nb, s, _ = p3.shape

    def body(u_ref, v_ref, z_ref, w_ref, b_ref, hm_ref, dy_ref, du_ref, dv_ref, dz_ref, dw_ref, db_ref):
        @pl.when((pl.program_id(0) == 0) & (pl.program_id(1) == 0))
        def _():
            dw_ref[...] = jnp.zeros_like(dw_ref)
            db_ref[...] = jnp.zeros_like(db_ref)
        hm = hm_ref[...]
        _, vjp = jax.vjp(lambda u, v, z, w_, b_: _sg_chunk(u, v, z, w_, b_, hm),
                         u_ref[0], v_ref[0], z_ref[0], w_ref[...], b_ref[...])
        du, dv, dz, dw, db = vjp(dy_ref[0])
        du_ref[0] = du
        dv_ref[0] = dv
        dz_ref[0] = dz
        dw_ref[...] += dw
        db_ref[...] += db

    def seg(k):
        return pl.BlockSpec((1, RC, BW), lambda bi, i: (bi, i, k))

    blk = pl.BlockSpec((1, RC, BW), lambda bi, i: (bi, i, 0))
    return pl.pallas_call(
        body, name="sg_bwd", grid=(nb, s // RC),
        in_specs=[seg(4), seg(5), seg(6), _full((NH, RC, RC)), _full((NH, RC)), _full((NH, BW)), seg(1)],
        out_specs=[blk, blk, blk, _full((NH, RC, RC)), _full((NH, RC))],
        out_shape=[_sds((nb, s, BW))] * 3 + [_sds((NH, RC, RC)), _sds((NH, RC))],
        compiler_params=_cparams(("arbitrary", "arbitrary")),
    )(p3, p3, p3, w, b, hm4, dy3)


def _sc_fn(b, c, h, z, w, sd, su):
    return b * _conv3(c * h, w, sd, su) * _silu(z)


def sc_fwd(p3, w, t_ctx):
    nb, s, _ = p3.shape
    sd, su = _make_shifts(t_ctx, s)

    def body(b_ref, c_ref, h_ref, z_ref, w_ref, y_ref):
        y_ref[0] = _sc_fn(b_ref[0], c_ref[0], h_ref[0], z_ref[0], w_ref[...], sd, su).astype(bf16)

    def seg(k):
        return pl.BlockSpec((1, s, 128), lambda bi, j: (bi, 0, 2 * k + j))

    return pl.pallas_call(
        body, name="sc_fwd", grid=(nb, 2),
        in_specs=[seg(7), seg(8), seg(9), seg(10), pl.BlockSpec((3, 128), lambda bi, j: (0, j))],
        out_specs=pl.BlockSpec((1, s, 128), lambda bi, j: (bi, 0, j)),
        out_shape=_sds((nb, s, BW), bf16),
        compiler_params=_cparams(("arbitrary", "arbitrary"), VMEM_BIG),
    )(p3, p3, p3, p3, w)


def sc_bwd(p3, w, dy3, t_ctx):
    nb, s, _ = p3.shape
    sd, su = _make_shifts(t_ctx, s)

    def body(b_ref, c_ref, h_ref, z_ref, w_ref, dy_ref, db_ref, dc_ref, dh_ref, dz_ref, dw_ref):
        @pl.when(pl.program_id(1) == 0)
        def _():
            dw_ref[...] = jnp.zeros_like(dw_ref)
        _, vjp = jax.vjp(lambda b, c, h, z, w_: _sc_fn(b, c, h, z, w_, sd, su),
                         b_ref[0], c_ref[0], h_ref[0], z_ref[0], w_ref[...])
        db, dc, dh, dz, dw = vjp(dy_ref[0])
        db_ref[0] = db
        dc_ref[0] = dc
        dh_ref[0] = dh
        dz_ref[0] = dz
        dw_ref[...] += dw

    def seg(k):
        return pl.BlockSpec((1, s, 128), lambda j, bi: (bi, 0, 2 * k + j))

    blk = pl.BlockSpec((1, s, 128), lambda j, bi: (bi, 0, j))
    wspec = pl.BlockSpec((3, 128), lambda j, bi: (0, j))
    return pl.pallas_call(
        body, name="sc_bwd", grid=(2, nb),
        in_specs=[seg(7), seg(8), seg(9), seg(10), wspec, seg(2)],
        out_specs=[blk, blk, blk, blk, wspec],
        out_shape=[_sds((nb, s, BW))] * 4 + [_sds((3, BW))],
        compiler_params=_cparams(("arbitrary", "arbitrary"), VMEM_BIG),
    )(p3, p3, p3, p3, w, dy3)


def assemble_dp(pairs, singles_a, gdn_x, singles_b, gates):
    nb, s, _ = singles_a[0].shape
    flat = [a for pr in pairs for a in pr] + list(singles_a) + list(gdn_x) + list(singles_b) + list(gates)
    n_pairs, n_a, n_x, n_b = len(pairs), len(singles_a), len(gdn_x), len(singles_b)

    def body(*refs):
        out = refs[-1]
        ins = refs[:-1]
        col = 0
        for p in range(n_pairs):
            out[0, :, col:col + BW] = (ins[2 * p][0] + ins[2 * p + 1][0]).astype(bf16)
            col += BW
        k = 2 * n_pairs
        for _ in range(n_a + n_x + n_b):
            out[0, :, col:col + BW] = ins[k][0].astype(bf16)
            col += BW
            k += 1
        out[0, :, col:col + 128] = (ins[k][0] + ins[k + 1][0]).astype(bf16)
        out[0, :, col + 128:] = jnp.zeros((TM, PW - col - 128), bf16)

    def spec(a):
        return pl.BlockSpec((1, TM, a.shape[-1]), lambda b, j: (b, j, 0))

    return pl.pallas_call(
        body, name="assemble_dp", grid=(nb, s // TM),
        in_specs=[spec(a) for a in flat],
        out_specs=pl.BlockSpec((1, TM, PW), lambda b, j: (b, j, 0)),
        out_shape=_sds((nb, s, PW), bf16),
        compiler_params=_cparams(("arbitrary", "arbitrary")),
    )(*flat)


def mod_fwd(c_rows, w_mod, b_cols):
    nl, _, wc = w_mod.shape
    nr = c_rows.shape[0]

    def body(c_ref, w_ref, b_ref, o_ref):
        o_ref[0] = _dot(_silu(c_ref[...]), w_ref[0], precision=HI) + b_ref[0]

    return pl.pallas_call(
        body, name="mod_fwd", grid=(nl,),
        in_specs=[_full((nr, D)), pl.BlockSpec((1, D, wc), lambda l: (l, 0, 0)), pl.BlockSpec((1, 1, wc), lambda l: (l, 0, 0))],
        out_specs=pl.BlockSpec((1, nr, wc), lambda l: (l, 0, 0)),
        out_shape=_sds((nl, nr, wc)),
        compiler_params=_cparams(("arbitrary",)),
    )(c_rows, w_mod, b_cols)


def mod_bwd(c_rows, w_mod, dm_cols, dm_full):
    nl, _, wc = w_mod.shape
    nr = c_rows.shape[0]

    def body(c_ref, w_ref, dmc_ref, dmf_ref, gw_ref, gb_ref, dcc_ref):
        @pl.when(pl.program_id(0) == 0)
        def _():
            dcc_ref[...] = jnp.zeros_like(dcc_ref)
        a = _silu(c_ref[...])
        dmc = dmc_ref[0]
        gw_ref[0] = _dot_tn(a, dmc, precision=HI)
        gb_ref[0] = jnp.sum(dmf_ref[0], axis=0, keepdims=True)
        dcc_ref[...] += _dot_nt(dmc[nr - 8:nr], w_ref[0], precision=HI)

    return pl.pallas_call(
        body, name="mod_bwd", grid=(nl,),
        in_specs=[_full((nr, D)), pl.BlockSpec((1, D, wc), lambda l: (l, 0, 0)),
                  pl.BlockSpec((1, nr, wc), lambda l: (l, 0, 0)), pl.BlockSpec((1, nr, 3 * D), lambda l: (l, 0, 0))],
        out_specs=[pl.BlockSpec((1, D, wc), lambda l: (l, 0, 0)), pl.BlockSpec((1, 1, 3 * D), lambda l: (l, 0, 0)),
                   _full((8, D))],
        out_shape=[_sds((nl, D, wc)), _sds((nl, 1, 3 * D)), _sds((8, D))],
        compiler_params=_cparams(("arbitrary",)),
    )(c_rows, w_mod, dm_cols, dm_full)


def cctx_grad(parts, c_ctx):
    def body(p_ref, c_ref, o_ref):
        tot = p_ref[0, 0:1, :]
        for k in (2, 4, 6):
            tot = tot + p_ref[k, 0:1, :]
        c = c_ref[...]
        sg = jax.nn.sigmoid(c)
        o_ref[...] = tot * (sg * (1.0 + c * (1.0 - sg)))

    return pl.pallas_call(body, name="cctx_grad", out_shape=_sds((1, D)))(parts, c_ctx)


def sum_lead(x, out_dtype=f32, tr=256):
    k, r, c = x.shape
    tr = min(tr, r)
    assert r % tr == 0

    def body(x_ref, o_ref):
        tot = x_ref[0].astype(f32)
        for i in range(1, k):
            tot = tot + x_ref[i].astype(f32)
        o_ref[...] = tot.astype(out_dtype)

    return pl.pallas_call(
        body, name="sum_lead", grid=(r // tr,),
        in_specs=[pl.BlockSpec((k, tr, c), lambda i: (0, i, 0))],
        out_specs=pl.BlockSpec((tr, c), lambda i: (i, 0)),
        out_shape=_sds((r, c), out_dtype),
        compiler_params=_cparams(("arbitrary",)),
    )(x)


def adamw(w, m, v, g1, g2=None, tr=256):
    r, c = w.shape
    tr = min(tr, r)
    assert r % tr == 0
    two = g2 is not None
    c1 = 1.0 / (1.0 - ADAM_B1 ** ADAM_STEP)
    c2 = 1.0 / (1.0 - ADAM_B2 ** ADAM_STEP)

    def body(*refs):
        w_ref, m_ref, v_ref, g_ref = refs[:4]
        g = g_ref[...]
        if two:
            g = g + refs[4][...]
        go_ref, d_ref, mo_ref, vo_ref = refs[-4:]
        mn = ADAM_B1 * m_ref[...] + (1.0 - ADAM_B1) * g
        vn = ADAM_B2 * v_ref[...] + (1.0 - ADAM_B2) * (g * g)
        go_ref[...] = g
        mo_ref[...] = mn
        vo_ref[...] = vn
        d_ref[...] = -ADAM_LR * ((mn * c1) / (jnp.sqrt(vn * c2) + ADAM_EPS) + ADAM_WD * w_ref[...])

    blk = pl.BlockSpec((tr, c), lambda i: (i, 0))
    args = [w, m, v, g1] + ([g2] if two else [])
    return pl.pallas_call(
        body, name="adamw", grid=(r // tr,),
        in_specs=[blk] * len(args), out_specs=[blk] * 4, out_shape=[_sds((r, c))] * 4,
        compiler_params=_cparams(("arbitrary",)),
    )(*args)


def _my_pos():
    return lax.axis_index("x"), lax.axis_index("y"), lax.axis_index("c")


def gather8(x):
    shape = x.shape

    def body(x_ref, out_ref, send_sems, recv_sems, local_sem):
        mx, my, mc = _my_pos()
        me = 4 * mx + 2 * my + mc
        mine = pltpu.make_async_copy(x_ref, out_ref.at[me], local_sem)
        mine.start()
        copies = []
        for k in range(1, N_DEV):
            peer = (mx ^ (k >> 2), my ^ ((k >> 1) & 1), mc ^ (k & 1))
            cp = pltpu.make_async_remote_copy(src_ref=x_ref, dst_ref=out_ref.at[me], send_sem=send_sems.at[k - 1],
                                              recv_sem=recv_sems.at[k - 1], device_id=peer, device_id_type=MESH)
            cp.start()
            copies.append(cp)
        for k in range(1, N_DEV):
            src = me ^ k
            pltpu.make_async_remote_copy(src_ref=x_ref, dst_ref=out_ref.at[src], send_sem=send_sems.at[k - 1],
                                         recv_sem=recv_sems.at[k - 1], device_id=(mx, my, mc),
                                         device_id_type=MESH).wait_recv()
        for cp in copies:
            cp.wait_send()
        mine.wait()

    return pl.pallas_call(
        body, name="gather8", out_shape=_sds((N_DEV,) + shape, x.dtype),
        in_specs=[pl.BlockSpec(memory_space=pl.ANY)], out_specs=pl.BlockSpec(memory_space=pl.ANY),
        scratch_shapes=[pltpu.SemaphoreType.DMA((N_DEV - 1,)), pltpu.SemaphoreType.DMA((N_DEV - 1,)),
                        pltpu.SemaphoreType.DMA(())],
    )(x)


def gather4(x):
    shape = x.shape

    def body(x_ref, out_ref, send_sems, recv_sems, local_sem):
        mx, my, mc = _my_pos()
        me = 2 * mx + my
        mine = pltpu.make_async_copy(x_ref, out_ref.at[me], local_sem)
        mine.start()
        copies = []
        for k in range(1, N_CHIPS):
            peer = (mx ^ (k >> 1), my ^ (k & 1), mc)
            cp = pltpu.make_async_remote_copy(src_ref=x_ref, dst_ref=out_ref.at[me], send_sem=send_sems.at[k - 1],
                                              recv_sem=recv_sems.at[k - 1], device_id=peer, device_id_type=MESH)
            cp.start()
            copies.append(cp)
        for k in range(1, N_CHIPS):
            src = me ^ k
            pltpu.make_async_remote_copy(src_ref=x_ref, dst_ref=out_ref.at[src], send_sem=send_sems.at[k - 1],
                                         recv_sem=recv_sems.at[k - 1], device_id=(mx, my, mc),
                                         device_id_type=MESH).wait_recv()
        for cp in copies:
            cp.wait_send()
        mine.wait()

    return pl.pallas_call(
        body, name="gather4", out_shape=_sds((N_CHIPS,) + shape, x.dtype),
        in_specs=[pl.BlockSpec(memory_space=pl.ANY)], out_specs=pl.BlockSpec(memory_space=pl.ANY),
        scratch_shapes=[pltpu.SemaphoreType.DMA((N_CHIPS - 1,)), pltpu.SemaphoreType.DMA((N_CHIPS - 1,)),
                        pltpu.SemaphoreType.DMA(())],
    )(x)


def scatter4(g):
    shape = g.shape[1:]

    def body(g_ref, out_ref, send_sems, recv_sems, local_sem):
        mx, my, mc = _my_pos()
        me = 2 * mx + my
        mine = pltpu.make_async_copy(g_ref.at[me], out_ref.at[me], local_sem)
        mine.start()
        copies = []
        for k in range(1, N_CHIPS):
            peer = (mx ^ (k >> 1), my ^ (k & 1), mc)
            cp = pltpu.make_async_remote_copy(src_ref=g_ref.at[me ^ k], dst_ref=out_ref.at[me], send_sem=send_sems.at[k - 1],
                                              recv_sem=recv_sems.at[k - 1], device_id=peer, device_id_type=MESH)
            cp.start()
            copies.append(cp)
        for k in range(1, N_CHIPS):
            src = me ^ k
            pltpu.make_async_remote_copy(src_ref=g_ref.at[src], dst_ref=out_ref.at[src], send_sem=send_sems.at[k - 1],
                                         recv_sem=recv_sems.at[k - 1], device_id=(mx, my, mc),
                                         device_id_type=MESH).wait_recv()
        for cp in copies:
            cp.wait_send()
        mine.wait()

    return pl.pallas_call(
        body, name="scatter4", out_shape=_sds((N_CHIPS,) + shape, g.dtype),
        in_specs=[pl.BlockSpec(memory_space=pl.ANY)], out_specs=pl.BlockSpec(memory_space=pl.ANY),
        scratch_shapes=[pltpu.SemaphoreType.DMA((N_CHIPS - 1,)), pltpu.SemaphoreType.DMA((N_CHIPS - 1,)),
                        pltpu.SemaphoreType.DMA(())],
    )(g)


def swap_sibling(x):
    def body(x_ref, out_ref, send_sem, recv_sem):
        mx, my, mc = _my_pos()
        cp = pltpu.make_async_remote_copy(src_ref=x_ref, dst_ref=out_ref, send_sem=send_sem, recv_sem=recv_sem,
                                          device_id=(mx, my, 1 - mc), device_id_type=MESH)
        cp.start()
        cp.wait()

    return pl.pallas_call(
        body, name="swap_sibling", out_shape=_sds(x.shape, x.dtype),
        in_specs=[pl.BlockSpec(memory_space=pl.ANY)], out_specs=pl.BlockSpec(memory_space=pl.ANY),
        scratch_shapes=[pltpu.SemaphoreType.DMA(()), pltpu.SemaphoreType.DMA(())],
    )(x)


PACK_ROWS = 64
SMALL = ("c_ctx", "b_mod", "g_pre", "g_post", "ret_norm_g", "sg_w", "sg_b", "sc_conv_w", "gdn_conv_w",
         "gdn_a_log", "gdn_dt_bias", "gdn_norm_g")


def _pack(arrs, width=D):
    rows = []
    for a in arrs:
        flat = a.reshape(-1)
        pad = (-flat.shape[0]) % width
        rows.append(jnp.pad(flat, (0, pad)).reshape(-1, width))
    out = jnp.concatenate(rows, axis=0)
    return jnp.pad(out, ((0, (-out.shape[0]) % PACK_ROWS), (0, 0)))


def _unpack(packed, shapes, width=D):
    outs, r = [], 0
    for shp in shapes:
        size = int(np.prod(shp))
        nr = -(-size // width)
        outs.append(packed[r:r + nr].reshape(-1)[:size].reshape(shp))
        r += nr
    return outs


def kernel(x, c, ctx, c_ctx, w_mod, b_mod, g_pre, g_post, w_in, w_out, ret_norm_g, sg_w, sg_b, sc_conv_w, gdn_conv_w, gdn_a_log, gdn_dt_bias, gdn_norm_g, loss_target, m_c_ctx, m_w_mod, m_b_mod, m_g_pre, m_g_post, m_w_in, m_w_out, m_ret_norm_g, m_sg_w, m_sg_b, m_sc_conv_w, m_gdn_conv_w, m_gdn_a_log, m_gdn_dt_bias, m_gdn_norm_g, v_c_ctx, v_w_mod, v_b_mod, v_g_pre, v_g_post, v_w_in, v_w_out, v_ret_norm_g, v_sg_w, v_sg_b, v_sc_conv_w, v_gdn_conv_w, v_gdn_a_log, v_gdn_dt_bias, v_gdn_norm_g):
    weights = dict(c_ctx=c_ctx, w_mod=w_mod, b_mod=b_mod, g_pre=g_pre, g_post=g_post, w_in=w_in, w_out=w_out,
                   ret_norm_g=ret_norm_g, sg_w=sg_w, sg_b=sg_b, sc_conv_w=sc_conv_w, gdn_conv_w=gdn_conv_w,
                   gdn_a_log=gdn_a_log, gdn_dt_bias=gdn_dt_bias, gdn_norm_g=gdn_norm_g)
    mom = dict(c_ctx=m_c_ctx, w_mod=m_w_mod, b_mod=m_b_mod, g_pre=m_g_pre, g_post=m_g_post, w_in=m_w_in,
               w_out=m_w_out, ret_norm_g=m_ret_norm_g, sg_w=m_sg_w, sg_b=m_sg_b, sc_conv_w=m_sc_conv_w,
               gdn_conv_w=m_gdn_conv_w, gdn_a_log=m_gdn_a_log, gdn_dt_bias=m_gdn_dt_bias, gdn_norm_g=m_gdn_norm_g)
    var = dict(c_ctx=v_c_ctx, w_mod=v_w_mod, b_mod=v_b_mod, g_pre=v_g_pre, g_post=v_g_post, w_in=v_w_in,
               w_out=v_w_out, ret_norm_g=v_ret_norm_g, sg_w=v_sg_w, sg_b=v_sg_b, sc_conv_w=v_sc_conv_w,
               gdn_conv_w=v_gdn_conv_w, gdn_a_log=v_gdn_a_log, gdn_dt_bias=v_gdn_dt_bias, gdn_norm_g=v_gdn_norm_g)

    nb, t_lat, _ = x.shape
    t_ctx = ctx.shape[1]
    s = t_ctx + t_lat
    n = nb * s
    sb = s // TM
    nl = w_in.shape[0]
    wc_in = w_in.shape[2]
    wc_mod = w_mod.shape[2]
    rows_out = w_out.shape[1]
    n_all = nb * N_DEV
    mx, my, mc = _my_pos()
    chip = 2 * mx + my
    dev = 2 * chip + mc

    hm = jnp.asarray(_head_masks())
    hm4 = hm[:, 0, :]
    bd = jnp.asarray(_block_diag())
    ret_c = _ret_consts() + [bd, hm]
    gdn_c = _gdn_consts(nb)
    cos, sins = _rope_tables(t_lat, t_ctx)

    pre = _pack([c, sc_conv_w, gdn_conv_w])
    pre_all = gather8(pre)
    c_parts, scw_parts, gcw_parts = [], [], []
    for k in range(N_DEV):
        ck, sk, gk = _unpack(pre_all[k], [c.shape, sc_conv_w.shape, gdn_conv_w.shape])
        c_parts.append(ck)
        if k % 2 == 0:
            scw_parts.append(sk)
            gcw_parts.append(gk)
    c_all = jnp.concatenate(c_parts, axis=0)
    sc_w_full = jnp.concatenate(scw_parts, axis=-1)
    gdn_w_full = jnp.concatenate(gcw_parts, axis=-1)
    c_rows = jnp.concatenate([c_all, c_ctx[None, :], jnp.zeros((7, D), f32)], axis=0)

    b_cols = lax.dynamic_slice_in_dim(b_mod, chip * wc_mod, wc_mod, axis=1)[:, None, :]
    mod_part = mod_fwd(c_rows, w_mod, b_cols)
    mod_all = gather8(mod_part)
    mod = jnp.concatenate([mod_all[2 * k] for k in range(N_CHIPS)], axis=-1)
    my_rows = jnp.concatenate([lax.dynamic_slice_in_dim(mod, dev * nb, nb, axis=1), mod[:, n_all:n_all + 1]], axis=1)
    shift_t = my_rows[:, :, None, 0:D]
    scale_t = my_rows[:, :, None, D:2 * D]
    gate_t = my_rows[:, :, None, 2 * D:3 * D]

    w_in_all = gather4(w_in.astype(bf16))
    w_in_full = jnp.concatenate([w_in_all[k] for k in range(N_CHIPS)], axis=-1)
    w_in_full = jnp.pad(w_in_full, ((0, 0), (0, 0), (0, PW - IN_W)))
    w_in_t = jnp.transpose(w_in_full, (0, 2, 1))
    w_out_all = gather4(w_out.astype(bf16))
    w_out_full = jnp.concatenate([w_out_all[k] for k in range(N_CHIPS)], axis=1)
    w_out_t = jnp.transpose(w_out_full, (0, 2, 1))

    alog = jnp.pad(gdn_a_log.reshape(nl, 1, 8), ((0, 0), (0, 0), (0, 120)))
    dtb = jnp.pad(gdn_dt_bias.reshape(nl, 1, 8), ((0, 0), (0, 0), (0, 120)))
    gdn_ng = jnp.tile(gdn_norm_g, (1, NH))[:, None, :]
    ret_ng = ret_norm_g[:, None, :]

    xs = jnp.concatenate([ctx, x], axis=1).reshape(n, D)
    saved = []
    for l in range(nl):
        p, h = inproj_fwd(xs, shift_t[l], scale_t[l], g_pre[l][None, :], w_in_full[l], nb, sb)
        p3 = p.reshape(nb, s, PW)
        ro_f, ro_b, rs_all = ret_scan_fwd(p3, cos, sins, ret_c, t_ctx)
        y_ret = mix_finish_fwd(_ret_finish, "ret_finish_fwd", ro_f, ro_b, p3, 3, ret_ng[l], bd)
        y_sg = sg_fwd(p3, sg_w[l], sg_b[l], hm4)
        y_sc = sc_fwd(p3, sc_w_full[l], t_ctx)
        cq, ck, cv = [gdn_conv_fwd(p3, gdn_w_full[l][:, BW * i:BW * (i + 1)], 11 + i, t_ctx) for i in range(3)]
        go_f, go_b, *gs_all = gdn_scan_fwd(cq, ck, cv, p3, alog[l], dtb[l], gdn_c, t_ctx)
        y_gdn = mix_finish_fwd(_gdn_finish, "gdn_finish_fwd", go_f, go_b, p3, 14, gdn_ng[l], bd)
        ys = [a.reshape(n, BW) for a in (y_ret, y_sg, y_sc, y_gdn)]
        x_new, o = outproj_fwd(ys, w_out_full[l], xs, gate_t[l], g_post[l][None, :], nb, sb)
        saved.append(dict(x=xs, h=h, p3=p3, ro=(ro_f, ro_b), rs=rs_all, c=(cq, ck, cv), go=(go_f, go_b), gs=gs_all,
                          ys=ys, o=o))
        xs = x_new

    dx3, loss_part = loss_head(xs.reshape(nb, s, D), loss_target, t_ctx)
    loss = lax.psum(loss_part[0, 0], ("x", "y", "c"))

    dxs = dx3.reshape(n, D)
    g_small = {k: [None] * nl for k in SMALL if k not in ("c_ctx", "b_mod")}
    dm_rows = [None] * nl
    gw_in = [None] * nl
    gw_out = [None] * nl
    for l in reversed(range(nl)):
        sv = saved[l]
        p3 = sv["p3"]
        dy, gw_out[l], dg_post, dgate = outproj_bwd(dxs, sv["o"], gate_t[l], g_post[l][None, :], sv["ys"], w_out_t[l], nb, sb)
        dy3 = dy.reshape(nb, s, D)
        r_do, r_dz, d_rng = mix_finish_bwd(_ret_finish, "ret_finish_bwd", *sv["ro"], p3, 3, ret_ng[l], bd, dy3, 0)
        r_d = ret_scan_bwd(p3, cos, sins, ret_c, sv["rs"], r_do, t_ctx)
        s_du, s_dv, s_dz, d_sgw, d_sgb = sg_bwd(p3, sg_w[l], sg_b[l], hm4, dy3)
        c_db, c_dc, c_dh, c_dz, d_scw = sc_bwd(p3, sc_w_full[l], dy3, t_ctx)
        g_do, g_dz, d_gng = mix_finish_bwd(_gdn_finish, "gdn_finish_bwd", *sv["go"], p3, 14, gdn_ng[l], bd, dy3, 3)
        g_d = gdn_scan_bwd(*sv["c"], p3, alog[l], dtb[l], gdn_c, *sv["gs"], g_do, t_ctx)
        gx, d_gcw = [], []
        for i in range(3):
            dxi, dwi = gdn_conv_bwd(p3, gdn_w_full[l][:, BW * i:BW * (i + 1)], 11 + i, g_d[i], g_d[4 + i], t_ctx)
            gx.append(dxi)
            d_gcw.append(dwi)
        dp3 = assemble_dp([(r_d[0], r_d[3]), (r_d[1], r_d[4]), (r_d[2], r_d[5])],
                          [r_dz, s_du, s_dv, s_dz, c_db, c_dc, c_dh, c_dz], gx, [g_dz], [g_d[3], g_d[7]])
        dp = dp3.reshape(n, PW)
        dxs, dg_pre, dshift, dscale = inproj_bwd_x(dp, w_in_t[l], sv["x"], shift_t[l], scale_t[l], g_pre[l][None, :], dxs, nb, sb)
        gw_in[l] = dw_in(sv["h"], dp)
        g_small["g_pre"][l] = dg_pre[0]
        g_small["g_post"][l] = dg_post[0]
        g_small["ret_norm_g"][l] = d_rng[0]
        g_small["sg_w"][l] = d_sgw
        g_small["sg_b"][l] = d_sgb
        g_small["sc_conv_w"][l] = d_scw
        g_small["gdn_conv_w"][l] = jnp.concatenate(d_gcw, axis=-1)
        g_small["gdn_a_log"][l] = g_d[8][0, :8].reshape(2, NH)
        g_small["gdn_dt_bias"][l] = g_d[9][0, :8].reshape(2, NH)
        g_small["gdn_norm_g"][l] = d_gng[0].reshape(NH, HD)
        dm_rows[l] = jnp.concatenate([dshift, dscale, dgate], axis=-1)[:nb + 1]
    grad_x = dxs.reshape(nb, s, D)[:, t_ctx:, :]

    g_small = {k: jnp.stack(v) for k, v in g_small.items()}
    dm_rows = jnp.stack(dm_rows)
    dm_slot = jnp.zeros((nl, n_all + 8, 3 * D), f32)
    dm_slot = lax.dynamic_update_slice_in_dim(dm_slot, dm_rows[:, :nb], dev * nb, axis=1)
    dm_slot = lax.dynamic_update_slice_in_dim(dm_slot, dm_rows[:, nb:], n_all, axis=1)
    names2 = [k for k in SMALL if k not in ("c_ctx", "b_mod")]
    pack2 = _pack([g_small[k] for k in names2] + [dm_slot])
    tot2 = sum_lead(gather8(pack2), tr=PACK_ROWS)
    outs2 = _unpack(tot2, [g_small[k].shape for k in names2] + [dm_slot.shape])
    grads = dict(zip(names2, outs2[:-1]))
    dm_all = outs2[-1]
    grads["gdn_norm_g"] = sum_lead(jnp.transpose(grads["gdn_norm_g"], (1, 0, 2)), tr=nl)
    for k in ("sc_conv_w", "gdn_conv_w"):
        wc = weights[k].shape[2]
        grads[k] = lax.dynamic_slice_in_dim(grads[k], chip * wc, wc, axis=2)

    dm_cols = lax.dynamic_slice_in_dim(dm_all, chip * wc_mod, wc_mod, axis=2)
    g_w_mod, g_b_mod, dcc_part = mod_bwd(c_rows, w_mod, dm_cols, dm_all)
    grads["b_mod"] = g_b_mod[:, 0, :]
    grads["c_ctx"] = cctx_grad(gather8(dcc_part), c_ctx[None, :])[0]

    gw_in = jnp.stack(gw_in)[:, :, :IN_W].reshape(nl, D, N_CHIPS, wc_in)
    gw_in = jnp.transpose(gw_in, (2, 0, 1, 3)).astype(bf16).reshape(N_CHIPS, nl * D, wc_in)
    gin_mine = sum_lead(scatter4(gw_in))
    gin_sib = swap_sibling(gin_mine)
    gw_out = jnp.stack(gw_out).reshape(nl, N_CHIPS, rows_out, D)
    gw_out = jnp.transpose(gw_out, (1, 0, 2, 3)).astype(bf16).reshape(N_CHIPS, nl * rows_out, D)
    gout_mine = sum_lead(scatter4(gw_out))
    gout_sib = swap_sibling(gout_mine)

    res = {}
    res["w_in"] = [a.reshape(w_in.shape) for a in adamw(w_in.reshape(nl * D, wc_in), m_w_in.reshape(nl * D, wc_in),
                                                          v_w_in.reshape(nl * D, wc_in), gin_mine, gin_sib)]
    res["w_out"] = [a.reshape(w_out.shape) for a in adamw(w_out.reshape(nl * rows_out, D), m_w_out.reshape(nl * rows_out, D),
                                                            v_w_out.reshape(nl * rows_out, D), gout_mine, gout_sib)]
    res["w_mod"] = [a.reshape(w_mod.shape) for a in adamw(w_mod.reshape(nl * D, wc_mod), m_w_mod.reshape(nl * D, wc_mod),
                                                            v_w_mod.reshape(nl * D, wc_mod), g_w_mod.reshape(nl * D, wc_mod))]
    shapes = [weights[k].shape for k in SMALL]
    small = adamw(_pack([weights[k] for k in SMALL]), _pack([mom[k] for k in SMALL]), _pack([var[k] for k in SMALL]),
                  _pack([grads[k].reshape(weights[k].shape) for k in SMALL]), tr=PACK_ROWS)
    small = [_unpack(a, shapes) for a in small]
    for i, k in enumerate(SMALL):
        res[k] = [small[j][i] for j in range(4)]

    order = ["c_ctx", "w_mod", "b_mod", "g_pre", "g_post", "w_in", "w_out", "ret_norm_g", "sg_w", "sg_b", "sc_conv_w",
             "gdn_conv_w", "gdn_a_log", "gdn_dt_bias", "gdn_norm_g"]
    return (loss, grad_x, *[res[k][0] for k in order], *[res[k][1] for k in order], *[res[k][2] for k in order],
            *[res[k][3] for k in order])
```

```python
import functools

import jax
import jax.numpy as jnp
import numpy as np
from jax import lax
from jax.experimental import pallas as pl
from jax.experimental.pallas import tpu as pltpu

f32 = jnp.float32
bf16 = jnp.bfloat16
HI = lax.Precision.HIGHEST
P3 = lax.Precision.HIGH
MESH = pl.DeviceIdType.MESH

EPS = 1e-6
D = 1024
NH = 4
HD = 64
BW = NH * HD
PAIR_W = 2 * HD
RC = 128
GC = 64
GRID_W = 64
ROPE_BASE = 10000.0
IN_W = 15 * BW + 16
PW = 4096
GATE_COL = 15 * BW
N_CHIPS = 4
N_DEV = 8
TM = 256
TP = 2 * TM
ADAM_LR, ADAM_B1, ADAM_B2, ADAM_EPS, ADAM_WD, ADAM_STEP = 0.001, 0.9, 0.999, 1e-08, 0.01, 10
LANE_HEAD = np.arange(BW) // HD
VMEM_BIG = 56 * 1024 * 1024


def _dot(a, b, precision=None):
    return jnp.dot(a, b, precision=precision, preferred_element_type=f32)


def _dot_nt(a, b, precision=None):
    return lax.dot_general(a, b, (((1,), (1,)), ((), ())), precision=precision, preferred_element_type=f32)


def _dot_tn(a, b, precision=None):
    return lax.dot_general(a, b, (((0,), (0,)), ((), ())), precision=precision, preferred_element_type=f32)


def _sds(shape, dtype=f32):
    return jax.ShapeDtypeStruct(shape, dtype)


def _cparams(sem=None, vmem=None):
    kw = {}
    if sem is not None:
        kw["dimension_semantics"] = sem
    if vmem is not None:
        kw["vmem_limit_bytes"] = vmem
    return pltpu.CompilerParams(**kw)


def _full(shape):
    n = len(shape)
    return pl.BlockSpec(shape, lambda *_: (0,) * n)


def _head_masks():
    return np.stack([(LANE_HEAD == h).astype(np.float32)[None, :] for h in range(NH)])


def _block_diag():
    return (LANE_HEAD[:, None] == LANE_HEAD[None, :]).astype(np.float32)


def _tau(c, d):
    return np.arange(c) if d == 0 else c - 1 - np.arange(c)


def _ret_consts():
    lg = np.log(1.0 - 2.0 ** (-5.0 - np.arange(NH)))
    intra = np.zeros((2, NH, RC, RC)); qdec = np.zeros((2, RC, BW)); kdec = np.zeros((2, RC, BW))
    for d in range(2):
        t = _tau(RC, d)
        diff = t[:, None] - t[None, :]
        for h in range(NH):
            intra[d, h] = np.where(diff >= 0, np.exp(np.maximum(diff, 0) * lg[h]), 0.0)
        qdec[d] = np.exp((t[:, None] + 1.0) * lg[LANE_HEAD][None, :])
        kdec[d] = np.exp((RC - 1.0 - t[:, None]) * lg[LANE_HEAD][None, :])
    cd = np.exp(RC * lg[LANE_HEAD])[:, None] * np.ones((1, BW))
    return [jnp.asarray(a, f32) for a in (intra, qdec, kdec, cd)]


def _rope_tables(t_lat, t_ctx):
    nf = HD // 4
    inv = ROPE_BASE ** (-np.arange(nf) / nf)
    pos = np.arange(t_lat)
    ang_r = (pos // GRID_W)[:, None] * inv[None, :]
    ang_c = (pos % GRID_W)[:, None] * inv[None, :]
    ang = np.concatenate([ang_r, ang_r, ang_c, ang_c], axis=1)
    sign = np.concatenate([-np.ones(nf), np.ones(nf), -np.ones(nf), np.ones(nf)])
    cos = np.tile(np.cos(ang), (1, NH)); sins = np.tile(np.sin(ang) * sign, (1, NH))
    cos = np.concatenate([np.ones((t_ctx, BW)), cos]); sins = np.concatenate([np.zeros((t_ctx, BW)), sins])
    return jnp.asarray(cos, f32), jnp.asarray(sins, f32)


def _gdn_consts(nb):
    tmask = np.zeros((2, 2, GC, GC)); tmask2 = np.zeros((2, 2, GC, PAIR_W)); strict2 = np.zeros((2, 2, GC, PAIR_W))
    exp_g = np.zeros((2, 2, 128, PAIR_W)); exp_b = np.zeros((2, 2, 128, PAIR_W))
    for d in range(2):
        t = _tau(GC, d)
        tmask[d, :] = (t[:, None] >= t[None, :])
        tmask2[d, :] = np.tile(t[:, None] >= t[None, :], (1, 2))
        strict2[d, :] = np.tile(t[:, None] > t[None, :], (1, 2))
        for h in range(NH):
            exp_g[d, h // 2, 4 * d + h, (h % 2) * HD:(h % 2 + 1) * HD] = 1.0
            exp_b[d, h // 2, 8 + 4 * d + h, (h % 2) * HD:(h % 2 + 1) * HD] = 1.0
    exp_gt = np.transpose(exp_g, (0, 1, 3, 2))
    per_z = [np.tile(a.reshape((4,) + a.shape[2:]), (nb, 1, 1)) for a in (tmask, tmask2, strict2, exp_g, exp_b, exp_gt)]
    dsel2 = np.tile(np.eye(GC), (1, 2))
    eye2 = np.tile(np.eye(GC), (1, 2))
    bd2 = (np.arange(PAIR_W)[:, None] // HD == np.arange(PAIR_W)[None, :] // HD)
    return [jnp.asarray(a, f32) for a in per_z + [dsel2, eye2, bd2]]


def _swap16(x):
    lane = lax.broadcasted_iota(jnp.int32, x.shape, x.ndim - 1)
    n = x.shape[-1]
    return jnp.where(lane % 32 < 16, pltpu.roll(x, n - 16, axis=x.ndim - 1), pltpu.roll(x, 16, axis=x.ndim - 1))


@jax.custom_vjp
def _rot(x, cos, sins):
    return x * cos + _swap16(x) * sins


def _rot_fwd(x, cos, sins):
    return _rot(x, cos, sins), (cos, sins)


def _rot_bwd(res, g):
    cos, sins = res
    return g * cos + _swap16(g * sins), jnp.zeros_like(cos), jnp.zeros_like(sins)


_rot.defvjp(_rot_fwd, _rot_bwd)


def _silu(z):
    return z * jax.nn.sigmoid(z)


def _head_sum(x, bd):
    return _dot(x, bd, precision=P3)


def _ret_step(s, q, k, v, cos, sins, intra, qdec, kdec, cd, bd, hm):
    qr = _rot(q, cos, sins)
    kr = _rot(k, cos, sins) * (HD ** -0.5)
    o = _dot(qr * qdec, s)
    for h in range(NH):
        sc = _dot_nt(qr * hm[h], kr) * intra[h]
        o = o + _dot(sc, v) * hm[h]
    s_new = s * cd + bd * _dot_tn(kr * kdec, v)
    return s_new, o


def _ret_finish(o_f, o_b, z, norm_g, bd):
    o = o_f + o_b
    mu = _head_sum(o, bd) * (1.0 / HD)
    xc = o - mu
    var = _head_sum(xc * xc, bd) * (1.0 / HD)
    return xc * lax.rsqrt(var + EPS) * norm_g * _silu(z)


def _softplus(x):
    return jnp.maximum(x, 0.0) + jnp.log(1.0 + jnp.exp(-jnp.abs(x)))


def _bmm(a, b, precision=None):
    return lax.dot_general(a, b, (((2,), (1,)), ((0,), (0,))), precision=precision, preferred_element_type=f32)


def _bmm_nt(a, b, precision=None):
    return lax.dot_general(a, b, (((2,), (2,)), ((0,), (0,))), precision=precision, preferred_element_type=f32)


def _bmm_tn(a, b, precision=None):
    return lax.dot_general(a, b, (((1,), (1,)), ((0,), (0,))), precision=precision, preferred_element_type=f32)


def _bdiag(x, bd2):
    return jnp.concatenate([x, x], axis=1) * bd2


@jax.custom_vjp
def _solve_given_inv(m, vb, kbg, inv, bd2):
    return _bmm(inv, _bdiag(vb, bd2), P3), _bmm(inv, _bdiag(kbg, bd2), P3)


def _solve_fwd(m, vb, kbg, inv, bd2):
    u, w = _solve_given_inv(m, vb, kbg, inv, bd2)
    return (u, w), (inv, u, w, bd2)


def _solve_bwd(res, cts):
    inv, u, w, bd2 = res
    du, dw = cts
    c = inv.shape[1]
    t = jnp.swapaxes(_bdiag(inv, bd2), 1, 2)
    inv_t = t[:, :c] + t[:, c:]
    dvb = _bmm(inv_t, _bdiag(du, bd2), P3)
    dkbg = _bmm(inv_t, _bdiag(dw, bd2), P3)
    dm = _bmm_nt(dvb, _bdiag(u, bd2), P3) + _bmm_nt(dkbg, _bdiag(w, bd2), P3)
    return dm, dvb, dkbg, jnp.zeros_like(inv), jnp.zeros_like(bd2)


_solve_given_inv.defvjp(_solve_fwd, _solve_bwd)


def _gdn_step(s, q, k, v, gate, alog, dtb, tmask, tmask2, strict2, exp_g, exp_b, exp_gt, dsel2, eye2, bd2, inv=None):
    z, c, w_ = q.shape
    ne = gate.shape[0]

    def per_pair(a):
        return jnp.broadcast_to(a[:, None], (ne, z // ne) + a.shape[1:]).reshape((z,) + a.shape[1:])

    def rows(a):
        return a.reshape(z * c, w_)

    def bdiag(x):
        return _bdiag(x, bd2)

    g = per_pair(-jnp.exp(alog) * _softplus(gate + dtb))
    beta = per_pair(jax.nn.sigmoid(gate))
    gl = _bmm(g, exp_g, P3)
    gc_l = _bmm(tmask, gl, P3)
    glast_l = jnp.sum(gl, axis=1, keepdims=True)
    glast = jnp.sum(g, axis=1, keepdims=True)
    beta_l = _bmm(beta, exp_b, P3)
    gc_r = jnp.sum(gc_l * dsel2, axis=1, keepdims=True)
    qn = q * lax.rsqrt(_dot(rows(q * q), bd2, P3).reshape(z, c, w_) + EPS)
    kn = k * lax.rsqrt(_dot(rows(k * k), bd2, P3).reshape(z, c, w_) + EPS)
    eg = jnp.exp(gc_l)
    kb = kn * beta_l
    vb = v * beta_l
    kbg = kb * eg
    qs = qn * (HD ** -0.5)
    dec = jnp.exp(jnp.where(tmask2 > 0, gc_l - gc_r, -1e30))
    kns = bdiag(kn)
    m = -(_bmm_nt(kb, kns) * dec * strict2)
    if inv is None:
        inv = eye2 + m
        p = m
        for _ in range(5):
            p = _bmm(p, bdiag(p), P3)
            inv = inv + _bmm(inv, bdiag(p), P3)
        u = _bmm(inv, bdiag(vb), P3)
        w = _bmm(inv, bdiag(kbg), P3)
    else:
        u, w = _solve_given_inv(m, vb, kbg, inv, bd2)
    v_new = u - _bmm(w, s)
    k_tail = kn * jnp.exp(glast_l - gc_l)
    cdec = jnp.sum(exp_gt * jnp.exp(glast), axis=-1, keepdims=True)
    s_new = s * cdec + bd2 * _bmm_tn(k_tail, v_new)
    a = _bmm_nt(qs, kns) * dec
    o = _bmm(qs * eg, s) + _bmm(a, bdiag(v_new))
    return s_new, o, inv


def _gdn_finish(o_f, o_b, z, norm_g, bd):
    o = o_f + o_b
    ms = _head_sum(o * o, bd) * (1.0 / HD)
    return o * lax.rsqrt(ms + EPS) * norm_g * _silu(z)


def _gelu(x):
    return 0.5 * x * (1.0 + jnp.tanh(0.7978845608028654 * (x + 0.044715 * (x * x * x))))


def _sg_chunk(u, v, z, w, b, hm4):
    u = _gelu(u)
    gv = _gelu(v)
    mu = jnp.mean(gv, axis=-1, keepdims=True)
    xc = gv - mu
    var = jnp.mean(xc * xc, axis=-1, keepdims=True)
    vn = xc * lax.rsqrt(var + EPS)
    s = _dot_tn(b, hm4, precision=HI)
    for h in range(NH):
        s = s + _dot(w[h], vn) * hm4[h:h + 1]
    return u * s * _silu(z)


def _make_shifts(t_ctx, n):
    def dn(x):
        t = lax.broadcasted_iota(jnp.int32, x.shape, 0)
        return jnp.where((t != 0) & (t != t_ctx), pltpu.roll(x, 1, axis=0), 0.0)

    def up(x):
        t = lax.broadcasted_iota(jnp.int32, x.shape, 0)
        return jnp.where((t != t_ctx - 1) & (t != n - 1), pltpu.roll(x, n - 1, axis=0), 0.0)

    @jax.custom_vjp
    def shift_dn(x):
        return dn(x)
    shift_dn.defvjp(lambda x: (dn(x), None), lambda _, g: (up(g),))

    @jax.custom_vjp
    def shift_up(x):
        return up(x)
    shift_up.defvjp(lambda x: (up(x), None), lambda _, g: (dn(g),))
    return shift_dn, shift_up


def _conv3(x, w, shift_dn, shift_up):
    return shift_dn(x) * w[0:1] + x * w[1:2] + shift_up(x) * w[2:3]


def inproj_fwd(x, shift_t, scale_t, g_pre, w_in, n_batch, sb):
    n = x.shape[0]

    def sel(i):
        return jnp.where(i % sb == 0, n_batch, i // sb)

    def body(x_ref, sh0, sh1, sc0, sc1, g_ref, w_ref, p_ref, h_ref):
        hs = []
        for k, (sh_ref, sc_ref) in enumerate(((sh0, sc0), (sh1, sc1))):
            xv = x_ref[k * TM:(k + 1) * TM, :]
            r = xv * lax.rsqrt(jnp.mean(xv * xv, axis=-1, keepdims=True) + EPS)
            hs.append(((r * g_ref[...]) * (1.0 + sc_ref[0]) + sh_ref[0]).astype(bf16))
        hb = jnp.concatenate(hs, axis=0)
        h_ref[...] = hb
        p_ref[...] = _dot(hb, w_ref[...])

    def mrow(k):
        return pl.BlockSpec((1, 1, D), lambda i: (sel(2 * i + k), 0, 0))

    return pl.pallas_call(
        body, name="inproj_fwd", grid=(n // TP,),
        in_specs=[pl.BlockSpec((TP, D), lambda i: (i, 0)), mrow(0), mrow(1), mrow(0), mrow(1),
                  _full((1, D)), _full((D, PW))],
        out_specs=[pl.BlockSpec((TP, PW), lambda i: (i, 0)), pl.BlockSpec((TP, D), lambda i: (i, 0))],
        out_shape=[_sds((n, PW)), _sds((n, D), bf16)],
        compiler_params=_cparams(("arbitrary",), VMEM_BIG),
    )(x, shift_t, shift_t, scale_t, scale_t, g_pre, w_in)


def outproj_fwd(ys, w_out, x, gate_t, g_post, n_batch, sb):
    n = x.shape[0]

    def sel(i):
        return jnp.where(i % sb == 0, n_batch, i // sb)

    def body(y0, y1, y2, y3, w_ref, x_ref, gt0, gt1, g_ref, xn_ref, o_ref):
        y = jnp.concatenate([y0[...], y1[...], y2[...], y3[...]], axis=1)
        o = _dot(y, w_ref[...])
        o_ref[...] = o
        nrm = o * lax.rsqrt(jnp.mean(o * o, axis=-1, keepdims=True) + EPS) * g_ref[...]
        for k, gt_ref in enumerate((gt0, gt1)):
            rows = slice(k * TM, (k + 1) * TM)
            xn_ref[rows, :] = x_ref[rows, :] + gt_ref[0] * nrm[rows]

    def mrow(k):
        return pl.BlockSpec((1, 1, D), lambda i: (sel(2 * i + k), 0, 0))

    yspec = pl.BlockSpec((TP, BW), lambda i: (i, 0))
    return pl.pallas_call(
        body, name="outproj_fwd", grid=(n // TP,),
        in_specs=[yspec, yspec, yspec, yspec, _full((D, D)), pl.BlockSpec((TP, D), lambda i: (i, 0)),
                  mrow(0), mrow(1), _full((1, D))],
        out_specs=[pl.BlockSpec((TP, D), lambda i: (i, 0)), pl.BlockSpec((TP, D), lambda i: (i, 0))],
        out_shape=[_sds((n, D)), _sds((n, D))],
        compiler_params=_cparams(("arbitrary",), VMEM_BIG),
    )(*ys, w_out, x, gate_t, gate_t, g_post)


def _row_onehot(r):
    return lax.broadcasted_iota(jnp.int32, (8, 1), 0) == r


def outproj_bwd(dxn, o, gate_t, g_post, ys, w_out, n_batch, sb):
    n = dxn.shape[0]

    def sel(i):
        return jnp.where(i % sb == 0, n_batch, i // sb)

    def body(dxn_ref, o_ref, gt0, gt1, g_ref, y0, y1, y2, y3, w_ref, dy_ref, dw_ref, dg_ref, dgate_ref):
        i = pl.program_id(0)

        @pl.when(i == 0)
        def _():
            dw_ref[...] = jnp.zeros_like(dw_ref)
            dg_ref[...] = jnp.zeros_like(dg_ref)
            dgate_ref[...] = jnp.zeros_like(dgate_ref)

        g = g_ref[...]
        dos = []
        for k, gt_ref in enumerate((gt0, gt1)):
            rows = slice(k * TM, (k + 1) * TM)
            ov = o_ref[rows, :]
            rstd = lax.rsqrt(jnp.mean(ov * ov, axis=-1, keepdims=True) + EPS)
            r = ov * rstd
            dx = dxn_ref[rows, :]
            dgate_ref[...] += jnp.where(_row_onehot(sel(2 * i + k)), jnp.sum(dx * (r * g), axis=0, keepdims=True), 0.0)
            dn = dx * gt_ref[0]
            dg_ref[...] += jnp.sum(dn * r, axis=0, keepdims=True)
            dr = dn * g
            dos.append((rstd * (dr - r * jnp.mean(dr * r, axis=-1, keepdims=True))).astype(bf16))
        dob = jnp.concatenate(dos, axis=0)
        dy_ref[...] = _dot_nt(dob, w_ref[...])
        y = jnp.concatenate([y0[...], y1[...], y2[...], y3[...]], axis=1)
        dw_ref[...] += _dot_tn(y, dob)

    def mrow(k):
        return pl.BlockSpec((1, 1, D), lambda i: (sel(2 * i + k), 0, 0))

    yspec = pl.BlockSpec((TP, BW), lambda i: (i, 0))
    row = pl.BlockSpec((TP, D), lambda i: (i, 0))
    return pl.pallas_call(
        body, name="outproj_bwd", grid=(n // TP,),
        in_specs=[row, row, mrow(0), mrow(1), _full((1, D)), yspec, yspec, yspec, yspec, _full((D, D))],
        out_specs=[row, _full((D, D)), _full((1, D)), _full((8, D))],
        out_shape=[_sds((n, D)), _sds((D, D)), _sds((1, D)), _sds((8, D))],
        compiler_params=_cparams(("arbitrary",), VMEM_BIG),
    )(dxn, o, gate_t, gate_t, g_post, *ys, w_out)


def inproj_bwd_x(dp, w_in, x, scale_t, g_pre, dxn, n_batch, sb):
    n = x.shape[0]

    def sel(i):
        return jnp.where(i % sb == 0, n_batch, i // sb)

    def body(dp_ref, w_ref, x_ref, sc0, sc1, g_ref, dxn_ref, dx_ref, dg_ref, dsh_ref, dsc_ref):
        i = pl.program_id(0)

        @pl.when(i == 0)
        def _():
            dg_ref[...] = jnp.zeros_like(dg_ref)
            dsh_ref[...] = jnp.zeros_like(dsh_ref)
            dsc_ref[...] = jnp.zeros_like(dsc_ref)

        dh_all = _dot_nt(dp_ref[...], w_ref[...])
        g = g_ref[...]
        for k, sc_ref in enumerate((sc0, sc1)):
            rows = slice(k * TM, (k + 1) * TM)
            dh = dh_all[rows]
            xv = x_ref[rows, :]
            rstd = lax.rsqrt(jnp.mean(xv * xv, axis=-1, keepdims=True) + EPS)
            r = xv * rstd
            hot = _row_onehot(sel(2 * i + k))
            dsh_ref[...] += jnp.where(hot, jnp.sum(dh, axis=0, keepdims=True), 0.0)
            dsc_ref[...] += jnp.where(hot, jnp.sum(dh * (r * g), axis=0, keepdims=True), 0.0)
            t = dh * (1.0 + sc_ref[0])
            dg_ref[...] += jnp.sum(t * r, axis=0, keepdims=True)
            dr = t * g
            dx_ref[rows, :] = dxn_ref[rows, :] + rstd * (dr - r * jnp.mean(dr * r, axis=-1, keepdims=True))

    def mrow(k):
        return pl.BlockSpec((1, 1, D), lambda i: (sel(2 * i + k), 0, 0))

    row = pl.BlockSpec((TP, D), lambda i: (i, 0))
    return pl.pallas_call(
        body, name="inproj_bwd_x", grid=(n // TP,),
        in_specs=[pl.BlockSpec((TP, PW), lambda i: (i, 0)), _full((D, PW)), row, mrow(0), mrow(1), _full((1, D)), row],
        out_specs=[row, _full((1, D)), _full((8, D)), _full((8, D))],
        out_shape=[_sds((n, D)), _sds((1, D)), _sds((8, D)), _sds((8, D))],
        compiler_params=_cparams(("arbitrary",), VMEM_BIG),
    )(dp, w_in, x, scale_t, scale_t, g_pre, dxn)


def dw_in(h, dp):
    n = h.shape[0]
    tk, tn = 512, 1024

    def body(h_ref, dp_ref, o_ref):
        @pl.when(pl.program_id(1) == 0)
        def _():
            o_ref[...] = jnp.zeros_like(o_ref)
        o_ref[...] += _dot_tn(h_ref[...], dp_ref[...])

    return pl.pallas_call(
        body, name="dw_in", grid=(PW // tn, n // tk),
        in_specs=[pl.BlockSpec((tk, D), lambda j, k: (k, 0)), pl.BlockSpec((tk, tn), lambda j, k: (k, j))],
        out_specs=pl.BlockSpec((D, tn), lambda j, k: (0, j)),
        out_shape=_sds((D, PW)),
        compiler_params=_cparams(("parallel", "arbitrary"), VMEM_BIG),
    )(h, dp)


def loss_head(xf, target, t_ctx):
    nb, s, _ = xf.shape
    jc = t_ctx // TM

    def body(x_ref, t_ref, dx_ref, l_ref):
        b, j = pl.program_id(0), pl.program_id(1)

        @pl.when((b == 0) & (j == 0))
        def _():
            l_ref[...] = jnp.zeros_like(l_ref)

        @pl.when(j < jc)
        def _():
            dx_ref[...] = jnp.zeros_like(dx_ref)

        @pl.when(j >= jc)
        def _():
            diff = x_ref[0] - t_ref[0]
            dx_ref[0] = diff * (1.0 / D)
            l_ref[...] += 0.5 * jnp.sum(diff * diff) * (1.0 / D)

    return pl.pallas_call(
        body, name="loss_head", grid=(nb, s // TM),
        in_specs=[pl.BlockSpec((1, TM, D), lambda b, j: (b, j, 0)),
                  pl.BlockSpec((1, TM, D), lambda b, j: (b, jnp.maximum(j - jc, 0), 0))],
        out_specs=[pl.BlockSpec((1, TM, D), lambda b, j: (b, j, 0)), _full((1, 128))],
        out_shape=[_sds((nb, s, D)), _sds((1, 128))],
        compiler_params=_cparams(("arbitrary", "arbitrary")),
    )(xf, target)


def _chunk_maps(n_ctx, n_lat):
    n = n_ctx + n_lat

    def cf(t):
        return t

    def cb(t):
        return jnp.where(t < n_ctx, n_ctx - 1 - t, n - 1 - t + n_ctx)
    return n, cf, cb


def ret_scan_fwd(p3, cos, sins, consts, t_ctx):
    nb, s, _ = p3.shape
    n, cf, cb = _chunk_maps(t_ctx // RC, (s - t_ctx) // RC)
    intra, qdec, kdec, cd, bd, hm = consts
    cmaps = (cf, cb)

    def body(qf, kf, vf, qb, kb, vb, cosf, sinf, cosb, sinb, intra_r, qdec_r, kdec_r, cd_r, bd_r, hm_r,
             of_ref, ob_ref, sall_ref, s_sc):
        @pl.when(pl.program_id(1) == 0)
        def _():
            s_sc[...] = jnp.zeros_like(s_sc)
        ins = ((qf, kf, vf, cosf, sinf, of_ref), (qb, kb, vb, cosb, sinb, ob_ref))
        for d, (q, k, v, c_, s_, o_ref) in enumerate(ins):
            st = s_sc[d]
            sall_ref[0, d, 0] = st
            s_new, o = _ret_step(st, q[0], k[0], v[0], c_[...], s_[...], intra_r[d], qdec_r[d], kdec_r[d],
                                 cd_r[...], bd_r[...], hm_r[...])
            s_sc[d] = s_new
            o_ref[0] = o

    def pspec(m, seg):
        return pl.BlockSpec((1, RC, BW), lambda b, t: (b, m(t), seg))

    def tspec(m):
        return pl.BlockSpec((RC, BW), lambda b, t: (m(t), 0))

    return pl.pallas_call(
        body, name="ret_scan_fwd", grid=(nb, n),
        in_specs=[pspec(cf, 0), pspec(cf, 1), pspec(cf, 2), pspec(cb, 0), pspec(cb, 1), pspec(cb, 2),
                  tspec(cf), tspec(cf), tspec(cb), tspec(cb),
                  _full(intra.shape), _full(qdec.shape), _full(kdec.shape), _full(cd.shape), _full(bd.shape),
                  _full(hm.shape)],
        out_specs=[pl.BlockSpec((1, RC, BW), lambda b, t: (b, cf(t), 0)),
                   pl.BlockSpec((1, RC, BW), lambda b, t: (b, cb(t), 0)),
                   pl.BlockSpec((1, 2, 1, BW, BW), lambda b, t: (b, 0, t, 0, 0))],
        out_shape=[_sds((nb, s, BW)), _sds((nb, s, BW)), _sds((nb, 2, n, BW, BW))],
        scratch_shapes=[pltpu.VMEM((2, BW, BW), f32)],
        compiler_params=_cparams(("arbitrary", "arbitrary")),
    )(p3, p3, p3, p3, p3, p3, cos, sins, cos, sins, intra, qdec, kdec, cd, bd, hm)


def ret_scan_bwd(p3, cos, sins, consts, s_all, do, t_ctx):
    nb, s, _ = p3.shape
    n, cf, cb = _chunk_maps(t_ctx // RC, (s - t_ctx) // RC)
    intra, qdec, kdec, cd, bd, hm = consts

    def rf(t):
        return cf(n - 1 - t)

    def rb(t):
        return cb(n - 1 - t)

    def body(qf, kf, vf, qb, kb, vb, cosf, sinf, cosb, sinb, intra_r, qdec_r, kdec_r, cd_r, bd_r, hm_r,
             sall_ref, dof, dob, dqf, dkf, dvf, dqb, dkb, dvb, ds_sc):
        @pl.when(pl.program_id(1) == 0)
        def _():
            ds_sc[...] = jnp.zeros_like(ds_sc)
        ins = ((qf, kf, vf, cosf, sinf, dof, (dqf, dkf, dvf)), (qb, kb, vb, cosb, sinb, dob, (dqb, dkb, dvb)))
        for d, (q, k, v, c_, s_, do_ref, outs) in enumerate(ins):
            step = functools.partial(_ret_step, cos=c_[...], sins=s_[...], intra=intra_r[d], qdec=qdec_r[d],
                                     kdec=kdec_r[d], cd=cd_r[...], bd=bd_r[...], hm=hm_r[...])
            _, vjp = jax.vjp(step, sall_ref[0, d, 0], q[0], k[0], v[0])
            ds, dq, dk, dv = vjp((ds_sc[d], do_ref[0]))
            ds_sc[d] = ds
            outs[0][0] = dq
            outs[1][0] = dk
            outs[2][0] = dv

    def pspec(m, seg):
        return pl.BlockSpec((1, RC, BW), lambda b, t: (b, m(t), seg))

    def tspec(m):
        return pl.BlockSpec((RC, BW), lambda b, t: (m(t), 0))

    def ospec(m):
        return pl.BlockSpec((1, RC, BW), lambda b, t: (b, m(t), 0))

    return pl.pallas_call(
        body, name="ret_scan_bwd", grid=(nb, n),
        in_specs=[pspec(rf, 0), pspec(rf, 1), pspec(rf, 2), pspec(rb, 0), pspec(rb, 1), pspec(rb, 2),
                  tspec(rf), tspec(rf), tspec(rb), tspec(rb),
                  _full(intra.shape), _full(qdec.shape), _full(kdec.shape), _full(cd.shape), _full(bd.shape),
                  _full(hm.shape),
                  pl.BlockSpec((1, 2, 1, BW, BW), lambda b, t: (b, 0, n - 1 - t, 0, 0)), ospec(rf), ospec(rb)],
        out_specs=[ospec(rf), ospec(rf), ospec(rf), ospec(rb), ospec(rb), ospec(rb)],
        out_shape=[_sds((nb, s, BW))] * 6,
        scratch_shapes=[pltpu.VMEM((2, BW, BW), f32)],
        compiler_params=_cparams(("arbitrary", "arbitrary")),
    )(p3, p3, p3, p3, p3, p3, cos, sins, cos, sins, intra, qdec, kdec, cd, bd, hm, s_all, do, do)


def mix_finish_fwd(fn, name, o_f, o_b, p3, zseg, norm_g, bd):
    nb, s, _ = p3.shape

    def body(of_ref, ob_ref, z_ref, g_ref, bd_ref, y_ref):
        y_ref[0] = fn(of_ref[0], ob_ref[0], z_ref[0], g_ref[...], bd_ref[...]).astype(bf16)

    blk = pl.BlockSpec((1, TM, BW), lambda b, j: (b, j, 0))
    return pl.pallas_call(
        body, name=name, grid=(nb, s // TM),
        in_specs=[blk, blk, pl.BlockSpec((1, TM, BW), lambda b, j: (b, j, zseg)), _full((1, BW)), _full((BW, BW))],
        out_specs=blk, out_shape=_sds((nb, s, BW), bf16),
        compiler_params=_cparams(("arbitrary", "arbitrary")),
    )(o_f, o_b, p3, norm_g, bd)


def mix_finish_bwd(fn, name, o_f, o_b, p3, zseg, norm_g, bd, dy3, yseg):
    nb, s, _ = p3.shape

    def body(of_ref, ob_ref, z_ref, g_ref, bd_ref, dy_ref, do_ref, dz_ref, dg_ref):
        @pl.when((pl.program_id(0) == 0) & (pl.program_id(1) == 0))
        def _():
            dg_ref[...] = jnp.zeros_like(dg_ref)
        bdv = bd_ref[...]
        _, vjp = jax.vjp(lambda a, b, z, g: fn(a, b, z, g, bdv), of_ref[0], ob_ref[0], z_ref[0], g_ref[...])
        do, _, dz, dg = vjp(dy_ref[0])
        do_ref[0] = do
        dz_ref[0] = dz
        dg_ref[...] += dg

    blk = pl.BlockSpec((1, TM, BW), lambda b, j: (b, j, 0))
    return pl.pallas_call(
        body, name=name, grid=(nb, s // TM),
        in_specs=[blk, blk, pl.BlockSpec((1, TM, BW), lambda b, j: (b, j, zseg)), _full((1, BW)), _full((BW, BW)),
                  pl.BlockSpec((1, TM, BW), lambda b, j: (b, j, yseg))],
        out_specs=[blk, blk, _full((1, BW))],
        out_shape=[_sds((nb, s, BW)), _sds((nb, s, BW)), _sds((1, BW))],
        compiler_params=_cparams(("arbitrary", "arbitrary")),
    )(o_f, o_b, p3, norm_g, bd, dy3)


def gdn_conv_fwd(p3, w, seg, t_ctx):
    nb, s, _ = p3.shape
    sd, su = _make_shifts(t_ctx, s)

    def body(x_ref, w_ref, o_ref):
        o_ref[0] = _silu(_conv3(x_ref[0], w_ref[...], sd, su))

    return pl.pallas_call(
        body, name="gdn_conv_fwd", grid=(nb, 2),
        in_specs=[pl.BlockSpec((1, s, 128), lambda b, j: (b, 0, 2 * seg + j)), pl.BlockSpec((3, 128), lambda b, j: (0, j))],
        out_specs=pl.BlockSpec((1, s, 128), lambda b, j: (b, 0, j)),
        out_shape=_sds((nb, s, BW)),
        compiler_params=_cparams(("arbitrary", "arbitrary")),
    )(p3, w)


def gdn_conv_bwd(p3, w, seg, d_f, d_b, t_ctx):
    nb, s, _ = p3.shape
    sd, su = _make_shifts(t_ctx, s)

    def body(x_ref, w_ref, df_ref, db_ref, dx_ref, dw_ref):
        @pl.when(pl.program_id(1) == 0)
        def _():
            dw_ref[...] = jnp.zeros_like(dw_ref)
        _, vjp = jax.vjp(lambda x, w_: _silu(_conv3(x, w_, sd, su)), x_ref[0], w_ref[...])
        dx, dw = vjp(df_ref[0] + db_ref[0])
        dx_ref[0] = dx
        dw_ref[...] += dw

    blk = pl.BlockSpec((1, s, 128), lambda j, b: (b, 0, j))
    return pl.pallas_call(
        body, name="gdn_conv_bwd", grid=(2, nb),
        in_specs=[pl.BlockSpec((1, s, 128), lambda j, b: (b, 0, 2 * seg + j)), pl.BlockSpec((3, 128), lambda j, b: (0, j)),
                  blk, blk],
        out_specs=[blk, pl.BlockSpec((3, 128), lambda j, b: (0, j))],
        out_shape=[_sds((nb, s, BW)), _sds((3, BW))],
        compiler_params=_cparams(("arbitrary", "arbitrary"), VMEM_BIG),
    )(p3, w, d_f, d_b)


def _pairs(f_ref, b_ref, nb):
    return jnp.stack([r[b, :, PAIR_W * p:PAIR_W * (p + 1)] for b in range(nb) for r in (f_ref, b_ref) for p in range(2)])


def _gates(f_ref, b_ref, nb):
    return jnp.stack([r[b] for b in range(nb) for r in (f_ref, b_ref)])


def _unpairs(a, f_ref, b_ref, nb):
    for b in range(nb):
        for d, r in enumerate((f_ref, b_ref)):
            for p in range(2):
                r[b, :, PAIR_W * p:PAIR_W * (p + 1)] = a[4 * b + 2 * d + p]


def gdn_scan_fwd(cq, ck, cv, p3, alog, dtb, consts, t_ctx):
    nb, s, _ = p3.shape
    n, cf, cb = _chunk_maps(t_ctx // GC, (s - t_ctx) // GC)
    gblk = GATE_COL // 128

    nz = 4 * nb

    def body(qf, kf, vf, gf, qb, kb, vb, gb, al_ref, dt_ref, tm_r, tm2_r, st2_r, eg_r, eb_r, egt_r, dsel_r, eye_r, bd_r,
             of_ref, ob_ref, sall_ref, inv_ref, s_sc):
        @pl.when(pl.program_id(0) == 0)
        def _():
            s_sc[...] = jnp.zeros_like(s_sc)
        st = s_sc[...]
        sall_ref[0] = st
        s_new, o, inv = _gdn_step(st, _pairs(qf, qb, nb), _pairs(kf, kb, nb), _pairs(vf, vb, nb), _gates(gf, gb, nb),
                                  al_ref[...], dt_ref[...], tm_r[...], tm2_r[...], st2_r[...], eg_r[...], eb_r[...],
                                  egt_r[...], dsel_r[...], eye_r[...], bd_r[...])
        s_sc[...] = s_new
        inv_ref[0] = inv
        _unpairs(o, of_ref, ob_ref, nb)

    def cspec(m):
        return pl.BlockSpec((nb, GC, BW), lambda t: (0, m(t), 0))

    def gspec(m):
        return pl.BlockSpec((nb, GC, 128), lambda t: (0, m(t), gblk))

    return pl.pallas_call(
        body, name="gdn_scan_fwd", grid=(n,),
        in_specs=[cspec(cf), cspec(cf), cspec(cf), gspec(cf), cspec(cb), cspec(cb), cspec(cb), gspec(cb),
                  _full((1, 128)), _full((1, 128))] + [_full(c.shape) for c in consts],
        out_specs=[cspec(cf), cspec(cb), pl.BlockSpec((1, nz, PAIR_W, PAIR_W), lambda t: (t, 0, 0, 0)),
                   pl.BlockSpec((1, nz, GC, PAIR_W), lambda t: (t, 0, 0, 0))],
        out_shape=[_sds((nb, s, BW)), _sds((nb, s, BW)), _sds((n, nz, PAIR_W, PAIR_W)), _sds((n, nz, GC, PAIR_W))],
        scratch_shapes=[pltpu.VMEM((nz, PAIR_W, PAIR_W), f32)],
        compiler_params=_cparams(("arbitrary",)),
    )(cq, ck, cv, p3, cq, ck, cv, p3, alog, dtb, *consts)


def gdn_scan_bwd(cq, ck, cv, p3, alog, dtb, consts, s_all, inv_all, do, t_ctx):
    nb, s, _ = p3.shape
    n, cf, cb = _chunk_maps(t_ctx // GC, (s - t_ctx) // GC)
    gblk = GATE_COL // 128

    def rf(t):
        return cf(n - 1 - t)

    def rb(t):
        return cb(n - 1 - t)

    nz = 4 * nb

    def body(qf, kf, vf, gf, qb, kb, vb, gb, al_ref, dt_ref, tm_r, tm2_r, st2_r, eg_r, eb_r, egt_r, dsel_r, eye_r, bd_r,
             sall_ref, inv_ref, dof, dob, dqf, dkf, dvf, dgf, dqb, dkb, dvb, dgb, dal_ref, ddt_ref, ds_sc):
        @pl.when(pl.program_id(0) == 0)
        def _():
            dal_ref[...] = jnp.zeros_like(dal_ref)
            ddt_ref[...] = jnp.zeros_like(ddt_ref)
            ds_sc[...] = jnp.zeros_like(ds_sc)
        consts = dict(tmask=tm_r[...], tmask2=tm2_r[...], strict2=st2_r[...], exp_g=eg_r[...], exp_b=eb_r[...],
                      exp_gt=egt_r[...], dsel2=dsel_r[...], eye2=eye_r[...], bd2=bd_r[...], inv=inv_ref[0])

        def step(*a):
            return _gdn_step(*a, **consts)[:2]

        _, vjp = jax.vjp(step, sall_ref[0], _pairs(qf, qb, nb), _pairs(kf, kb, nb), _pairs(vf, vb, nb),
                         _gates(gf, gb, nb), al_ref[...], dt_ref[...])
        ds, dq, dk, dv, dg, dal, ddt = vjp((ds_sc[...], _pairs(dof, dob, nb)))
        ds_sc[...] = ds
        _unpairs(dq, dqf, dqb, nb)
        _unpairs(dk, dkf, dkb, nb)
        _unpairs(dv, dvf, dvb, nb)
        for b in range(nb):
            dgf[b] = dg[2 * b]
            dgb[b] = dg[2 * b + 1]
        dal_ref[...] += dal
        ddt_ref[...] += ddt

    def cspec(m):
        return pl.BlockSpec((nb, GC, BW), lambda t: (0, m(t), 0))

    def gspec(m):
        return pl.BlockSpec((nb, GC, 128), lambda t: (0, m(t), gblk))

    def gout(m):
        return pl.BlockSpec((nb, GC, 128), lambda t: (0, m(t), 0))

    return pl.pallas_call(
        body, name="gdn_scan_bwd", grid=(n,),
        in_specs=[cspec(rf), cspec(rf), cspec(rf), gspec(rf), cspec(rb), cspec(rb), cspec(rb), gspec(rb),
                  _full((1, 128)), _full((1, 128))] + [_full(c.shape) for c in consts]
                 + [pl.BlockSpec((1, nz, PAIR_W, PAIR_W), lambda t: (n - 1 - t, 0, 0, 0)),
                    pl.BlockSpec((1, nz, GC, PAIR_W), lambda t: (n - 1 - t, 0, 0, 0)), cspec(rf), cspec(rb)],
        out_specs=[cspec(rf), cspec(rf), cspec(rf), gout(rf), cspec(rb), cspec(rb), cspec(rb), gout(rb),
                   _full((1, 128)), _full((1, 128))],
        out_shape=[_sds((nb, s, BW))] * 3 + [_sds((nb, s, 128))] + [_sds((nb, s, BW))] * 3 + [_sds((nb, s, 128))]
                  + [_sds((1, 128)), _sds((1, 128))],
        scratch_shapes=[pltpu.VMEM((nz, PAIR_W, PAIR_W), f32)],
        compiler_params=_cparams(("arbitrary",), VMEM_BIG),
    )(cq, ck, cv, p3, cq, ck, cv, p3, alog, dtb, *consts, s_all, inv_all, do, do)


def sg_fwd(p3, w, b, hm4):
    nb, s, _ = p3.shape

    def body(u_ref, v_ref, z_ref, w_ref, b_ref, hm_ref, y_ref):
        y_ref[0] = _sg_chunk(u_ref[0], v_ref[0], z_ref[0], w_ref[...], b_ref[...], hm_ref[...]).astype(bf16)

    def seg(k):
        return pl.BlockSpec((1, RC, BW), lambda bi, i: (bi, i, k))

    return pl.pallas_call(
        body, name="sg_fwd", grid=(nb, s // RC),
        in_specs=[seg(4), seg(5), seg(6), _full((NH, RC, RC)), _full((NH, RC)), _full((NH, BW))],
        out_specs=pl.BlockSpec((1, RC, BW), lambda bi, i: (bi, i, 0)),
        out_shape=_sds((nb, s, BW), bf16),
        compiler_params=_cparams(("arbitrary", "arbitrary")),
    )(p3, p3, p3, w, b, hm4)


def sg_bwd(p3, w, b, hm4, dy3):
    nb, s, _ = p3.shape

    def body(u_ref, v_ref, z_ref, w_ref, b_ref, hm_ref, dy_ref, du_ref, dv_ref, dz_ref, dw_ref, db_ref):
        @pl.when((pl.program_id(0) == 0) & (pl.program_id(1) == 0))
        def _():
            dw_ref[...] = jnp.zeros_like(dw_ref)
            db_ref[...] = jnp.zeros_like(db_ref)
        hm = hm_ref[...]
        _, vjp = jax.vjp(lambda u, v, z, w_, b_: _sg_chunk(u, v, z, w_, b_, hm),
                         u_ref[0], v_ref[0], z_ref[0], w_ref[...], b_ref[...])
        du, dv, dz, dw, db = vjp(dy_ref[0])
        du_ref[0] = du
        dv_ref[0] = dv
        dz_ref[0] = dz
        dw_ref[...] += dw
        db_ref[...] += db

    def seg(k):
        return pl.BlockSpec((1, RC, BW), lambda bi, i: (bi, i, k))

    blk = pl.BlockSpec((1, RC, BW), lambda bi, i: (bi, i, 0))
    return pl.pallas_call(
        body, name="sg_bwd", grid=(nb, s // RC),
        in_specs=[seg(4), seg(5), seg(6), _full((NH, RC, RC)), _full((NH, RC)), _full((NH, BW)), seg(1)],
        out_specs=[blk, blk, blk, _full((NH, RC, RC)), _full((NH, RC))],
        out_shape=[_sds((nb, s, BW))] * 3 + [_sds((NH, RC, RC)), _sds((NH, RC))],
        compiler_params=_cparams(("arbitrary", "arbitrary")),
    )(p3, p3, p3, w, b, hm4, dy3)


def _sc_fn(b, c, h, z, w, sd, su):
    return b * _conv3(c * h, w, sd, su) * _silu(z)


def sc_fwd(p3, w, t_ctx):
    nb, s, _ = p3.shape
    sd, su = _make_shifts(t_ctx, s)

    def body(b_ref, c_ref, h_ref, z_ref, w_ref, y_ref):
        y_ref[0] = _sc_fn(b_ref[0], c_ref[0], h_ref[0], z_ref[0], w_ref[...], sd, su).astype(bf16)

    def seg(k):
        return pl.BlockSpec((1, s, 128), lambda bi, j: (bi, 0, 2 * k + j))

    return pl.pallas_call(
        body, name="sc_fwd", grid=(nb, 2),
        in_specs=[seg(7), seg(8), seg(9), seg(10), pl.BlockSpec((3, 128), lambda bi, j: (0, j))],
        out_specs=pl.BlockSpec((1, s, 128), lambda bi, j: (bi, 0, j)),
        out_shape=_sds((nb, s, BW), bf16),
        compiler_params=_cparams(("arbitrary", "arbitrary"), VMEM_BIG),
    )(p3, p3, p3, p3, w)


def sc_bwd(p3, w, dy3, t_ctx):
    nb, s, _ = p3.shape
    sd, su = _make_shifts(t_ctx, s)

    def body(b_ref, c_ref, h_ref, z_ref, w_ref, dy_ref, db_ref, dc_ref, dh_ref, dz_ref, dw_ref):
        @pl.when(pl.program_id(1) == 0)
        def _():
            dw_ref[...] = jnp.zeros_like(dw_ref)
        _, vjp = jax.vjp(lambda b, c, h, z, w_: _sc_fn(b, c, h, z, w_, sd, su),
                         b_ref[0], c_ref[0], h_ref[0], z_ref[0], w_ref[...])
        db, dc, dh, dz, dw = vjp(dy_ref[0])
        db_ref[0] = db
        dc_ref[0] = dc
        dh_ref[0] = dh
        dz_ref[0] = dz
        dw_ref[...] += dw

    def seg(k):
        return pl.BlockSpec((1, s, 128), lambda j, bi: (bi, 0, 2 * k + j))

    blk = pl.BlockSpec((1, s, 128), lambda j, bi: (bi, 0, j))
    wspec = pl.BlockSpec((3, 128), lambda j, bi: (0, j))
    return pl.pallas_call(
        body, name="sc_bwd", grid=(2, nb),
        in_specs=[seg(7), seg(8), seg(9), seg(10), wspec, seg(2)],
        out_specs=[blk, blk, blk, blk, wspec],
        out_shape=[_sds((nb, s, BW))] * 4 + [_sds((3, BW))],
        compiler_params=_cparams(("arbitrary", "arbitrary"), VMEM_BIG),
    )(p3, p3, p3, p3, w, dy3)


def assemble_dp(pairs, singles_a, gdn_x, singles_b, gates):
    nb, s, _ = singles_a[0].shape
    flat = [a for pr in pairs for a in pr] + list(singles_a) + list(gdn_x) + list(singles_b) + list(gates)
    n_pairs, n_a, n_x, n_b = len(pairs), len(singles_a), len(gdn_x), len(singles_b)

    def body(*refs):
        out = refs[-1]
        ins = refs[:-1]
        col = 0
        for p in range(n_pairs):
            out[0, :, col:col + BW] = (ins[2 * p][0] + ins[2 * p + 1][0]).astype(bf16)
            col += BW
        k = 2 * n_pairs
        for _ in range(n_a + n_x + n_b):
            out[0, :, col:col + BW] = ins[k][0].astype(bf16)
            col += BW
            k += 1
        out[0, :, col:col + 128] = (ins[k][0] + ins[k + 1][0]).astype(bf16)
        out[0, :, col + 128:] = jnp.zeros((TM, PW - col - 128), bf16)

    def spec(a):
        return pl.BlockSpec((1, TM, a.shape[-1]), lambda b, j: (b, j, 0))

    return pl.pallas_call(
        body, name="assemble_dp", grid=(nb, s // TM),
        in_specs=[spec(a) for a in flat],
        out_specs=pl.BlockSpec((1, TM, PW), lambda b, j: (b, j, 0)),
        out_shape=_sds((nb, s, PW), bf16),
        compiler_params=_cparams(("arbitrary", "arbitrary")),
    )(*flat)


def mod_fwd(c_rows, w_mod, b_cols):
    nl, _, wc = w_mod.shape
    nr = c_rows.shape[0]

    def body(c_ref, w_ref, b_ref, o_ref):
        o_ref[0] = _dot(_silu(c_ref[...]), w_ref[0], precision=HI) + b_ref[0]

    return pl.pallas_call(
        body, name="mod_fwd", grid=(nl,),
        in_specs=[_full((nr, D)), pl.BlockSpec((1, D, wc), lambda l: (l, 0, 0)), pl.BlockSpec((1, 1, wc), lambda l: (l, 0, 0))],
        out_specs=pl.BlockSpec((1, nr, wc), lambda l: (l, 0, 0)),
        out_shape=_sds((nl, nr, wc)),
        compiler_params=_cparams(("arbitrary",)),
    )(c_rows, w_mod, b_cols)


def mod_bwd(c_rows, w_mod, dm_cols, dm_full):
    nl, _, wc = w_mod.shape
    nr = c_rows.shape[0]

    def body(c_ref, w_ref, dmc_ref, dmf_ref, gw_ref, gb_ref, dcc_ref):
        @pl.when(pl.program_id(0) == 0)
        def _():
            dcc_ref[...] = jnp.zeros_like(dcc_ref)
        a = _silu(c_ref[...])
        dmc = dmc_ref[0]
        gw_ref[0] = _dot_tn(a, dmc, precision=HI)
        gb_ref[0] = jnp.sum(dmf_ref[0], axis=0, keepdims=True)
        dcc_ref[...] += _dot_nt(dmc[nr - 8:nr], w_ref[0], precision=HI)

    return pl.pallas_call(
        body, name="mod_bwd", grid=(nl,),
        in_specs=[_full((nr, D)), pl.BlockSpec((1, D, wc), lambda l: (l, 0, 0)),
                  pl.BlockSpec((1, nr, wc), lambda l: (l, 0, 0)), pl.BlockSpec((1, nr, 3 * D), lambda l: (l, 0, 0))],
        out_specs=[pl.BlockSpec((1, D, wc), lambda l: (l, 0, 0)), pl.BlockSpec((1, 1, 3 * D), lambda l: (l, 0, 0)),
                   _full((8, D))],
        out_shape=[_sds((nl, D, wc)), _sds((nl, 1, 3 * D)), _sds((8, D))],
        compiler_params=_cparams(("arbitrary",)),
    )(c_rows, w_mod, dm_cols, dm_full)


def cctx_grad(parts, c_ctx):
    def body(p_ref, c_ref, o_ref):
        tot = p_ref[0, 0:1, :]
        for k in (2, 4, 6):
            tot = tot + p_ref[k, 0:1, :]
        c = c_ref[...]
        sg = jax.nn.sigmoid(c)
        o_ref[...] = tot * (sg * (1.0 + c * (1.0 - sg)))

    return pl.pallas_call(body, name="cctx_grad", out_shape=_sds((1, D)))(parts, c_ctx)


def sum_lead(x, out_dtype=f32, tr=256, rows=None):
    k, r, c = x.shape
    r = r if rows is None else rows
    tr = min(tr, r)
    assert r % tr == 0

    def body(x_ref, o_ref):
        tot = x_ref[0].astype(f32)
        for i in range(1, k):
            tot = tot + x_ref[i].astype(f32)
        o_ref[...] = tot.astype(out_dtype)

    return pl.pallas_call(
        body, name="sum_lead", grid=(r // tr,),
        in_specs=[pl.BlockSpec((k, tr, c), lambda i: (0, i, 0))],
        out_specs=pl.BlockSpec((tr, c), lambda i: (i, 0)),
        out_shape=_sds((r, c), out_dtype),
        compiler_params=_cparams(("arbitrary",)),
    )(x)


def adamw(w, m, v, g1, g2=None, tr=256):
    r, c = w.shape
    tr = min(tr, r)
    assert r % tr == 0
    two = g2 is not None
    c1 = 1.0 / (1.0 - ADAM_B1 ** ADAM_STEP)
    c2 = 1.0 / (1.0 - ADAM_B2 ** ADAM_STEP)

    def body(*refs):
        w_ref, m_ref, v_ref, g_ref = refs[:4]
        g = g_ref[...]
        if two:
            g = g + refs[4][...]
        go_ref, d_ref, mo_ref, vo_ref = refs[-4:]
        mn = ADAM_B1 * m_ref[...] + (1.0 - ADAM_B1) * g
        vn = ADAM_B2 * v_ref[...] + (1.0 - ADAM_B2) * (g * g)
        go_ref[...] = g
        mo_ref[...] = mn
        vo_ref[...] = vn
        d_ref[...] = -ADAM_LR * ((mn * c1) / (jnp.sqrt(vn * c2) + ADAM_EPS) + ADAM_WD * w_ref[...])

    blk = pl.BlockSpec((tr, c), lambda i: (i, 0))
    args = [w, m, v, g1] + ([g2] if two else [])
    return pl.pallas_call(
        body, name="adamw", grid=(r // tr,),
        in_specs=[blk] * len(args), out_specs=[blk] * 4, out_shape=[_sds((r, c))] * 4,
        compiler_params=_cparams(("arbitrary",)),
    )(*args)


def _my_pos():
    return lax.axis_index("x"), lax.axis_index("y"), lax.axis_index("c")


def gather8(x):
    shape = x.shape

    def body(x_ref, out_ref, send_sems, recv_sems, local_sem):
        mx, my, mc = _my_pos()
        me = 4 * mx + 2 * my + mc
        mine = pltpu.make_async_copy(x_ref, out_ref.at[me], local_sem)
        mine.start()
        copies = []
        for k in range(1, N_DEV):
            peer = (mx ^ (k >> 2), my ^ ((k >> 1) & 1), mc ^ (k & 1))
            cp = pltpu.make_async_remote_copy(src_ref=x_ref, dst_ref=out_ref.at[me], send_sem=send_sems.at[k - 1],
                                              recv_sem=recv_sems.at[k - 1], device_id=peer, device_id_type=MESH)
            cp.start()
            copies.append(cp)
        for k in range(1, N_DEV):
            src = me ^ k
            pltpu.make_async_remote_copy(src_ref=x_ref, dst_ref=out_ref.at[src], send_sem=send_sems.at[k - 1],
                                         recv_sem=recv_sems.at[k - 1], device_id=(mx, my, mc),
                                         device_id_type=MESH).wait_recv()
        for cp in copies:
            cp.wait_send()
        mine.wait()

    return pl.pallas_call(
        body, name="gather8", out_shape=_sds((N_DEV,) + shape, x.dtype),
        in_specs=[pl.BlockSpec(memory_space=pl.ANY)], out_specs=pl.BlockSpec(memory_space=pl.ANY),
        scratch_shapes=[pltpu.SemaphoreType.DMA((N_DEV - 1,)), pltpu.SemaphoreType.DMA((N_DEV - 1,)),
                        pltpu.SemaphoreType.DMA(())],
    )(x)


def gather4(x):
    shape = x.shape

    def body(x_ref, out_ref, send_sems, recv_sems, local_sem):
        mx, my, mc = _my_pos()
        me = 2 * mx + my
        mine = pltpu.make_async_copy(x_ref, out_ref.at[me], local_sem)
        mine.start()
        copies = []
        for k in range(1, N_CHIPS):
            peer = (mx ^ (k >> 1), my ^ (k & 1), mc)
            cp = pltpu.make_async_remote_copy(src_ref=x_ref, dst_ref=out_ref.at[me], send_sem=send_sems.at[k - 1],
                                              recv_sem=recv_sems.at[k - 1], device_id=peer, device_id_type=MESH)
            cp.start()
            copies.append(cp)
        for k in range(1, N_CHIPS):
            src = me ^ k
            pltpu.make_async_remote_copy(src_ref=x_ref, dst_ref=out_ref.at[src], send_sem=send_sems.at[k - 1],
                                         recv_sem=recv_sems.at[k - 1], device_id=(mx, my, mc),
                                         device_id_type=MESH).wait_recv()
        for cp in copies:
            cp.wait_send()
        mine.wait()

    return pl.pallas_call(
        body, name="gather4", out_shape=_sds((N_CHIPS,) + shape, x.dtype),
        in_specs=[pl.BlockSpec(memory_space=pl.ANY)], out_specs=pl.BlockSpec(memory_space=pl.ANY),
        scratch_shapes=[pltpu.SemaphoreType.DMA((N_CHIPS - 1,)), pltpu.SemaphoreType.DMA((N_CHIPS - 1,)),
                        pltpu.SemaphoreType.DMA(())],
    )(x)


def scatter4(g):
    shape = g.shape[1:]

    def body(g_ref, out_ref, send_sems, recv_sems, local_sem):
        mx, my, mc = _my_pos()
        me = 2 * mx + my
        mine = pltpu.make_async_copy(g_ref.at[me], out_ref.at[me], local_sem)
        mine.start()
        copies = []
        for k in range(1, N_CHIPS):
            peer = (mx ^ (k >> 1), my ^ (k & 1), mc)
            cp = pltpu.make_async_remote_copy(src_ref=g_ref.at[me ^ k], dst_ref=out_ref.at[me], send_sem=send_sems.at[k - 1],
                                              recv_sem=recv_sems.at[k - 1], device_id=peer, device_id_type=MESH)
            cp.start()
            copies.append(cp)
        for k in range(1, N_CHIPS):
            src = me ^ k
            pltpu.make_async_remote_copy(src_ref=g_ref.at[src], dst_ref=out_ref.at[src], send_sem=send_sems.at[k - 1],
                                         recv_sem=recv_sems.at[k - 1], device_id=(mx, my, mc),
                                         device_id_type=MESH).wait_recv()
        for cp in copies:
            cp.wait_send()
        mine.wait()

    return pl.pallas_call(
        body, name="scatter4", out_shape=_sds((N_CHIPS,) + shape, g.dtype),
        in_specs=[pl.BlockSpec(memory_space=pl.ANY)], out_specs=pl.BlockSpec(memory_space=pl.ANY),
        scratch_shapes=[pltpu.SemaphoreType.DMA((N_CHIPS - 1,)), pltpu.SemaphoreType.DMA((N_CHIPS - 1,)),
                        pltpu.SemaphoreType.DMA(())],
    )(g)


def swap_sibling(x):
    def body(x_ref, out_ref, send_sem, recv_sem):
        mx, my, mc = _my_pos()
        cp = pltpu.make_async_remote_copy(src_ref=x_ref, dst_ref=out_ref, send_sem=send_sem, recv_sem=recv_sem,
                                          device_id=(mx, my, 1 - mc), device_id_type=MESH)
        cp.start()
        cp.wait()

    return pl.pallas_call(
        body, name="swap_sibling", out_shape=_sds(x.shape, x.dtype),
        in_specs=[pl.BlockSpec(memory_space=pl.ANY)], out_specs=pl.BlockSpec(memory_space=pl.ANY),
        scratch_shapes=[pltpu.SemaphoreType.DMA(()), pltpu.SemaphoreType.DMA(())],
    )(x)


PACK_ROWS = 64
SMALL = ("c_ctx", "b_mod", "g_pre", "g_post", "ret_norm_g", "sg_w", "sg_b", "sc_conv_w", "gdn_conv_w",
         "gdn_a_log", "gdn_dt_bias", "gdn_norm_g")


def _pack(arrs, width=D):
    rows = []
    for a in arrs:
        flat = a.reshape(-1)
        pad = (-flat.shape[0]) % width
        rows.append(jnp.pad(flat, (0, pad)).reshape(-1, width))
    out = jnp.concatenate(rows, axis=0)
    return jnp.pad(out, ((0, (-out.shape[0]) % PACK_ROWS), (0, 0)))


def _unpack(packed, shapes, width=D):
    outs, r = [], 0
    for shp in shapes:
        size = int(np.prod(shp))
        nr = -(-size // width)
        outs.append(packed[r:r + nr].reshape(-1)[:size].reshape(shp))
        r += nr
    return outs


def kernel(x, c, ctx, c_ctx, w_mod, b_mod, g_pre, g_post, w_in, w_out, ret_norm_g, sg_w, sg_b, sc_conv_w, gdn_conv_w, gdn_a_log, gdn_dt_bias, gdn_norm_g, loss_target, m_c_ctx, m_w_mod, m_b_mod, m_g_pre, m_g_post, m_w_in, m_w_out, m_ret_norm_g, m_sg_w, m_sg_b, m_sc_conv_w, m_gdn_conv_w, m_gdn_a_log, m_gdn_dt_bias, m_gdn_norm_g, v_c_ctx, v_w_mod, v_b_mod, v_g_pre, v_g_post, v_w_in, v_w_out, v_ret_norm_g, v_sg_w, v_sg_b, v_sc_conv_w, v_gdn_conv_w, v_gdn_a_log, v_gdn_dt_bias, v_gdn_norm_g):
    weights = dict(c_ctx=c_ctx, w_mod=w_mod, b_mod=b_mod, g_pre=g_pre, g_post=g_post, w_in=w_in, w_out=w_out,
                   ret_norm_g=ret_norm_g, sg_w=sg_w, sg_b=sg_b, sc_conv_w=sc_conv_w, gdn_conv_w=gdn_conv_w,
                   gdn_a_log=gdn_a_log, gdn_dt_bias=gdn_dt_bias, gdn_norm_g=gdn_norm_g)
    mom = dict(c_ctx=m_c_ctx, w_mod=m_w_mod, b_mod=m_b_mod, g_pre=m_g_pre, g_post=m_g_post, w_in=m_w_in,
               w_out=m_w_out, ret_norm_g=m_ret_norm_g, sg_w=m_sg_w, sg_b=m_sg_b, sc_conv_w=m_sc_conv_w,
               gdn_conv_w=m_gdn_conv_w, gdn_a_log=m_gdn_a_log, gdn_dt_bias=m_gdn_dt_bias, gdn_norm_g=m_gdn_norm_g)
    var = dict(c_ctx=v_c_ctx, w_mod=v_w_mod, b_mod=v_b_mod, g_pre=v_g_pre, g_post=v_g_post, w_in=v_w_in,
               w_out=v_w_out, ret_norm_g=v_ret_norm_g, sg_w=v_sg_w, sg_b=v_sg_b, sc_conv_w=v_sc_conv_w,
               gdn_conv_w=v_gdn_conv_w, gdn_a_log=v_gdn_a_log, gdn_dt_bias=v_gdn_dt_bias, gdn_norm_g=v_gdn_norm_g)

    nb, t_lat, _ = x.shape
    t_ctx = ctx.shape[1]
    s = t_ctx + t_lat
    n = nb * s
    sb = s // TM
    nl = w_in.shape[0]
    wc_in = w_in.shape[2]
    wc_mod = w_mod.shape[2]
    rows_out = w_out.shape[1]
    n_all = nb * N_DEV
    mx, my, mc = _my_pos()
    chip = 2 * mx + my
    dev = 2 * chip + mc

    hm = jnp.asarray(_head_masks())
    hm4 = hm[:, 0, :]
    bd = jnp.asarray(_block_diag())
    ret_c = _ret_consts() + [bd, hm]
    gdn_c = _gdn_consts(nb)
    cos, sins = _rope_tables(t_lat, t_ctx)

    pre = _pack([c, sc_conv_w, gdn_conv_w])
    pre_all = gather8(pre)
    c_parts, scw_parts, gcw_parts = [], [], []
    for k in range(N_DEV):
        ck, sk, gk = _unpack(pre_all[k], [c.shape, sc_conv_w.shape, gdn_conv_w.shape])
        c_parts.append(ck)
        if k % 2 == 0:
            scw_parts.append(sk)
            gcw_parts.append(gk)
    c_all = jnp.concatenate(c_parts, axis=0)
    sc_w_full = jnp.concatenate(scw_parts, axis=-1)
    gdn_w_full = jnp.concatenate(gcw_parts, axis=-1)
    c_rows = jnp.concatenate([c_all, c_ctx[None, :], jnp.zeros((7, D), f32)], axis=0)

    b_cols = lax.dynamic_slice_in_dim(b_mod, chip * wc_mod, wc_mod, axis=1)[:, None, :]
    mod_part = mod_fwd(c_rows, w_mod, b_cols)
    mod_all = gather8(mod_part)
    mod = jnp.concatenate([mod_all[2 * k] for k in range(N_CHIPS)], axis=-1)
    my_rows = jnp.concatenate([lax.dynamic_slice_in_dim(mod, dev * nb, nb, axis=1), mod[:, n_all:n_all + 1]], axis=1)
    shift_t = my_rows[:, :, None, 0:D]
    scale_t = my_rows[:, :, None, D:2 * D]
    gate_t = my_rows[:, :, None, 2 * D:3 * D]

    w_in_all = gather4(w_in.astype(bf16))
    w_in_full = jnp.concatenate([w_in_all[k] for k in range(N_CHIPS)], axis=-1)
    w_in_full = jnp.pad(w_in_full, ((0, 0), (0, 0), (0, PW - IN_W)))
    w_out_all = gather4(w_out.astype(bf16))
    w_out_full = jnp.concatenate([w_out_all[k] for k in range(N_CHIPS)], axis=1)

    alog = jnp.pad(gdn_a_log.reshape(nl, 1, 8), ((0, 0), (0, 0), (0, 120)))
    dtb = jnp.pad(gdn_dt_bias.reshape(nl, 1, 8), ((0, 0), (0, 0), (0, 120)))
    gdn_ng = jnp.tile(gdn_norm_g, (1, NH))[:, None, :]
    ret_ng = ret_norm_g[:, None, :]

    xs = jnp.concatenate([ctx, x], axis=1).reshape(n, D)
    saved = []
    for l in range(nl):
        p, h = inproj_fwd(xs, shift_t[l], scale_t[l], g_pre[l][None, :], w_in_full[l], nb, sb)
        p3 = p.reshape(nb, s, PW)
        ro_f, ro_b, rs_all = ret_scan_fwd(p3, cos, sins, ret_c, t_ctx)
        y_ret = mix_finish_fwd(_ret_finish, "ret_finish_fwd", ro_f, ro_b, p3, 3, ret_ng[l], bd)
        y_sg = sg_fwd(p3, sg_w[l], sg_b[l], hm4)
        y_sc = sc_fwd(p3, sc_w_full[l], t_ctx)
        cq, ck, cv = [gdn_conv_fwd(p3, gdn_w_full[l][:, BW * i:BW * (i + 1)], 11 + i, t_ctx) for i in range(3)]
        go_f, go_b, *gs_all = gdn_scan_fwd(cq, ck, cv, p3, alog[l], dtb[l], gdn_c, t_ctx)
        y_gdn = mix_finish_fwd(_gdn_finish, "gdn_finish_fwd", go_f, go_b, p3, 14, gdn_ng[l], bd)
        ys = [a.reshape(n, BW) for a in (y_ret, y_sg, y_sc, y_gdn)]
        x_new, o = outproj_fwd(ys, w_out_full[l], xs, gate_t[l], g_post[l][None, :], nb, sb)
        saved.append(dict(x=xs, h=h, p3=p3, ro=(ro_f, ro_b), rs=rs_all, c=(cq, ck, cv), go=(go_f, go_b), gs=gs_all,
                          ys=ys, o=o))
        xs = x_new

    dx3, loss_part = loss_head(xs.reshape(nb, s, D), loss_target, t_ctx)
    loss = lax.psum(loss_part[0, 0], ("x", "y", "c"))

    dxs = dx3.reshape(n, D)
    g_small = {k: [None] * nl for k in SMALL if k not in ("c_ctx", "b_mod")}
    dm_rows = [None] * nl
    gw_in = [None] * nl
    gw_out = [None] * nl
    for l in reversed(range(nl)):
        sv = saved[l]
        p3 = sv["p3"]
        dy, gw_out[l], dg_post, dgate = outproj_bwd(dxs, sv["o"], gate_t[l], g_post[l][None, :], sv["ys"], w_out_full[l], nb, sb)
        dy3 = dy.reshape(nb, s, D)
        r_do, r_dz, d_rng = mix_finish_bwd(_ret_finish, "ret_finish_bwd", *sv["ro"], p3, 3, ret_ng[l], bd, dy3, 0)
        r_d = ret_scan_bwd(p3, cos, sins, ret_c, sv["rs"], r_do, t_ctx)
        s_du, s_dv, s_dz, d_sgw, d_sgb = sg_bwd(p3, sg_w[l], sg_b[l], hm4, dy3)
        c_db, c_dc, c_dh, c_dz, d_scw = sc_bwd(p3, sc_w_full[l], dy3, t_ctx)
        g_do, g_dz, d_gng = mix_finish_bwd(_gdn_finish, "gdn_finish_bwd", *sv["go"], p3, 14, gdn_ng[l], bd, dy3, 3)
        g_d = gdn_scan_bwd(*sv["c"], p3, alog[l], dtb[l], gdn_c, *sv["gs"], g_do, t_ctx)
        gx, d_gcw = [], []
        for i in range(3):
            dxi, dwi = gdn_conv_bwd(p3, gdn_w_full[l][:, BW * i:BW * (i + 1)], 11 + i, g_d[i], g_d[4 + i], t_ctx)
            gx.append(dxi)
            d_gcw.append(dwi)
        dp3 = assemble_dp([(r_d[0], r_d[3]), (r_d[1], r_d[4]), (r_d[2], r_d[5])],
                          [r_dz, s_du, s_dv, s_dz, c_db, c_dc, c_dh, c_dz], gx, [g_dz], [g_d[3], g_d[7]])
        dp = dp3.reshape(n, PW)
        dxs, dg_pre, dshift, dscale = inproj_bwd_x(dp, w_in_full[l], sv["x"], scale_t[l], g_pre[l][None, :], dxs, nb, sb)
        gw_in[l] = dw_in(sv["h"], dp)
        g_small["g_pre"][l] = dg_pre[0]
        g_small["g_post"][l] = dg_post[0]
        g_small["ret_norm_g"][l] = d_rng[0]
        g_small["sg_w"][l] = d_sgw
        g_small["sg_b"][l] = d_sgb
        g_small["sc_conv_w"][l] = d_scw
        g_small["gdn_conv_w"][l] = jnp.concatenate(d_gcw, axis=-1)
        g_small["gdn_a_log"][l] = g_d[8][0, :8].reshape(2, NH)
        g_small["gdn_dt_bias"][l] = g_d[9][0, :8].reshape(2, NH)
        g_small["gdn_norm_g"][l] = d_gng[0].reshape(NH, HD)
        dm_rows[l] = jnp.concatenate([dshift, dscale, dgate], axis=-1)[:nb + 1]
    grad_x = dxs.reshape(nb, s, D)[:, t_ctx:, :]

    g_small = {k: jnp.stack(v) for k, v in g_small.items()}
    dm_rows = jnp.stack(dm_rows)
    names2 = [k for k in SMALL if k not in ("c_ctx", "b_mod")]
    pack_sum = _pack([g_small[k] for k in names2] + [dm_rows[:, nb:]])
    pack_own = _pack([dm_rows[:, :nb]])
    all2 = gather8(jnp.concatenate([pack_sum, pack_own], axis=0))
    tot2 = sum_lead(all2, tr=PACK_ROWS, rows=pack_sum.shape[0])
    outs2 = _unpack(tot2, [g_small[k].shape for k in names2] + [(nl, 1, 3 * D)])
    grads = dict(zip(names2, outs2[:-1]))
    dm_own = jnp.stack([_unpack(all2[k, pack_sum.shape[0]:], [(nl, nb, 3 * D)])[0] for k in range(N_DEV)])
    dm_own = jnp.transpose(dm_own, (1, 0, 2, 3)).reshape(nl, n_all, 3 * D)
    dm_all = jnp.concatenate([dm_own, jnp.pad(outs2[-1], ((0, 0), (0, 7), (0, 0)))], axis=1)
    grads["gdn_norm_g"] = sum_lead(jnp.transpose(grads["gdn_norm_g"], (1, 0, 2)), tr=nl)
    for k in ("sc_conv_w", "gdn_conv_w"):
        wc = weights[k].shape[2]
        grads[k] = lax.dynamic_slice_in_dim(grads[k], chip * wc, wc, axis=2)

    dm_cols = lax.dynamic_slice_in_dim(dm_all, chip * wc_mod, wc_mod, axis=2)
    g_w_mod, g_b_mod, dcc_part = mod_bwd(c_rows, w_mod, dm_cols, dm_all)
    grads["b_mod"] = g_b_mod[:, 0, :]
    grads["c_ctx"] = cctx_grad(gather8(dcc_part), c_ctx[None, :])[0]

    gw_in = jnp.stack(gw_in)[:, :, :IN_W].reshape(nl, D, N_CHIPS, wc_in)
    gw_in = jnp.transpose(gw_in, (2, 0, 1, 3)).astype(bf16).reshape(N_CHIPS, nl * D, wc_in)
    gin_mine = sum_lead(scatter4(gw_in))
    gin_sib = swap_sibling(gin_mine)
    gw_out = jnp.stack(gw_out).reshape(nl, N_CHIPS, rows_out, D)
    gw_out = jnp.transpose(gw_out, (1, 0, 2, 3)).astype(bf16).reshape(N_CHIPS, nl * rows_out, D)
    gout_mine = sum_lead(scatter4(gw_out))
    gout_sib = swap_sibling(gout_mine)

    res = {}
    res["w_in"] = [a.reshape(w_in.shape) for a in adamw(w_in.reshape(nl * D, wc_in), m_w_in.reshape(nl * D, wc_in),
                                                          v_w_in.reshape(nl * D, wc_in), gin_mine, gin_sib)]
    res["w_out"] = [a.reshape(w_out.shape) for a in adamw(w_out.reshape(nl * rows_out, D), m_w_out.reshape(nl * rows_out, D),
                                                            v_w_out.reshape(nl * rows_out, D), gout_mine, gout_sib)]
    res["w_mod"] = [a.reshape(w_mod.shape) for a in adamw(w_mod.reshape(nl * D, wc_mod), m_w_mod.reshape(nl * D, wc_mod),
                                                            v_w_mod.reshape(nl * D, wc_mod), g_w_mod.reshape(nl * D, wc_mod))]
    shapes = [weights[k].shape for k in SMALL]
    small = adamw(_pack([weights[k] for k in SMALL]), _pack([mom[k] for k in SMALL]), _pack([var[k] for k in SMALL]),
                  _pack([grads[k].reshape(weights[k].shape) for k in SMALL]), tr=PACK_ROWS)
    small = [_unpack(a, shapes) for a in small]
    for i, k in enumerate(SMALL):
        res[k] = [small[j][i] for j in range(4)]

    order = ["c_ctx", "w_mod", "b_mod", "g_pre", "g_post", "w_in", "w_out", "ret_norm_g", "sg_w", "sg_b", "sc_conv_w",
             "gdn_conv_w", "gdn_a_log", "gdn_dt_bias", "gdn_norm_g"]
    return (loss, grad_x, *[res[k][0] for k in order], *[res[k][1] for k in order], *[res[k][2] for k in order],
            *[res[k][3] for k in order])
```

```python
import functools

import jax
import jax.numpy as jnp
import numpy as np
from jax import lax
from jax.experimental import pallas as pl
from jax.experimental.pallas import tpu as pltpu

f32 = jnp.float32
bf16 = jnp.bfloat16
HI = lax.Precision.HIGHEST
P3 = lax.Precision.HIGH
MESH = pl.DeviceIdType.MESH

EPS = 1e-6
D = 1024
NH = 4
HD = 64
BW = NH * HD
PAIR_W = 2 * HD
RC = 128
GC = 64
GRID_W = 64
ROPE_BASE = 10000.0
IN_W = 15 * BW + 16
PW = 4096
GATE_COL = 15 * BW
N_CHIPS = 4
N_DEV = 8
TM = 256
TP = 2 * TM
ADAM_LR, ADAM_B1, ADAM_B2, ADAM_EPS, ADAM_WD, ADAM_STEP = 0.001, 0.9, 0.999, 1e-08, 0.01, 10
LANE_HEAD = np.arange(BW) // HD
VMEM_BIG = 56 * 1024 * 1024


def _dot(a, b, precision=None):
    return jnp.dot(a, b, precision=precision, preferred_element_type=f32)


def _dot_nt(a, b, precision=None):
    return lax.dot_general(a, b, (((1,), (1,)), ((), ())), precision=precision, preferred_element_type=f32)


def _dot_tn(a, b, precision=None):
    return lax.dot_general(a, b, (((0,), (0,)), ((), ())), precision=precision, preferred_element_type=f32)


def _sds(shape, dtype=f32):
    return jax.ShapeDtypeStruct(shape, dtype)


def _cparams(sem=None, vmem=None):
    kw = {}
    if sem is not None:
        kw["dimension_semantics"] = sem
    if vmem is not None:
        kw["vmem_limit_bytes"] = vmem
    return pltpu.CompilerParams(**kw)


def _full(shape):
    n = len(shape)
    return pl.BlockSpec(shape, lambda *_: (0,) * n)


def _head_masks():
    return np.stack([(LANE_HEAD == h).astype(np.float32)[None, :] for h in range(NH)])


def _block_diag():
    return (LANE_HEAD[:, None] == LANE_HEAD[None, :]).astype(np.float32)


def _tau(c, d):
    return np.arange(c) if d == 0 else c - 1 - np.arange(c)


def _ret_consts():
    lg = np.log(1.0 - 2.0 ** (-5.0 - np.arange(NH)))
    intra = np.zeros((2, NH, RC, RC)); qdec = np.zeros((2, RC, BW)); kdec = np.zeros((2, RC, BW))
    for d in range(2):
        t = _tau(RC, d)
        diff = t[:, None] - t[None, :]
        for h in range(NH):
            intra[d, h] = np.where(diff >= 0, np.exp(np.maximum(diff, 0) * lg[h]), 0.0)
        qdec[d] = np.exp((t[:, None] + 1.0) * lg[LANE_HEAD][None, :])
        kdec[d] = np.exp((RC - 1.0 - t[:, None]) * lg[LANE_HEAD][None, :])
    cd = np.exp(RC * lg[LANE_HEAD])[:, None] * np.ones((1, BW))
    return [jnp.asarray(a, f32) for a in (intra, qdec, kdec, cd)]


def _rope_tables(t_lat, t_ctx):
    nf = HD // 4
    inv = ROPE_BASE ** (-np.arange(nf) / nf)
    pos = np.arange(t_lat)
    ang_r = (pos // GRID_W)[:, None] * inv[None, :]
    ang_c = (pos % GRID_W)[:, None] * inv[None, :]
    ang = np.concatenate([ang_r, ang_r, ang_c, ang_c], axis=1)
    sign = np.concatenate([-np.ones(nf), np.ones(nf), -np.ones(nf), np.ones(nf)])
    cos = np.tile(np.cos(ang), (1, NH)); sins = np.tile(np.sin(ang) * sign, (1, NH))
    cos = np.concatenate([np.ones((t_ctx, BW)), cos]); sins = np.concatenate([np.zeros((t_ctx, BW)), sins])
    return jnp.asarray(cos, f32), jnp.asarray(sins, f32)


def _gdn_consts(nb):
    tmask = np.zeros((2, 2, GC, GC)); tmask2 = np.zeros((2, 2, GC, PAIR_W)); strict2 = np.zeros((2, 2, GC, PAIR_W))
    exp_g = np.zeros((2, 2, 128, PAIR_W)); exp_b = np.zeros((2, 2, 128, PAIR_W))
    for d in range(2):
        t = _tau(GC, d)
        tmask[d, :] = (t[:, None] >= t[None, :])
        tmask2[d, :] = np.tile(t[:, None] >= t[None, :], (1, 2))
        strict2[d, :] = np.tile(t[:, None] > t[None, :], (1, 2))
        for h in range(NH):
            exp_g[d, h // 2, 4 * d + h, (h % 2) * HD:(h % 2 + 1) * HD] = 1.0
            exp_b[d, h // 2, 8 + 4 * d + h, (h % 2) * HD:(h % 2 + 1) * HD] = 1.0
    exp_gt = np.transpose(exp_g, (0, 1, 3, 2))
    per_z = [np.tile(a.reshape((4,) + a.shape[2:]), (nb, 1, 1)) for a in (tmask, tmask2, strict2, exp_g, exp_b, exp_gt)]
    dsel2 = np.tile(np.eye(GC), (1, 2))
    eye2 = np.tile(np.eye(GC), (1, 2))
    bd2 = (np.arange(PAIR_W)[:, None] // HD == np.arange(PAIR_W)[None, :] // HD)
    return [jnp.asarray(a, f32) for a in per_z + [dsel2, eye2, bd2]]


def _swap16(x):
    lane = lax.broadcasted_iota(jnp.int32, x.shape, x.ndim - 1)
    n = x.shape[-1]
    return jnp.where(lane % 32 < 16, pltpu.roll(x, n - 16, axis=x.ndim - 1), pltpu.roll(x, 16, axis=x.ndim - 1))


@jax.custom_vjp
def _rot(x, cos, sins):
    return x * cos + _swap16(x) * sins


def _rot_fwd(x, cos, sins):
    return _rot(x, cos, sins), (cos, sins)


def _rot_bwd(res, g):
    cos, sins = res
    return g * cos + _swap16(g * sins), jnp.zeros_like(cos), jnp.zeros_like(sins)


_rot.defvjp(_rot_fwd, _rot_bwd)


def _silu(z):
    return z * jax.nn.sigmoid(z)


def _head_sum(x, bd):
    return _dot(x, bd, precision=P3)


def _ret_step(s, q, k, v, cos, sins, intra, qdec, kdec, cd, bd, hm):
    qr = _rot(q, cos, sins)
    kr = _rot(k, cos, sins) * (HD ** -0.5)
    o = _dot(qr * qdec, s)
    for h in range(NH):
        sc = _dot_nt(qr * hm[h], kr) * intra[h]
        o = o + _dot(sc, v) * hm[h]
    s_new = s * cd + bd * _dot_tn(kr * kdec, v)
    return s_new, o


def _ret_finish(o_f, o_b, z, norm_g, bd):
    o = o_f + o_b
    mu = _head_sum(o, bd) * (1.0 / HD)
    xc = o - mu
    var = _head_sum(xc * xc, bd) * (1.0 / HD)
    return xc * lax.rsqrt(var + EPS) * norm_g * _silu(z)


def _softplus(x):
    return jnp.maximum(x, 0.0) + jnp.log(1.0 + jnp.exp(-jnp.abs(x)))


def _bmm(a, b, precision=None):
    return lax.dot_general(a, b, (((2,), (1,)), ((0,), (0,))), precision=precision, preferred_element_type=f32)


def _bmm_nt(a, b, precision=None):
    return lax.dot_general(a, b, (((2,), (2,)), ((0,), (0,))), precision=precision, preferred_element_type=f32)


def _bmm_tn(a, b, precision=None):
    return lax.dot_general(a, b, (((1,), (1,)), ((0,), (0,))), precision=precision, preferred_element_type=f32)


def _bdiag(x, bd2):
    return jnp.concatenate([x, x], axis=1) * bd2


@jax.custom_vjp
def _solve_given_inv(m, vb, kbg, inv, bd2):
    return _bmm(inv, _bdiag(vb, bd2), P3), _bmm(inv, _bdiag(kbg, bd2), P3)


def _solve_fwd(m, vb, kbg, inv, bd2):
    u, w = _solve_given_inv(m, vb, kbg, inv, bd2)
    return (u, w), (inv, u, w, bd2)


def _solve_bwd(res, cts):
    inv, u, w, bd2 = res
    du, dw = cts
    c = inv.shape[1]
    t = jnp.swapaxes(_bdiag(inv, bd2), 1, 2)
    inv_t = t[:, :c] + t[:, c:]
    dvb = _bmm(inv_t, _bdiag(du, bd2), P3)
    dkbg = _bmm(inv_t, _bdiag(dw, bd2), P3)
    dm = _bmm_nt(dvb, _bdiag(u, bd2), P3) + _bmm_nt(dkbg, _bdiag(w, bd2), P3)
    return dm, dvb, dkbg, jnp.zeros_like(inv), jnp.zeros_like(bd2)


_solve_given_inv.defvjp(_solve_fwd, _solve_bwd)


def _gdn_step(s, q, k, v, gate, alog, dtb, tmask, tmask2, strict2, exp_g, exp_b, exp_gt, dsel2, eye2, bd2, inv=None):
    z, c, w_ = q.shape
    ne = gate.shape[0]

    def per_pair(a):
        return jnp.broadcast_to(a[:, None], (ne, z // ne) + a.shape[1:]).reshape((z,) + a.shape[1:])

    def rows(a):
        return a.reshape(z * c, w_)

    def bdiag(x):
        return _bdiag(x, bd2)

    g = per_pair(-jnp.exp(alog) * _softplus(gate + dtb))
    beta = per_pair(jax.nn.sigmoid(gate))
    gl = _bmm(g, exp_g, P3)
    gc_l = _bmm(tmask, gl, P3)
    glast_l = jnp.sum(gl, axis=1, keepdims=True)
    glast = jnp.sum(g, axis=1, keepdims=True)
    beta_l = _bmm(beta, exp_b, P3)
    gc_r = jnp.sum(gc_l * dsel2, axis=1, keepdims=True)
    qn = q * lax.rsqrt(_dot(rows(q * q), bd2, P3).reshape(z, c, w_) + EPS)
    kn = k * lax.rsqrt(_dot(rows(k * k), bd2, P3).reshape(z, c, w_) + EPS)
    eg = jnp.exp(gc_l)
    kb = kn * beta_l
    vb = v * beta_l
    kbg = kb * eg
    qs = qn * (HD ** -0.5)
    dec = jnp.exp(jnp.where(tmask2 > 0, gc_l - gc_r, -1e30))
    kns = bdiag(kn)
    m = -(_bmm_nt(kb, kns) * dec * strict2)
    if inv is None:
        inv = eye2 + m
        p = m
        for _ in range(5):
            p = _bmm(p, bdiag(p), P3)
            inv = inv + _bmm(inv, bdiag(p), P3)
        u = _bmm(inv, bdiag(vb), P3)
        w = _bmm(inv, bdiag(kbg), P3)
    else:
        u, w = _solve_given_inv(m, vb, kbg, inv, bd2)
    v_new = u - _bmm(w, s)
    k_tail = kn * jnp.exp(glast_l - gc_l)
    cdec = jnp.sum(exp_gt * jnp.exp(glast), axis=-1, keepdims=True)
    s_new = s * cdec + bd2 * _bmm_tn(k_tail, v_new)
    a = _bmm_nt(qs, kns) * dec
    o = _bmm(qs * eg, s) + _bmm(a, bdiag(v_new))
    return s_new, o, inv


def _gdn_finish(o_f, o_b, z, norm_g, bd):
    o = o_f + o_b
    ms = _head_sum(o * o, bd) * (1.0 / HD)
    return o * lax.rsqrt(ms + EPS) * norm_g * _silu(z)


def _gelu(x):
    return 0.5 * x * (1.0 + jnp.tanh(0.7978845608028654 * (x + 0.044715 * (x * x * x))))


def _sg_chunk(u, v, z, w, b, hm4):
    u = _gelu(u)
    gv = _gelu(v)
    mu = jnp.mean(gv, axis=-1, keepdims=True)
    xc = gv - mu
    var = jnp.mean(xc * xc, axis=-1, keepdims=True)
    vn = xc * lax.rsqrt(var + EPS)
    s = _dot_tn(b, hm4, precision=HI)
    for h in range(NH):
        s = s + _dot(w[h], vn) * hm4[h:h + 1]
    return u * s * _silu(z)


def _make_shifts(t_ctx, n):
    def dn(x):
        t = lax.broadcasted_iota(jnp.int32, x.shape, 0)
        return jnp.where((t != 0) & (t != t_ctx), pltpu.roll(x, 1, axis=0), 0.0)

    def up(x):
        t = lax.broadcasted_iota(jnp.int32, x.shape, 0)
        return jnp.where((t != t_ctx - 1) & (t != n - 1), pltpu.roll(x, n - 1, axis=0), 0.0)

    @jax.custom_vjp
    def shift_dn(x):
        return dn(x)
    shift_dn.defvjp(lambda x: (dn(x), None), lambda _, g: (up(g),))

    @jax.custom_vjp
    def shift_up(x):
        return up(x)
    shift_up.defvjp(lambda x: (up(x), None), lambda _, g: (dn(g),))
    return shift_dn, shift_up


def _conv3(x, w, shift_dn, shift_up):
    return shift_dn(x) * w[0:1] + x * w[1:2] + shift_up(x) * w[2:3]


def inproj_fwd(x, shift_t, scale_t, g_pre, w_in, n_batch, sb):
    n = x.shape[0]

    def sel(i):
        return jnp.where(i % sb == 0, n_batch, i // sb)

    def body(x_ref, sh0, sh1, sc0, sc1, g_ref, w_ref, p_ref, h_ref):
        hs = []
        for k, (sh_ref, sc_ref) in enumerate(((sh0, sc0), (sh1, sc1))):
            xv = x_ref[k * TM:(k + 1) * TM, :]
            r = xv * lax.rsqrt(jnp.mean(xv * xv, axis=-1, keepdims=True) + EPS)
            hs.append(((r * g_ref[...]) * (1.0 + sc_ref[0]) + sh_ref[0]).astype(bf16))
        hb = jnp.concatenate(hs, axis=0)
        h_ref[...] = hb
        p_ref[...] = _dot(hb, w_ref[...])

    def mrow(k):
        return pl.BlockSpec((1, 1, D), lambda i: (sel(2 * i + k), 0, 0))

    return pl.pallas_call(
        body, name="inproj_fwd", grid=(n // TP,),
        in_specs=[pl.BlockSpec((TP, D), lambda i: (i, 0)), mrow(0), mrow(1), mrow(0), mrow(1),
                  _full((1, D)), _full((D, PW))],
        out_specs=[pl.BlockSpec((TP, PW), lambda i: (i, 0)), pl.BlockSpec((TP, D), lambda i: (i, 0))],
        out_shape=[_sds((n, PW)), _sds((n, D), bf16)],
        compiler_params=_cparams(("arbitrary",), VMEM_BIG),
    )(x, shift_t, shift_t, scale_t, scale_t, g_pre, w_in)


def outproj_fwd(ys, w_out, x, gate_t, g_post, n_batch, sb):
    n = x.shape[0]

    def sel(i):
        return jnp.where(i % sb == 0, n_batch, i // sb)

    def body(y0, y1, y2, y3, w_ref, x_ref, gt0, gt1, g_ref, xn_ref, o_ref):
        y = jnp.concatenate([y0[...], y1[...], y2[...], y3[...]], axis=1)
        o = _dot(y, w_ref[...])
        o_ref[...] = o
        nrm = o * lax.rsqrt(jnp.mean(o * o, axis=-1, keepdims=True) + EPS) * g_ref[...]
        for k, gt_ref in enumerate((gt0, gt1)):
            rows = slice(k * TM, (k + 1) * TM)
            xn_ref[rows, :] = x_ref[rows, :] + gt_ref[0] * nrm[rows]

    def mrow(k):
        return pl.BlockSpec((1, 1, D), lambda i: (sel(2 * i + k), 0, 0))

    yspec = pl.BlockSpec((TP, BW), lambda i: (i, 0))
    return pl.pallas_call(
        body, name="outproj_fwd", grid=(n // TP,),
        in_specs=[yspec, yspec, yspec, yspec, _full((D, D)), pl.BlockSpec((TP, D), lambda i: (i, 0)),
                  mrow(0), mrow(1), _full((1, D))],
        out_specs=[pl.BlockSpec((TP, D), lambda i: (i, 0)), pl.BlockSpec((TP, D), lambda i: (i, 0))],
        out_shape=[_sds((n, D)), _sds((n, D))],
        compiler_params=_cparams(("arbitrary",), VMEM_BIG),
    )(*ys, w_out, x, gate_t, gate_t, g_post)


def _row_onehot(r):
    return lax.broadcasted_iota(jnp.int32, (8, 1), 0) == r


def outproj_bwd(dxn, o, gate_t, g_post, ys, w_out, n_batch, sb):
    n = dxn.shape[0]

    def sel(i):
        return jnp.where(i % sb == 0, n_batch, i // sb)

    def body(dxn_ref, o_ref, gt0, gt1, g_ref, y0, y1, y2, y3, w_ref, dy_ref, dw_ref, dg_ref, dgate_ref):
        i = pl.program_id(0)

        @pl.when(i == 0)
        def _():
            dw_ref[...] = jnp.zeros_like(dw_ref)
            dg_ref[...] = jnp.zeros_like(dg_ref)
            dgate_ref[...] = jnp.zeros_like(dgate_ref)

        g = g_ref[...]
        dos = []
        for k, gt_ref in enumerate((gt0, gt1)):
            rows = slice(k * TM, (k + 1) * TM)
            ov = o_ref[rows, :]
            rstd = lax.rsqrt(jnp.mean(ov * ov, axis=-1, keepdims=True) + EPS)
            r = ov * rstd
            dx = dxn_ref[rows, :]
            dgate_ref[...] += jnp.where(_row_onehot(sel(2 * i + k)), jnp.sum(dx * (r * g), axis=0, keepdims=True), 0.0)
            dn = dx * gt_ref[0]
            dg_ref[...] += jnp.sum(dn * r, axis=0, keepdims=True)
            dr = dn * g
            dos.append((rstd * (dr - r * jnp.mean(dr * r, axis=-1, keepdims=True))).astype(bf16))
        dob = jnp.concatenate(dos, axis=0)
        dy_ref[...] = _dot_nt(dob, w_ref[...])
        y = jnp.concatenate([y0[...], y1[...], y2[...], y3[...]], axis=1)
        dw_ref[...] += _dot_tn(y, dob)

    def mrow(k):
        return pl.BlockSpec((1, 1, D), lambda i: (sel(2 * i + k), 0, 0))

    yspec = pl.BlockSpec((TP, BW), lambda i: (i, 0))
    row = pl.BlockSpec((TP, D), lambda i: (i, 0))
    return pl.pallas_call(
        body, name="outproj_bwd", grid=(n // TP,),
        in_specs=[row, row, mrow(0), mrow(1), _full((1, D)), yspec, yspec, yspec, yspec, _full((D, D))],
        out_specs=[row, _full((D, D)), _full((1, D)), _full((8, D))],
        out_shape=[_sds((n, D)), _sds((D, D)), _sds((1, D)), _sds((8, D))],
        compiler_params=_cparams(("arbitrary",), VMEM_BIG),
    )(dxn, o, gate_t, gate_t, g_post, *ys, w_out)


def inproj_bwd_x(dp, w_in, x, scale_t, g_pre, dxn, n_batch, sb):
    n = x.shape[0]

    def sel(i):
        return jnp.where(i % sb == 0, n_batch, i // sb)

    def body(dp_ref, w_ref, x_ref, sc0, sc1, g_ref, dxn_ref, dx_ref, dg_ref, dsh_ref, dsc_ref):
        i = pl.program_id(0)

        @pl.when(i == 0)
        def _():
            dg_ref[...] = jnp.zeros_like(dg_ref)
            dsh_ref[...] = jnp.zeros_like(dsh_ref)
            dsc_ref[...] = jnp.zeros_like(dsc_ref)

        dh_all = _dot_nt(dp_ref[...], w_ref[...])
        g = g_ref[...]
        for k, sc_ref in enumerate((sc0, sc1)):
            rows = slice(k * TM, (k + 1) * TM)
            dh = dh_all[rows]
            xv = x_ref[rows, :]
            rstd = lax.rsqrt(jnp.mean(xv * xv, axis=-1, keepdims=True) + EPS)
            r = xv * rstd
            hot = _row_onehot(sel(2 * i + k))
            dsh_ref[...] += jnp.where(hot, jnp.sum(dh, axis=0, keepdims=True), 0.0)
            dsc_ref[...] += jnp.where(hot, jnp.sum(dh * (r * g), axis=0, keepdims=True), 0.0)
            t = dh * (1.0 + sc_ref[0])
            dg_ref[...] += jnp.sum(t * r, axis=0, keepdims=True)
            dr = t * g
            dx_ref[rows, :] = dxn_ref[rows, :] + rstd * (dr - r * jnp.mean(dr * r, axis=-1, keepdims=True))

    def mrow(k):
        return pl.BlockSpec((1, 1, D), lambda i: (sel(2 * i + k), 0, 0))

    row = pl.BlockSpec((TP, D), lambda i: (i, 0))
    return pl.pallas_call(
        body, name="inproj_bwd_x", grid=(n // TP,),
        in_specs=[pl.BlockSpec((TP, PW), lambda i: (i, 0)), _full((D, PW)), row, mrow(0), mrow(1), _full((1, D)), row],
        out_specs=[row, _full((1, D)), _full((8, D)), _full((8, D))],
        out_shape=[_sds((n, D)), _sds((1, D)), _sds((8, D)), _sds((8, D))],
        compiler_params=_cparams(("arbitrary",), VMEM_BIG),
    )(dp, w_in, x, scale_t, scale_t, g_pre, dxn)


def dw_in(h, dp):
    n = h.shape[0]
    tk, tn = 512, 1024

    def body(h_ref, dp_ref, o_ref):
        @pl.when(pl.program_id(1) == 0)
        def _():
            o_ref[...] = jnp.zeros_like(o_ref)
        o_ref[...] += _dot_tn(h_ref[...], dp_ref[...])

    return pl.pallas_call(
        body, name="dw_in", grid=(PW // tn, n // tk),
        in_specs=[pl.BlockSpec((tk, D), lambda j, k: (k, 0)), pl.BlockSpec((tk, tn), lambda j, k: (k, j))],
        out_specs=pl.BlockSpec((D, tn), lambda j, k: (0, j)),
        out_shape=_sds((D, PW)),
        compiler_params=_cparams(("parallel", "arbitrary"), VMEM_BIG),
    )(h, dp)


def loss_head(xf, target, t_ctx):
    nb, s, _ = xf.shape
    jc = t_ctx // TM

    def body(x_ref, t_ref, dx_ref, l_ref):
        b, j = pl.program_id(0), pl.program_id(1)

        @pl.when((b == 0) & (j == 0))
        def _():
            l_ref[...] = jnp.zeros_like(l_ref)

        @pl.when(j < jc)
        def _():
            dx_ref[...] = jnp.zeros_like(dx_ref)

        @pl.when(j >= jc)
        def _():
            diff = x_ref[0] - t_ref[0]
            dx_ref[0] = diff * (1.0 / D)
            l_ref[...] += 0.5 * jnp.sum(diff * diff) * (1.0 / D)

    return pl.pallas_call(
        body, name="loss_head", grid=(nb, s // TM),
        in_specs=[pl.BlockSpec((1, TM, D), lambda b, j: (b, j, 0)),
                  pl.BlockSpec((1, TM, D), lambda b, j: (b, jnp.maximum(j - jc, 0), 0))],
        out_specs=[pl.BlockSpec((1, TM, D), lambda b, j: (b, j, 0)), _full((1, 128))],
        out_shape=[_sds((nb, s, D)), _sds((1, 128))],
        compiler_params=_cparams(("arbitrary", "arbitrary")),
    )(xf, target)


def _chunk_maps(n_ctx, n_lat):
    n = n_ctx + n_lat

    def cf(t):
        return t

    def cb(t):
        return jnp.where(t < n_ctx, n_ctx - 1 - t, n - 1 - t + n_ctx)
    return n, cf, cb


def ret_scan_fwd(p3, cos, sins, consts, t_ctx):
    nb, s, _ = p3.shape
    n, cf, cb = _chunk_maps(t_ctx // RC, (s - t_ctx) // RC)
    intra, qdec, kdec, cd, bd, hm = consts
    cmaps = (cf, cb)

    def body(qf, kf, vf, qb, kb, vb, cosf, sinf, cosb, sinb, intra_r, qdec_r, kdec_r, cd_r, bd_r, hm_r,
             of_ref, ob_ref, sall_ref, s_sc):
        @pl.when(pl.program_id(1) == 0)
        def _():
            s_sc[...] = jnp.zeros_like(s_sc)
        ins = ((qf, kf, vf, cosf, sinf, of_ref), (qb, kb, vb, cosb, sinb, ob_ref))
        for d, (q, k, v, c_, s_, o_ref) in enumerate(ins):
            st = s_sc[d]
            sall_ref[0, d, 0] = st
            s_new, o = _ret_step(st, q[0], k[0], v[0], c_[...], s_[...], intra_r[d], qdec_r[d], kdec_r[d],
                                 cd_r[...], bd_r[...], hm_r[...])
            s_sc[d] = s_new
            o_ref[0] = o

    def pspec(m, seg):
        return pl.BlockSpec((1, RC, BW), lambda b, t: (b, m(t), seg))

    def tspec(m):
        return pl.BlockSpec((RC, BW), lambda b, t: (m(t), 0))

    return pl.pallas_call(
        body, name="ret_scan_fwd", grid=(nb, n),
        in_specs=[pspec(cf, 0), pspec(cf, 1), pspec(cf, 2), pspec(cb, 0), pspec(cb, 1), pspec(cb, 2),
                  tspec(cf), tspec(cf), tspec(cb), tspec(cb),
                  _full(intra.shape), _full(qdec.shape), _full(kdec.shape), _full(cd.shape), _full(bd.shape),
                  _full(hm.shape)],
        out_specs=[pl.BlockSpec((1, RC, BW), lambda b, t: (b, cf(t), 0)),
                   pl.BlockSpec((1, RC, BW), lambda b, t: (b, cb(t), 0)),
                   pl.BlockSpec((1, 2, 1, BW, BW), lambda b, t: (b, 0, t, 0, 0))],
        out_shape=[_sds((nb, s, BW)), _sds((nb, s, BW)), _sds((nb, 2, n, BW, BW))],
        scratch_shapes=[pltpu.VMEM((2, BW, BW), f32)],
        compiler_params=_cparams(("arbitrary", "arbitrary")),
    )(p3, p3, p3, p3, p3, p3, cos, sins, cos, sins, intra, qdec, kdec, cd, bd, hm)


def ret_scan_bwd(p3, cos, sins, consts, s_all, do, t_ctx):
    nb, s, _ = p3.shape
    n, cf, cb = _chunk_maps(t_ctx // RC, (s - t_ctx) // RC)
    intra, qdec, kdec, cd, bd, hm = consts

    def rf(t):
        return cf(n - 1 - t)

    def rb(t):
        return cb(n - 1 - t)

    def body(qf, kf, vf, qb, kb, vb, cosf, sinf, cosb, sinb, intra_r, qdec_r, kdec_r, cd_r, bd_r, hm_r,
             sall_ref, dof, dob, dqf, dkf, dvf, dqb, dkb, dvb, ds_sc):
        @pl.when(pl.program_id(1) == 0)
        def _():
            ds_sc[...] = jnp.zeros_like(ds_sc)
        ins = ((qf, kf, vf, cosf, sinf, dof, (dqf, dkf, dvf)), (qb, kb, vb, cosb, sinb, dob, (dqb, dkb, dvb)))
        for d, (q, k, v, c_, s_, do_ref, outs) in enumerate(ins):
            step = functools.partial(_ret_step, cos=c_[...], sins=s_[...], intra=intra_r[d], qdec=qdec_r[d],
                                     kdec=kdec_r[d], cd=cd_r[...], bd=bd_r[...], hm=hm_r[...])
            _, vjp = jax.vjp(step, sall_ref[0, d, 0], q[0], k[0], v[0])
            ds, dq, dk, dv = vjp((ds_sc[d], do_ref[0]))
            ds_sc[d] = ds
            outs[0][0] = dq
            outs[1][0] = dk
            outs[2][0] = dv

    def pspec(m, seg):
        return pl.BlockSpec((1, RC, BW), lambda b, t: (b, m(t), seg))

    def tspec(m):
        return pl.BlockSpec((RC, BW), lambda b, t: (m(t), 0))

    def ospec(m):
        return pl.BlockSpec((1, RC, BW), lambda b, t: (b, m(t), 0))

    return pl.pallas_call(
        body, name="ret_scan_bwd", grid=(nb, n),
        in_specs=[pspec(rf, 0), pspec(rf, 1), pspec(rf, 2), pspec(rb, 0), pspec(rb, 1), pspec(rb, 2),
                  tspec(rf), tspec(rf), tspec(rb), tspec(rb),
                  _full(intra.shape), _full(qdec.shape), _full(kdec.shape), _full(cd.shape), _full(bd.shape),
                  _full(hm.shape),
                  pl.BlockSpec((1, 2, 1, BW, BW), lambda b, t: (b, 0, n - 1 - t, 0, 0)), ospec(rf), ospec(rb)],
        out_specs=[ospec(rf), ospec(rf), ospec(rf), ospec(rb), ospec(rb), ospec(rb)],
        out_shape=[_sds((nb, s, BW))] * 6,
        scratch_shapes=[pltpu.VMEM((2, BW, BW), f32)],
        compiler_params=_cparams(("arbitrary", "arbitrary")),
    )(p3, p3, p3, p3, p3, p3, cos, sins, cos, sins, intra, qdec, kdec, cd, bd, hm, s_all, do, do)


def mix_finish_fwd(fn, name, o_f, o_b, p3, zseg, norm_g, bd):
    nb, s, _ = p3.shape

    def body(of_ref, ob_ref, z_ref, g_ref, bd_ref, y_ref):
        y_ref[0] = fn(of_ref[0], ob_ref[0], z_ref[0], g_ref[...], bd_ref[...]).astype(bf16)

    blk = pl.BlockSpec((1, TM, BW), lambda b, j: (b, j, 0))
    return pl.pallas_call(
        body, name=name, grid=(nb, s // TM),
        in_specs=[blk, blk, pl.BlockSpec((1, TM, BW), lambda b, j: (b, j, zseg)), _full((1, BW)), _full((BW, BW))],
        out_specs=blk, out_shape=_sds((nb, s, BW), bf16),
        compiler_params=_cparams(("arbitrary", "arbitrary")),
    )(o_f, o_b, p3, norm_g, bd)


def mix_finish_bwd(fn, name, o_f, o_b, p3, zseg, norm_g, bd, dy3, yseg):
    nb, s, _ = p3.shape

    def body(of_ref, ob_ref, z_ref, g_ref, bd_ref, dy_ref, do_ref, dz_ref, dg_ref):
        @pl.when((pl.program_id(0) == 0) & (pl.program_id(1) == 0))
        def _():
            dg_ref[...] = jnp.zeros_like(dg_ref)
        bdv = bd_ref[...]
        _, vjp = jax.vjp(lambda a, b, z, g: fn(a, b, z, g, bdv), of_ref[0], ob_ref[0], z_ref[0], g_ref[...])
        do, _, dz, dg = vjp(dy_ref[0])
        do_ref[0] = do
        dz_ref[0] = dz
        dg_ref[...] += dg

    blk = pl.BlockSpec((1, TM, BW), lambda b, j: (b, j, 0))
    return pl.pallas_call(
        body, name=name, grid=(nb, s // TM),
        in_specs=[blk, blk, pl.BlockSpec((1, TM, BW), lambda b, j: (b, j, zseg)), _full((1, BW)), _full((BW, BW)),
                  pl.BlockSpec((1, TM, BW), lambda b, j: (b, j, yseg))],
        out_specs=[blk, blk, _full((1, BW))],
        out_shape=[_sds((nb, s, BW)), _sds((nb, s, BW)), _sds((1, BW))],
        compiler_params=_cparams(("arbitrary", "arbitrary")),
    )(o_f, o_b, p3, norm_g, bd, dy3)


def gdn_conv_fwd(p3, w, seg, t_ctx):
    nb, s, _ = p3.shape
    sd, su = _make_shifts(t_ctx, s)

    def body(x_ref, w_ref, o_ref):
        o_ref[0] = _silu(_conv3(x_ref[0], w_ref[...], sd, su))

    return pl.pallas_call(
        body, name="gdn_conv_fwd", grid=(nb, 2),
        in_specs=[pl.BlockSpec((1, s, 128), lambda b, j: (b, 0, 2 * seg + j)), pl.BlockSpec((3, 128), lambda b, j: (0, j))],
        out_specs=pl.BlockSpec((1, s, 128), lambda b, j: (b, 0, j)),
        out_shape=_sds((nb, s, BW)),
        compiler_params=_cparams(("arbitrary", "arbitrary")),
    )(p3, w)


def gdn_conv_bwd(p3, w, seg, d_f, d_b, t_ctx):
    nb, s, _ = p3.shape
    sd, su = _make_shifts(t_ctx, s)

    def body(x_ref, w_ref, df_ref, db_ref, dx_ref, dw_ref):
        @pl.when(pl.program_id(1) == 0)
        def _():
            dw_ref[...] = jnp.zeros_like(dw_ref)
        _, vjp = jax.vjp(lambda x, w_: _silu(_conv3(x, w_, sd, su)), x_ref[0], w_ref[...])
        dx, dw = vjp(df_ref[0] + db_ref[0])
        dx_ref[0] = dx
        dw_ref[...] += dw

    blk = pl.BlockSpec((1, s, 128), lambda j, b: (b, 0, j))
    return pl.pallas_call(
        body, name="gdn_conv_bwd", grid=(2, nb),
        in_specs=[pl.BlockSpec((1, s, 128), lambda j, b: (b, 0, 2 * seg + j)), pl.BlockSpec((3, 128), lambda j, b: (0, j)),
                  blk, blk],
        out_specs=[blk, pl.BlockSpec((3, 128), lambda j, b: (0, j))],
        out_shape=[_sds((nb, s, BW)), _sds((3, BW))],
        compiler_params=_cparams(("arbitrary", "arbitrary"), VMEM_BIG),
    )(p3, w, d_f, d_b)


def _pairs(f_ref, b_ref, nb):
    return jnp.stack([r[b, :, PAIR_W * p:PAIR_W * (p + 1)] for b in range(nb) for r in (f_ref, b_ref) for p in range(2)])


def _gates(f_ref, b_ref, nb):
    return jnp.stack([r[b] for b in range(nb) for r in (f_ref, b_ref)])


def _unpairs(a, f_ref, b_ref, nb):
    for b in range(nb):
        for d, r in enumerate((f_ref, b_ref)):
            for p in range(2):
                r[b, :, PAIR_W * p:PAIR_W * (p + 1)] = a[4 * b + 2 * d + p]


def _with_exchange(body, n_in, n_out, n_scratch, xchg, n_steps):
    if xchg is None:
        return body, [], [], [], []
    kind, arrs = xchg
    nx = len(arrs)

    def fused(*refs):
        ins, rest = refs[:n_in], refs[n_in:]
        srcs, rest = rest[:nx], rest[nx:]
        outs, rest = rest[:n_out], rest[n_out:]
        dsts, rest = rest[:nx], rest[nx:]
        scratch, sems = rest[:n_scratch], rest[n_scratch:]
        start, wait = _chip_exchange(kind, srcs, dsts, *sems)
        pl.when(pl.program_id(0) == 0)(start)
        body(*ins, *outs, *scratch)
        pl.when(pl.program_id(0) == n_steps - 1)(wait)

    any_ = pl.BlockSpec(memory_space=pl.ANY)
    return fused, [any_] * nx, [any_] * nx, _exchange_shapes(kind, arrs), _exchange_scratch(nx)


def gdn_scan_fwd(cq, ck, cv, p3, alog, dtb, consts, t_ctx, xchg=None):
    nb, s, _ = p3.shape
    n, cf, cb = _chunk_maps(t_ctx // GC, (s - t_ctx) // GC)
    gblk = GATE_COL // 128

    nz = 4 * nb

    def body(qf, kf, vf, gf, qb, kb, vb, gb, al_ref, dt_ref, tm_r, tm2_r, st2_r, eg_r, eb_r, egt_r, dsel_r, eye_r, bd_r,
             of_ref, ob_ref, sall_ref, inv_ref, s_sc):
        @pl.when(pl.program_id(0) == 0)
        def _():
            s_sc[...] = jnp.zeros_like(s_sc)
        st = s_sc[...]
        sall_ref[0] = st
        s_new, o, inv = _gdn_step(st, _pairs(qf, qb, nb), _pairs(kf, kb, nb), _pairs(vf, vb, nb), _gates(gf, gb, nb),
                                  al_ref[...], dt_ref[...], tm_r[...], tm2_r[...], st2_r[...], eg_r[...], eb_r[...],
                                  egt_r[...], dsel_r[...], eye_r[...], bd_r[...])
        s_sc[...] = s_new
        inv_ref[0] = inv
        _unpairs(o, of_ref, ob_ref, nb)

    def cspec(m):
        return pl.BlockSpec((nb, GC, BW), lambda t: (0, m(t), 0))

    def gspec(m):
        return pl.BlockSpec((nb, GC, 128), lambda t: (0, m(t), gblk))

    fused, x_in, x_out, x_shape, x_scratch = _with_exchange(body, 10 + len(consts), 4, 1, xchg, n)
    return pl.pallas_call(
        fused, name="gdn_scan_fwd" + ("" if xchg is None else "_" + xchg[0]), grid=(n,),
        in_specs=[cspec(cf), cspec(cf), cspec(cf), gspec(cf), cspec(cb), cspec(cb), cspec(cb), gspec(cb),
                  _full((1, 128)), _full((1, 128))] + [_full(c.shape) for c in consts] + x_in,
        out_specs=[cspec(cf), cspec(cb), pl.BlockSpec((1, nz, PAIR_W, PAIR_W), lambda t: (t, 0, 0, 0)),
                   pl.BlockSpec((1, nz, GC, PAIR_W), lambda t: (t, 0, 0, 0))] + x_out,
        out_shape=[_sds((nb, s, BW)), _sds((nb, s, BW)), _sds((n, nz, PAIR_W, PAIR_W)), _sds((n, nz, GC, PAIR_W))]
                  + x_shape,
        scratch_shapes=[pltpu.VMEM((nz, PAIR_W, PAIR_W), f32)] + x_scratch,
        compiler_params=_cparams(("arbitrary",)),
    )(cq, ck, cv, p3, cq, ck, cv, p3, alog, dtb, *consts, *([] if xchg is None else xchg[1]))


def gdn_scan_bwd(cq, ck, cv, p3, alog, dtb, consts, s_all, inv_all, do, t_ctx, xchg=None):
    nb, s, _ = p3.shape
    n, cf, cb = _chunk_maps(t_ctx // GC, (s - t_ctx) // GC)
    gblk = GATE_COL // 128

    def rf(t):
        return cf(n - 1 - t)

    def rb(t):
        return cb(n - 1 - t)

    nz = 4 * nb

    def body(qf, kf, vf, gf, qb, kb, vb, gb, al_ref, dt_ref, tm_r, tm2_r, st2_r, eg_r, eb_r, egt_r, dsel_r, eye_r, bd_r,
             sall_ref, inv_ref, dof, dob, dqf, dkf, dvf, dgf, dqb, dkb, dvb, dgb, dal_ref, ddt_ref, ds_sc):
        @pl.when(pl.program_id(0) == 0)
        def _():
            dal_ref[...] = jnp.zeros_like(dal_ref)
            ddt_ref[...] = jnp.zeros_like(ddt_ref)
            ds_sc[...] = jnp.zeros_like(ds_sc)
        consts = dict(tmask=tm_r[...], tmask2=tm2_r[...], strict2=st2_r[...], exp_g=eg_r[...], exp_b=eb_r[...],
                      exp_gt=egt_r[...], dsel2=dsel_r[...], eye2=eye_r[...], bd2=bd_r[...], inv=inv_ref[0])

        def step(*a):
            return _gdn_step(*a, **consts)[:2]

        _, vjp = jax.vjp(step, sall_ref[0], _pairs(qf, qb, nb), _pairs(kf, kb, nb), _pairs(vf, vb, nb),
                         _gates(gf, gb, nb), al_ref[...], dt_ref[...])
        ds, dq, dk, dv, dg, dal, ddt = vjp((ds_sc[...], _pairs(dof, dob, nb)))
        ds_sc[...] = ds
        _unpairs(dq, dqf, dqb, nb)
        _unpairs(dk, dkf, dkb, nb)
        _unpairs(dv, dvf, dvb, nb)
        for b in range(nb):
            dgf[b] = dg[2 * b]
            dgb[b] = dg[2 * b + 1]
        dal_ref[...] += dal
        ddt_ref[...] += ddt

    def cspec(m):
        return pl.BlockSpec((nb, GC, BW), lambda t: (0, m(t), 0))

    def gspec(m):
        return pl.BlockSpec((nb, GC, 128), lambda t: (0, m(t), gblk))

    def gout(m):
        return pl.BlockSpec((nb, GC, 128), lambda t: (0, m(t), 0))

    fused, x_in, x_out, x_shape, x_scratch = _with_exchange(body, 14 + len(consts), 10, 1, xchg, n)
    return pl.pallas_call(
        fused, name="gdn_scan_bwd" + ("" if xchg is None else "_" + xchg[0]), grid=(n,),
        in_specs=[cspec(rf), cspec(rf), cspec(rf), gspec(rf), cspec(rb), cspec(rb), cspec(rb), gspec(rb),
                  _full((1, 128)), _full((1, 128))] + [_full(c.shape) for c in consts]
                 + [pl.BlockSpec((1, nz, PAIR_W, PAIR_W), lambda t: (n - 1 - t, 0, 0, 0)),
                    pl.BlockSpec((1, nz, GC, PAIR_W), lambda t: (n - 1 - t, 0, 0, 0)), cspec(rf), cspec(rb)] + x_in,
        out_specs=[cspec(rf), cspec(rf), cspec(rf), gout(rf), cspec(rb), cspec(rb), cspec(rb), gout(rb),
                   _full((1, 128)), _full((1, 128))] + x_out,
        out_shape=[_sds((nb, s, BW))] * 3 + [_sds((nb, s, 128))] + [_sds((nb, s, BW))] * 3 + [_sds((nb, s, 128))]
                  + [_sds((1, 128)), _sds((1, 128))] + x_shape,
        scratch_shapes=[pltpu.VMEM((nz, PAIR_W, PAIR_W), f32)] + x_scratch,
        compiler_params=_cparams(("arbitrary",), VMEM_BIG),
    )(cq, ck, cv, p3, cq, ck, cv, p3, alog, dtb, *consts, s_all, inv_all, do, do, *([] if xchg is None else xchg[1]))


def sg_fwd(p3, w, b, hm4):
    nb, s, _ = p3.shape

    def body(u_ref, v_ref, z_ref, w_ref, b_ref, hm_ref, y_ref):
        y_ref[0] = _sg_chunk(u_ref[0], v_ref[0], z_ref[0], w_ref[...], b_ref[...], hm_ref[...]).astype(bf16)

    def seg(k):
        return pl.BlockSpec((1, RC, BW), lambda bi, i: (bi, i, k))

    return pl.pallas_call(
        body, name="sg_fwd", grid=(nb, s // RC),
        in_specs=[seg(4), seg(5), seg(6), _full((NH, RC, RC)), _full((NH, RC)), _full((NH, BW))],
        out_specs=pl.BlockSpec((1, RC, BW), lambda bi, i: (bi, i, 0)),
        out_shape=_sds((nb, s, BW), bf16),
        compiler_params=_cparams(("arbitrary", "arbitrary")),
    )(p3, p3, p3, w, b, hm4)


def sg_bwd(p3, w, b, hm4, dy3):
    nb, s, _ = p3.shape

    def body(u_ref, v_ref, z_ref, w_ref, b_ref, hm_ref, dy_ref, du_ref, dv_ref, dz_ref, dw_ref, db_ref):
        @pl.when((pl.program_id(0) == 0) & (pl.program_id(1) == 0))
        def _():
            dw_ref[...] = jnp.zeros_like(dw_ref)
            db_ref[...] = jnp.zeros_like(db_ref)
        hm = hm_ref[...]
        _, vjp = jax.vjp(lambda u, v, z, w_, b_: _sg_chunk(u, v, z, w_, b_, hm),
                         u_ref[0], v_ref[0], z_ref[0], w_ref[...], b_ref[...])
        du, dv, dz, dw, db = vjp(dy_ref[0])
        du_ref[0] = du
        dv_ref[0] = dv
        dz_ref[0] = dz
        dw_ref[...] += dw
        db_ref[...] += db

    def seg(k):
        return pl.BlockSpec((1, RC, BW), lambda bi, i: (bi, i, k))

    blk = pl.BlockSpec((1, RC, BW), lambda bi, i: (bi, i, 0))
    return pl.pallas_call(
        body, name="sg_bwd", grid=(nb, s // RC),
        in_specs=[seg(4), seg(5), seg(6), _full((NH, RC, RC)), _full((NH, RC)), _full((NH, BW)), seg(1)],
        out_specs=[blk, blk, blk, _full((NH, RC, RC)), _full((NH, RC))],
        out_shape=[_sds((nb, s, BW))] * 3 + [_sds((NH, RC, RC)), _sds((NH, RC))],
        compiler_params=_cparams(("arbitrary", "arbitrary")),
    )(p3, p3, p3, w, b, hm4, dy3)


def _sc_fn(b, c, h, z, w, sd, su):
    return b * _conv3(c * h, w, sd, su) * _silu(z)


def sc_fwd(p3, w, t_ctx):
    nb, s, _ = p3.shape
    sd, su = _make_shifts(t_ctx, s)

    def body(b_ref, c_ref, h_ref, z_ref, w_ref, y_ref):
        y_ref[0] = _sc_fn(b_ref[0], c_ref[0], h_ref[0], z_ref[0], w_ref[...], sd, su).astype(bf16)

    def seg(k):
        return pl.BlockSpec((1, s, 128), lambda bi, j: (bi, 0, 2 * k + j))

    return pl.pallas_call(
        body, name="sc_fwd", grid=(nb, 2),
        in_specs=[seg(7), seg(8), seg(9), seg(10), pl.BlockSpec((3, 128), lambda bi, j: (0, j))],
        out_specs=pl.BlockSpec((1, s, 128), lambda bi, j: (bi, 0, j)),
        out_shape=_sds((nb, s, BW), bf16),
        compiler_params=_cparams(("arbitrary", "arbitrary"), VMEM_BIG),
    )(p3, p3, p3, p3, w)


def sc_bwd(p3, w, dy3, t_ctx):
    nb, s, _ = p3.shape
    sd, su = _make_shifts(t_ctx, s)

    def body(b_ref, c_ref, h_ref, z_ref, w_ref, dy_ref, db_ref, dc_ref, dh_ref, dz_ref, dw_ref):
        @pl.when(pl.program_id(1) == 0)
        def _():
            dw_ref[...] = jnp.zeros_like(dw_ref)
        _, vjp = jax.vjp(lambda b, c, h, z, w_: _sc_fn(b, c, h, z, w_, sd, su),
                         b_ref[0], c_ref[0], h_ref[0], z_ref[0], w_ref[...])
        db, dc, dh, dz, dw = vjp(dy_ref[0])
        db_ref[0] = db
        dc_ref[0] = dc
        dh_ref[0] = dh
        dz_ref[0] = dz
        dw_ref[...] += dw

    def seg(k):
        return pl.BlockSpec((1, s, 128), lambda j, bi: (bi, 0, 2 * k + j))

    blk = pl.BlockSpec((1, s, 128), lambda j, bi: (bi, 0, j))
    wspec = pl.BlockSpec((3, 128), lambda j, bi: (0, j))
    return pl.pallas_call(
        body, name="sc_bwd", grid=(2, nb),
        in_specs=[seg(7), seg(8), seg(9), seg(10), wspec, seg(2)],
        out_specs=[blk, blk, blk, blk, wspec],
        out_shape=[_sds((nb, s, BW))] * 4 + [_sds((3, BW))],
        compiler_params=_cparams(("arbitrary", "arbitrary"), VMEM_BIG),
    )(p3, p3, p3, p3, w, dy3)


def assemble_dp(pairs, singles_a, gdn_x, singles_b, gates):
    nb, s, _ = singles_a[0].shape
    flat = [a for pr in pairs for a in pr] + list(singles_a) + list(gdn_x) + list(singles_b) + list(gates)
    n_pairs, n_a, n_x, n_b = len(pairs), len(singles_a), len(gdn_x), len(singles_b)

    def body(*refs):
        out = refs[-1]
        ins = refs[:-1]
        col = 0
        for p in range(n_pairs):
            out[0, :, col:col + BW] = (ins[2 * p][0] + ins[2 * p + 1][0]).astype(bf16)
            col += BW
        k = 2 * n_pairs
        for _ in range(n_a + n_x + n_b):
            out[0, :, col:col + BW] = ins[k][0].astype(bf16)
            col += BW
            k += 1
        out[0, :, col:col + 128] = (ins[k][0] + ins[k + 1][0]).astype(bf16)
        out[0, :, col + 128:] = jnp.zeros((TM, PW - col - 128), bf16)

    def spec(a):
        return pl.BlockSpec((1, TM, a.shape[-1]), lambda b, j: (b, j, 0))

    return pl.pallas_call(
        body, name="assemble_dp", grid=(nb, s // TM),
        in_specs=[spec(a) for a in flat],
        out_specs=pl.BlockSpec((1, TM, PW), lambda b, j: (b, j, 0)),
        out_shape=_sds((nb, s, PW), bf16),
        compiler_params=_cparams(("arbitrary", "arbitrary")),
    )(*flat)


def mod_fwd(c_rows, w_mod, b_cols):
    nl, _, wc = w_mod.shape
    nr = c_rows.shape[0]

    def body(c_ref, w_ref, b_ref, o_ref):
        o_ref[0] = _dot(_silu(c_ref[...]), w_ref[0], precision=HI) + b_ref[0]

    return pl.pallas_call(
        body, name="mod_fwd", grid=(nl,),
        in_specs=[_full((nr, D)), pl.BlockSpec((1, D, wc), lambda l: (l, 0, 0)), pl.BlockSpec((1, 1, wc), lambda l: (l, 0, 0))],
        out_specs=pl.BlockSpec((1, nr, wc), lambda l: (l, 0, 0)),
        out_shape=_sds((nl, nr, wc)),
        compiler_params=_cparams(("arbitrary",)),
    )(c_rows, w_mod, b_cols)


def mod_bwd(c_rows, w_mod, dm_cols, dm_full):
    nl, _, wc = w_mod.shape
    nr = c_rows.shape[0]

    def body(c_ref, w_ref, dmc_ref, dmf_ref, gw_ref, gb_ref, dcc_ref):
        @pl.when(pl.program_id(0) == 0)
        def _():
            dcc_ref[...] = jnp.zeros_like(dcc_ref)
        a = _silu(c_ref[...])
        dmc = dmc_ref[0]
        gw_ref[0] = _dot_tn(a, dmc, precision=HI)
        gb_ref[0] = jnp.sum(dmf_ref[0], axis=0, keepdims=True)
        dcc_ref[...] += _dot_nt(dmc[nr - 8:nr], w_ref[0], precision=HI)

    return pl.pallas_call(
        body, name="mod_bwd", grid=(nl,),
        in_specs=[_full((nr, D)), pl.BlockSpec((1, D, wc), lambda l: (l, 0, 0)),
                  pl.BlockSpec((1, nr, wc), lambda l: (l, 0, 0)), pl.BlockSpec((1, nr, 3 * D), lambda l: (l, 0, 0))],
        out_specs=[pl.BlockSpec((1, D, wc), lambda l: (l, 0, 0)), pl.BlockSpec((1, 1, 3 * D), lambda l: (l, 0, 0)),
                   _full((8, D))],
        out_shape=[_sds((nl, D, wc)), _sds((nl, 1, 3 * D)), _sds((8, D))],
        compiler_params=_cparams(("arbitrary",)),
    )(c_rows, w_mod, dm_cols, dm_full)


def cctx_grad(parts, c_ctx):
    def body(p_ref, c_ref, o_ref):
        tot = p_ref[0, 0:1, :]
        for k in (2, 4, 6):
            tot = tot + p_ref[k, 0:1, :]
        c = c_ref[...]
        sg = jax.nn.sigmoid(c)
        o_ref[...] = tot * (sg * (1.0 + c * (1.0 - sg)))

    return pl.pallas_call(body, name="cctx_grad", out_shape=_sds((1, D)))(parts, c_ctx)


def sum_lead(x, out_dtype=f32, tr=256, rows=None):
    k, r, c = x.shape
    r = r if rows is None else rows
    tr = min(tr, r)
    assert r % tr == 0

    def body(x_ref, o_ref):
        tot = x_ref[0].astype(f32)
        for i in range(1, k):
            tot = tot + x_ref[i].astype(f32)
        o_ref[...] = tot.astype(out_dtype)

    return pl.pallas_call(
        body, name="sum_lead", grid=(r // tr,),
        in_specs=[pl.BlockSpec((k, tr, c), lambda i: (0, i, 0))],
        out_specs=pl.BlockSpec((tr, c), lambda i: (i, 0)),
        out_shape=_sds((r, c), out_dtype),
        compiler_params=_cparams(("arbitrary",)),
    )(x)


def adamw(w, m, v, g1, g2=None, tr=256):
    r, c = w.shape
    tr = min(tr, r)
    assert r % tr == 0
    two = g2 is not None
    c1 = 1.0 / (1.0 - ADAM_B1 ** ADAM_STEP)
    c2 = 1.0 / (1.0 - ADAM_B2 ** ADAM_STEP)

    def body(*refs):
        w_ref, m_ref, v_ref, g_ref = refs[:4]
        g = g_ref[...]
        if two:
            g = g + refs[4][...]
        go_ref, d_ref, mo_ref, vo_ref = refs[-4:]
        mn = ADAM_B1 * m_ref[...] + (1.0 - ADAM_B1) * g
        vn = ADAM_B2 * v_ref[...] + (1.0 - ADAM_B2) * (g * g)
        go_ref[...] = g
        mo_ref[...] = mn
        vo_ref[...] = vn
        d_ref[...] = -ADAM_LR * ((mn * c1) / (jnp.sqrt(vn * c2) + ADAM_EPS) + ADAM_WD * w_ref[...])

    blk = pl.BlockSpec((tr, c), lambda i: (i, 0))
    args = [w, m, v, g1] + ([g2] if two else [])
    return pl.pallas_call(
        body, name="adamw", grid=(r // tr,),
        in_specs=[blk] * len(args), out_specs=[blk] * 4, out_shape=[_sds((r, c))] * 4,
        compiler_params=_cparams(("arbitrary",)),
    )(*args)


def _my_pos():
    return lax.axis_index("x"), lax.axis_index("y"), lax.axis_index("c")


def gather8(x):
    shape = x.shape

    def body(x_ref, out_ref, send_sems, recv_sems, local_sem):
        mx, my, mc = _my_pos()
        me = 4 * mx + 2 * my + mc
        mine = pltpu.make_async_copy(x_ref, out_ref.at[me], local_sem)
        mine.start()
        copies = []
        for k in range(1, N_DEV):
            peer = (mx ^ (k >> 2), my ^ ((k >> 1) & 1), mc ^ (k & 1))
            cp = pltpu.make_async_remote_copy(src_ref=x_ref, dst_ref=out_ref.at[me], send_sem=send_sems.at[k - 1],
                                              recv_sem=recv_sems.at[k - 1], device_id=peer, device_id_type=MESH)
            cp.start()
            copies.append(cp)
        for k in range(1, N_DEV):
            src = me ^ k
            pltpu.make_async_remote_copy(src_ref=x_ref, dst_ref=out_ref.at[src], send_sem=send_sems.at[k - 1],
                                         recv_sem=recv_sems.at[k - 1], device_id=(mx, my, mc),
                                         device_id_type=MESH).wait_recv()
        for cp in copies:
            cp.wait_send()
        mine.wait()

    return pl.pallas_call(
        body, name="gather8", out_shape=_sds((N_DEV,) + shape, x.dtype),
        in_specs=[pl.BlockSpec(memory_space=pl.ANY)], out_specs=pl.BlockSpec(memory_space=pl.ANY),
        scratch_shapes=[pltpu.SemaphoreType.DMA((N_DEV - 1,)), pltpu.SemaphoreType.DMA((N_DEV - 1,)),
                        pltpu.SemaphoreType.DMA(())],
    )(x)


def _chip_exchange(kind, src_refs, dst_refs, send_sems, recv_sems, local_sems):
    mx, my, mc = _my_pos()
    me = 2 * mx + my

    def copies():
        local, sends, recvs = [], [], []
        for i, (src, dst) in enumerate(zip(src_refs, dst_refs)):
            def part(k):
                return src if kind == "gather" else src.at[k]
            local.append(pltpu.make_async_copy(part(me), dst.at[me], local_sems.at[i]))
            for k in range(1, N_CHIPS):
                sem = dict(send_sem=send_sems.at[i, k - 1], recv_sem=recv_sems.at[i, k - 1], device_id_type=MESH)
                sends.append(pltpu.make_async_remote_copy(src_ref=part(me ^ k), dst_ref=dst.at[me],
                                                          device_id=(mx ^ (k >> 1), my ^ (k & 1), mc), **sem))
                recvs.append(pltpu.make_async_remote_copy(src_ref=part(me ^ k), dst_ref=dst.at[me ^ k],
                                                          device_id=(mx, my, mc), **sem))
        return local, sends, recvs

    def start():
        local, sends, _ = copies()
        for cp in local + sends:
            cp.start()

    def wait():
        local, sends, recvs = copies()
        for cp in recvs:
            cp.wait_recv()
        for cp in sends:
            cp.wait_send()
        for cp in local:
            cp.wait()

    return start, wait


def _exchange_scratch(n):
    return [pltpu.SemaphoreType.DMA((n, N_CHIPS - 1)), pltpu.SemaphoreType.DMA((n, N_CHIPS - 1)),
            pltpu.SemaphoreType.DMA((n,))]


def _exchange_shapes(kind, arrs):
    return [_sds(((N_CHIPS,) + a.shape) if kind == "gather" else a.shape, a.dtype) for a in arrs]


def exchange4(kind, arrs):
    n = len(arrs)

    def body(*refs):
        start, wait = _chip_exchange(kind, refs[:n], refs[n:2 * n], *refs[2 * n:])
        start()
        wait()

    any_ = pl.BlockSpec(memory_space=pl.ANY)
    return pl.pallas_call(
        body, name=kind + "4", out_shape=_exchange_shapes(kind, arrs),
        in_specs=[any_] * n, out_specs=[any_] * n, scratch_shapes=_exchange_scratch(n),
    )(*arrs)


def swap_sibling(x):
    def body(x_ref, out_ref, send_sem, recv_sem):
        mx, my, mc = _my_pos()
        cp = pltpu.make_async_remote_copy(src_ref=x_ref, dst_ref=out_ref, send_sem=send_sem, recv_sem=recv_sem,
                                          device_id=(mx, my, 1 - mc), device_id_type=MESH)
        cp.start()
        cp.wait()

    return pl.pallas_call(
        body, name="swap_sibling", out_shape=_sds(x.shape, x.dtype),
        in_specs=[pl.BlockSpec(memory_space=pl.ANY)], out_specs=pl.BlockSpec(memory_space=pl.ANY),
        scratch_shapes=[pltpu.SemaphoreType.DMA(()), pltpu.SemaphoreType.DMA(())],
    )(x)


PACK_ROWS = 64
SMALL = ("c_ctx", "b_mod", "g_pre", "g_post", "ret_norm_g", "sg_w", "sg_b", "sc_conv_w", "gdn_conv_w",
         "gdn_a_log", "gdn_dt_bias", "gdn_norm_g")


def _pack(arrs, width=D):
    rows = []
    for a in arrs:
        flat = a.reshape(-1)
        pad = (-flat.shape[0]) % width
        rows.append(jnp.pad(flat, (0, pad)).reshape(-1, width))
    out = jnp.concatenate(rows, axis=0)
    return jnp.pad(out, ((0, (-out.shape[0]) % PACK_ROWS), (0, 0)))


def _unpack(packed, shapes, width=D):
    outs, r = [], 0
    for shp in shapes:
        size = int(np.prod(shp))
        nr = -(-size // width)
        outs.append(packed[r:r + nr].reshape(-1)[:size].reshape(shp))
        r += nr
    return outs


def kernel(x, c, ctx, c_ctx, w_mod, b_mod, g_pre, g_post, w_in, w_out, ret_norm_g, sg_w, sg_b, sc_conv_w, gdn_conv_w, gdn_a_log, gdn_dt_bias, gdn_norm_g, loss_target, m_c_ctx, m_w_mod, m_b_mod, m_g_pre, m_g_post, m_w_in, m_w_out, m_ret_norm_g, m_sg_w, m_sg_b, m_sc_conv_w, m_gdn_conv_w, m_gdn_a_log, m_gdn_dt_bias, m_gdn_norm_g, v_c_ctx, v_w_mod, v_b_mod, v_g_pre, v_g_post, v_w_in, v_w_out, v_ret_norm_g, v_sg_w, v_sg_b, v_sc_conv_w, v_gdn_conv_w, v_gdn_a_log, v_gdn_dt_bias, v_gdn_norm_g):
    weights = dict(c_ctx=c_ctx, w_mod=w_mod, b_mod=b_mod, g_pre=g_pre, g_post=g_post, w_in=w_in, w_out=w_out,
                   ret_norm_g=ret_norm_g, sg_w=sg_w, sg_b=sg_b, sc_conv_w=sc_conv_w, gdn_conv_w=gdn_conv_w,
                   gdn_a_log=gdn_a_log, gdn_dt_bias=gdn_dt_bias, gdn_norm_g=gdn_norm_g)
    mom = dict(c_ctx=m_c_ctx, w_mod=m_w_mod, b_mod=m_b_mod, g_pre=m_g_pre, g_post=m_g_post, w_in=m_w_in,
               w_out=m_w_out, ret_norm_g=m_ret_norm_g, sg_w=m_sg_w, sg_b=m_sg_b, sc_conv_w=m_sc_conv_w,
               gdn_conv_w=m_gdn_conv_w, gdn_a_log=m_gdn_a_log, gdn_dt_bias=m_gdn_dt_bias, gdn_norm_g=m_gdn_norm_g)
    var = dict(c_ctx=v_c_ctx, w_mod=v_w_mod, b_mod=v_b_mod, g_pre=v_g_pre, g_post=v_g_post, w_in=v_w_in,
               w_out=v_w_out, ret_norm_g=v_ret_norm_g, sg_w=v_sg_w, sg_b=v_sg_b, sc_conv_w=v_sc_conv_w,
               gdn_conv_w=v_gdn_conv_w, gdn_a_log=v_gdn_a_log, gdn_dt_bias=v_gdn_dt_bias, gdn_norm_g=v_gdn_norm_g)

    nb, t_lat, _ = x.shape
    t_ctx = ctx.shape[1]
    s = t_ctx + t_lat
    n = nb * s
    sb = s // TM
    nl = w_in.shape[0]
    wc_in = w_in.shape[2]
    wc_mod = w_mod.shape[2]
    rows_out = w_out.shape[1]
    n_all = nb * N_DEV
    mx, my, mc = _my_pos()
    chip = 2 * mx + my
    dev = 2 * chip + mc

    hm = jnp.asarray(_head_masks())
    hm4 = hm[:, 0, :]
    bd = jnp.asarray(_block_diag())
    ret_c = _ret_consts() + [bd, hm]
    gdn_c = _gdn_consts(nb)
    cos, sins = _rope_tables(t_lat, t_ctx)

    pre = _pack([c, sc_conv_w, gdn_conv_w])
    pre_all = gather8(pre)
    c_parts, scw_parts, gcw_parts = [], [], []
    for k in range(N_DEV):
        ck, sk, gk = _unpack(pre_all[k], [c.shape, sc_conv_w.shape, gdn_conv_w.shape])
        c_parts.append(ck)
        if k % 2 == 0:
            scw_parts.append(sk)
            gcw_parts.append(gk)
    c_all = jnp.concatenate(c_parts, axis=0)
    sc_w_full = jnp.concatenate(scw_parts, axis=-1)
    gdn_w_full = jnp.concatenate(gcw_parts, axis=-1)
    c_rows = jnp.concatenate([c_all, c_ctx[None, :], jnp.zeros((7, D), f32)], axis=0)

    b_cols = lax.dynamic_slice_in_dim(b_mod, chip * wc_mod, wc_mod, axis=1)[:, None, :]
    mod_part = mod_fwd(c_rows, w_mod, b_cols)
    mod_all = gather8(mod_part)
    mod = jnp.concatenate([mod_all[2 * k] for k in range(N_CHIPS)], axis=-1)
    my_rows = jnp.concatenate([lax.dynamic_slice_in_dim(mod, dev * nb, nb, axis=1), mod[:, n_all:n_all + 1]], axis=1)
    shift_t = my_rows[:, :, None, 0:D]
    scale_t = my_rows[:, :, None, D:2 * D]
    gate_t = my_rows[:, :, None, 2 * D:3 * D]

    w_in_b, w_out_b = w_in.astype(bf16), w_out.astype(bf16)

    def full_weights(parts):
        wi = jnp.concatenate([parts[0][k] for k in range(N_CHIPS)], axis=-1)
        wo = jnp.concatenate([parts[1][k] for k in range(N_CHIPS)], axis=0)
        return jnp.pad(wi, ((0, 0), (0, PW - IN_W))), wo

    w_in_full, w_out_full = [None] * nl, [None] * nl
    w_in_full[0], w_out_full[0] = full_weights(exchange4("gather", [w_in_b[0], w_out_b[0]]))

    alog = jnp.pad(gdn_a_log.reshape(nl, 1, 8), ((0, 0), (0, 0), (0, 120)))
    dtb = jnp.pad(gdn_dt_bias.reshape(nl, 1, 8), ((0, 0), (0, 0), (0, 120)))
    gdn_ng = jnp.tile(gdn_norm_g, (1, NH))[:, None, :]
    ret_ng = ret_norm_g[:, None, :]

    xs = jnp.concatenate([ctx, x], axis=1).reshape(n, D)
    saved = []
    for l in range(nl):
        p, h = inproj_fwd(xs, shift_t[l], scale_t[l], g_pre[l][None, :], w_in_full[l], nb, sb)
        p3 = p.reshape(nb, s, PW)
        ro_f, ro_b, rs_all = ret_scan_fwd(p3, cos, sins, ret_c, t_ctx)
        y_ret = mix_finish_fwd(_ret_finish, "ret_finish_fwd", ro_f, ro_b, p3, 3, ret_ng[l], bd)
        y_sg = sg_fwd(p3, sg_w[l], sg_b[l], hm4)
        y_sc = sc_fwd(p3, sc_w_full[l], t_ctx)
        cq, ck, cv = [gdn_conv_fwd(p3, gdn_w_full[l][:, BW * i:BW * (i + 1)], 11 + i, t_ctx) for i in range(3)]
        nxt = None if l + 1 == nl else ("gather", [w_in_b[l + 1], w_out_b[l + 1]])
        go_f, go_b, *gs_all = gdn_scan_fwd(cq, ck, cv, p3, alog[l], dtb[l], gdn_c, t_ctx, nxt)
        if nxt is not None:
            w_in_full[l + 1], w_out_full[l + 1] = full_weights(gs_all[2:])
            gs_all = gs_all[:2]
        y_gdn = mix_finish_fwd(_gdn_finish, "gdn_finish_fwd", go_f, go_b, p3, 14, gdn_ng[l], bd)
        ys = [a.reshape(n, BW) for a in (y_ret, y_sg, y_sc, y_gdn)]
        x_new, o = outproj_fwd(ys, w_out_full[l], xs, gate_t[l], g_post[l][None, :], nb, sb)
        saved.append(dict(x=xs, h=h, p3=p3, ro=(ro_f, ro_b), rs=rs_all, c=(cq, ck, cv), go=(go_f, go_b), gs=gs_all,
                          ys=ys, o=o))
        xs = x_new

    dx3, loss_part = loss_head(xs.reshape(nb, s, D), loss_target, t_ctx)
    loss = lax.psum(loss_part[0, 0], ("x", "y", "c"))

    dxs = dx3.reshape(n, D)
    g_small = {k: [None] * nl for k in SMALL if k not in ("c_ctx", "b_mod")}
    dm_rows = [None] * nl
    slabs = None
    got_in, got_out = [None] * nl, [None] * nl
    for l in reversed(range(nl)):
        sv = saved[l]
        p3 = sv["p3"]
        dy, gw_out, dg_post, dgate = outproj_bwd(dxs, sv["o"], gate_t[l], g_post[l][None, :], sv["ys"], w_out_full[l], nb, sb)
        dy3 = dy.reshape(nb, s, D)
        r_do, r_dz, d_rng = mix_finish_bwd(_ret_finish, "ret_finish_bwd", *sv["ro"], p3, 3, ret_ng[l], bd, dy3, 0)
        r_d = ret_scan_bwd(p3, cos, sins, ret_c, sv["rs"], r_do, t_ctx)
        s_du, s_dv, s_dz, d_sgw, d_sgb = sg_bwd(p3, sg_w[l], sg_b[l], hm4, dy3)
        c_db, c_dc, c_dh, c_dz, d_scw = sc_bwd(p3, sc_w_full[l], dy3, t_ctx)
        g_do, g_dz, d_gng = mix_finish_bwd(_gdn_finish, "gdn_finish_bwd", *sv["go"], p3, 14, gdn_ng[l], bd, dy3, 3)
        g_d = gdn_scan_bwd(*sv["c"], p3, alog[l], dtb[l], gdn_c, *sv["gs"], g_do, t_ctx,
                           None if slabs is None else ("scatter", slabs))
        if slabs is not None:
            got_in[l + 1], got_out[l + 1] = g_d[10:]
        gx, d_gcw = [], []
        for i in range(3):
            dxi, dwi = gdn_conv_bwd(p3, gdn_w_full[l][:, BW * i:BW * (i + 1)], 11 + i, g_d[i], g_d[4 + i], t_ctx)
            gx.append(dxi)
            d_gcw.append(dwi)
        dp3 = assemble_dp([(r_d[0], r_d[3]), (r_d[1], r_d[4]), (r_d[2], r_d[5])],
                          [r_dz, s_du, s_dv, s_dz, c_db, c_dc, c_dh, c_dz], gx, [g_dz], [g_d[3], g_d[7]])
        dp = dp3.reshape(n, PW)
        dxs, dg_pre, dshift, dscale = inproj_bwd_x(dp, w_in_full[l], sv["x"], scale_t[l], g_pre[l][None, :], dxs, nb, sb)
        gw_in = dw_in(sv["h"], dp)
        slabs = [jnp.transpose(gw_in[:, :IN_W].reshape(D, N_CHIPS, wc_in), (1, 0, 2)).astype(bf16),
                 gw_out.reshape(N_CHIPS, rows_out, D).astype(bf16)]
        g_small["g_pre"][l] = dg_pre[0]
        g_small["g_post"][l] = dg_post[0]
        g_small["ret_norm_g"][l] = d_rng[0]
        g_small["sg_w"][l] = d_sgw
        g_small["sg_b"][l] = d_sgb
        g_small["sc_conv_w"][l] = d_scw
        g_small["gdn_conv_w"][l] = jnp.concatenate(d_gcw, axis=-1)
        g_small["gdn_a_log"][l] = g_d[8][0, :8].reshape(2, NH)
        g_small["gdn_dt_bias"][l] = g_d[9][0, :8].reshape(2, NH)
        g_small["gdn_norm_g"][l] = d_gng[0].reshape(NH, HD)
        dm_rows[l] = jnp.concatenate([dshift, dscale, dgate], axis=-1)[:nb + 1]
    grad_x = dxs.reshape(nb, s, D)[:, t_ctx:, :]

    g_small = {k: jnp.stack(v) for k, v in g_small.items()}
    dm_rows = jnp.stack(dm_rows)
    names2 = [k for k in SMALL if k not in ("c_ctx", "b_mod")]
    pack_sum = _pack([g_small[k] for k in names2] + [dm_rows[:, nb:]])
    pack_own = _pack([dm_rows[:, :nb]])
    all2 = gather8(jnp.concatenate([pack_sum, pack_own], axis=0))
    tot2 = sum_lead(all2, tr=PACK_ROWS, rows=pack_sum.shape[0])
    outs2 = _unpack(tot2, [g_small[k].shape for k in names2] + [(nl, 1, 3 * D)])
    grads = dict(zip(names2, outs2[:-1]))
    dm_own = jnp.stack([_unpack(all2[k, pack_sum.shape[0]:], [(nl, nb, 3 * D)])[0] for k in range(N_DEV)])
    dm_own = jnp.transpose(dm_own, (1, 0, 2, 3)).reshape(nl, n_all, 3 * D)
    dm_all = jnp.concatenate([dm_own, jnp.pad(outs2[-1], ((0, 0), (0, 7), (0, 0)))], axis=1)
    grads["gdn_norm_g"] = sum_lead(jnp.transpose(grads["gdn_norm_g"], (1, 0, 2)), tr=nl)
    for k in ("sc_conv_w", "gdn_conv_w"):
        wc = weights[k].shape[2]
        grads[k] = lax.dynamic_slice_in_dim(grads[k], chip * wc, wc, axis=2)

    dm_cols = lax.dynamic_slice_in_dim(dm_all, chip * wc_mod, wc_mod, axis=2)
    g_w_mod, g_b_mod, dcc_part = mod_bwd(c_rows, w_mod, dm_cols, dm_all)
    grads["b_mod"] = g_b_mod[:, 0, :]
    grads["c_ctx"] = cctx_grad(gather8(dcc_part), c_ctx[None, :])[0]

    got_in[0], got_out[0] = exchange4("scatter", slabs)
    gin_mine = jnp.concatenate([sum_lead(a) for a in got_in], axis=0)
    gin_sib = swap_sibling(gin_mine)
    gout_mine = jnp.concatenate([sum_lead(a) for a in got_out], axis=0)
    gout_sib = swap_sibling(gout_mine)

    res = {}
    res["w_in"] = [a.reshape(w_in.shape) for a in adamw(w_in.reshape(nl * D, wc_in), m_w_in.reshape(nl * D, wc_in),
                                                          v_w_in.reshape(nl * D, wc_in), gin_mine, gin_sib)]
    res["w_out"] = [a.reshape(w_out.shape) for a in adamw(w_out.reshape(nl * rows_out, D), m_w_out.reshape(nl * rows_out, D),
                                                            v_w_out.reshape(nl * rows_out, D), gout_mine, gout_sib)]
    res["w_mod"] = [a.reshape(w_mod.shape) for a in adamw(w_mod.reshape(nl * D, wc_mod), m_w_mod.reshape(nl * D, wc_mod),
                                                            v_w_mod.reshape(nl * D, wc_mod), g_w_mod.reshape(nl * D, wc_mod))]
    shapes = [weights[k].shape for k in SMALL]
    small = adamw(_pack([weights[k] for k in SMALL]), _pack([mom[k] for k in SMALL]), _pack([var[k] for k in SMALL]),
                  _pack([grads[k].reshape(weights[k].shape) for k in SMALL]), tr=PACK_ROWS)
    small = [_unpack(a, shapes) for a in small]
    for i, k in enumerate(SMALL):
        res[k] = [small[j][i] for j in range(4)]

    order = ["c_ctx", "w_mod", "b_mod", "g_pre", "g_post", "w_in", "w_out", "ret_norm_g", "sg_w", "sg_b", "sc_conv_w",
             "gdn_conv_w", "gdn_a_log", "gdn_dt_bias", "gdn_norm_g"]
    return (loss, grad_x, *[res[k][0] for k in order], *[res[k][1] for k in order], *[res[k][2] for k in order],
            *[res[k][3] for k in order])
```

```python
import functools

import jax
import jax.numpy as jnp
import numpy as np
from jax import lax
from jax.experimental import pallas as pl
from jax.experimental.pallas import tpu as pltpu

f32 = jnp.float32
bf16 = jnp.bfloat16
HI = lax.Precision.HIGHEST
P3 = lax.Precision.HIGH
MESH = pl.DeviceIdType.MESH

EPS = 1e-6
D = 1024
NH = 4
HD = 64
BW = NH * HD
PAIR_W = 2 * HD
RC = 128
GC = 64
GRID_W = 64
ROPE_BASE = 10000.0
IN_W = 15 * BW + 16
PW = 4096
GATE_COL = 15 * BW
N_CHIPS = 4
N_DEV = 8
TM = 256
TP = 2 * TM
ADAM_LR, ADAM_B1, ADAM_B2, ADAM_EPS, ADAM_WD, ADAM_STEP = 0.001, 0.9, 0.999, 1e-08, 0.01, 10
LANE_HEAD = np.arange(BW) // HD
VMEM_BIG = 56 * 1024 * 1024


def _dot(a, b, precision=None):
    return jnp.dot(a, b, precision=precision, preferred_element_type=f32)


def _dot_nt(a, b, precision=None):
    return lax.dot_general(a, b, (((1,), (1,)), ((), ())), precision=precision, preferred_element_type=f32)


def _dot_tn(a, b, precision=None):
    return lax.dot_general(a, b, (((0,), (0,)), ((), ())), precision=precision, preferred_element_type=f32)


def _sds(shape, dtype=f32):
    return jax.ShapeDtypeStruct(shape, dtype)


def _cparams(sem=None, vmem=None):
    kw = {}
    if sem is not None:
        kw["dimension_semantics"] = sem
    if vmem is not None:
        kw["vmem_limit_bytes"] = vmem
    return pltpu.CompilerParams(**kw)


def _full(shape):
    n = len(shape)
    return pl.BlockSpec(shape, lambda *_: (0,) * n)


def _head_masks():
    return np.stack([(LANE_HEAD == h).astype(np.float32)[None, :] for h in range(NH)])


def _block_diag():
    return (LANE_HEAD[:, None] == LANE_HEAD[None, :]).astype(np.float32)


def _tau(c, d):
    return np.arange(c) if d == 0 else c - 1 - np.arange(c)


def _ret_consts(nb):
    lg = np.log(1.0 - 2.0 ** (-5.0 - np.arange(NH)))
    intra = np.zeros((2, 2, RC, 2 * RC)); qdec = np.zeros((2, 2, RC, PAIR_W)); kdec = np.zeros((2, 2, RC, PAIR_W))
    cd = np.zeros((2, 2, PAIR_W, PAIR_W))
    for d in range(2):
        t = _tau(RC, d)
        diff = t[:, None] - t[None, :]
        for p in range(2):
            lane_lg = lg[2 * p + np.arange(PAIR_W) // HD]
            for h in range(2):
                intra[d, p, :, h * RC:(h + 1) * RC] = np.where(diff >= 0, np.exp(np.maximum(diff, 0) * lg[2 * p + h]), 0.0)
            qdec[d, p] = np.exp((t[:, None] + 1.0) * lane_lg[None, :])
            kdec[d, p] = np.exp((RC - 1.0 - t[:, None]) * lane_lg[None, :])
            cd[d, p] = np.exp(RC * lane_lg)[:, None] * np.ones((1, PAIR_W))
    per_z = [np.tile(a.reshape((4,) + a.shape[2:]), (nb, 1, 1)) for a in (intra, qdec, kdec, cd)]
    bd2 = (np.arange(PAIR_W)[:, None] // HD == np.arange(PAIR_W)[None, :] // HD)
    bdr = (np.arange(2 * RC)[:, None] // RC == np.arange(PAIR_W)[None, :] // HD)
    return [jnp.asarray(a, f32) for a in per_z + [bd2, bdr]]


def _rope_tables(t_lat, t_ctx):
    nf = HD // 4
    inv = ROPE_BASE ** (-np.arange(nf) / nf)
    pos = np.arange(t_lat)
    ang_r = (pos // GRID_W)[:, None] * inv[None, :]
    ang_c = (pos % GRID_W)[:, None] * inv[None, :]
    ang = np.concatenate([ang_r, ang_r, ang_c, ang_c], axis=1)
    sign = np.concatenate([-np.ones(nf), np.ones(nf), -np.ones(nf), np.ones(nf)])
    cos = np.tile(np.cos(ang), (1, 2)); sins = np.tile(np.sin(ang) * sign, (1, 2))
    cos = np.concatenate([np.ones((t_ctx, PAIR_W)), cos]); sins = np.concatenate([np.zeros((t_ctx, PAIR_W)), sins])
    return jnp.asarray(cos, f32), jnp.asarray(sins, f32)


def _gdn_consts(nb):
    tmask = np.zeros((2, 2, GC, GC)); tmask2 = np.zeros((2, 2, GC, PAIR_W)); strict2 = np.zeros((2, 2, GC, PAIR_W))
    exp_g = np.zeros((2, 2, 128, PAIR_W)); exp_b = np.zeros((2, 2, 128, PAIR_W))
    for d in range(2):
        t = _tau(GC, d)
        tmask[d, :] = (t[:, None] >= t[None, :])
        tmask2[d, :] = np.tile(t[:, None] >= t[None, :], (1, 2))
        strict2[d, :] = np.tile(t[:, None] > t[None, :], (1, 2))
        for h in range(NH):
            exp_g[d, h // 2, 4 * d + h, (h % 2) * HD:(h % 2 + 1) * HD] = 1.0
            exp_b[d, h // 2, 8 + 4 * d + h, (h % 2) * HD:(h % 2 + 1) * HD] = 1.0
    exp_gt = np.transpose(exp_g, (0, 1, 3, 2))
    per_z = [np.tile(a.reshape((4,) + a.shape[2:]), (nb, 1, 1)) for a in (tmask, tmask2, strict2, exp_g, exp_b, exp_gt)]
    dsel2 = np.tile(np.eye(GC), (1, 2))
    eye2 = np.tile(np.eye(GC), (1, 2))
    bd2 = (np.arange(PAIR_W)[:, None] // HD == np.arange(PAIR_W)[None, :] // HD)
    return [jnp.asarray(a, f32) for a in per_z + [dsel2, eye2, bd2]]


def _swap16(x):
    lane = lax.broadcasted_iota(jnp.int32, x.shape, x.ndim - 1)
    n = x.shape[-1]
    return jnp.where(lane % 32 < 16, pltpu.roll(x, n - 16, axis=x.ndim - 1), pltpu.roll(x, 16, axis=x.ndim - 1))


@jax.custom_vjp
def _rot(x, cos, sins):
    return x * cos + _swap16(x) * sins


def _rot_fwd(x, cos, sins):
    return _rot(x, cos, sins), (cos, sins)


def _rot_bwd(res, g):
    cos, sins = res
    return g * cos + _swap16(g * sins), jnp.zeros_like(cos), jnp.zeros_like(sins)


_rot.defvjp(_rot_fwd, _rot_bwd)


def _silu(z):
    return z * jax.nn.sigmoid(z)


def _head_sum(x, bd):
    return _dot(x, bd, precision=P3)


def _ret_step(s, q, k, v, cos, sins, intra, qdec, kdec, cd, bd2, bdr):
    def bdiag(x):
        return jnp.concatenate([x, x], axis=1) * bdr

    qr = _rot(q, cos, sins)
    kr = _rot(k, cos, sins) * (HD ** -0.5)
    sc = _bmm_nt(qr, bdiag(kr)) * intra
    o = _bmm(qr * qdec, s) + _bmm(sc, bdiag(v))
    s_new = s * cd + bd2 * _bmm_tn(kr * kdec, v)
    return s_new, o


def _ret_finish(o_f, o_b, z, norm_g, bd):
    o = o_f + o_b
    mu = _head_sum(o, bd) * (1.0 / HD)
    xc = o - mu
    var = _head_sum(xc * xc, bd) * (1.0 / HD)
    return xc * lax.rsqrt(var + EPS) * norm_g * _silu(z)


def _softplus(x):
    return jnp.maximum(x, 0.0) + jnp.log(1.0 + jnp.exp(-jnp.abs(x)))


def _bmm(a, b, precision=None):
    return lax.dot_general(a, b, (((2,), (1,)), ((0,), (0,))), precision=precision, preferred_element_type=f32)


def _bmm_nt(a, b, precision=None):
    return lax.dot_general(a, b, (((2,), (2,)), ((0,), (0,))), precision=precision, preferred_element_type=f32)


def _bmm_tn(a, b, precision=None):
    return lax.dot_general(a, b, (((1,), (1,)), ((0,), (0,))), precision=precision, preferred_element_type=f32)


def _bdiag(x, bd2):
    return jnp.concatenate([x, x], axis=1) * bd2


@jax.custom_vjp
def _solve_given_inv(m, vb, kbg, inv, bd2):
    return _bmm(inv, _bdiag(vb, bd2), P3), _bmm(inv, _bdiag(kbg, bd2), P3)


def _solve_fwd(m, vb, kbg, inv, bd2):
    u, w = _solve_given_inv(m, vb, kbg, inv, bd2)
    return (u, w), (inv, u, w, bd2)


def _solve_bwd(res, cts):
    inv, u, w, bd2 = res
    du, dw = cts
    c = inv.shape[1]
    t = jnp.swapaxes(_bdiag(inv, bd2), 1, 2)
    inv_t = t[:, :c] + t[:, c:]
    dvb = _bmm(inv_t, _bdiag(du, bd2), P3)
    dkbg = _bmm(inv_t, _bdiag(dw, bd2), P3)
    dm = _bmm_nt(dvb, _bdiag(u, bd2), P3) + _bmm_nt(dkbg, _bdiag(w, bd2), P3)
    return dm, dvb, dkbg, jnp.zeros_like(inv), jnp.zeros_like(bd2)


_solve_given_inv.defvjp(_solve_fwd, _solve_bwd)


def _gdn_step(s, q, k, v, gate, alog, dtb, tmask, tmask2, strict2, exp_g, exp_b, exp_gt, dsel2, eye2, bd2, inv=None):
    z, c, w_ = q.shape
    ne = gate.shape[0]

    def per_pair(a):
        return jnp.broadcast_to(a[:, None], (ne, z // ne) + a.shape[1:]).reshape((z,) + a.shape[1:])

    def rows(a):
        return a.reshape(z * c, w_)

    def bdiag(x):
        return _bdiag(x, bd2)

    g = per_pair(-jnp.exp(alog) * _softplus(gate + dtb))
    beta = per_pair(jax.nn.sigmoid(gate))
    gl = _bmm(g, exp_g, P3)
    gc_l = _bmm(tmask, gl, P3)
    glast_l = jnp.sum(gl, axis=1, keepdims=True)
    glast = jnp.sum(g, axis=1, keepdims=True)
    beta_l = _bmm(beta, exp_b, P3)
    gc_r = jnp.sum(gc_l * dsel2, axis=1, keepdims=True)
    qn = q * lax.rsqrt(_dot(rows(q * q), bd2, P3).reshape(z, c, w_) + EPS)
    kn = k * lax.rsqrt(_dot(rows(k * k), bd2, P3).reshape(z, c, w_) + EPS)
    eg = jnp.exp(gc_l)
    kb = kn * beta_l
    vb = v * beta_l
    kbg = kb * eg
    qs = qn * (HD ** -0.5)
    dec = jnp.exp(jnp.where(tmask2 > 0, gc_l - gc_r, -1e30))
    kns = bdiag(kn)
    m = -(_bmm_nt(kb, kns) * dec * strict2)
    if inv is None:
        inv = eye2 + m
        p = m
        for _ in range(5):
            p = _bmm(p, bdiag(p), P3)
            inv = inv + _bmm(inv, bdiag(p), P3)
        u = _bmm(inv, bdiag(vb), P3)
        w = _bmm(inv, bdiag(kbg), P3)
    else:
        u, w = _solve_given_inv(m, vb, kbg, inv, bd2)
    v_new = u - _bmm(w, s)
    k_tail = kn * jnp.exp(glast_l - gc_l)
    cdec = jnp.sum(exp_gt * jnp.exp(glast), axis=-1, keepdims=True)
    s_new = s * cdec + bd2 * _bmm_tn(k_tail, v_new)
    a = _bmm_nt(qs, kns) * dec
    o = _bmm(qs * eg, s) + _bmm(a, bdiag(v_new))
    return s_new, o, inv


def _gdn_finish(o_f, o_b, z, norm_g, bd):
    o = o_f + o_b
    ms = _head_sum(o * o, bd) * (1.0 / HD)
    return o * lax.rsqrt(ms + EPS) * norm_g * _silu(z)


def _gelu(x):
    return 0.5 * x * (1.0 + jnp.tanh(0.7978845608028654 * (x + 0.044715 * (x * x * x))))


def _sg_chunk(u, v, z, w, b, hm4):
    u = _gelu(u)
    gv = _gelu(v)
    mu = jnp.mean(gv, axis=-1, keepdims=True)
    xc = gv - mu
    var = jnp.mean(xc * xc, axis=-1, keepdims=True)
    vn = xc * lax.rsqrt(var + EPS)
    s = _dot_tn(b, hm4, precision=HI)
    for h in range(NH):
        s = s + _dot(w[h], vn) * hm4[h:h + 1]
    return u * s * _silu(z)


def _make_shifts(t_ctx, n):
    def dn(x):
        t = lax.broadcasted_iota(jnp.int32, x.shape, 0)
        return jnp.where((t != 0) & (t != t_ctx), pltpu.roll(x, 1, axis=0), 0.0)

    def up(x):
        t = lax.broadcasted_iota(jnp.int32, x.shape, 0)
        return jnp.where((t != t_ctx - 1) & (t != n - 1), pltpu.roll(x, n - 1, axis=0), 0.0)

    @jax.custom_vjp
    def shift_dn(x):
        return dn(x)
    shift_dn.defvjp(lambda x: (dn(x), None), lambda _, g: (up(g),))

    @jax.custom_vjp
    def shift_up(x):
        return up(x)
    shift_up.defvjp(lambda x: (up(x), None), lambda _, g: (dn(g),))
    return shift_dn, shift_up


def _conv3(x, w, shift_dn, shift_up):
    return shift_dn(x) * w[0:1] + x * w[1:2] + shift_up(x) * w[2:3]


def inproj_fwd(x, shift_t, scale_t, g_pre, w_in, n_batch, sb):
    n = x.shape[0]

    def sel(i):
        return jnp.where(i % sb == 0, n_batch, i // sb)

    def body(x_ref, sh0, sh1, sc0, sc1, g_ref, w_ref, p_ref, h_ref):
        hs = []
        for k, (sh_ref, sc_ref) in enumerate(((sh0, sc0), (sh1, sc1))):
            xv = x_ref[k * TM:(k + 1) * TM, :]
            r = xv * lax.rsqrt(jnp.mean(xv * xv, axis=-1, keepdims=True) + EPS)
            hs.append(((r * g_ref[...]) * (1.0 + sc_ref[0]) + sh_ref[0]).astype(bf16))
        hb = jnp.concatenate(hs, axis=0)
        h_ref[...] = hb
        p_ref[...] = _dot(hb, w_ref[...])

    def mrow(k):
        return pl.BlockSpec((1, 1, D), lambda i: (sel(2 * i + k), 0, 0))

    return pl.pallas_call(
        body, name="inproj_fwd", grid=(n // TP,),
        in_specs=[pl.BlockSpec((TP, D), lambda i: (i, 0)), mrow(0), mrow(1), mrow(0), mrow(1),
                  _full((1, D)), _full((D, PW))],
        out_specs=[pl.BlockSpec((TP, PW), lambda i: (i, 0)), pl.BlockSpec((TP, D), lambda i: (i, 0))],
        out_shape=[_sds((n, PW)), _sds((n, D), bf16)],
        compiler_params=_cparams(("arbitrary",), VMEM_BIG),
    )(x, shift_t, shift_t, scale_t, scale_t, g_pre, w_in)


def outproj_fwd(ys, w_out, x, gate_t, g_post, n_batch, sb):
    n = x.shape[0]

    def sel(i):
        return jnp.where(i % sb == 0, n_batch, i // sb)

    def body(y0, y1, y2, y3, w_ref, x_ref, gt0, gt1, g_ref, xn_ref, o_ref):
        y = jnp.concatenate([y0[...], y1[...], y2[...], y3[...]], axis=1)
        o = _dot(y, w_ref[...])
        o_ref[...] = o
        nrm = o * lax.rsqrt(jnp.mean(o * o, axis=-1, keepdims=True) + EPS) * g_ref[...]
        for k, gt_ref in enumerate((gt0, gt1)):
            rows = slice(k * TM, (k + 1) * TM)
            xn_ref[rows, :] = x_ref[rows, :] + gt_ref[0] * nrm[rows]

    def mrow(k):
        return pl.BlockSpec((1, 1, D), lambda i: (sel(2 * i + k), 0, 0))

    yspec = pl.BlockSpec((TP, BW), lambda i: (i, 0))
    return pl.pallas_call(
        body, name="outproj_fwd", grid=(n // TP,),
        in_specs=[yspec, yspec, yspec, yspec, _full((D, D)), pl.BlockSpec((TP, D), lambda i: (i, 0)),
                  mrow(0), mrow(1), _full((1, D))],
        out_specs=[pl.BlockSpec((TP, D), lambda i: (i, 0)), pl.BlockSpec((TP, D), lambda i: (i, 0))],
        out_shape=[_sds((n, D)), _sds((n, D))],
        compiler_params=_cparams(("arbitrary",), VMEM_BIG),
    )(*ys, w_out, x, gate_t, gate_t, g_post)


def _row_onehot(r):
    return lax.broadcasted_iota(jnp.int32, (8, 1), 0) == r


def outproj_bwd(dxn, o, gate_t, g_post, ys, w_out, n_batch, sb):
    n = dxn.shape[0]

    def sel(i):
        return jnp.where(i % sb == 0, n_batch, i // sb)

    def body(dxn_ref, o_ref, gt0, gt1, g_ref, y0, y1, y2, y3, w_ref, dy_ref, dw_ref, dg_ref, dgate_ref):
        i = pl.program_id(0)

        @pl.when(i == 0)
        def _():
            dw_ref[...] = jnp.zeros_like(dw_ref)
            dg_ref[...] = jnp.zeros_like(dg_ref)
            dgate_ref[...] = jnp.zeros_like(dgate_ref)

        g = g_ref[...]
        dos = []
        for k, gt_ref in enumerate((gt0, gt1)):
            rows = slice(k * TM, (k + 1) * TM)
            ov = o_ref[rows, :]
            rstd = lax.rsqrt(jnp.mean(ov * ov, axis=-1, keepdims=True) + EPS)
            r = ov * rstd
            dx = dxn_ref[rows, :]
            dgate_ref[...] += jnp.where(_row_onehot(sel(2 * i + k)), jnp.sum(dx * (r * g), axis=0, keepdims=True), 0.0)
            dn = dx * gt_ref[0]
            dg_ref[...] += jnp.sum(dn * r, axis=0, keepdims=True)
            dr = dn * g
            dos.append((rstd * (dr - r * jnp.mean(dr * r, axis=-1, keepdims=True))).astype(bf16))
        dob = jnp.concatenate(dos, axis=0)
        dy_ref[...] = _dot_nt(dob, w_ref[...])
        y = jnp.concatenate([y0[...], y1[...], y2[...], y3[...]], axis=1)
        dw_ref[...] += _dot_tn(y, dob)

    def mrow(k):
        return pl.BlockSpec((1, 1, D), lambda i: (sel(2 * i + k), 0, 0))

    yspec = pl.BlockSpec((TP, BW), lambda i: (i, 0))
    row = pl.BlockSpec((TP, D), lambda i: (i, 0))
    return pl.pallas_call(
        body, name="outproj_bwd", grid=(n // TP,),
        in_specs=[row, row, mrow(0), mrow(1), _full((1, D)), yspec, yspec, yspec, yspec, _full((D, D))],
        out_specs=[row, _full((D, D)), _full((1, D)), _full((8, D))],
        out_shape=[_sds((n, D)), _sds((D, D)), _sds((1, D)), _sds((8, D))],
        compiler_params=_cparams(("arbitrary",), VMEM_BIG),
    )(dxn, o, gate_t, gate_t, g_post, *ys, w_out)


def inproj_bwd_x(dp, w_in, x, scale_t, g_pre, dxn, n_batch, sb):
    n = x.shape[0]

    def sel(i):
        return jnp.where(i % sb == 0, n_batch, i // sb)

    def body(dp_ref, w_ref, x_ref, sc0, sc1, g_ref, dxn_ref, dx_ref, dg_ref, dsh_ref, dsc_ref):
        i = pl.program_id(0)

        @pl.when(i == 0)
        def _():
            dg_ref[...] = jnp.zeros_like(dg_ref)
            dsh_ref[...] = jnp.zeros_like(dsh_ref)
            dsc_ref[...] = jnp.zeros_like(dsc_ref)

        dh_all = _dot_nt(dp_ref[...], w_ref[...])
        g = g_ref[...]
        for k, sc_ref in enumerate((sc0, sc1)):
            rows = slice(k * TM, (k + 1) * TM)
            dh = dh_all[rows]
            xv = x_ref[rows, :]
            rstd = lax.rsqrt(jnp.mean(xv * xv, axis=-1, keepdims=True) + EPS)
            r = xv * rstd
            hot = _row_onehot(sel(2 * i + k))
            dsh_ref[...] += jnp.where(hot, jnp.sum(dh, axis=0, keepdims=True), 0.0)
            dsc_ref[...] += jnp.where(hot, jnp.sum(dh * (r * g), axis=0, keepdims=True), 0.0)
            t = dh * (1.0 + sc_ref[0])
            dg_ref[...] += jnp.sum(t * r, axis=0, keepdims=True)
            dr = t * g
            dx_ref[rows, :] = dxn_ref[rows, :] + rstd * (dr - r * jnp.mean(dr * r, axis=-1, keepdims=True))

    def mrow(k):
        return pl.BlockSpec((1, 1, D), lambda i: (sel(2 * i + k), 0, 0))

    row = pl.BlockSpec((TP, D), lambda i: (i, 0))
    return pl.pallas_call(
        body, name="inproj_bwd_x", grid=(n // TP,),
        in_specs=[pl.BlockSpec((TP, PW), lambda i: (i, 0)), _full((D, PW)), row, mrow(0), mrow(1), _full((1, D)), row],
        out_specs=[row, _full((1, D)), _full((8, D)), _full((8, D))],
        out_shape=[_sds((n, D)), _sds((1, D)), _sds((8, D)), _sds((8, D))],
        compiler_params=_cparams(("arbitrary",), VMEM_BIG),
    )(dp, w_in, x, scale_t, scale_t, g_pre, dxn)


def dw_in(h, dp):
    n = h.shape[0]
    tk, tn = (1536 if n % 1536 == 0 else 512), 1024

    def body(h_ref, dp_ref, o_ref):
        @pl.when(pl.program_id(1) == 0)
        def _():
            o_ref[...] = jnp.zeros_like(o_ref)
        o_ref[...] += _dot_tn(h_ref[...], dp_ref[...])

    return pl.pallas_call(
        body, name="dw_in", grid=(PW // tn, n // tk),
        in_specs=[pl.BlockSpec((tk, D), lambda j, k: (k, 0)), pl.BlockSpec((tk, tn), lambda j, k: (k, j))],
        out_specs=pl.BlockSpec((D, tn), lambda j, k: (0, j)),
        out_shape=_sds((D, PW)),
        compiler_params=_cparams(("parallel", "arbitrary"), VMEM_BIG),
    )(h, dp)


def loss_head(xf, target, t_ctx):
    nb, s, _ = xf.shape
    jc = t_ctx // TM

    def body(x_ref, t_ref, dx_ref, l_ref):
        b, j = pl.program_id(0), pl.program_id(1)

        @pl.when((b == 0) & (j == 0))
        def _():
            l_ref[...] = jnp.zeros_like(l_ref)

        @pl.when(j < jc)
        def _():
            dx_ref[...] = jnp.zeros_like(dx_ref)

        @pl.when(j >= jc)
        def _():
            diff = x_ref[0] - t_ref[0]
            dx_ref[0] = diff * (1.0 / D)
            l_ref[...] += 0.5 * jnp.sum(diff * diff) * (1.0 / D)

    return pl.pallas_call(
        body, name="loss_head", grid=(nb, s // TM),
        in_specs=[pl.BlockSpec((1, TM, D), lambda b, j: (b, j, 0)),
                  pl.BlockSpec((1, TM, D), lambda b, j: (b, jnp.maximum(j - jc, 0), 0))],
        out_specs=[pl.BlockSpec((1, TM, D), lambda b, j: (b, j, 0)), _full((1, 128))],
        out_shape=[_sds((nb, s, D)), _sds((1, 128))],
        compiler_params=_cparams(("arbitrary", "arbitrary")),
    )(xf, target)


def _chunk_maps(n_ctx, n_lat):
    n = n_ctx + n_lat

    def cf(t):
        return t

    def cb(t):
        return jnp.where(t < n_ctx, n_ctx - 1 - t, n - 1 - t + n_ctx)
    return n, cf, cb


def ret_scan_fwd(p3, cos, sins, consts, t_ctx):
    nb, s, _ = p3.shape
    n, cf, cb = _chunk_maps(t_ctx // RC, (s - t_ctx) // RC)
    nz = 4 * nb

    def body(qf, kf, vf, qb, kb, vb, cosf, sinf, cosb, sinb, intra_r, qdec_r, kdec_r, cd_r, bd_r, bdr_r,
             of_ref, ob_ref, sall_ref, s_sc):
        @pl.when(pl.program_id(0) == 0)
        def _():
            s_sc[...] = jnp.zeros_like(s_sc)
        st = s_sc[...]
        sall_ref[0] = st
        s_new, o = _ret_step(st, _pairs(qf, qb, nb), _pairs(kf, kb, nb), _pairs(vf, vb, nb),
                             _pair_tables(cosf, cosb, nb), _pair_tables(sinf, sinb, nb), intra_r[...], qdec_r[...],
                             kdec_r[...], cd_r[...], bd_r[...], bdr_r[...])
        s_sc[...] = s_new
        _unpairs(o, of_ref, ob_ref, nb)

    def pspec(m, seg):
        return pl.BlockSpec((nb, RC, BW), lambda t: (0, m(t), seg))

    def tspec(m):
        return pl.BlockSpec((RC, PAIR_W), lambda t: (m(t), 0))

    return pl.pallas_call(
        body, name="ret_scan_fwd", grid=(n,),
        in_specs=[pspec(cf, 0), pspec(cf, 1), pspec(cf, 2), pspec(cb, 0), pspec(cb, 1), pspec(cb, 2),
                  tspec(cf), tspec(cf), tspec(cb), tspec(cb)] + [_full(c.shape) for c in consts],
        out_specs=[pl.BlockSpec((nb, RC, BW), lambda t: (0, cf(t), 0)),
                   pl.BlockSpec((nb, RC, BW), lambda t: (0, cb(t), 0)),
                   pl.BlockSpec((1, nz, PAIR_W, PAIR_W), lambda t: (t, 0, 0, 0))],
        out_shape=[_sds((nb, s, BW)), _sds((nb, s, BW)), _sds((n, nz, PAIR_W, PAIR_W))],
        scratch_shapes=[pltpu.VMEM((nz, PAIR_W, PAIR_W), f32)],
        compiler_params=_cparams(("arbitrary",)),
    )(p3, p3, p3, p3, p3, p3, cos, sins, cos, sins, *consts)


def ret_scan_bwd(p3, cos, sins, consts, s_all, do, t_ctx):
    nb, s, _ = p3.shape
    n, cf, cb = _chunk_maps(t_ctx // RC, (s - t_ctx) // RC)
    nz = 4 * nb

    def rf(t):
        return cf(n - 1 - t)

    def rb(t):
        return cb(n - 1 - t)

    def body(qf, kf, vf, qb, kb, vb, cosf, sinf, cosb, sinb, intra_r, qdec_r, kdec_r, cd_r, bd_r, bdr_r,
             sall_ref, dof, dob, dqf, dkf, dvf, dqb, dkb, dvb, ds_sc):
        @pl.when(pl.program_id(0) == 0)
        def _():
            ds_sc[...] = jnp.zeros_like(ds_sc)
        step = functools.partial(_ret_step, cos=_pair_tables(cosf, cosb, nb), sins=_pair_tables(sinf, sinb, nb),
                                 intra=intra_r[...], qdec=qdec_r[...], kdec=kdec_r[...], cd=cd_r[...], bd2=bd_r[...],
                                 bdr=bdr_r[...])
        _, vjp = jax.vjp(step, sall_ref[0], _pairs(qf, qb, nb), _pairs(kf, kb, nb), _pairs(vf, vb, nb))
        ds, dq, dk, dv = vjp((ds_sc[...], _pairs(dof, dob, nb)))
        ds_sc[...] = ds
        _unpairs(dq, dqf, dqb, nb)
        _unpairs(dk, dkf, dkb, nb)
        _unpairs(dv, dvf, dvb, nb)

    def pspec(m, seg):
        return pl.BlockSpec((nb, RC, BW), lambda t: (0, m(t), seg))

    def tspec(m):
        return pl.BlockSpec((RC, PAIR_W), lambda t: (m(t), 0))

    def ospec(m):
        return pl.BlockSpec((nb, RC, BW), lambda t: (0, m(t), 0))

    return pl.pallas_call(
        body, name="ret_scan_bwd", grid=(n,),
        in_specs=[pspec(rf, 0), pspec(rf, 1), pspec(rf, 2), pspec(rb, 0), pspec(rb, 1), pspec(rb, 2),
                  tspec(rf), tspec(rf), tspec(rb), tspec(rb)] + [_full(c.shape) for c in consts]
                 + [pl.BlockSpec((1, nz, PAIR_W, PAIR_W), lambda t: (n - 1 - t, 0, 0, 0)), ospec(rf), ospec(rb)],
        out_specs=[ospec(rf), ospec(rf), ospec(rf), ospec(rb), ospec(rb), ospec(rb)],
        out_shape=[_sds((nb, s, BW))] * 6,
        scratch_shapes=[pltpu.VMEM((nz, PAIR_W, PAIR_W), f32)],
        compiler_params=_cparams(("arbitrary",), VMEM_BIG),
    )(p3, p3, p3, p3, p3, p3, cos, sins, cos, sins, *consts, s_all, do, do)


def mix_finish_fwd(fn, name, o_f, o_b, p3, zseg, norm_g, bd):
    nb, s, _ = p3.shape

    def body(of_ref, ob_ref, z_ref, g_ref, bd_ref, y_ref):
        y_ref[0] = fn(of_ref[0], ob_ref[0], z_ref[0], g_ref[...], bd_ref[...]).astype(bf16)

    blk = pl.BlockSpec((1, TM, BW), lambda b, j: (b, j, 0))
    return pl.pallas_call(
        body, name=name, grid=(nb, s // TM),
        in_specs=[blk, blk, pl.BlockSpec((1, TM, BW), lambda b, j: (b, j, zseg)), _full((1, BW)), _full((BW, BW))],
        out_specs=blk, out_shape=_sds((nb, s, BW), bf16),
        compiler_params=_cparams(("arbitrary", "arbitrary")),
    )(o_f, o_b, p3, norm_g, bd)


def mix_finish_bwd(fn, name, o_f, o_b, p3, zseg, norm_g, bd, dy3, yseg):
    nb, s, _ = p3.shape

    def body(of_ref, ob_ref, z_ref, g_ref, bd_ref, dy_ref, do_ref, dz_ref, dg_ref):
        @pl.when((pl.program_id(0) == 0) & (pl.program_id(1) == 0))
        def _():
            dg_ref[...] = jnp.zeros_like(dg_ref)
        bdv = bd_ref[...]
        _, vjp = jax.vjp(lambda a, b, z, g: fn(a, b, z, g, bdv), of_ref[0], ob_ref[0], z_ref[0], g_ref[...])
        do, _, dz, dg = vjp(dy_ref[0])
        do_ref[0] = do
        dz_ref[0] = dz
        dg_ref[...] += dg

    blk = pl.BlockSpec((1, TM, BW), lambda b, j: (b, j, 0))
    return pl.pallas_call(
        body, name=name, grid=(nb, s // TM),
        in_specs=[blk, blk, pl.BlockSpec((1, TM, BW), lambda b, j: (b, j, zseg)), _full((1, BW)), _full((BW, BW)),
                  pl.BlockSpec((1, TM, BW), lambda b, j: (b, j, yseg))],
        out_specs=[blk, blk, _full((1, BW))],
        out_shape=[_sds((nb, s, BW)), _sds((nb, s, BW)), _sds((1, BW))],
        compiler_params=_cparams(("arbitrary", "arbitrary")),
    )(o_f, o_b, p3, norm_g, bd, dy3)


def gdn_conv_fwd(p3, w, seg, t_ctx):
    nb, s, _ = p3.shape
    sd, su = _make_shifts(t_ctx, s)

    def body(x_ref, w_ref, o_ref):
        o_ref[0] = _silu(_conv3(x_ref[0], w_ref[...], sd, su))

    return pl.pallas_call(
        body, name="gdn_conv_fwd", grid=(nb, 2),
        in_specs=[pl.BlockSpec((1, s, 128), lambda b, j: (b, 0, 2 * seg + j)), pl.BlockSpec((3, 128), lambda b, j: (0, j))],
        out_specs=pl.BlockSpec((1, s, 128), lambda b, j: (b, 0, j)),
        out_shape=_sds((nb, s, BW)),
        compiler_params=_cparams(("arbitrary", "arbitrary")),
    )(p3, w)


def gdn_conv_bwd(p3, w, seg, d_f, d_b, t_ctx):
    nb, s, _ = p3.shape
    sd, su = _make_shifts(t_ctx, s)

    def body(x_ref, w_ref, df_ref, db_ref, dx_ref, dw_ref):
        @pl.when(pl.program_id(1) == 0)
        def _():
            dw_ref[...] = jnp.zeros_like(dw_ref)
        _, vjp = jax.vjp(lambda x, w_: _silu(_conv3(x, w_, sd, su)), x_ref[0], w_ref[...])
        dx, dw = vjp(df_ref[0] + db_ref[0])
        dx_ref[0] = dx
        dw_ref[...] += dw

    blk = pl.BlockSpec((1, s, 128), lambda j, b: (b, 0, j))
    return pl.pallas_call(
        body, name="gdn_conv_bwd", grid=(2, nb),
        in_specs=[pl.BlockSpec((1, s, 128), lambda j, b: (b, 0, 2 * seg + j)), pl.BlockSpec((3, 128), lambda j, b: (0, j)),
                  blk, blk],
        out_specs=[blk, pl.BlockSpec((3, 128), lambda j, b: (0, j))],
        out_shape=[_sds((nb, s, BW)), _sds((3, BW))],
        compiler_params=_cparams(("arbitrary", "arbitrary"), VMEM_BIG),
    )(p3, w, d_f, d_b)


def _pairs(f_ref, b_ref, nb):
    return jnp.stack([r[b, :, PAIR_W * p:PAIR_W * (p + 1)] for b in range(nb) for r in (f_ref, b_ref) for p in range(2)])


def _pair_tables(f_ref, b_ref, nb):
    return jnp.stack([r[...] for _ in range(nb) for r in (f_ref, b_ref) for _ in range(2)])


def _gates(f_ref, b_ref, nb):
    return jnp.stack([r[b] for b in range(nb) for r in (f_ref, b_ref)])


def _unpairs(a, f_ref, b_ref, nb):
    for b in range(nb):
        for d, r in enumerate((f_ref, b_ref)):
            for p in range(2):
                r[b, :, PAIR_W * p:PAIR_W * (p + 1)] = a[4 * b + 2 * d + p]


def _with_exchange(body, n_in, n_out, n_scratch, xchg, n_steps):
    if xchg is None:
        return body, [], [], [], []
    kind, arrs = xchg
    nx = len(arrs)

    def fused(*refs):
        ins, rest = refs[:n_in], refs[n_in:]
        srcs, rest = rest[:nx], rest[nx:]
        outs, rest = rest[:n_out], rest[n_out:]
        dsts, rest = rest[:nx], rest[nx:]
        scratch, sems = rest[:n_scratch], rest[n_scratch:]
        start, wait = _chip_exchange(kind, srcs, dsts, *sems)
        pl.when(pl.program_id(0) == 0)(start)
        body(*ins, *outs, *scratch)
        pl.when(pl.program_id(0) == n_steps - 1)(wait)

    any_ = pl.BlockSpec(memory_space=pl.ANY)
    return fused, [any_] * nx, [any_] * nx, _exchange_shapes(kind, arrs), _exchange_scratch(nx)


def gdn_scan_fwd(cq, ck, cv, p3, alog, dtb, consts, t_ctx, xchg=None):
    nb, s, _ = p3.shape
    n, cf, cb = _chunk_maps(t_ctx // GC, (s - t_ctx) // GC)
    gblk = GATE_COL // 128

    nz = 4 * nb

    def body(qf, kf, vf, gf, qb, kb, vb, gb, al_ref, dt_ref, tm_r, tm2_r, st2_r, eg_r, eb_r, egt_r, dsel_r, eye_r, bd_r,
             of_ref, ob_ref, sall_ref, inv_ref, s_sc):
        @pl.when(pl.program_id(0) == 0)
        def _():
            s_sc[...] = jnp.zeros_like(s_sc)
        st = s_sc[...]
        sall_ref[0] = st
        s_new, o, inv = _gdn_step(st, _pairs(qf, qb, nb), _pairs(kf, kb, nb), _pairs(vf, vb, nb), _gates(gf, gb, nb),
                                  al_ref[...], dt_ref[...], tm_r[...], tm2_r[...], st2_r[...], eg_r[...], eb_r[...],
                                  egt_r[...], dsel_r[...], eye_r[...], bd_r[...])
        s_sc[...] = s_new
        inv_ref[0] = inv
        _unpairs(o, of_ref, ob_ref, nb)

    def cspec(m):
        return pl.BlockSpec((nb, GC, BW), lambda t: (0, m(t), 0))

    def gspec(m):
        return pl.BlockSpec((nb, GC, 128), lambda t: (0, m(t), gblk))

    fused, x_in, x_out, x_shape, x_scratch = _with_exchange(body, 10 + len(consts), 4, 1, xchg, n)
    return pl.pallas_call(
        fused, name="gdn_scan_fwd" + ("" if xchg is None else "_" + xchg[0]), grid=(n,),
        in_specs=[cspec(cf), cspec(cf), cspec(cf), gspec(cf), cspec(cb), cspec(cb), cspec(cb), gspec(cb),
                  _full((1, 128)), _full((1, 128))] + [_full(c.shape) for c in consts] + x_in,
        out_specs=[cspec(cf), cspec(cb), pl.BlockSpec((1, nz, PAIR_W, PAIR_W), lambda t: (t, 0, 0, 0)),
                   pl.BlockSpec((1, nz, GC, PAIR_W), lambda t: (t, 0, 0, 0))] + x_out,
        out_shape=[_sds((nb, s, BW)), _sds((nb, s, BW)), _sds((n, nz, PAIR_W, PAIR_W)), _sds((n, nz, GC, PAIR_W))]
                  + x_shape,
        scratch_shapes=[pltpu.VMEM((nz, PAIR_W, PAIR_W), f32)] + x_scratch,
        compiler_params=_cparams(("arbitrary",)),
    )(cq, ck, cv, p3, cq, ck, cv, p3, alog, dtb, *consts, *([] if xchg is None else xchg[1]))


def gdn_scan_bwd(cq, ck, cv, p3, alog, dtb, consts, s_all, inv_all, do, t_ctx, xchg=None):
    nb, s, _ = p3.shape
    n, cf, cb = _chunk_maps(t_ctx // GC, (s - t_ctx) // GC)
    gblk = GATE_COL // 128

    def rf(t):
        return cf(n - 1 - t)

    def rb(t):
        return cb(n - 1 - t)

    nz = 4 * nb

    def body(qf, kf, vf, gf, qb, kb, vb, gb, al_ref, dt_ref, tm_r, tm2_r, st2_r, eg_r, eb_r, egt_r, dsel_r, eye_r, bd_r,
             sall_ref, inv_ref, dof, dob, dqf, dkf, dvf, dgf, dqb, dkb, dvb, dgb, dal_ref, ddt_ref, ds_sc):
        @pl.when(pl.program_id(0) == 0)
        def _():
            dal_ref[...] = jnp.zeros_like(dal_ref)
            ddt_ref[...] = jnp.zeros_like(ddt_ref)
            ds_sc[...] = jnp.zeros_like(ds_sc)
        consts = dict(tmask=tm_r[...], tmask2=tm2_r[...], strict2=st2_r[...], exp_g=eg_r[...], exp_b=eb_r[...],
                      exp_gt=egt_r[...], dsel2=dsel_r[...], eye2=eye_r[...], bd2=bd_r[...], inv=inv_ref[0])

        def step(*a):
            return _gdn_step(*a, **consts)[:2]

        _, vjp = jax.vjp(step, sall_ref[0], _pairs(qf, qb, nb), _pairs(kf, kb, nb), _pairs(vf, vb, nb),
                         _gates(gf, gb, nb), al_ref[...], dt_ref[...])
        ds, dq, dk, dv, dg, dal, ddt = vjp((ds_sc[...], _pairs(dof, dob, nb)))
        ds_sc[...] = ds
        _unpairs(dq, dqf, dqb, nb)
        _unpairs(dk, dkf, dkb, nb)
        _unpairs(dv, dvf, dvb, nb)
        for b in range(nb):
            dgf[b] = dg[2 * b]
            dgb[b] = dg[2 * b + 1]
        dal_ref[...] += dal
        ddt_ref[...] += ddt

    def cspec(m):
        return pl.BlockSpec((nb, GC, BW), lambda t: (0, m(t), 0))

    def gspec(m):
        return pl.BlockSpec((nb, GC, 128), lambda t: (0, m(t), gblk))

    def gout(m):
        return pl.BlockSpec((nb, GC, 128), lambda t: (0, m(t), 0))

    fused, x_in, x_out, x_shape, x_scratch = _with_exchange(body, 14 + len(consts), 10, 1, xchg, n)
    return pl.pallas_call(
        fused, name="gdn_scan_bwd" + ("" if xchg is None else "_" + xchg[0]), grid=(n,),
        in_specs=[cspec(rf), cspec(rf), cspec(rf), gspec(rf), cspec(rb), cspec(rb), cspec(rb), gspec(rb),
                  _full((1, 128)), _full((1, 128))] + [_full(c.shape) for c in consts]
                 + [pl.BlockSpec((1, nz, PAIR_W, PAIR_W), lambda t: (n - 1 - t, 0, 0, 0)),
                    pl.BlockSpec((1, nz, GC, PAIR_W), lambda t: (n - 1 - t, 0, 0, 0)), cspec(rf), cspec(rb)] + x_in,
        out_specs=[cspec(rf), cspec(rf), cspec(rf), gout(rf), cspec(rb), cspec(rb), cspec(rb), gout(rb),
                   _full((1, 128)), _full((1, 128))] + x_out,
        out_shape=[_sds((nb, s, BW))] * 3 + [_sds((nb, s, 128))] + [_sds((nb, s, BW))] * 3 + [_sds((nb, s, 128))]
                  + [_sds((1, 128)), _sds((1, 128))] + x_shape,
        scratch_shapes=[pltpu.VMEM((nz, PAIR_W, PAIR_W), f32)] + x_scratch,
        compiler_params=_cparams(("arbitrary",), VMEM_BIG),
    )(cq, ck, cv, p3, cq, ck, cv, p3, alog, dtb, *consts, s_all, inv_all, do, do, *([] if xchg is None else xchg[1]))


def sg_fwd(p3, w, b, hm4):
    nb, s, _ = p3.shape

    def body(u_ref, v_ref, z_ref, w_ref, b_ref, hm_ref, y_ref):
        y_ref[0] = _sg_chunk(u_ref[0], v_ref[0], z_ref[0], w_ref[...], b_ref[...], hm_ref[...]).astype(bf16)

    def seg(k):
        return pl.BlockSpec((1, RC, BW), lambda bi, i: (bi, i, k))

    return pl.pallas_call(
        body, name="sg_fwd", grid=(nb, s // RC),
        in_specs=[seg(4), seg(5), seg(6), _full((NH, RC, RC)), _full((NH, RC)), _full((NH, BW))],
        out_specs=pl.BlockSpec((1, RC, BW), lambda bi, i: (bi, i, 0)),
        out_shape=_sds((nb, s, BW), bf16),
        compiler_params=_cparams(("arbitrary", "arbitrary")),
    )(p3, p3, p3, w, b, hm4)


def sg_bwd(p3, w, b, hm4, dy3):
    nb, s, _ = p3.shape

    def body(u_ref, v_ref, z_ref, w_ref, b_ref, hm_ref, dy_ref, du_ref, dv_ref, dz_ref, dw_ref, db_ref):
        @pl.when((pl.program_id(0) == 0) & (pl.program_id(1) == 0))
        def _():
            dw_ref[...] = jnp.zeros_like(dw_ref)
            db_ref[...] = jnp.zeros_like(db_ref)
        hm = hm_ref[...]
        _, vjp = jax.vjp(lambda u, v, z, w_, b_: _sg_chunk(u, v, z, w_, b_, hm),
                         u_ref[0], v_ref[0], z_ref[0], w_ref[...], b_ref[...])
        du, dv, dz, dw, db = vjp(dy_ref[0])
        du_ref[0] = du
        dv_ref[0] = dv
        dz_ref[0] = dz
        dw_ref[...] += dw
        db_ref[...] += db

    def seg(k):
        return pl.BlockSpec((1, RC, BW), lambda bi, i: (bi, i, k))

    blk = pl.BlockSpec((1, RC, BW), lambda bi, i: (bi, i, 0))
    return pl.pallas_call(
        body, name="sg_bwd", grid=(nb, s // RC),
        in_specs=[seg(4), seg(5), seg(6), _full((NH, RC, RC)), _full((NH, RC)), _full((NH, BW)), seg(1)],
        out_specs=[blk, blk, blk, _full((NH, RC, RC)), _full((NH, RC))],
        out_shape=[_sds((nb, s, BW))] * 3 + [_sds((NH, RC, RC)), _sds((NH, RC))],
        compiler_params=_cparams(("arbitrary", "arbitrary")),
    )(p3, p3, p3, w, b, hm4, dy3)


def _sc_fn(b, c, h, z, w, sd, su):
    return b * _conv3(c * h, w, sd, su) * _silu(z)


def sc_fwd(p3, w, t_ctx):
    nb, s, _ = p3.shape
    sd, su = _make_shifts(t_ctx, s)

    def body(b_ref, c_ref, h_ref, z_ref, w_ref, y_ref):
        y_ref[0] = _sc_fn(b_ref[0], c_ref[0], h_ref[0], z_ref[0], w_ref[...], sd, su).astype(bf16)

    def seg(k):
        return pl.BlockSpec((1, s, 128), lambda bi, j: (bi, 0, 2 * k + j))

    return pl.pallas_call(
        body, name="sc_fwd", grid=(nb, 2),
        in_specs=[seg(7), seg(8), seg(9), seg(10), pl.BlockSpec((3, 128), lambda bi, j: (0, j))],
        out_specs=pl.BlockSpec((1, s, 128), lambda bi, j: (bi, 0, j)),
        out_shape=_sds((nb, s, BW), bf16),
        compiler_params=_cparams(("arbitrary", "arbitrary"), VMEM_BIG),
    )(p3, p3, p3, p3, w)


def sc_bwd(p3, w, dy3, t_ctx):
    nb, s, _ = p3.shape
    sd, su = _make_shifts(t_ctx, s)

    def body(b_ref, c_ref, h_ref, z_ref, w_ref, dy_ref, db_ref, dc_ref, dh_ref, dz_ref, dw_ref):
        @pl.when(pl.program_id(1) == 0)
        def _():
            dw_ref[...] = jnp.zeros_like(dw_ref)
        _, vjp = jax.vjp(lambda b, c, h, z, w_: _sc_fn(b, c, h, z, w_, sd, su),
                         b_ref[0], c_ref[0], h_ref[0], z_ref[0], w_ref[...])
        db, dc, dh, dz, dw = vjp(dy_ref[0])
        db_ref[0] = db
        dc_ref[0] = dc
        dh_ref[0] = dh
        dz_ref[0] = dz
        dw_ref[...] += dw

    def seg(k):
        return pl.BlockSpec((1, s, 128), lambda j, bi: (bi, 0, 2 * k + j))

    blk = pl.BlockSpec((1, s, 128), lambda j, bi: (bi, 0, j))
    wspec = pl.BlockSpec((3, 128), lambda j, bi: (0, j))
    return pl.pallas_call(
        body, name="sc_bwd", grid=(2, nb),
        in_specs=[seg(7), seg(8), seg(9), seg(10), wspec, seg(2)],
        out_specs=[blk, blk, blk, blk, wspec],
        out_shape=[_sds((nb, s, BW))] * 4 + [_sds((3, BW))],
        compiler_params=_cparams(("arbitrary", "arbitrary"), VMEM_BIG),
    )(p3, p3, p3, p3, w, dy3)


def assemble_dp(pairs, singles_a, gdn_x, singles_b, gates):
    nb, s, _ = singles_a[0].shape
    flat = [a for pr in pairs for a in pr] + list(singles_a) + list(gdn_x) + list(singles_b) + list(gates)
    n_pairs, n_a, n_x, n_b = len(pairs), len(singles_a), len(gdn_x), len(singles_b)

    def body(*refs):
        out = refs[-1]
        ins = refs[:-1]
        col = 0
        for p in range(n_pairs):
            out[0, :, col:col + BW] = (ins[2 * p][0] + ins[2 * p + 1][0]).astype(bf16)
            col += BW
        k = 2 * n_pairs
        for _ in range(n_a + n_x + n_b):
            out[0, :, col:col + BW] = ins[k][0].astype(bf16)
            col += BW
            k += 1
        out[0, :, col:col + 128] = (ins[k][0] + ins[k + 1][0]).astype(bf16)
        out[0, :, col + 128:] = jnp.zeros((TM, PW - col - 128), bf16)

    def spec(a):
        return pl.BlockSpec((1, TM, a.shape[-1]), lambda b, j: (b, j, 0))

    return pl.pallas_call(
        body, name="assemble_dp", grid=(nb, s // TM),
        in_specs=[spec(a) for a in flat],
        out_specs=pl.BlockSpec((1, TM, PW), lambda b, j: (b, j, 0)),
        out_shape=_sds((nb, s, PW), bf16),
        compiler_params=_cparams(("arbitrary", "arbitrary")),
    )(*flat)


def mod_fwd(c_rows, w_mod, b_cols):
    nl, _, wc = w_mod.shape
    nr = c_rows.shape[0]

    def body(c_ref, w_ref, b_ref, o_ref):
        o_ref[0] = _dot(_silu(c_ref[...]), w_ref[0], precision=HI) + b_ref[0]

    return pl.pallas_call(
        body, name="mod_fwd", grid=(nl,),
        in_specs=[_full((nr, D)), pl.BlockSpec((1, D, wc), lambda l: (l, 0, 0)), pl.BlockSpec((1, 1, wc), lambda l: (l, 0, 0))],
        out_specs=pl.BlockSpec((1, nr, wc), lambda l: (l, 0, 0)),
        out_shape=_sds((nl, nr, wc)),
        compiler_params=_cparams(("arbitrary",)),
    )(c_rows, w_mod, b_cols)


def mod_bwd(c_rows, w_mod, dm_cols, dm_full):
    nl, _, wc = w_mod.shape
    nr = c_rows.shape[0]

    def body(c_ref, w_ref, dmc_ref, dmf_ref, gw_ref, gb_ref, dcc_ref):
        @pl.when(pl.program_id(0) == 0)
        def _():
            dcc_ref[...] = jnp.zeros_like(dcc_ref)
        a = _silu(c_ref[...])
        dmc = dmc_ref[0]
        gw_ref[0] = _dot_tn(a, dmc, precision=HI)
        gb_ref[0] = jnp.sum(dmf_ref[0], axis=0, keepdims=True)
        dcc_ref[...] += _dot_nt(dmc[nr - 8:nr], w_ref[0], precision=HI)

    return pl.pallas_call(
        body, name="mod_bwd", grid=(nl,),
        in_specs=[_full((nr, D)), pl.BlockSpec((1, D, wc), lambda l: (l, 0, 0)),
                  pl.BlockSpec((1, nr, wc), lambda l: (l, 0, 0)), pl.BlockSpec((1, nr, 3 * D), lambda l: (l, 0, 0))],
        out_specs=[pl.BlockSpec((1, D, wc), lambda l: (l, 0, 0)), pl.BlockSpec((1, 1, 3 * D), lambda l: (l, 0, 0)),
                   _full((8, D))],
        out_shape=[_sds((nl, D, wc)), _sds((nl, 1, 3 * D)), _sds((8, D))],
        compiler_params=_cparams(("arbitrary",)),
    )(c_rows, w_mod, dm_cols, dm_full)


def cctx_grad(parts, c_ctx):
    def body(p_ref, c_ref, o_ref):
        tot = p_ref[0, 0:1, :]
        for k in (2, 4, 6):
            tot = tot + p_ref[k, 0:1, :]
        c = c_ref[...]
        sg = jax.nn.sigmoid(c)
        o_ref[...] = tot * (sg * (1.0 + c * (1.0 - sg)))

    return pl.pallas_call(body, name="cctx_grad", out_shape=_sds((1, D)))(parts, c_ctx)


def sum_lead(x, out_dtype=f32, tr=256, rows=None):
    k, r, c = x.shape
    r = r if rows is None else rows
    tr = min(tr, r)
    assert r % tr == 0

    def body(x_ref, o_ref):
        tot = x_ref[0].astype(f32)
        for i in range(1, k):
            tot = tot + x_ref[i].astype(f32)
        o_ref[...] = tot.astype(out_dtype)

    return pl.pallas_call(
        body, name="sum_lead", grid=(r // tr,),
        in_specs=[pl.BlockSpec((k, tr, c), lambda i: (0, i, 0))],
        out_specs=pl.BlockSpec((tr, c), lambda i: (i, 0)),
        out_shape=_sds((r, c), out_dtype),
        compiler_params=_cparams(("arbitrary",)),
    )(x)


def adamw(w, m, v, g1, g2=None, tr=256):
    r, c = w.shape
    tr = min(tr, r)
    assert r % tr == 0
    two = g2 is not None
    c1 = 1.0 / (1.0 - ADAM_B1 ** ADAM_STEP)
    c2 = 1.0 / (1.0 - ADAM_B2 ** ADAM_STEP)

    def body(*refs):
        w_ref, m_ref, v_ref, g_ref = refs[:4]
        g = g_ref[...]
        if two:
            g = g + refs[4][...]
        go_ref, d_ref, mo_ref, vo_ref = refs[-4:]
        mn = ADAM_B1 * m_ref[...] + (1.0 - ADAM_B1) * g
        vn = ADAM_B2 * v_ref[...] + (1.0 - ADAM_B2) * (g * g)
        go_ref[...] = g
        mo_ref[...] = mn
        vo_ref[...] = vn
        d_ref[...] = -ADAM_LR * ((mn * c1) / (jnp.sqrt(vn * c2) + ADAM_EPS) + ADAM_WD * w_ref[...])

    blk = pl.BlockSpec((tr, c), lambda i: (i, 0))
    args = [w, m, v, g1] + ([g2] if two else [])
    return pl.pallas_call(
        body, name="adamw", grid=(r // tr,),
        in_specs=[blk] * len(args), out_specs=[blk] * 4, out_shape=[_sds((r, c))] * 4,
        compiler_params=_cparams(("arbitrary",)),
    )(*args)


def _my_pos():
    return lax.axis_index("x"), lax.axis_index("y"), lax.axis_index("c")


def gather8(x):
    shape = x.shape

    def body(x_ref, out_ref, send_sems, recv_sems, local_sem):
        mx, my, mc = _my_pos()
        me = 4 * mx + 2 * my + mc
        mine = pltpu.make_async_copy(x_ref, out_ref.at[me], local_sem)
        mine.start()
        copies = []
        for k in range(1, N_DEV):
            peer = (mx ^ (k >> 2), my ^ ((k >> 1) & 1), mc ^ (k & 1))
            cp = pltpu.make_async_remote_copy(src_ref=x_ref, dst_ref=out_ref.at[me], send_sem=send_sems.at[k - 1],
                                              recv_sem=recv_sems.at[k - 1], device_id=peer, device_id_type=MESH)
            cp.start()
            copies.append(cp)
        for k in range(1, N_DEV):
            src = me ^ k
            pltpu.make_async_remote_copy(src_ref=x_ref, dst_ref=out_ref.at[src], send_sem=send_sems.at[k - 1],
                                         recv_sem=recv_sems.at[k - 1], device_id=(mx, my, mc),
                                         device_id_type=MESH).wait_recv()
        for cp in copies:
            cp.wait_send()
        mine.wait()

    return pl.pallas_call(
        body, name="gather8", out_shape=_sds((N_DEV,) + shape, x.dtype),
        in_specs=[pl.BlockSpec(memory_space=pl.ANY)], out_specs=pl.BlockSpec(memory_space=pl.ANY),
        scratch_shapes=[pltpu.SemaphoreType.DMA((N_DEV - 1,)), pltpu.SemaphoreType.DMA((N_DEV - 1,)),
                        pltpu.SemaphoreType.DMA(())],
    )(x)


def _chip_exchange(kind, src_refs, dst_refs, send_sems, recv_sems, local_sems):
    mx, my, mc = _my_pos()
    me = 2 * mx + my

    def copies():
        local, sends, recvs = [], [], []
        for i, (src, dst) in enumerate(zip(src_refs, dst_refs)):
            def part(k):
                return src if kind == "gather" else src.at[k]
            local.append(pltpu.make_async_copy(part(me), dst.at[me], local_sems.at[i]))
            for k in range(1, N_CHIPS):
                sem = dict(send_sem=send_sems.at[i, k - 1], recv_sem=recv_sems.at[i, k - 1], device_id_type=MESH)
                sends.append(pltpu.make_async_remote_copy(src_ref=part(me ^ k), dst_ref=dst.at[me],
                                                          device_id=(mx ^ (k >> 1), my ^ (k & 1), mc), **sem))
                recvs.append(pltpu.make_async_remote_copy(src_ref=part(me ^ k), dst_ref=dst.at[me ^ k],
                                                          device_id=(mx, my, mc), **sem))
        return local, sends, recvs

    def start():
        local, sends, _ = copies()
        for cp in local + sends:
            cp.start()

    def wait():
        local, sends, recvs = copies()
        for cp in recvs:
            cp.wait_recv()
        for cp in sends:
            cp.wait_send()
        for cp in local:
            cp.wait()

    return start, wait


def _exchange_scratch(n):
    return [pltpu.SemaphoreType.DMA((n, N_CHIPS - 1)), pltpu.SemaphoreType.DMA((n, N_CHIPS - 1)),
            pltpu.SemaphoreType.DMA((n,))]


def _exchange_shapes(kind, arrs):
    return [_sds(((N_CHIPS,) + a.shape) if kind == "gather" else a.shape, a.dtype) for a in arrs]


def exchange4(kind, arrs):
    n = len(arrs)

    def body(*refs):
        start, wait = _chip_exchange(kind, refs[:n], refs[n:2 * n], *refs[2 * n:])
        start()
        wait()

    any_ = pl.BlockSpec(memory_space=pl.ANY)
    return pl.pallas_call(
        body, name=kind + "4", out_shape=_exchange_shapes(kind, arrs),
        in_specs=[any_] * n, out_specs=[any_] * n, scratch_shapes=_exchange_scratch(n),
    )(*arrs)


def swap_sibling(x):
    def body(x_ref, out_ref, send_sem, recv_sem):
        mx, my, mc = _my_pos()
        cp = pltpu.make_async_remote_copy(src_ref=x_ref, dst_ref=out_ref, send_sem=send_sem, recv_sem=recv_sem,
                                          device_id=(mx, my, 1 - mc), device_id_type=MESH)
        cp.start()
        cp.wait()

    return pl.pallas_call(
        body, name="swap_sibling", out_shape=_sds(x.shape, x.dtype),
        in_specs=[pl.BlockSpec(memory_space=pl.ANY)], out_specs=pl.BlockSpec(memory_space=pl.ANY),
        scratch_shapes=[pltpu.SemaphoreType.DMA(()), pltpu.SemaphoreType.DMA(())],
    )(x)


PACK_ROWS = 64
SMALL = ("c_ctx", "b_mod", "g_pre", "g_post", "ret_norm_g", "sg_w", "sg_b", "sc_conv_w", "gdn_conv_w",
         "gdn_a_log", "gdn_dt_bias", "gdn_norm_g")


def _pack(arrs, width=D):
    rows = []
    for a in arrs:
        flat = a.reshape(-1)
        pad = (-flat.shape[0]) % width
        rows.append(jnp.pad(flat, (0, pad)).reshape(-1, width))
    out = jnp.concatenate(rows, axis=0)
    return jnp.pad(out, ((0, (-out.shape[0]) % PACK_ROWS), (0, 0)))


def _unpack(packed, shapes, width=D):
    outs, r = [], 0
    for shp in shapes:
        size = int(np.prod(shp))
        nr = -(-size // width)
        outs.append(packed[r:r + nr].reshape(-1)[:size].reshape(shp))
        r += nr
    return outs


def kernel(x, c, ctx, c_ctx, w_mod, b_mod, g_pre, g_post, w_in, w_out, ret_norm_g, sg_w, sg_b, sc_conv_w, gdn_conv_w, gdn_a_log, gdn_dt_bias, gdn_norm_g, loss_target, m_c_ctx, m_w_mod, m_b_mod, m_g_pre, m_g_post, m_w_in, m_w_out, m_ret_norm_g, m_sg_w, m_sg_b, m_sc_conv_w, m_gdn_conv_w, m_gdn_a_log, m_gdn_dt_bias, m_gdn_norm_g, v_c_ctx, v_w_mod, v_b_mod, v_g_pre, v_g_post, v_w_in, v_w_out, v_ret_norm_g, v_sg_w, v_sg_b, v_sc_conv_w, v_gdn_conv_w, v_gdn_a_log, v_gdn_dt_bias, v_gdn_norm_g):
    weights = dict(c_ctx=c_ctx, w_mod=w_mod, b_mod=b_mod, g_pre=g_pre, g_post=g_post, w_in=w_in, w_out=w_out,
                   ret_norm_g=ret_norm_g, sg_w=sg_w, sg_b=sg_b, sc_conv_w=sc_conv_w, gdn_conv_w=gdn_conv_w,
                   gdn_a_log=gdn_a_log, gdn_dt_bias=gdn_dt_bias, gdn_norm_g=gdn_norm_g)
    mom = dict(c_ctx=m_c_ctx, w_mod=m_w_mod, b_mod=m_b_mod, g_pre=m_g_pre, g_post=m_g_post, w_in=m_w_in,
               w_out=m_w_out, ret_norm_g=m_ret_norm_g, sg_w=m_sg_w, sg_b=m_sg_b, sc_conv_w=m_sc_conv_w,
               gdn_conv_w=m_gdn_conv_w, gdn_a_log=m_gdn_a_log, gdn_dt_bias=m_gdn_dt_bias, gdn_norm_g=m_gdn_norm_g)
    var = dict(c_ctx=v_c_ctx, w_mod=v_w_mod, b_mod=v_b_mod, g_pre=v_g_pre, g_post=v_g_post, w_in=v_w_in,
               w_out=v_w_out, ret_norm_g=v_ret_norm_g, sg_w=v_sg_w, sg_b=v_sg_b, sc_conv_w=v_sc_conv_w,
               gdn_conv_w=v_gdn_conv_w, gdn_a_log=v_gdn_a_log, gdn_dt_bias=v_gdn_dt_bias, gdn_norm_g=v_gdn_norm_g)

    nb, t_lat, _ = x.shape
    t_ctx = ctx.shape[1]
    s = t_ctx + t_lat
    n = nb * s
    sb = s // TM
    nl = w_in.shape[0]
    wc_in = w_in.shape[2]
    wc_mod = w_mod.shape[2]
    rows_out = w_out.shape[1]
    n_all = nb * N_DEV
    mx, my, mc = _my_pos()
    chip = 2 * mx + my
    dev = 2 * chip + mc

    hm = jnp.asarray(_head_masks())
    hm4 = hm[:, 0, :]
    bd = jnp.asarray(_block_diag())
    ret_c = _ret_consts(nb)
    gdn_c = _gdn_consts(nb)
    cos, sins = _rope_tables(t_lat, t_ctx)

    pre = _pack([c, sc_conv_w, gdn_conv_w])
    pre_all = gather8(pre)
    c_parts, scw_parts, gcw_parts = [], [], []
    for k in range(N_DEV):
        ck, sk, gk = _unpack(pre_all[k], [c.shape, sc_conv_w.shape, gdn_conv_w.shape])
        c_parts.append(ck)
        if k % 2 == 0:
            scw_parts.append(sk)
            gcw_parts.append(gk)
    c_all = jnp.concatenate(c_parts, axis=0)
    sc_w_full = jnp.concatenate(scw_parts, axis=-1)
    gdn_w_full = jnp.concatenate(gcw_parts, axis=-1)
    c_rows = jnp.concatenate([c_all, c_ctx[None, :], jnp.zeros((7, D), f32)], axis=0)

    b_cols = lax.dynamic_slice_in_dim(b_mod, chip * wc_mod, wc_mod, axis=1)[:, None, :]
    mod_part = mod_fwd(c_rows, w_mod, b_cols)
    mod_all = gather8(mod_part)
    mod = jnp.concatenate([mod_all[2 * k] for k in range(N_CHIPS)], axis=-1)
    my_rows = jnp.concatenate([lax.dynamic_slice_in_dim(mod, dev * nb, nb, axis=1), mod[:, n_all:n_all + 1]], axis=1)
    shift_t = my_rows[:, :, None, 0:D]
    scale_t = my_rows[:, :, None, D:2 * D]
    gate_t = my_rows[:, :, None, 2 * D:3 * D]

    w_in_b, w_out_b = w_in.astype(bf16), w_out.astype(bf16)

    def full_weights(parts):
        wi = jnp.concatenate([parts[0][k] for k in range(N_CHIPS)], axis=-1)
        wo = jnp.concatenate([parts[1][k] for k in range(N_CHIPS)], axis=0)
        return jnp.pad(wi, ((0, 0), (0, PW - IN_W))), wo

    w_in_full, w_out_full = [None] * nl, [None] * nl
    w_in_full[0], w_out_full[0] = full_weights(exchange4("gather", [w_in_b[0], w_out_b[0]]))

    alog = jnp.pad(gdn_a_log.reshape(nl, 1, 8), ((0, 0), (0, 0), (0, 120)))
    dtb = jnp.pad(gdn_dt_bias.reshape(nl, 1, 8), ((0, 0), (0, 0), (0, 120)))
    gdn_ng = jnp.tile(gdn_norm_g, (1, NH))[:, None, :]
    ret_ng = ret_norm_g[:, None, :]

    xs = jnp.concatenate([ctx, x], axis=1).reshape(n, D)
    saved = []
    for l in range(nl):
        p, h = inproj_fwd(xs, shift_t[l], scale_t[l], g_pre[l][None, :], w_in_full[l], nb, sb)
        p3 = p.reshape(nb, s, PW)
        ro_f, ro_b, rs_all = ret_scan_fwd(p3, cos, sins, ret_c, t_ctx)
        y_ret = mix_finish_fwd(_ret_finish, "ret_finish_fwd", ro_f, ro_b, p3, 3, ret_ng[l], bd)
        y_sg = sg_fwd(p3, sg_w[l], sg_b[l], hm4)
        y_sc = sc_fwd(p3, sc_w_full[l], t_ctx)
        cq, ck, cv = [gdn_conv_fwd(p3, gdn_w_full[l][:, BW * i:BW * (i + 1)], 11 + i, t_ctx) for i in range(3)]
        nxt = None if l + 1 == nl else ("gather", [w_in_b[l + 1], w_out_b[l + 1]])
        go_f, go_b, *gs_all = gdn_scan_fwd(cq, ck, cv, p3, alog[l], dtb[l], gdn_c, t_ctx, nxt)
        if nxt is not None:
            w_in_full[l + 1], w_out_full[l + 1] = full_weights(gs_all[2:])
            gs_all = gs_all[:2]
        y_gdn = mix_finish_fwd(_gdn_finish, "gdn_finish_fwd", go_f, go_b, p3, 14, gdn_ng[l], bd)
        ys = [a.reshape(n, BW) for a in (y_ret, y_sg, y_sc, y_gdn)]
        x_new, o = outproj_fwd(ys, w_out_full[l], xs, gate_t[l], g_post[l][None, :], nb, sb)
        saved.append(dict(x=xs, h=h, p3=p3, ro=(ro_f, ro_b), rs=rs_all, c=(cq, ck, cv), go=(go_f, go_b), gs=gs_all,
                          ys=ys, o=o))
        xs = x_new

    dx3, loss_part = loss_head(xs.reshape(nb, s, D), loss_target, t_ctx)
    loss = lax.psum(loss_part[0, 0], ("x", "y", "c"))

    dxs = dx3.reshape(n, D)
    g_small = {k: [None] * nl for k in SMALL if k not in ("c_ctx", "b_mod")}
    dm_rows = [None] * nl
    slabs = None
    got_in, got_out = [None] * nl, [None] * nl
    for l in reversed(range(nl)):
        sv = saved[l]
        p3 = sv["p3"]
        dy, gw_out, dg_post, dgate = outproj_bwd(dxs, sv["o"], gate_t[l], g_post[l][None, :], sv["ys"], w_out_full[l], nb, sb)
        dy3 = dy.reshape(nb, s, D)
        r_do, r_dz, d_rng = mix_finish_bwd(_ret_finish, "ret_finish_bwd", *sv["ro"], p3, 3, ret_ng[l], bd, dy3, 0)
        r_d = ret_scan_bwd(p3, cos, sins, ret_c, sv["rs"], r_do, t_ctx)
        s_du, s_dv, s_dz, d_sgw, d_sgb = sg_bwd(p3, sg_w[l], sg_b[l], hm4, dy3)
        c_db, c_dc, c_dh, c_dz, d_scw = sc_bwd(p3, sc_w_full[l], dy3, t_ctx)
        g_do, g_dz, d_gng = mix_finish_bwd(_gdn_finish, "gdn_finish_bwd", *sv["go"], p3, 14, gdn_ng[l], bd, dy3, 3)
        g_d = gdn_scan_bwd(*sv["c"], p3, alog[l], dtb[l], gdn_c, *sv["gs"], g_do, t_ctx,
                           None if slabs is None else ("scatter", slabs))
        if slabs is not None:
            got_in[l + 1], got_out[l + 1] = g_d[10:]
        gx, d_gcw = [], []
        for i in range(3):
            dxi, dwi = gdn_conv_bwd(p3, gdn_w_full[l][:, BW * i:BW * (i + 1)], 11 + i, g_d[i], g_d[4 + i], t_ctx)
            gx.append(dxi)
            d_gcw.append(dwi)
        dp3 = assemble_dp([(r_d[0], r_d[3]), (r_d[1], r_d[4]), (r_d[2], r_d[5])],
                          [r_dz, s_du, s_dv, s_dz, c_db, c_dc, c_dh, c_dz], gx, [g_dz], [g_d[3], g_d[7]])
        dp = dp3.reshape(n, PW)
        dxs, dg_pre, dshift, dscale = inproj_bwd_x(dp, w_in_full[l], sv["x"], scale_t[l], g_pre[l][None, :], dxs, nb, sb)
        gw_in = dw_in(sv["h"], dp)
        slabs = [jnp.transpose(gw_in[:, :IN_W].reshape(D, N_CHIPS, wc_in), (1, 0, 2)).astype(bf16),
                 gw_out.reshape(N_CHIPS, rows_out, D).astype(bf16)]
        g_small["g_pre"][l] = dg_pre[0]
        g_small["g_post"][l] = dg_post[0]
        g_small["ret_norm_g"][l] = d_rng[0]
        g_small["sg_w"][l] = d_sgw
        g_small["sg_b"][l] = d_sgb
        g_small["sc_conv_w"][l] = d_scw
        g_small["gdn_conv_w"][l] = jnp.concatenate(d_gcw, axis=-1)
        g_small["gdn_a_log"][l] = g_d[8][0, :8].reshape(2, NH)
        g_small["gdn_dt_bias"][l] = g_d[9][0, :8].reshape(2, NH)
        g_small["gdn_norm_g"][l] = d_gng[0].reshape(NH, HD)
        dm_rows[l] = jnp.concatenate([dshift, dscale, dgate], axis=-1)[:nb + 1]
    grad_x = dxs.reshape(nb, s, D)[:, t_ctx:, :]

    g_small = {k: jnp.stack(v) for k, v in g_small.items()}
    dm_rows = jnp.stack(dm_rows)
    names2 = [k for k in SMALL if k not in ("c_ctx", "b_mod")]
    pack_sum = _pack([g_small[k] for k in names2] + [dm_rows[:, nb:]])
    pack_own = _pack([dm_rows[:, :nb]])
    all2 = gather8(jnp.concatenate([pack_sum, pack_own], axis=0))
    tot2 = sum_lead(all2, tr=PACK_ROWS, rows=pack_sum.shape[0])
    outs2 = _unpack(tot2, [g_small[k].shape for k in names2] + [(nl, 1, 3 * D)])
    grads = dict(zip(names2, outs2[:-1]))
    dm_own = jnp.stack([_unpack(all2[k, pack_sum.shape[0]:], [(nl, nb, 3 * D)])[0] for k in range(N_DEV)])
    dm_own = jnp.transpose(dm_own, (1, 0, 2, 3)).reshape(nl, n_all, 3 * D)
    dm_all = jnp.concatenate([dm_own, jnp.pad(outs2[-1], ((0, 0), (0, 7), (0, 0)))], axis=1)
    grads["gdn_norm_g"] = sum_lead(jnp.transpose(grads["gdn_norm_g"], (1, 0, 2)), tr=nl)
    for k in ("sc_conv_w", "gdn_conv_w"):
        wc = weights[k].shape[2]
        grads[k] = lax.dynamic_slice_in_dim(grads[k], chip * wc, wc, axis=2)

    dm_cols = lax.dynamic_slice_in_dim(dm_all, chip * wc_mod, wc_mod, axis=2)
    g_w_mod, g_b_mod, dcc_part = mod_bwd(c_rows, w_mod, dm_cols, dm_all)
    grads["b_mod"] = g_b_mod[:, 0, :]
    grads["c_ctx"] = cctx_grad(gather8(dcc_part), c_ctx[None, :])[0]

    got_in[0], got_out[0] = exchange4("scatter", slabs)
    gin_mine = jnp.concatenate([sum_lead(a) for a in got_in], axis=0)
    gin_sib = swap_sibling(gin_mine)
    gout_mine = jnp.concatenate([sum_lead(a) for a in got_out], axis=0)
    gout_sib = swap_sibling(gout_mine)

    res = {}
    res["w_in"] = [a.reshape(w_in.shape) for a in adamw(w_in.reshape(nl * D, wc_in), m_w_in.reshape(nl * D, wc_in),
                                                          v_w_in.reshape(nl * D, wc_in), gin_mine, gin_sib)]
    res["w_out"] = [a.reshape(w_out.shape) for a in adamw(w_out.reshape(nl * rows_out, D), m_w_out.reshape(nl * rows_out, D),
                                                            v_w_out.reshape(nl * rows_out, D), gout_mine, gout_sib)]
    res["w_mod"] = [a.reshape(w_mod.shape) for a in adamw(w_mod.reshape(nl * D, wc_mod), m_w_mod.reshape(nl * D, wc_mod),
                                                            v_w_mod.reshape(nl * D, wc_mod), g_w_mod.reshape(nl * D, wc_mod))]
    shapes = [weights[k].shape for k in SMALL]
    small = adamw(_pack([weights[k] for k in SMALL]), _pack([mom[k] for k in SMALL]), _pack([var[k] for k in SMALL]),
                  _pack([grads[k].reshape(weights[k].shape) for k in SMALL]), tr=PACK_ROWS)
    small = [_unpack(a, shapes) for a in small]
    for i, k in enumerate(SMALL):
        res[k] = [small[j][i] for j in range(4)]

    order = ["c_ctx", "w_mod", "b_mod", "g_pre", "g_post", "w_in", "w_out", "ret_norm_g", "sg_w", "sg_b", "sc_conv_w",
             "gdn_conv_w", "gdn_a_log", "gdn_dt_bias", "gdn_norm_g"]
    return (loss, grad_x, *[res[k][0] for k in order], *[res[k][1] for k in order], *[res[k][2] for k in order],
            *[res[k][3] for k in order])
```

```python
import functools

import jax
import jax.numpy as jnp
import numpy as np
from jax import lax
from jax.experimental import pallas as pl
from jax.experimental.pallas import tpu as pltpu

f32 = jnp.float32
bf16 = jnp.bfloat16
HI = lax.Precision.HIGHEST
P3 = lax.Precision.HIGH
MESH = pl.DeviceIdType.MESH

EPS = 1e-6
D = 1024
NH = 4
HD = 64
BW = NH * HD
PAIR_W = 2 * HD
RC = 128
GC = 64
GRID_W = 64
ROPE_BASE = 10000.0
IN_W = 15 * BW + 16
PW = 4096
GATE_COL = 15 * BW
N_CHIPS = 4
N_DEV = 8
TM = 256
TP = 2 * TM
ADAM_LR, ADAM_B1, ADAM_B2, ADAM_EPS, ADAM_WD, ADAM_STEP = 0.001, 0.9, 0.999, 1e-08, 0.01, 10
LANE_HEAD = np.arange(BW) // HD
VMEM_BIG = 56 * 1024 * 1024


def _dot(a, b, precision=None):
    return jnp.dot(a, b, precision=precision, preferred_element_type=f32)


def _dot_nt(a, b, precision=None):
    return lax.dot_general(a, b, (((1,), (1,)), ((), ())), precision=precision, preferred_element_type=f32)


def _dot_tn(a, b, precision=None):
    return lax.dot_general(a, b, (((0,), (0,)), ((), ())), precision=precision, preferred_element_type=f32)


def _sds(shape, dtype=f32):
    return jax.ShapeDtypeStruct(shape, dtype)


def _cparams(sem=None, vmem=None):
    kw = {}
    if sem is not None:
        kw["dimension_semantics"] = sem
    if vmem is not None:
        kw["vmem_limit_bytes"] = vmem
    return pltpu.CompilerParams(**kw)


def _full(shape):
    n = len(shape)
    return pl.BlockSpec(shape, lambda *_: (0,) * n)


def _block_diag():
    return (LANE_HEAD[:, None] == LANE_HEAD[None, :]).astype(np.float32)


def _tau(c, d):
    return np.arange(c) if d == 0 else c - 1 - np.arange(c)


def _ret_consts(nb):
    lg = np.log(1.0 - 2.0 ** (-5.0 - np.arange(NH)))
    intra = np.zeros((2, 2, RC, 2 * RC)); qdec = np.zeros((2, 2, RC, PAIR_W)); kdec = np.zeros((2, 2, RC, PAIR_W))
    cd = np.zeros((2, 2, PAIR_W, PAIR_W))
    for d in range(2):
        t = _tau(RC, d)
        diff = t[:, None] - t[None, :]
        for p in range(2):
            lane_lg = lg[2 * p + np.arange(PAIR_W) // HD]
            for h in range(2):
                intra[d, p, :, h * RC:(h + 1) * RC] = np.where(diff >= 0, np.exp(np.maximum(diff, 0) * lg[2 * p + h]), 0.0)
            qdec[d, p] = np.exp((t[:, None] + 1.0) * lane_lg[None, :])
            kdec[d, p] = np.exp((RC - 1.0 - t[:, None]) * lane_lg[None, :])
            cd[d, p] = np.exp(RC * lane_lg)[:, None] * np.ones((1, PAIR_W))
    per_z = [np.tile(a.reshape((4,) + a.shape[2:]), (nb, 1, 1)) for a in (intra, qdec, kdec, cd)]
    bd2 = (np.arange(PAIR_W)[:, None] // HD == np.arange(PAIR_W)[None, :] // HD)
    bdr = (np.arange(2 * RC)[:, None] // RC == np.arange(PAIR_W)[None, :] // HD)
    return [jnp.asarray(a, f32) for a in per_z + [bd2, bdr]]


def _rope_tables(t_lat, t_ctx):
    nf = HD // 4
    inv = ROPE_BASE ** (-np.arange(nf) / nf)
    pos = np.arange(t_lat)
    ang_r = (pos // GRID_W)[:, None] * inv[None, :]
    ang_c = (pos % GRID_W)[:, None] * inv[None, :]
    ang = np.concatenate([ang_r, ang_r, ang_c, ang_c], axis=1)
    sign = np.concatenate([-np.ones(nf), np.ones(nf), -np.ones(nf), np.ones(nf)])
    cos = np.tile(np.cos(ang), (1, 2)); sins = np.tile(np.sin(ang) * sign, (1, 2))
    cos = np.concatenate([np.ones((t_ctx, PAIR_W)), cos]); sins = np.concatenate([np.zeros((t_ctx, PAIR_W)), sins])
    return jnp.asarray(cos, f32), jnp.asarray(sins, f32)


def _gdn_consts(nb):
    tmask = np.zeros((2, 2, GC, GC)); tmask2 = np.zeros((2, 2, GC, PAIR_W)); strict2 = np.zeros((2, 2, GC, PAIR_W))
    exp_g = np.zeros((2, 2, 128, PAIR_W)); exp_b = np.zeros((2, 2, 128, PAIR_W))
    for d in range(2):
        t = _tau(GC, d)
        tmask[d, :] = (t[:, None] >= t[None, :])
        tmask2[d, :] = np.tile(t[:, None] >= t[None, :], (1, 2))
        strict2[d, :] = np.tile(t[:, None] > t[None, :], (1, 2))
        for h in range(NH):
            exp_g[d, h // 2, 4 * d + h, (h % 2) * HD:(h % 2 + 1) * HD] = 1.0
            exp_b[d, h // 2, 8 + 4 * d + h, (h % 2) * HD:(h % 2 + 1) * HD] = 1.0
    exp_gt = np.transpose(exp_g, (0, 1, 3, 2))
    per_z = [np.tile(a.reshape((4,) + a.shape[2:]), (nb, 1, 1)) for a in (tmask, tmask2, strict2, exp_g, exp_b, exp_gt)]
    dsel2 = np.tile(np.eye(GC), (1, 2))
    eye2 = np.tile(np.eye(GC), (1, 2))
    bd2 = (np.arange(PAIR_W)[:, None] // HD == np.arange(PAIR_W)[None, :] // HD)
    return [jnp.asarray(a, f32) for a in per_z + [dsel2, eye2, bd2]]


def _swap16(x):
    lane = lax.broadcasted_iota(jnp.int32, x.shape, x.ndim - 1)
    n = x.shape[-1]
    return jnp.where(lane % 32 < 16, pltpu.roll(x, n - 16, axis=x.ndim - 1), pltpu.roll(x, 16, axis=x.ndim - 1))


@jax.custom_vjp
def _rot(x, cos, sins):
    return x * cos + _swap16(x) * sins


def _rot_fwd(x, cos, sins):
    return _rot(x, cos, sins), (cos, sins)


def _rot_bwd(res, g):
    cos, sins = res
    return g * cos + _swap16(g * sins), jnp.zeros_like(cos), jnp.zeros_like(sins)


_rot.defvjp(_rot_fwd, _rot_bwd)


def _silu(z):
    return z * jax.nn.sigmoid(z)


def _head_sum(x, bd):
    return _dot(x, bd, precision=P3)


def _ret_step(s, q, k, v, cos, sins, intra, qdec, kdec, cd, bd2, bdr):
    def bdiag(x):
        return jnp.concatenate([x, x], axis=1) * bdr

    qr = _rot(q, cos, sins)
    kr = _rot(k, cos, sins) * (HD ** -0.5)
    sc = _bmm_nt(qr, bdiag(kr)) * intra
    o = _bmm(qr * qdec, s) + _bmm(sc, bdiag(v))
    s_new = s * cd + bd2 * _bmm_tn(kr * kdec, v)
    return s_new, o


def _ret_finish(o_f, o_b, z, norm_g, bd):
    o = o_f + o_b
    mu = _head_sum(o, bd) * (1.0 / HD)
    xc = o - mu
    var = _head_sum(xc * xc, bd) * (1.0 / HD)
    return xc * lax.rsqrt(var + EPS) * norm_g * _silu(z)


def _softplus(x):
    return jnp.maximum(x, 0.0) + jnp.log(1.0 + jnp.exp(-jnp.abs(x)))


def _bmm(a, b, precision=None):
    return lax.dot_general(a, b, (((2,), (1,)), ((0,), (0,))), precision=precision, preferred_element_type=f32)


def _bmm_nt(a, b, precision=None):
    return lax.dot_general(a, b, (((2,), (2,)), ((0,), (0,))), precision=precision, preferred_element_type=f32)


def _bmm_tn(a, b, precision=None):
    return lax.dot_general(a, b, (((1,), (1,)), ((0,), (0,))), precision=precision, preferred_element_type=f32)


def _bdiag(x, bd2):
    return jnp.concatenate([x, x], axis=1) * bd2


@jax.custom_vjp
def _solve_given_inv(m, vb, kbg, inv, bd2):
    return _bmm(inv, _bdiag(vb, bd2), P3), _bmm(inv, _bdiag(kbg, bd2), P3)


def _solve_fwd(m, vb, kbg, inv, bd2):
    u, w = _solve_given_inv(m, vb, kbg, inv, bd2)
    return (u, w), (inv, u, w, bd2)


def _solve_bwd(res, cts):
    inv, u, w, bd2 = res
    du, dw = cts
    c = inv.shape[1]
    t = jnp.swapaxes(_bdiag(inv, bd2), 1, 2)
    inv_t = t[:, :c] + t[:, c:]
    dvb = _bmm(inv_t, _bdiag(du, bd2), P3)
    dkbg = _bmm(inv_t, _bdiag(dw, bd2), P3)
    dm = _bmm_nt(dvb, _bdiag(u, bd2), P3) + _bmm_nt(dkbg, _bdiag(w, bd2), P3)
    return dm, dvb, dkbg, jnp.zeros_like(inv), jnp.zeros_like(bd2)


_solve_given_inv.defvjp(_solve_fwd, _solve_bwd)


def _gdn_step(s, q, k, v, gate, alog, dtb, tmask, tmask2, strict2, exp_g, exp_b, exp_gt, dsel2, eye2, bd2, inv=None):
    z, c, w_ = q.shape
    ne = gate.shape[0]

    def per_pair(a):
        return jnp.broadcast_to(a[:, None], (ne, z // ne) + a.shape[1:]).reshape((z,) + a.shape[1:])

    def rows(a):
        return a.reshape(z * c, w_)

    def bdiag(x):
        return _bdiag(x, bd2)

    g = per_pair(-jnp.exp(alog) * _softplus(gate + dtb))
    beta = per_pair(jax.nn.sigmoid(gate))
    gl = _bmm(g, exp_g, P3)
    gc_l = _bmm(tmask, gl, P3)
    glast_l = jnp.sum(gl, axis=1, keepdims=True)
    glast = jnp.sum(g, axis=1, keepdims=True)
    beta_l = _bmm(beta, exp_b, P3)
    gc_r = jnp.sum(gc_l * dsel2, axis=1, keepdims=True)
    qn = q * lax.rsqrt(_dot(rows(q * q), bd2, P3).reshape(z, c, w_) + EPS)
    kn = k * lax.rsqrt(_dot(rows(k * k), bd2, P3).reshape(z, c, w_) + EPS)
    eg = jnp.exp(gc_l)
    kb = kn * beta_l
    vb = v * beta_l
    kbg = kb * eg
    qs = qn * (HD ** -0.5)
    dec = jnp.exp(jnp.where(tmask2 > 0, gc_l - gc_r, -1e30))
    kns = bdiag(kn)
    m = -(_bmm_nt(kb, kns) * dec * strict2)
    if inv is None:
        inv = eye2 + m
        p = m
        for _ in range(5):
            p = _bmm(p, bdiag(p), P3)
            inv = inv + _bmm(inv, bdiag(p), P3)
        u = _bmm(inv, bdiag(vb), P3)
        w = _bmm(inv, bdiag(kbg), P3)
    else:
        u, w = _solve_given_inv(m, vb, kbg, inv, bd2)
    v_new = u - _bmm(w, s)
    k_tail = kn * jnp.exp(glast_l - gc_l)
    cdec = jnp.sum(exp_gt * jnp.exp(glast), axis=-1, keepdims=True)
    s_new = s * cdec + bd2 * _bmm_tn(k_tail, v_new)
    a = _bmm_nt(qs, kns) * dec
    o = _bmm(qs * eg, s) + _bmm(a, bdiag(v_new))
    return s_new, o, inv


def _gdn_finish(o_f, o_b, z, norm_g, bd):
    o = o_f + o_b
    ms = _head_sum(o * o, bd) * (1.0 / HD)
    return o * lax.rsqrt(ms + EPS) * norm_g * _silu(z)


def _gelu(x):
    return 0.5 * x * (1.0 + jnp.tanh(0.7978845608028654 * (x + 0.044715 * (x * x * x))))


def _sg_block(u0, u1, v0, v1, z0, z1, w, b, hmp, bdr):
    ts = u0.shape[0]
    nc = ts // RC
    g0, g1 = _gelu(v0), _gelu(v1)
    mu = (jnp.sum(g0, axis=-1, keepdims=True) + jnp.sum(g1, axis=-1, keepdims=True)) * (1.0 / BW)
    x0, x1 = g0 - mu, g1 - mu
    var = (jnp.sum(x0 * x0, axis=-1, keepdims=True) + jnp.sum(x1 * x1, axis=-1, keepdims=True)) * (1.0 / BW)
    rstd = lax.rsqrt(var + EPS)
    ys = []
    for p, (u, xc, z) in enumerate(((u0, x0, z0), (u1, x1, z1))):
        vn = (xc * rstd).reshape(nc, RC, PAIR_W)
        wp = jnp.concatenate([w[2 * p], w[2 * p + 1]], axis=1)
        mix = _bmm(jnp.broadcast_to(wp, (nc, RC, 2 * RC)), jnp.concatenate([vn, vn], axis=1) * bdr)
        bias = _dot_tn(b, hmp[p], precision=HI)
        s = (mix + bias).reshape(ts, PAIR_W)
        ys.append(_gelu(u) * s * _silu(z))
    return ys[0], ys[1]


def _make_shifts(t_ctx, n):
    def dn(x):
        t = lax.broadcasted_iota(jnp.int32, x.shape, 0)
        return jnp.where((t != 0) & (t != t_ctx), pltpu.roll(x, 1, axis=0), 0.0)

    def up(x):
        t = lax.broadcasted_iota(jnp.int32, x.shape, 0)
        return jnp.where((t != t_ctx - 1) & (t != n - 1), pltpu.roll(x, n - 1, axis=0), 0.0)

    @jax.custom_vjp
    def shift_dn(x):
        return dn(x)
    shift_dn.defvjp(lambda x: (dn(x), None), lambda _, g: (up(g),))

    @jax.custom_vjp
    def shift_up(x):
        return up(x)
    shift_up.defvjp(lambda x: (up(x), None), lambda _, g: (dn(g),))
    return shift_dn, shift_up


def _conv3(x, w, shift_dn, shift_up):
    return shift_dn(x) * w[0:1] + x * w[1:2] + shift_up(x) * w[2:3]


def inproj_fwd(x, shift_t, scale_t, g_pre, w_in, n_batch, sb):
    n = x.shape[0]

    def sel(i):
        return jnp.where(i % sb == 0, n_batch, i // sb)

    def body(x_ref, sh0, sh1, sc0, sc1, g_ref, w_ref, p_ref, h_ref):
        hs = []
        for k, (sh_ref, sc_ref) in enumerate(((sh0, sc0), (sh1, sc1))):
            xv = x_ref[k * TM:(k + 1) * TM, :]
            r = xv * lax.rsqrt(jnp.mean(xv * xv, axis=-1, keepdims=True) + EPS)
            hs.append(((r * g_ref[...]) * (1.0 + sc_ref[0]) + sh_ref[0]).astype(bf16))
        hb = jnp.concatenate(hs, axis=0)
        h_ref[...] = hb
        p_ref[...] = _dot(hb, w_ref[...])

    def mrow(k):
        return pl.BlockSpec((1, 1, D), lambda i: (sel(2 * i + k), 0, 0))

    return pl.pallas_call(
        body, name="inproj_fwd", grid=(n // TP,),
        in_specs=[pl.BlockSpec((TP, D), lambda i: (i, 0)), mrow(0), mrow(1), mrow(0), mrow(1),
                  _full((1, D)), _full((D, PW))],
        out_specs=[pl.BlockSpec((TP, PW), lambda i: (i, 0)), pl.BlockSpec((TP, D), lambda i: (i, 0))],
        out_shape=[_sds((n, PW)), _sds((n, D), bf16)],
        compiler_params=_cparams(("arbitrary",), VMEM_BIG),
    )(x, shift_t, shift_t, scale_t, scale_t, g_pre, w_in)


def outproj_fwd(ys, w_out, x, gate_t, g_post, n_batch, sb):
    n = x.shape[0]

    def sel(i):
        return jnp.where(i % sb == 0, n_batch, i // sb)

    def body(y0, y1, y2, y3, w_ref, x_ref, gt0, gt1, g_ref, xn_ref, o_ref):
        y = jnp.concatenate([y0[...], y1[...], y2[...], y3[...]], axis=1)
        o = _dot(y, w_ref[...])
        o_ref[...] = o
        nrm = o * lax.rsqrt(jnp.mean(o * o, axis=-1, keepdims=True) + EPS) * g_ref[...]
        for k, gt_ref in enumerate((gt0, gt1)):
            rows = slice(k * TM, (k + 1) * TM)
            xn_ref[rows, :] = x_ref[rows, :] + gt_ref[0] * nrm[rows]

    def mrow(k):
        return pl.BlockSpec((1, 1, D), lambda i: (sel(2 * i + k), 0, 0))

    yspec = pl.BlockSpec((TP, BW), lambda i: (i, 0))
    return pl.pallas_call(
        body, name="outproj_fwd", grid=(n // TP,),
        in_specs=[yspec, yspec, yspec, yspec, _full((D, D)), pl.BlockSpec((TP, D), lambda i: (i, 0)),
                  mrow(0), mrow(1), _full((1, D))],
        out_specs=[pl.BlockSpec((TP, D), lambda i: (i, 0)), pl.BlockSpec((TP, D), lambda i: (i, 0))],
        out_shape=[_sds((n, D)), _sds((n, D))],
        compiler_params=_cparams(("arbitrary",), VMEM_BIG),
    )(*ys, w_out, x, gate_t, gate_t, g_post)


def _row_onehot(r):
    return lax.broadcasted_iota(jnp.int32, (8, 1), 0) == r


def outproj_bwd(dxn, o, gate_t, g_post, ys, w_out, n_batch, sb):
    n = dxn.shape[0]

    def sel(i):
        return jnp.where(i % sb == 0, n_batch, i // sb)

    def body(dxn_ref, o_ref, gt0, gt1, g_ref, y0, y1, y2, y3, w_ref, dy_ref, dw_ref, dg_ref, dgate_ref):
        i = pl.program_id(0)

        @pl.when(i == 0)
        def _():
            dw_ref[...] = jnp.zeros_like(dw_ref)
            dg_ref[...] = jnp.zeros_like(dg_ref)
            dgate_ref[...] = jnp.zeros_like(dgate_ref)

        g = g_ref[...]
        dos = []
        for k, gt_ref in enumerate((gt0, gt1)):
            rows = slice(k * TM, (k + 1) * TM)
            ov = o_ref[rows, :]
            rstd = lax.rsqrt(jnp.mean(ov * ov, axis=-1, keepdims=True) + EPS)
            r = ov * rstd
            dx = dxn_ref[rows, :]
            dgate_ref[...] += jnp.where(_row_onehot(sel(2 * i + k)), jnp.sum(dx * (r * g), axis=0, keepdims=True), 0.0)
            dn = dx * gt_ref[0]
            dg_ref[...] += jnp.sum(dn * r, axis=0, keepdims=True)
            dr = dn * g
            dos.append((rstd * (dr - r * jnp.mean(dr * r, axis=-1, keepdims=True))).astype(bf16))
        dob = jnp.concatenate(dos, axis=0)
        dy_ref[...] = _dot_nt(dob, w_ref[...])
        y = jnp.concatenate([y0[...], y1[...], y2[...], y3[...]], axis=1)
        dw_ref[...] += _dot_tn(y, dob)

    def mrow(k):
        return pl.BlockSpec((1, 1, D), lambda i: (sel(2 * i + k), 0, 0))

    yspec = pl.BlockSpec((TP, BW), lambda i: (i, 0))
    row = pl.BlockSpec((TP, D), lambda i: (i, 0))
    return pl.pallas_call(
        body, name="outproj_bwd", grid=(n // TP,),
        in_specs=[row, row, mrow(0), mrow(1), _full((1, D)), yspec, yspec, yspec, yspec, _full((D, D))],
        out_specs=[row, _full((D, D)), _full((1, D)), _full((8, D))],
        out_shape=[_sds((n, D)), _sds((D, D)), _sds((1, D)), _sds((8, D))],
        compiler_params=_cparams(("arbitrary",), VMEM_BIG),
    )(dxn, o, gate_t, gate_t, g_post, *ys, w_out)


def inproj_bwd_x(dp, w_in, x, scale_t, g_pre, dxn, n_batch, sb):
    n = x.shape[0]

    def sel(i):
        return jnp.where(i % sb == 0, n_batch, i // sb)

    def body(dp_ref, w_ref, x_ref, sc0, sc1, g_ref, dxn_ref, dx_ref, dg_ref, dsh_ref, dsc_ref):
        i = pl.program_id(0)

        @pl.when(i == 0)
        def _():
            dg_ref[...] = jnp.zeros_like(dg_ref)
            dsh_ref[...] = jnp.zeros_like(dsh_ref)
            dsc_ref[...] = jnp.zeros_like(dsc_ref)

        dh_all = _dot_nt(dp_ref[...], w_ref[...])
        g = g_ref[...]
        for k, sc_ref in enumerate((sc0, sc1)):
            rows = slice(k * TM, (k + 1) * TM)
            dh = dh_all[rows]
            xv = x_ref[rows, :]
            rstd = lax.rsqrt(jnp.mean(xv * xv, axis=-1, keepdims=True) + EPS)
            r = xv * rstd
            hot = _row_onehot(sel(2 * i + k))
            dsh_ref[...] += jnp.where(hot, jnp.sum(dh, axis=0, keepdims=True), 0.0)
            dsc_ref[...] += jnp.where(hot, jnp.sum(dh * (r * g), axis=0, keepdims=True), 0.0)
            t = dh * (1.0 + sc_ref[0])
            dg_ref[...] += jnp.sum(t * r, axis=0, keepdims=True)
            dr = t * g
            dx_ref[rows, :] = dxn_ref[rows, :] + rstd * (dr - r * jnp.mean(dr * r, axis=-1, keepdims=True))

    def mrow(k):
        return pl.BlockSpec((1, 1, D), lambda i: (sel(2 * i + k), 0, 0))

    row = pl.BlockSpec((TP, D), lambda i: (i, 0))
    return pl.pallas_call(
        body, name="inproj_bwd_x", grid=(n // TP,),
        in_specs=[pl.BlockSpec((TP, PW), lambda i: (i, 0)), _full((D, PW)), row, mrow(0), mrow(1), _full((1, D)), row],
        out_specs=[row, _full((1, D)), _full((8, D)), _full((8, D))],
        out_shape=[_sds((n, D)), _sds((1, D)), _sds((8, D)), _sds((8, D))],
        compiler_params=_cparams(("arbitrary",), VMEM_BIG),
    )(dp, w_in, x, scale_t, scale_t, g_pre, dxn)


def dw_in(h, dp):
    n = h.shape[0]
    tk, tn = (1536 if n % 1536 == 0 else 512), 1024
    nk = n // tk

    def body(h_ref, dp_ref, o_ref, acc):
        k = pl.program_id(1)

        @pl.when(k == 0)
        def _():
            acc[...] = jnp.zeros_like(acc)
        acc[...] += _dot_tn(h_ref[...], dp_ref[...])

        @pl.when(k == nk - 1)
        def _():
            o_ref[...] = acc[...].astype(bf16)

    return pl.pallas_call(
        body, name="dw_in", grid=(PW // tn, nk),
        in_specs=[pl.BlockSpec((tk, D), lambda j, k: (k, 0)), pl.BlockSpec((tk, tn), lambda j, k: (k, j))],
        out_specs=pl.BlockSpec((D, tn), lambda j, k: (0, j)),
        out_shape=_sds((D, PW), bf16),
        scratch_shapes=[pltpu.VMEM((D, tn), f32)],
        compiler_params=_cparams(("parallel", "arbitrary"), VMEM_BIG),
    )(h, dp)


def loss_head(xf, target, t_ctx):
    nb, s, _ = xf.shape
    jc = t_ctx // TM

    def body(x_ref, t_ref, dx_ref, l_ref):
        b, j = pl.program_id(0), pl.program_id(1)

        @pl.when((b == 0) & (j == 0))
        def _():
            l_ref[...] = jnp.zeros_like(l_ref)

        @pl.when(j < jc)
        def _():
            dx_ref[...] = jnp.zeros_like(dx_ref)

        @pl.when(j >= jc)
        def _():
            diff = x_ref[0] - t_ref[0]
            dx_ref[0] = diff * (1.0 / D)
            l_ref[...] += 0.5 * jnp.sum(diff * diff) * (1.0 / D)

    return pl.pallas_call(
        body, name="loss_head", grid=(nb, s // TM),
        in_specs=[pl.BlockSpec((1, TM, D), lambda b, j: (b, j, 0)),
                  pl.BlockSpec((1, TM, D), lambda b, j: (b, jnp.maximum(j - jc, 0), 0))],
        out_specs=[pl.BlockSpec((1, TM, D), lambda b, j: (b, j, 0)), _full((1, 128))],
        out_shape=[_sds((nb, s, D)), _sds((1, 128))],
        compiler_params=_cparams(("arbitrary", "arbitrary")),
    )(xf, target)


def _chunk_maps(n_ctx, n_lat):
    n = n_ctx + n_lat

    def cf(t):
        return t

    def cb(t):
        return jnp.where(t < n_ctx, n_ctx - 1 - t, n - 1 - t + n_ctx)
    return n, cf, cb


def ret_scan_fwd(p3, cos, sins, consts, t_ctx):
    nb, s, _ = p3.shape
    n, cf, cb = _chunk_maps(t_ctx // RC, (s - t_ctx) // RC)
    nz = 4 * nb

    def body(qf, kf, vf, qb, kb, vb, cosf, sinf, cosb, sinb, intra_r, qdec_r, kdec_r, cd_r, bd_r, bdr_r,
             of_ref, ob_ref, sall_ref, s_sc):
        @pl.when(pl.program_id(0) == 0)
        def _():
            s_sc[...] = jnp.zeros_like(s_sc)
        st = s_sc[...]
        sall_ref[0] = st
        s_new, o = _ret_step(st, _pairs(qf, qb, nb), _pairs(kf, kb, nb), _pairs(vf, vb, nb),
                             _pair_tables(cosf, cosb, nb), _pair_tables(sinf, sinb, nb), intra_r[...], qdec_r[...],
                             kdec_r[...], cd_r[...], bd_r[...], bdr_r[...])
        s_sc[...] = s_new
        _unpairs(o, of_ref, ob_ref, nb)

    def pspec(m, seg):
        return pl.BlockSpec((nb, RC, BW), lambda t: (0, m(t), seg))

    def tspec(m):
        return pl.BlockSpec((RC, PAIR_W), lambda t: (m(t), 0))

    return pl.pallas_call(
        body, name="ret_scan_fwd", grid=(n,),
        in_specs=[pspec(cf, 0), pspec(cf, 1), pspec(cf, 2), pspec(cb, 0), pspec(cb, 1), pspec(cb, 2),
                  tspec(cf), tspec(cf), tspec(cb), tspec(cb)] + [_full(c.shape) for c in consts],
        out_specs=[pl.BlockSpec((nb, RC, BW), lambda t: (0, cf(t), 0)),
                   pl.BlockSpec((nb, RC, BW), lambda t: (0, cb(t), 0)),
                   pl.BlockSpec((1, nz, PAIR_W, PAIR_W), lambda t: (t, 0, 0, 0))],
        out_shape=[_sds((nb, s, BW)), _sds((nb, s, BW)), _sds((n, nz, PAIR_W, PAIR_W))],
        scratch_shapes=[pltpu.VMEM((nz, PAIR_W, PAIR_W), f32)],
        compiler_params=_cparams(("arbitrary",)),
    )(p3, p3, p3, p3, p3, p3, cos, sins, cos, sins, *consts)


def ret_scan_bwd(p3, cos, sins, consts, s_all, do, t_ctx):
    nb, s, _ = p3.shape
    n, cf, cb = _chunk_maps(t_ctx // RC, (s - t_ctx) // RC)
    nz = 4 * nb

    def rf(t):
        return cf(n - 1 - t)

    def rb(t):
        return cb(n - 1 - t)

    def body(qf, kf, vf, qb, kb, vb, cosf, sinf, cosb, sinb, intra_r, qdec_r, kdec_r, cd_r, bd_r, bdr_r,
             sall_ref, dof, dob, dqf, dkf, dvf, dqb, dkb, dvb, ds_sc):
        @pl.when(pl.program_id(0) == 0)
        def _():
            ds_sc[...] = jnp.zeros_like(ds_sc)
        step = functools.partial(_ret_step, cos=_pair_tables(cosf, cosb, nb), sins=_pair_tables(sinf, sinb, nb),
                                 intra=intra_r[...], qdec=qdec_r[...], kdec=kdec_r[...], cd=cd_r[...], bd2=bd_r[...],
                                 bdr=bdr_r[...])
        _, vjp = jax.vjp(step, sall_ref[0], _pairs(qf, qb, nb), _pairs(kf, kb, nb), _pairs(vf, vb, nb))
        ds, dq, dk, dv = vjp((ds_sc[...], _pairs(dof, dob, nb)))
        ds_sc[...] = ds
        _unpairs(dq, dqf, dqb, nb)
        _unpairs(dk, dkf, dkb, nb)
        _unpairs(dv, dvf, dvb, nb)

    def pspec(m, seg):
        return pl.BlockSpec((nb, RC, BW), lambda t: (0, m(t), seg))

    def tspec(m):
        return pl.BlockSpec((RC, PAIR_W), lambda t: (m(t), 0))

    def ospec(m):
        return pl.BlockSpec((nb, RC, BW), lambda t: (0, m(t), 0))

    return pl.pallas_call(
        body, name="ret_scan_bwd", grid=(n,),
        in_specs=[pspec(rf, 0), pspec(rf, 1), pspec(rf, 2), pspec(rb, 0), pspec(rb, 1), pspec(rb, 2),
                  tspec(rf), tspec(rf), tspec(rb), tspec(rb)] + [_full(c.shape) for c in consts]
                 + [pl.BlockSpec((1, nz, PAIR_W, PAIR_W), lambda t: (n - 1 - t, 0, 0, 0)), ospec(rf), ospec(rb)],
        out_specs=[ospec(rf), ospec(rf), ospec(rf), ospec(rb), ospec(rb), ospec(rb)],
        out_shape=[_sds((nb, s, BW))] * 6,
        scratch_shapes=[pltpu.VMEM((nz, PAIR_W, PAIR_W), f32)],
        compiler_params=_cparams(("arbitrary",), VMEM_BIG),
    )(p3, p3, p3, p3, p3, p3, cos, sins, cos, sins, *consts, s_all, do, do)


def mix_finish_fwd(fn, name, o_f, o_b, p3, zseg, norm_g, bd):
    nb, s, _ = p3.shape

    def body(of_ref, ob_ref, z_ref, g_ref, bd_ref, y_ref):
        y_ref[0] = fn(of_ref[0], ob_ref[0], z_ref[0], g_ref[...], bd_ref[...]).astype(bf16)

    blk = pl.BlockSpec((1, TM, BW), lambda b, j: (b, j, 0))
    return pl.pallas_call(
        body, name=name, grid=(nb, s // TM),
        in_specs=[blk, blk, pl.BlockSpec((1, TM, BW), lambda b, j: (b, j, zseg)), _full((1, BW)), _full((BW, BW))],
        out_specs=blk, out_shape=_sds((nb, s, BW), bf16),
        compiler_params=_cparams(("arbitrary", "arbitrary")),
    )(o_f, o_b, p3, norm_g, bd)


def mix_finish_bwd(fn, name, o_f, o_b, p3, zseg, norm_g, bd, dy3, yseg):
    nb, s, _ = p3.shape

    def body(of_ref, ob_ref, z_ref, g_ref, bd_ref, dy_ref, do_ref, dz_ref, dg_ref):
        @pl.when((pl.program_id(0) == 0) & (pl.program_id(1) == 0))
        def _():
            dg_ref[...] = jnp.zeros_like(dg_ref)
        bdv = bd_ref[...]
        _, vjp = jax.vjp(lambda a, b, z, g: fn(a, b, z, g, bdv), of_ref[0], ob_ref[0], z_ref[0], g_ref[...])
        do, _, dz, dg = vjp(dy_ref[0])
        do_ref[0] = do
        dz_ref[0] = dz
        dg_ref[...] += dg

    blk = pl.BlockSpec((1, TM, BW), lambda b, j: (b, j, 0))
    return pl.pallas_call(
        body, name=name, grid=(nb, s // TM),
        in_specs=[blk, blk, pl.BlockSpec((1, TM, BW), lambda b, j: (b, j, zseg)), _full((1, BW)), _full((BW, BW)),
                  pl.BlockSpec((1, TM, BW), lambda b, j: (b, j, yseg))],
        out_specs=[blk, blk, _full((1, BW))],
        out_shape=[_sds((nb, s, BW)), _sds((nb, s, BW)), _sds((1, BW))],
        compiler_params=_cparams(("arbitrary", "arbitrary")),
    )(o_f, o_b, p3, norm_g, bd, dy3)


def gdn_conv_fwd(p3, w, seg, t_ctx):
    nb, s, _ = p3.shape
    sd, su = _make_shifts(t_ctx, s)

    def body(x_ref, w_ref, o_ref):
        o_ref[0] = _silu(_conv3(x_ref[0], w_ref[...], sd, su))

    return pl.pallas_call(
        body, name="gdn_conv_fwd", grid=(nb, 2),
        in_specs=[pl.BlockSpec((1, s, 128), lambda b, j: (b, 0, 2 * seg + j)), pl.BlockSpec((3, 128), lambda b, j: (0, j))],
        out_specs=pl.BlockSpec((1, s, 128), lambda b, j: (b, 0, j)),
        out_shape=_sds((nb, s, BW)),
        compiler_params=_cparams(("arbitrary", "arbitrary")),
    )(p3, w)


def gdn_conv_bwd(p3, w, seg, d_f, d_b, t_ctx):
    nb, s, _ = p3.shape
    sd, su = _make_shifts(t_ctx, s)

    def body(x_ref, w_ref, df_ref, db_ref, dx_ref, dw_ref):
        @pl.when(pl.program_id(1) == 0)
        def _():
            dw_ref[...] = jnp.zeros_like(dw_ref)
        _, vjp = jax.vjp(lambda x, w_: _silu(_conv3(x, w_, sd, su)), x_ref[0], w_ref[...])
        dx, dw = vjp(df_ref[0] + db_ref[0])
        dx_ref[0] = dx
        dw_ref[...] += dw

    blk = pl.BlockSpec((1, s, 128), lambda j, b: (b, 0, j))
    return pl.pallas_call(
        body, name="gdn_conv_bwd", grid=(2, nb),
        in_specs=[pl.BlockSpec((1, s, 128), lambda j, b: (b, 0, 2 * seg + j)), pl.BlockSpec((3, 128), lambda j, b: (0, j)),
                  blk, blk],
        out_specs=[blk, pl.BlockSpec((3, 128), lambda j, b: (0, j))],
        out_shape=[_sds((nb, s, BW)), _sds((3, BW))],
        compiler_params=_cparams(("arbitrary", "arbitrary"), VMEM_BIG),
    )(p3, w, d_f, d_b)


def _pairs(f_ref, b_ref, nb):
    return jnp.stack([r[b, :, PAIR_W * p:PAIR_W * (p + 1)] for b in range(nb) for r in (f_ref, b_ref) for p in range(2)])


def _pair_tables(f_ref, b_ref, nb):
    return jnp.stack([r[...] for _ in range(nb) for r in (f_ref, b_ref) for _ in range(2)])


def _gates(f_ref, b_ref, nb):
    return jnp.stack([r[b] for b in range(nb) for r in (f_ref, b_ref)])


def _unpairs(a, f_ref, b_ref, nb):
    for b in range(nb):
        for d, r in enumerate((f_ref, b_ref)):
            for p in range(2):
                r[b, :, PAIR_W * p:PAIR_W * (p + 1)] = a[4 * b + 2 * d + p]


def _with_exchange(body, n_in, n_out, n_scratch, xchg, n_steps):
    if xchg is None:
        return body, [], [], [], []
    kind, arrs = xchg
    nx = len(arrs)

    def fused(*refs):
        ins, rest = refs[:n_in], refs[n_in:]
        srcs, rest = rest[:nx], rest[nx:]
        outs, rest = rest[:n_out], rest[n_out:]
        dsts, rest = rest[:nx], rest[nx:]
        scratch, sems = rest[:n_scratch], rest[n_scratch:]
        start, wait = _chip_exchange(kind, srcs, dsts, *sems)
        pl.when(pl.program_id(0) == 0)(start)
        body(*ins, *outs, *scratch)
        pl.when(pl.program_id(0) == n_steps - 1)(wait)

    any_ = pl.BlockSpec(memory_space=pl.ANY)
    return fused, [any_] * nx, [any_] * nx, _exchange_shapes(kind, arrs), _exchange_scratch(nx)


def gdn_scan_fwd(cq, ck, cv, p3, alog, dtb, consts, t_ctx, xchg=None):
    nb, s, _ = p3.shape
    n, cf, cb = _chunk_maps(t_ctx // GC, (s - t_ctx) // GC)
    gblk = GATE_COL // 128

    nz = 4 * nb

    def body(qf, kf, vf, gf, qb, kb, vb, gb, al_ref, dt_ref, tm_r, tm2_r, st2_r, eg_r, eb_r, egt_r, dsel_r, eye_r, bd_r,
             of_ref, ob_ref, sall_ref, inv_ref, s_sc):
        @pl.when(pl.program_id(0) == 0)
        def _():
            s_sc[...] = jnp.zeros_like(s_sc)
        st = s_sc[...]
        sall_ref[0] = st
        s_new, o, inv = _gdn_step(st, _pairs(qf, qb, nb), _pairs(kf, kb, nb), _pairs(vf, vb, nb), _gates(gf, gb, nb),
                                  al_ref[...], dt_ref[...], tm_r[...], tm2_r[...], st2_r[...], eg_r[...], eb_r[...],
                                  egt_r[...], dsel_r[...], eye_r[...], bd_r[...])
        s_sc[...] = s_new
        inv_ref[0] = inv
        _unpairs(o, of_ref, ob_ref, nb)

    def cspec(m):
        return pl.BlockSpec((nb, GC, BW), lambda t: (0, m(t), 0))

    def gspec(m):
        return pl.BlockSpec((nb, GC, 128), lambda t: (0, m(t), gblk))

    fused, x_in, x_out, x_shape, x_scratch = _with_exchange(body, 10 + len(consts), 4, 1, xchg, n)
    return pl.pallas_call(
        fused, name="gdn_scan_fwd" + ("" if xchg is None else "_" + xchg[0]), grid=(n,),
        in_specs=[cspec(cf), cspec(cf), cspec(cf), gspec(cf), cspec(cb), cspec(cb), cspec(cb), gspec(cb),
                  _full((1, 128)), _full((1, 128))] + [_full(c.shape) for c in consts] + x_in,
        out_specs=[cspec(cf), cspec(cb), pl.BlockSpec((1, nz, PAIR_W, PAIR_W), lambda t: (t, 0, 0, 0)),
                   pl.BlockSpec((1, nz, GC, PAIR_W), lambda t: (t, 0, 0, 0))] + x_out,
        out_shape=[_sds((nb, s, BW)), _sds((nb, s, BW)), _sds((n, nz, PAIR_W, PAIR_W)), _sds((n, nz, GC, PAIR_W))]
                  + x_shape,
        scratch_shapes=[pltpu.VMEM((nz, PAIR_W, PAIR_W), f32)] + x_scratch,
        compiler_params=_cparams(("arbitrary",)),
    )(cq, ck, cv, p3, cq, ck, cv, p3, alog, dtb, *consts, *([] if xchg is None else xchg[1]))


def gdn_scan_bwd(cq, ck, cv, p3, alog, dtb, consts, s_all, inv_all, do, t_ctx, xchg=None):
    nb, s, _ = p3.shape
    n, cf, cb = _chunk_maps(t_ctx // GC, (s - t_ctx) // GC)
    gblk = GATE_COL // 128

    def rf(t):
        return cf(n - 1 - t)

    def rb(t):
        return cb(n - 1 - t)

    nz = 4 * nb

    def body(qf, kf, vf, gf, qb, kb, vb, gb, al_ref, dt_ref, tm_r, tm2_r, st2_r, eg_r, eb_r, egt_r, dsel_r, eye_r, bd_r,
             sall_ref, inv_ref, dof, dob, dqf, dkf, dvf, dgf, dqb, dkb, dvb, dgb, dal_ref, ddt_ref, ds_sc):
        @pl.when(pl.program_id(0) == 0)
        def _():
            dal_ref[...] = jnp.zeros_like(dal_ref)
            ddt_ref[...] = jnp.zeros_like(ddt_ref)
            ds_sc[...] = jnp.zeros_like(ds_sc)
        consts = dict(tmask=tm_r[...], tmask2=tm2_r[...], strict2=st2_r[...], exp_g=eg_r[...], exp_b=eb_r[...],
                      exp_gt=egt_r[...], dsel2=dsel_r[...], eye2=eye_r[...], bd2=bd_r[...], inv=inv_ref[0])

        def step(*a):
            return _gdn_step(*a, **consts)[:2]

        _, vjp = jax.vjp(step, sall_ref[0], _pairs(qf, qb, nb), _pairs(kf, kb, nb), _pairs(vf, vb, nb),
                         _gates(gf, gb, nb), al_ref[...], dt_ref[...])
        ds, dq, dk, dv, dg, dal, ddt = vjp((ds_sc[...], _pairs(dof, dob, nb)))
        ds_sc[...] = ds
        _unpairs(dq, dqf, dqb, nb)
        _unpairs(dk, dkf, dkb, nb)
        _unpairs(dv, dvf, dvb, nb)
        for b in range(nb):
            dgf[b] = dg[2 * b]
            dgb[b] = dg[2 * b + 1]
        dal_ref[...] += dal
        ddt_ref[...] += ddt

    def cspec(m):
        return pl.BlockSpec((nb, GC, BW), lambda t: (0, m(t), 0))

    def gspec(m):
        return pl.BlockSpec((nb, GC, 128), lambda t: (0, m(t), gblk))

    def gout(m):
        return pl.BlockSpec((nb, GC, 128), lambda t: (0, m(t), 0))

    fused, x_in, x_out, x_shape, x_scratch = _with_exchange(body, 14 + len(consts), 10, 1, xchg, n)
    return pl.pallas_call(
        fused, name="gdn_scan_bwd" + ("" if xchg is None else "_" + xchg[0]), grid=(n,),
        in_specs=[cspec(rf), cspec(rf), cspec(rf), gspec(rf), cspec(rb), cspec(rb), cspec(rb), gspec(rb),
                  _full((1, 128)), _full((1, 128))] + [_full(c.shape) for c in consts]
                 + [pl.BlockSpec((1, nz, PAIR_W, PAIR_W), lambda t: (n - 1 - t, 0, 0, 0)),
                    pl.BlockSpec((1, nz, GC, PAIR_W), lambda t: (n - 1 - t, 0, 0, 0)), cspec(rf), cspec(rb)] + x_in,
        out_specs=[cspec(rf), cspec(rf), cspec(rf), gout(rf), cspec(rb), cspec(rb), cspec(rb), gout(rb),
                   _full((1, 128)), _full((1, 128))] + x_out,
        out_shape=[_sds((nb, s, BW))] * 3 + [_sds((nb, s, 128))] + [_sds((nb, s, BW))] * 3 + [_sds((nb, s, 128))]
                  + [_sds((1, 128)), _sds((1, 128))] + x_shape,
        scratch_shapes=[pltpu.VMEM((nz, PAIR_W, PAIR_W), f32)] + x_scratch,
        compiler_params=_cparams(("arbitrary",), VMEM_BIG),
    )(cq, ck, cv, p3, cq, ck, cv, p3, alog, dtb, *consts, s_all, inv_all, do, do, *([] if xchg is None else xchg[1]))


def _sg_consts():
    hmp = np.zeros((2, NH, PAIR_W))
    for h in range(NH):
        hmp[h // 2, h, (h % 2) * HD:(h % 2 + 1) * HD] = 1.0
    bdr = (np.arange(2 * RC)[:, None] // RC == np.arange(PAIR_W)[None, :] // HD)
    return jnp.asarray(hmp, f32), jnp.asarray(bdr, f32)


def _sg_rows(s):
    return 6 * RC if s % (6 * RC) == 0 else 2 * RC


def _halves(ref):
    return ref[0, :, :PAIR_W], ref[0, :, PAIR_W:]


def sg_fwd(p3, w, b, hmp, bdr):
    nb, s, _ = p3.shape
    ts = _sg_rows(s)

    def body(u_ref, v_ref, z_ref, w_ref, b_ref, hm_ref, bdr_ref, y_ref):
        y0, y1 = _sg_block(*_halves(u_ref), *_halves(v_ref), *_halves(z_ref), w_ref[...], b_ref[...], hm_ref[...],
                           bdr_ref[...])
        y_ref[0, :, :PAIR_W] = y0.astype(bf16)
        y_ref[0, :, PAIR_W:] = y1.astype(bf16)

    def seg(k):
        return pl.BlockSpec((1, ts, BW), lambda bi, i: (bi, i, k))

    return pl.pallas_call(
        body, name="sg_fwd", grid=(nb, s // ts),
        in_specs=[seg(4), seg(5), seg(6), _full((NH, RC, RC)), _full((NH, RC)), _full(hmp.shape), _full(bdr.shape)],
        out_specs=pl.BlockSpec((1, ts, BW), lambda bi, i: (bi, i, 0)),
        out_shape=_sds((nb, s, BW), bf16),
        compiler_params=_cparams(("arbitrary", "arbitrary")),
    )(p3, p3, p3, w, b, hmp, bdr)


def sg_bwd(p3, w, b, hmp, bdr, dy3):
    nb, s, _ = p3.shape
    ts = _sg_rows(s)

    def body(u_ref, v_ref, z_ref, w_ref, b_ref, hm_ref, bdr_ref, dy_ref, du_ref, dv_ref, dz_ref, dw_ref, db_ref):
        @pl.when((pl.program_id(0) == 0) & (pl.program_id(1) == 0))
        def _():
            dw_ref[...] = jnp.zeros_like(dw_ref)
            db_ref[...] = jnp.zeros_like(db_ref)
        hm, bdr_v = hm_ref[...], bdr_ref[...]
        _, vjp = jax.vjp(lambda *a: _sg_block(*a, hm, bdr_v), *_halves(u_ref), *_halves(v_ref), *_halves(z_ref),
                         w_ref[...], b_ref[...])
        du0, du1, dv0, dv1, dz0, dz1, dw, db = vjp(_halves(dy_ref))
        for ref, a0, a1 in ((du_ref, du0, du1), (dv_ref, dv0, dv1), (dz_ref, dz0, dz1)):
            ref[0, :, :PAIR_W] = a0
            ref[0, :, PAIR_W:] = a1
        dw_ref[...] += dw
        db_ref[...] += db

    def seg(k):
        return pl.BlockSpec((1, ts, BW), lambda bi, i: (bi, i, k))

    blk = pl.BlockSpec((1, ts, BW), lambda bi, i: (bi, i, 0))
    return pl.pallas_call(
        body, name="sg_bwd", grid=(nb, s // ts),
        in_specs=[seg(4), seg(5), seg(6), _full((NH, RC, RC)), _full((NH, RC)), _full(hmp.shape), _full(bdr.shape),
                  seg(1)],
        out_specs=[blk, blk, blk, _full((NH, RC, RC)), _full((NH, RC))],
        out_shape=[_sds((nb, s, BW))] * 3 + [_sds((NH, RC, RC)), _sds((NH, RC))],
        compiler_params=_cparams(("arbitrary", "arbitrary"), VMEM_BIG),
    )(p3, p3, p3, w, b, hmp, bdr, dy3)


def _sc_fn(b, c, h, z, w, sd, su):
    return b * _conv3(c * h, w, sd, su) * _silu(z)


def sc_fwd(p3, w, t_ctx):
    nb, s, _ = p3.shape
    sd, su = _make_shifts(t_ctx, s)

    def body(b_ref, c_ref, h_ref, z_ref, w_ref, y_ref):
        y_ref[0] = _sc_fn(b_ref[0], c_ref[0], h_ref[0], z_ref[0], w_ref[...], sd, su).astype(bf16)

    def seg(k):
        return pl.BlockSpec((1, s, 128), lambda bi, j: (bi, 0, 2 * k + j))

    return pl.pallas_call(
        body, name="sc_fwd", grid=(nb, 2),
        in_specs=[seg(7), seg(8), seg(9), seg(10), pl.BlockSpec((3, 128), lambda bi, j: (0, j))],
        out_specs=pl.BlockSpec((1, s, 128), lambda bi, j: (bi, 0, j)),
        out_shape=_sds((nb, s, BW), bf16),
        compiler_params=_cparams(("arbitrary", "arbitrary"), VMEM_BIG),
    )(p3, p3, p3, p3, w)


def sc_bwd(p3, w, dy3, t_ctx):
    nb, s, _ = p3.shape
    sd, su = _make_shifts(t_ctx, s)

    def body(b_ref, c_ref, h_ref, z_ref, w_ref, dy_ref, db_ref, dc_ref, dh_ref, dz_ref, dw_ref):
        @pl.when(pl.program_id(1) == 0)
        def _():
            dw_ref[...] = jnp.zeros_like(dw_ref)
        _, vjp = jax.vjp(lambda b, c, h, z, w_: _sc_fn(b, c, h, z, w_, sd, su),
                         b_ref[0], c_ref[0], h_ref[0], z_ref[0], w_ref[...])
        db, dc, dh, dz, dw = vjp(dy_ref[0])
        db_ref[0] = db
        dc_ref[0] = dc
        dh_ref[0] = dh
        dz_ref[0] = dz
        dw_ref[...] += dw

    def seg(k):
        return pl.BlockSpec((1, s, 128), lambda j, bi: (bi, 0, 2 * k + j))

    blk = pl.BlockSpec((1, s, 128), lambda j, bi: (bi, 0, j))
    wspec = pl.BlockSpec((3, 128), lambda j, bi: (0, j))
    return pl.pallas_call(
        body, name="sc_bwd", grid=(2, nb),
        in_specs=[seg(7), seg(8), seg(9), seg(10), wspec, seg(2)],
        out_specs=[blk, blk, blk, blk, wspec],
        out_shape=[_sds((nb, s, BW))] * 4 + [_sds((3, BW))],
        compiler_params=_cparams(("arbitrary", "arbitrary"), VMEM_BIG),
    )(p3, p3, p3, p3, w, dy3)


def assemble_dp(pairs, singles_a, gdn_x, singles_b, gates):
    nb, s, _ = singles_a[0].shape
    flat = [a for pr in pairs for a in pr] + list(singles_a) + list(gdn_x) + list(singles_b) + list(gates)
    n_pairs, n_a, n_x, n_b = len(pairs), len(singles_a), len(gdn_x), len(singles_b)

    def body(*refs):
        out = refs[-1]
        ins = refs[:-1]
        col = 0
        for p in range(n_pairs):
            out[0, :, col:col + BW] = (ins[2 * p][0] + ins[2 * p + 1][0]).astype(bf16)
            col += BW
        k = 2 * n_pairs
        for _ in range(n_a + n_x + n_b):
            out[0, :, col:col + BW] = ins[k][0].astype(bf16)
            col += BW
            k += 1
        out[0, :, col:col + 128] = (ins[k][0] + ins[k + 1][0]).astype(bf16)
        out[0, :, col + 128:] = jnp.zeros((TM, PW - col - 128), bf16)

    def spec(a):
        return pl.BlockSpec((1, TM, a.shape[-1]), lambda b, j: (b, j, 0))

    return pl.pallas_call(
        body, name="assemble_dp", grid=(nb, s // TM),
        in_specs=[spec(a) for a in flat],
        out_specs=pl.BlockSpec((1, TM, PW), lambda b, j: (b, j, 0)),
        out_shape=_sds((nb, s, PW), bf16),
        compiler_params=_cparams(("arbitrary", "arbitrary")),
    )(*flat)


def mod_fwd(c_rows, w_mod, b_cols):
    nl, _, wc = w_mod.shape
    nr = c_rows.shape[0]

    def body(c_ref, w_ref, b_ref, o_ref):
        o_ref[0] = _dot(_silu(c_ref[...]), w_ref[0], precision=HI) + b_ref[0]

    return pl.pallas_call(
        body, name="mod_fwd", grid=(nl,),
        in_specs=[_full((nr, D)), pl.BlockSpec((1, D, wc), lambda l: (l, 0, 0)), pl.BlockSpec((1, 1, wc), lambda l: (l, 0, 0))],
        out_specs=pl.BlockSpec((1, nr, wc), lambda l: (l, 0, 0)),
        out_shape=_sds((nl, nr, wc)),
        compiler_params=_cparams(("arbitrary",)),
    )(c_rows, w_mod, b_cols)


def mod_bwd(c_rows, w_mod, dm_cols, dm_full):
    nl, _, wc = w_mod.shape
    nr = c_rows.shape[0]

    def body(c_ref, w_ref, dmc_ref, dmf_ref, gw_ref, gb_ref, dcc_ref):
        @pl.when(pl.program_id(0) == 0)
        def _():
            dcc_ref[...] = jnp.zeros_like(dcc_ref)
        a = _silu(c_ref[...])
        dmc = dmc_ref[0]
        gw_ref[0] = _dot_tn(a, dmc, precision=HI)
        gb_ref[0] = jnp.sum(dmf_ref[0], axis=0, keepdims=True)
        dcc_ref[...] += _dot_nt(dmc[nr - 8:nr], w_ref[0], precision=HI)

    return pl.pallas_call(
        body, name="mod_bwd", grid=(nl,),
        in_specs=[_full((nr, D)), pl.BlockSpec((1, D, wc), lambda l: (l, 0, 0)),
                  pl.BlockSpec((1, nr, wc), lambda l: (l, 0, 0)), pl.BlockSpec((1, nr, 3 * D), lambda l: (l, 0, 0))],
        out_specs=[pl.BlockSpec((1, D, wc), lambda l: (l, 0, 0)), pl.BlockSpec((1, 1, 3 * D), lambda l: (l, 0, 0)),
                   _full((8, D))],
        out_shape=[_sds((nl, D, wc)), _sds((nl, 1, 3 * D)), _sds((8, D))],
        compiler_params=_cparams(("arbitrary",)),
    )(c_rows, w_mod, dm_cols, dm_full)


def cctx_grad(parts, c_ctx):
    def body(p_ref, c_ref, o_ref):
        tot = p_ref[0, 0:1, :]
        for k in (2, 4, 6):
            tot = tot + p_ref[k, 0:1, :]
        c = c_ref[...]
        sg = jax.nn.sigmoid(c)
        o_ref[...] = tot * (sg * (1.0 + c * (1.0 - sg)))

    return pl.pallas_call(body, name="cctx_grad", out_shape=_sds((1, D)))(parts, c_ctx)


def sum_lead(x, out_dtype=f32, tr=256, rows=None):
    k, r, c = x.shape
    r = r if rows is None else rows
    tr = min(tr, r)
    assert r % tr == 0

    def body(x_ref, o_ref):
        tot = x_ref[0].astype(f32)
        for i in range(1, k):
            tot = tot + x_ref[i].astype(f32)
        o_ref[...] = tot.astype(out_dtype)

    return pl.pallas_call(
        body, name="sum_lead", grid=(r // tr,),
        in_specs=[pl.BlockSpec((k, tr, c), lambda i: (0, i, 0))],
        out_specs=pl.BlockSpec((tr, c), lambda i: (i, 0)),
        out_shape=_sds((r, c), out_dtype),
        compiler_params=_cparams(("arbitrary",)),
    )(x)


def adamw(w, m, v, g1, g2=None, tr=256):
    r, c = w.shape
    tr = min(tr, r)
    assert r % tr == 0
    two = g2 is not None
    c1 = 1.0 / (1.0 - ADAM_B1 ** ADAM_STEP)
    c2 = 1.0 / (1.0 - ADAM_B2 ** ADAM_STEP)

    def body(*refs):
        w_ref, m_ref, v_ref, g_ref = refs[:4]
        g = g_ref[...]
        if two:
            g = g + refs[4][...]
        go_ref, d_ref, mo_ref, vo_ref = refs[-4:]
        mn = ADAM_B1 * m_ref[...] + (1.0 - ADAM_B1) * g
        vn = ADAM_B2 * v_ref[...] + (1.0 - ADAM_B2) * (g * g)
        go_ref[...] = g
        mo_ref[...] = mn
        vo_ref[...] = vn
        d_ref[...] = -ADAM_LR * ((mn * c1) / (jnp.sqrt(vn * c2) + ADAM_EPS) + ADAM_WD * w_ref[...])

    blk = pl.BlockSpec((tr, c), lambda i: (i, 0))
    args = [w, m, v, g1] + ([g2] if two else [])
    return pl.pallas_call(
        body, name="adamw", grid=(r // tr,),
        in_specs=[blk] * len(args), out_specs=[blk] * 4, out_shape=[_sds((r, c))] * 4,
        compiler_params=_cparams(("arbitrary",)),
    )(*args)


def _my_pos():
    return lax.axis_index("x"), lax.axis_index("y"), lax.axis_index("c")


def exchange8(kind, x):
    out_shape = (N_DEV,) + x.shape if kind == "gather" else x.shape

    def body(x_ref, out_ref, send_sems, recv_sems, local_sem):
        mx, my, mc = _my_pos()
        me = 4 * mx + 2 * my + mc

        def part(k):
            return x_ref if kind == "gather" else x_ref.at[k]

        mine = pltpu.make_async_copy(part(me), out_ref.at[me], local_sem)
        mine.start()
        copies = []
        for k in range(1, N_DEV):
            peer = (mx ^ (k >> 2), my ^ ((k >> 1) & 1), mc ^ (k & 1))
            cp = pltpu.make_async_remote_copy(src_ref=part(me ^ k), dst_ref=out_ref.at[me], send_sem=send_sems.at[k - 1],
                                              recv_sem=recv_sems.at[k - 1], device_id=peer, device_id_type=MESH)
            cp.start()
            copies.append(cp)
        for k in range(1, N_DEV):
            pltpu.make_async_remote_copy(src_ref=part(me ^ k), dst_ref=out_ref.at[me ^ k], send_sem=send_sems.at[k - 1],
                                         recv_sem=recv_sems.at[k - 1], device_id=(mx, my, mc),
                                         device_id_type=MESH).wait_recv()
        for cp in copies:
            cp.wait_send()
        mine.wait()

    return pl.pallas_call(
        body, name=kind + "8", out_shape=_sds(out_shape, x.dtype),
        in_specs=[pl.BlockSpec(memory_space=pl.ANY)], out_specs=pl.BlockSpec(memory_space=pl.ANY),
        scratch_shapes=[pltpu.SemaphoreType.DMA((N_DEV - 1,)), pltpu.SemaphoreType.DMA((N_DEV - 1,)),
                        pltpu.SemaphoreType.DMA(())],
    )(x)


def gather8(x):
    return exchange8("gather", x)


def _chip_exchange(kind, src_refs, dst_refs, send_sems, recv_sems, local_sems):
    mx, my, mc = _my_pos()
    me = 2 * mx + my

    def copies():
        local, sends, recvs = [], [], []
        for i, (src, dst) in enumerate(zip(src_refs, dst_refs)):
            def part(k):
                return src if kind == "gather" else src.at[k]
            local.append(pltpu.make_async_copy(part(me), dst.at[me], local_sems.at[i]))
            for k in range(1, N_CHIPS):
                sem = dict(send_sem=send_sems.at[i, k - 1], recv_sem=recv_sems.at[i, k - 1], device_id_type=MESH)
                sends.append(pltpu.make_async_remote_copy(src_ref=part(me ^ k), dst_ref=dst.at[me],
                                                          device_id=(mx ^ (k >> 1), my ^ (k & 1), mc), **sem))
                recvs.append(pltpu.make_async_remote_copy(src_ref=part(me ^ k), dst_ref=dst.at[me ^ k],
                                                          device_id=(mx, my, mc), **sem))
        return local, sends, recvs

    def start():
        local, sends, _ = copies()
        for cp in local + sends:
            cp.start()

    def wait():
        local, sends, recvs = copies()
        for cp in recvs:
            cp.wait_recv()
        for cp in sends:
            cp.wait_send()
        for cp in local:
            cp.wait()

    return start, wait


def _exchange_scratch(n):
    return [pltpu.SemaphoreType.DMA((n, N_CHIPS - 1)), pltpu.SemaphoreType.DMA((n, N_CHIPS - 1)),
            pltpu.SemaphoreType.DMA((n,))]


def _exchange_shapes(kind, arrs):
    return [_sds(((N_CHIPS,) + a.shape) if kind == "gather" else a.shape, a.dtype) for a in arrs]


def exchange4(kind, arrs):
    n = len(arrs)

    def body(*refs):
        start, wait = _chip_exchange(kind, refs[:n], refs[n:2 * n], *refs[2 * n:])
        start()
        wait()

    any_ = pl.BlockSpec(memory_space=pl.ANY)
    return pl.pallas_call(
        body, name=kind + "4", out_shape=_exchange_shapes(kind, arrs),
        in_specs=[any_] * n, out_specs=[any_] * n, scratch_shapes=_exchange_scratch(n),
    )(*arrs)


def swap_sibling(x):
    def body(x_ref, out_ref, send_sem, recv_sem):
        mx, my, mc = _my_pos()
        cp = pltpu.make_async_remote_copy(src_ref=x_ref, dst_ref=out_ref, send_sem=send_sem, recv_sem=recv_sem,
                                          device_id=(mx, my, 1 - mc), device_id_type=MESH)
        cp.start()
        cp.wait()

    return pl.pallas_call(
        body, name="swap_sibling", out_shape=_sds(x.shape, x.dtype),
        in_specs=[pl.BlockSpec(memory_space=pl.ANY)], out_specs=pl.BlockSpec(memory_space=pl.ANY),
        scratch_shapes=[pltpu.SemaphoreType.DMA(()), pltpu.SemaphoreType.DMA(())],
    )(x)


PACK_ROWS = 64
SMALL = ("c_ctx", "b_mod", "g_pre", "g_post", "ret_norm_g", "sg_w", "sg_b", "sc_conv_w", "gdn_conv_w",
         "gdn_a_log", "gdn_dt_bias", "gdn_norm_g")


def _pack(arrs, width=D, mult=PACK_ROWS):
    rows = []
    for a in arrs:
        flat = a.reshape(-1)
        pad = (-flat.shape[0]) % width
        rows.append(jnp.pad(flat, (0, pad)).reshape(-1, width))
    out = jnp.concatenate(rows, axis=0)
    return jnp.pad(out, ((0, (-out.shape[0]) % mult), (0, 0)))


def _unpack(packed, shapes, width=D):
    outs, r = [], 0
    for shp in shapes:
        size = int(np.prod(shp))
        nr = -(-size // width)
        outs.append(packed[r:r + nr].reshape(-1)[:size].reshape(shp))
        r += nr
    return outs


def kernel(x, c, ctx, c_ctx, w_mod, b_mod, g_pre, g_post, w_in, w_out, ret_norm_g, sg_w, sg_b, sc_conv_w, gdn_conv_w, gdn_a_log, gdn_dt_bias, gdn_norm_g, loss_target, m_c_ctx, m_w_mod, m_b_mod, m_g_pre, m_g_post, m_w_in, m_w_out, m_ret_norm_g, m_sg_w, m_sg_b, m_sc_conv_w, m_gdn_conv_w, m_gdn_a_log, m_gdn_dt_bias, m_gdn_norm_g, v_c_ctx, v_w_mod, v_b_mod, v_g_pre, v_g_post, v_w_in, v_w_out, v_ret_norm_g, v_sg_w, v_sg_b, v_sc_conv_w, v_gdn_conv_w, v_gdn_a_log, v_gdn_dt_bias, v_gdn_norm_g):
    weights = dict(c_ctx=c_ctx, w_mod=w_mod, b_mod=b_mod, g_pre=g_pre, g_post=g_post, w_in=w_in, w_out=w_out,
                   ret_norm_g=ret_norm_g, sg_w=sg_w, sg_b=sg_b, sc_conv_w=sc_conv_w, gdn_conv_w=gdn_conv_w,
                   gdn_a_log=gdn_a_log, gdn_dt_bias=gdn_dt_bias, gdn_norm_g=gdn_norm_g)
    mom = dict(c_ctx=m_c_ctx, w_mod=m_w_mod, b_mod=m_b_mod, g_pre=m_g_pre, g_post=m_g_post, w_in=m_w_in,
               w_out=m_w_out, ret_norm_g=m_ret_norm_g, sg_w=m_sg_w, sg_b=m_sg_b, sc_conv_w=m_sc_conv_w,
               gdn_conv_w=m_gdn_conv_w, gdn_a_log=m_gdn_a_log, gdn_dt_bias=m_gdn_dt_bias, gdn_norm_g=m_gdn_norm_g)
    var = dict(c_ctx=v_c_ctx, w_mod=v_w_mod, b_mod=v_b_mod, g_pre=v_g_pre, g_post=v_g_post, w_in=v_w_in,
               w_out=v_w_out, ret_norm_g=v_ret_norm_g, sg_w=v_sg_w, sg_b=v_sg_b, sc_conv_w=v_sc_conv_w,
               gdn_conv_w=v_gdn_conv_w, gdn_a_log=v_gdn_a_log, gdn_dt_bias=v_gdn_dt_bias, gdn_norm_g=v_gdn_norm_g)

    nb, t_lat, _ = x.shape
    t_ctx = ctx.shape[1]
    s = t_ctx + t_lat
    n = nb * s
    sb = s // TM
    nl = w_in.shape[0]
    wc_in = w_in.shape[2]
    wc_mod = w_mod.shape[2]
    rows_out = w_out.shape[1]
    n_all = nb * N_DEV
    mx, my, mc = _my_pos()
    chip = 2 * mx + my
    dev = 2 * chip + mc

    sg_c = _sg_consts()
    bd = jnp.asarray(_block_diag())
    ret_c = _ret_consts(nb)
    gdn_c = _gdn_consts(nb)
    cos, sins = _rope_tables(t_lat, t_ctx)

    pre = _pack([c, sc_conv_w, gdn_conv_w], mult=8)
    pre_all = gather8(pre)
    c_parts, scw_parts, gcw_parts = [], [], []
    for k in range(N_DEV):
        ck, sk, gk = _unpack(pre_all[k], [c.shape, sc_conv_w.shape, gdn_conv_w.shape])
        c_parts.append(ck)
        if k % 2 == 0:
            scw_parts.append(sk)
            gcw_parts.append(gk)
    c_all = jnp.concatenate(c_parts, axis=0)
    sc_w_full = jnp.concatenate(scw_parts, axis=-1)
    gdn_w_full = jnp.concatenate(gcw_parts, axis=-1)
    c_rows = jnp.concatenate([c_all, c_ctx[None, :], jnp.zeros((7, D), f32)], axis=0)

    b_cols = lax.dynamic_slice_in_dim(b_mod, chip * wc_mod, wc_mod, axis=1)[:, None, :]
    mod_part = mod_fwd(c_rows, w_mod, b_cols)
    mod_all = gather8(mod_part)
    mod = jnp.concatenate([mod_all[2 * k] for k in range(N_CHIPS)], axis=-1)
    my_rows = jnp.concatenate([lax.dynamic_slice_in_dim(mod, dev * nb, nb, axis=1), mod[:, n_all:n_all + 1]], axis=1)
    shift_t = my_rows[:, :, None, 0:D]
    scale_t = my_rows[:, :, None, D:2 * D]
    gate_t = my_rows[:, :, None, 2 * D:3 * D]

    w_in_b, w_out_b = w_in.astype(bf16), w_out.astype(bf16)

    def full_weights(parts):
        wi = jnp.concatenate([parts[0][k] for k in range(N_CHIPS)], axis=-1)
        wo = jnp.concatenate([parts[1][k] for k in range(N_CHIPS)], axis=0)
        return jnp.pad(wi, ((0, 0), (0, PW - IN_W))), wo

    w_in_full, w_out_full = [None] * nl, [None] * nl
    w_in_full[0], w_out_full[0] = full_weights(exchange4("gather", [w_in_b[0], w_out_b[0]]))

    alog = jnp.pad(gdn_a_log.reshape(nl, 1, 8), ((0, 0), (0, 0), (0, 120)))
    dtb = jnp.pad(gdn_dt_bias.reshape(nl, 1, 8), ((0, 0), (0, 0), (0, 120)))
    gdn_ng = jnp.tile(gdn_norm_g, (1, NH))[:, None, :]
    ret_ng = ret_norm_g[:, None, :]

    xs = jnp.concatenate([ctx, x], axis=1).reshape(n, D)
    saved = []
    for l in range(nl):
        p, h = inproj_fwd(xs, shift_t[l], scale_t[l], g_pre[l][None, :], w_in_full[l], nb, sb)
        p3 = p.reshape(nb, s, PW)
        ro_f, ro_b, rs_all = ret_scan_fwd(p3, cos, sins, ret_c, t_ctx)
        y_ret = mix_finish_fwd(_ret_finish, "ret_finish_fwd", ro_f, ro_b, p3, 3, ret_ng[l], bd)
        y_sg = sg_fwd(p3, sg_w[l], sg_b[l], *sg_c)
        y_sc = sc_fwd(p3, sc_w_full[l], t_ctx)
        cq, ck, cv = [gdn_conv_fwd(p3, gdn_w_full[l][:, BW * i:BW * (i + 1)], 11 + i, t_ctx) for i in range(3)]
        nxt = None if l + 1 == nl else ("gather", [w_in_b[l + 1], w_out_b[l + 1]])
        go_f, go_b, *gs_all = gdn_scan_fwd(cq, ck, cv, p3, alog[l], dtb[l], gdn_c, t_ctx, nxt)
        if nxt is not None:
            w_in_full[l + 1], w_out_full[l + 1] = full_weights(gs_all[2:])
            gs_all = gs_all[:2]
        y_gdn = mix_finish_fwd(_gdn_finish, "gdn_finish_fwd", go_f, go_b, p3, 14, gdn_ng[l], bd)
        ys = [a.reshape(n, BW) for a in (y_ret, y_sg, y_sc, y_gdn)]
        x_new, o = outproj_fwd(ys, w_out_full[l], xs, gate_t[l], g_post[l][None, :], nb, sb)
        saved.append(dict(x=xs, h=h, p3=p3, ro=(ro_f, ro_b), rs=rs_all, c=(cq, ck, cv), go=(go_f, go_b), gs=gs_all,
                          ys=ys, o=o))
        xs = x_new

    dx3, loss_part = loss_head(xs.reshape(nb, s, D), loss_target, t_ctx)
    loss = lax.psum(loss_part[0, 0], ("x", "y", "c"))

    dxs = dx3.reshape(n, D)
    g_small = {k: [None] * nl for k in SMALL if k not in ("c_ctx", "b_mod")}
    dm_rows = [None] * nl
    slabs = None
    got_in, got_out = [None] * nl, [None] * nl
    for l in reversed(range(nl)):
        sv = saved[l]
        p3 = sv["p3"]
        dy, gw_out, dg_post, dgate = outproj_bwd(dxs, sv["o"], gate_t[l], g_post[l][None, :], sv["ys"], w_out_full[l], nb, sb)
        dy3 = dy.reshape(nb, s, D)
        r_do, r_dz, d_rng = mix_finish_bwd(_ret_finish, "ret_finish_bwd", *sv["ro"], p3, 3, ret_ng[l], bd, dy3, 0)
        r_d = ret_scan_bwd(p3, cos, sins, ret_c, sv["rs"], r_do, t_ctx)
        s_du, s_dv, s_dz, d_sgw, d_sgb = sg_bwd(p3, sg_w[l], sg_b[l], *sg_c, dy3)
        c_db, c_dc, c_dh, c_dz, d_scw = sc_bwd(p3, sc_w_full[l], dy3, t_ctx)
        g_do, g_dz, d_gng = mix_finish_bwd(_gdn_finish, "gdn_finish_bwd", *sv["go"], p3, 14, gdn_ng[l], bd, dy3, 3)
        g_d = gdn_scan_bwd(*sv["c"], p3, alog[l], dtb[l], gdn_c, *sv["gs"], g_do, t_ctx,
                           None if slabs is None else ("scatter", slabs))
        if slabs is not None:
            got_in[l + 1], got_out[l + 1] = g_d[10:]
        gx, d_gcw = [], []
        for i in range(3):
            dxi, dwi = gdn_conv_bwd(p3, gdn_w_full[l][:, BW * i:BW * (i + 1)], 11 + i, g_d[i], g_d[4 + i], t_ctx)
            gx.append(dxi)
            d_gcw.append(dwi)
        dp3 = assemble_dp([(r_d[0], r_d[3]), (r_d[1], r_d[4]), (r_d[2], r_d[5])],
                          [r_dz, s_du, s_dv, s_dz, c_db, c_dc, c_dh, c_dz], gx, [g_dz], [g_d[3], g_d[7]])
        dp = dp3.reshape(n, PW)
        dxs, dg_pre, dshift, dscale = inproj_bwd_x(dp, w_in_full[l], sv["x"], scale_t[l], g_pre[l][None, :], dxs, nb, sb)
        gw_in = dw_in(sv["h"], dp)
        slabs = [jnp.stack([gw_in[:, k * wc_in:(k + 1) * wc_in] for k in range(N_CHIPS)]),
                 gw_out.reshape(N_CHIPS, rows_out, D).astype(bf16)]
        g_small["g_pre"][l] = dg_pre[0]
        g_small["g_post"][l] = dg_post[0]
        g_small["ret_norm_g"][l] = d_rng[0]
        g_small["sg_w"][l] = d_sgw
        g_small["sg_b"][l] = d_sgb
        g_small["sc_conv_w"][l] = d_scw
        g_small["gdn_conv_w"][l] = jnp.concatenate(d_gcw, axis=-1)
        g_small["gdn_a_log"][l] = g_d[8][0, :8].reshape(2, NH)
        g_small["gdn_dt_bias"][l] = g_d[9][0, :8].reshape(2, NH)
        g_small["gdn_norm_g"][l] = d_gng[0].reshape(NH, HD)
        dm_rows[l] = jnp.concatenate([dshift, dscale, dgate], axis=-1)[:nb + 1]
    grad_x = dxs.reshape(nb, s, D)[:, t_ctx:, :]

    g_small = {k: jnp.stack(v) for k, v in g_small.items()}
    dm_rows = jnp.stack(dm_rows)
    names2 = [k for k in SMALL if k not in ("c_ctx", "b_mod")]
    pack_sum = _pack([g_small[k] for k in names2] + [dm_rows[:, nb:]])
    pack_own = _pack([dm_rows[:, :nb]], mult=8)
    rs = -(-pack_sum.shape[0] // (8 * N_DEV)) * 8
    slabs_sum = jnp.pad(pack_sum, ((0, N_DEV * rs - pack_sum.shape[0]), (0, 0))).reshape(N_DEV, rs, D)
    my_slab = sum_lead(exchange8("scatter", slabs_sum), tr=rs)
    all2 = gather8(jnp.concatenate([my_slab, pack_own], axis=0))
    tot2 = all2[:, :rs].reshape(N_DEV * rs, D)
    outs2 = _unpack(tot2, [g_small[k].shape for k in names2] + [(nl, 1, 3 * D)])
    grads = dict(zip(names2, outs2[:-1]))
    dm_own = jnp.stack([_unpack(all2[k, rs:], [(nl, nb, 3 * D)])[0] for k in range(N_DEV)])
    dm_own = jnp.transpose(dm_own, (1, 0, 2, 3)).reshape(nl, n_all, 3 * D)
    dm_all = jnp.concatenate([dm_own, jnp.pad(outs2[-1], ((0, 0), (0, 7), (0, 0)))], axis=1)
    grads["gdn_norm_g"] = sum_lead(jnp.transpose(grads["gdn_norm_g"], (1, 0, 2)), tr=nl)
    for k in ("sc_conv_w", "gdn_conv_w"):
        wc = weights[k].shape[2]
        grads[k] = lax.dynamic_slice_in_dim(grads[k], chip * wc, wc, axis=2)

    dm_cols = lax.dynamic_slice_in_dim(dm_all, chip * wc_mod, wc_mod, axis=2)
    g_w_mod, g_b_mod, dcc_part = mod_bwd(c_rows, w_mod, dm_cols, dm_all)
    grads["b_mod"] = g_b_mod[:, 0, :]
    grads["c_ctx"] = cctx_grad(gather8(dcc_part), c_ctx[None, :])[0]

    got_in[0], got_out[0] = exchange4("scatter", slabs)
    gin_mine = jnp.concatenate([sum_lead(a) for a in got_in], axis=0)
    gin_sib = swap_sibling(gin_mine)
    gout_mine = jnp.concatenate([sum_lead(a) for a in got_out], axis=0)
    gout_sib = swap_sibling(gout_mine)

    res = {}
    res["w_in"] = [a.reshape(w_in.shape) for a in adamw(w_in.reshape(nl * D, wc_in), m_w_in.reshape(nl * D, wc_in),
                                                          v_w_in.reshape(nl * D, wc_in), gin_mine, gin_sib)]
    res["w_out"] = [a.reshape(w_out.shape) for a in adamw(w_out.reshape(nl * rows_out, D), m_w_out.reshape(nl * rows_out, D),
                                                            v_w_out.reshape(nl * rows_out, D), gout_mine, gout_sib)]
    res["w_mod"] = [a.reshape(w_mod.shape) for a in adamw(w_mod.reshape(nl * D, wc_mod), m_w_mod.reshape(nl * D, wc_mod),
                                                            v_w_mod.reshape(nl * D, wc_mod), g_w_mod.reshape(nl * D, wc_mod))]
    shapes = [weights[k].shape for k in SMALL]
    small = adamw(_pack([weights[k] for k in SMALL]), _pack([mom[k] for k in SMALL]), _pack([var[k] for k in SMALL]),
                  _pack([grads[k].reshape(weights[k].shape) for k in SMALL]), tr=PACK_ROWS)
    small = [_unpack(a, shapes) for a in small]
    for i, k in enumerate(SMALL):
        res[k] = [small[j][i] for j in range(4)]

    order = ["c_ctx", "w_mod", "b_mod", "g_pre", "g_post", "w_in", "w_out", "ret_norm_g", "sg_w", "sg_b", "sc_conv_w",
             "gdn_conv_w", "gdn_a_log", "gdn_dt_bias", "gdn_norm_g"]
    return (loss, grad_x, *[res[k][0] for k in order], *[res[k][1] for k in order], *[res[k][2] for k in order],
            *[res[k][3] for k in order])
```

```python
import functools

import jax
import jax.numpy as jnp
import numpy as np
from jax import lax
from jax.experimental import pallas as pl
from jax.experimental.pallas import tpu as pltpu

f32 = jnp.float32
bf16 = jnp.bfloat16
HI = lax.Precision.HIGHEST
P3 = lax.Precision.HIGH
MESH = pl.DeviceIdType.MESH

EPS = 1e-6
D = 1024
NH = 4
HD = 64
BW = NH * HD
PAIR_W = 2 * HD
RC = 128
GC = 64
GRID_W = 64
ROPE_BASE = 10000.0
IN_W = 15 * BW + 16
PW = 4096
GATE_COL = 15 * BW
N_CHIPS = 4
N_DEV = 8
TM = 256
TP = 2 * TM
ADAM_LR, ADAM_B1, ADAM_B2, ADAM_EPS, ADAM_WD, ADAM_STEP = 0.001, 0.9, 0.999, 1e-08, 0.01, 10
LANE_HEAD = np.arange(BW) // HD
VMEM_BIG = 56 * 1024 * 1024


def _dot(a, b, precision=None):
    return jnp.dot(a, b, precision=precision, preferred_element_type=f32)


def _dot_nt(a, b, precision=None):
    return lax.dot_general(a, b, (((1,), (1,)), ((), ())), precision=precision, preferred_element_type=f32)


def _dot_tn(a, b, precision=None):
    return lax.dot_general(a, b, (((0,), (0,)), ((), ())), precision=precision, preferred_element_type=f32)


def _sds(shape, dtype=f32):
    return jax.ShapeDtypeStruct(shape, dtype)


def _cparams(sem=None, vmem=None):
    kw = {}
    if sem is not None:
        kw["dimension_semantics"] = sem
    if vmem is not None:
        kw["vmem_limit_bytes"] = vmem
    return pltpu.CompilerParams(**kw)


def _full(shape):
    n = len(shape)
    return pl.BlockSpec(shape, lambda *_: (0,) * n)


def _block_diag():
    return (LANE_HEAD[:, None] == LANE_HEAD[None, :]).astype(np.float32)


def _tau(c, d):
    return np.arange(c) if d == 0 else c - 1 - np.arange(c)


def _ret_consts(nb):
    lg = np.log(1.0 - 2.0 ** (-5.0 - np.arange(NH)))
    intra = np.zeros((2, 2, RC, 2 * RC)); qdec = np.zeros((2, 2, RC, PAIR_W)); kdec = np.zeros((2, 2, RC, PAIR_W))
    cd = np.zeros((2, 2, PAIR_W, PAIR_W))
    for d in range(2):
        t = _tau(RC, d)
        diff = t[:, None] - t[None, :]
        for p in range(2):
            lane_lg = lg[2 * p + np.arange(PAIR_W) // HD]
            for h in range(2):
                intra[d, p, :, h * RC:(h + 1) * RC] = np.where(diff >= 0, np.exp(np.maximum(diff, 0) * lg[2 * p + h]), 0.0)
            qdec[d, p] = np.exp((t[:, None] + 1.0) * lane_lg[None, :])
            kdec[d, p] = np.exp((RC - 1.0 - t[:, None]) * lane_lg[None, :])
            cd[d, p] = np.exp(RC * lane_lg)[:, None] * np.ones((1, PAIR_W))
    per_z = [np.tile(a.reshape((4,) + a.shape[2:]), (nb, 1, 1)) for a in (intra, qdec, kdec, cd)]
    bd2 = (np.arange(PAIR_W)[:, None] // HD == np.arange(PAIR_W)[None, :] // HD)
    bdr = (np.arange(2 * RC)[:, None] // RC == np.arange(PAIR_W)[None, :] // HD)
    return [jnp.asarray(a, f32) for a in per_z + [bd2, bdr]]


def _rope_tables(t_lat, t_ctx):
    nf = HD // 4
    inv = ROPE_BASE ** (-np.arange(nf) / nf)
    pos = np.arange(t_lat)
    ang_r = (pos // GRID_W)[:, None] * inv[None, :]
    ang_c = (pos % GRID_W)[:, None] * inv[None, :]
    ang = np.concatenate([ang_r, ang_r, ang_c, ang_c], axis=1)
    sign = np.concatenate([-np.ones(nf), np.ones(nf), -np.ones(nf), np.ones(nf)])
    cos = np.tile(np.cos(ang), (1, 2)); sins = np.tile(np.sin(ang) * sign, (1, 2))
    cos = np.concatenate([np.ones((t_ctx, PAIR_W)), cos]); sins = np.concatenate([np.zeros((t_ctx, PAIR_W)), sins])
    return jnp.asarray(cos, f32), jnp.asarray(sins, f32)


def _gdn_consts(nb):
    tmask = np.zeros((2, 2, GC, GC)); tmask2 = np.zeros((2, 2, GC, PAIR_W)); strict2 = np.zeros((2, 2, GC, PAIR_W))
    exp_g = np.zeros((2, 2, 128, PAIR_W)); exp_b = np.zeros((2, 2, 128, PAIR_W))
    for d in range(2):
        t = _tau(GC, d)
        tmask[d, :] = (t[:, None] >= t[None, :])
        tmask2[d, :] = np.tile(t[:, None] >= t[None, :], (1, 2))
        strict2[d, :] = np.tile(t[:, None] > t[None, :], (1, 2))
        for h in range(NH):
            exp_g[d, h // 2, 4 * d + h, (h % 2) * HD:(h % 2 + 1) * HD] = 1.0
            exp_b[d, h // 2, 8 + 4 * d + h, (h % 2) * HD:(h % 2 + 1) * HD] = 1.0
    exp_gt = np.transpose(exp_g, (0, 1, 3, 2))
    per_z = [np.tile(a.reshape((4,) + a.shape[2:]), (nb, 1, 1)) for a in (tmask, tmask2, strict2, exp_g, exp_b, exp_gt)]
    dsel2 = np.tile(np.eye(GC), (1, 2))
    eye2 = np.tile(np.eye(GC), (1, 2))
    bd2 = (np.arange(PAIR_W)[:, None] // HD == np.arange(PAIR_W)[None, :] // HD)
    return [jnp.asarray(a, f32) for a in per_z + [dsel2, eye2, bd2]]


def _swap16(x):
    lane = lax.broadcasted_iota(jnp.int32, x.shape, x.ndim - 1)
    n = x.shape[-1]
    return jnp.where(lane % 32 < 16, pltpu.roll(x, n - 16, axis=x.ndim - 1), pltpu.roll(x, 16, axis=x.ndim - 1))


@jax.custom_vjp
def _rot(x, cos, sins):
    return x * cos + _swap16(x) * sins


def _rot_fwd(x, cos, sins):
    return _rot(x, cos, sins), (cos, sins)


def _rot_bwd(res, g):
    cos, sins = res
    return g * cos + _swap16(g * sins), jnp.zeros_like(cos), jnp.zeros_like(sins)


_rot.defvjp(_rot_fwd, _rot_bwd)


def _silu(z):
    return z * jax.nn.sigmoid(z)


def _head_sum(x, bd):
    return _dot(x, bd, precision=P3)


def _ret_step(s, q, k, v, cos, sins, intra, qdec, kdec, cd, bd2, bdr):
    def bdiag(x):
        return jnp.concatenate([x, x], axis=1) * bdr

    qr = _rot(q, cos, sins)
    kr = _rot(k, cos, sins) * (HD ** -0.5)
    sc = _bmm_nt(qr, bdiag(kr)) * intra
    o = _bmm(qr * qdec, s) + _bmm(sc, bdiag(v))
    s_new = s * cd + bd2 * _bmm_tn(kr * kdec, v)
    return s_new, o


def _ret_finish(o_f, o_b, z, norm_g, bd):
    o = o_f + o_b
    mu = _head_sum(o, bd) * (1.0 / HD)
    xc = o - mu
    var = _head_sum(xc * xc, bd) * (1.0 / HD)
    return xc * lax.rsqrt(var + EPS) * norm_g * _silu(z)


def _softplus(x):
    return jnp.maximum(x, 0.0) + jnp.log(1.0 + jnp.exp(-jnp.abs(x)))


def _bmm(a, b, precision=None):
    return lax.dot_general(a, b, (((2,), (1,)), ((0,), (0,))), precision=precision, preferred_element_type=f32)


def _bmm_nt(a, b, precision=None):
    return lax.dot_general(a, b, (((2,), (2,)), ((0,), (0,))), precision=precision, preferred_element_type=f32)


def _bmm_tn(a, b, precision=None):
    return lax.dot_general(a, b, (((1,), (1,)), ((0,), (0,))), precision=precision, preferred_element_type=f32)


def _bdiag(x, bd2):
    return jnp.concatenate([x, x], axis=1) * bd2


@jax.custom_vjp
def _solve_given_inv(m, vb, kbg, inv, bd2):
    return _bmm(inv, _bdiag(vb, bd2), P3), _bmm(inv, _bdiag(kbg, bd2), P3)


def _solve_fwd(m, vb, kbg, inv, bd2):
    u, w = _solve_given_inv(m, vb, kbg, inv, bd2)
    return (u, w), (inv, u, w, bd2)


def _solve_bwd(res, cts):
    inv, u, w, bd2 = res
    du, dw = cts
    c = inv.shape[1]
    t = jnp.swapaxes(_bdiag(inv, bd2), 1, 2)
    inv_t = t[:, :c] + t[:, c:]
    dvb = _bmm(inv_t, _bdiag(du, bd2), P3)
    dkbg = _bmm(inv_t, _bdiag(dw, bd2), P3)
    dm = _bmm_nt(dvb, _bdiag(u, bd2), P3) + _bmm_nt(dkbg, _bdiag(w, bd2), P3)
    return dm, dvb, dkbg, jnp.zeros_like(inv), jnp.zeros_like(bd2)


_solve_given_inv.defvjp(_solve_fwd, _solve_bwd)


def _gdn_step(s, q, k, v, gate, alog, dtb, tmask, tmask2, strict2, exp_g, exp_b, exp_gt, dsel2, eye2, bd2, inv=None):
    z, c, w_ = q.shape
    ne = gate.shape[0]

    def per_pair(a):
        return jnp.broadcast_to(a[:, None], (ne, z // ne) + a.shape[1:]).reshape((z,) + a.shape[1:])

    def rows(a):
        return a.reshape(z * c, w_)

    def bdiag(x):
        return _bdiag(x, bd2)

    g = per_pair(-jnp.exp(alog) * _softplus(gate + dtb))
    beta = per_pair(jax.nn.sigmoid(gate))
    gl = _bmm(g, exp_g, P3)
    gc_l = _bmm(tmask, gl, P3)
    glast_l = jnp.sum(gl, axis=1, keepdims=True)
    glast = jnp.sum(g, axis=1, keepdims=True)
    beta_l = _bmm(beta, exp_b, P3)
    gc_r = jnp.sum(gc_l * dsel2, axis=1, keepdims=True)
    qn = q * lax.rsqrt(_dot(rows(q * q), bd2, P3).reshape(z, c, w_) + EPS)
    kn = k * lax.rsqrt(_dot(rows(k * k), bd2, P3).reshape(z, c, w_) + EPS)
    eg = jnp.exp(gc_l)
    kb = kn * beta_l
    vb = v * beta_l
    kbg = kb * eg
    qs = qn * (HD ** -0.5)
    dec = jnp.exp(jnp.where(tmask2 > 0, gc_l - gc_r, -1e30))
    kns = bdiag(kn)
    m = -(_bmm_nt(kb, kns) * dec * strict2)
    if inv is None:
        inv = eye2 + m
        p = m
        for _ in range(5):
            p = _bmm(p, bdiag(p), P3)
            inv = inv + _bmm(inv, bdiag(p), P3)
        u = _bmm(inv, bdiag(vb), P3)
        w = _bmm(inv, bdiag(kbg), P3)
    else:
        u, w = _solve_given_inv(m, vb, kbg, inv, bd2)
    v_new = u - _bmm(w, s)
    k_tail = kn * jnp.exp(glast_l - gc_l)
    cdec = jnp.sum(exp_gt * jnp.exp(glast), axis=-1, keepdims=True)
    s_new = s * cdec + bd2 * _bmm_tn(k_tail, v_new)
    a = _bmm_nt(qs, kns) * dec
    o = _bmm(qs * eg, s) + _bmm(a, bdiag(v_new))
    return s_new, o, inv


def _gdn_finish(o_f, o_b, z, norm_g, bd):
    o = o_f + o_b
    ms = _head_sum(o * o, bd) * (1.0 / HD)
    return o * lax.rsqrt(ms + EPS) * norm_g * _silu(z)


def _gelu(x):
    return 0.5 * x * (1.0 + jnp.tanh(0.7978845608028654 * (x + 0.044715 * (x * x * x))))


def _sg_block(u0, u1, v0, v1, z0, z1, w, b, hmp, bdr):
    ts = u0.shape[0]
    nc = ts // RC
    g0, g1 = _gelu(v0), _gelu(v1)
    mu = (jnp.sum(g0, axis=-1, keepdims=True) + jnp.sum(g1, axis=-1, keepdims=True)) * (1.0 / BW)
    x0, x1 = g0 - mu, g1 - mu
    var = (jnp.sum(x0 * x0, axis=-1, keepdims=True) + jnp.sum(x1 * x1, axis=-1, keepdims=True)) * (1.0 / BW)
    rstd = lax.rsqrt(var + EPS)
    ys = []
    for p, (u, xc, z) in enumerate(((u0, x0, z0), (u1, x1, z1))):
        vn = (xc * rstd).reshape(nc, RC, PAIR_W)
        wp = jnp.concatenate([w[2 * p], w[2 * p + 1]], axis=1)
        mix = _bmm(jnp.broadcast_to(wp, (nc, RC, 2 * RC)), jnp.concatenate([vn, vn], axis=1) * bdr)
        bias = _dot_tn(b, hmp[p], precision=HI)
        s = (mix + bias).reshape(ts, PAIR_W)
        ys.append(_gelu(u) * s * _silu(z))
    return ys[0], ys[1]


def _make_shifts(t_ctx, n):
    def dn(x):
        t = lax.broadcasted_iota(jnp.int32, x.shape, 0)
        return jnp.where((t != 0) & (t != t_ctx), pltpu.roll(x, 1, axis=0), 0.0)

    def up(x):
        t = lax.broadcasted_iota(jnp.int32, x.shape, 0)
        return jnp.where((t != t_ctx - 1) & (t != n - 1), pltpu.roll(x, n - 1, axis=0), 0.0)

    @jax.custom_vjp
    def shift_dn(x):
        return dn(x)
    shift_dn.defvjp(lambda x: (dn(x), None), lambda _, g: (up(g),))

    @jax.custom_vjp
    def shift_up(x):
        return up(x)
    shift_up.defvjp(lambda x: (up(x), None), lambda _, g: (dn(g),))
    return shift_dn, shift_up


def _conv3(x, w, shift_dn, shift_up):
    return shift_dn(x) * w[0:1] + x * w[1:2] + shift_up(x) * w[2:3]


def inproj_fwd(x, shift_t, scale_t, g_pre, w_in, n_batch, sb):
    n = x.shape[0]

    def sel(i):
        return jnp.where(i % sb == 0, n_batch, i // sb)

    def body(x_ref, sh0, sh1, sc0, sc1, g_ref, w_ref, p_ref, h_ref):
        hs = []
        for k, (sh_ref, sc_ref) in enumerate(((sh0, sc0), (sh1, sc1))):
            xv = x_ref[k * TM:(k + 1) * TM, :]
            r = xv * lax.rsqrt(jnp.mean(xv * xv, axis=-1, keepdims=True) + EPS)
            hs.append(((r * g_ref[...]) * (1.0 + sc_ref[0]) + sh_ref[0]).astype(bf16))
        hb = jnp.concatenate(hs, axis=0)
        h_ref[...] = hb
        p_ref[...] = _dot(hb, w_ref[...])

    def mrow(k):
        return pl.BlockSpec((1, 1, D), lambda i: (sel(2 * i + k), 0, 0))

    return pl.pallas_call(
        body, name="inproj_fwd", grid=(n // TP,),
        in_specs=[pl.BlockSpec((TP, D), lambda i: (i, 0)), mrow(0), mrow(1), mrow(0), mrow(1),
                  _full((1, D)), _full((D, PW))],
        out_specs=[pl.BlockSpec((TP, PW), lambda i: (i, 0)), pl.BlockSpec((TP, D), lambda i: (i, 0))],
        out_shape=[_sds((n, PW)), _sds((n, D), bf16)],
        compiler_params=_cparams(("arbitrary",), VMEM_BIG),
    )(x, shift_t, shift_t, scale_t, scale_t, g_pre, w_in)


def outproj_fwd(ys, w_out, x, gate_t, g_post, n_batch, sb):
    n = x.shape[0]

    def sel(i):
        return jnp.where(i % sb == 0, n_batch, i // sb)

    def body(y0, y1, y2, y3, w_ref, x_ref, gt0, gt1, g_ref, xn_ref, o_ref):
        y = jnp.concatenate([y0[...], y1[...], y2[...], y3[...]], axis=1)
        o = _dot(y, w_ref[...])
        o_ref[...] = o
        nrm = o * lax.rsqrt(jnp.mean(o * o, axis=-1, keepdims=True) + EPS) * g_ref[...]
        for k, gt_ref in enumerate((gt0, gt1)):
            rows = slice(k * TM, (k + 1) * TM)
            xn_ref[rows, :] = x_ref[rows, :] + gt_ref[0] * nrm[rows]

    def mrow(k):
        return pl.BlockSpec((1, 1, D), lambda i: (sel(2 * i + k), 0, 0))

    yspec = pl.BlockSpec((TP, BW), lambda i: (i, 0))
    return pl.pallas_call(
        body, name="outproj_fwd", grid=(n // TP,),
        in_specs=[yspec, yspec, yspec, yspec, _full((D, D)), pl.BlockSpec((TP, D), lambda i: (i, 0)),
                  mrow(0), mrow(1), _full((1, D))],
        out_specs=[pl.BlockSpec((TP, D), lambda i: (i, 0)), pl.BlockSpec((TP, D), lambda i: (i, 0))],
        out_shape=[_sds((n, D)), _sds((n, D))],
        compiler_params=_cparams(("arbitrary",), VMEM_BIG),
    )(*ys, w_out, x, gate_t, gate_t, g_post)


def _row_onehot(r):
    return lax.broadcasted_iota(jnp.int32, (8, 1), 0) == r


def outproj_bwd(dxn, o, gate_t, g_post, ys, w_out, n_batch, sb):
    n = dxn.shape[0]

    def sel(i):
        return jnp.where(i % sb == 0, n_batch, i // sb)

    def body(dxn_ref, o_ref, gt0, gt1, g_ref, y0, y1, y2, y3, w_ref, dy_ref, dw_ref, dg_ref, dgate_ref):
        i = pl.program_id(0)

        @pl.when(i == 0)
        def _():
            dw_ref[...] = jnp.zeros_like(dw_ref)
            dg_ref[...] = jnp.zeros_like(dg_ref)
            dgate_ref[...] = jnp.zeros_like(dgate_ref)

        g = g_ref[...]
        dos = []
        for k, gt_ref in enumerate((gt0, gt1)):
            rows = slice(k * TM, (k + 1) * TM)
            ov = o_ref[rows, :]
            rstd = lax.rsqrt(jnp.mean(ov * ov, axis=-1, keepdims=True) + EPS)
            r = ov * rstd
            dx = dxn_ref[rows, :]
            dgate_ref[...] += jnp.where(_row_onehot(sel(2 * i + k)), jnp.sum(dx * (r * g), axis=0, keepdims=True), 0.0)
            dn = dx * gt_ref[0]
            dg_ref[...] += jnp.sum(dn * r, axis=0, keepdims=True)
            dr = dn * g
            dos.append((rstd * (dr - r * jnp.mean(dr * r, axis=-1, keepdims=True))).astype(bf16))
        dob = jnp.concatenate(dos, axis=0)
        dy_ref[...] = _dot_nt(dob, w_ref[...])
        y = jnp.concatenate([y0[...], y1[...], y2[...], y3[...]], axis=1)
        dw_ref[...] += _dot_tn(y, dob)

    def mrow(k):
        return pl.BlockSpec((1, 1, D), lambda i: (sel(2 * i + k), 0, 0))

    yspec = pl.BlockSpec((TP, BW), lambda i: (i, 0))
    row = pl.BlockSpec((TP, D), lambda i: (i, 0))
    return pl.pallas_call(
        body, name="outproj_bwd", grid=(n // TP,),
        in_specs=[row, row, mrow(0), mrow(1), _full((1, D)), yspec, yspec, yspec, yspec, _full((D, D))],
        out_specs=[row, _full((D, D)), _full((1, D)), _full((8, D))],
        out_shape=[_sds((n, D)), _sds((D, D)), _sds((1, D)), _sds((8, D))],
        compiler_params=_cparams(("arbitrary",), VMEM_BIG),
    )(dxn, o, gate_t, gate_t, g_post, *ys, w_out)


def inproj_bwd_x(dp, w_in, x, scale_t, g_pre, dxn, n_batch, sb):
    n = x.shape[0]

    def sel(i):
        return jnp.where(i % sb == 0, n_batch, i // sb)

    def body(dp_ref, w_ref, x_ref, sc0, sc1, g_ref, dxn_ref, dx_ref, dg_ref, dsh_ref, dsc_ref):
        i = pl.program_id(0)

        @pl.when(i == 0)
        def _():
            dg_ref[...] = jnp.zeros_like(dg_ref)
            dsh_ref[...] = jnp.zeros_like(dsh_ref)
            dsc_ref[...] = jnp.zeros_like(dsc_ref)

        dh_all = _dot_nt(dp_ref[...], w_ref[...])
        g = g_ref[...]
        for k, sc_ref in enumerate((sc0, sc1)):
            rows = slice(k * TM, (k + 1) * TM)
            dh = dh_all[rows]
            xv = x_ref[rows, :]
            rstd = lax.rsqrt(jnp.mean(xv * xv, axis=-1, keepdims=True) + EPS)
            r = xv * rstd
            hot = _row_onehot(sel(2 * i + k))
            dsh_ref[...] += jnp.where(hot, jnp.sum(dh, axis=0, keepdims=True), 0.0)
            dsc_ref[...] += jnp.where(hot, jnp.sum(dh * (r * g), axis=0, keepdims=True), 0.0)
            t = dh * (1.0 + sc_ref[0])
            dg_ref[...] += jnp.sum(t * r, axis=0, keepdims=True)
            dr = t * g
            dx_ref[rows, :] = dxn_ref[rows, :] + rstd * (dr - r * jnp.mean(dr * r, axis=-1, keepdims=True))

    def mrow(k):
        return pl.BlockSpec((1, 1, D), lambda i: (sel(2 * i + k), 0, 0))

    row = pl.BlockSpec((TP, D), lambda i: (i, 0))
    return pl.pallas_call(
        body, name="inproj_bwd_x", grid=(n // TP,),
        in_specs=[pl.BlockSpec((TP, PW), lambda i: (i, 0)), _full((D, PW)), row, mrow(0), mrow(1), _full((1, D)), row],
        out_specs=[row, _full((1, D)), _full((8, D)), _full((8, D))],
        out_shape=[_sds((n, D)), _sds((1, D)), _sds((8, D)), _sds((8, D))],
        compiler_params=_cparams(("arbitrary",), VMEM_BIG),
    )(dp, w_in, x, scale_t, scale_t, g_pre, dxn)


def dw_in(h, dp):
    n = h.shape[0]
    tk, tn = (1536 if n % 1536 == 0 else 512), 1024
    nk = n // tk

    def body(h_ref, dp_ref, o_ref, acc):
        k = pl.program_id(1)

        @pl.when(k == 0)
        def _():
            acc[...] = jnp.zeros_like(acc)
        acc[...] += _dot_tn(h_ref[...], dp_ref[...])

        @pl.when(k == nk - 1)
        def _():
            o_ref[...] = acc[...].astype(bf16)

    return pl.pallas_call(
        body, name="dw_in", grid=(PW // tn, nk),
        in_specs=[pl.BlockSpec((tk, D), lambda j, k: (k, 0)), pl.BlockSpec((tk, tn), lambda j, k: (k, j))],
        out_specs=pl.BlockSpec((D, tn), lambda j, k: (0, j)),
        out_shape=_sds((D, PW), bf16),
        scratch_shapes=[pltpu.VMEM((D, tn), f32)],
        compiler_params=_cparams(("parallel", "arbitrary"), VMEM_BIG),
    )(h, dp)


def loss_head(xf, target, t_ctx):
    nb, s, _ = xf.shape
    jc = t_ctx // TM

    def body(x_ref, t_ref, dx_ref, l_ref):
        b, j = pl.program_id(0), pl.program_id(1)

        @pl.when((b == 0) & (j == 0))
        def _():
            l_ref[...] = jnp.zeros_like(l_ref)

        @pl.when(j < jc)
        def _():
            dx_ref[...] = jnp.zeros_like(dx_ref)

        @pl.when(j >= jc)
        def _():
            diff = x_ref[0] - t_ref[0]
            dx_ref[0] = diff * (1.0 / D)
            l_ref[...] += 0.5 * jnp.sum(diff * diff) * (1.0 / D)

    return pl.pallas_call(
        body, name="loss_head", grid=(nb, s // TM),
        in_specs=[pl.BlockSpec((1, TM, D), lambda b, j: (b, j, 0)),
                  pl.BlockSpec((1, TM, D), lambda b, j: (b, jnp.maximum(j - jc, 0), 0))],
        out_specs=[pl.BlockSpec((1, TM, D), lambda b, j: (b, j, 0)), _full((1, 128))],
        out_shape=[_sds((nb, s, D)), _sds((1, 128))],
        compiler_params=_cparams(("arbitrary", "arbitrary")),
    )(xf, target)


def _chunk_maps(n_ctx, n_lat):
    n = n_ctx + n_lat

    def cf(t):
        return t

    def cb(t):
        return jnp.where(t < n_ctx, n_ctx - 1 - t, n - 1 - t + n_ctx)
    return n, cf, cb


def ret_scan_fwd(p3, cos, sins, consts, t_ctx):
    nb, s, _ = p3.shape
    n, cf, cb = _chunk_maps(t_ctx // RC, (s - t_ctx) // RC)
    nz = 4 * nb

    def body(qf, kf, vf, qb, kb, vb, cosf, sinf, cosb, sinb, intra_r, qdec_r, kdec_r, cd_r, bd_r, bdr_r,
             of_ref, ob_ref, sall_ref, s_sc):
        @pl.when(pl.program_id(0) == 0)
        def _():
            s_sc[...] = jnp.zeros_like(s_sc)
        st = s_sc[...]
        sall_ref[0] = st
        s_new, o = _ret_step(st, _pairs(qf, qb, nb), _pairs(kf, kb, nb), _pairs(vf, vb, nb),
                             _pair_tables(cosf, cosb, nb), _pair_tables(sinf, sinb, nb), intra_r[...], qdec_r[...],
                             kdec_r[...], cd_r[...], bd_r[...], bdr_r[...])
        s_sc[...] = s_new
        _unpairs(o, of_ref, ob_ref, nb)

    def pspec(m, seg):
        return pl.BlockSpec((nb, RC, BW), lambda t: (0, m(t), seg))

    def tspec(m):
        return pl.BlockSpec((RC, PAIR_W), lambda t: (m(t), 0))

    return pl.pallas_call(
        body, name="ret_scan_fwd", grid=(n,),
        in_specs=[pspec(cf, 0), pspec(cf, 1), pspec(cf, 2), pspec(cb, 0), pspec(cb, 1), pspec(cb, 2),
                  tspec(cf), tspec(cf), tspec(cb), tspec(cb)] + [_full(c.shape) for c in consts],
        out_specs=[pl.BlockSpec((nb, RC, BW), lambda t: (0, cf(t), 0)),
                   pl.BlockSpec((nb, RC, BW), lambda t: (0, cb(t), 0)),
                   pl.BlockSpec((1, nz, PAIR_W, PAIR_W), lambda t: (t, 0, 0, 0))],
        out_shape=[_sds((nb, s, BW)), _sds((nb, s, BW)), _sds((n, nz, PAIR_W, PAIR_W))],
        scratch_shapes=[pltpu.VMEM((nz, PAIR_W, PAIR_W), f32)],
        compiler_params=_cparams(("arbitrary",)),
    )(p3, p3, p3, p3, p3, p3, cos, sins, cos, sins, *consts)


def ret_scan_bwd(p3, cos, sins, consts, s_all, do, t_ctx):
    nb, s, _ = p3.shape
    n, cf, cb = _chunk_maps(t_ctx // RC, (s - t_ctx) // RC)
    nz = 4 * nb

    def rf(t):
        return cf(n - 1 - t)

    def rb(t):
        return cb(n - 1 - t)

    def body(qf, kf, vf, qb, kb, vb, cosf, sinf, cosb, sinb, intra_r, qdec_r, kdec_r, cd_r, bd_r, bdr_r,
             sall_ref, dof, dob, dqf, dkf, dvf, dqb, dkb, dvb, ds_sc):
        @pl.when(pl.program_id(0) == 0)
        def _():
            ds_sc[...] = jnp.zeros_like(ds_sc)
        step = functools.partial(_ret_step, cos=_pair_tables(cosf, cosb, nb), sins=_pair_tables(sinf, sinb, nb),
                                 intra=intra_r[...], qdec=qdec_r[...], kdec=kdec_r[...], cd=cd_r[...], bd2=bd_r[...],
                                 bdr=bdr_r[...])
        _, vjp = jax.vjp(step, sall_ref[0], _pairs(qf, qb, nb), _pairs(kf, kb, nb), _pairs(vf, vb, nb))
        ds, dq, dk, dv = vjp((ds_sc[...], _pairs(dof, dob, nb)))
        ds_sc[...] = ds
        _unpairs(dq, dqf, dqb, nb)
        _unpairs(dk, dkf, dkb, nb)
        _unpairs(dv, dvf, dvb, nb)

    def pspec(m, seg):
        return pl.BlockSpec((nb, RC, BW), lambda t: (0, m(t), seg))

    def tspec(m):
        return pl.BlockSpec((RC, PAIR_W), lambda t: (m(t), 0))

    def ospec(m):
        return pl.BlockSpec((nb, RC, BW), lambda t: (0, m(t), 0))

    return pl.pallas_call(
        body, name="ret_scan_bwd", grid=(n,),
        in_specs=[pspec(rf, 0), pspec(rf, 1), pspec(rf, 2), pspec(rb, 0), pspec(rb, 1), pspec(rb, 2),
                  tspec(rf), tspec(rf), tspec(rb), tspec(rb)] + [_full(c.shape) for c in consts]
                 + [pl.BlockSpec((1, nz, PAIR_W, PAIR_W), lambda t: (n - 1 - t, 0, 0, 0)), ospec(rf), ospec(rb)],
        out_specs=[ospec(rf), ospec(rf), ospec(rf), ospec(rb), ospec(rb), ospec(rb)],
        out_shape=[_sds((nb, s, BW))] * 6,
        scratch_shapes=[pltpu.VMEM((nz, PAIR_W, PAIR_W), f32)],
        compiler_params=_cparams(("arbitrary",), VMEM_BIG),
    )(p3, p3, p3, p3, p3, p3, cos, sins, cos, sins, *consts, s_all, do, do)


def mix_finish_fwd(fn, name, o_f, o_b, p3, zseg, norm_g, bd):
    nb, s, _ = p3.shape

    def body(of_ref, ob_ref, z_ref, g_ref, bd_ref, y_ref):
        y_ref[0] = fn(of_ref[0], ob_ref[0], z_ref[0], g_ref[...], bd_ref[...]).astype(bf16)

    blk = pl.BlockSpec((1, TM, BW), lambda b, j: (b, j, 0))
    return pl.pallas_call(
        body, name=name, grid=(nb, s // TM),
        in_specs=[blk, blk, pl.BlockSpec((1, TM, BW), lambda b, j: (b, j, zseg)), _full((1, BW)), _full((BW, BW))],
        out_specs=blk, out_shape=_sds((nb, s, BW), bf16),
        compiler_params=_cparams(("arbitrary", "arbitrary")),
    )(o_f, o_b, p3, norm_g, bd)


def mix_finish_bwd(fn, name, o_f, o_b, p3, zseg, norm_g, bd, dy3, yseg):
    nb, s, _ = p3.shape

    def body(of_ref, ob_ref, z_ref, g_ref, bd_ref, dy_ref, do_ref, dz_ref, dg_ref):
        @pl.when((pl.program_id(0) == 0) & (pl.program_id(1) == 0))
        def _():
            dg_ref[...] = jnp.zeros_like(dg_ref)
        bdv = bd_ref[...]
        _, vjp = jax.vjp(lambda a, b, z, g: fn(a, b, z, g, bdv), of_ref[0], ob_ref[0], z_ref[0], g_ref[...])
        do, _, dz, dg = vjp(dy_ref[0])
        do_ref[0] = do
        dz_ref[0] = dz
        dg_ref[...] += dg

    blk = pl.BlockSpec((1, TM, BW), lambda b, j: (b, j, 0))
    return pl.pallas_call(
        body, name=name, grid=(nb, s // TM),
        in_specs=[blk, blk, pl.BlockSpec((1, TM, BW), lambda b, j: (b, j, zseg)), _full((1, BW)), _full((BW, BW)),
                  pl.BlockSpec((1, TM, BW), lambda b, j: (b, j, yseg))],
        out_specs=[blk, blk, _full((1, BW))],
        out_shape=[_sds((nb, s, BW)), _sds((nb, s, BW)), _sds((1, BW))],
        compiler_params=_cparams(("arbitrary", "arbitrary")),
    )(o_f, o_b, p3, norm_g, bd, dy3)


def gdn_conv_fwd(p3, w, seg, t_ctx):
    nb, s, _ = p3.shape
    sd, su = _make_shifts(t_ctx, s)

    def body(x_ref, w_ref, o_ref):
        o_ref[0] = _silu(_conv3(x_ref[0], w_ref[...], sd, su))

    return pl.pallas_call(
        body, name="gdn_conv_fwd", grid=(nb, 2),
        in_specs=[pl.BlockSpec((1, s, 128), lambda b, j: (b, 0, 2 * seg + j)), pl.BlockSpec((3, 128), lambda b, j: (0, j))],
        out_specs=pl.BlockSpec((1, s, 128), lambda b, j: (b, 0, j)),
        out_shape=_sds((nb, s, BW)),
        compiler_params=_cparams(("arbitrary", "arbitrary")),
    )(p3, w)


def gdn_conv_bwd(p3, w, seg, d_f, d_b, t_ctx):
    nb, s, _ = p3.shape
    sd, su = _make_shifts(t_ctx, s)

    def body(x_ref, w_ref, df_ref, db_ref, dx_ref, dw_ref):
        @pl.when(pl.program_id(1) == 0)
        def _():
            dw_ref[...] = jnp.zeros_like(dw_ref)
        _, vjp = jax.vjp(lambda x, w_: _silu(_conv3(x, w_, sd, su)), x_ref[0], w_ref[...])
        dx, dw = vjp(df_ref[0] + db_ref[0])
        dx_ref[0] = dx
        dw_ref[...] += dw

    blk = pl.BlockSpec((1, s, 128), lambda j, b: (b, 0, j))
    return pl.pallas_call(
        body, name="gdn_conv_bwd", grid=(2, nb),
        in_specs=[pl.BlockSpec((1, s, 128), lambda j, b: (b, 0, 2 * seg + j)), pl.BlockSpec((3, 128), lambda j, b: (0, j)),
                  blk, blk],
        out_specs=[blk, pl.BlockSpec((3, 128), lambda j, b: (0, j))],
        out_shape=[_sds((nb, s, BW)), _sds((3, BW))],
        compiler_params=_cparams(("arbitrary", "arbitrary"), VMEM_BIG),
    )(p3, w, d_f, d_b)


def _pairs(f_ref, b_ref, nb):
    return jnp.stack([r[b, :, PAIR_W * p:PAIR_W * (p + 1)] for b in range(nb) for r in (f_ref, b_ref) for p in range(2)])


def _pair_tables(f_ref, b_ref, nb):
    return jnp.stack([r[...] for _ in range(nb) for r in (f_ref, b_ref) for _ in range(2)])


def _gates(f_ref, b_ref, nb):
    return jnp.stack([r[b] for b in range(nb) for r in (f_ref, b_ref)])


def _unpairs(a, f_ref, b_ref, nb):
    for b in range(nb):
        for d, r in enumerate((f_ref, b_ref)):
            for p in range(2):
                r[b, :, PAIR_W * p:PAIR_W * (p + 1)] = a[4 * b + 2 * d + p]


def _with_exchange(body, n_in, n_out, n_scratch, xchg, n_steps):
    if xchg is None:
        return body, [], [], [], []
    kind, arrs = xchg
    nx = len(arrs)

    def fused(*refs):
        ins, rest = refs[:n_in], refs[n_in:]
        srcs, rest = rest[:nx], rest[nx:]
        outs, rest = rest[:n_out], rest[n_out:]
        dsts, rest = rest[:nx], rest[nx:]
        scratch, sems = rest[:n_scratch], rest[n_scratch:]
        start, wait = _peer_exchange(kind, "chips", srcs, dsts, *sems)
        pl.when(pl.program_id(0) == 0)(start)
        body(*ins, *outs, *scratch)
        pl.when(pl.program_id(0) == n_steps - 1)(wait)

    any_ = pl.BlockSpec(memory_space=pl.ANY)
    return fused, [any_] * nx, [any_] * nx, _exchange_shapes(kind, "chips", arrs), _exchange_scratch("chips", nx)


def gdn_scan_fwd(cq, ck, cv, p3, alog, dtb, consts, t_ctx, xchg=None):
    nb, s, _ = p3.shape
    n, cf, cb = _chunk_maps(t_ctx // GC, (s - t_ctx) // GC)
    gblk = GATE_COL // 128

    nz = 4 * nb

    def body(qf, kf, vf, gf, qb, kb, vb, gb, al_ref, dt_ref, tm_r, tm2_r, st2_r, eg_r, eb_r, egt_r, dsel_r, eye_r, bd_r,
             of_ref, ob_ref, sall_ref, inv_ref, s_sc):
        @pl.when(pl.program_id(0) == 0)
        def _():
            s_sc[...] = jnp.zeros_like(s_sc)
        st = s_sc[...]
        sall_ref[0] = st
        s_new, o, inv = _gdn_step(st, _pairs(qf, qb, nb), _pairs(kf, kb, nb), _pairs(vf, vb, nb), _gates(gf, gb, nb),
                                  al_ref[...], dt_ref[...], tm_r[...], tm2_r[...], st2_r[...], eg_r[...], eb_r[...],
                                  egt_r[...], dsel_r[...], eye_r[...], bd_r[...])
        s_sc[...] = s_new
        inv_ref[0] = inv
        _unpairs(o, of_ref, ob_ref, nb)

    def cspec(m):
        return pl.BlockSpec((nb, GC, BW), lambda t: (0, m(t), 0))

    def gspec(m):
        return pl.BlockSpec((nb, GC, 128), lambda t: (0, m(t), gblk))

    fused, x_in, x_out, x_shape, x_scratch = _with_exchange(body, 10 + len(consts), 4, 1, xchg, n)
    return pl.pallas_call(
        fused, name="gdn_scan_fwd" + ("" if xchg is None else "_" + xchg[0]), grid=(n,),
        in_specs=[cspec(cf), cspec(cf), cspec(cf), gspec(cf), cspec(cb), cspec(cb), cspec(cb), gspec(cb),
                  _full((1, 128)), _full((1, 128))] + [_full(c.shape) for c in consts] + x_in,
        out_specs=[cspec(cf), cspec(cb), pl.BlockSpec((1, nz, PAIR_W, PAIR_W), lambda t: (t, 0, 0, 0)),
                   pl.BlockSpec((1, nz, GC, PAIR_W), lambda t: (t, 0, 0, 0))] + x_out,
        out_shape=[_sds((nb, s, BW)), _sds((nb, s, BW)), _sds((n, nz, PAIR_W, PAIR_W)), _sds((n, nz, GC, PAIR_W))]
                  + x_shape,
        scratch_shapes=[pltpu.VMEM((nz, PAIR_W, PAIR_W), f32)] + x_scratch,
        compiler_params=_cparams(("arbitrary",)),
    )(cq, ck, cv, p3, cq, ck, cv, p3, alog, dtb, *consts, *([] if xchg is None else xchg[1]))


def gdn_scan_bwd(cq, ck, cv, p3, alog, dtb, consts, s_all, inv_all, do, t_ctx, xchg=None):
    nb, s, _ = p3.shape
    n, cf, cb = _chunk_maps(t_ctx // GC, (s - t_ctx) // GC)
    gblk = GATE_COL // 128

    def rf(t):
        return cf(n - 1 - t)

    def rb(t):
        return cb(n - 1 - t)

    nz = 4 * nb

    def body(qf, kf, vf, gf, qb, kb, vb, gb, al_ref, dt_ref, tm_r, tm2_r, st2_r, eg_r, eb_r, egt_r, dsel_r, eye_r, bd_r,
             sall_ref, inv_ref, dof, dob, dqf, dkf, dvf, dgf, dqb, dkb, dvb, dgb, dal_ref, ddt_ref, ds_sc):
        @pl.when(pl.program_id(0) == 0)
        def _():
            dal_ref[...] = jnp.zeros_like(dal_ref)
            ddt_ref[...] = jnp.zeros_like(ddt_ref)
            ds_sc[...] = jnp.zeros_like(ds_sc)
        consts = dict(tmask=tm_r[...], tmask2=tm2_r[...], strict2=st2_r[...], exp_g=eg_r[...], exp_b=eb_r[...],
                      exp_gt=egt_r[...], dsel2=dsel_r[...], eye2=eye_r[...], bd2=bd_r[...], inv=inv_ref[0])

        def step(*a):
            return _gdn_step(*a, **consts)[:2]

        _, vjp = jax.vjp(step, sall_ref[0], _pairs(qf, qb, nb), _pairs(kf, kb, nb), _pairs(vf, vb, nb),
                         _gates(gf, gb, nb), al_ref[...], dt_ref[...])
        ds, dq, dk, dv, dg, dal, ddt = vjp((ds_sc[...], _pairs(dof, dob, nb)))
        ds_sc[...] = ds
        _unpairs(dq, dqf, dqb, nb)
        _unpairs(dk, dkf, dkb, nb)
        _unpairs(dv, dvf, dvb, nb)
        for b in range(nb):
            dgf[b] = dg[2 * b]
            dgb[b] = dg[2 * b + 1]
        dal_ref[...] += dal
        ddt_ref[...] += ddt

    def cspec(m):
        return pl.BlockSpec((nb, GC, BW), lambda t: (0, m(t), 0))

    def gspec(m):
        return pl.BlockSpec((nb, GC, 128), lambda t: (0, m(t), gblk))

    def gout(m):
        return pl.BlockSpec((nb, GC, 128), lambda t: (0, m(t), 0))

    fused, x_in, x_out, x_shape, x_scratch = _with_exchange(body, 14 + len(consts), 10, 1, xchg, n)
    return pl.pallas_call(
        fused, name="gdn_scan_bwd" + ("" if xchg is None else "_" + xchg[0]), grid=(n,),
        in_specs=[cspec(rf), cspec(rf), cspec(rf), gspec(rf), cspec(rb), cspec(rb), cspec(rb), gspec(rb),
                  _full((1, 128)), _full((1, 128))] + [_full(c.shape) for c in consts]
                 + [pl.BlockSpec((1, nz, PAIR_W, PAIR_W), lambda t: (n - 1 - t, 0, 0, 0)),
                    pl.BlockSpec((1, nz, GC, PAIR_W), lambda t: (n - 1 - t, 0, 0, 0)), cspec(rf), cspec(rb)] + x_in,
        out_specs=[cspec(rf), cspec(rf), cspec(rf), gout(rf), cspec(rb), cspec(rb), cspec(rb), gout(rb),
                   _full((1, 128)), _full((1, 128))] + x_out,
        out_shape=[_sds((nb, s, BW))] * 3 + [_sds((nb, s, 128))] + [_sds((nb, s, BW))] * 3 + [_sds((nb, s, 128))]
                  + [_sds((1, 128)), _sds((1, 128))] + x_shape,
        scratch_shapes=[pltpu.VMEM((nz, PAIR_W, PAIR_W), f32)] + x_scratch,
        compiler_params=_cparams(("arbitrary",), VMEM_BIG),
    )(cq, ck, cv, p3, cq, ck, cv, p3, alog, dtb, *consts, s_all, inv_all, do, do, *([] if xchg is None else xchg[1]))


def _sg_consts():
    hmp = np.zeros((2, NH, PAIR_W))
    for h in range(NH):
        hmp[h // 2, h, (h % 2) * HD:(h % 2 + 1) * HD] = 1.0
    bdr = (np.arange(2 * RC)[:, None] // RC == np.arange(PAIR_W)[None, :] // HD)
    return jnp.asarray(hmp, f32), jnp.asarray(bdr, f32)


def _sg_rows(s):
    return 6 * RC if s % (6 * RC) == 0 else 2 * RC


def _halves(ref):
    return ref[0, :, :PAIR_W], ref[0, :, PAIR_W:]


def sg_fwd(p3, w, b, hmp, bdr):
    nb, s, _ = p3.shape
    ts = _sg_rows(s)

    def body(u_ref, v_ref, z_ref, w_ref, b_ref, hm_ref, bdr_ref, y_ref):
        y0, y1 = _sg_block(*_halves(u_ref), *_halves(v_ref), *_halves(z_ref), w_ref[...], b_ref[...], hm_ref[...],
                           bdr_ref[...])
        y_ref[0, :, :PAIR_W] = y0.astype(bf16)
        y_ref[0, :, PAIR_W:] = y1.astype(bf16)

    def seg(k):
        return pl.BlockSpec((1, ts, BW), lambda bi, i: (bi, i, k))

    return pl.pallas_call(
        body, name="sg_fwd", grid=(nb, s // ts),
        in_specs=[seg(4), seg(5), seg(6), _full((NH, RC, RC)), _full((NH, RC)), _full(hmp.shape), _full(bdr.shape)],
        out_specs=pl.BlockSpec((1, ts, BW), lambda bi, i: (bi, i, 0)),
        out_shape=_sds((nb, s, BW), bf16),
        compiler_params=_cparams(("arbitrary", "arbitrary")),
    )(p3, p3, p3, w, b, hmp, bdr)


def sg_bwd(p3, w, b, hmp, bdr, dy3):
    nb, s, _ = p3.shape
    ts = _sg_rows(s)

    def body(u_ref, v_ref, z_ref, w_ref, b_ref, hm_ref, bdr_ref, dy_ref, du_ref, dv_ref, dz_ref, dw_ref, db_ref):
        @pl.when((pl.program_id(0) == 0) & (pl.program_id(1) == 0))
        def _():
            dw_ref[...] = jnp.zeros_like(dw_ref)
            db_ref[...] = jnp.zeros_like(db_ref)
        hm, bdr_v = hm_ref[...], bdr_ref[...]
        _, vjp = jax.vjp(lambda *a: _sg_block(*a, hm, bdr_v), *_halves(u_ref), *_halves(v_ref), *_halves(z_ref),
                         w_ref[...], b_ref[...])
        du0, du1, dv0, dv1, dz0, dz1, dw, db = vjp(_halves(dy_ref))
        for ref, a0, a1 in ((du_ref, du0, du1), (dv_ref, dv0, dv1), (dz_ref, dz0, dz1)):
            ref[0, :, :PAIR_W] = a0
            ref[0, :, PAIR_W:] = a1
        dw_ref[...] += dw
        db_ref[...] += db

    def seg(k):
        return pl.BlockSpec((1, ts, BW), lambda bi, i: (bi, i, k))

    blk = pl.BlockSpec((1, ts, BW), lambda bi, i: (bi, i, 0))
    return pl.pallas_call(
        body, name="sg_bwd", grid=(nb, s // ts),
        in_specs=[seg(4), seg(5), seg(6), _full((NH, RC, RC)), _full((NH, RC)), _full(hmp.shape), _full(bdr.shape),
                  seg(1)],
        out_specs=[blk, blk, blk, _full((NH, RC, RC)), _full((NH, RC))],
        out_shape=[_sds((nb, s, BW))] * 3 + [_sds((NH, RC, RC)), _sds((NH, RC))],
        compiler_params=_cparams(("arbitrary", "arbitrary"), VMEM_BIG),
    )(p3, p3, p3, w, b, hmp, bdr, dy3)


def _sc_fn(b, c, h, z, w, sd, su):
    return b * _conv3(c * h, w, sd, su) * _silu(z)


def sc_fwd(p3, w, t_ctx):
    nb, s, _ = p3.shape
    sd, su = _make_shifts(t_ctx, s)

    def body(b_ref, c_ref, h_ref, z_ref, w_ref, y_ref):
        y_ref[0] = _sc_fn(b_ref[0], c_ref[0], h_ref[0], z_ref[0], w_ref[...], sd, su).astype(bf16)

    def seg(k):
        return pl.BlockSpec((1, s, 128), lambda bi, j: (bi, 0, 2 * k + j))

    return pl.pallas_call(
        body, name="sc_fwd", grid=(nb, 2),
        in_specs=[seg(7), seg(8), seg(9), seg(10), pl.BlockSpec((3, 128), lambda bi, j: (0, j))],
        out_specs=pl.BlockSpec((1, s, 128), lambda bi, j: (bi, 0, j)),
        out_shape=_sds((nb, s, BW), bf16),
        compiler_params=_cparams(("arbitrary", "arbitrary"), VMEM_BIG),
    )(p3, p3, p3, p3, w)


def sc_bwd(p3, w, dy3, t_ctx):
    nb, s, _ = p3.shape
    sd, su = _make_shifts(t_ctx, s)

    def body(b_ref, c_ref, h_ref, z_ref, w_ref, dy_ref, db_ref, dc_ref, dh_ref, dz_ref, dw_ref):
        @pl.when(pl.program_id(1) == 0)
        def _():
            dw_ref[...] = jnp.zeros_like(dw_ref)
        _, vjp = jax.vjp(lambda b, c, h, z, w_: _sc_fn(b, c, h, z, w_, sd, su),
                         b_ref[0], c_ref[0], h_ref[0], z_ref[0], w_ref[...])
        db, dc, dh, dz, dw = vjp(dy_ref[0])
        db_ref[0] = db
        dc_ref[0] = dc
        dh_ref[0] = dh
        dz_ref[0] = dz
        dw_ref[...] += dw

    def seg(k):
        return pl.BlockSpec((1, s, 128), lambda j, bi: (bi, 0, 2 * k + j))

    blk = pl.BlockSpec((1, s, 128), lambda j, bi: (bi, 0, j))
    wspec = pl.BlockSpec((3, 128), lambda j, bi: (0, j))
    return pl.pallas_call(
        body, name="sc_bwd", grid=(2, nb),
        in_specs=[seg(7), seg(8), seg(9), seg(10), wspec, seg(2)],
        out_specs=[blk, blk, blk, blk, wspec],
        out_shape=[_sds((nb, s, BW))] * 4 + [_sds((3, BW))],
        compiler_params=_cparams(("arbitrary", "arbitrary"), VMEM_BIG),
    )(p3, p3, p3, p3, w, dy3)


def assemble_dp(pairs, singles_a, gdn_x, singles_b, gates):
    nb, s, _ = singles_a[0].shape
    flat = [a for pr in pairs for a in pr] + list(singles_a) + list(gdn_x) + list(singles_b) + list(gates)
    n_pairs, n_a, n_x, n_b = len(pairs), len(singles_a), len(gdn_x), len(singles_b)

    def body(*refs):
        out = refs[-1]
        ins = refs[:-1]
        col = 0
        for p in range(n_pairs):
            out[0, :, col:col + BW] = (ins[2 * p][0] + ins[2 * p + 1][0]).astype(bf16)
            col += BW
        k = 2 * n_pairs
        for _ in range(n_a + n_x + n_b):
            out[0, :, col:col + BW] = ins[k][0].astype(bf16)
            col += BW
            k += 1
        out[0, :, col:col + 128] = (ins[k][0] + ins[k + 1][0]).astype(bf16)
        out[0, :, col + 128:] = jnp.zeros((TM, PW - col - 128), bf16)

    def spec(a):
        return pl.BlockSpec((1, TM, a.shape[-1]), lambda b, j: (b, j, 0))

    return pl.pallas_call(
        body, name="assemble_dp", grid=(nb, s // TM),
        in_specs=[spec(a) for a in flat],
        out_specs=pl.BlockSpec((1, TM, PW), lambda b, j: (b, j, 0)),
        out_shape=_sds((nb, s, PW), bf16),
        compiler_params=_cparams(("arbitrary", "arbitrary")),
    )(*flat)


def mod_fwd(c_rows, w_mod, b_cols):
    nl, _, wc = w_mod.shape
    nr = c_rows.shape[0]

    def body(c_ref, w_ref, b_ref, o_ref):
        o_ref[0] = _dot(_silu(c_ref[...]), w_ref[0], precision=HI) + b_ref[0]

    return pl.pallas_call(
        body, name="mod_fwd", grid=(nl,),
        in_specs=[_full((nr, D)), pl.BlockSpec((1, D, wc), lambda l: (l, 0, 0)), pl.BlockSpec((1, 1, wc), lambda l: (l, 0, 0))],
        out_specs=pl.BlockSpec((1, nr, wc), lambda l: (l, 0, 0)),
        out_shape=_sds((nl, nr, wc)),
        compiler_params=_cparams(("arbitrary",)),
    )(c_rows, w_mod, b_cols)


def mod_bwd(c_rows, w_mod, dm_cols, dm_full):
    nl, _, wc = w_mod.shape
    nr = c_rows.shape[0]

    def body(c_ref, w_ref, dmc_ref, dmf_ref, gw_ref, gb_ref, dcc_ref):
        @pl.when(pl.program_id(0) == 0)
        def _():
            dcc_ref[...] = jnp.zeros_like(dcc_ref)
        a = _silu(c_ref[...])
        dmc = dmc_ref[0]
        gw_ref[0] = _dot_tn(a, dmc, precision=HI)
        gb_ref[0] = jnp.sum(dmf_ref[0], axis=0, keepdims=True)
        dcc_ref[...] += _dot_nt(dmc[nr - 8:nr], w_ref[0], precision=HI)

    return pl.pallas_call(
        body, name="mod_bwd", grid=(nl,),
        in_specs=[_full((nr, D)), pl.BlockSpec((1, D, wc), lambda l: (l, 0, 0)),
                  pl.BlockSpec((1, nr, wc), lambda l: (l, 0, 0)), pl.BlockSpec((1, nr, 3 * D), lambda l: (l, 0, 0))],
        out_specs=[pl.BlockSpec((1, D, wc), lambda l: (l, 0, 0)), pl.BlockSpec((1, 1, 3 * D), lambda l: (l, 0, 0)),
                   _full((8, D))],
        out_shape=[_sds((nl, D, wc)), _sds((nl, 1, 3 * D)), _sds((8, D))],
        compiler_params=_cparams(("arbitrary",)),
    )(c_rows, w_mod, dm_cols, dm_full)


def cctx_grad(parts, c_ctx):
    def body(p_ref, c_ref, o_ref):
        tot = p_ref[0, 0:1, :]
        for k in (2, 4, 6):
            tot = tot + p_ref[k, 0:1, :]
        c = c_ref[...]
        sg = jax.nn.sigmoid(c)
        o_ref[...] = tot * (sg * (1.0 + c * (1.0 - sg)))

    return pl.pallas_call(body, name="cctx_grad", out_shape=_sds((1, D)))(parts, c_ctx)


def sum_lead(x, out_dtype=f32, tr=256, rows=None):
    k, r, c = x.shape
    r = r if rows is None else rows
    tr = min(tr, r)
    assert r % tr == 0

    def body(x_ref, o_ref):
        tot = x_ref[0].astype(f32)
        for i in range(1, k):
            tot = tot + x_ref[i].astype(f32)
        o_ref[...] = tot.astype(out_dtype)

    return pl.pallas_call(
        body, name="sum_lead", grid=(r // tr,),
        in_specs=[pl.BlockSpec((k, tr, c), lambda i: (0, i, 0))],
        out_specs=pl.BlockSpec((tr, c), lambda i: (i, 0)),
        out_shape=_sds((r, c), out_dtype),
        compiler_params=_cparams(("arbitrary",)),
    )(x)


def adamw(w, m, v, g1, g2=None, tr=256):
    lead = w.shape[:-2]
    r, c = w.shape[-2:]
    tr = min(tr, r)
    assert r % tr == 0 and len(lead) <= 1
    two = g2 is not None
    c1 = 1.0 / (1.0 - ADAM_B1 ** ADAM_STEP)
    c2 = 1.0 / (1.0 - ADAM_B2 ** ADAM_STEP)

    def body(*refs):
        w_ref, m_ref, v_ref, g_ref = refs[:4]
        g = g_ref[...]
        if two:
            g = g + refs[4][...]
        go_ref, d_ref, mo_ref, vo_ref = refs[-4:]
        mn = ADAM_B1 * m_ref[...] + (1.0 - ADAM_B1) * g
        vn = ADAM_B2 * v_ref[...] + (1.0 - ADAM_B2) * (g * g)
        go_ref[...] = g
        mo_ref[...] = mn
        vo_ref[...] = vn
        d_ref[...] = -ADAM_LR * ((mn * c1) / (jnp.sqrt(vn * c2) + ADAM_EPS) + ADAM_WD * w_ref[...])

    if lead:
        blk = pl.BlockSpec((1, tr, c), lambda l, i: (l, i, 0))
        grid = (lead[0], r // tr)
    else:
        blk = pl.BlockSpec((tr, c), lambda i: (i, 0))
        grid = (r // tr,)
    args = [w, m, v, g1] + ([g2] if two else [])
    return pl.pallas_call(
        body, name="adamw", grid=grid,
        in_specs=[blk] * len(args), out_specs=[blk] * 4, out_shape=[_sds(w.shape)] * 4,
        compiler_params=_cparams(("arbitrary",) * len(grid)),
    )(*args)


def _my_pos():
    return lax.axis_index("x"), lax.axis_index("y"), lax.axis_index("c")


GROUP_SIZE = {"devices": N_DEV, "chips": N_CHIPS, "cores": 2}


def _peer_exchange(kind, group, src_refs, dst_refs, send_sems, recv_sems, local_sems):
    mx, my, mc = _my_pos()
    n = GROUP_SIZE[group]
    if group == "devices":
        me = 4 * mx + 2 * my + mc
    elif group == "chips":
        me = 2 * mx + my
    else:
        me = mc

    def peer(k):
        if group == "devices":
            return (mx ^ (k >> 2), my ^ ((k >> 1) & 1), mc ^ (k & 1))
        if group == "chips":
            return (mx ^ (k >> 1), my ^ (k & 1), mc)
        return (mx, my, mc ^ k)

    def copies():
        local, sends, recvs = [], [], []
        for i, (src, dst) in enumerate(zip(src_refs, dst_refs)):
            def part(k):
                return src.at[k] if kind == "scatter" else src

            def slab(k):
                return dst if kind == "send" else dst.at[k]

            if kind != "send":
                local.append(pltpu.make_async_copy(part(me), dst.at[me], local_sems.at[i]))
            for k in range(1, n):
                sem = dict(send_sem=send_sems.at[i, k - 1], recv_sem=recv_sems.at[i, k - 1], device_id_type=MESH)
                sends.append(pltpu.make_async_remote_copy(src_ref=part(me ^ k), dst_ref=slab(me), device_id=peer(k), **sem))
                recvs.append(pltpu.make_async_remote_copy(src_ref=part(me ^ k), dst_ref=slab(me ^ k),
                                                          device_id=(mx, my, mc), **sem))
        return local, sends, recvs

    def start():
        local, sends, _ = copies()
        for cp in local + sends:
            cp.start()

    def wait():
        local, sends, recvs = copies()
        for cp in recvs:
            cp.wait_recv()
        for cp in sends:
            cp.wait_send()
        for cp in local:
            cp.wait()

    return start, wait


def _exchange_scratch(group, n):
    k = GROUP_SIZE[group] - 1
    return [pltpu.SemaphoreType.DMA((n, k)), pltpu.SemaphoreType.DMA((n, k)), pltpu.SemaphoreType.DMA((n,))]


def _exchange_shapes(kind, group, arrs):
    return [_sds(((GROUP_SIZE[group],) + a.shape) if kind == "gather" else a.shape, a.dtype) for a in arrs]


def exchange(name, parts):
    counts = [len(arrs) for _, _, arrs in parts]
    total = sum(counts)

    def body(*refs):
        srcs, dsts, sems = refs[:total], refs[total:2 * total], refs[2 * total:]
        ops, at = [], 0
        for j, (kind, group, arrs) in enumerate(parts):
            ops.append(_peer_exchange(kind, group, srcs[at:at + counts[j]], dsts[at:at + counts[j]], *sems[3 * j:3 * j + 3]))
            at += counts[j]
        for start, _ in ops:
            start()
        for _, wait in ops:
            wait()

    any_ = pl.BlockSpec(memory_space=pl.ANY)
    flat = [a for _, _, arrs in parts for a in arrs]
    outs = pl.pallas_call(
        body, name=name, out_shape=[sh for kind, group, arrs in parts for sh in _exchange_shapes(kind, group, arrs)],
        in_specs=[any_] * total, out_specs=[any_] * total,
        scratch_shapes=[sc for _, group, arrs in parts for sc in _exchange_scratch(group, len(arrs))],
    )(*flat)
    res, at = [], 0
    for cnt in counts:
        res.append(list(outs[at:at + cnt]))
        at += cnt
    return res


def gather8(x):
    return exchange("gather8", [("gather", "devices", [x])])[0][0]


PACK_ROWS = 64
SMALL = ("c_ctx", "b_mod", "g_pre", "g_post", "ret_norm_g", "sg_w", "sg_b", "sc_conv_w", "gdn_conv_w",
         "gdn_a_log", "gdn_dt_bias", "gdn_norm_g")


def _pack(arrs, width=D, mult=PACK_ROWS):
    rows = []
    for a in arrs:
        flat = a.reshape(-1)
        pad = (-flat.shape[0]) % width
        rows.append(jnp.pad(flat, (0, pad)).reshape(-1, width))
    out = jnp.concatenate(rows, axis=0)
    return jnp.pad(out, ((0, (-out.shape[0]) % mult), (0, 0)))


def _unpack(packed, shapes, width=D):
    outs, r = [], 0
    for shp in shapes:
        size = int(np.prod(shp))
        nr = -(-size // width)
        outs.append(packed[r:r + nr].reshape(-1)[:size].reshape(shp))
        r += nr
    return outs


def kernel(x, c, ctx, c_ctx, w_mod, b_mod, g_pre, g_post, w_in, w_out, ret_norm_g, sg_w, sg_b, sc_conv_w, gdn_conv_w, gdn_a_log, gdn_dt_bias, gdn_norm_g, loss_target, m_c_ctx, m_w_mod, m_b_mod, m_g_pre, m_g_post, m_w_in, m_w_out, m_ret_norm_g, m_sg_w, m_sg_b, m_sc_conv_w, m_gdn_conv_w, m_gdn_a_log, m_gdn_dt_bias, m_gdn_norm_g, v_c_ctx, v_w_mod, v_b_mod, v_g_pre, v_g_post, v_w_in, v_w_out, v_ret_norm_g, v_sg_w, v_sg_b, v_sc_conv_w, v_gdn_conv_w, v_gdn_a_log, v_gdn_dt_bias, v_gdn_norm_g):
    weights = dict(c_ctx=c_ctx, w_mod=w_mod, b_mod=b_mod, g_pre=g_pre, g_post=g_post, w_in=w_in, w_out=w_out,
                   ret_norm_g=ret_norm_g, sg_w=sg_w, sg_b=sg_b, sc_conv_w=sc_conv_w, gdn_conv_w=gdn_conv_w,
                   gdn_a_log=gdn_a_log, gdn_dt_bias=gdn_dt_bias, gdn_norm_g=gdn_norm_g)
    mom = dict(c_ctx=m_c_ctx, w_mod=m_w_mod, b_mod=m_b_mod, g_pre=m_g_pre, g_post=m_g_post, w_in=m_w_in,
               w_out=m_w_out, ret_norm_g=m_ret_norm_g, sg_w=m_sg_w, sg_b=m_sg_b, sc_conv_w=m_sc_conv_w,
               gdn_conv_w=m_gdn_conv_w, gdn_a_log=m_gdn_a_log, gdn_dt_bias=m_gdn_dt_bias, gdn_norm_g=m_gdn_norm_g)
    var = dict(c_ctx=v_c_ctx, w_mod=v_w_mod, b_mod=v_b_mod, g_pre=v_g_pre, g_post=v_g_post, w_in=v_w_in,
               w_out=v_w_out, ret_norm_g=v_ret_norm_g, sg_w=v_sg_w, sg_b=v_sg_b, sc_conv_w=v_sc_conv_w,
               gdn_conv_w=v_gdn_conv_w, gdn_a_log=v_gdn_a_log, gdn_dt_bias=v_gdn_dt_bias, gdn_norm_g=v_gdn_norm_g)

    nb, t_lat, _ = x.shape
    t_ctx = ctx.shape[1]
    s = t_ctx + t_lat
    n = nb * s
    sb = s // TM
    nl = w_in.shape[0]
    wc_in = w_in.shape[2]
    wc_mod = w_mod.shape[2]
    rows_out = w_out.shape[1]
    n_all = nb * N_DEV
    mx, my, mc = _my_pos()
    chip = 2 * mx + my
    dev = 2 * chip + mc

    sg_c = _sg_consts()
    bd = jnp.asarray(_block_diag())
    ret_c = _ret_consts(nb)
    gdn_c = _gdn_consts(nb)
    cos, sins = _rope_tables(t_lat, t_ctx)

    w_in_b, w_out_b = w_in.astype(bf16), w_out.astype(bf16)
    pre = _pack([c, sc_conv_w, gdn_conv_w], mult=8)
    (pre_all,), w0_parts = exchange("startup_gather", [("gather", "devices", [pre]),
                                                       ("gather", "chips", [w_in_b[0], w_out_b[0]])])
    c_parts, scw_parts, gcw_parts = [], [], []
    for k in range(N_DEV):
        ck, sk, gk = _unpack(pre_all[k], [c.shape, sc_conv_w.shape, gdn_conv_w.shape])
        c_parts.append(ck)
        if k % 2 == 0:
            scw_parts.append(sk)
            gcw_parts.append(gk)
    c_all = jnp.concatenate(c_parts, axis=0)
    sc_w_full = jnp.concatenate(scw_parts, axis=-1)
    gdn_w_full = jnp.concatenate(gcw_parts, axis=-1)
    c_rows = jnp.concatenate([c_all, c_ctx[None, :], jnp.zeros((7, D), f32)], axis=0)

    b_cols = lax.dynamic_slice_in_dim(b_mod, chip * wc_mod, wc_mod, axis=1)[:, None, :]
    mod_part = mod_fwd(c_rows, w_mod, b_cols)
    mod_all = gather8(mod_part)
    mod = jnp.concatenate([mod_all[2 * k] for k in range(N_CHIPS)], axis=-1)
    my_rows = jnp.concatenate([lax.dynamic_slice_in_dim(mod, dev * nb, nb, axis=1), mod[:, n_all:n_all + 1]], axis=1)
    shift_t = my_rows[:, :, None, 0:D]
    scale_t = my_rows[:, :, None, D:2 * D]
    gate_t = my_rows[:, :, None, 2 * D:3 * D]

    def full_weights(parts):
        wi = jnp.concatenate([parts[0][k] for k in range(N_CHIPS)], axis=-1)
        wo = jnp.concatenate([parts[1][k] for k in range(N_CHIPS)], axis=0)
        return jnp.pad(wi, ((0, 0), (0, PW - IN_W))), wo

    w_in_full, w_out_full = [None] * nl, [None] * nl
    w_in_full[0], w_out_full[0] = full_weights(w0_parts)

    alog = jnp.pad(gdn_a_log.reshape(nl, 1, 8), ((0, 0), (0, 0), (0, 120)))
    dtb = jnp.pad(gdn_dt_bias.reshape(nl, 1, 8), ((0, 0), (0, 0), (0, 120)))
    gdn_ng = jnp.tile(gdn_norm_g, (1, NH))[:, None, :]
    ret_ng = ret_norm_g[:, None, :]

    xs = jnp.concatenate([ctx, x], axis=1).reshape(n, D)
    saved = []
    for l in range(nl):
        p, h = inproj_fwd(xs, shift_t[l], scale_t[l], g_pre[l][None, :], w_in_full[l], nb, sb)
        p3 = p.reshape(nb, s, PW)
        ro_f, ro_b, rs_all = ret_scan_fwd(p3, cos, sins, ret_c, t_ctx)
        y_ret = mix_finish_fwd(_ret_finish, "ret_finish_fwd", ro_f, ro_b, p3, 3, ret_ng[l], bd)
        y_sg = sg_fwd(p3, sg_w[l], sg_b[l], *sg_c)
        y_sc = sc_fwd(p3, sc_w_full[l], t_ctx)
        cq, ck, cv = [gdn_conv_fwd(p3, gdn_w_full[l][:, BW * i:BW * (i + 1)], 11 + i, t_ctx) for i in range(3)]
        nxt = None if l + 1 == nl else ("gather", [w_in_b[l + 1], w_out_b[l + 1]])
        go_f, go_b, *gs_all = gdn_scan_fwd(cq, ck, cv, p3, alog[l], dtb[l], gdn_c, t_ctx, nxt)
        if nxt is not None:
            w_in_full[l + 1], w_out_full[l + 1] = full_weights(gs_all[2:])
            gs_all = gs_all[:2]
        y_gdn = mix_finish_fwd(_gdn_finish, "gdn_finish_fwd", go_f, go_b, p3, 14, gdn_ng[l], bd)
        ys = [a.reshape(n, BW) for a in (y_ret, y_sg, y_sc, y_gdn)]
        x_new, o = outproj_fwd(ys, w_out_full[l], xs, gate_t[l], g_post[l][None, :], nb, sb)
        saved.append(dict(x=xs, h=h, p3=p3, ro=(ro_f, ro_b), rs=rs_all, c=(cq, ck, cv), go=(go_f, go_b), gs=gs_all,
                          ys=ys, o=o))
        xs = x_new

    dx3, loss_part = loss_head(xs.reshape(nb, s, D), loss_target, t_ctx)
    loss = lax.psum(loss_part[0, 0], ("x", "y", "c"))

    dxs = dx3.reshape(n, D)
    g_small = {k: [None] * nl for k in SMALL if k not in ("c_ctx", "b_mod")}
    dm_rows = [None] * nl
    slabs = None
    got_in, got_out = [None] * nl, [None] * nl
    for l in reversed(range(nl)):
        sv = saved[l]
        p3 = sv["p3"]
        dy, gw_out, dg_post, dgate = outproj_bwd(dxs, sv["o"], gate_t[l], g_post[l][None, :], sv["ys"], w_out_full[l], nb, sb)
        dy3 = dy.reshape(nb, s, D)
        r_do, r_dz, d_rng = mix_finish_bwd(_ret_finish, "ret_finish_bwd", *sv["ro"], p3, 3, ret_ng[l], bd, dy3, 0)
        r_d = ret_scan_bwd(p3, cos, sins, ret_c, sv["rs"], r_do, t_ctx)
        s_du, s_dv, s_dz, d_sgw, d_sgb = sg_bwd(p3, sg_w[l], sg_b[l], *sg_c, dy3)
        c_db, c_dc, c_dh, c_dz, d_scw = sc_bwd(p3, sc_w_full[l], dy3, t_ctx)
        g_do, g_dz, d_gng = mix_finish_bwd(_gdn_finish, "gdn_finish_bwd", *sv["go"], p3, 14, gdn_ng[l], bd, dy3, 3)
        g_d = gdn_scan_bwd(*sv["c"], p3, alog[l], dtb[l], gdn_c, *sv["gs"], g_do, t_ctx,
                           None if slabs is None else ("scatter", slabs))
        if slabs is not None:
            got_in[l + 1], got_out[l + 1] = g_d[10:]
        gx, d_gcw = [], []
        for i in range(3):
            dxi, dwi = gdn_conv_bwd(p3, gdn_w_full[l][:, BW * i:BW * (i + 1)], 11 + i, g_d[i], g_d[4 + i], t_ctx)
            gx.append(dxi)
            d_gcw.append(dwi)
        dp3 = assemble_dp([(r_d[0], r_d[3]), (r_d[1], r_d[4]), (r_d[2], r_d[5])],
                          [r_dz, s_du, s_dv, s_dz, c_db, c_dc, c_dh, c_dz], gx, [g_dz], [g_d[3], g_d[7]])
        dp = dp3.reshape(n, PW)
        dxs, dg_pre, dshift, dscale = inproj_bwd_x(dp, w_in_full[l], sv["x"], scale_t[l], g_pre[l][None, :], dxs, nb, sb)
        gw_in = dw_in(sv["h"], dp)
        slabs = [jnp.stack([gw_in[:, k * wc_in:(k + 1) * wc_in] for k in range(N_CHIPS)]),
                 gw_out.reshape(N_CHIPS, rows_out, D).astype(bf16)]
        g_small["g_pre"][l] = dg_pre[0]
        g_small["g_post"][l] = dg_post[0]
        g_small["ret_norm_g"][l] = d_rng[0]
        g_small["sg_w"][l] = d_sgw
        g_small["sg_b"][l] = d_sgb
        g_small["sc_conv_w"][l] = d_scw
        g_small["gdn_conv_w"][l] = jnp.concatenate(d_gcw, axis=-1)
        g_small["gdn_a_log"][l] = g_d[8][0, :8].reshape(2, NH)
        g_small["gdn_dt_bias"][l] = g_d[9][0, :8].reshape(2, NH)
        g_small["gdn_norm_g"][l] = d_gng[0].reshape(NH, HD)
        dm_rows[l] = jnp.concatenate([dshift, dscale, dgate], axis=-1)[:nb + 1]
    grad_x = dxs.reshape(nb, s, D)[:, t_ctx:, :]

    g_small = {k: jnp.stack(v) for k, v in g_small.items()}
    dm_rows = jnp.stack(dm_rows)
    names2 = [k for k in SMALL if k not in ("c_ctx", "b_mod")]
    pack_sum = _pack([g_small[k] for k in names2] + [dm_rows[:, nb:]])
    pack_own = _pack([dm_rows[:, :nb]], mult=8)
    rs = -(-pack_sum.shape[0] // (8 * N_DEV)) * 8
    slabs_sum = jnp.pad(pack_sum, ((0, N_DEV * rs - pack_sum.shape[0]), (0, 0))).reshape(N_DEV, rs, D)
    (got_small,), (got_in[0], got_out[0]) = exchange("tail_scatter", [("scatter", "devices", [slabs_sum]),
                                                                      ("scatter", "chips", slabs)])
    my_slab = sum_lead(got_small, tr=rs)
    gin_mine = jnp.stack([sum_lead(a) for a in got_in])
    gout_mine = jnp.stack([sum_lead(a) for a in got_out])
    (all2,), (gin_sib, gout_sib) = exchange("tail_gather", [
        ("gather", "devices", [jnp.concatenate([my_slab, pack_own], axis=0)]), ("send", "cores", [gin_mine, gout_mine])])
    tot2 = all2[:, :rs].reshape(N_DEV * rs, D)
    outs2 = _unpack(tot2, [g_small[k].shape for k in names2] + [(nl, 1, 3 * D)])
    grads = dict(zip(names2, outs2[:-1]))
    dm_own = jnp.stack([_unpack(all2[k, rs:], [(nl, nb, 3 * D)])[0] for k in range(N_DEV)])
    dm_own = jnp.transpose(dm_own, (1, 0, 2, 3)).reshape(nl, n_all, 3 * D)
    dm_all = jnp.concatenate([dm_own, jnp.pad(outs2[-1], ((0, 0), (0, 7), (0, 0)))], axis=1)
    grads["gdn_norm_g"] = sum_lead(jnp.transpose(grads["gdn_norm_g"], (1, 0, 2)), tr=nl)
    for k in ("sc_conv_w", "gdn_conv_w"):
        wc = weights[k].shape[2]
        grads[k] = lax.dynamic_slice_in_dim(grads[k], chip * wc, wc, axis=2)

    dm_cols = lax.dynamic_slice_in_dim(dm_all, chip * wc_mod, wc_mod, axis=2)
    g_w_mod, g_b_mod, dcc_part = mod_bwd(c_rows, w_mod, dm_cols, dm_all)
    grads["b_mod"] = g_b_mod[:, 0, :]
    grads["c_ctx"] = cctx_grad(gather8(dcc_part), c_ctx[None, :])[0]

    res = {}
    res["w_in"] = adamw(w_in, m_w_in, v_w_in, gin_mine, gin_sib)
    res["w_out"] = adamw(w_out, m_w_out, v_w_out, gout_mine, gout_sib)
    res["w_mod"] = adamw(w_mod, m_w_mod, v_w_mod, g_w_mod)
    shapes = [weights[k].shape for k in SMALL]
    small = adamw(_pack([weights[k] for k in SMALL]), _pack([mom[k] for k in SMALL]), _pack([var[k] for k in SMALL]),
                  _pack([grads[k].reshape(weights[k].shape) for k in SMALL]), tr=PACK_ROWS)
    small = [_unpack(a, shapes) for a in small]
    for i, k in enumerate(SMALL):
        res[k] = [small[j][i] for j in range(4)]

    order = ["c_ctx", "w_mod", "b_mod", "g_pre", "g_post", "w_in", "w_out", "ret_norm_g", "sg_w", "sg_b", "sc_conv_w",
             "gdn_conv_w", "gdn_a_log", "gdn_dt_bias", "gdn_norm_g"]
    return (loss, grad_x, *[res[k][0] for k in order], *[res[k][1] for k in order], *[res[k][2] for k in order],
            *[res[k][3] for k in order])
```

```python
import functools

import jax
import jax.numpy as jnp
import numpy as np
from jax import lax
from jax.experimental import pallas as pl
from jax.experimental.pallas import tpu as pltpu

f32 = jnp.float32
bf16 = jnp.bfloat16
HI = lax.Precision.HIGHEST
P3 = lax.Precision.HIGH
MESH = pl.DeviceIdType.MESH

EPS = 1e-6
D = 1024
NH = 4
HD = 64
BW = NH * HD
PAIR_W = 2 * HD
RC = 128
GC = 64
GRID_W = 64
ROPE_BASE = 10000.0
IN_W = 15 * BW + 16
PW = 4096
GATE_COL = 15 * BW
N_CHIPS = 4
N_DEV = 8
TM = 256
TP = 2 * TM
ADAM_LR, ADAM_B1, ADAM_B2, ADAM_EPS, ADAM_WD, ADAM_STEP = 0.001, 0.9, 0.999, 1e-08, 0.01, 10
LANE_HEAD = np.arange(BW) // HD
VMEM_BIG = 56 * 1024 * 1024


def _dot(a, b, precision=None):
    return jnp.dot(a, b, precision=precision, preferred_element_type=f32)


def _dot_nt(a, b, precision=None):
    return lax.dot_general(a, b, (((1,), (1,)), ((), ())), precision=precision, preferred_element_type=f32)


def _dot_tn(a, b, precision=None):
    return lax.dot_general(a, b, (((0,), (0,)), ((), ())), precision=precision, preferred_element_type=f32)


def _sds(shape, dtype=f32):
    return jax.ShapeDtypeStruct(shape, dtype)


def _cparams(sem=None, vmem=None):
    kw = {}
    if sem is not None:
        kw["dimension_semantics"] = sem
    if vmem is not None:
        kw["vmem_limit_bytes"] = vmem
    return pltpu.CompilerParams(**kw)


def _full(shape):
    n = len(shape)
    return pl.BlockSpec(shape, lambda *_: (0,) * n)


def _block_diag():
    return (LANE_HEAD[:, None] == LANE_HEAD[None, :]).astype(np.float32)


def _tau(c, d):
    return np.arange(c) if d == 0 else c - 1 - np.arange(c)


def _ret_consts(nb):
    lg = np.log(1.0 - 2.0 ** (-5.0 - np.arange(NH)))
    intra = np.zeros((2, 2, RC, 2 * RC)); qdec = np.zeros((2, 2, RC, PAIR_W)); kdec = np.zeros((2, 2, RC, PAIR_W))
    cd = np.zeros((2, 2, PAIR_W, PAIR_W))
    for d in range(2):
        t = _tau(RC, d)
        diff = t[:, None] - t[None, :]
        for p in range(2):
            lane_lg = lg[2 * p + np.arange(PAIR_W) // HD]
            for h in range(2):
                intra[d, p, :, h * RC:(h + 1) * RC] = np.where(diff >= 0, np.exp(np.maximum(diff, 0) * lg[2 * p + h]), 0.0)
            qdec[d, p] = np.exp((t[:, None] + 1.0) * lane_lg[None, :])
            kdec[d, p] = np.exp((RC - 1.0 - t[:, None]) * lane_lg[None, :])
            cd[d, p] = np.exp(RC * lane_lg)[:, None] * np.ones((1, PAIR_W))
    per_z = [np.tile(a.reshape((4,) + a.shape[2:]), (nb, 1, 1)) for a in (intra, qdec, kdec, cd)]
    bd2 = (np.arange(PAIR_W)[:, None] // HD == np.arange(PAIR_W)[None, :] // HD)
    bdr = (np.arange(2 * RC)[:, None] // RC == np.arange(PAIR_W)[None, :] // HD)
    return [jnp.asarray(a, f32) for a in per_z + [bd2, bdr]]


def _rope_tables(t_lat, t_ctx):
    nf = HD // 4
    inv = ROPE_BASE ** (-np.arange(nf) / nf)
    pos = np.arange(t_lat)
    ang_r = (pos // GRID_W)[:, None] * inv[None, :]
    ang_c = (pos % GRID_W)[:, None] * inv[None, :]
    ang = np.concatenate([ang_r, ang_r, ang_c, ang_c], axis=1)
    sign = np.concatenate([-np.ones(nf), np.ones(nf), -np.ones(nf), np.ones(nf)])
    cos = np.tile(np.cos(ang), (1, 2)); sins = np.tile(np.sin(ang) * sign, (1, 2))
    cos = np.concatenate([np.ones((t_ctx, PAIR_W)), cos]); sins = np.concatenate([np.zeros((t_ctx, PAIR_W)), sins])
    return jnp.asarray(cos, f32), jnp.asarray(sins, f32)


def _gdn_consts(nb):
    tmask = np.zeros((2, 2, GC, GC)); tmask2 = np.zeros((2, 2, GC, PAIR_W)); strict2 = np.zeros((2, 2, GC, PAIR_W))
    exp_g = np.zeros((2, 2, 128, PAIR_W)); exp_b = np.zeros((2, 2, 128, PAIR_W))
    for d in range(2):
        t = _tau(GC, d)
        tmask[d, :] = (t[:, None] >= t[None, :])
        tmask2[d, :] = np.tile(t[:, None] >= t[None, :], (1, 2))
        strict2[d, :] = np.tile(t[:, None] > t[None, :], (1, 2))
        for h in range(NH):
            exp_g[d, h // 2, 4 * d + h, (h % 2) * HD:(h % 2 + 1) * HD] = 1.0
            exp_b[d, h // 2, 8 + 4 * d + h, (h % 2) * HD:(h % 2 + 1) * HD] = 1.0
    exp_gt = np.transpose(exp_g, (0, 1, 3, 2))
    per_z = [np.tile(a.reshape((4,) + a.shape[2:]), (nb, 1, 1)) for a in (tmask, tmask2, strict2, exp_g, exp_b, exp_gt)]
    dsel2 = np.tile(np.eye(GC), (1, 2))
    eye2 = np.tile(np.eye(GC), (1, 2))
    bd2 = (np.arange(PAIR_W)[:, None] // HD == np.arange(PAIR_W)[None, :] // HD)
    return [jnp.asarray(a, f32) for a in per_z + [dsel2, eye2, bd2]]


def _swap16(x):
    lane = lax.broadcasted_iota(jnp.int32, x.shape, x.ndim - 1)
    n = x.shape[-1]
    return jnp.where(lane % 32 < 16, pltpu.roll(x, n - 16, axis=x.ndim - 1), pltpu.roll(x, 16, axis=x.ndim - 1))


@jax.custom_vjp
def _rot(x, cos, sins):
    return x * cos + _swap16(x) * sins


def _rot_fwd(x, cos, sins):
    return _rot(x, cos, sins), (cos, sins)


def _rot_bwd(res, g):
    cos, sins = res
    return g * cos + _swap16(g * sins), jnp.zeros_like(cos), jnp.zeros_like(sins)


_rot.defvjp(_rot_fwd, _rot_bwd)


def _silu(z):
    return z * jax.nn.sigmoid(z)


def _head_sum(x, bd):
    return _dot(x, bd, precision=P3)


def _ret_step(s, q, k, v, cos, sins, intra, qdec, kdec, cd, bd2, bdr):
    def bdiag(x):
        return jnp.concatenate([x, x], axis=1) * bdr

    qr = _rot(q, cos, sins)
    kr = _rot(k, cos, sins) * (HD ** -0.5)
    sc = _bmm_nt(qr, bdiag(kr)) * intra
    o = _bmm(qr * qdec, s) + _bmm(sc, bdiag(v))
    s_new = s * cd + bd2 * _bmm_tn(kr * kdec, v)
    return s_new, o


def _ret_finish(o_f, o_b, z, norm_g, bd):
    o = o_f + o_b
    mu = _head_sum(o, bd) * (1.0 / HD)
    xc = o - mu
    var = _head_sum(xc * xc, bd) * (1.0 / HD)
    return xc * lax.rsqrt(var + EPS) * norm_g * _silu(z)


def _softplus(x):
    return jnp.maximum(x, 0.0) + jnp.log(1.0 + jnp.exp(-jnp.abs(x)))


def _bmm(a, b, precision=None):
    return lax.dot_general(a, b, (((2,), (1,)), ((0,), (0,))), precision=precision, preferred_element_type=f32)


def _bmm_nt(a, b, precision=None):
    return lax.dot_general(a, b, (((2,), (2,)), ((0,), (0,))), precision=precision, preferred_element_type=f32)


def _bmm_tn(a, b, precision=None):
    return lax.dot_general(a, b, (((1,), (1,)), ((0,), (0,))), precision=precision, preferred_element_type=f32)


def _bdiag(x, bd2):
    return jnp.concatenate([x, x], axis=1) * bd2


@jax.custom_vjp
def _solve_given_inv(m, vb, kbg, inv, bd2):
    return _bmm(inv, _bdiag(vb, bd2), P3), _bmm(inv, _bdiag(kbg, bd2), P3)


def _solve_fwd(m, vb, kbg, inv, bd2):
    u, w = _solve_given_inv(m, vb, kbg, inv, bd2)
    return (u, w), (inv, u, w, bd2)


def _solve_bwd(res, cts):
    inv, u, w, bd2 = res
    du, dw = cts
    c = inv.shape[1]
    t = jnp.swapaxes(_bdiag(inv, bd2), 1, 2)
    inv_t = t[:, :c] + t[:, c:]
    dvb = _bmm(inv_t, _bdiag(du, bd2), P3)
    dkbg = _bmm(inv_t, _bdiag(dw, bd2), P3)
    dm = _bmm_nt(dvb, _bdiag(u, bd2), P3) + _bmm_nt(dkbg, _bdiag(w, bd2), P3)
    return dm, dvb, dkbg, jnp.zeros_like(inv), jnp.zeros_like(bd2)


_solve_given_inv.defvjp(_solve_fwd, _solve_bwd)


def _gdn_step(s, q, k, v, gate, alog, dtb, tmask, tmask2, strict2, exp_g, exp_b, exp_gt, dsel2, eye2, bd2, inv=None):
    z, c, w_ = q.shape
    ne = gate.shape[0]

    def per_pair(a):
        return jnp.broadcast_to(a[:, None], (ne, z // ne) + a.shape[1:]).reshape((z,) + a.shape[1:])

    def rows(a):
        return a.reshape(z * c, w_)

    def bdiag(x):
        return _bdiag(x, bd2)

    g = per_pair(-jnp.exp(alog) * _softplus(gate + dtb))
    beta = per_pair(jax.nn.sigmoid(gate))
    gl = _bmm(g, exp_g, P3)
    gc_l = _bmm(tmask, gl, P3)
    glast_l = jnp.sum(gl, axis=1, keepdims=True)
    glast = jnp.sum(g, axis=1, keepdims=True)
    beta_l = _bmm(beta, exp_b, P3)
    gc_r = jnp.sum(gc_l * dsel2, axis=1, keepdims=True)
    qn = q * lax.rsqrt(_dot(rows(q * q), bd2, P3).reshape(z, c, w_) + EPS)
    kn = k * lax.rsqrt(_dot(rows(k * k), bd2, P3).reshape(z, c, w_) + EPS)
    eg = jnp.exp(gc_l)
    kb = kn * beta_l
    vb = v * beta_l
    kbg = kb * eg
    qs = qn * (HD ** -0.5)
    dec = jnp.exp(jnp.where(tmask2 > 0, gc_l - gc_r, -1e30))
    kns = bdiag(kn)
    m = -(_bmm_nt(kb, kns) * dec * strict2)
    if inv is None:
        inv = eye2 + m
        p = m
        for _ in range(5):
            p = _bmm(p, bdiag(p), P3)
            inv = inv + _bmm(inv, bdiag(p), P3)
        u = _bmm(inv, bdiag(vb), P3)
        w = _bmm(inv, bdiag(kbg), P3)
    else:
        u, w = _solve_given_inv(m, vb, kbg, inv, bd2)
    v_new = u - _bmm(w, s)
    k_tail = kn * jnp.exp(glast_l - gc_l)
    cdec = jnp.sum(exp_gt * jnp.exp(glast), axis=-1, keepdims=True)
    s_new = s * cdec + bd2 * _bmm_tn(k_tail, v_new)
    a = _bmm_nt(qs, kns) * dec
    o = _bmm(qs * eg, s) + _bmm(a, bdiag(v_new))
    return s_new, o, inv


def _gdn_finish(o_f, o_b, z, norm_g, bd):
    o = o_f + o_b
    ms = _head_sum(o * o, bd) * (1.0 / HD)
    return o * lax.rsqrt(ms + EPS) * norm_g * _silu(z)


def _gelu(x):
    return 0.5 * x * (1.0 + jnp.tanh(0.7978845608028654 * (x + 0.044715 * (x * x * x))))


def _sg_block(u0, u1, v0, v1, z0, z1, w, b, hmp, bdr):
    ts = u0.shape[0]
    nc = ts // RC
    g0, g1 = _gelu(v0), _gelu(v1)
    mu = (jnp.sum(g0, axis=-1, keepdims=True) + jnp.sum(g1, axis=-1, keepdims=True)) * (1.0 / BW)
    x0, x1 = g0 - mu, g1 - mu
    var = (jnp.sum(x0 * x0, axis=-1, keepdims=True) + jnp.sum(x1 * x1, axis=-1, keepdims=True)) * (1.0 / BW)
    rstd = lax.rsqrt(var + EPS)
    ys = []
    for p, (u, xc, z) in enumerate(((u0, x0, z0), (u1, x1, z1))):
        vn = (xc * rstd).reshape(nc, RC, PAIR_W)
        wp = jnp.concatenate([w[2 * p], w[2 * p + 1]], axis=1)
        mix = _bmm(jnp.broadcast_to(wp, (nc, RC, 2 * RC)), jnp.concatenate([vn, vn], axis=1) * bdr)
        bias = _dot_tn(b, hmp[p], precision=HI)
        s = (mix + bias).reshape(ts, PAIR_W)
        ys.append(_gelu(u) * s * _silu(z))
    return ys[0], ys[1]


def _make_shifts(t_ctx, n):
    def dn(x):
        t = lax.broadcasted_iota(jnp.int32, x.shape, 0)
        return jnp.where((t != 0) & (t != t_ctx), pltpu.roll(x, 1, axis=0), 0.0)

    def up(x):
        t = lax.broadcasted_iota(jnp.int32, x.shape, 0)
        return jnp.where((t != t_ctx - 1) & (t != n - 1), pltpu.roll(x, n - 1, axis=0), 0.0)

    @jax.custom_vjp
    def shift_dn(x):
        return dn(x)
    shift_dn.defvjp(lambda x: (dn(x), None), lambda _, g: (up(g),))

    @jax.custom_vjp
    def shift_up(x):
        return up(x)
    shift_up.defvjp(lambda x: (up(x), None), lambda _, g: (dn(g),))
    return shift_dn, shift_up


def _conv3(x, w, shift_dn, shift_up):
    return shift_dn(x) * w[0:1] + x * w[1:2] + shift_up(x) * w[2:3]


def inproj_fwd(x, shift_t, scale_t, g_pre, w_in, n_batch, sb):
    n = x.shape[0]

    def sel(i):
        return jnp.where(i % sb == 0, n_batch, i // sb)

    def body(x_ref, sh0, sh1, sc0, sc1, g_ref, w_ref, p_ref, h_ref):
        hs = []
        for k, (sh_ref, sc_ref) in enumerate(((sh0, sc0), (sh1, sc1))):
            xv = x_ref[k * TM:(k + 1) * TM, :]
            r = xv * lax.rsqrt(jnp.mean(xv * xv, axis=-1, keepdims=True) + EPS)
            hs.append(((r * g_ref[...]) * (1.0 + sc_ref[0]) + sh_ref[0]).astype(bf16))
        hb = jnp.concatenate(hs, axis=0)
        h_ref[...] = hb
        p_ref[...] = _dot(hb, w_ref[...])

    def mrow(k):
        return pl.BlockSpec((1, 1, D), lambda i: (sel(2 * i + k), 0, 0))

    return pl.pallas_call(
        body, name="inproj_fwd", grid=(n // TP,),
        in_specs=[pl.BlockSpec((TP, D), lambda i: (i, 0)), mrow(0), mrow(1), mrow(0), mrow(1),
                  _full((1, D)), _full((D, PW))],
        out_specs=[pl.BlockSpec((TP, PW), lambda i: (i, 0)), pl.BlockSpec((TP, D), lambda i: (i, 0))],
        out_shape=[_sds((n, PW)), _sds((n, D), bf16)],
        compiler_params=_cparams(("arbitrary",), VMEM_BIG),
    )(x, shift_t, shift_t, scale_t, scale_t, g_pre, w_in)


def outproj_fwd(ys, w_out, x, gate_t, g_post, n_batch, sb):
    n = x.shape[0]

    def sel(i):
        return jnp.where(i % sb == 0, n_batch, i // sb)

    def body(y0, y1, y2, y3, w_ref, x_ref, gt0, gt1, g_ref, xn_ref, o_ref):
        y = jnp.concatenate([y0[...], y1[...], y2[...], y3[...]], axis=1)
        o = _dot(y, w_ref[...])
        o_ref[...] = o
        nrm = o * lax.rsqrt(jnp.mean(o * o, axis=-1, keepdims=True) + EPS) * g_ref[...]
        for k, gt_ref in enumerate((gt0, gt1)):
            rows = slice(k * TM, (k + 1) * TM)
            xn_ref[rows, :] = x_ref[rows, :] + gt_ref[0] * nrm[rows]

    def mrow(k):
        return pl.BlockSpec((1, 1, D), lambda i: (sel(2 * i + k), 0, 0))

    yspec = pl.BlockSpec((TP, BW), lambda i: (i, 0))
    return pl.pallas_call(
        body, name="outproj_fwd", grid=(n // TP,),
        in_specs=[yspec, yspec, yspec, yspec, _full((D, D)), pl.BlockSpec((TP, D), lambda i: (i, 0)),
                  mrow(0), mrow(1), _full((1, D))],
        out_specs=[pl.BlockSpec((TP, D), lambda i: (i, 0)), pl.BlockSpec((TP, D), lambda i: (i, 0))],
        out_shape=[_sds((n, D)), _sds((n, D))],
        compiler_params=_cparams(("arbitrary",), VMEM_BIG),
    )(*ys, w_out, x, gate_t, gate_t, g_post)


def _row_onehot(r):
    return lax.broadcasted_iota(jnp.int32, (8, 1), 0) == r


def outproj_bwd(dxn, o, gate_t, g_post, ys, w_out, n_batch, sb):
    n = dxn.shape[0]

    def sel(i):
        return jnp.where(i % sb == 0, n_batch, i // sb)

    def body(dxn_ref, o_ref, gt0, gt1, g_ref, y0, y1, y2, y3, w_ref, dy_ref, dw_ref, dg_ref, dgate_ref):
        i = pl.program_id(0)

        @pl.when(i == 0)
        def _():
            dw_ref[...] = jnp.zeros_like(dw_ref)
            dg_ref[...] = jnp.zeros_like(dg_ref)
            dgate_ref[...] = jnp.zeros_like(dgate_ref)

        g = g_ref[...]
        dos = []
        for k, gt_ref in enumerate((gt0, gt1)):
            rows = slice(k * TM, (k + 1) * TM)
            ov = o_ref[rows, :]
            rstd = lax.rsqrt(jnp.mean(ov * ov, axis=-1, keepdims=True) + EPS)
            r = ov * rstd
            dx = dxn_ref[rows, :]
            dgate_ref[...] += jnp.where(_row_onehot(sel(2 * i + k)), jnp.sum(dx * (r * g), axis=0, keepdims=True), 0.0)
            dn = dx * gt_ref[0]
            dg_ref[...] += jnp.sum(dn * r, axis=0, keepdims=True)
            dr = dn * g
            dos.append((rstd * (dr - r * jnp.mean(dr * r, axis=-1, keepdims=True))).astype(bf16))
        dob = jnp.concatenate(dos, axis=0)
        dy_ref[...] = _dot_nt(dob, w_ref[...])
        y = jnp.concatenate([y0[...], y1[...], y2[...], y3[...]], axis=1)
        dw_ref[...] += _dot_tn(y, dob)

    def mrow(k):
        return pl.BlockSpec((1, 1, D), lambda i: (sel(2 * i + k), 0, 0))

    yspec = pl.BlockSpec((TP, BW), lambda i: (i, 0))
    row = pl.BlockSpec((TP, D), lambda i: (i, 0))
    return pl.pallas_call(
        body, name="outproj_bwd", grid=(n // TP,),
        in_specs=[row, row, mrow(0), mrow(1), _full((1, D)), yspec, yspec, yspec, yspec, _full((D, D))],
        out_specs=[row, _full((D, D)), _full((1, D)), _full((8, D))],
        out_shape=[_sds((n, D)), _sds((D, D)), _sds((1, D)), _sds((8, D))],
        compiler_params=_cparams(("arbitrary",), VMEM_BIG),
    )(dxn, o, gate_t, gate_t, g_post, *ys, w_out)


def inproj_bwd_x(dp, w_in, x, scale_t, g_pre, dxn, n_batch, sb):
    n = x.shape[0]

    def sel(i):
        return jnp.where(i % sb == 0, n_batch, i // sb)

    def body(dp_ref, w_ref, x_ref, sc0, sc1, g_ref, dxn_ref, dx_ref, dg_ref, dsh_ref, dsc_ref):
        i = pl.program_id(0)

        @pl.when(i == 0)
        def _():
            dg_ref[...] = jnp.zeros_like(dg_ref)
            dsh_ref[...] = jnp.zeros_like(dsh_ref)
            dsc_ref[...] = jnp.zeros_like(dsc_ref)

        dh_all = _dot_nt(dp_ref[...], w_ref[...])
        g = g_ref[...]
        for k, sc_ref in enumerate((sc0, sc1)):
            rows = slice(k * TM, (k + 1) * TM)
            dh = dh_all[rows]
            xv = x_ref[rows, :]
            rstd = lax.rsqrt(jnp.mean(xv * xv, axis=-1, keepdims=True) + EPS)
            r = xv * rstd
            hot = _row_onehot(sel(2 * i + k))
            dsh_ref[...] += jnp.where(hot, jnp.sum(dh, axis=0, keepdims=True), 0.0)
            dsc_ref[...] += jnp.where(hot, jnp.sum(dh * (r * g), axis=0, keepdims=True), 0.0)
            t = dh * (1.0 + sc_ref[0])
            dg_ref[...] += jnp.sum(t * r, axis=0, keepdims=True)
            dr = t * g
            dx_ref[rows, :] = dxn_ref[rows, :] + rstd * (dr - r * jnp.mean(dr * r, axis=-1, keepdims=True))

    def mrow(k):
        return pl.BlockSpec((1, 1, D), lambda i: (sel(2 * i + k), 0, 0))

    row = pl.BlockSpec((TP, D), lambda i: (i, 0))
    return pl.pallas_call(
        body, name="inproj_bwd_x", grid=(n // TP,),
        in_specs=[pl.BlockSpec((TP, PW), lambda i: (i, 0)), _full((D, PW)), row, mrow(0), mrow(1), _full((1, D)), row],
        out_specs=[row, _full((1, D)), _full((8, D)), _full((8, D))],
        out_shape=[_sds((n, D)), _sds((1, D)), _sds((8, D)), _sds((8, D))],
        compiler_params=_cparams(("arbitrary",), VMEM_BIG),
    )(dp, w_in, x, scale_t, scale_t, g_pre, dxn)


def dw_in(h, dp):
    n = h.shape[0]
    tk, tn = (1536 if n % 1536 == 0 else 512), 1024
    nk = n // tk

    def body(h_ref, dp_ref, o_ref, acc):
        k = pl.program_id(1)

        @pl.when(k == 0)
        def _():
            acc[...] = jnp.zeros_like(acc)
        acc[...] += _dot_tn(dp_ref[...], h_ref[...])

        @pl.when(k == nk - 1)
        def _():
            o_ref[...] = acc[...].astype(bf16)

    return pl.pallas_call(
        body, name="dw_in", grid=(PW // tn, nk),
        in_specs=[pl.BlockSpec((tk, D), lambda j, k: (k, 0)), pl.BlockSpec((tk, tn), lambda j, k: (k, j))],
        out_specs=pl.BlockSpec((tn, D), lambda j, k: (j, 0)),
        out_shape=_sds((PW, D), bf16),
        scratch_shapes=[pltpu.VMEM((tn, D), f32)],
        compiler_params=_cparams(("parallel", "arbitrary"), VMEM_BIG),
    )(h, dp)


def loss_head(xf, target, t_ctx):
    nb, s, _ = xf.shape
    jc = t_ctx // TM

    def body(x_ref, t_ref, dx_ref, l_ref):
        b, j = pl.program_id(0), pl.program_id(1)

        @pl.when((b == 0) & (j == 0))
        def _():
            l_ref[...] = jnp.zeros_like(l_ref)

        @pl.when(j < jc)
        def _():
            dx_ref[...] = jnp.zeros_like(dx_ref)

        @pl.when(j >= jc)
        def _():
            diff = x_ref[0] - t_ref[0]
            dx_ref[0] = diff * (1.0 / D)
            l_ref[...] += 0.5 * jnp.sum(diff * diff) * (1.0 / D)

    return pl.pallas_call(
        body, name="loss_head", grid=(nb, s // TM),
        in_specs=[pl.BlockSpec((1, TM, D), lambda b, j: (b, j, 0)),
                  pl.BlockSpec((1, TM, D), lambda b, j: (b, jnp.maximum(j - jc, 0), 0))],
        out_specs=[pl.BlockSpec((1, TM, D), lambda b, j: (b, j, 0)), _full((1, 128))],
        out_shape=[_sds((nb, s, D)), _sds((1, 128))],
        compiler_params=_cparams(("arbitrary", "arbitrary")),
    )(xf, target)


def _chunk_maps(n_ctx, n_lat):
    n = n_ctx + n_lat

    def cf(t):
        return t

    def cb(t):
        return jnp.where(t < n_ctx, n_ctx - 1 - t, n - 1 - t + n_ctx)
    return n, cf, cb


def ret_scan_fwd(p3, cos, sins, consts, t_ctx):
    nb, s, _ = p3.shape
    n, cf, cb = _chunk_maps(t_ctx // RC, (s - t_ctx) // RC)
    nz = 4 * nb

    def body(qf, kf, vf, qb, kb, vb, cosf, sinf, cosb, sinb, intra_r, qdec_r, kdec_r, cd_r, bd_r, bdr_r,
             of_ref, ob_ref, sall_ref, s_sc):
        @pl.when(pl.program_id(0) == 0)
        def _():
            s_sc[...] = jnp.zeros_like(s_sc)
        st = s_sc[...]
        sall_ref[0] = st
        s_new, o = _ret_step(st, _pairs(qf, qb, nb), _pairs(kf, kb, nb), _pairs(vf, vb, nb),
                             _pair_tables(cosf, cosb, nb), _pair_tables(sinf, sinb, nb), intra_r[...], qdec_r[...],
                             kdec_r[...], cd_r[...], bd_r[...], bdr_r[...])
        s_sc[...] = s_new
        _unpairs(o, of_ref, ob_ref, nb)

    def pspec(m, seg):
        return pl.BlockSpec((nb, RC, BW), lambda t: (0, m(t), seg))

    def tspec(m):
        return pl.BlockSpec((RC, PAIR_W), lambda t: (m(t), 0))

    return pl.pallas_call(
        body, name="ret_scan_fwd", grid=(n,),
        in_specs=[pspec(cf, 0), pspec(cf, 1), pspec(cf, 2), pspec(cb, 0), pspec(cb, 1), pspec(cb, 2),
                  tspec(cf), tspec(cf), tspec(cb), tspec(cb)] + [_full(c.shape) for c in consts],
        out_specs=[pl.BlockSpec((nb, RC, BW), lambda t: (0, cf(t), 0)),
                   pl.BlockSpec((nb, RC, BW), lambda t: (0, cb(t), 0)),
                   pl.BlockSpec((1, nz, PAIR_W, PAIR_W), lambda t: (t, 0, 0, 0))],
        out_shape=[_sds((nb, s, BW)), _sds((nb, s, BW)), _sds((n, nz, PAIR_W, PAIR_W))],
        scratch_shapes=[pltpu.VMEM((nz, PAIR_W, PAIR_W), f32)],
        compiler_params=_cparams(("arbitrary",)),
    )(p3, p3, p3, p3, p3, p3, cos, sins, cos, sins, *consts)


def ret_scan_bwd(p3, cos, sins, consts, s_all, do, t_ctx):
    nb, s, _ = p3.shape
    n, cf, cb = _chunk_maps(t_ctx // RC, (s - t_ctx) // RC)
    nz = 4 * nb

    def rf(t):
        return cf(n - 1 - t)

    def rb(t):
        return cb(n - 1 - t)

    def body(qf, kf, vf, qb, kb, vb, cosf, sinf, cosb, sinb, intra_r, qdec_r, kdec_r, cd_r, bd_r, bdr_r,
             sall_ref, dof, dob, dqf, dkf, dvf, dqb, dkb, dvb, ds_sc):
        @pl.when(pl.program_id(0) == 0)
        def _():
            ds_sc[...] = jnp.zeros_like(ds_sc)
        step = functools.partial(_ret_step, cos=_pair_tables(cosf, cosb, nb), sins=_pair_tables(sinf, sinb, nb),
                                 intra=intra_r[...], qdec=qdec_r[...], kdec=kdec_r[...], cd=cd_r[...], bd2=bd_r[...],
                                 bdr=bdr_r[...])
        _, vjp = jax.vjp(step, sall_ref[0], _pairs(qf, qb, nb), _pairs(kf, kb, nb), _pairs(vf, vb, nb))
        ds, dq, dk, dv = vjp((ds_sc[...], _pairs(dof, dob, nb)))
        ds_sc[...] = ds
        _unpairs(dq, dqf, dqb, nb)
        _unpairs(dk, dkf, dkb, nb)
        _unpairs(dv, dvf, dvb, nb)

    def pspec(m, seg):
        return pl.BlockSpec((nb, RC, BW), lambda t: (0, m(t), seg))

    def tspec(m):
        return pl.BlockSpec((RC, PAIR_W), lambda t: (m(t), 0))

    def ospec(m):
        return pl.BlockSpec((nb, RC, BW), lambda t: (0, m(t), 0))

    return pl.pallas_call(
        body, name="ret_scan_bwd", grid=(n,),
        in_specs=[pspec(rf, 0), pspec(rf, 1), pspec(rf, 2), pspec(rb, 0), pspec(rb, 1), pspec(rb, 2),
                  tspec(rf), tspec(rf), tspec(rb), tspec(rb)] + [_full(c.shape) for c in consts]
                 + [pl.BlockSpec((1, nz, PAIR_W, PAIR_W), lambda t: (n - 1 - t, 0, 0, 0)), ospec(rf), ospec(rb)],
        out_specs=[ospec(rf), ospec(rf), ospec(rf), ospec(rb), ospec(rb), ospec(rb)],
        out_shape=[_sds((nb, s, BW))] * 6,
        scratch_shapes=[pltpu.VMEM((nz, PAIR_W, PAIR_W), f32)],
        compiler_params=_cparams(("arbitrary",), VMEM_BIG),
    )(p3, p3, p3, p3, p3, p3, cos, sins, cos, sins, *consts, s_all, do, do)


def mix_finish_fwd(fn, name, o_f, o_b, p3, zseg, norm_g, bd):
    nb, s, _ = p3.shape

    def body(of_ref, ob_ref, z_ref, g_ref, bd_ref, y_ref):
        y_ref[0] = fn(of_ref[0], ob_ref[0], z_ref[0], g_ref[...], bd_ref[...]).astype(bf16)

    blk = pl.BlockSpec((1, TM, BW), lambda b, j: (b, j, 0))
    return pl.pallas_call(
        body, name=name, grid=(nb, s // TM),
        in_specs=[blk, blk, pl.BlockSpec((1, TM, BW), lambda b, j: (b, j, zseg)), _full((1, BW)), _full((BW, BW))],
        out_specs=blk, out_shape=_sds((nb, s, BW), bf16),
        compiler_params=_cparams(("arbitrary", "arbitrary")),
    )(o_f, o_b, p3, norm_g, bd)


def mix_finish_bwd(fn, name, o_f, o_b, p3, zseg, norm_g, bd, dy3, yseg):
    nb, s, _ = p3.shape

    def body(of_ref, ob_ref, z_ref, g_ref, bd_ref, dy_ref, do_ref, dz_ref, dg_ref):
        @pl.when((pl.program_id(0) == 0) & (pl.program_id(1) == 0))
        def _():
            dg_ref[...] = jnp.zeros_like(dg_ref)
        bdv = bd_ref[...]
        _, vjp = jax.vjp(lambda a, b, z, g: fn(a, b, z, g, bdv), of_ref[0], ob_ref[0], z_ref[0], g_ref[...])
        do, _, dz, dg = vjp(dy_ref[0])
        do_ref[0] = do
        dz_ref[0] = dz
        dg_ref[...] += dg

    blk = pl.BlockSpec((1, TM, BW), lambda b, j: (b, j, 0))
    return pl.pallas_call(
        body, name=name, grid=(nb, s // TM),
        in_specs=[blk, blk, pl.BlockSpec((1, TM, BW), lambda b, j: (b, j, zseg)), _full((1, BW)), _full((BW, BW)),
                  pl.BlockSpec((1, TM, BW), lambda b, j: (b, j, yseg))],
        out_specs=[blk, blk, _full((1, BW))],
        out_shape=[_sds((nb, s, BW)), _sds((nb, s, BW)), _sds((1, BW))],
        compiler_params=_cparams(("arbitrary", "arbitrary")),
    )(o_f, o_b, p3, norm_g, bd, dy3)


def gdn_conv_fwd(p3, w, seg, t_ctx):
    nb, s, _ = p3.shape
    sd, su = _make_shifts(t_ctx, s)

    def body(x_ref, w_ref, o_ref):
        o_ref[0] = _silu(_conv3(x_ref[0], w_ref[...], sd, su))

    return pl.pallas_call(
        body, name="gdn_conv_fwd", grid=(nb, 2),
        in_specs=[pl.BlockSpec((1, s, 128), lambda b, j: (b, 0, 2 * seg + j)), pl.BlockSpec((3, 128), lambda b, j: (0, j))],
        out_specs=pl.BlockSpec((1, s, 128), lambda b, j: (b, 0, j)),
        out_shape=_sds((nb, s, BW)),
        compiler_params=_cparams(("arbitrary", "arbitrary")),
    )(p3, w)


def gdn_conv_bwd(p3, w, seg, d_f, d_b, t_ctx):
    nb, s, _ = p3.shape
    sd, su = _make_shifts(t_ctx, s)

    def body(x_ref, w_ref, df_ref, db_ref, dx_ref, dw_ref):
        @pl.when(pl.program_id(1) == 0)
        def _():
            dw_ref[...] = jnp.zeros_like(dw_ref)
        _, vjp = jax.vjp(lambda x, w_: _silu(_conv3(x, w_, sd, su)), x_ref[0], w_ref[...])
        dx, dw = vjp(df_ref[0] + db_ref[0])
        dx_ref[0] = dx
        dw_ref[...] += dw

    blk = pl.BlockSpec((1, s, 128), lambda j, b: (b, 0, j))
    return pl.pallas_call(
        body, name="gdn_conv_bwd", grid=(2, nb),
        in_specs=[pl.BlockSpec((1, s, 128), lambda j, b: (b, 0, 2 * seg + j)), pl.BlockSpec((3, 128), lambda j, b: (0, j)),
                  blk, blk],
        out_specs=[blk, pl.BlockSpec((3, 128), lambda j, b: (0, j))],
        out_shape=[_sds((nb, s, BW)), _sds((3, BW))],
        compiler_params=_cparams(("arbitrary", "arbitrary"), VMEM_BIG),
    )(p3, w, d_f, d_b)


def _pairs(f_ref, b_ref, nb):
    return jnp.stack([r[b, :, PAIR_W * p:PAIR_W * (p + 1)] for b in range(nb) for r in (f_ref, b_ref) for p in range(2)])


def _pair_tables(f_ref, b_ref, nb):
    return jnp.stack([r[...] for _ in range(nb) for r in (f_ref, b_ref) for _ in range(2)])


def _gates(f_ref, b_ref, nb):
    return jnp.stack([r[b] for b in range(nb) for r in (f_ref, b_ref)])


def _unpairs(a, f_ref, b_ref, nb):
    for b in range(nb):
        for d, r in enumerate((f_ref, b_ref)):
            for p in range(2):
                r[b, :, PAIR_W * p:PAIR_W * (p + 1)] = a[4 * b + 2 * d + p]


def _with_exchange(body, n_in, n_out, n_scratch, xchg, n_steps):
    if xchg is None:
        return body, [], [], [], []
    kind, arrs = xchg
    nx = len(arrs)

    def fused(*refs):
        ins, rest = refs[:n_in], refs[n_in:]
        srcs, rest = rest[:nx], rest[nx:]
        outs, rest = rest[:n_out], rest[n_out:]
        dsts, rest = rest[:nx], rest[nx:]
        scratch, sems = rest[:n_scratch], rest[n_scratch:]
        start, wait = _peer_exchange(kind, "chips", srcs, dsts, *sems)
        pl.when(pl.program_id(0) == 0)(start)
        body(*ins, *outs, *scratch)
        pl.when(pl.program_id(0) == n_steps - 1)(wait)

    any_ = pl.BlockSpec(memory_space=pl.ANY)
    return fused, [any_] * nx, [any_] * nx, _exchange_shapes(kind, "chips", arrs), _exchange_scratch("chips", nx)


def gdn_scan_fwd(cq, ck, cv, p3, alog, dtb, consts, t_ctx, xchg=None):
    nb, s, _ = p3.shape
    n, cf, cb = _chunk_maps(t_ctx // GC, (s - t_ctx) // GC)
    gblk = GATE_COL // 128

    nz = 4 * nb

    def body(qf, kf, vf, gf, qb, kb, vb, gb, al_ref, dt_ref, tm_r, tm2_r, st2_r, eg_r, eb_r, egt_r, dsel_r, eye_r, bd_r,
             of_ref, ob_ref, sall_ref, inv_ref, s_sc):
        @pl.when(pl.program_id(0) == 0)
        def _():
            s_sc[...] = jnp.zeros_like(s_sc)
        st = s_sc[...]
        sall_ref[0] = st
        s_new, o, inv = _gdn_step(st, _pairs(qf, qb, nb), _pairs(kf, kb, nb), _pairs(vf, vb, nb), _gates(gf, gb, nb),
                                  al_ref[...], dt_ref[...], tm_r[...], tm2_r[...], st2_r[...], eg_r[...], eb_r[...],
                                  egt_r[...], dsel_r[...], eye_r[...], bd_r[...])
        s_sc[...] = s_new
        inv_ref[0] = inv
        _unpairs(o, of_ref, ob_ref, nb)

    def cspec(m):
        return pl.BlockSpec((nb, GC, BW), lambda t: (0, m(t), 0))

    def gspec(m):
        return pl.BlockSpec((nb, GC, 128), lambda t: (0, m(t), gblk))

    fused, x_in, x_out, x_shape, x_scratch = _with_exchange(body, 10 + len(consts), 4, 1, xchg, n)
    return pl.pallas_call(
        fused, name="gdn_scan_fwd" + ("" if xchg is None else "_" + xchg[0]), grid=(n,),
        in_specs=[cspec(cf), cspec(cf), cspec(cf), gspec(cf), cspec(cb), cspec(cb), cspec(cb), gspec(cb),
                  _full((1, 128)), _full((1, 128))] + [_full(c.shape) for c in consts] + x_in,
        out_specs=[cspec(cf), cspec(cb), pl.BlockSpec((1, nz, PAIR_W, PAIR_W), lambda t: (t, 0, 0, 0)),
                   pl.BlockSpec((1, nz, GC, PAIR_W), lambda t: (t, 0, 0, 0))] + x_out,
        out_shape=[_sds((nb, s, BW)), _sds((nb, s, BW)), _sds((n, nz, PAIR_W, PAIR_W)), _sds((n, nz, GC, PAIR_W))]
                  + x_shape,
        scratch_shapes=[pltpu.VMEM((nz, PAIR_W, PAIR_W), f32)] + x_scratch,
        compiler_params=_cparams(("arbitrary",)),
    )(cq, ck, cv, p3, cq, ck, cv, p3, alog, dtb, *consts, *([] if xchg is None else xchg[1]))


def gdn_scan_bwd(cq, ck, cv, p3, alog, dtb, consts, s_all, inv_all, do, t_ctx, xchg=None):
    nb, s, _ = p3.shape
    n, cf, cb = _chunk_maps(t_ctx // GC, (s - t_ctx) // GC)
    gblk = GATE_COL // 128

    def rf(t):
        return cf(n - 1 - t)

    def rb(t):
        return cb(n - 1 - t)

    nz = 4 * nb

    def body(qf, kf, vf, gf, qb, kb, vb, gb, al_ref, dt_ref, tm_r, tm2_r, st2_r, eg_r, eb_r, egt_r, dsel_r, eye_r, bd_r,
             sall_ref, inv_ref, dof, dob, dqf, dkf, dvf, dgf, dqb, dkb, dvb, dgb, dal_ref, ddt_ref, ds_sc):
        @pl.when(pl.program_id(0) == 0)
        def _():
            dal_ref[...] = jnp.zeros_like(dal_ref)
            ddt_ref[...] = jnp.zeros_like(ddt_ref)
            ds_sc[...] = jnp.zeros_like(ds_sc)
        consts = dict(tmask=tm_r[...], tmask2=tm2_r[...], strict2=st2_r[...], exp_g=eg_r[...], exp_b=eb_r[...],
                      exp_gt=egt_r[...], dsel2=dsel_r[...], eye2=eye_r[...], bd2=bd_r[...], inv=inv_ref[0])

        def step(*a):
            return _gdn_step(*a, **consts)[:2]

        _, vjp = jax.vjp(step, sall_ref[0], _pairs(qf, qb, nb), _pairs(kf, kb, nb), _pairs(vf, vb, nb),
                         _gates(gf, gb, nb), al_ref[...], dt_ref[...])
        ds, dq, dk, dv, dg, dal, ddt = vjp((ds_sc[...], _pairs(dof, dob, nb)))
        ds_sc[...] = ds
        _unpairs(dq, dqf, dqb, nb)
        _unpairs(dk, dkf, dkb, nb)
        _unpairs(dv, dvf, dvb, nb)
        for b in range(nb):
            dgf[b] = dg[2 * b]
            dgb[b] = dg[2 * b + 1]
        dal_ref[...] += dal
        ddt_ref[...] += ddt

    def cspec(m):
        return pl.BlockSpec((nb, GC, BW), lambda t: (0, m(t), 0))

    def gspec(m):
        return pl.BlockSpec((nb, GC, 128), lambda t: (0, m(t), gblk))

    def gout(m):
        return pl.BlockSpec((nb, GC, 128), lambda t: (0, m(t), 0))

    fused, x_in, x_out, x_shape, x_scratch = _with_exchange(body, 14 + len(consts), 10, 1, xchg, n)
    return pl.pallas_call(
        fused, name="gdn_scan_bwd" + ("" if xchg is None else "_" + xchg[0]), grid=(n,),
        in_specs=[cspec(rf), cspec(rf), cspec(rf), gspec(rf), cspec(rb), cspec(rb), cspec(rb), gspec(rb),
                  _full((1, 128)), _full((1, 128))] + [_full(c.shape) for c in consts]
                 + [pl.BlockSpec((1, nz, PAIR_W, PAIR_W), lambda t: (n - 1 - t, 0, 0, 0)),
                    pl.BlockSpec((1, nz, GC, PAIR_W), lambda t: (n - 1 - t, 0, 0, 0)), cspec(rf), cspec(rb)] + x_in,
        out_specs=[cspec(rf), cspec(rf), cspec(rf), gout(rf), cspec(rb), cspec(rb), cspec(rb), gout(rb),
                   _full((1, 128)), _full((1, 128))] + x_out,
        out_shape=[_sds((nb, s, BW))] * 3 + [_sds((nb, s, 128))] + [_sds((nb, s, BW))] * 3 + [_sds((nb, s, 128))]
                  + [_sds((1, 128)), _sds((1, 128))] + x_shape,
        scratch_shapes=[pltpu.VMEM((nz, PAIR_W, PAIR_W), f32)] + x_scratch,
        compiler_params=_cparams(("arbitrary",), VMEM_BIG),
    )(cq, ck, cv, p3, cq, ck, cv, p3, alog, dtb, *consts, s_all, inv_all, do, do, *([] if xchg is None else xchg[1]))


def _sg_consts():
    hmp = np.zeros((2, NH, PAIR_W))
    for h in range(NH):
        hmp[h // 2, h, (h % 2) * HD:(h % 2 + 1) * HD] = 1.0
    bdr = (np.arange(2 * RC)[:, None] // RC == np.arange(PAIR_W)[None, :] // HD)
    return jnp.asarray(hmp, f32), jnp.asarray(bdr, f32)


def _sg_rows(s):
    return 6 * RC if s % (6 * RC) == 0 else 2 * RC


def _halves(ref):
    return ref[0, :, :PAIR_W], ref[0, :, PAIR_W:]


def sg_fwd(p3, w, b, hmp, bdr):
    nb, s, _ = p3.shape
    ts = _sg_rows(s)

    def body(u_ref, v_ref, z_ref, w_ref, b_ref, hm_ref, bdr_ref, y_ref):
        y0, y1 = _sg_block(*_halves(u_ref), *_halves(v_ref), *_halves(z_ref), w_ref[...], b_ref[...], hm_ref[...],
                           bdr_ref[...])
        y_ref[0, :, :PAIR_W] = y0.astype(bf16)
        y_ref[0, :, PAIR_W:] = y1.astype(bf16)

    def seg(k):
        return pl.BlockSpec((1, ts, BW), lambda bi, i: (bi, i, k))

    return pl.pallas_call(
        body, name="sg_fwd", grid=(nb, s // ts),
        in_specs=[seg(4), seg(5), seg(6), _full((NH, RC, RC)), _full((NH, RC)), _full(hmp.shape), _full(bdr.shape)],
        out_specs=pl.BlockSpec((1, ts, BW), lambda bi, i: (bi, i, 0)),
        out_shape=_sds((nb, s, BW), bf16),
        compiler_params=_cparams(("arbitrary", "arbitrary")),
    )(p3, p3, p3, w, b, hmp, bdr)


def sg_bwd(p3, w, b, hmp, bdr, dy3):
    nb, s, _ = p3.shape
    ts = _sg_rows(s)

    def body(u_ref, v_ref, z_ref, w_ref, b_ref, hm_ref, bdr_ref, dy_ref, du_ref, dv_ref, dz_ref, dw_ref, db_ref):
        @pl.when((pl.program_id(0) == 0) & (pl.program_id(1) == 0))
        def _():
            dw_ref[...] = jnp.zeros_like(dw_ref)
            db_ref[...] = jnp.zeros_like(db_ref)
        hm, bdr_v = hm_ref[...], bdr_ref[...]
        _, vjp = jax.vjp(lambda *a: _sg_block(*a, hm, bdr_v), *_halves(u_ref), *_halves(v_ref), *_halves(z_ref),
                         w_ref[...], b_ref[...])
        du0, du1, dv0, dv1, dz0, dz1, dw, db = vjp(_halves(dy_ref))
        for ref, a0, a1 in ((du_ref, du0, du1), (dv_ref, dv0, dv1), (dz_ref, dz0, dz1)):
            ref[0, :, :PAIR_W] = a0
            ref[0, :, PAIR_W:] = a1
        dw_ref[...] += dw
        db_ref[...] += db

    def seg(k):
        return pl.BlockSpec((1, ts, BW), lambda bi, i: (bi, i, k))

    blk = pl.BlockSpec((1, ts, BW), lambda bi, i: (bi, i, 0))
    return pl.pallas_call(
        body, name="sg_bwd", grid=(nb, s // ts),
        in_specs=[seg(4), seg(5), seg(6), _full((NH, RC, RC)), _full((NH, RC)), _full(hmp.shape), _full(bdr.shape),
                  seg(1)],
        out_specs=[blk, blk, blk, _full((NH, RC, RC)), _full((NH, RC))],
        out_shape=[_sds((nb, s, BW))] * 3 + [_sds((NH, RC, RC)), _sds((NH, RC))],
        compiler_params=_cparams(("arbitrary", "arbitrary"), VMEM_BIG),
    )(p3, p3, p3, w, b, hmp, bdr, dy3)


def _sc_fn(b, c, h, z, w, sd, su):
    return b * _conv3(c * h, w, sd, su) * _silu(z)


def sc_fwd(p3, w, t_ctx):
    nb, s, _ = p3.shape
    sd, su = _make_shifts(t_ctx, s)

    def body(b_ref, c_ref, h_ref, z_ref, w_ref, y_ref):
        y_ref[0] = _sc_fn(b_ref[0], c_ref[0], h_ref[0], z_ref[0], w_ref[...], sd, su).astype(bf16)

    def seg(k):
        return pl.BlockSpec((1, s, 128), lambda bi, j: (bi, 0, 2 * k + j))

    return pl.pallas_call(
        body, name="sc_fwd", grid=(nb, 2),
        in_specs=[seg(7), seg(8), seg(9), seg(10), pl.BlockSpec((3, 128), lambda bi, j: (0, j))],
        out_specs=pl.BlockSpec((1, s, 128), lambda bi, j: (bi, 0, j)),
        out_shape=_sds((nb, s, BW), bf16),
        compiler_params=_cparams(("arbitrary", "arbitrary"), VMEM_BIG),
    )(p3, p3, p3, p3, w)


def sc_bwd(p3, w, dy3, t_ctx):
    nb, s, _ = p3.shape
    sd, su = _make_shifts(t_ctx, s)

    def body(b_ref, c_ref, h_ref, z_ref, w_ref, dy_ref, db_ref, dc_ref, dh_ref, dz_ref, dw_ref):
        @pl.when(pl.program_id(1) == 0)
        def _():
            dw_ref[...] = jnp.zeros_like(dw_ref)
        _, vjp = jax.vjp(lambda b, c, h, z, w_: _sc_fn(b, c, h, z, w_, sd, su),
                         b_ref[0], c_ref[0], h_ref[0], z_ref[0], w_ref[...])
        db, dc, dh, dz, dw = vjp(dy_ref[0])
        db_ref[0] = db
        dc_ref[0] = dc
        dh_ref[0] = dh
        dz_ref[0] = dz
        dw_ref[...] += dw

    def seg(k):
        return pl.BlockSpec((1, s, 128), lambda j, bi: (bi, 0, 2 * k + j))

    blk = pl.BlockSpec((1, s, 128), lambda j, bi: (bi, 0, j))
    wspec = pl.BlockSpec((3, 128), lambda j, bi: (0, j))
    return pl.pallas_call(
        body, name="sc_bwd", grid=(2, nb),
        in_specs=[seg(7), seg(8), seg(9), seg(10), wspec, seg(2)],
        out_specs=[blk, blk, blk, blk, wspec],
        out_shape=[_sds((nb, s, BW))] * 4 + [_sds((3, BW))],
        compiler_params=_cparams(("arbitrary", "arbitrary"), VMEM_BIG),
    )(p3, p3, p3, p3, w, dy3)


def assemble_dp(pairs, singles_a, gdn_x, singles_b, gates):
    nb, s, _ = singles_a[0].shape
    flat = [a for pr in pairs for a in pr] + list(singles_a) + list(gdn_x) + list(singles_b) + list(gates)
    n_pairs, n_a, n_x, n_b = len(pairs), len(singles_a), len(gdn_x), len(singles_b)

    def body(*refs):
        out = refs[-1]
        ins = refs[:-1]
        col = 0
        for p in range(n_pairs):
            out[0, :, col:col + BW] = (ins[2 * p][0] + ins[2 * p + 1][0]).astype(bf16)
            col += BW
        k = 2 * n_pairs
        for _ in range(n_a + n_x + n_b):
            out[0, :, col:col + BW] = ins[k][0].astype(bf16)
            col += BW
            k += 1
        out[0, :, col:col + 128] = (ins[k][0] + ins[k + 1][0]).astype(bf16)
        out[0, :, col + 128:] = jnp.zeros((TM, PW - col - 128), bf16)

    def spec(a):
        return pl.BlockSpec((1, TM, a.shape[-1]), lambda b, j: (b, j, 0))

    return pl.pallas_call(
        body, name="assemble_dp", grid=(nb, s // TM),
        in_specs=[spec(a) for a in flat],
        out_specs=pl.BlockSpec((1, TM, PW), lambda b, j: (b, j, 0)),
        out_shape=_sds((nb, s, PW), bf16),
        compiler_params=_cparams(("arbitrary", "arbitrary")),
    )(*flat)


def mod_fwd(c_rows, w_mod, b_cols):
    nl, _, wc = w_mod.shape
    nr = c_rows.shape[0]

    def body(c_ref, w_ref, b_ref, o_ref):
        o_ref[0] = _dot(_silu(c_ref[...]), w_ref[0], precision=HI) + b_ref[0]

    return pl.pallas_call(
        body, name="mod_fwd", grid=(nl,),
        in_specs=[_full((nr, D)), pl.BlockSpec((1, D, wc), lambda l: (l, 0, 0)), pl.BlockSpec((1, 1, wc), lambda l: (l, 0, 0))],
        out_specs=pl.BlockSpec((1, nr, wc), lambda l: (l, 0, 0)),
        out_shape=_sds((nl, nr, wc)),
        compiler_params=_cparams(("arbitrary",)),
    )(c_rows, w_mod, b_cols)


def mod_bwd(c_rows, w_mod, dm_cols, dm_full):
    nl, _, wc = w_mod.shape
    nr = c_rows.shape[0]

    def body(c_ref, w_ref, dmc_ref, dmf_ref, gw_ref, gb_ref, dcc_ref):
        @pl.when(pl.program_id(0) == 0)
        def _():
            dcc_ref[...] = jnp.zeros_like(dcc_ref)
        a = _silu(c_ref[...])
        dmc = dmc_ref[0]
        gw_ref[0] = _dot_tn(a, dmc, precision=HI)
        gb_ref[0] = jnp.sum(dmf_ref[0], axis=0, keepdims=True)
        dcc_ref[...] += _dot_nt(dmc[nr - 8:nr], w_ref[0], precision=HI)

    return pl.pallas_call(
        body, name="mod_bwd", grid=(nl,),
        in_specs=[_full((nr, D)), pl.BlockSpec((1, D, wc), lambda l: (l, 0, 0)),
                  pl.BlockSpec((1, nr, wc), lambda l: (l, 0, 0)), pl.BlockSpec((1, nr, 3 * D), lambda l: (l, 0, 0))],
        out_specs=[pl.BlockSpec((1, D, wc), lambda l: (l, 0, 0)), pl.BlockSpec((1, 1, 3 * D), lambda l: (l, 0, 0)),
                   _full((8, D))],
        out_shape=[_sds((nl, D, wc)), _sds((nl, 1, 3 * D)), _sds((8, D))],
        compiler_params=_cparams(("arbitrary",)),
    )(c_rows, w_mod, dm_cols, dm_full)


def cctx_grad(parts, c_ctx):
    def body(p_ref, c_ref, o_ref):
        tot = p_ref[0, 0:1, :]
        for k in (2, 4, 6):
            tot = tot + p_ref[k, 0:1, :]
        c = c_ref[...]
        sg = jax.nn.sigmoid(c)
        o_ref[...] = tot * (sg * (1.0 + c * (1.0 - sg)))

    return pl.pallas_call(body, name="cctx_grad", out_shape=_sds((1, D)))(parts, c_ctx)


def sum_lead(x, out_dtype=f32, tr=256, tc=None):
    k, r, c = x.shape
    tr = min(tr, r)
    tc = c if tc is None else tc
    assert r % tr == 0 and c % tc == 0

    def body(x_ref, o_ref):
        tot = x_ref[0].astype(f32)
        for i in range(1, k):
            tot = tot + x_ref[i].astype(f32)
        o_ref[...] = tot.astype(out_dtype)

    return pl.pallas_call(
        body, name="sum_lead", grid=(r // tr, c // tc),
        in_specs=[pl.BlockSpec((k, tr, tc), lambda i, j: (0, i, j))],
        out_specs=pl.BlockSpec((tr, tc), lambda i, j: (i, j)),
        out_shape=_sds((r, c), out_dtype),
        compiler_params=_cparams(("arbitrary", "arbitrary")),
    )(x)


def adamw(w, m, v, g1, g2=None, tr=256, block=None):
    if block is None:
        block = (1,) * (w.ndim - 2) + (min(tr, w.shape[-2]), w.shape[-1])
    assert len(block) == w.ndim and all(d % b == 0 for d, b in zip(w.shape, block))
    two = g2 is not None
    c1 = 1.0 / (1.0 - ADAM_B1 ** ADAM_STEP)
    c2 = 1.0 / (1.0 - ADAM_B2 ** ADAM_STEP)

    def body(*refs):
        w_ref, m_ref, v_ref, g_ref = refs[:4]
        g = g_ref[...]
        if two:
            g = g + refs[4][...]
        go_ref, d_ref, mo_ref, vo_ref = refs[-4:]
        mn = ADAM_B1 * m_ref[...] + (1.0 - ADAM_B1) * g
        vn = ADAM_B2 * v_ref[...] + (1.0 - ADAM_B2) * (g * g)
        go_ref[...] = g
        mo_ref[...] = mn
        vo_ref[...] = vn
        d_ref[...] = -ADAM_LR * ((mn * c1) / (jnp.sqrt(vn * c2) + ADAM_EPS) + ADAM_WD * w_ref[...])

    blk = pl.BlockSpec(block, lambda *i: i)
    grid = tuple(d // b for d, b in zip(w.shape, block))
    args = [w, m, v, g1] + ([g2] if two else [])
    return pl.pallas_call(
        body, name="adamw", grid=grid,
        in_specs=[blk] * len(args), out_specs=[blk] * 4, out_shape=[_sds(w.shape)] * 4,
        compiler_params=_cparams(("arbitrary",) * len(grid)),
    )(*args)


def _my_pos():
    return lax.axis_index("x"), lax.axis_index("y"), lax.axis_index("c")


GROUP_SIZE = {"devices": N_DEV, "chips": N_CHIPS, "cores": 2}


def _peer_exchange(kind, group, src_refs, dst_refs, send_sems, recv_sems, local_sems):
    mx, my, mc = _my_pos()
    n = GROUP_SIZE[group]
    if group == "devices":
        me = 4 * mx + 2 * my + mc
    elif group == "chips":
        me = 2 * mx + my
    else:
        me = mc

    def peer(k):
        if group == "devices":
            return (mx ^ (k >> 2), my ^ ((k >> 1) & 1), mc ^ (k & 1))
        if group == "chips":
            return (mx ^ (k >> 1), my ^ (k & 1), mc)
        return (mx, my, mc ^ k)

    def copies():
        local, sends, recvs = [], [], []
        for i, (src, dst) in enumerate(zip(src_refs, dst_refs)):
            def part(k):
                return src.at[k] if kind == "scatter" else src

            def slab(k):
                return dst if kind == "send" else dst.at[k]

            if kind != "send":
                local.append(pltpu.make_async_copy(part(me), dst.at[me], local_sems.at[i]))
            for k in range(1, n):
                sem = dict(send_sem=send_sems.at[i, k - 1], recv_sem=recv_sems.at[i, k - 1], device_id_type=MESH)
                sends.append(pltpu.make_async_remote_copy(src_ref=part(me ^ k), dst_ref=slab(me), device_id=peer(k), **sem))
                recvs.append(pltpu.make_async_remote_copy(src_ref=part(me ^ k), dst_ref=slab(me ^ k),
                                                          device_id=(mx, my, mc), **sem))
        return local, sends, recvs

    def start():
        local, sends, _ = copies()
        for cp in local + sends:
            cp.start()

    def wait():
        local, sends, recvs = copies()
        for cp in recvs:
            cp.wait_recv()
        for cp in sends:
            cp.wait_send()
        for cp in local:
            cp.wait()

    return start, wait


def _exchange_scratch(group, n):
    k = GROUP_SIZE[group] - 1
    return [pltpu.SemaphoreType.DMA((n, k)), pltpu.SemaphoreType.DMA((n, k)), pltpu.SemaphoreType.DMA((n,))]


def _exchange_shapes(kind, group, arrs):
    return [_sds(((GROUP_SIZE[group],) + a.shape) if kind == "gather" else a.shape, a.dtype) for a in arrs]


def exchange(name, parts):
    counts = [len(arrs) for _, _, arrs in parts]
    total = sum(counts)

    def body(*refs):
        srcs, dsts, sems = refs[:total], refs[total:2 * total], refs[2 * total:]
        ops, at = [], 0
        for j, (kind, group, arrs) in enumerate(parts):
            ops.append(_peer_exchange(kind, group, srcs[at:at + counts[j]], dsts[at:at + counts[j]], *sems[3 * j:3 * j + 3]))
            at += counts[j]
        for start, _ in ops:
            start()
        for _, wait in ops:
            wait()

    any_ = pl.BlockSpec(memory_space=pl.ANY)
    flat = [a for _, _, arrs in parts for a in arrs]
    outs = pl.pallas_call(
        body, name=name, out_shape=[sh for kind, group, arrs in parts for sh in _exchange_shapes(kind, group, arrs)],
        in_specs=[any_] * total, out_specs=[any_] * total,
        scratch_shapes=[sc for _, group, arrs in parts for sc in _exchange_scratch(group, len(arrs))],
    )(*flat)
    res, at = [], 0
    for cnt in counts:
        res.append(list(outs[at:at + cnt]))
        at += cnt
    return res


def gather8(x):
    return exchange("gather8", [("gather", "devices", [x])])[0][0]


PACK_ROWS = 64
SMALL = ("c_ctx", "b_mod", "g_pre", "g_post", "ret_norm_g", "sg_w", "sg_b", "sc_conv_w", "gdn_conv_w",
         "gdn_a_log", "gdn_dt_bias", "gdn_norm_g")


def _pack(arrs, width=D, mult=PACK_ROWS):
    rows = []
    for a in arrs:
        flat = a.reshape(-1)
        pad = (-flat.shape[0]) % width
        rows.append(jnp.pad(flat, (0, pad)).reshape(-1, width))
    out = jnp.concatenate(rows, axis=0)
    return jnp.pad(out, ((0, (-out.shape[0]) % mult), (0, 0)))


def _unpack(packed, shapes, width=D):
    outs, r = [], 0
    for shp in shapes:
        size = int(np.prod(shp))
        nr = -(-size // width)
        outs.append(packed[r:r + nr].reshape(-1)[:size].reshape(shp))
        r += nr
    return outs


def kernel(x, c, ctx, c_ctx, w_mod, b_mod, g_pre, g_post, w_in, w_out, ret_norm_g, sg_w, sg_b, sc_conv_w, gdn_conv_w, gdn_a_log, gdn_dt_bias, gdn_norm_g, loss_target, m_c_ctx, m_w_mod, m_b_mod, m_g_pre, m_g_post, m_w_in, m_w_out, m_ret_norm_g, m_sg_w, m_sg_b, m_sc_conv_w, m_gdn_conv_w, m_gdn_a_log, m_gdn_dt_bias, m_gdn_norm_g, v_c_ctx, v_w_mod, v_b_mod, v_g_pre, v_g_post, v_w_in, v_w_out, v_ret_norm_g, v_sg_w, v_sg_b, v_sc_conv_w, v_gdn_conv_w, v_gdn_a_log, v_gdn_dt_bias, v_gdn_norm_g):
    weights = dict(c_ctx=c_ctx, w_mod=w_mod, b_mod=b_mod, g_pre=g_pre, g_post=g_post, w_in=w_in, w_out=w_out,
                   ret_norm_g=ret_norm_g, sg_w=sg_w, sg_b=sg_b, sc_conv_w=sc_conv_w, gdn_conv_w=gdn_conv_w,
                   gdn_a_log=gdn_a_log, gdn_dt_bias=gdn_dt_bias, gdn_norm_g=gdn_norm_g)
    mom = dict(c_ctx=m_c_ctx, w_mod=m_w_mod, b_mod=m_b_mod, g_pre=m_g_pre, g_post=m_g_post, w_in=m_w_in,
               w_out=m_w_out, ret_norm_g=m_ret_norm_g, sg_w=m_sg_w, sg_b=m_sg_b, sc_conv_w=m_sc_conv_w,
               gdn_conv_w=m_gdn_conv_w, gdn_a_log=m_gdn_a_log, gdn_dt_bias=m_gdn_dt_bias, gdn_norm_g=m_gdn_norm_g)
    var = dict(c_ctx=v_c_ctx, w_mod=v_w_mod, b_mod=v_b_mod, g_pre=v_g_pre, g_post=v_g_post, w_in=v_w_in,
               w_out=v_w_out, ret_norm_g=v_ret_norm_g, sg_w=v_sg_w, sg_b=v_sg_b, sc_conv_w=v_sc_conv_w,
               gdn_conv_w=v_gdn_conv_w, gdn_a_log=v_gdn_a_log, gdn_dt_bias=v_gdn_dt_bias, gdn_norm_g=v_gdn_norm_g)

    nb, t_lat, _ = x.shape
    t_ctx = ctx.shape[1]
    s = t_ctx + t_lat
    n = nb * s
    sb = s // TM
    nl = w_in.shape[0]
    wc_in = w_in.shape[2]
    wc_mod = w_mod.shape[2]
    rows_out = w_out.shape[1]
    n_all = nb * N_DEV
    mx, my, mc = _my_pos()
    chip = 2 * mx + my
    dev = 2 * chip + mc

    sg_c = _sg_consts()
    bd = jnp.asarray(_block_diag())
    ret_c = _ret_consts(nb)
    gdn_c = _gdn_consts(nb)
    cos, sins = _rope_tables(t_lat, t_ctx)

    w_in_b, w_out_b = w_in.astype(bf16), w_out.astype(bf16)
    pre = _pack([c, sc_conv_w, gdn_conv_w], mult=8)
    (pre_all,), w0_parts = exchange("startup_gather", [("gather", "devices", [pre]),
                                                       ("gather", "chips", [w_in_b[0], w_out_b[0]])])
    c_parts, scw_parts, gcw_parts = [], [], []
    for k in range(N_DEV):
        ck, sk, gk = _unpack(pre_all[k], [c.shape, sc_conv_w.shape, gdn_conv_w.shape])
        c_parts.append(ck)
        if k % 2 == 0:
            scw_parts.append(sk)
            gcw_parts.append(gk)
    c_all = jnp.concatenate(c_parts, axis=0)
    sc_w_full = jnp.concatenate(scw_parts, axis=-1)
    gdn_w_full = jnp.concatenate(gcw_parts, axis=-1)
    c_rows = jnp.concatenate([c_all, c_ctx[None, :], jnp.zeros((7, D), f32)], axis=0)

    b_cols = lax.dynamic_slice_in_dim(b_mod, chip * wc_mod, wc_mod, axis=1)[:, None, :]
    mod_part = mod_fwd(c_rows, w_mod, b_cols)
    mod_all = gather8(mod_part)
    mod = jnp.concatenate([mod_all[2 * k] for k in range(N_CHIPS)], axis=-1)
    my_rows = jnp.concatenate([lax.dynamic_slice_in_dim(mod, dev * nb, nb, axis=1), mod[:, n_all:n_all + 1]], axis=1)
    shift_t = my_rows[:, :, None, 0:D]
    scale_t = my_rows[:, :, None, D:2 * D]
    gate_t = my_rows[:, :, None, 2 * D:3 * D]

    def full_weights(parts):
        wi = jnp.concatenate([parts[0][k] for k in range(N_CHIPS)], axis=-1)
        wo = jnp.concatenate([parts[1][k] for k in range(N_CHIPS)], axis=0)
        return jnp.pad(wi, ((0, 0), (0, PW - IN_W))), wo

    w_in_full, w_out_full = [None] * nl, [None] * nl
    w_in_full[0], w_out_full[0] = full_weights(w0_parts)

    alog = jnp.pad(gdn_a_log.reshape(nl, 1, 8), ((0, 0), (0, 0), (0, 120)))
    dtb = jnp.pad(gdn_dt_bias.reshape(nl, 1, 8), ((0, 0), (0, 0), (0, 120)))
    gdn_ng = jnp.tile(gdn_norm_g, (1, NH))[:, None, :]
    ret_ng = ret_norm_g[:, None, :]

    xs = jnp.concatenate([ctx, x], axis=1).reshape(n, D)
    saved = []
    for l in range(nl):
        p, h = inproj_fwd(xs, shift_t[l], scale_t[l], g_pre[l][None, :], w_in_full[l], nb, sb)
        p3 = p.reshape(nb, s, PW)
        ro_f, ro_b, rs_all = ret_scan_fwd(p3, cos, sins, ret_c, t_ctx)
        y_ret = mix_finish_fwd(_ret_finish, "ret_finish_fwd", ro_f, ro_b, p3, 3, ret_ng[l], bd)
        y_sg = sg_fwd(p3, sg_w[l], sg_b[l], *sg_c)
        y_sc = sc_fwd(p3, sc_w_full[l], t_ctx)
        cq, ck, cv = [gdn_conv_fwd(p3, gdn_w_full[l][:, BW * i:BW * (i + 1)], 11 + i, t_ctx) for i in range(3)]
        nxt = None if l + 1 == nl else ("gather", [w_in_b[l + 1], w_out_b[l + 1]])
        go_f, go_b, *gs_all = gdn_scan_fwd(cq, ck, cv, p3, alog[l], dtb[l], gdn_c, t_ctx, nxt)
        if nxt is not None:
            w_in_full[l + 1], w_out_full[l + 1] = full_weights(gs_all[2:])
            gs_all = gs_all[:2]
        y_gdn = mix_finish_fwd(_gdn_finish, "gdn_finish_fwd", go_f, go_b, p3, 14, gdn_ng[l], bd)
        ys = [a.reshape(n, BW) for a in (y_ret, y_sg, y_sc, y_gdn)]
        x_new, o = outproj_fwd(ys, w_out_full[l], xs, gate_t[l], g_post[l][None, :], nb, sb)
        saved.append(dict(x=xs, h=h, p3=p3, ro=(ro_f, ro_b), rs=rs_all, c=(cq, ck, cv), go=(go_f, go_b), gs=gs_all,
                          ys=ys, o=o))
        xs = x_new

    dx3, loss_part = loss_head(xs.reshape(nb, s, D), loss_target, t_ctx)
    loss = lax.psum(loss_part[0, 0], ("x", "y", "c"))

    dxs = dx3.reshape(n, D)
    g_small = {k: [None] * nl for k in SMALL if k not in ("c_ctx", "b_mod")}
    dm_rows = [None] * nl
    slabs = None
    got_in, got_out = [None] * nl, [None] * nl
    for l in reversed(range(nl)):
        sv = saved[l]
        p3 = sv["p3"]
        dy, gw_out, dg_post, dgate = outproj_bwd(dxs, sv["o"], gate_t[l], g_post[l][None, :], sv["ys"], w_out_full[l], nb, sb)
        dy3 = dy.reshape(nb, s, D)
        r_do, r_dz, d_rng = mix_finish_bwd(_ret_finish, "ret_finish_bwd", *sv["ro"], p3, 3, ret_ng[l], bd, dy3, 0)
        r_d = ret_scan_bwd(p3, cos, sins, ret_c, sv["rs"], r_do, t_ctx)
        s_du, s_dv, s_dz, d_sgw, d_sgb = sg_bwd(p3, sg_w[l], sg_b[l], *sg_c, dy3)
        c_db, c_dc, c_dh, c_dz, d_scw = sc_bwd(p3, sc_w_full[l], dy3, t_ctx)
        g_do, g_dz, d_gng = mix_finish_bwd(_gdn_finish, "gdn_finish_bwd", *sv["go"], p3, 14, gdn_ng[l], bd, dy3, 3)
        g_d = gdn_scan_bwd(*sv["c"], p3, alog[l], dtb[l], gdn_c, *sv["gs"], g_do, t_ctx,
                           None if slabs is None else ("scatter", slabs))
        if slabs is not None:
            got_in[l + 1], got_out[l + 1] = g_d[10:]
        gx, d_gcw = [], []
        for i in range(3):
            dxi, dwi = gdn_conv_bwd(p3, gdn_w_full[l][:, BW * i:BW * (i + 1)], 11 + i, g_d[i], g_d[4 + i], t_ctx)
            gx.append(dxi)
            d_gcw.append(dwi)
        dp3 = assemble_dp([(r_d[0], r_d[3]), (r_d[1], r_d[4]), (r_d[2], r_d[5])],
                          [r_dz, s_du, s_dv, s_dz, c_db, c_dc, c_dh, c_dz], gx, [g_dz], [g_d[3], g_d[7]])
        dp = dp3.reshape(n, PW)
        dxs, dg_pre, dshift, dscale = inproj_bwd_x(dp, w_in_full[l], sv["x"], scale_t[l], g_pre[l][None, :], dxs, nb, sb)
        gw_in = dw_in(sv["h"], dp)
        slabs = [jnp.stack([gw_in[k * wc_in:(k + 1) * wc_in] for k in range(N_CHIPS)]),
                 gw_out.reshape(N_CHIPS, rows_out, D).astype(bf16)]
        g_small["g_pre"][l] = dg_pre[0]
        g_small["g_post"][l] = dg_post[0]
        g_small["ret_norm_g"][l] = d_rng[0]
        g_small["sg_w"][l] = d_sgw
        g_small["sg_b"][l] = d_sgb
        g_small["sc_conv_w"][l] = d_scw
        g_small["gdn_conv_w"][l] = jnp.concatenate(d_gcw, axis=-1)
        g_small["gdn_a_log"][l] = g_d[8][0, :8].reshape(2, NH)
        g_small["gdn_dt_bias"][l] = g_d[9][0, :8].reshape(2, NH)
        g_small["gdn_norm_g"][l] = d_gng[0].reshape(NH, HD)
        dm_rows[l] = jnp.concatenate([dshift, dscale, dgate], axis=-1)[:nb + 1]
    grad_x = dxs.reshape(nb, s, D)[:, t_ctx:, :]

    g_small = {k: jnp.stack(v) for k, v in g_small.items()}
    dm_rows = jnp.stack(dm_rows)
    names2 = [k for k in SMALL if k not in ("c_ctx", "b_mod")]
    pack_sum = _pack([g_small[k] for k in names2] + [dm_rows[:, nb:]])
    pack_own = _pack([dm_rows[:, :nb]], mult=8)
    rs = -(-pack_sum.shape[0] // (8 * N_DEV)) * 8
    slabs_sum = jnp.pad(pack_sum, ((0, N_DEV * rs - pack_sum.shape[0]), (0, 0))).reshape(N_DEV, rs, D)
    (got_small,), (got_in[0], got_out[0]) = exchange("tail_scatter", [("scatter", "devices", [slabs_sum]),
                                                                      ("scatter", "chips", slabs)])
    my_slab = sum_lead(got_small, tr=rs)
    gin_mine = jnp.stack([sum_lead(a, tr=wc_in, tc=256) for a in got_in], axis=1)
    gout_mine = jnp.stack([sum_lead(a) for a in got_out])
    (all2,), (gin_sib, gout_sib) = exchange("tail_gather", [
        ("gather", "devices", [jnp.concatenate([my_slab, pack_own], axis=0)]), ("send", "cores", [gin_mine, gout_mine])])
    tot2 = all2[:, :rs].reshape(N_DEV * rs, D)
    outs2 = _unpack(tot2, [g_small[k].shape for k in names2] + [(nl, 1, 3 * D)])
    grads = dict(zip(names2, outs2[:-1]))
    dm_own = jnp.stack([_unpack(all2[k, rs:], [(nl, nb, 3 * D)])[0] for k in range(N_DEV)])
    dm_own = jnp.transpose(dm_own, (1, 0, 2, 3)).reshape(nl, n_all, 3 * D)
    dm_all = jnp.concatenate([dm_own, jnp.pad(outs2[-1], ((0, 0), (0, 7), (0, 0)))], axis=1)
    grads["gdn_norm_g"] = sum_lead(jnp.transpose(grads["gdn_norm_g"], (1, 0, 2)), tr=nl)
    for k in ("sc_conv_w", "gdn_conv_w"):
        wc = weights[k].shape[2]
        grads[k] = lax.dynamic_slice_in_dim(grads[k], chip * wc, wc, axis=2)

    dm_cols = lax.dynamic_slice_in_dim(dm_all, chip * wc_mod, wc_mod, axis=2)
    g_w_mod, g_b_mod, dcc_part = mod_bwd(c_rows, w_mod, dm_cols, dm_all)
    grads["b_mod"] = g_b_mod[:, 0, :]
    grads["c_ctx"] = cctx_grad(gather8(dcc_part), c_ctx[None, :])[0]

    res = {}
    w_in_t, m_in_t, v_in_t = [jnp.transpose(a, (2, 0, 1)) for a in (w_in, m_w_in, v_w_in)]
    res["w_in"] = [jnp.transpose(a, (1, 2, 0)) for a in
                   adamw(w_in_t, m_in_t, v_in_t, gin_mine, gin_sib, block=(wc_in // 4, nl, 256))]
    res["w_out"] = adamw(w_out, m_w_out, v_w_out, gout_mine, gout_sib)
    res["w_mod"] = adamw(w_mod, m_w_mod, v_w_mod, g_w_mod)
    shapes = [weights[k].shape for k in SMALL]
    small = adamw(_pack([weights[k] for k in SMALL]), _pack([mom[k] for k in SMALL]), _pack([var[k] for k in SMALL]),
                  _pack([grads[k].reshape(weights[k].shape) for k in SMALL]), tr=PACK_ROWS)
    small = [_unpack(a, shapes) for a in small]
    for i, k in enumerate(SMALL):
        res[k] = [small[j][i] for j in range(4)]

    order = ["c_ctx", "w_mod", "b_mod", "g_pre", "g_post", "w_in", "w_out", "ret_norm_g", "sg_w", "sg_b", "sc_conv_w",
             "gdn_conv_w", "gdn_a_log", "gdn_dt_bias", "gdn_norm_g"]
    return (loss, grad_x, *[res[k][0] for k in order], *[res[k][1] for k in order], *[res[k][2] for k in order],
            *[res[k][3] for k in order])
```

```python
import functools

import jax
import jax.numpy as jnp
import numpy as np
from jax import lax
from jax.experimental import pallas as pl
from jax.experimental.pallas import tpu as pltpu

f32 = jnp.float32
bf16 = jnp.bfloat16
HI = lax.Precision.HIGHEST
P3 = lax.Precision.HIGH
MESH = pl.DeviceIdType.MESH

EPS = 1e-6
D = 1024
NH = 4
HD = 64
BW = NH * HD
PAIR_W = 2 * HD
RC = 128
GC = 64
GRID_W = 64
ROPE_BASE = 10000.0
IN_W = 15 * BW + 16
PW = 4096
GATE_COL = 15 * BW
N_CHIPS = 4
N_DEV = 8
TM = 256
TP = 2 * TM
ADAM_LR, ADAM_B1, ADAM_B2, ADAM_EPS, ADAM_WD, ADAM_STEP = 0.001, 0.9, 0.999, 1e-08, 0.01, 10
LANE_HEAD = np.arange(BW) // HD
VMEM_BIG = 56 * 1024 * 1024


def _dot(a, b, precision=None):
    return jnp.dot(a, b, precision=precision, preferred_element_type=f32)


def _dot_nt(a, b, precision=None):
    return lax.dot_general(a, b, (((1,), (1,)), ((), ())), precision=precision, preferred_element_type=f32)


def _dot_tn(a, b, precision=None):
    return lax.dot_general(a, b, (((0,), (0,)), ((), ())), precision=precision, preferred_element_type=f32)


def _sds(shape, dtype=f32):
    return jax.ShapeDtypeStruct(shape, dtype)


def _cparams(sem=None, vmem=None):
    kw = {}
    if sem is not None:
        kw["dimension_semantics"] = sem
    if vmem is not None:
        kw["vmem_limit_bytes"] = vmem
    return pltpu.CompilerParams(**kw)


def _full(shape):
    n = len(shape)
    return pl.BlockSpec(shape, lambda *_: (0,) * n)


def _block_diag():
    return (LANE_HEAD[:, None] == LANE_HEAD[None, :]).astype(np.float32)


def _tau(c, d):
    return np.arange(c) if d == 0 else c - 1 - np.arange(c)


def _ret_consts(nb):
    lg = np.log(1.0 - 2.0 ** (-5.0 - np.arange(NH)))
    intra = np.zeros((2, 2, RC, 2 * RC)); qdec = np.zeros((2, 2, RC, PAIR_W)); kdec = np.zeros((2, 2, RC, PAIR_W))
    cd = np.zeros((2, 2, PAIR_W, PAIR_W))
    for d in range(2):
        t = _tau(RC, d)
        diff = t[:, None] - t[None, :]
        for p in range(2):
            lane_lg = lg[2 * p + np.arange(PAIR_W) // HD]
            for h in range(2):
                intra[d, p, :, h * RC:(h + 1) * RC] = np.where(diff >= 0, np.exp(np.maximum(diff, 0) * lg[2 * p + h]), 0.0)
            qdec[d, p] = np.exp((t[:, None] + 1.0) * lane_lg[None, :])
            kdec[d, p] = np.exp((RC - 1.0 - t[:, None]) * lane_lg[None, :])
            cd[d, p] = np.exp(RC * lane_lg)[:, None] * np.ones((1, PAIR_W))
    per_z = [np.tile(a.reshape((4,) + a.shape[2:]), (nb, 1, 1)) for a in (intra, qdec, kdec, cd)]
    bd2 = (np.arange(PAIR_W)[:, None] // HD == np.arange(PAIR_W)[None, :] // HD)
    bdr = (np.arange(2 * RC)[:, None] // RC == np.arange(PAIR_W)[None, :] // HD)
    return [jnp.asarray(a, f32) for a in per_z + [bd2, bdr]]


def _rope_tables(t_lat, t_ctx):
    nf = HD // 4
    inv = ROPE_BASE ** (-np.arange(nf) / nf)
    pos = np.arange(t_lat)
    ang_r = (pos // GRID_W)[:, None] * inv[None, :]
    ang_c = (pos % GRID_W)[:, None] * inv[None, :]
    ang = np.concatenate([ang_r, ang_r, ang_c, ang_c], axis=1)
    sign = np.concatenate([-np.ones(nf), np.ones(nf), -np.ones(nf), np.ones(nf)])
    cos = np.tile(np.cos(ang), (1, 2)); sins = np.tile(np.sin(ang) * sign, (1, 2))
    cos = np.concatenate([np.ones((t_ctx, PAIR_W)), cos]); sins = np.concatenate([np.zeros((t_ctx, PAIR_W)), sins])
    return jnp.asarray(cos, f32), jnp.asarray(sins, f32)


def _gdn_consts(nb):
    tmask = np.zeros((2, 2, GC, GC)); tmask2 = np.zeros((2, 2, GC, PAIR_W)); strict2 = np.zeros((2, 2, GC, PAIR_W))
    exp_g = np.zeros((2, 2, 128, PAIR_W)); exp_b = np.zeros((2, 2, 128, PAIR_W))
    for d in range(2):
        t = _tau(GC, d)
        tmask[d, :] = (t[:, None] >= t[None, :])
        tmask2[d, :] = np.tile(t[:, None] >= t[None, :], (1, 2))
        strict2[d, :] = np.tile(t[:, None] > t[None, :], (1, 2))
        for h in range(NH):
            exp_g[d, h // 2, 4 * d + h, (h % 2) * HD:(h % 2 + 1) * HD] = 1.0
            exp_b[d, h // 2, 8 + 4 * d + h, (h % 2) * HD:(h % 2 + 1) * HD] = 1.0
    exp_gt = np.transpose(exp_g, (0, 1, 3, 2))
    per_z = [np.tile(a.reshape((4,) + a.shape[2:]), (nb, 1, 1)) for a in (tmask, tmask2, strict2, exp_g, exp_b, exp_gt)]
    dsel2 = np.tile(np.eye(GC), (1, 2))
    eye2 = np.tile(np.eye(GC), (1, 2))
    bd2 = (np.arange(PAIR_W)[:, None] // HD == np.arange(PAIR_W)[None, :] // HD)
    return [jnp.asarray(a, f32) for a in per_z + [dsel2, eye2, bd2]]


def _swap16(x):
    lane = lax.broadcasted_iota(jnp.int32, x.shape, x.ndim - 1)
    n = x.shape[-1]
    return jnp.where(lane % 32 < 16, pltpu.roll(x, n - 16, axis=x.ndim - 1), pltpu.roll(x, 16, axis=x.ndim - 1))


@jax.custom_vjp
def _rot(x, cos, sins):
    return x * cos + _swap16(x) * sins


def _rot_fwd(x, cos, sins):
    return _rot(x, cos, sins), (cos, sins)


def _rot_bwd(res, g):
    cos, sins = res
    return g * cos + _swap16(g * sins), jnp.zeros_like(cos), jnp.zeros_like(sins)


_rot.defvjp(_rot_fwd, _rot_bwd)


def _silu(z):
    return z * jax.nn.sigmoid(z)


def _head_sum(x, bd):
    return _dot(x, bd, precision=P3)


def _ret_step(s, q, k, v, cos, sins, intra, qdec, kdec, cd, bd2, bdr):
    def bdiag(x):
        return jnp.concatenate([x, x], axis=1) * bdr

    qr = _rot(q, cos, sins)
    kr = _rot(k, cos, sins) * (HD ** -0.5)
    sc = _bmm_nt(qr, bdiag(kr)) * intra
    o = _bmm(qr * qdec, s) + _bmm(sc, bdiag(v))
    s_new = s * cd + bd2 * _bmm_tn(kr * kdec, v)
    return s_new, o


def _ret_finish(o_f, o_b, z, norm_g, bd):
    o = o_f + o_b
    mu = _head_sum(o, bd) * (1.0 / HD)
    xc = o - mu
    var = _head_sum(xc * xc, bd) * (1.0 / HD)
    return xc * lax.rsqrt(var + EPS) * norm_g * _silu(z)


def _softplus(x):
    return jnp.maximum(x, 0.0) + jnp.log(1.0 + jnp.exp(-jnp.abs(x)))


def _bmm(a, b, precision=None):
    return lax.dot_general(a, b, (((2,), (1,)), ((0,), (0,))), precision=precision, preferred_element_type=f32)


def _bmm_nt(a, b, precision=None):
    return lax.dot_general(a, b, (((2,), (2,)), ((0,), (0,))), precision=precision, preferred_element_type=f32)


def _bmm_tn(a, b, precision=None):
    return lax.dot_general(a, b, (((1,), (1,)), ((0,), (0,))), precision=precision, preferred_element_type=f32)


def _bdiag(x, bd2):
    return jnp.concatenate([x, x], axis=1) * bd2


@jax.custom_vjp
def _solve_given_inv(m, vb, kbg, inv, bd2):
    return _bmm(inv, _bdiag(vb, bd2), P3), _bmm(inv, _bdiag(kbg, bd2), P3)


def _solve_fwd(m, vb, kbg, inv, bd2):
    u, w = _solve_given_inv(m, vb, kbg, inv, bd2)
    return (u, w), (inv, u, w, bd2)


def _solve_bwd(res, cts):
    inv, u, w, bd2 = res
    du, dw = cts
    c = inv.shape[1]
    t = jnp.swapaxes(_bdiag(inv, bd2), 1, 2)
    inv_t = t[:, :c] + t[:, c:]
    dvb = _bmm(inv_t, _bdiag(du, bd2), P3)
    dkbg = _bmm(inv_t, _bdiag(dw, bd2), P3)
    dm = _bmm_nt(dvb, _bdiag(u, bd2), P3) + _bmm_nt(dkbg, _bdiag(w, bd2), P3)
    return dm, dvb, dkbg, jnp.zeros_like(inv), jnp.zeros_like(bd2)


_solve_given_inv.defvjp(_solve_fwd, _solve_bwd)


def _gdn_step(s, q, k, v, gate, alog, dtb, tmask, tmask2, strict2, exp_g, exp_b, exp_gt, dsel2, eye2, bd2, inv=None):
    z, c, w_ = q.shape
    ne = gate.shape[0]

    def per_pair(a):
        return jnp.broadcast_to(a[:, None], (ne, z // ne) + a.shape[1:]).reshape((z,) + a.shape[1:])

    def rows(a):
        return a.reshape(z * c, w_)

    def bdiag(x):
        return _bdiag(x, bd2)

    g = per_pair(-jnp.exp(alog) * _softplus(gate + dtb))
    beta = per_pair(jax.nn.sigmoid(gate))
    gl = _bmm(g, exp_g, P3)
    gc_l = _bmm(tmask, gl, P3)
    glast_l = jnp.sum(gl, axis=1, keepdims=True)
    glast = jnp.sum(g, axis=1, keepdims=True)
    beta_l = _bmm(beta, exp_b, P3)
    gc_r = jnp.sum(gc_l * dsel2, axis=1, keepdims=True)
    qn = q * lax.rsqrt(_dot(rows(q * q), bd2, P3).reshape(z, c, w_) + EPS)
    kn = k * lax.rsqrt(_dot(rows(k * k), bd2, P3).reshape(z, c, w_) + EPS)
    eg = jnp.exp(gc_l)
    kb = kn * beta_l
    vb = v * beta_l
    kbg = kb * eg
    qs = qn * (HD ** -0.5)
    dec = jnp.exp(jnp.where(tmask2 > 0, gc_l - gc_r, -1e30))
    kns = bdiag(kn)
    m = -(_bmm_nt(kb, kns) * dec * strict2)
    if inv is None:
        inv = eye2 + m
        p = m
        for _ in range(5):
            p = _bmm(p, bdiag(p), P3)
            inv = inv + _bmm(inv, bdiag(p), P3)
        u = _bmm(inv, bdiag(vb), P3)
        w = _bmm(inv, bdiag(kbg), P3)
    else:
        u, w = _solve_given_inv(m, vb, kbg, inv, bd2)
    v_new = u - _bmm(w, s)
    k_tail = kn * jnp.exp(glast_l - gc_l)
    cdec = jnp.sum(exp_gt * jnp.exp(glast), axis=-1, keepdims=True)
    s_new = s * cdec + bd2 * _bmm_tn(k_tail, v_new)
    a = _bmm_nt(qs, kns) * dec
    o = _bmm(qs * eg, s) + _bmm(a, bdiag(v_new))
    return s_new, o, inv


def _gdn_finish(o_f, o_b, z, norm_g, bd):
    o = o_f + o_b
    ms = _head_sum(o * o, bd) * (1.0 / HD)
    return o * lax.rsqrt(ms + EPS) * norm_g * _silu(z)


def _gelu(x):
    return 0.5 * x * (1.0 + jnp.tanh(0.7978845608028654 * (x + 0.044715 * (x * x * x))))


def _sg_block(u0, u1, v0, v1, z0, z1, w, b, hmp, bdr):
    ts = u0.shape[0]
    nc = ts // RC
    g0, g1 = _gelu(v0), _gelu(v1)
    mu = (jnp.sum(g0, axis=-1, keepdims=True) + jnp.sum(g1, axis=-1, keepdims=True)) * (1.0 / BW)
    x0, x1 = g0 - mu, g1 - mu
    var = (jnp.sum(x0 * x0, axis=-1, keepdims=True) + jnp.sum(x1 * x1, axis=-1, keepdims=True)) * (1.0 / BW)
    rstd = lax.rsqrt(var + EPS)
    ys = []
    for p, (u, xc, z) in enumerate(((u0, x0, z0), (u1, x1, z1))):
        vn = (xc * rstd).reshape(nc, RC, PAIR_W)
        wp = jnp.concatenate([w[2 * p], w[2 * p + 1]], axis=1)
        mix = _bmm(jnp.broadcast_to(wp, (nc, RC, 2 * RC)), jnp.concatenate([vn, vn], axis=1) * bdr)
        bias = _dot_tn(b, hmp[p], precision=HI)
        s = (mix + bias).reshape(ts, PAIR_W)
        ys.append(_gelu(u) * s * _silu(z))
    return ys[0], ys[1]


def _make_shifts(t_ctx, n):
    def dn(x):
        t = lax.broadcasted_iota(jnp.int32, x.shape, 0)
        return jnp.where((t != 0) & (t != t_ctx), pltpu.roll(x, 1, axis=0), 0.0)

    def up(x):
        t = lax.broadcasted_iota(jnp.int32, x.shape, 0)
        return jnp.where((t != t_ctx - 1) & (t != n - 1), pltpu.roll(x, n - 1, axis=0), 0.0)

    @jax.custom_vjp
    def shift_dn(x):
        return dn(x)
    shift_dn.defvjp(lambda x: (dn(x), None), lambda _, g: (up(g),))

    @jax.custom_vjp
    def shift_up(x):
        return up(x)
    shift_up.defvjp(lambda x: (up(x), None), lambda _, g: (dn(g),))
    return shift_dn, shift_up


def _conv3(x, w, shift_dn, shift_up):
    return shift_dn(x) * w[0:1] + x * w[1:2] + shift_up(x) * w[2:3]


def inproj_fwd(x, shift_t, scale_t, g_pre, w_in, n_batch, sb):
    n = x.shape[0]

    def sel(i):
        return jnp.where(i % sb == 0, n_batch, i // sb)

    def body(x_ref, sh0, sh1, sc0, sc1, g_ref, w_ref, p_ref, h_ref):
        hs = []
        for k, (sh_ref, sc_ref) in enumerate(((sh0, sc0), (sh1, sc1))):
            xv = x_ref[k * TM:(k + 1) * TM, :]
            r = xv * lax.rsqrt(jnp.mean(xv * xv, axis=-1, keepdims=True) + EPS)
            hs.append(((r * g_ref[...]) * (1.0 + sc_ref[0]) + sh_ref[0]).astype(bf16))
        hb = jnp.concatenate(hs, axis=0)
        h_ref[...] = hb
        p_ref[...] = _dot(hb, w_ref[...])

    def mrow(k):
        return pl.BlockSpec((1, 1, D), lambda i: (sel(2 * i + k), 0, 0))

    return pl.pallas_call(
        body, name="inproj_fwd", grid=(n // TP,),
        in_specs=[pl.BlockSpec((TP, D), lambda i: (i, 0)), mrow(0), mrow(1), mrow(0), mrow(1),
                  _full((1, D)), _full((D, PW))],
        out_specs=[pl.BlockSpec((TP, PW), lambda i: (i, 0)), pl.BlockSpec((TP, D), lambda i: (i, 0))],
        out_shape=[_sds((n, PW)), _sds((n, D), bf16)],
        compiler_params=_cparams(("arbitrary",), VMEM_BIG),
    )(x, shift_t, shift_t, scale_t, scale_t, g_pre, w_in)


def outproj_fwd(ys, w_out, x, gate_t, g_post, n_batch, sb):
    n = x.shape[0]

    def sel(i):
        return jnp.where(i % sb == 0, n_batch, i // sb)

    def body(y0, y1, y2, y3, w_ref, x_ref, gt0, gt1, g_ref, xn_ref, o_ref):
        y = jnp.concatenate([y0[...], y1[...], y2[...], y3[...]], axis=1)
        o = _dot(y, w_ref[...])
        o_ref[...] = o
        nrm = o * lax.rsqrt(jnp.mean(o * o, axis=-1, keepdims=True) + EPS) * g_ref[...]
        for k, gt_ref in enumerate((gt0, gt1)):
            rows = slice(k * TM, (k + 1) * TM)
            xn_ref[rows, :] = x_ref[rows, :] + gt_ref[0] * nrm[rows]

    def mrow(k):
        return pl.BlockSpec((1, 1, D), lambda i: (sel(2 * i + k), 0, 0))

    yspec = pl.BlockSpec((TP, BW), lambda i: (i, 0))
    return pl.pallas_call(
        body, name="outproj_fwd", grid=(n // TP,),
        in_specs=[yspec, yspec, yspec, yspec, _full((D, D)), pl.BlockSpec((TP, D), lambda i: (i, 0)),
                  mrow(0), mrow(1), _full((1, D))],
        out_specs=[pl.BlockSpec((TP, D), lambda i: (i, 0)), pl.BlockSpec((TP, D), lambda i: (i, 0))],
        out_shape=[_sds((n, D)), _sds((n, D))],
        compiler_params=_cparams(("arbitrary",), VMEM_BIG),
    )(*ys, w_out, x, gate_t, gate_t, g_post)


def _row_onehot(r):
    return lax.broadcasted_iota(jnp.int32, (8, 1), 0) == r


def outproj_bwd(dxn, o, gate_t, g_post, ys, w_out, n_batch, sb):
    n = dxn.shape[0]

    def sel(i):
        return jnp.where(i % sb == 0, n_batch, i // sb)

    def body(dxn_ref, o_ref, gt0, gt1, g_ref, y0, y1, y2, y3, w_ref, dy_ref, dw_ref, dg_ref, dgate_ref):
        i = pl.program_id(0)

        @pl.when(i == 0)
        def _():
            dw_ref[...] = jnp.zeros_like(dw_ref)
            dg_ref[...] = jnp.zeros_like(dg_ref)
            dgate_ref[...] = jnp.zeros_like(dgate_ref)

        g = g_ref[...]
        dos = []
        for k, gt_ref in enumerate((gt0, gt1)):
            rows = slice(k * TM, (k + 1) * TM)
            ov = o_ref[rows, :]
            rstd = lax.rsqrt(jnp.mean(ov * ov, axis=-1, keepdims=True) + EPS)
            r = ov * rstd
            dx = dxn_ref[rows, :]
            dgate_ref[...] += jnp.where(_row_onehot(sel(2 * i + k)), jnp.sum(dx * (r * g), axis=0, keepdims=True), 0.0)
            dn = dx * gt_ref[0]
            dg_ref[...] += jnp.sum(dn * r, axis=0, keepdims=True)
            dr = dn * g
            dos.append((rstd * (dr - r * jnp.mean(dr * r, axis=-1, keepdims=True))).astype(bf16))
        dob = jnp.concatenate(dos, axis=0)
        dy_ref[...] = _dot_nt(dob, w_ref[...])
        y = jnp.concatenate([y0[...], y1[...], y2[...], y3[...]], axis=1)
        dw_ref[...] += _dot_tn(y, dob)

    def mrow(k):
        return pl.BlockSpec((1, 1, D), lambda i: (sel(2 * i + k), 0, 0))

    yspec = pl.BlockSpec((TP, BW), lambda i: (i, 0))
    row = pl.BlockSpec((TP, D), lambda i: (i, 0))
    return pl.pallas_call(
        body, name="outproj_bwd", grid=(n // TP,),
        in_specs=[row, row, mrow(0), mrow(1), _full((1, D)), yspec, yspec, yspec, yspec, _full((D, D))],
        out_specs=[row, _full((D, D)), _full((1, D)), _full((8, D))],
        out_shape=[_sds((n, D)), _sds((D, D)), _sds((1, D)), _sds((8, D))],
        compiler_params=_cparams(("arbitrary",), VMEM_BIG),
    )(dxn, o, gate_t, gate_t, g_post, *ys, w_out)


def inproj_bwd_x(dp, w_in, x, scale_t, g_pre, dxn, n_batch, sb):
    n = x.shape[0]

    def sel(i):
        return jnp.where(i % sb == 0, n_batch, i // sb)

    def body(dp_ref, w_ref, x_ref, sc0, sc1, g_ref, dxn_ref, dx_ref, dg_ref, dsh_ref, dsc_ref):
        i = pl.program_id(0)

        @pl.when(i == 0)
        def _():
            dg_ref[...] = jnp.zeros_like(dg_ref)
            dsh_ref[...] = jnp.zeros_like(dsh_ref)
            dsc_ref[...] = jnp.zeros_like(dsc_ref)

        dh_all = _dot_nt(dp_ref[...], w_ref[...])
        g = g_ref[...]
        for k, sc_ref in enumerate((sc0, sc1)):
            rows = slice(k * TM, (k + 1) * TM)
            dh = dh_all[rows]
            xv = x_ref[rows, :]
            rstd = lax.rsqrt(jnp.mean(xv * xv, axis=-1, keepdims=True) + EPS)
            r = xv * rstd
            hot = _row_onehot(sel(2 * i + k))
            dsh_ref[...] += jnp.where(hot, jnp.sum(dh, axis=0, keepdims=True), 0.0)
            dsc_ref[...] += jnp.where(hot, jnp.sum(dh * (r * g), axis=0, keepdims=True), 0.0)
            t = dh * (1.0 + sc_ref[0])
            dg_ref[...] += jnp.sum(t * r, axis=0, keepdims=True)
            dr = t * g
            dx_ref[rows, :] = dxn_ref[rows, :] + rstd * (dr - r * jnp.mean(dr * r, axis=-1, keepdims=True))

    def mrow(k):
        return pl.BlockSpec((1, 1, D), lambda i: (sel(2 * i + k), 0, 0))

    row = pl.BlockSpec((TP, D), lambda i: (i, 0))
    return pl.pallas_call(
        body, name="inproj_bwd_x", grid=(n // TP,),
        in_specs=[pl.BlockSpec((TP, PW), lambda i: (i, 0)), _full((D, PW)), row, mrow(0), mrow(1), _full((1, D)), row],
        out_specs=[row, _full((1, D)), _full((8, D)), _full((8, D))],
        out_shape=[_sds((n, D)), _sds((1, D)), _sds((8, D)), _sds((8, D))],
        compiler_params=_cparams(("arbitrary",), VMEM_BIG),
    )(dp, w_in, x, scale_t, scale_t, g_pre, dxn)


def dw_in(h, dp):
    n = h.shape[0]
    tk, tn = (1536 if n % 1536 == 0 else 512), 1024
    nk = n // tk

    def body(h_ref, dp_ref, o_ref, acc):
        k = pl.program_id(1)

        @pl.when(k == 0)
        def _():
            acc[...] = jnp.zeros_like(acc)
        acc[...] += _dot_tn(dp_ref[...], h_ref[...])

        @pl.when(k == nk - 1)
        def _():
            o_ref[...] = acc[...].astype(bf16)

    return pl.pallas_call(
        body, name="dw_in", grid=(PW // tn, nk),
        in_specs=[pl.BlockSpec((tk, D), lambda j, k: (k, 0)), pl.BlockSpec((tk, tn), lambda j, k: (k, j))],
        out_specs=pl.BlockSpec((tn, D), lambda j, k: (j, 0)),
        out_shape=_sds((PW, D), bf16),
        scratch_shapes=[pltpu.VMEM((tn, D), f32)],
        compiler_params=_cparams(("parallel", "arbitrary"), VMEM_BIG),
    )(h, dp)


def place_weights(slabs):
    n_ch, d, wc = slabs.shape

    def body(w_ref, o_ref):
        acc = jnp.pad(w_ref[0].astype(f32), ((0, 0), (0, PW - wc)))
        for k in range(1, n_ch):
            acc = acc + pltpu.roll(jnp.pad(w_ref[k].astype(f32), ((0, 0), (0, PW - wc))), wc * k, axis=1)
        o_ref[...] = acc.astype(bf16)

    return pl.pallas_call(
        body, name="place_weights", grid=(d // TM,),
        in_specs=[pl.BlockSpec((n_ch, TM, wc), lambda i: (0, i, 0))],
        out_specs=pl.BlockSpec((TM, PW), lambda i: (i, 0)),
        out_shape=_sds((d, PW), bf16),
        compiler_params=_cparams(("arbitrary",), VMEM_BIG),
    )(slabs)


def loss_head(xf, target, t_ctx):
    nb, s, _ = xf.shape
    jc = t_ctx // TM

    def body(x_ref, t_ref, dx_ref, l_ref):
        b, j = pl.program_id(0), pl.program_id(1)

        @pl.when((b == 0) & (j == 0))
        def _():
            l_ref[...] = jnp.zeros_like(l_ref)

        @pl.when(j < jc)
        def _():
            dx_ref[...] = jnp.zeros_like(dx_ref)

        @pl.when(j >= jc)
        def _():
            diff = x_ref[0] - t_ref[0]
            dx_ref[0] = diff * (1.0 / D)
            l_ref[...] += 0.5 * jnp.sum(diff * diff) * (1.0 / D)

    return pl.pallas_call(
        body, name="loss_head", grid=(nb, s // TM),
        in_specs=[pl.BlockSpec((1, TM, D), lambda b, j: (b, j, 0)),
                  pl.BlockSpec((1, TM, D), lambda b, j: (b, jnp.maximum(j - jc, 0), 0))],
        out_specs=[pl.BlockSpec((1, TM, D), lambda b, j: (b, j, 0)), _full((1, 128))],
        out_shape=[_sds((nb, s, D)), _sds((1, 128))],
        compiler_params=_cparams(("arbitrary", "arbitrary")),
    )(xf, target)


def _chunk_maps(n_ctx, n_lat):
    n = n_ctx + n_lat

    def cf(t):
        return t

    def cb(t):
        return jnp.where(t < n_ctx, n_ctx - 1 - t, n - 1 - t + n_ctx)
    return n, cf, cb


def ret_scan_fwd(p3, cos, sins, consts, t_ctx):
    nb, s, _ = p3.shape
    n, cf, cb = _chunk_maps(t_ctx // RC, (s - t_ctx) // RC)
    nz = 4 * nb

    def body(qf, kf, vf, qb, kb, vb, cosf, sinf, cosb, sinb, intra_r, qdec_r, kdec_r, cd_r, bd_r, bdr_r,
             of_ref, ob_ref, sall_ref, s_sc):
        @pl.when(pl.program_id(0) == 0)
        def _():
            s_sc[...] = jnp.zeros_like(s_sc)
        st = s_sc[...]
        sall_ref[0] = st
        s_new, o = _ret_step(st, _pairs(qf, qb, nb), _pairs(kf, kb, nb), _pairs(vf, vb, nb),
                             _pair_tables(cosf, cosb, nb), _pair_tables(sinf, sinb, nb), intra_r[...], qdec_r[...],
                             kdec_r[...], cd_r[...], bd_r[...], bdr_r[...])
        s_sc[...] = s_new
        _unpairs(o, of_ref, ob_ref, nb)

    def pspec(m, seg):
        return pl.BlockSpec((nb, RC, BW), lambda t: (0, m(t), seg))

    def tspec(m):
        return pl.BlockSpec((RC, PAIR_W), lambda t: (m(t), 0))

    return pl.pallas_call(
        body, name="ret_scan_fwd", grid=(n,),
        in_specs=[pspec(cf, 0), pspec(cf, 1), pspec(cf, 2), pspec(cb, 0), pspec(cb, 1), pspec(cb, 2),
                  tspec(cf), tspec(cf), tspec(cb), tspec(cb)] + [_full(c.shape) for c in consts],
        out_specs=[pl.BlockSpec((nb, RC, BW), lambda t: (0, cf(t), 0)),
                   pl.BlockSpec((nb, RC, BW), lambda t: (0, cb(t), 0)),
                   pl.BlockSpec((1, nz, PAIR_W, PAIR_W), lambda t: (t, 0, 0, 0))],
        out_shape=[_sds((nb, s, BW)), _sds((nb, s, BW)), _sds((n, nz, PAIR_W, PAIR_W))],
        scratch_shapes=[pltpu.VMEM((nz, PAIR_W, PAIR_W), f32)],
        compiler_params=_cparams(("arbitrary",)),
    )(p3, p3, p3, p3, p3, p3, cos, sins, cos, sins, *consts)


def ret_scan_bwd(p3, cos, sins, consts, s_all, do, t_ctx):
    nb, s, _ = p3.shape
    n, cf, cb = _chunk_maps(t_ctx // RC, (s - t_ctx) // RC)
    nz = 4 * nb

    def rf(t):
        return cf(n - 1 - t)

    def rb(t):
        return cb(n - 1 - t)

    def body(qf, kf, vf, qb, kb, vb, cosf, sinf, cosb, sinb, intra_r, qdec_r, kdec_r, cd_r, bd_r, bdr_r,
             sall_ref, dof, dob, dqf, dkf, dvf, dqb, dkb, dvb, ds_sc):
        @pl.when(pl.program_id(0) == 0)
        def _():
            ds_sc[...] = jnp.zeros_like(ds_sc)
        step = functools.partial(_ret_step, cos=_pair_tables(cosf, cosb, nb), sins=_pair_tables(sinf, sinb, nb),
                                 intra=intra_r[...], qdec=qdec_r[...], kdec=kdec_r[...], cd=cd_r[...], bd2=bd_r[...],
                                 bdr=bdr_r[...])
        _, vjp = jax.vjp(step, sall_ref[0], _pairs(qf, qb, nb), _pairs(kf, kb, nb), _pairs(vf, vb, nb))
        ds, dq, dk, dv = vjp((ds_sc[...], _pairs(dof, dob, nb)))
        ds_sc[...] = ds
        _unpairs(dq, dqf, dqb, nb)
        _unpairs(dk, dkf, dkb, nb)
        _unpairs(dv, dvf, dvb, nb)

    def pspec(m, seg):
        return pl.BlockSpec((nb, RC, BW), lambda t: (0, m(t), seg))

    def tspec(m):
        return pl.BlockSpec((RC, PAIR_W), lambda t: (m(t), 0))

    def ospec(m):
        return pl.BlockSpec((nb, RC, BW), lambda t: (0, m(t), 0))

    return pl.pallas_call(
        body, name="ret_scan_bwd", grid=(n,),
        in_specs=[pspec(rf, 0), pspec(rf, 1), pspec(rf, 2), pspec(rb, 0), pspec(rb, 1), pspec(rb, 2),
                  tspec(rf), tspec(rf), tspec(rb), tspec(rb)] + [_full(c.shape) for c in consts]
                 + [pl.BlockSpec((1, nz, PAIR_W, PAIR_W), lambda t: (n - 1 - t, 0, 0, 0)), ospec(rf), ospec(rb)],
        out_specs=[ospec(rf), ospec(rf), ospec(rf), ospec(rb), ospec(rb), ospec(rb)],
        out_shape=[_sds((nb, s, BW), bf16)] * 6,
        scratch_shapes=[pltpu.VMEM((nz, PAIR_W, PAIR_W), f32)],
        compiler_params=_cparams(("arbitrary",), VMEM_BIG),
    )(p3, p3, p3, p3, p3, p3, cos, sins, cos, sins, *consts, s_all, do, do)


def mix_finish_fwd(fn, name, o_f, o_b, p3, zseg, norm_g, bd):
    nb, s, _ = p3.shape

    def body(of_ref, ob_ref, z_ref, g_ref, bd_ref, y_ref):
        y_ref[0] = fn(of_ref[0], ob_ref[0], z_ref[0], g_ref[...], bd_ref[...]).astype(bf16)

    blk = pl.BlockSpec((1, TM, BW), lambda b, j: (b, j, 0))
    return pl.pallas_call(
        body, name=name, grid=(nb, s // TM),
        in_specs=[blk, blk, pl.BlockSpec((1, TM, BW), lambda b, j: (b, j, zseg)), _full((1, BW)), _full((BW, BW))],
        out_specs=blk, out_shape=_sds((nb, s, BW), bf16),
        compiler_params=_cparams(("arbitrary", "arbitrary")),
    )(o_f, o_b, p3, norm_g, bd)


def mix_finish_bwd(fn, name, o_f, o_b, p3, zseg, norm_g, bd, dy3, yseg):
    nb, s, _ = p3.shape

    def body(of_ref, ob_ref, z_ref, g_ref, bd_ref, dy_ref, do_ref, dz_ref, dg_ref):
        @pl.when((pl.program_id(0) == 0) & (pl.program_id(1) == 0))
        def _():
            dg_ref[...] = jnp.zeros_like(dg_ref)
        bdv = bd_ref[...]
        _, vjp = jax.vjp(lambda a, b, z, g: fn(a, b, z, g, bdv), of_ref[0], ob_ref[0], z_ref[0], g_ref[...])
        do, _, dz, dg = vjp(dy_ref[0])
        do_ref[0] = do
        dz_ref[0] = dz.astype(bf16)
        dg_ref[...] += dg

    blk = pl.BlockSpec((1, TM, BW), lambda b, j: (b, j, 0))
    return pl.pallas_call(
        body, name=name, grid=(nb, s // TM),
        in_specs=[blk, blk, pl.BlockSpec((1, TM, BW), lambda b, j: (b, j, zseg)), _full((1, BW)), _full((BW, BW)),
                  pl.BlockSpec((1, TM, BW), lambda b, j: (b, j, yseg))],
        out_specs=[blk, blk, _full((1, BW))],
        out_shape=[_sds((nb, s, BW)), _sds((nb, s, BW), bf16), _sds((1, BW))],
        compiler_params=_cparams(("arbitrary", "arbitrary")),
    )(o_f, o_b, p3, norm_g, bd, dy3)


def gdn_conv_fwd(p3, w, seg, t_ctx):
    nb, s, _ = p3.shape
    sd, su = _make_shifts(t_ctx, s)

    def body(x_ref, w_ref, o_ref):
        o_ref[0] = _silu(_conv3(x_ref[0], w_ref[...], sd, su))

    return pl.pallas_call(
        body, name="gdn_conv_fwd", grid=(nb, 2),
        in_specs=[pl.BlockSpec((1, s, 128), lambda b, j: (b, 0, 2 * seg + j)), pl.BlockSpec((3, 128), lambda b, j: (0, j))],
        out_specs=pl.BlockSpec((1, s, 128), lambda b, j: (b, 0, j)),
        out_shape=_sds((nb, s, BW)),
        compiler_params=_cparams(("arbitrary", "arbitrary")),
    )(p3, w)


def gdn_conv_bwd(p3, w, seg, d_f, d_b, t_ctx):
    nb, s, _ = p3.shape
    sd, su = _make_shifts(t_ctx, s)

    def body(x_ref, w_ref, df_ref, db_ref, dx_ref, dw_ref):
        @pl.when(pl.program_id(1) == 0)
        def _():
            dw_ref[...] = jnp.zeros_like(dw_ref)
        _, vjp = jax.vjp(lambda x, w_: _silu(_conv3(x, w_, sd, su)), x_ref[0], w_ref[...])
        dx, dw = vjp(df_ref[0] + db_ref[0])
        dx_ref[0] = dx.astype(bf16)
        dw_ref[...] += dw

    blk = pl.BlockSpec((1, s, 128), lambda j, b: (b, 0, j))
    return pl.pallas_call(
        body, name="gdn_conv_bwd", grid=(2, nb),
        in_specs=[pl.BlockSpec((1, s, 128), lambda j, b: (b, 0, 2 * seg + j)), pl.BlockSpec((3, 128), lambda j, b: (0, j)),
                  blk, blk],
        out_specs=[blk, pl.BlockSpec((3, 128), lambda j, b: (0, j))],
        out_shape=[_sds((nb, s, BW), bf16), _sds((3, BW))],
        compiler_params=_cparams(("arbitrary", "arbitrary"), VMEM_BIG),
    )(p3, w, d_f, d_b)


def _pairs(f_ref, b_ref, nb):
    return jnp.stack([r[b, :, PAIR_W * p:PAIR_W * (p + 1)] for b in range(nb) for r in (f_ref, b_ref) for p in range(2)])


def _pair_tables(f_ref, b_ref, nb):
    return jnp.stack([r[...] for _ in range(nb) for r in (f_ref, b_ref) for _ in range(2)])


def _gates(f_ref, b_ref, nb):
    return jnp.stack([r[b] for b in range(nb) for r in (f_ref, b_ref)])


def _unpairs(a, f_ref, b_ref, nb):
    for b in range(nb):
        for d, r in enumerate((f_ref, b_ref)):
            for p in range(2):
                r[b, :, PAIR_W * p:PAIR_W * (p + 1)] = a[4 * b + 2 * d + p].astype(r.dtype)


def _with_exchange(body, n_in, n_out, n_scratch, xchg, n_steps):
    if xchg is None:
        return body, [], [], [], []
    kind, arrs = xchg
    nx = len(arrs)

    def fused(*refs):
        ins, rest = refs[:n_in], refs[n_in:]
        srcs, rest = rest[:nx], rest[nx:]
        outs, rest = rest[:n_out], rest[n_out:]
        dsts, rest = rest[:nx], rest[nx:]
        scratch, sems = rest[:n_scratch], rest[n_scratch:]
        start, wait = _peer_exchange(kind, "chips", srcs, dsts, *sems)
        pl.when(pl.program_id(0) == 0)(start)
        body(*ins, *outs, *scratch)
        pl.when(pl.program_id(0) == n_steps - 1)(wait)

    any_ = pl.BlockSpec(memory_space=pl.ANY)
    return fused, [any_] * nx, [any_] * nx, _exchange_shapes(kind, "chips", arrs), _exchange_scratch("chips", nx)


def gdn_scan_fwd(cq, ck, cv, p3, alog, dtb, consts, t_ctx, xchg=None):
    nb, s, _ = p3.shape
    n, cf, cb = _chunk_maps(t_ctx // GC, (s - t_ctx) // GC)
    gblk = GATE_COL // 128

    nz = 4 * nb

    def body(qf, kf, vf, gf, qb, kb, vb, gb, al_ref, dt_ref, tm_r, tm2_r, st2_r, eg_r, eb_r, egt_r, dsel_r, eye_r, bd_r,
             of_ref, ob_ref, sall_ref, inv_ref, s_sc):
        @pl.when(pl.program_id(0) == 0)
        def _():
            s_sc[...] = jnp.zeros_like(s_sc)
        st = s_sc[...]
        sall_ref[0] = st
        s_new, o, inv = _gdn_step(st, _pairs(qf, qb, nb), _pairs(kf, kb, nb), _pairs(vf, vb, nb), _gates(gf, gb, nb),
                                  al_ref[...], dt_ref[...], tm_r[...], tm2_r[...], st2_r[...], eg_r[...], eb_r[...],
                                  egt_r[...], dsel_r[...], eye_r[...], bd_r[...])
        s_sc[...] = s_new
        inv_ref[0] = inv
        _unpairs(o, of_ref, ob_ref, nb)

    def cspec(m):
        return pl.BlockSpec((nb, GC, BW), lambda t: (0, m(t), 0))

    def gspec(m):
        return pl.BlockSpec((nb, GC, 128), lambda t: (0, m(t), gblk))

    fused, x_in, x_out, x_shape, x_scratch = _with_exchange(body, 10 + len(consts), 4, 1, xchg, n)
    return pl.pallas_call(
        fused, name="gdn_scan_fwd" + ("" if xchg is None else "_" + xchg[0]), grid=(n,),
        in_specs=[cspec(cf), cspec(cf), cspec(cf), gspec(cf), cspec(cb), cspec(cb), cspec(cb), gspec(cb),
                  _full((1, 128)), _full((1, 128))] + [_full(c.shape) for c in consts] + x_in,
        out_specs=[cspec(cf), cspec(cb), pl.BlockSpec((1, nz, PAIR_W, PAIR_W), lambda t: (t, 0, 0, 0)),
                   pl.BlockSpec((1, nz, GC, PAIR_W), lambda t: (t, 0, 0, 0))] + x_out,
        out_shape=[_sds((nb, s, BW)), _sds((nb, s, BW)), _sds((n, nz, PAIR_W, PAIR_W)), _sds((n, nz, GC, PAIR_W))]
                  + x_shape,
        scratch_shapes=[pltpu.VMEM((nz, PAIR_W, PAIR_W), f32)] + x_scratch,
        compiler_params=_cparams(("arbitrary",)),
    )(cq, ck, cv, p3, cq, ck, cv, p3, alog, dtb, *consts, *([] if xchg is None else xchg[1]))


def gdn_scan_bwd(cq, ck, cv, p3, alog, dtb, consts, s_all, inv_all, do, t_ctx, xchg=None):
    nb, s, _ = p3.shape
    n, cf, cb = _chunk_maps(t_ctx // GC, (s - t_ctx) // GC)
    gblk = GATE_COL // 128

    def rf(t):
        return cf(n - 1 - t)

    def rb(t):
        return cb(n - 1 - t)

    nz = 4 * nb

    def body(qf, kf, vf, gf, qb, kb, vb, gb, al_ref, dt_ref, tm_r, tm2_r, st2_r, eg_r, eb_r, egt_r, dsel_r, eye_r, bd_r,
             sall_ref, inv_ref, dof, dob, dqf, dkf, dvf, dgf, dqb, dkb, dvb, dgb, dal_ref, ddt_ref, ds_sc):
        @pl.when(pl.program_id(0) == 0)
        def _():
            dal_ref[...] = jnp.zeros_like(dal_ref)
            ddt_ref[...] = jnp.zeros_like(ddt_ref)
            ds_sc[...] = jnp.zeros_like(ds_sc)
        consts = dict(tmask=tm_r[...], tmask2=tm2_r[...], strict2=st2_r[...], exp_g=eg_r[...], exp_b=eb_r[...],
                      exp_gt=egt_r[...], dsel2=dsel_r[...], eye2=eye_r[...], bd2=bd_r[...], inv=inv_ref[0])

        def step(*a):
            return _gdn_step(*a, **consts)[:2]

        _, vjp = jax.vjp(step, sall_ref[0], _pairs(qf, qb, nb), _pairs(kf, kb, nb), _pairs(vf, vb, nb),
                         _gates(gf, gb, nb), al_ref[...], dt_ref[...])
        ds, dq, dk, dv, dg, dal, ddt = vjp((ds_sc[...], _pairs(dof, dob, nb)))
        ds_sc[...] = ds
        _unpairs(dq, dqf, dqb, nb)
        _unpairs(dk, dkf, dkb, nb)
        _unpairs(dv, dvf, dvb, nb)
        for b in range(nb):
            dgf[b] = dg[2 * b].astype(bf16)
            dgb[b] = dg[2 * b + 1].astype(bf16)
        dal_ref[...] += dal
        ddt_ref[...] += ddt

    def cspec(m):
        return pl.BlockSpec((nb, GC, BW), lambda t: (0, m(t), 0))

    def gspec(m):
        return pl.BlockSpec((nb, GC, 128), lambda t: (0, m(t), gblk))

    def gout(m):
        return pl.BlockSpec((nb, GC, 128), lambda t: (0, m(t), 0))

    fused, x_in, x_out, x_shape, x_scratch = _with_exchange(body, 14 + len(consts), 10, 1, xchg, n)
    return pl.pallas_call(
        fused, name="gdn_scan_bwd" + ("" if xchg is None else "_" + xchg[0]), grid=(n,),
        in_specs=[cspec(rf), cspec(rf), cspec(rf), gspec(rf), cspec(rb), cspec(rb), cspec(rb), gspec(rb),
                  _full((1, 128)), _full((1, 128))] + [_full(c.shape) for c in consts]
                 + [pl.BlockSpec((1, nz, PAIR_W, PAIR_W), lambda t: (n - 1 - t, 0, 0, 0)),
                    pl.BlockSpec((1, nz, GC, PAIR_W), lambda t: (n - 1 - t, 0, 0, 0)), cspec(rf), cspec(rb)] + x_in,
        out_specs=[cspec(rf), cspec(rf), cspec(rf), gout(rf), cspec(rb), cspec(rb), cspec(rb), gout(rb),
                   _full((1, 128)), _full((1, 128))] + x_out,
        out_shape=[_sds((nb, s, BW))] * 3 + [_sds((nb, s, 128), bf16)] + [_sds((nb, s, BW))] * 3 + [_sds((nb, s, 128), bf16)]
                  + [_sds((1, 128)), _sds((1, 128))] + x_shape,
        scratch_shapes=[pltpu.VMEM((nz, PAIR_W, PAIR_W), f32)] + x_scratch,
        compiler_params=_cparams(("arbitrary",), VMEM_BIG),
    )(cq, ck, cv, p3, cq, ck, cv, p3, alog, dtb, *consts, s_all, inv_all, do, do, *([] if xchg is None else xchg[1]))


def _sg_consts():
    hmp = np.zeros((2, NH, PAIR_W))
    for h in range(NH):
        hmp[h // 2, h, (h % 2) * HD:(h % 2 + 1) * HD] = 1.0
    bdr = (np.arange(2 * RC)[:, None] // RC == np.arange(PAIR_W)[None, :] // HD)
    return jnp.asarray(hmp, f32), jnp.asarray(bdr, f32)


def _sg_rows(s):
    return 6 * RC if s % (6 * RC) == 0 else 2 * RC


def _halves(ref):
    return ref[0, :, :PAIR_W], ref[0, :, PAIR_W:]


def sg_fwd(p3, w, b, hmp, bdr):
    nb, s, _ = p3.shape
    ts = _sg_rows(s)

    def body(u_ref, v_ref, z_ref, w_ref, b_ref, hm_ref, bdr_ref, y_ref):
        y0, y1 = _sg_block(*_halves(u_ref), *_halves(v_ref), *_halves(z_ref), w_ref[...], b_ref[...], hm_ref[...],
                           bdr_ref[...])
        y_ref[0, :, :PAIR_W] = y0.astype(bf16)
        y_ref[0, :, PAIR_W:] = y1.astype(bf16)

    def seg(k):
        return pl.BlockSpec((1, ts, BW), lambda bi, i: (bi, i, k))

    return pl.pallas_call(
        body, name="sg_fwd", grid=(nb, s // ts),
        in_specs=[seg(4), seg(5), seg(6), _full((NH, RC, RC)), _full((NH, RC)), _full(hmp.shape), _full(bdr.shape)],
        out_specs=pl.BlockSpec((1, ts, BW), lambda bi, i: (bi, i, 0)),
        out_shape=_sds((nb, s, BW), bf16),
        compiler_params=_cparams(("arbitrary", "arbitrary")),
    )(p3, p3, p3, w, b, hmp, bdr)


def sg_bwd(p3, w, b, hmp, bdr, dy3):
    nb, s, _ = p3.shape
    ts = _sg_rows(s)

    def body(u_ref, v_ref, z_ref, w_ref, b_ref, hm_ref, bdr_ref, dy_ref, du_ref, dv_ref, dz_ref, dw_ref, db_ref):
        @pl.when((pl.program_id(0) == 0) & (pl.program_id(1) == 0))
        def _():
            dw_ref[...] = jnp.zeros_like(dw_ref)
            db_ref[...] = jnp.zeros_like(db_ref)
        hm, bdr_v = hm_ref[...], bdr_ref[...]
        _, vjp = jax.vjp(lambda *a: _sg_block(*a, hm, bdr_v), *_halves(u_ref), *_halves(v_ref), *_halves(z_ref),
                         w_ref[...], b_ref[...])
        du0, du1, dv0, dv1, dz0, dz1, dw, db = vjp(_halves(dy_ref))
        for ref, a0, a1 in ((du_ref, du0, du1), (dv_ref, dv0, dv1), (dz_ref, dz0, dz1)):
            ref[0, :, :PAIR_W] = a0.astype(bf16)
            ref[0, :, PAIR_W:] = a1.astype(bf16)
        dw_ref[...] += dw
        db_ref[...] += db

    def seg(k):
        return pl.BlockSpec((1, ts, BW), lambda bi, i: (bi, i, k))

    blk = pl.BlockSpec((1, ts, BW), lambda bi, i: (bi, i, 0))
    return pl.pallas_call(
        body, name="sg_bwd", grid=(nb, s // ts),
        in_specs=[seg(4), seg(5), seg(6), _full((NH, RC, RC)), _full((NH, RC)), _full(hmp.shape), _full(bdr.shape),
                  seg(1)],
        out_specs=[blk, blk, blk, _full((NH, RC, RC)), _full((NH, RC))],
        out_shape=[_sds((nb, s, BW), bf16)] * 3 + [_sds((NH, RC, RC)), _sds((NH, RC))],
        compiler_params=_cparams(("arbitrary", "arbitrary"), VMEM_BIG),
    )(p3, p3, p3, w, b, hmp, bdr, dy3)


def _sc_fn(b, c, h, z, w, sd, su):
    return b * _conv3(c * h, w, sd, su) * _silu(z)


def sc_fwd(p3, w, t_ctx):
    nb, s, _ = p3.shape
    sd, su = _make_shifts(t_ctx, s)

    def body(b_ref, c_ref, h_ref, z_ref, w_ref, y_ref):
        y_ref[0] = _sc_fn(b_ref[0], c_ref[0], h_ref[0], z_ref[0], w_ref[...], sd, su).astype(bf16)

    def seg(k):
        return pl.BlockSpec((1, s, 128), lambda bi, j: (bi, 0, 2 * k + j))

    return pl.pallas_call(
        body, name="sc_fwd", grid=(nb, 2),
        in_specs=[seg(7), seg(8), seg(9), seg(10), pl.BlockSpec((3, 128), lambda bi, j: (0, j))],
        out_specs=pl.BlockSpec((1, s, 128), lambda bi, j: (bi, 0, j)),
        out_shape=_sds((nb, s, BW), bf16),
        compiler_params=_cparams(("arbitrary", "arbitrary"), VMEM_BIG),
    )(p3, p3, p3, p3, w)


def sc_bwd(p3, w, dy3, t_ctx):
    nb, s, _ = p3.shape
    sd, su = _make_shifts(t_ctx, s)

    def body(b_ref, c_ref, h_ref, z_ref, w_ref, dy_ref, db_ref, dc_ref, dh_ref, dz_ref, dw_ref):
        @pl.when(pl.program_id(1) == 0)
        def _():
            dw_ref[...] = jnp.zeros_like(dw_ref)
        _, vjp = jax.vjp(lambda b, c, h, z, w_: _sc_fn(b, c, h, z, w_, sd, su),
                         b_ref[0], c_ref[0], h_ref[0], z_ref[0], w_ref[...])
        db, dc, dh, dz, dw = vjp(dy_ref[0])
        db_ref[0] = db.astype(bf16)
        dc_ref[0] = dc.astype(bf16)
        dh_ref[0] = dh.astype(bf16)
        dz_ref[0] = dz.astype(bf16)
        dw_ref[...] += dw

    def seg(k):
        return pl.BlockSpec((1, s, 128), lambda j, bi: (bi, 0, 2 * k + j))

    blk = pl.BlockSpec((1, s, 128), lambda j, bi: (bi, 0, j))
    wspec = pl.BlockSpec((3, 128), lambda j, bi: (0, j))
    return pl.pallas_call(
        body, name="sc_bwd", grid=(2, nb),
        in_specs=[seg(7), seg(8), seg(9), seg(10), wspec, seg(2)],
        out_specs=[blk, blk, blk, blk, wspec],
        out_shape=[_sds((nb, s, BW), bf16)] * 4 + [_sds((3, BW))],
        compiler_params=_cparams(("arbitrary", "arbitrary"), VMEM_BIG),
    )(p3, p3, p3, p3, w, dy3)


def assemble_dp(pairs, singles_a, gdn_x, singles_b, gates):
    nb, s, _ = singles_a[0].shape
    flat = [a for pr in pairs for a in pr] + list(singles_a) + list(gdn_x) + list(singles_b) + list(gates)
    n_pairs, n_a, n_x, n_b = len(pairs), len(singles_a), len(gdn_x), len(singles_b)

    def body(*refs):
        out = refs[-1]
        ins = refs[:-1]
        col = 0
        for p in range(n_pairs):
            out[0, :, col:col + BW] = (ins[2 * p][0].astype(f32) + ins[2 * p + 1][0].astype(f32)).astype(bf16)
            col += BW
        k = 2 * n_pairs
        for _ in range(n_a + n_x + n_b):
            out[0, :, col:col + BW] = ins[k][0]
            col += BW
            k += 1
        out[0, :, col:col + 128] = (ins[k][0].astype(f32) + ins[k + 1][0].astype(f32)).astype(bf16)
        out[0, :, col + 128:] = jnp.zeros((TM, PW - col - 128), bf16)

    def spec(a):
        return pl.BlockSpec((1, TM, a.shape[-1]), lambda b, j: (b, j, 0))

    return pl.pallas_call(
        body, name="assemble_dp", grid=(nb, s // TM),
        in_specs=[spec(a) for a in flat],
        out_specs=pl.BlockSpec((1, TM, PW), lambda b, j: (b, j, 0)),
        out_shape=_sds((nb, s, PW), bf16),
        compiler_params=_cparams(("arbitrary", "arbitrary")),
    )(*flat)


def mod_fwd(c_rows, w_mod, b_cols):
    nl, _, wc = w_mod.shape
    nr = c_rows.shape[0]

    def body(c_ref, w_ref, b_ref, o_ref):
        o_ref[0] = _dot(_silu(c_ref[...]), w_ref[0], precision=HI) + b_ref[0]

    return pl.pallas_call(
        body, name="mod_fwd", grid=(nl,),
        in_specs=[_full((nr, D)), pl.BlockSpec((1, D, wc), lambda l: (l, 0, 0)), pl.BlockSpec((1, 1, wc), lambda l: (l, 0, 0))],
        out_specs=pl.BlockSpec((1, nr, wc), lambda l: (l, 0, 0)),
        out_shape=_sds((nl, nr, wc)),
        compiler_params=_cparams(("arbitrary",)),
    )(c_rows, w_mod, b_cols)


def mod_bwd(c_rows, w_mod, dm_cols, dm_full):
    nl, _, wc = w_mod.shape
    nr = c_rows.shape[0]

    def body(c_ref, w_ref, dmc_ref, dmf_ref, gw_ref, gb_ref, dcc_ref):
        @pl.when(pl.program_id(0) == 0)
        def _():
            dcc_ref[...] = jnp.zeros_like(dcc_ref)
        a = _silu(c_ref[...])
        dmc = dmc_ref[0]
        gw_ref[0] = _dot_tn(a, dmc, precision=HI)
        gb_ref[0] = jnp.sum(dmf_ref[0], axis=0, keepdims=True)
        dcc_ref[...] += _dot_nt(dmc[nr - 8:nr], w_ref[0], precision=HI)

    return pl.pallas_call(
        body, name="mod_bwd", grid=(nl,),
        in_specs=[_full((nr, D)), pl.BlockSpec((1, D, wc), lambda l: (l, 0, 0)),
                  pl.BlockSpec((1, nr, wc), lambda l: (l, 0, 0)), pl.BlockSpec((1, nr, 3 * D), lambda l: (l, 0, 0))],
        out_specs=[pl.BlockSpec((1, D, wc), lambda l: (l, 0, 0)), pl.BlockSpec((1, 1, 3 * D), lambda l: (l, 0, 0)),
                   _full((8, D))],
        out_shape=[_sds((nl, D, wc)), _sds((nl, 1, 3 * D)), _sds((8, D))],
        compiler_params=_cparams(("arbitrary",)),
    )(c_rows, w_mod, dm_cols, dm_full)


def cctx_grad(parts, c_ctx):
    def body(p_ref, c_ref, o_ref):
        tot = p_ref[0, 0:1, :]
        for k in (2, 4, 6):
            tot = tot + p_ref[k, 0:1, :]
        c = c_ref[...]
        sg = jax.nn.sigmoid(c)
        o_ref[...] = tot * (sg * (1.0 + c * (1.0 - sg)))

    return pl.pallas_call(body, name="cctx_grad", out_shape=_sds((1, D)))(parts, c_ctx)


def sum_lead(x, out_dtype=f32, tr=256, tc=None):
    k, r, c = x.shape
    tr = min(tr, r)
    tc = c if tc is None else tc
    assert r % tr == 0 and c % tc == 0

    def body(x_ref, o_ref):
        tot = x_ref[0].astype(f32)
        for i in range(1, k):
            tot = tot + x_ref[i].astype(f32)
        o_ref[...] = tot.astype(out_dtype)

    return pl.pallas_call(
        body, name="sum_lead", grid=(r // tr, c // tc),
        in_specs=[pl.BlockSpec((k, tr, tc), lambda i, j: (0, i, j))],
        out_specs=pl.BlockSpec((tr, tc), lambda i, j: (i, j)),
        out_shape=_sds((r, c), out_dtype),
        compiler_params=_cparams(("arbitrary", "arbitrary")),
    )(x)


def adamw(w, m, v, g1, g2=None, tr=256, block=None):
    if block is None:
        block = (1,) * (w.ndim - 2) + (min(tr, w.shape[-2]), w.shape[-1])
    assert len(block) == w.ndim and all(d % b == 0 for d, b in zip(w.shape, block))
    two = g2 is not None
    c1 = 1.0 / (1.0 - ADAM_B1 ** ADAM_STEP)
    c2 = 1.0 / (1.0 - ADAM_B2 ** ADAM_STEP)

    def body(*refs):
        w_ref, m_ref, v_ref, g_ref = refs[:4]
        g = g_ref[...]
        if two:
            g = g + refs[4][...]
        go_ref, d_ref, mo_ref, vo_ref = refs[-4:]
        mn = ADAM_B1 * m_ref[...] + (1.0 - ADAM_B1) * g
        vn = ADAM_B2 * v_ref[...] + (1.0 - ADAM_B2) * (g * g)
        go_ref[...] = g
        mo_ref[...] = mn
        vo_ref[...] = vn
        d_ref[...] = -ADAM_LR * ((mn * c1) / (jnp.sqrt(vn * c2) + ADAM_EPS) + ADAM_WD * w_ref[...])

    blk = pl.BlockSpec(block, lambda *i: i)
    grid = tuple(d // b for d, b in zip(w.shape, block))
    args = [w, m, v, g1] + ([g2] if two else [])
    return pl.pallas_call(
        body, name="adamw", grid=grid,
        in_specs=[blk] * len(args), out_specs=[blk] * 4, out_shape=[_sds(w.shape)] * 4,
        compiler_params=_cparams(("arbitrary",) * len(grid)),
    )(*args)


def _my_pos():
    return lax.axis_index("x"), lax.axis_index("y"), lax.axis_index("c")


GROUP_SIZE = {"devices": N_DEV, "chips": N_CHIPS, "cores": 2}


def _peer_exchange(kind, group, src_refs, dst_refs, send_sems, recv_sems, local_sems):
    mx, my, mc = _my_pos()
    n = GROUP_SIZE[group]
    if group == "devices":
        me = 4 * mx + 2 * my + mc
    elif group == "chips":
        me = 2 * mx + my
    else:
        me = mc

    def peer(k):
        if group == "devices":
            return (mx ^ (k >> 2), my ^ ((k >> 1) & 1), mc ^ (k & 1))
        if group == "chips":
            return (mx ^ (k >> 1), my ^ (k & 1), mc)
        return (mx, my, mc ^ k)

    def copies():
        local, sends, recvs = [], [], []
        for i, (src, dst) in enumerate(zip(src_refs, dst_refs)):
            def part(k):
                return src.at[k] if kind == "scatter" else src

            def slab(k):
                return dst if kind == "send" else dst.at[k]

            if kind != "send":
                local.append(pltpu.make_async_copy(part(me), dst.at[me], local_sems.at[i]))
            for k in range(1, n):
                sem = dict(send_sem=send_sems.at[i, k - 1], recv_sem=recv_sems.at[i, k - 1], device_id_type=MESH)
                sends.append(pltpu.make_async_remote_copy(src_ref=part(me ^ k), dst_ref=slab(me), device_id=peer(k), **sem))
                recvs.append(pltpu.make_async_remote_copy(src_ref=part(me ^ k), dst_ref=slab(me ^ k),
                                                          device_id=(mx, my, mc), **sem))
        return local, sends, recvs

    def start():
        local, sends, _ = copies()
        for cp in local + sends:
            cp.start()

    def wait():
        local, sends, recvs = copies()
        for cp in recvs:
            cp.wait_recv()
        for cp in sends:
            cp.wait_send()
        for cp in local:
            cp.wait()

    return start, wait


def _exchange_scratch(group, n):
    k = GROUP_SIZE[group] - 1
    return [pltpu.SemaphoreType.DMA((n, k)), pltpu.SemaphoreType.DMA((n, k)), pltpu.SemaphoreType.DMA((n,))]


def _exchange_shapes(kind, group, arrs):
    return [_sds(((GROUP_SIZE[group],) + a.shape) if kind == "gather" else a.shape, a.dtype) for a in arrs]


def exchange(name, parts):
    counts = [len(arrs) for _, _, arrs in parts]
    total = sum(counts)

    def body(*refs):
        srcs, dsts, sems = refs[:total], refs[total:2 * total], refs[2 * total:]
        ops, at = [], 0
        for j, (kind, group, arrs) in enumerate(parts):
            ops.append(_peer_exchange(kind, group, srcs[at:at + counts[j]], dsts[at:at + counts[j]], *sems[3 * j:3 * j + 3]))
            at += counts[j]
        for start, _ in ops:
            start()
        for _, wait in ops:
            wait()

    any_ = pl.BlockSpec(memory_space=pl.ANY)
    flat = [a for _, _, arrs in parts for a in arrs]
    outs = pl.pallas_call(
        body, name=name, out_shape=[sh for kind, group, arrs in parts for sh in _exchange_shapes(kind, group, arrs)],
        in_specs=[any_] * total, out_specs=[any_] * total,
        scratch_shapes=[sc for _, group, arrs in parts for sc in _exchange_scratch(group, len(arrs))],
    )(*flat)
    res, at = [], 0
    for cnt in counts:
        res.append(list(outs[at:at + cnt]))
        at += cnt
    return res


def gather8(x):
    return exchange("gather8", [("gather", "devices", [x])])[0][0]


PACK_ROWS = 64
SMALL = ("c_ctx", "b_mod", "g_pre", "g_post", "ret_norm_g", "sg_w", "sg_b", "sc_conv_w", "gdn_conv_w",
         "gdn_a_log", "gdn_dt_bias", "gdn_norm_g")


def _pack(arrs, width=D, mult=PACK_ROWS):
    rows = []
    for a in arrs:
        flat = a.reshape(-1)
        pad = (-flat.shape[0]) % width
        rows.append(jnp.pad(flat, (0, pad)).reshape(-1, width))
    out = jnp.concatenate(rows, axis=0)
    return jnp.pad(out, ((0, (-out.shape[0]) % mult), (0, 0)))


def _unpack(packed, shapes, width=D):
    outs, r = [], 0
    for shp in shapes:
        size = int(np.prod(shp))
        nr = -(-size // width)
        outs.append(packed[r:r + nr].reshape(-1)[:size].reshape(shp))
        r += nr
    return outs


def kernel(x, c, ctx, c_ctx, w_mod, b_mod, g_pre, g_post, w_in, w_out, ret_norm_g, sg_w, sg_b, sc_conv_w, gdn_conv_w, gdn_a_log, gdn_dt_bias, gdn_norm_g, loss_target, m_c_ctx, m_w_mod, m_b_mod, m_g_pre, m_g_post, m_w_in, m_w_out, m_ret_norm_g, m_sg_w, m_sg_b, m_sc_conv_w, m_gdn_conv_w, m_gdn_a_log, m_gdn_dt_bias, m_gdn_norm_g, v_c_ctx, v_w_mod, v_b_mod, v_g_pre, v_g_post, v_w_in, v_w_out, v_ret_norm_g, v_sg_w, v_sg_b, v_sc_conv_w, v_gdn_conv_w, v_gdn_a_log, v_gdn_dt_bias, v_gdn_norm_g):
    weights = dict(c_ctx=c_ctx, w_mod=w_mod, b_mod=b_mod, g_pre=g_pre, g_post=g_post, w_in=w_in, w_out=w_out,
                   ret_norm_g=ret_norm_g, sg_w=sg_w, sg_b=sg_b, sc_conv_w=sc_conv_w, gdn_conv_w=gdn_conv_w,
                   gdn_a_log=gdn_a_log, gdn_dt_bias=gdn_dt_bias, gdn_norm_g=gdn_norm_g)
    mom = dict(c_ctx=m_c_ctx, w_mod=m_w_mod, b_mod=m_b_mod, g_pre=m_g_pre, g_post=m_g_post, w_in=m_w_in,
               w_out=m_w_out, ret_norm_g=m_ret_norm_g, sg_w=m_sg_w, sg_b=m_sg_b, sc_conv_w=m_sc_conv_w,
               gdn_conv_w=m_gdn_conv_w, gdn_a_log=m_gdn_a_log, gdn_dt_bias=m_gdn_dt_bias, gdn_norm_g=m_gdn_norm_g)
    var = dict(c_ctx=v_c_ctx, w_mod=v_w_mod, b_mod=v_b_mod, g_pre=v_g_pre, g_post=v_g_post, w_in=v_w_in,
               w_out=v_w_out, ret_norm_g=v_ret_norm_g, sg_w=v_sg_w, sg_b=v_sg_b, sc_conv_w=v_sc_conv_w,
               gdn_conv_w=v_gdn_conv_w, gdn_a_log=v_gdn_a_log, gdn_dt_bias=v_gdn_dt_bias, gdn_norm_g=v_gdn_norm_g)

    nb, t_lat, _ = x.shape
    t_ctx = ctx.shape[1]
    s = t_ctx + t_lat
    n = nb * s
    sb = s // TM
    nl = w_in.shape[0]
    wc_in = w_in.shape[2]
    wc_mod = w_mod.shape[2]
    rows_out = w_out.shape[1]
    n_all = nb * N_DEV
    mx, my, mc = _my_pos()
    chip = 2 * mx + my
    dev = 2 * chip + mc

    sg_c = _sg_consts()
    bd = jnp.asarray(_block_diag())
    ret_c = _ret_consts(nb)
    gdn_c = _gdn_consts(nb)
    cos, sins = _rope_tables(t_lat, t_ctx)

    w_in_b, w_out_b = w_in.astype(bf16), w_out.astype(bf16)
    pre = _pack([c, sc_conv_w, gdn_conv_w], mult=8)
    (pre_all,), w0_parts = exchange("startup_gather", [("gather", "devices", [pre]),
                                                       ("gather", "chips", [w_in_b[0], w_out_b[0]])])
    c_parts, scw_parts, gcw_parts = [], [], []
    for k in range(N_DEV):
        ck, sk, gk = _unpack(pre_all[k], [c.shape, sc_conv_w.shape, gdn_conv_w.shape])
        c_parts.append(ck)
        if k % 2 == 0:
            scw_parts.append(sk)
            gcw_parts.append(gk)
    c_all = jnp.concatenate(c_parts, axis=0)
    sc_w_full = jnp.concatenate(scw_parts, axis=-1)
    gdn_w_full = jnp.concatenate(gcw_parts, axis=-1)
    c_rows = jnp.concatenate([c_all, c_ctx[None, :], jnp.zeros((7, D), f32)], axis=0)

    b_cols = lax.dynamic_slice_in_dim(b_mod, chip * wc_mod, wc_mod, axis=1)[:, None, :]
    mod_part = mod_fwd(c_rows, w_mod, b_cols)
    mod_all = gather8(mod_part)
    mod = jnp.concatenate([mod_all[2 * k] for k in range(N_CHIPS)], axis=-1)
    my_rows = jnp.concatenate([lax.dynamic_slice_in_dim(mod, dev * nb, nb, axis=1), mod[:, n_all:n_all + 1]], axis=1)
    shift_t = my_rows[:, :, None, 0:D]
    scale_t = my_rows[:, :, None, D:2 * D]
    gate_t = my_rows[:, :, None, 2 * D:3 * D]

    def full_weights(parts):
        return place_weights(parts[0]), parts[1].reshape(D, D)

    w_in_full, w_out_full = [None] * nl, [None] * nl
    w_in_full[0], w_out_full[0] = full_weights(w0_parts)

    alog = jnp.pad(gdn_a_log.reshape(nl, 1, 8), ((0, 0), (0, 0), (0, 120)))
    dtb = jnp.pad(gdn_dt_bias.reshape(nl, 1, 8), ((0, 0), (0, 0), (0, 120)))
    gdn_ng = jnp.tile(gdn_norm_g, (1, NH))[:, None, :]
    ret_ng = ret_norm_g[:, None, :]

    xs = jnp.concatenate([ctx, x], axis=1).reshape(n, D)
    saved = []
    for l in range(nl):
        p, h = inproj_fwd(xs, shift_t[l], scale_t[l], g_pre[l][None, :], w_in_full[l], nb, sb)
        p3 = p.reshape(nb, s, PW)
        ro_f, ro_b, rs_all = ret_scan_fwd(p3, cos, sins, ret_c, t_ctx)
        y_ret = mix_finish_fwd(_ret_finish, "ret_finish_fwd", ro_f, ro_b, p3, 3, ret_ng[l], bd)
        y_sg = sg_fwd(p3, sg_w[l], sg_b[l], *sg_c)
        y_sc = sc_fwd(p3, sc_w_full[l], t_ctx)
        cq, ck, cv = [gdn_conv_fwd(p3, gdn_w_full[l][:, BW * i:BW * (i + 1)], 11 + i, t_ctx) for i in range(3)]
        nxt = None if l + 1 == nl else ("gather", [w_in_b[l + 1], w_out_b[l + 1]])
        go_f, go_b, *gs_all = gdn_scan_fwd(cq, ck, cv, p3, alog[l], dtb[l], gdn_c, t_ctx, nxt)
        if nxt is not None:
            w_in_full[l + 1], w_out_full[l + 1] = full_weights(gs_all[2:])
            gs_all = gs_all[:2]
        y_gdn = mix_finish_fwd(_gdn_finish, "gdn_finish_fwd", go_f, go_b, p3, 14, gdn_ng[l], bd)
        ys = [a.reshape(n, BW) for a in (y_ret, y_sg, y_sc, y_gdn)]
        x_new, o = outproj_fwd(ys, w_out_full[l], xs, gate_t[l], g_post[l][None, :], nb, sb)
        saved.append(dict(x=xs, h=h, p3=p3, ro=(ro_f, ro_b), rs=rs_all, c=(cq, ck, cv), go=(go_f, go_b), gs=gs_all,
                          ys=ys, o=o))
        xs = x_new

    dx3, loss_part = loss_head(xs.reshape(nb, s, D), loss_target, t_ctx)
    loss = lax.psum(loss_part[0, 0], ("x", "y", "c"))

    dxs = dx3.reshape(n, D)
    g_small = {k: [None] * nl for k in SMALL if k not in ("c_ctx", "b_mod")}
    dm_rows = [None] * nl
    slabs = None
    got_in, got_out = [None] * nl, [None] * nl
    for l in reversed(range(nl)):
        sv = saved[l]
        p3 = sv["p3"]
        dy, gw_out, dg_post, dgate = outproj_bwd(dxs, sv["o"], gate_t[l], g_post[l][None, :], sv["ys"], w_out_full[l], nb, sb)
        dy3 = dy.reshape(nb, s, D)
        r_do, r_dz, d_rng = mix_finish_bwd(_ret_finish, "ret_finish_bwd", *sv["ro"], p3, 3, ret_ng[l], bd, dy3, 0)
        r_d = ret_scan_bwd(p3, cos, sins, ret_c, sv["rs"], r_do, t_ctx)
        s_du, s_dv, s_dz, d_sgw, d_sgb = sg_bwd(p3, sg_w[l], sg_b[l], *sg_c, dy3)
        c_db, c_dc, c_dh, c_dz, d_scw = sc_bwd(p3, sc_w_full[l], dy3, t_ctx)
        g_do, g_dz, d_gng = mix_finish_bwd(_gdn_finish, "gdn_finish_bwd", *sv["go"], p3, 14, gdn_ng[l], bd, dy3, 3)
        g_d = gdn_scan_bwd(*sv["c"], p3, alog[l], dtb[l], gdn_c, *sv["gs"], g_do, t_ctx,
                           None if slabs is None else ("scatter", slabs))
        if slabs is not None:
            got_in[l + 1], got_out[l + 1] = g_d[10:]
        gx, d_gcw = [], []
        for i in range(3):
            dxi, dwi = gdn_conv_bwd(p3, gdn_w_full[l][:, BW * i:BW * (i + 1)], 11 + i, g_d[i], g_d[4 + i], t_ctx)
            gx.append(dxi)
            d_gcw.append(dwi)
        dp3 = assemble_dp([(r_d[0], r_d[3]), (r_d[1], r_d[4]), (r_d[2], r_d[5])],
                          [r_dz, s_du, s_dv, s_dz, c_db, c_dc, c_dh, c_dz], gx, [g_dz], [g_d[3], g_d[7]])
        dp = dp3.reshape(n, PW)
        dxs, dg_pre, dshift, dscale = inproj_bwd_x(dp, w_in_full[l], sv["x"], scale_t[l], g_pre[l][None, :], dxs, nb, sb)
        gw_in = dw_in(sv["h"], dp)
        slabs = [jnp.stack([gw_in[k * wc_in:(k + 1) * wc_in] for k in range(N_CHIPS)]),
                 gw_out.reshape(N_CHIPS, rows_out, D).astype(bf16)]
        g_small["g_pre"][l] = dg_pre[0]
        g_small["g_post"][l] = dg_post[0]
        g_small["ret_norm_g"][l] = d_rng[0]
        g_small["sg_w"][l] = d_sgw
        g_small["sg_b"][l] = d_sgb
        g_small["sc_conv_w"][l] = d_scw
        g_small["gdn_conv_w"][l] = jnp.concatenate(d_gcw, axis=-1)
        g_small["gdn_a_log"][l] = g_d[8][0, :8].reshape(2, NH)
        g_small["gdn_dt_bias"][l] = g_d[9][0, :8].reshape(2, NH)
        g_small["gdn_norm_g"][l] = d_gng[0].reshape(NH, HD)
        dm_rows[l] = jnp.concatenate([dshift, dscale, dgate], axis=-1)[:nb + 1]
    grad_x = dxs.reshape(nb, s, D)[:, t_ctx:, :]

    g_small = {k: jnp.stack(v) for k, v in g_small.items()}
    dm_rows = jnp.stack(dm_rows)
    names2 = [k for k in SMALL if k not in ("c_ctx", "b_mod")]
    pack_sum = _pack([g_small[k] for k in names2] + [dm_rows[:, nb:]])
    pack_own = _pack([dm_rows[:, :nb]], mult=8)
    rs = -(-pack_sum.shape[0] // (8 * N_DEV)) * 8
    slabs_sum = jnp.pad(pack_sum, ((0, N_DEV * rs - pack_sum.shape[0]), (0, 0))).reshape(N_DEV, rs, D)
    (got_small,), (got_in[0], got_out[0]) = exchange("tail_scatter", [("scatter", "devices", [slabs_sum]),
                                                                      ("scatter", "chips", slabs)])
    my_slab = sum_lead(got_small, tr=rs)
    gin_mine = jnp.stack([sum_lead(a, tr=wc_in, tc=256) for a in got_in], axis=1)
    gout_mine = jnp.stack([sum_lead(a) for a in got_out])
    (all2,), (gin_sib, gout_sib) = exchange("tail_gather", [
        ("gather", "devices", [jnp.concatenate([my_slab, pack_own], axis=0)]), ("send", "cores", [gin_mine, gout_mine])])
    tot2 = all2[:, :rs].reshape(N_DEV * rs, D)
    outs2 = _unpack(tot2, [g_small[k].shape for k in names2] + [(nl, 1, 3 * D)])
    grads = dict(zip(names2, outs2[:-1]))
    dm_own = jnp.stack([_unpack(all2[k, rs:], [(nl, nb, 3 * D)])[0] for k in range(N_DEV)])
    dm_own = jnp.transpose(dm_own, (1, 0, 2, 3)).reshape(nl, n_all, 3 * D)
    dm_all = jnp.concatenate([dm_own, jnp.pad(outs2[-1], ((0, 0), (0, 7), (0, 0)))], axis=1)
    grads["gdn_norm_g"] = sum_lead(jnp.transpose(grads["gdn_norm_g"], (1, 0, 2)), tr=nl)
    for k in ("sc_conv_w", "gdn_conv_w"):
        wc = weights[k].shape[2]
        grads[k] = lax.dynamic_slice_in_dim(grads[k], chip * wc, wc, axis=2)

    dm_cols = lax.dynamic_slice_in_dim(dm_all, chip * wc_mod, wc_mod, axis=2)
    g_w_mod, g_b_mod, dcc_part = mod_bwd(c_rows, w_mod, dm_cols, dm_all)
    grads["b_mod"] = g_b_mod[:, 0, :]
    grads["c_ctx"] = cctx_grad(gather8(dcc_part), c_ctx[None, :])[0]

    res = {}
    w_in_t, m_in_t, v_in_t = [jnp.transpose(a, (2, 0, 1)) for a in (w_in, m_w_in, v_w_in)]
    res["w_in"] = [jnp.transpose(a, (1, 2, 0)) for a in
                   adamw(w_in_t, m_in_t, v_in_t, gin_mine, gin_sib, block=(wc_in // 4, nl, 256))]
    res["w_out"] = adamw(w_out, m_w_out, v_w_out, gout_mine, gout_sib)
    res["w_mod"] = adamw(w_mod, m_w_mod, v_w_mod, g_w_mod)
    shapes = [weights[k].shape for k in SMALL]
    small = adamw(_pack([weights[k] for k in SMALL]), _pack([mom[k] for k in SMALL]), _pack([var[k] for k in SMALL]),
                  _pack([grads[k].reshape(weights[k].shape) for k in SMALL]), tr=PACK_ROWS)
    small = [_unpack(a, shapes) for a in small]
    for i, k in enumerate(SMALL):
        res[k] = [small[j][i] for j in range(4)]

    order = ["c_ctx", "w_mod", "b_mod", "g_pre", "g_post", "w_in", "w_out", "ret_norm_g", "sg_w", "sg_b", "sc_conv_w",
             "gdn_conv_w", "gdn_a_log", "gdn_dt_bias", "gdn_norm_g"]
    return (loss, grad_x, *[res[k][0] for k in order], *[res[k][1] for k in order], *[res[k][2] for k in order],
            *[res[k][3] for k in order])
```

```python
import functools

import jax
import jax.numpy as jnp
import numpy as np
from jax import lax
from jax.experimental import pallas as pl
from jax.experimental.pallas import tpu as pltpu

f32 = jnp.float32
bf16 = jnp.bfloat16
HI = lax.Precision.HIGHEST
P3 = lax.Precision.HIGH
MESH = pl.DeviceIdType.MESH

EPS = 1e-6
D = 1024
NH = 4
HD = 64
BW = NH * HD
PAIR_W = 2 * HD
RC = 128
GC = 64
GRID_W = 64
ROPE_BASE = 10000.0
IN_W = 15 * BW + 16
PW = 4096
GATE_COL = 15 * BW
N_CHIPS = 4
N_DEV = 8
TM = 256
TP = 2 * TM
ADAM_LR, ADAM_B1, ADAM_B2, ADAM_EPS, ADAM_WD, ADAM_STEP = 0.001, 0.9, 0.999, 1e-08, 0.01, 10
LANE_HEAD = np.arange(BW) // HD
VMEM_BIG = 56 * 1024 * 1024


def _dot(a, b, precision=None):
    return jnp.dot(a, b, precision=precision, preferred_element_type=f32)


def _dot_nt(a, b, precision=None):
    return lax.dot_general(a, b, (((1,), (1,)), ((), ())), precision=precision, preferred_element_type=f32)


def _dot_tn(a, b, precision=None):
    return lax.dot_general(a, b, (((0,), (0,)), ((), ())), precision=precision, preferred_element_type=f32)


def _sds(shape, dtype=f32):
    return jax.ShapeDtypeStruct(shape, dtype)


def _cparams(sem=None, vmem=None):
    kw = {}
    if sem is not None:
        kw["dimension_semantics"] = sem
    if vmem is not None:
        kw["vmem_limit_bytes"] = vmem
    return pltpu.CompilerParams(**kw)


def _full(shape):
    n = len(shape)
    return pl.BlockSpec(shape, lambda *_: (0,) * n)


def _block_diag():
    return (LANE_HEAD[:, None] == LANE_HEAD[None, :]).astype(np.float32)


def _tau(c, d):
    return np.arange(c) if d == 0 else c - 1 - np.arange(c)


def _ret_consts(nb):
    lg = np.log(1.0 - 2.0 ** (-5.0 - np.arange(NH)))
    intra = np.zeros((2, 2, RC, 2 * RC)); qdec = np.zeros((2, 2, RC, PAIR_W)); kdec = np.zeros((2, 2, RC, PAIR_W))
    cd = np.zeros((2, 2, PAIR_W, PAIR_W))
    for d in range(2):
        t = _tau(RC, d)
        diff = t[:, None] - t[None, :]
        for p in range(2):
            lane_lg = lg[2 * p + np.arange(PAIR_W) // HD]
            for h in range(2):
                intra[d, p, :, h * RC:(h + 1) * RC] = np.where(diff >= 0, np.exp(np.maximum(diff, 0) * lg[2 * p + h]), 0.0)
            qdec[d, p] = np.exp((t[:, None] + 1.0) * lane_lg[None, :])
            kdec[d, p] = np.exp((RC - 1.0 - t[:, None]) * lane_lg[None, :])
            cd[d, p] = np.exp(RC * lane_lg)[:, None] * np.ones((1, PAIR_W))
    per_z = [np.tile(a.reshape((4,) + a.shape[2:]), (nb, 1, 1)) for a in (intra, qdec, kdec, cd)]
    bd2 = (np.arange(PAIR_W)[:, None] // HD == np.arange(PAIR_W)[None, :] // HD)
    bdr = (np.arange(2 * RC)[:, None] // RC == np.arange(PAIR_W)[None, :] // HD)
    return [jnp.asarray(a, f32) for a in per_z + [bd2, bdr]]


def _rope_tables(t_lat, t_ctx):
    nf = HD // 4
    inv = ROPE_BASE ** (-np.arange(nf) / nf)
    pos = np.arange(t_lat)
    ang_r = (pos // GRID_W)[:, None] * inv[None, :]
    ang_c = (pos % GRID_W)[:, None] * inv[None, :]
    ang = np.concatenate([ang_r, ang_r, ang_c, ang_c], axis=1)
    sign = np.concatenate([-np.ones(nf), np.ones(nf), -np.ones(nf), np.ones(nf)])
    cos = np.tile(np.cos(ang), (1, 2)); sins = np.tile(np.sin(ang) * sign, (1, 2))
    cos = np.concatenate([np.ones((t_ctx, PAIR_W)), cos]); sins = np.concatenate([np.zeros((t_ctx, PAIR_W)), sins])
    return jnp.asarray(cos, f32), jnp.asarray(sins, f32)


def _gdn_consts(nb):
    tmask = np.zeros((2, 2, GC, GC)); tmask2 = np.zeros((2, 2, GC, PAIR_W)); strict2 = np.zeros((2, 2, GC, PAIR_W))
    exp_g = np.zeros((2, 2, 128, PAIR_W)); exp_b = np.zeros((2, 2, 128, PAIR_W))
    for d in range(2):
        t = _tau(GC, d)
        tmask[d, :] = (t[:, None] >= t[None, :])
        tmask2[d, :] = np.tile(t[:, None] >= t[None, :], (1, 2))
        strict2[d, :] = np.tile(t[:, None] > t[None, :], (1, 2))
        for h in range(NH):
            exp_g[d, h // 2, 4 * d + h, (h % 2) * HD:(h % 2 + 1) * HD] = 1.0
            exp_b[d, h // 2, 8 + 4 * d + h, (h % 2) * HD:(h % 2 + 1) * HD] = 1.0
    exp_gt = np.transpose(exp_g, (0, 1, 3, 2))
    per_z = [np.tile(a.reshape((4,) + a.shape[2:]), (nb, 1, 1)) for a in (tmask, tmask2, strict2, exp_g, exp_b, exp_gt)]
    dsel2 = np.tile(np.eye(GC), (1, 2))
    eye2 = np.tile(np.eye(GC), (1, 2))
    bd2 = (np.arange(PAIR_W)[:, None] // HD == np.arange(PAIR_W)[None, :] // HD)
    return [jnp.asarray(a, f32) for a in per_z + [dsel2, eye2, bd2]]


def _swap16(x):
    lane = lax.broadcasted_iota(jnp.int32, x.shape, x.ndim - 1)
    n = x.shape[-1]
    return jnp.where(lane % 32 < 16, pltpu.roll(x, n - 16, axis=x.ndim - 1), pltpu.roll(x, 16, axis=x.ndim - 1))


@jax.custom_vjp
def _rot(x, cos, sins):
    return x * cos + _swap16(x) * sins


def _rot_fwd(x, cos, sins):
    return _rot(x, cos, sins), (cos, sins)


def _rot_bwd(res, g):
    cos, sins = res
    return g * cos + _swap16(g * sins), jnp.zeros_like(cos), jnp.zeros_like(sins)


_rot.defvjp(_rot_fwd, _rot_bwd)


def _silu(z):
    return z * jax.nn.sigmoid(z)


def _head_sum(x, bd):
    return _sel_r(x, bd)


def _ret_step(s, q, k, v, cos, sins, intra, qdec, kdec, cd, bd2, bdr):
    def bdiag(x):
        return jnp.concatenate([x, x], axis=1) * bdr

    qr = _rot(q, cos, sins)
    kr = _rot(k, cos, sins) * (HD ** -0.5)
    sc = _bmm_nt(qr, bdiag(kr)) * intra
    o = _bmm(qr * qdec, s) + _bmm(sc, bdiag(v))
    s_new = s * cd + bd2 * _bmm_tn(kr * kdec, v)
    return s_new, o


def _ret_finish(o_f, o_b, z, norm_g, bd):
    o = o_f + o_b
    mu = _head_sum(o, bd) * (1.0 / HD)
    xc = o - mu
    var = _head_sum(xc * xc, bd) * (1.0 / HD)
    return xc * lax.rsqrt(var + EPS) * norm_g * _silu(z)


def _softplus(x):
    return jnp.maximum(x, 0.0) + jnp.log(1.0 + jnp.exp(-jnp.abs(x)))


def _bmm(a, b, precision=None):
    return lax.dot_general(a, b, (((2,), (1,)), ((0,), (0,))), precision=precision, preferred_element_type=f32)


def _bmm_nt(a, b, precision=None):
    return lax.dot_general(a, b, (((2,), (2,)), ((0,), (0,))), precision=precision, preferred_element_type=f32)


def _bmm_tn(a, b, precision=None):
    return lax.dot_general(a, b, (((1,), (1,)), ((0,), (0,))), precision=precision, preferred_element_type=f32)


def _mm(a, b, mode):
    ca, cb = {"nn": (1, 0), "nt": (1, 1), "tn": (0, 0)}[mode]
    if a.ndim == 3:
        dims = (((ca + 1,), (cb + 1,)), ((0,), (0,)))
    else:
        dims = (((ca,), (cb,)), ((), ()))
    return lax.dot_general(a, b, dims, preferred_element_type=f32)


def _split(a):
    hi = a.astype(bf16)
    return hi, (a - hi.astype(f32)).astype(bf16)


def _sel2(a, e, mode, e_left):
    hi, lo = _split(a)
    eb = e.astype(bf16)
    if e_left:
        return _mm(eb, hi, mode) + _mm(eb, lo, mode)
    return _mm(hi, eb, mode) + _mm(lo, eb, mode)


@jax.custom_vjp
def _sel_r(a, e):
    return _sel2(a, e, "nn", False)


_sel_r.defvjp(lambda a, e: (_sel_r(a, e), e), lambda e, g: (_sel2(g, e, "nt", False), jnp.zeros_like(e)))


@jax.custom_vjp
def _sel_l(e, b):
    return _sel2(b, e, "nn", True)


_sel_l.defvjp(lambda e, b: (_sel_l(e, b), e), lambda e, g: (jnp.zeros_like(e), _sel2(g, e, "tn", True)))


def _bdiag(x, bd2):
    return jnp.concatenate([x, x], axis=1) * bd2


@jax.custom_vjp
def _solve_given_inv(m, vb, kbg, inv, bd2):
    return _bmm(inv, _bdiag(vb, bd2), P3), _bmm(inv, _bdiag(kbg, bd2), P3)


def _solve_fwd(m, vb, kbg, inv, bd2):
    u, w = _solve_given_inv(m, vb, kbg, inv, bd2)
    return (u, w), (inv, u, w, bd2)


def _solve_bwd(res, cts):
    inv, u, w, bd2 = res
    du, dw = cts
    c = inv.shape[1]
    t = jnp.swapaxes(_bdiag(inv, bd2), 1, 2)
    inv_t = t[:, :c] + t[:, c:]
    dvb = _bmm(inv_t, _bdiag(du, bd2), P3)
    dkbg = _bmm(inv_t, _bdiag(dw, bd2), P3)
    dm = _bmm_nt(dvb, _bdiag(u, bd2), P3) + _bmm_nt(dkbg, _bdiag(w, bd2), P3)
    return dm, dvb, dkbg, jnp.zeros_like(inv), jnp.zeros_like(bd2)


_solve_given_inv.defvjp(_solve_fwd, _solve_bwd)


def _gdn_step(s, q, k, v, gate, alog, dtb, tmask, tmask2, strict2, exp_g, exp_b, exp_gt, dsel2, eye2, bd2, inv=None):
    z, c, w_ = q.shape
    ne = gate.shape[0]

    def per_pair(a):
        return jnp.broadcast_to(a[:, None], (ne, z // ne) + a.shape[1:]).reshape((z,) + a.shape[1:])

    def rows(a):
        return a.reshape(z * c, w_)

    def bdiag(x):
        return _bdiag(x, bd2)

    g = per_pair(-jnp.exp(alog) * _softplus(gate + dtb))
    beta = per_pair(jax.nn.sigmoid(gate))
    gl = _sel_r(g, exp_g)
    gc_l = _sel_l(tmask, gl)
    glast_l = jnp.sum(gl, axis=1, keepdims=True)
    glast = jnp.sum(g, axis=1, keepdims=True)
    beta_l = _sel_r(beta, exp_b)
    gc_r = jnp.sum(gc_l * dsel2, axis=1, keepdims=True)
    qn = q * lax.rsqrt(_sel_r(rows(q * q), bd2).reshape(z, c, w_) + EPS)
    kn = k * lax.rsqrt(_sel_r(rows(k * k), bd2).reshape(z, c, w_) + EPS)
    eg = jnp.exp(gc_l)
    kb = kn * beta_l
    vb = v * beta_l
    kbg = kb * eg
    qs = qn * (HD ** -0.5)
    dec = jnp.exp(jnp.where(tmask2 > 0, gc_l - gc_r, -1e30))
    kns = bdiag(kn)
    m = -(_bmm_nt(kb, kns) * dec * strict2)
    if inv is None:
        inv = eye2 + m
        p = m
        for _ in range(5):
            p = _bmm(p, bdiag(p), P3)
            inv = inv + _bmm(inv, bdiag(p), P3)
        u = _bmm(inv, bdiag(vb), P3)
        w = _bmm(inv, bdiag(kbg), P3)
    else:
        u, w = _solve_given_inv(m, vb, kbg, inv, bd2)
    v_new = u - _bmm(w, s)
    k_tail = kn * jnp.exp(glast_l - gc_l)
    cdec = jnp.sum(exp_gt * jnp.exp(glast), axis=-1, keepdims=True)
    s_new = s * cdec + bd2 * _bmm_tn(k_tail, v_new)
    a = _bmm_nt(qs, kns) * dec
    o = _bmm(qs * eg, s) + _bmm(a, bdiag(v_new))
    return s_new, o, inv


def _gdn_finish(o_f, o_b, z, norm_g, bd):
    o = o_f + o_b
    ms = _head_sum(o * o, bd) * (1.0 / HD)
    return o * lax.rsqrt(ms + EPS) * norm_g * _silu(z)


def _gelu(x):
    return 0.5 * x * (1.0 + jnp.tanh(0.7978845608028654 * (x + 0.044715 * (x * x * x))))


def _sg_block(u0, u1, v0, v1, z0, z1, w, b, hmp, bdr):
    ts = u0.shape[0]
    nc = ts // RC
    g0, g1 = _gelu(v0), _gelu(v1)
    mu = (jnp.sum(g0, axis=-1, keepdims=True) + jnp.sum(g1, axis=-1, keepdims=True)) * (1.0 / BW)
    x0, x1 = g0 - mu, g1 - mu
    var = (jnp.sum(x0 * x0, axis=-1, keepdims=True) + jnp.sum(x1 * x1, axis=-1, keepdims=True)) * (1.0 / BW)
    rstd = lax.rsqrt(var + EPS)
    ys = []
    for p, (u, xc, z) in enumerate(((u0, x0, z0), (u1, x1, z1))):
        vn = (xc * rstd).reshape(nc, RC, PAIR_W)
        wp = jnp.concatenate([w[2 * p], w[2 * p + 1]], axis=1)
        mix = _bmm(jnp.broadcast_to(wp, (nc, RC, 2 * RC)), jnp.concatenate([vn, vn], axis=1) * bdr)
        bias = _dot_tn(b, hmp[p], precision=HI)
        s = (mix + bias).reshape(ts, PAIR_W)
        ys.append(_gelu(u) * s * _silu(z))
    return ys[0], ys[1]


def _make_shifts(t_ctx, n):
    def dn(x):
        t = lax.broadcasted_iota(jnp.int32, x.shape, 0)
        return jnp.where((t != 0) & (t != t_ctx), pltpu.roll(x, 1, axis=0), 0.0)

    def up(x):
        t = lax.broadcasted_iota(jnp.int32, x.shape, 0)
        return jnp.where((t != t_ctx - 1) & (t != n - 1), pltpu.roll(x, n - 1, axis=0), 0.0)

    @jax.custom_vjp
    def shift_dn(x):
        return dn(x)
    shift_dn.defvjp(lambda x: (dn(x), None), lambda _, g: (up(g),))

    @jax.custom_vjp
    def shift_up(x):
        return up(x)
    shift_up.defvjp(lambda x: (up(x), None), lambda _, g: (dn(g),))
    return shift_dn, shift_up


def _conv3(x, w, shift_dn, shift_up):
    return shift_dn(x) * w[0:1] + x * w[1:2] + shift_up(x) * w[2:3]


def inproj_fwd(x, shift_t, scale_t, g_pre, w_in, n_batch, sb):
    n = x.shape[0]

    def sel(i):
        return jnp.where(i % sb == 0, n_batch, i // sb)

    def body(x_ref, sh0, sh1, sc0, sc1, g_ref, w_ref, p_ref, h_ref):
        hs = []
        for k, (sh_ref, sc_ref) in enumerate(((sh0, sc0), (sh1, sc1))):
            xv = x_ref[k * TM:(k + 1) * TM, :]
            r = xv * lax.rsqrt(jnp.mean(xv * xv, axis=-1, keepdims=True) + EPS)
            hs.append(((r * g_ref[...]) * (1.0 + sc_ref[0]) + sh_ref[0]).astype(bf16))
        hb = jnp.concatenate(hs, axis=0)
        h_ref[...] = hb
        p_ref[...] = _dot(hb, w_ref[...])

    def mrow(k):
        return pl.BlockSpec((1, 1, D), lambda i: (sel(2 * i + k), 0, 0))

    return pl.pallas_call(
        body, name="inproj_fwd", grid=(n // TP,),
        in_specs=[pl.BlockSpec((TP, D), lambda i: (i, 0)), mrow(0), mrow(1), mrow(0), mrow(1),
                  _full((1, D)), _full((D, PW))],
        out_specs=[pl.BlockSpec((TP, PW), lambda i: (i, 0)), pl.BlockSpec((TP, D), lambda i: (i, 0))],
        out_shape=[_sds((n, PW)), _sds((n, D), bf16)],
        compiler_params=_cparams(("arbitrary",), VMEM_BIG),
    )(x, shift_t, shift_t, scale_t, scale_t, g_pre, w_in)


def outproj_fwd(ys, w_out, x, gate_t, g_post, n_batch, sb):
    n = x.shape[0]

    def sel(i):
        return jnp.where(i % sb == 0, n_batch, i // sb)

    def body(y0, y1, y2, y3, w_ref, x_ref, gt0, gt1, g_ref, xn_ref, o_ref):
        y = jnp.concatenate([y0[...], y1[...], y2[...], y3[...]], axis=1)
        o = _dot(y, w_ref[...])
        o_ref[...] = o
        nrm = o * lax.rsqrt(jnp.mean(o * o, axis=-1, keepdims=True) + EPS) * g_ref[...]
        for k, gt_ref in enumerate((gt0, gt1)):
            rows = slice(k * TM, (k + 1) * TM)
            xn_ref[rows, :] = x_ref[rows, :] + gt_ref[0] * nrm[rows]

    def mrow(k):
        return pl.BlockSpec((1, 1, D), lambda i: (sel(2 * i + k), 0, 0))

    yspec = pl.BlockSpec((TP, BW), lambda i: (i, 0))
    return pl.pallas_call(
        body, name="outproj_fwd", grid=(n // TP,),
        in_specs=[yspec, yspec, yspec, yspec, _full((D, D)), pl.BlockSpec((TP, D), lambda i: (i, 0)),
                  mrow(0), mrow(1), _full((1, D))],
        out_specs=[pl.BlockSpec((TP, D), lambda i: (i, 0)), pl.BlockSpec((TP, D), lambda i: (i, 0))],
        out_shape=[_sds((n, D)), _sds((n, D))],
        compiler_params=_cparams(("arbitrary",), VMEM_BIG),
    )(*ys, w_out, x, gate_t, gate_t, g_post)


def _row_onehot(r):
    return lax.broadcasted_iota(jnp.int32, (8, 1), 0) == r


def outproj_bwd(dxn, o, gate_t, g_post, ys, w_out, n_batch, sb):
    n = dxn.shape[0]

    def sel(i):
        return jnp.where(i % sb == 0, n_batch, i // sb)

    def body(dxn_ref, o_ref, gt0, gt1, g_ref, y0, y1, y2, y3, w_ref, dy_ref, dw_ref, dg_ref, dgate_ref):
        i = pl.program_id(0)

        @pl.when(i == 0)
        def _():
            dw_ref[...] = jnp.zeros_like(dw_ref)
            dg_ref[...] = jnp.zeros_like(dg_ref)
            dgate_ref[...] = jnp.zeros_like(dgate_ref)

        g = g_ref[...]
        dos = []
        for k, gt_ref in enumerate((gt0, gt1)):
            rows = slice(k * TM, (k + 1) * TM)
            ov = o_ref[rows, :]
            rstd = lax.rsqrt(jnp.mean(ov * ov, axis=-1, keepdims=True) + EPS)
            r = ov * rstd
            dx = dxn_ref[rows, :]
            dgate_ref[...] += jnp.where(_row_onehot(sel(2 * i + k)), jnp.sum(dx * (r * g), axis=0, keepdims=True), 0.0)
            dn = dx * gt_ref[0]
            dg_ref[...] += jnp.sum(dn * r, axis=0, keepdims=True)
            dr = dn * g
            dos.append((rstd * (dr - r * jnp.mean(dr * r, axis=-1, keepdims=True))).astype(bf16))
        dob = jnp.concatenate(dos, axis=0)
        dy_ref[...] = _dot_nt(dob, w_ref[...])
        y = jnp.concatenate([y0[...], y1[...], y2[...], y3[...]], axis=1)
        dw_ref[...] += _dot_tn(y, dob)

    def mrow(k):
        return pl.BlockSpec((1, 1, D), lambda i: (sel(2 * i + k), 0, 0))

    yspec = pl.BlockSpec((TP, BW), lambda i: (i, 0))
    row = pl.BlockSpec((TP, D), lambda i: (i, 0))
    return pl.pallas_call(
        body, name="outproj_bwd", grid=(n // TP,),
        in_specs=[row, row, mrow(0), mrow(1), _full((1, D)), yspec, yspec, yspec, yspec, _full((D, D))],
        out_specs=[row, _full((D, D)), _full((1, D)), _full((8, D))],
        out_shape=[_sds((n, D)), _sds((D, D)), _sds((1, D)), _sds((8, D))],
        compiler_params=_cparams(("arbitrary",), VMEM_BIG),
    )(dxn, o, gate_t, gate_t, g_post, *ys, w_out)


def inproj_bwd_x(dp, w_in, x, scale_t, g_pre, dxn, n_batch, sb):
    n = x.shape[0]

    def sel(i):
        return jnp.where(i % sb == 0, n_batch, i // sb)

    def body(dp_ref, w_ref, x_ref, sc0, sc1, g_ref, dxn_ref, dx_ref, dg_ref, dsh_ref, dsc_ref):
        i = pl.program_id(0)

        @pl.when(i == 0)
        def _():
            dg_ref[...] = jnp.zeros_like(dg_ref)
            dsh_ref[...] = jnp.zeros_like(dsh_ref)
            dsc_ref[...] = jnp.zeros_like(dsc_ref)

        dh_all = _dot_nt(dp_ref[...], w_ref[...])
        g = g_ref[...]
        for k, sc_ref in enumerate((sc0, sc1)):
            rows = slice(k * TM, (k + 1) * TM)
            dh = dh_all[rows]
            xv = x_ref[rows, :]
            rstd = lax.rsqrt(jnp.mean(xv * xv, axis=-1, keepdims=True) + EPS)
            r = xv * rstd
            hot = _row_onehot(sel(2 * i + k))
            dsh_ref[...] += jnp.where(hot, jnp.sum(dh, axis=0, keepdims=True), 0.0)
            dsc_ref[...] += jnp.where(hot, jnp.sum(dh * (r * g), axis=0, keepdims=True), 0.0)
            t = dh * (1.0 + sc_ref[0])
            dg_ref[...] += jnp.sum(t * r, axis=0, keepdims=True)
            dr = t * g
            dx_ref[rows, :] = dxn_ref[rows, :] + rstd * (dr - r * jnp.mean(dr * r, axis=-1, keepdims=True))

    def mrow(k):
        return pl.BlockSpec((1, 1, D), lambda i: (sel(2 * i + k), 0, 0))

    row = pl.BlockSpec((TP, D), lambda i: (i, 0))
    return pl.pallas_call(
        body, name="inproj_bwd_x", grid=(n // TP,),
        in_specs=[pl.BlockSpec((TP, PW), lambda i: (i, 0)), _full((D, PW)), row, mrow(0), mrow(1), _full((1, D)), row],
        out_specs=[row, _full((1, D)), _full((8, D)), _full((8, D))],
        out_shape=[_sds((n, D)), _sds((1, D)), _sds((8, D)), _sds((8, D))],
        compiler_params=_cparams(("arbitrary",), VMEM_BIG),
    )(dp, w_in, x, scale_t, scale_t, g_pre, dxn)


def dw_in(h, dp):
    n = h.shape[0]
    tk, tn = (1536 if n % 1536 == 0 else 512), 1024
    nk = n // tk

    def body(h_ref, dp_ref, o_ref, acc):
        k = pl.program_id(1)

        @pl.when(k == 0)
        def _():
            acc[...] = jnp.zeros_like(acc)
        acc[...] += _dot_tn(dp_ref[...], h_ref[...])

        @pl.when(k == nk - 1)
        def _():
            o_ref[...] = acc[...].astype(bf16)

    return pl.pallas_call(
        body, name="dw_in", grid=(PW // tn, nk),
        in_specs=[pl.BlockSpec((tk, D), lambda j, k: (k, 0)), pl.BlockSpec((tk, tn), lambda j, k: (k, j))],
        out_specs=pl.BlockSpec((tn, D), lambda j, k: (j, 0)),
        out_shape=_sds((PW, D), bf16),
        scratch_shapes=[pltpu.VMEM((tn, D), f32)],
        compiler_params=_cparams(("parallel", "arbitrary"), VMEM_BIG),
    )(h, dp)


def place_weights(slabs):
    n_ch, d, wc = slabs.shape

    def body(w_ref, o_ref):
        acc = jnp.pad(w_ref[0].astype(f32), ((0, 0), (0, PW - wc)))
        for k in range(1, n_ch):
            acc = acc + pltpu.roll(jnp.pad(w_ref[k].astype(f32), ((0, 0), (0, PW - wc))), wc * k, axis=1)
        o_ref[...] = acc.astype(bf16)

    return pl.pallas_call(
        body, name="place_weights", grid=(d // TM,),
        in_specs=[pl.BlockSpec((n_ch, TM, wc), lambda i: (0, i, 0))],
        out_specs=pl.BlockSpec((TM, PW), lambda i: (i, 0)),
        out_shape=_sds((d, PW), bf16),
        compiler_params=_cparams(("arbitrary",), VMEM_BIG),
    )(slabs)


def loss_head(xf, target, t_ctx):
    nb, s, _ = xf.shape
    jc = t_ctx // TM

    def body(x_ref, t_ref, dx_ref, l_ref):
        b, j = pl.program_id(0), pl.program_id(1)

        @pl.when((b == 0) & (j == 0))
        def _():
            l_ref[...] = jnp.zeros_like(l_ref)

        @pl.when(j < jc)
        def _():
            dx_ref[...] = jnp.zeros_like(dx_ref)

        @pl.when(j >= jc)
        def _():
            diff = x_ref[0] - t_ref[0]
            dx_ref[0] = diff * (1.0 / D)
            l_ref[...] += 0.5 * jnp.sum(diff * diff) * (1.0 / D)

    return pl.pallas_call(
        body, name="loss_head", grid=(nb, s // TM),
        in_specs=[pl.BlockSpec((1, TM, D), lambda b, j: (b, j, 0)),
                  pl.BlockSpec((1, TM, D), lambda b, j: (b, jnp.maximum(j - jc, 0), 0))],
        out_specs=[pl.BlockSpec((1, TM, D), lambda b, j: (b, j, 0)), _full((1, 128))],
        out_shape=[_sds((nb, s, D)), _sds((1, 128))],
        compiler_params=_cparams(("arbitrary", "arbitrary")),
    )(xf, target)


def _chunk_maps(n_ctx, n_lat):
    n = n_ctx + n_lat

    def cf(t):
        return t

    def cb(t):
        return jnp.where(t < n_ctx, n_ctx - 1 - t, n - 1 - t + n_ctx)
    return n, cf, cb


def ret_scan_fwd(p3, cos, sins, consts, t_ctx):
    nb, s, _ = p3.shape
    n, cf, cb = _chunk_maps(t_ctx // RC, (s - t_ctx) // RC)
    nz = 4 * nb

    def body(qf, kf, vf, qb, kb, vb, cosf, sinf, cosb, sinb, intra_r, qdec_r, kdec_r, cd_r, bd_r, bdr_r,
             of_ref, ob_ref, sall_ref, s_sc):
        @pl.when(pl.program_id(0) == 0)
        def _():
            s_sc[...] = jnp.zeros_like(s_sc)
        st = s_sc[...]
        sall_ref[0] = st
        s_new, o = _ret_step(st, _pairs(qf, qb, nb), _pairs(kf, kb, nb), _pairs(vf, vb, nb),
                             _pair_tables(cosf, cosb, nb), _pair_tables(sinf, sinb, nb), intra_r[...], qdec_r[...],
                             kdec_r[...], cd_r[...], bd_r[...], bdr_r[...])
        s_sc[...] = s_new
        _unpairs(o, of_ref, ob_ref, nb)

    def pspec(m, seg):
        return pl.BlockSpec((nb, RC, BW), lambda t: (0, m(t), seg))

    def tspec(m):
        return pl.BlockSpec((RC, PAIR_W), lambda t: (m(t), 0))

    return pl.pallas_call(
        body, name="ret_scan_fwd", grid=(n,),
        in_specs=[pspec(cf, 0), pspec(cf, 1), pspec(cf, 2), pspec(cb, 0), pspec(cb, 1), pspec(cb, 2),
                  tspec(cf), tspec(cf), tspec(cb), tspec(cb)] + [_full(c.shape) for c in consts],
        out_specs=[pl.BlockSpec((nb, RC, BW), lambda t: (0, cf(t), 0)),
                   pl.BlockSpec((nb, RC, BW), lambda t: (0, cb(t), 0)),
                   pl.BlockSpec((1, nz, PAIR_W, PAIR_W), lambda t: (t, 0, 0, 0))],
        out_shape=[_sds((nb, s, BW)), _sds((nb, s, BW)), _sds((n, nz, PAIR_W, PAIR_W))],
        scratch_shapes=[pltpu.VMEM((nz, PAIR_W, PAIR_W), f32)],
        compiler_params=_cparams(("arbitrary",)),
    )(p3, p3, p3, p3, p3, p3, cos, sins, cos, sins, *consts)


def ret_scan_bwd(p3, cos, sins, consts, s_all, do, t_ctx):
    nb, s, _ = p3.shape
    n, cf, cb = _chunk_maps(t_ctx // RC, (s - t_ctx) // RC)
    nz = 4 * nb

    def rf(t):
        return cf(n - 1 - t)

    def rb(t):
        return cb(n - 1 - t)

    def body(qf, kf, vf, qb, kb, vb, cosf, sinf, cosb, sinb, intra_r, qdec_r, kdec_r, cd_r, bd_r, bdr_r,
             sall_ref, dof, dob, dqf, dkf, dvf, dqb, dkb, dvb, ds_sc):
        @pl.when(pl.program_id(0) == 0)
        def _():
            ds_sc[...] = jnp.zeros_like(ds_sc)
        step = functools.partial(_ret_step, cos=_pair_tables(cosf, cosb, nb), sins=_pair_tables(sinf, sinb, nb),
                                 intra=intra_r[...], qdec=qdec_r[...], kdec=kdec_r[...], cd=cd_r[...], bd2=bd_r[...],
                                 bdr=bdr_r[...])
        _, vjp = jax.vjp(step, sall_ref[0], _pairs(qf, qb, nb), _pairs(kf, kb, nb), _pairs(vf, vb, nb))
        ds, dq, dk, dv = vjp((ds_sc[...], _pairs(dof, dob, nb)))
        ds_sc[...] = ds
        _unpairs(dq, dqf, dqb, nb)
        _unpairs(dk, dkf, dkb, nb)
        _unpairs(dv, dvf, dvb, nb)

    def pspec(m, seg):
        return pl.BlockSpec((nb, RC, BW), lambda t: (0, m(t), seg))

    def tspec(m):
        return pl.BlockSpec((RC, PAIR_W), lambda t: (m(t), 0))

    def ospec(m):
        return pl.BlockSpec((nb, RC, BW), lambda t: (0, m(t), 0))

    return pl.pallas_call(
        body, name="ret_scan_bwd", grid=(n,),
        in_specs=[pspec(rf, 0), pspec(rf, 1), pspec(rf, 2), pspec(rb, 0), pspec(rb, 1), pspec(rb, 2),
                  tspec(rf), tspec(rf), tspec(rb), tspec(rb)] + [_full(c.shape) for c in consts]
                 + [pl.BlockSpec((1, nz, PAIR_W, PAIR_W), lambda t: (n - 1 - t, 0, 0, 0)), ospec(rf), ospec(rb)],
        out_specs=[ospec(rf), ospec(rf), ospec(rf), ospec(rb), ospec(rb), ospec(rb)],
        out_shape=[_sds((nb, s, BW), bf16)] * 6,
        scratch_shapes=[pltpu.VMEM((nz, PAIR_W, PAIR_W), f32)],
        compiler_params=_cparams(("arbitrary",), VMEM_BIG),
    )(p3, p3, p3, p3, p3, p3, cos, sins, cos, sins, *consts, s_all, do, do)


def mix_finish_fwd(fn, name, o_f, o_b, p3, zseg, norm_g, bd):
    nb, s, _ = p3.shape

    def body(of_ref, ob_ref, z_ref, g_ref, bd_ref, y_ref):
        y_ref[0] = fn(of_ref[0], ob_ref[0], z_ref[0], g_ref[...], bd_ref[...]).astype(bf16)

    blk = pl.BlockSpec((1, TM, BW), lambda b, j: (b, j, 0))
    return pl.pallas_call(
        body, name=name, grid=(nb, s // TM),
        in_specs=[blk, blk, pl.BlockSpec((1, TM, BW), lambda b, j: (b, j, zseg)), _full((1, BW)), _full((BW, BW))],
        out_specs=blk, out_shape=_sds((nb, s, BW), bf16),
        compiler_params=_cparams(("arbitrary", "arbitrary")),
    )(o_f, o_b, p3, norm_g, bd)


def mix_finish_bwd(fn, name, o_f, o_b, p3, zseg, norm_g, bd, dy3, yseg):
    nb, s, _ = p3.shape

    def body(of_ref, ob_ref, z_ref, g_ref, bd_ref, dy_ref, do_ref, dz_ref, dg_ref):
        @pl.when((pl.program_id(0) == 0) & (pl.program_id(1) == 0))
        def _():
            dg_ref[...] = jnp.zeros_like(dg_ref)
        bdv = bd_ref[...]
        _, vjp = jax.vjp(lambda a, b, z, g: fn(a, b, z, g, bdv), of_ref[0], ob_ref[0], z_ref[0], g_ref[...])
        do, _, dz, dg = vjp(dy_ref[0])
        do_ref[0] = do
        dz_ref[0] = dz.astype(bf16)
        dg_ref[...] += dg

    blk = pl.BlockSpec((1, TM, BW), lambda b, j: (b, j, 0))
    return pl.pallas_call(
        body, name=name, grid=(nb, s // TM),
        in_specs=[blk, blk, pl.BlockSpec((1, TM, BW), lambda b, j: (b, j, zseg)), _full((1, BW)), _full((BW, BW)),
                  pl.BlockSpec((1, TM, BW), lambda b, j: (b, j, yseg))],
        out_specs=[blk, blk, _full((1, BW))],
        out_shape=[_sds((nb, s, BW)), _sds((nb, s, BW), bf16), _sds((1, BW))],
        compiler_params=_cparams(("arbitrary", "arbitrary")),
    )(o_f, o_b, p3, norm_g, bd, dy3)


def gdn_conv_fwd(p3, w, seg, t_ctx):
    nb, s, _ = p3.shape
    sd, su = _make_shifts(t_ctx, s)

    def body(x_ref, w_ref, o_ref):
        o_ref[0] = _silu(_conv3(x_ref[0], w_ref[...], sd, su))

    return pl.pallas_call(
        body, name="gdn_conv_fwd", grid=(nb, 2),
        in_specs=[pl.BlockSpec((1, s, 128), lambda b, j: (b, 0, 2 * seg + j)), pl.BlockSpec((3, 128), lambda b, j: (0, j))],
        out_specs=pl.BlockSpec((1, s, 128), lambda b, j: (b, 0, j)),
        out_shape=_sds((nb, s, BW)),
        compiler_params=_cparams(("arbitrary", "arbitrary")),
    )(p3, w)


def gdn_conv_bwd(p3, w, seg, d_f, d_b, t_ctx):
    nb, s, _ = p3.shape
    sd, su = _make_shifts(t_ctx, s)

    def body(x_ref, w_ref, df_ref, db_ref, dx_ref, dw_ref):
        @pl.when(pl.program_id(1) == 0)
        def _():
            dw_ref[...] = jnp.zeros_like(dw_ref)
        _, vjp = jax.vjp(lambda x, w_: _silu(_conv3(x, w_, sd, su)), x_ref[0], w_ref[...])
        dx, dw = vjp(df_ref[0] + db_ref[0])
        dx_ref[0] = dx.astype(bf16)
        dw_ref[...] += dw

    blk = pl.BlockSpec((1, s, 128), lambda j, b: (b, 0, j))
    return pl.pallas_call(
        body, name="gdn_conv_bwd", grid=(2, nb),
        in_specs=[pl.BlockSpec((1, s, 128), lambda j, b: (b, 0, 2 * seg + j)), pl.BlockSpec((3, 128), lambda j, b: (0, j)),
                  blk, blk],
        out_specs=[blk, pl.BlockSpec((3, 128), lambda j, b: (0, j))],
        out_shape=[_sds((nb, s, BW), bf16), _sds((3, BW))],
        compiler_params=_cparams(("arbitrary", "arbitrary"), VMEM_BIG),
    )(p3, w, d_f, d_b)


def _pairs(f_ref, b_ref, nb):
    return jnp.stack([r[b, :, PAIR_W * p:PAIR_W * (p + 1)] for b in range(nb) for r in (f_ref, b_ref) for p in range(2)])


def _pair_tables(f_ref, b_ref, nb):
    return jnp.stack([r[...] for _ in range(nb) for r in (f_ref, b_ref) for _ in range(2)])


def _gates(f_ref, b_ref, nb):
    return jnp.stack([r[b] for b in range(nb) for r in (f_ref, b_ref)])


def _unpairs(a, f_ref, b_ref, nb):
    for b in range(nb):
        for d, r in enumerate((f_ref, b_ref)):
            for p in range(2):
                r[b, :, PAIR_W * p:PAIR_W * (p + 1)] = a[4 * b + 2 * d + p].astype(r.dtype)


def _with_exchange(body, n_in, n_out, n_scratch, xchg, n_steps):
    if xchg is None:
        return body, [], [], [], []
    kind, arrs = xchg
    nx = len(arrs)

    def fused(*refs):
        ins, rest = refs[:n_in], refs[n_in:]
        srcs, rest = rest[:nx], rest[nx:]
        outs, rest = rest[:n_out], rest[n_out:]
        dsts, rest = rest[:nx], rest[nx:]
        scratch, sems = rest[:n_scratch], rest[n_scratch:]
        start, wait = _peer_exchange(kind, "chips", srcs, dsts, *sems)
        pl.when(pl.program_id(0) == 0)(start)
        body(*ins, *outs, *scratch)
        pl.when(pl.program_id(0) == n_steps - 1)(wait)

    any_ = pl.BlockSpec(memory_space=pl.ANY)
    return fused, [any_] * nx, [any_] * nx, _exchange_shapes(kind, "chips", arrs), _exchange_scratch("chips", nx)


def gdn_scan_fwd(cq, ck, cv, p3, alog, dtb, consts, t_ctx, xchg=None):
    nb, s, _ = p3.shape
    n, cf, cb = _chunk_maps(t_ctx // GC, (s - t_ctx) // GC)
    gblk = GATE_COL // 128

    nz = 4 * nb

    def body(qf, kf, vf, gf, qb, kb, vb, gb, al_ref, dt_ref, tm_r, tm2_r, st2_r, eg_r, eb_r, egt_r, dsel_r, eye_r, bd_r,
             of_ref, ob_ref, sall_ref, inv_ref, s_sc):
        @pl.when(pl.program_id(0) == 0)
        def _():
            s_sc[...] = jnp.zeros_like(s_sc)
        st = s_sc[...]
        sall_ref[0] = st
        s_new, o, inv = _gdn_step(st, _pairs(qf, qb, nb), _pairs(kf, kb, nb), _pairs(vf, vb, nb), _gates(gf, gb, nb),
                                  al_ref[...], dt_ref[...], tm_r[...], tm2_r[...], st2_r[...], eg_r[...], eb_r[...],
                                  egt_r[...], dsel_r[...], eye_r[...], bd_r[...])
        s_sc[...] = s_new
        inv_ref[0] = inv
        _unpairs(o, of_ref, ob_ref, nb)

    def cspec(m):
        return pl.BlockSpec((nb, GC, BW), lambda t: (0, m(t), 0))

    def gspec(m):
        return pl.BlockSpec((nb, GC, 128), lambda t: (0, m(t), gblk))

    fused, x_in, x_out, x_shape, x_scratch = _with_exchange(body, 10 + len(consts), 4, 1, xchg, n)
    return pl.pallas_call(
        fused, name="gdn_scan_fwd" + ("" if xchg is None else "_" + xchg[0]), grid=(n,),
        in_specs=[cspec(cf), cspec(cf), cspec(cf), gspec(cf), cspec(cb), cspec(cb), cspec(cb), gspec(cb),
                  _full((1, 128)), _full((1, 128))] + [_full(c.shape) for c in consts] + x_in,
        out_specs=[cspec(cf), cspec(cb), pl.BlockSpec((1, nz, PAIR_W, PAIR_W), lambda t: (t, 0, 0, 0)),
                   pl.BlockSpec((1, nz, GC, PAIR_W), lambda t: (t, 0, 0, 0))] + x_out,
        out_shape=[_sds((nb, s, BW)), _sds((nb, s, BW)), _sds((n, nz, PAIR_W, PAIR_W)), _sds((n, nz, GC, PAIR_W))]
                  + x_shape,
        scratch_shapes=[pltpu.VMEM((nz, PAIR_W, PAIR_W), f32)] + x_scratch,
        compiler_params=_cparams(("arbitrary",)),
    )(cq, ck, cv, p3, cq, ck, cv, p3, alog, dtb, *consts, *([] if xchg is None else xchg[1]))


def gdn_scan_bwd(cq, ck, cv, p3, alog, dtb, consts, s_all, inv_all, do, t_ctx, xchg=None):
    nb, s, _ = p3.shape
    n, cf, cb = _chunk_maps(t_ctx // GC, (s - t_ctx) // GC)
    gblk = GATE_COL // 128

    def rf(t):
        return cf(n - 1 - t)

    def rb(t):
        return cb(n - 1 - t)

    nz = 4 * nb

    def body(qf, kf, vf, gf, qb, kb, vb, gb, al_ref, dt_ref, tm_r, tm2_r, st2_r, eg_r, eb_r, egt_r, dsel_r, eye_r, bd_r,
             sall_ref, inv_ref, dof, dob, dqf, dkf, dvf, dgf, dqb, dkb, dvb, dgb, dal_ref, ddt_ref, ds_sc):
        @pl.when(pl.program_id(0) == 0)
        def _():
            dal_ref[...] = jnp.zeros_like(dal_ref)
            ddt_ref[...] = jnp.zeros_like(ddt_ref)
            ds_sc[...] = jnp.zeros_like(ds_sc)
        consts = dict(tmask=tm_r[...], tmask2=tm2_r[...], strict2=st2_r[...], exp_g=eg_r[...], exp_b=eb_r[...],
                      exp_gt=egt_r[...], dsel2=dsel_r[...], eye2=eye_r[...], bd2=bd_r[...], inv=inv_ref[0])

        def step(*a):
            return _gdn_step(*a, **consts)[:2]

        _, vjp = jax.vjp(step, sall_ref[0], _pairs(qf, qb, nb), _pairs(kf, kb, nb), _pairs(vf, vb, nb),
                         _gates(gf, gb, nb), al_ref[...], dt_ref[...])
        ds, dq, dk, dv, dg, dal, ddt = vjp((ds_sc[...], _pairs(dof, dob, nb)))
        ds_sc[...] = ds
        _unpairs(dq, dqf, dqb, nb)
        _unpairs(dk, dkf, dkb, nb)
        _unpairs(dv, dvf, dvb, nb)
        for b in range(nb):
            dgf[b] = dg[2 * b].astype(bf16)
            dgb[b] = dg[2 * b + 1].astype(bf16)
        dal_ref[...] += dal
        ddt_ref[...] += ddt

    def cspec(m):
        return pl.BlockSpec((nb, GC, BW), lambda t: (0, m(t), 0))

    def gspec(m):
        return pl.BlockSpec((nb, GC, 128), lambda t: (0, m(t), gblk))

    def gout(m):
        return pl.BlockSpec((nb, GC, 128), lambda t: (0, m(t), 0))

    fused, x_in, x_out, x_shape, x_scratch = _with_exchange(body, 14 + len(consts), 10, 1, xchg, n)
    return pl.pallas_call(
        fused, name="gdn_scan_bwd" + ("" if xchg is None else "_" + xchg[0]), grid=(n,),
        in_specs=[cspec(rf), cspec(rf), cspec(rf), gspec(rf), cspec(rb), cspec(rb), cspec(rb), gspec(rb),
                  _full((1, 128)), _full((1, 128))] + [_full(c.shape) for c in consts]
                 + [pl.BlockSpec((1, nz, PAIR_W, PAIR_W), lambda t: (n - 1 - t, 0, 0, 0)),
                    pl.BlockSpec((1, nz, GC, PAIR_W), lambda t: (n - 1 - t, 0, 0, 0)), cspec(rf), cspec(rb)] + x_in,
        out_specs=[cspec(rf), cspec(rf), cspec(rf), gout(rf), cspec(rb), cspec(rb), cspec(rb), gout(rb),
                   _full((1, 128)), _full((1, 128))] + x_out,
        out_shape=[_sds((nb, s, BW))] * 3 + [_sds((nb, s, 128), bf16)] + [_sds((nb, s, BW))] * 3 + [_sds((nb, s, 128), bf16)]
                  + [_sds((1, 128)), _sds((1, 128))] + x_shape,
        scratch_shapes=[pltpu.VMEM((nz, PAIR_W, PAIR_W), f32)] + x_scratch,
        compiler_params=_cparams(("arbitrary",), VMEM_BIG),
    )(cq, ck, cv, p3, cq, ck, cv, p3, alog, dtb, *consts, s_all, inv_all, do, do, *([] if xchg is None else xchg[1]))


def _sg_consts():
    hmp = np.zeros((2, NH, PAIR_W))
    for h in range(NH):
        hmp[h // 2, h, (h % 2) * HD:(h % 2 + 1) * HD] = 1.0
    bdr = (np.arange(2 * RC)[:, None] // RC == np.arange(PAIR_W)[None, :] // HD)
    return jnp.asarray(hmp, f32), jnp.asarray(bdr, f32)


def _sg_rows(s):
    return 6 * RC if s % (6 * RC) == 0 else 2 * RC


def _halves(ref):
    return ref[0, :, :PAIR_W], ref[0, :, PAIR_W:]


def sg_fwd(p3, w, b, hmp, bdr):
    nb, s, _ = p3.shape
    ts = _sg_rows(s)

    def body(u_ref, v_ref, z_ref, w_ref, b_ref, hm_ref, bdr_ref, y_ref):
        y0, y1 = _sg_block(*_halves(u_ref), *_halves(v_ref), *_halves(z_ref), w_ref[...], b_ref[...], hm_ref[...],
                           bdr_ref[...])
        y_ref[0, :, :PAIR_W] = y0.astype(bf16)
        y_ref[0, :, PAIR_W:] = y1.astype(bf16)

    def seg(k):
        return pl.BlockSpec((1, ts, BW), lambda bi, i: (bi, i, k))

    return pl.pallas_call(
        body, name="sg_fwd", grid=(nb, s // ts),
        in_specs=[seg(4), seg(5), seg(6), _full((NH, RC, RC)), _full((NH, RC)), _full(hmp.shape), _full(bdr.shape)],
        out_specs=pl.BlockSpec((1, ts, BW), lambda bi, i: (bi, i, 0)),
        out_shape=_sds((nb, s, BW), bf16),
        compiler_params=_cparams(("arbitrary", "arbitrary")),
    )(p3, p3, p3, w, b, hmp, bdr)


def sg_bwd(p3, w, b, hmp, bdr, dy3):
    nb, s, _ = p3.shape
    ts = _sg_rows(s)

    def body(u_ref, v_ref, z_ref, w_ref, b_ref, hm_ref, bdr_ref, dy_ref, du_ref, dv_ref, dz_ref, dw_ref, db_ref):
        @pl.when((pl.program_id(0) == 0) & (pl.program_id(1) == 0))
        def _():
            dw_ref[...] = jnp.zeros_like(dw_ref)
            db_ref[...] = jnp.zeros_like(db_ref)
        hm, bdr_v = hm_ref[...], bdr_ref[...]
        _, vjp = jax.vjp(lambda *a: _sg_block(*a, hm, bdr_v), *_halves(u_ref), *_halves(v_ref), *_halves(z_ref),
                         w_ref[...], b_ref[...])
        du0, du1, dv0, dv1, dz0, dz1, dw, db = vjp(_halves(dy_ref))
        for ref, a0, a1 in ((du_ref, du0, du1), (dv_ref, dv0, dv1), (dz_ref, dz0, dz1)):
            ref[0, :, :PAIR_W] = a0.astype(bf16)
            ref[0, :, PAIR_W:] = a1.astype(bf16)
        dw_ref[...] += dw
        db_ref[...] += db

    def seg(k):
        return pl.BlockSpec((1, ts, BW), lambda bi, i: (bi, i, k))

    blk = pl.BlockSpec((1, ts, BW), lambda bi, i: (bi, i, 0))
    return pl.pallas_call(
        body, name="sg_bwd", grid=(nb, s // ts),
        in_specs=[seg(4), seg(5), seg(6), _full((NH, RC, RC)), _full((NH, RC)), _full(hmp.shape), _full(bdr.shape),
                  seg(1)],
        out_specs=[blk, blk, blk, _full((NH, RC, RC)), _full((NH, RC))],
        out_shape=[_sds((nb, s, BW), bf16)] * 3 + [_sds((NH, RC, RC)), _sds((NH, RC))],
        compiler_params=_cparams(("arbitrary", "arbitrary"), VMEM_BIG),
    )(p3, p3, p3, w, b, hmp, bdr, dy3)


def _sc_fn(b, c, h, z, w, sd, su):
    return b * _conv3(c * h, w, sd, su) * _silu(z)


def sc_fwd(p3, w, t_ctx):
    nb, s, _ = p3.shape
    sd, su = _make_shifts(t_ctx, s)

    def body(b_ref, c_ref, h_ref, z_ref, w_ref, y_ref):
        y_ref[0] = _sc_fn(b_ref[0], c_ref[0], h_ref[0], z_ref[0], w_ref[...], sd, su).astype(bf16)

    def seg(k):
        return pl.BlockSpec((1, s, 128), lambda bi, j: (bi, 0, 2 * k + j))

    return pl.pallas_call(
        body, name="sc_fwd", grid=(nb, 2),
        in_specs=[seg(7), seg(8), seg(9), seg(10), pl.BlockSpec((3, 128), lambda bi, j: (0, j))],
        out_specs=pl.BlockSpec((1, s, 128), lambda bi, j: (bi, 0, j)),
        out_shape=_sds((nb, s, BW), bf16),
        compiler_params=_cparams(("arbitrary", "arbitrary"), VMEM_BIG),
    )(p3, p3, p3, p3, w)


def sc_bwd(p3, w, dy3, t_ctx):
    nb, s, _ = p3.shape
    sd, su = _make_shifts(t_ctx, s)

    def body(b_ref, c_ref, h_ref, z_ref, w_ref, dy_ref, db_ref, dc_ref, dh_ref, dz_ref, dw_ref):
        @pl.when(pl.program_id(1) == 0)
        def _():
            dw_ref[...] = jnp.zeros_like(dw_ref)
        _, vjp = jax.vjp(lambda b, c, h, z, w_: _sc_fn(b, c, h, z, w_, sd, su),
                         b_ref[0], c_ref[0], h_ref[0], z_ref[0], w_ref[...])
        db, dc, dh, dz, dw = vjp(dy_ref[0])
        db_ref[0] = db.astype(bf16)
        dc_ref[0] = dc.astype(bf16)
        dh_ref[0] = dh.astype(bf16)
        dz_ref[0] = dz.astype(bf16)
        dw_ref[...] += dw

    def seg(k):
        return pl.BlockSpec((1, s, 128), lambda j, bi: (bi, 0, 2 * k + j))

    blk = pl.BlockSpec((1, s, 128), lambda j, bi: (bi, 0, j))
    wspec = pl.BlockSpec((3, 128), lambda j, bi: (0, j))
    return pl.pallas_call(
        body, name="sc_bwd", grid=(2, nb),
        in_specs=[seg(7), seg(8), seg(9), seg(10), wspec, seg(2)],
        out_specs=[blk, blk, blk, blk, wspec],
        out_shape=[_sds((nb, s, BW), bf16)] * 4 + [_sds((3, BW))],
        compiler_params=_cparams(("arbitrary", "arbitrary"), VMEM_BIG),
    )(p3, p3, p3, p3, w, dy3)


def assemble_dp(pairs, singles_a, gdn_x, singles_b, gates):
    nb, s, _ = singles_a[0].shape
    flat = [a for pr in pairs for a in pr] + list(singles_a) + list(gdn_x) + list(singles_b) + list(gates)
    n_pairs, n_a, n_x, n_b = len(pairs), len(singles_a), len(gdn_x), len(singles_b)

    def body(*refs):
        out = refs[-1]
        ins = refs[:-1]
        col = 0
        for p in range(n_pairs):
            out[0, :, col:col + BW] = (ins[2 * p][0].astype(f32) + ins[2 * p + 1][0].astype(f32)).astype(bf16)
            col += BW
        k = 2 * n_pairs
        for _ in range(n_a + n_x + n_b):
            out[0, :, col:col + BW] = ins[k][0]
            col += BW
            k += 1
        out[0, :, col:col + 128] = (ins[k][0].astype(f32) + ins[k + 1][0].astype(f32)).astype(bf16)
        out[0, :, col + 128:] = jnp.zeros((TM, PW - col - 128), bf16)

    def spec(a):
        return pl.BlockSpec((1, TM, a.shape[-1]), lambda b, j: (b, j, 0))

    return pl.pallas_call(
        body, name="assemble_dp", grid=(nb, s // TM),
        in_specs=[spec(a) for a in flat],
        out_specs=pl.BlockSpec((1, TM, PW), lambda b, j: (b, j, 0)),
        out_shape=_sds((nb, s, PW), bf16),
        compiler_params=_cparams(("arbitrary", "arbitrary")),
    )(*flat)


def mod_fwd(c_rows, w_mod, b_cols):
    nl, _, wc = w_mod.shape
    nr = c_rows.shape[0]

    def body(c_ref, w_ref, b_ref, o_ref):
        o_ref[0] = _dot(_silu(c_ref[...]), w_ref[0], precision=HI) + b_ref[0]

    return pl.pallas_call(
        body, name="mod_fwd", grid=(nl,),
        in_specs=[_full((nr, D)), pl.BlockSpec((1, D, wc), lambda l: (l, 0, 0)), pl.BlockSpec((1, 1, wc), lambda l: (l, 0, 0))],
        out_specs=pl.BlockSpec((1, nr, wc), lambda l: (l, 0, 0)),
        out_shape=_sds((nl, nr, wc)),
        compiler_params=_cparams(("arbitrary",)),
    )(c_rows, w_mod, b_cols)


def mod_bwd(c_rows, w_mod, dm_cols, dm_full):
    nl, _, wc = w_mod.shape
    nr = c_rows.shape[0]

    def body(c_ref, w_ref, dmc_ref, dmf_ref, gw_ref, gb_ref, dcc_ref):
        @pl.when(pl.program_id(0) == 0)
        def _():
            dcc_ref[...] = jnp.zeros_like(dcc_ref)
        a = _silu(c_ref[...])
        dmc = dmc_ref[0]
        gw_ref[0] = _dot_tn(a, dmc, precision=HI)
        gb_ref[0] = jnp.sum(dmf_ref[0], axis=0, keepdims=True)
        dcc_ref[...] += _dot_nt(dmc[nr - 8:nr], w_ref[0], precision=HI)

    return pl.pallas_call(
        body, name="mod_bwd", grid=(nl,),
        in_specs=[_full((nr, D)), pl.BlockSpec((1, D, wc), lambda l: (l, 0, 0)),
                  pl.BlockSpec((1, nr, wc), lambda l: (l, 0, 0)), pl.BlockSpec((1, nr, 3 * D), lambda l: (l, 0, 0))],
        out_specs=[pl.BlockSpec((1, D, wc), lambda l: (l, 0, 0)), pl.BlockSpec((1, 1, 3 * D), lambda l: (l, 0, 0)),
                   _full((8, D))],
        out_shape=[_sds((nl, D, wc)), _sds((nl, 1, 3 * D)), _sds((8, D))],
        compiler_params=_cparams(("arbitrary",)),
    )(c_rows, w_mod, dm_cols, dm_full)


def cctx_grad(parts, c_ctx):
    def body(p_ref, c_ref, o_ref):
        tot = p_ref[0, 0:1, :]
        for k in (2, 4, 6):
            tot = tot + p_ref[k, 0:1, :]
        c = c_ref[...]
        sg = jax.nn.sigmoid(c)
        o_ref[...] = tot * (sg * (1.0 + c * (1.0 - sg)))

    return pl.pallas_call(body, name="cctx_grad", out_shape=_sds((1, D)))(parts, c_ctx)


def sum_lead(x, out_dtype=f32, tr=256, tc=None):
    k, r, c = x.shape
    tr = min(tr, r)
    tc = c if tc is None else tc
    assert r % tr == 0 and c % tc == 0

    def body(x_ref, o_ref):
        tot = x_ref[0].astype(f32)
        for i in range(1, k):
            tot = tot + x_ref[i].astype(f32)
        o_ref[...] = tot.astype(out_dtype)

    return pl.pallas_call(
        body, name="sum_lead", grid=(r // tr, c // tc),
        in_specs=[pl.BlockSpec((k, tr, tc), lambda i, j: (0, i, j))],
        out_specs=pl.BlockSpec((tr, tc), lambda i, j: (i, j)),
        out_shape=_sds((r, c), out_dtype),
        compiler_params=_cparams(("arbitrary", "arbitrary")),
    )(x)


def adamw(w, m, v, g1, g2=None, tr=256, block=None):
    if block is None:
        block = (1,) * (w.ndim - 2) + (min(tr, w.shape[-2]), w.shape[-1])
    assert len(block) == w.ndim and all(d % b == 0 for d, b in zip(w.shape, block))
    two = g2 is not None
    c1 = 1.0 / (1.0 - ADAM_B1 ** ADAM_STEP)
    c2 = 1.0 / (1.0 - ADAM_B2 ** ADAM_STEP)

    def body(*refs):
        w_ref, m_ref, v_ref, g_ref = refs[:4]
        g = g_ref[...]
        if two:
            g = g + refs[4][...]
        go_ref, d_ref, mo_ref, vo_ref = refs[-4:]
        mn = ADAM_B1 * m_ref[...] + (1.0 - ADAM_B1) * g
        vn = ADAM_B2 * v_ref[...] + (1.0 - ADAM_B2) * (g * g)
        go_ref[...] = g
        mo_ref[...] = mn
        vo_ref[...] = vn
        d_ref[...] = -ADAM_LR * ((mn * c1) / (jnp.sqrt(vn * c2) + ADAM_EPS) + ADAM_WD * w_ref[...])

    blk = pl.BlockSpec(block, lambda *i: i)
    grid = tuple(d // b for d, b in zip(w.shape, block))
    args = [w, m, v, g1] + ([g2] if two else [])
    return pl.pallas_call(
        body, name="adamw", grid=grid,
        in_specs=[blk] * len(args), out_specs=[blk] * 4, out_shape=[_sds(w.shape)] * 4,
        compiler_params=_cparams(("arbitrary",) * len(grid)),
    )(*args)


def _my_pos():
    return lax.axis_index("x"), lax.axis_index("y"), lax.axis_index("c")


GROUP_SIZE = {"devices": N_DEV, "chips": N_CHIPS, "cores": 2}


def _peer_exchange(kind, group, src_refs, dst_refs, send_sems, recv_sems, local_sems):
    mx, my, mc = _my_pos()
    n = GROUP_SIZE[group]
    if group == "devices":
        me = 4 * mx + 2 * my + mc
    elif group == "chips":
        me = 2 * mx + my
    else:
        me = mc

    def peer(k):
        if group == "devices":
            return (mx ^ (k >> 2), my ^ ((k >> 1) & 1), mc ^ (k & 1))
        if group == "chips":
            return (mx ^ (k >> 1), my ^ (k & 1), mc)
        return (mx, my, mc ^ k)

    def copies():
        local, sends, recvs = [], [], []
        for i, (src, dst) in enumerate(zip(src_refs, dst_refs)):
            def part(k):
                return src.at[k] if kind == "scatter" else src

            def slab(k):
                return dst if kind == "send" else dst.at[k]

            if kind != "send":
                local.append(pltpu.make_async_copy(part(me), dst.at[me], local_sems.at[i]))
            for k in range(1, n):
                sem = dict(send_sem=send_sems.at[i, k - 1], recv_sem=recv_sems.at[i, k - 1], device_id_type=MESH)
                sends.append(pltpu.make_async_remote_copy(src_ref=part(me ^ k), dst_ref=slab(me), device_id=peer(k), **sem))
                recvs.append(pltpu.make_async_remote_copy(src_ref=part(me ^ k), dst_ref=slab(me ^ k),
                                                          device_id=(mx, my, mc), **sem))
        return local, sends, recvs

    def start():
        local, sends, _ = copies()
        for cp in local + sends:
            cp.start()

    def wait():
        local, sends, recvs = copies()
        for cp in recvs:
            cp.wait_recv()
        for cp in sends:
            cp.wait_send()
        for cp in local:
            cp.wait()

    return start, wait


def _exchange_scratch(group, n):
    k = GROUP_SIZE[group] - 1
    return [pltpu.SemaphoreType.DMA((n, k)), pltpu.SemaphoreType.DMA((n, k)), pltpu.SemaphoreType.DMA((n,))]


def _exchange_shapes(kind, group, arrs):
    return [_sds(((GROUP_SIZE[group],) + a.shape) if kind == "gather" else a.shape, a.dtype) for a in arrs]


def exchange(name, parts):
    counts = [len(arrs) for _, _, arrs in parts]
    total = sum(counts)

    def body(*refs):
        srcs, dsts, sems = refs[:total], refs[total:2 * total], refs[2 * total:]
        ops, at = [], 0
        for j, (kind, group, arrs) in enumerate(parts):
            ops.append(_peer_exchange(kind, group, srcs[at:at + counts[j]], dsts[at:at + counts[j]], *sems[3 * j:3 * j + 3]))
            at += counts[j]
        for start, _ in ops:
            start()
        for _, wait in ops:
            wait()

    any_ = pl.BlockSpec(memory_space=pl.ANY)
    flat = [a for _, _, arrs in parts for a in arrs]
    outs = pl.pallas_call(
        body, name=name, out_shape=[sh for kind, group, arrs in parts for sh in _exchange_shapes(kind, group, arrs)],
        in_specs=[any_] * total, out_specs=[any_] * total,
        scratch_shapes=[sc for _, group, arrs in parts for sc in _exchange_scratch(group, len(arrs))],
    )(*flat)
    res, at = [], 0
    for cnt in counts:
        res.append(list(outs[at:at + cnt]))
        at += cnt
    return res


def gather8(x):
    return exchange("gather8", [("gather", "devices", [x])])[0][0]


PACK_ROWS = 64
SMALL = ("c_ctx", "b_mod", "g_pre", "g_post", "ret_norm_g", "sg_w", "sg_b", "sc_conv_w", "gdn_conv_w",
         "gdn_a_log", "gdn_dt_bias", "gdn_norm_g")


def _pack(arrs, width=D, mult=PACK_ROWS):
    rows = []
    for a in arrs:
        flat = a.reshape(-1)
        pad = (-flat.shape[0]) % width
        rows.append(jnp.pad(flat, (0, pad)).reshape(-1, width))
    out = jnp.concatenate(rows, axis=0)
    return jnp.pad(out, ((0, (-out.shape[0]) % mult), (0, 0)))


def _unpack(packed, shapes, width=D):
    outs, r = [], 0
    for shp in shapes:
        size = int(np.prod(shp))
        nr = -(-size // width)
        outs.append(packed[r:r + nr].reshape(-1)[:size].reshape(shp))
        r += nr
    return outs


def kernel(x, c, ctx, c_ctx, w_mod, b_mod, g_pre, g_post, w_in, w_out, ret_norm_g, sg_w, sg_b, sc_conv_w, gdn_conv_w, gdn_a_log, gdn_dt_bias, gdn_norm_g, loss_target, m_c_ctx, m_w_mod, m_b_mod, m_g_pre, m_g_post, m_w_in, m_w_out, m_ret_norm_g, m_sg_w, m_sg_b, m_sc_conv_w, m_gdn_conv_w, m_gdn_a_log, m_gdn_dt_bias, m_gdn_norm_g, v_c_ctx, v_w_mod, v_b_mod, v_g_pre, v_g_post, v_w_in, v_w_out, v_ret_norm_g, v_sg_w, v_sg_b, v_sc_conv_w, v_gdn_conv_w, v_gdn_a_log, v_gdn_dt_bias, v_gdn_norm_g):
    weights = dict(c_ctx=c_ctx, w_mod=w_mod, b_mod=b_mod, g_pre=g_pre, g_post=g_post, w_in=w_in, w_out=w_out,
                   ret_norm_g=ret_norm_g, sg_w=sg_w, sg_b=sg_b, sc_conv_w=sc_conv_w, gdn_conv_w=gdn_conv_w,
                   gdn_a_log=gdn_a_log, gdn_dt_bias=gdn_dt_bias, gdn_norm_g=gdn_norm_g)
    mom = dict(c_ctx=m_c_ctx, w_mod=m_w_mod, b_mod=m_b_mod, g_pre=m_g_pre, g_post=m_g_post, w_in=m_w_in,
               w_out=m_w_out, ret_norm_g=m_ret_norm_g, sg_w=m_sg_w, sg_b=m_sg_b, sc_conv_w=m_sc_conv_w,
               gdn_conv_w=m_gdn_conv_w, gdn_a_log=m_gdn_a_log, gdn_dt_bias=m_gdn_dt_bias, gdn_norm_g=m_gdn_norm_g)
    var = dict(c_ctx=v_c_ctx, w_mod=v_w_mod, b_mod=v_b_mod, g_pre=v_g_pre, g_post=v_g_post, w_in=v_w_in,
               w_out=v_w_out, ret_norm_g=v_ret_norm_g, sg_w=v_sg_w, sg_b=v_sg_b, sc_conv_w=v_sc_conv_w,
               gdn_conv_w=v_gdn_conv_w, gdn_a_log=v_gdn_a_log, gdn_dt_bias=v_gdn_dt_bias, gdn_norm_g=v_gdn_norm_g)

    nb, t_lat, _ = x.shape
    t_ctx = ctx.shape[1]
    s = t_ctx + t_lat
    n = nb * s
    sb = s // TM
    nl = w_in.shape[0]
    wc_in = w_in.shape[2]
    wc_mod = w_mod.shape[2]
    rows_out = w_out.shape[1]
    n_all = nb * N_DEV
    mx, my, mc = _my_pos()
    chip = 2 * mx + my
    dev = 2 * chip + mc

    sg_c = _sg_consts()
    bd = jnp.asarray(_block_diag())
    ret_c = _ret_consts(nb)
    gdn_c = _gdn_consts(nb)
    cos, sins = _rope_tables(t_lat, t_ctx)

    w_in_b, w_out_b = w_in.astype(bf16), w_out.astype(bf16)
    pre = _pack([c, sc_conv_w, gdn_conv_w], mult=8)
    (pre_all,), w0_parts = exchange("startup_gather", [("gather", "devices", [pre]),
                                                       ("gather", "chips", [w_in_b[0], w_out_b[0]])])
    c_parts, scw_parts, gcw_parts = [], [], []
    for k in range(N_DEV):
        ck, sk, gk = _unpack(pre_all[k], [c.shape, sc_conv_w.shape, gdn_conv_w.shape])
        c_parts.append(ck)
        if k % 2 == 0:
            scw_parts.append(sk)
            gcw_parts.append(gk)
    c_all = jnp.concatenate(c_parts, axis=0)
    sc_w_full = jnp.concatenate(scw_parts, axis=-1)
    gdn_w_full = jnp.concatenate(gcw_parts, axis=-1)
    c_rows = jnp.concatenate([c_all, c_ctx[None, :], jnp.zeros((7, D), f32)], axis=0)

    b_cols = lax.dynamic_slice_in_dim(b_mod, chip * wc_mod, wc_mod, axis=1)[:, None, :]
    mod_part = mod_fwd(c_rows, w_mod, b_cols)
    mod_all = gather8(mod_part)
    mod = jnp.concatenate([mod_all[2 * k] for k in range(N_CHIPS)], axis=-1)
    my_rows = jnp.concatenate([lax.dynamic_slice_in_dim(mod, dev * nb, nb, axis=1), mod[:, n_all:n_all + 1]], axis=1)
    shift_t = my_rows[:, :, None, 0:D]
    scale_t = my_rows[:, :, None, D:2 * D]
    gate_t = my_rows[:, :, None, 2 * D:3 * D]

    def full_weights(parts):
        return place_weights(parts[0]), parts[1].reshape(D, D)

    w_in_full, w_out_full = [None] * nl, [None] * nl
    w_in_full[0], w_out_full[0] = full_weights(w0_parts)

    alog = jnp.pad(gdn_a_log.reshape(nl, 1, 8), ((0, 0), (0, 0), (0, 120)))
    dtb = jnp.pad(gdn_dt_bias.reshape(nl, 1, 8), ((0, 0), (0, 0), (0, 120)))
    gdn_ng = jnp.tile(gdn_norm_g, (1, NH))[:, None, :]
    ret_ng = ret_norm_g[:, None, :]

    xs = jnp.concatenate([ctx, x], axis=1).reshape(n, D)
    saved = []
    for l in range(nl):
        p, h = inproj_fwd(xs, shift_t[l], scale_t[l], g_pre[l][None, :], w_in_full[l], nb, sb)
        p3 = p.reshape(nb, s, PW)
        ro_f, ro_b, rs_all = ret_scan_fwd(p3, cos, sins, ret_c, t_ctx)
        y_ret = mix_finish_fwd(_ret_finish, "ret_finish_fwd", ro_f, ro_b, p3, 3, ret_ng[l], bd)
        y_sg = sg_fwd(p3, sg_w[l], sg_b[l], *sg_c)
        y_sc = sc_fwd(p3, sc_w_full[l], t_ctx)
        cq, ck, cv = [gdn_conv_fwd(p3, gdn_w_full[l][:, BW * i:BW * (i + 1)], 11 + i, t_ctx) for i in range(3)]
        nxt = None if l + 1 == nl else ("gather", [w_in_b[l + 1], w_out_b[l + 1]])
        go_f, go_b, *gs_all = gdn_scan_fwd(cq, ck, cv, p3, alog[l], dtb[l], gdn_c, t_ctx, nxt)
        if nxt is not None:
            w_in_full[l + 1], w_out_full[l + 1] = full_weights(gs_all[2:])
            gs_all = gs_all[:2]
        y_gdn = mix_finish_fwd(_gdn_finish, "gdn_finish_fwd", go_f, go_b, p3, 14, gdn_ng[l], bd)
        ys = [a.reshape(n, BW) for a in (y_ret, y_sg, y_sc, y_gdn)]
        x_new, o = outproj_fwd(ys, w_out_full[l], xs, gate_t[l], g_post[l][None, :], nb, sb)
        saved.append(dict(x=xs, h=h, p3=p3, ro=(ro_f, ro_b), rs=rs_all, c=(cq, ck, cv), go=(go_f, go_b), gs=gs_all,
                          ys=ys, o=o))
        xs = x_new

    dx3, loss_part = loss_head(xs.reshape(nb, s, D), loss_target, t_ctx)
    loss = lax.psum(loss_part[0, 0], ("x", "y", "c"))

    dxs = dx3.reshape(n, D)
    g_small = {k: [None] * nl for k in SMALL if k not in ("c_ctx", "b_mod")}
    dm_rows = [None] * nl
    slabs = None
    got_in, got_out = [None] * nl, [None] * nl
    for l in reversed(range(nl)):
        sv = saved[l]
        p3 = sv["p3"]
        dy, gw_out, dg_post, dgate = outproj_bwd(dxs, sv["o"], gate_t[l], g_post[l][None, :], sv["ys"], w_out_full[l], nb, sb)
        dy3 = dy.reshape(nb, s, D)
        r_do, r_dz, d_rng = mix_finish_bwd(_ret_finish, "ret_finish_bwd", *sv["ro"], p3, 3, ret_ng[l], bd, dy3, 0)
        r_d = ret_scan_bwd(p3, cos, sins, ret_c, sv["rs"], r_do, t_ctx)
        s_du, s_dv, s_dz, d_sgw, d_sgb = sg_bwd(p3, sg_w[l], sg_b[l], *sg_c, dy3)
        c_db, c_dc, c_dh, c_dz, d_scw = sc_bwd(p3, sc_w_full[l], dy3, t_ctx)
        g_do, g_dz, d_gng = mix_finish_bwd(_gdn_finish, "gdn_finish_bwd", *sv["go"], p3, 14, gdn_ng[l], bd, dy3, 3)
        g_d = gdn_scan_bwd(*sv["c"], p3, alog[l], dtb[l], gdn_c, *sv["gs"], g_do, t_ctx,
                           None if slabs is None else ("scatter", slabs))
        if slabs is not None:
            got_in[l + 1], got_out[l + 1] = g_d[10:]
        gx, d_gcw = [], []
        for i in range(3):
            dxi, dwi = gdn_conv_bwd(p3, gdn_w_full[l][:, BW * i:BW * (i + 1)], 11 + i, g_d[i], g_d[4 + i], t_ctx)
            gx.append(dxi)
            d_gcw.append(dwi)
        dp3 = assemble_dp([(r_d[0], r_d[3]), (r_d[1], r_d[4]), (r_d[2], r_d[5])],
                          [r_dz, s_du, s_dv, s_dz, c_db, c_dc, c_dh, c_dz], gx, [g_dz], [g_d[3], g_d[7]])
        dp = dp3.reshape(n, PW)
        dxs, dg_pre, dshift, dscale = inproj_bwd_x(dp, w_in_full[l], sv["x"], scale_t[l], g_pre[l][None, :], dxs, nb, sb)
        gw_in = dw_in(sv["h"], dp)
        slabs = [jnp.stack([gw_in[k * wc_in:(k + 1) * wc_in] for k in range(N_CHIPS)]),
                 gw_out.reshape(N_CHIPS, rows_out, D).astype(bf16)]
        g_small["g_pre"][l] = dg_pre[0]
        g_small["g_post"][l] = dg_post[0]
        g_small["ret_norm_g"][l] = d_rng[0]
        g_small["sg_w"][l] = d_sgw
        g_small["sg_b"][l] = d_sgb
        g_small["sc_conv_w"][l] = d_scw
        g_small["gdn_conv_w"][l] = jnp.concatenate(d_gcw, axis=-1)
        g_small["gdn_a_log"][l] = g_d[8][0, :8].reshape(2, NH)
        g_small["gdn_dt_bias"][l] = g_d[9][0, :8].reshape(2, NH)
        g_small["gdn_norm_g"][l] = d_gng[0].reshape(NH, HD)
        dm_rows[l] = jnp.concatenate([dshift, dscale, dgate], axis=-1)[:nb + 1]
    grad_x = dxs.reshape(nb, s, D)[:, t_ctx:, :]

    g_small = {k: jnp.stack(v) for k, v in g_small.items()}
    dm_rows = jnp.stack(dm_rows)
    names2 = [k for k in SMALL if k not in ("c_ctx", "b_mod")]
    pack_sum = _pack([g_small[k] for k in names2] + [dm_rows[:, nb:]])
    pack_own = _pack([dm_rows[:, :nb]], mult=8)
    rs = -(-pack_sum.shape[0] // (8 * N_DEV)) * 8
    slabs_sum = jnp.pad(pack_sum, ((0, N_DEV * rs - pack_sum.shape[0]), (0, 0))).reshape(N_DEV, rs, D)
    (got_small,), (got_in[0], got_out[0]) = exchange("tail_scatter", [("scatter", "devices", [slabs_sum]),
                                                                      ("scatter", "chips", slabs)])
    my_slab = sum_lead(got_small, tr=rs)
    gin_mine = jnp.stack([sum_lead(a, tr=wc_in, tc=256) for a in got_in], axis=1)
    gout_mine = jnp.stack([sum_lead(a) for a in got_out])
    (all2,), (gin_sib, gout_sib) = exchange("tail_gather", [
        ("gather", "devices", [jnp.concatenate([my_slab, pack_own], axis=0)]), ("send", "cores", [gin_mine, gout_mine])])
    tot2 = all2[:, :rs].reshape(N_DEV * rs, D)
    outs2 = _unpack(tot2, [g_small[k].shape for k in names2] + [(nl, 1, 3 * D)])
    grads = dict(zip(names2, outs2[:-1]))
    dm_own = jnp.stack([_unpack(all2[k, rs:], [(nl, nb, 3 * D)])[0] for k in range(N_DEV)])
    dm_own = jnp.transpose(dm_own, (1, 0, 2, 3)).reshape(nl, n_all, 3 * D)
    dm_all = jnp.concatenate([dm_own, jnp.pad(outs2[-1], ((0, 0), (0, 7), (0, 0)))], axis=1)
    grads["gdn_norm_g"] = sum_lead(jnp.transpose(grads["gdn_norm_g"], (1, 0, 2)), tr=nl)
    for k in ("sc_conv_w", "gdn_conv_w"):
        wc = weights[k].shape[2]
        grads[k] = lax.dynamic_slice_in_dim(grads[k], chip * wc, wc, axis=2)

    dm_cols = lax.dynamic_slice_in_dim(dm_all, chip * wc_mod, wc_mod, axis=2)
    g_w_mod, g_b_mod, dcc_part = mod_bwd(c_rows, w_mod, dm_cols, dm_all)
    grads["b_mod"] = g_b_mod[:, 0, :]
    grads["c_ctx"] = cctx_grad(gather8(dcc_part), c_ctx[None, :])[0]

    res = {}
    w_in_t, m_in_t, v_in_t = [jnp.transpose(a, (2, 0, 1)) for a in (w_in, m_w_in, v_w_in)]
    res["w_in"] = [jnp.transpose(a, (1, 2, 0)) for a in
                   adamw(w_in_t, m_in_t, v_in_t, gin_mine, gin_sib, block=(wc_in // 4, nl, 256))]
    res["w_out"] = adamw(w_out, m_w_out, v_w_out, gout_mine, gout_sib)
    res["w_mod"] = adamw(w_mod, m_w_mod, v_w_mod, g_w_mod)
    shapes = [weights[k].shape for k in SMALL]
    small = adamw(_pack([weights[k] for k in SMALL]), _pack([mom[k] for k in SMALL]), _pack([var[k] for k in SMALL]),
                  _pack([grads[k].reshape(weights[k].shape) for k in SMALL]), tr=PACK_ROWS)
    small = [_unpack(a, shapes) for a in small]
    for i, k in enumerate(SMALL):
        res[k] = [small[j][i] for j in range(4)]

    order = ["c_ctx", "w_mod", "b_mod", "g_pre", "g_post", "w_in", "w_out", "ret_norm_g", "sg_w", "sg_b", "sc_conv_w",
             "gdn_conv_w", "gdn_a_log", "gdn_dt_bias", "gdn_norm_g"]
    return (loss, grad_x, *[res[k][0] for k in order], *[res[k][1] for k in order], *[res[k][2] for k in order],
            *[res[k][3] for k in order])
```

```python
import functools

import jax
import jax.numpy as jnp
import numpy as np
from jax import lax
from jax.experimental import pallas as pl
from jax.experimental.pallas import tpu as pltpu

f32 = jnp.float32
bf16 = jnp.bfloat16
HI = lax.Precision.HIGHEST
P3 = lax.Precision.HIGH
MESH = pl.DeviceIdType.MESH

EPS = 1e-6
D = 1024
NH = 4
HD = 64
BW = NH * HD
PAIR_W = 2 * HD
RC = 128
GC = 64
GRID_W = 64
ROPE_BASE = 10000.0
IN_W = 15 * BW + 16
PW = 4096
GATE_COL = 15 * BW
N_CHIPS = 4
N_DEV = 8
TM = 256
TP = 2 * TM
ADAM_LR, ADAM_B1, ADAM_B2, ADAM_EPS, ADAM_WD, ADAM_STEP = 0.001, 0.9, 0.999, 1e-08, 0.01, 10
LANE_HEAD = np.arange(BW) // HD
VMEM_BIG = 56 * 1024 * 1024


def _dot(a, b, precision=None):
    return jnp.dot(a, b, precision=precision, preferred_element_type=f32)


def _dot_nt(a, b, precision=None):
    return lax.dot_general(a, b, (((1,), (1,)), ((), ())), precision=precision, preferred_element_type=f32)


def _dot_tn(a, b, precision=None):
    return lax.dot_general(a, b, (((0,), (0,)), ((), ())), precision=precision, preferred_element_type=f32)


def _sds(shape, dtype=f32):
    return jax.ShapeDtypeStruct(shape, dtype)


def _cparams(sem=None, vmem=None):
    kw = {}
    if sem is not None:
        kw["dimension_semantics"] = sem
    if vmem is not None:
        kw["vmem_limit_bytes"] = vmem
    return pltpu.CompilerParams(**kw)


def _full(shape):
    n = len(shape)
    return pl.BlockSpec(shape, lambda *_: (0,) * n)


def _block_diag():
    return (LANE_HEAD[:, None] == LANE_HEAD[None, :]).astype(np.float32)


def _tau(c, d):
    return np.arange(c) if d == 0 else c - 1 - np.arange(c)


def _ret_consts(nb):
    lg = np.log(1.0 - 2.0 ** (-5.0 - np.arange(NH)))
    intra = np.zeros((2, 2, RC, 2 * RC)); qdec = np.zeros((2, 2, RC, PAIR_W)); kdec = np.zeros((2, 2, RC, PAIR_W))
    cd = np.zeros((2, 2, PAIR_W, PAIR_W))
    for d in range(2):
        t = _tau(RC, d)
        diff = t[:, None] - t[None, :]
        for p in range(2):
            lane_lg = lg[2 * p + np.arange(PAIR_W) // HD]
            for h in range(2):
                intra[d, p, :, h * RC:(h + 1) * RC] = np.where(diff >= 0, np.exp(np.maximum(diff, 0) * lg[2 * p + h]), 0.0)
            qdec[d, p] = np.exp((t[:, None] + 1.0) * lane_lg[None, :])
            kdec[d, p] = np.exp((RC - 1.0 - t[:, None]) * lane_lg[None, :])
            cd[d, p] = np.exp(RC * lane_lg)[:, None] * np.ones((1, PAIR_W))
    per_z = [np.tile(a.reshape((4,) + a.shape[2:]), (nb, 1, 1)) for a in (intra, qdec, kdec, cd)]
    bd2 = (np.arange(PAIR_W)[:, None] // HD == np.arange(PAIR_W)[None, :] // HD)
    bdr = (np.arange(2 * RC)[:, None] // RC == np.arange(PAIR_W)[None, :] // HD)
    return [jnp.asarray(a, f32) for a in per_z + [bd2, bdr]]


def _rope_tables(t_lat, t_ctx):
    nf = HD // 4
    inv = ROPE_BASE ** (-np.arange(nf) / nf)
    pos = np.arange(t_lat)
    ang_r = (pos // GRID_W)[:, None] * inv[None, :]
    ang_c = (pos % GRID_W)[:, None] * inv[None, :]
    ang = np.concatenate([ang_r, ang_r, ang_c, ang_c], axis=1)
    sign = np.concatenate([-np.ones(nf), np.ones(nf), -np.ones(nf), np.ones(nf)])
    cos = np.tile(np.cos(ang), (1, 2)); sins = np.tile(np.sin(ang) * sign, (1, 2))
    cos = np.concatenate([np.ones((t_ctx, PAIR_W)), cos]); sins = np.concatenate([np.zeros((t_ctx, PAIR_W)), sins])
    return jnp.asarray(cos, f32), jnp.asarray(sins, f32)


def _gdn_consts(nb):
    tmask = np.zeros((2, 2, GC, GC)); tmask2 = np.zeros((2, 2, GC, PAIR_W)); strict2 = np.zeros((2, 2, GC, PAIR_W))
    exp_g = np.zeros((2, 2, 128, PAIR_W)); exp_b = np.zeros((2, 2, 128, PAIR_W))
    for d in range(2):
        t = _tau(GC, d)
        tmask[d, :] = (t[:, None] >= t[None, :])
        tmask2[d, :] = np.tile(t[:, None] >= t[None, :], (1, 2))
        strict2[d, :] = np.tile(t[:, None] > t[None, :], (1, 2))
        for h in range(NH):
            exp_g[d, h // 2, 4 * d + h, (h % 2) * HD:(h % 2 + 1) * HD] = 1.0
            exp_b[d, h // 2, 8 + 4 * d + h, (h % 2) * HD:(h % 2 + 1) * HD] = 1.0
    exp_gt = np.transpose(exp_g, (0, 1, 3, 2))
    per_z = [np.tile(a.reshape((4,) + a.shape[2:]), (nb, 1, 1)) for a in (tmask, tmask2, strict2, exp_g, exp_b, exp_gt)]
    dsel2 = np.tile(np.eye(GC), (1, 2))
    eye2 = np.tile(np.eye(GC), (1, 2))
    bd2 = (np.arange(PAIR_W)[:, None] // HD == np.arange(PAIR_W)[None, :] // HD)
    return [jnp.asarray(a, f32) for a in per_z + [dsel2, eye2, bd2]]


def _swap16(x):
    lane = lax.broadcasted_iota(jnp.int32, x.shape, x.ndim - 1)
    n = x.shape[-1]
    return jnp.where(lane % 32 < 16, pltpu.roll(x, n - 16, axis=x.ndim - 1), pltpu.roll(x, 16, axis=x.ndim - 1))


@jax.custom_vjp
def _rot(x, cos, sins):
    return x * cos + _swap16(x) * sins


def _rot_fwd(x, cos, sins):
    return _rot(x, cos, sins), (cos, sins)


def _rot_bwd(res, g):
    cos, sins = res
    return g * cos + _swap16(g * sins), jnp.zeros_like(cos), jnp.zeros_like(sins)


_rot.defvjp(_rot_fwd, _rot_bwd)


def _silu(z):
    return z * jax.nn.sigmoid(z)


def _head_sum(x, bd):
    return _sel_r(x, bd)


def _ret_step(s, q, k, v, cos, sins, intra, qdec, kdec, cd, bd2, bdr):
    def bdiag(x):
        return jnp.concatenate([x, x], axis=1) * bdr

    qr = _rot(q, cos, sins)
    kr = _rot(k, cos, sins) * (HD ** -0.5)
    sc = _bmm_nt(qr, bdiag(kr)) * intra
    o = _bmm(qr * qdec, s) + _bmm(sc, bdiag(v))
    s_new = s * cd + bd2 * _bmm_tn(kr * kdec, v)
    return s_new, o


def _ret_finish(o_f, o_b, z, norm_g, bd):
    o = o_f + o_b
    mu = _head_sum(o, bd) * (1.0 / HD)
    xc = o - mu
    var = _head_sum(xc * xc, bd) * (1.0 / HD)
    return xc * lax.rsqrt(var + EPS) * norm_g * _silu(z)


def _softplus(x):
    return jnp.maximum(x, 0.0) + jnp.log(1.0 + jnp.exp(-jnp.abs(x)))


def _bmm(a, b, precision=None):
    return lax.dot_general(a, b, (((2,), (1,)), ((0,), (0,))), precision=precision, preferred_element_type=f32)


def _bmm_nt(a, b, precision=None):
    return lax.dot_general(a, b, (((2,), (2,)), ((0,), (0,))), precision=precision, preferred_element_type=f32)


def _bmm_tn(a, b, precision=None):
    return lax.dot_general(a, b, (((1,), (1,)), ((0,), (0,))), precision=precision, preferred_element_type=f32)


def _mm(a, b, mode):
    ca, cb = {"nn": (1, 0), "nt": (1, 1), "tn": (0, 0)}[mode]
    if a.ndim == 3:
        dims = (((ca + 1,), (cb + 1,)), ((0,), (0,)))
    else:
        dims = (((ca,), (cb,)), ((), ()))
    return lax.dot_general(a, b, dims, preferred_element_type=f32)


def _split(a):
    hi = a.astype(bf16)
    return hi, (a - hi.astype(f32)).astype(bf16)


def _sel2(a, e, mode, e_left):
    hi, lo = _split(a)
    eb = e.astype(bf16)
    if e_left:
        return _mm(eb, hi, mode) + _mm(eb, lo, mode)
    return _mm(hi, eb, mode) + _mm(lo, eb, mode)


@jax.custom_vjp
def _sel_r(a, e):
    return _sel2(a, e, "nn", False)


_sel_r.defvjp(lambda a, e: (_sel_r(a, e), e), lambda e, g: (_sel2(g, e, "nt", False), jnp.zeros_like(e)))


@jax.custom_vjp
def _sel_l(e, b):
    return _sel2(b, e, "nn", True)


_sel_l.defvjp(lambda e, b: (_sel_l(e, b), e), lambda e, g: (jnp.zeros_like(e), _sel2(g, e, "tn", True)))


def _bdiag(x, bd2):
    return jnp.concatenate([x, x], axis=1) * bd2


@jax.custom_vjp
def _solve_given_inv(m, vb, kbg, inv, bd2):
    return _bmm(inv, _bdiag(vb, bd2), P3), _bmm(inv, _bdiag(kbg, bd2), P3)


def _solve_fwd(m, vb, kbg, inv, bd2):
    u, w = _solve_given_inv(m, vb, kbg, inv, bd2)
    return (u, w), (inv, u, w, bd2)


def _solve_bwd(res, cts):
    inv, u, w, bd2 = res
    du, dw = cts
    c = inv.shape[1]
    t = jnp.swapaxes(_bdiag(inv, bd2), 1, 2)
    inv_t = t[:, :c] + t[:, c:]
    dvb = _bmm(inv_t, _bdiag(du, bd2), P3)
    dkbg = _bmm(inv_t, _bdiag(dw, bd2), P3)
    dm = _bmm_nt(dvb, _bdiag(u, bd2), P3) + _bmm_nt(dkbg, _bdiag(w, bd2), P3)
    return dm, dvb, dkbg, jnp.zeros_like(inv), jnp.zeros_like(bd2)


_solve_given_inv.defvjp(_solve_fwd, _solve_bwd)


def _gdn_step(s, q, k, v, gate, alog, dtb, tmask, tmask2, strict2, exp_g, exp_b, exp_gt, dsel2, eye2, bd2, inv=None):
    z, c, w_ = q.shape
    ne = gate.shape[0]

    def per_pair(a):
        return jnp.broadcast_to(a[:, None], (ne, z // ne) + a.shape[1:]).reshape((z,) + a.shape[1:])

    def rows(a):
        return a.reshape(z * c, w_)

    def bdiag(x):
        return _bdiag(x, bd2)

    g = per_pair(-jnp.exp(alog) * _softplus(gate + dtb))
    beta = per_pair(jax.nn.sigmoid(gate))
    gl = _sel_r(g, exp_g)
    gc_l = _sel_l(tmask, gl)
    glast_l = jnp.sum(gl, axis=1, keepdims=True)
    glast = jnp.sum(g, axis=1, keepdims=True)
    beta_l = _sel_r(beta, exp_b)
    gc_r = jnp.sum(gc_l * dsel2, axis=1, keepdims=True)
    qn = q * lax.rsqrt(_sel_r(rows(q * q), bd2).reshape(z, c, w_) + EPS)
    kn = k * lax.rsqrt(_sel_r(rows(k * k), bd2).reshape(z, c, w_) + EPS)
    eg = jnp.exp(gc_l)
    kb = kn * beta_l
    vb = v * beta_l
    kbg = kb * eg
    qs = qn * (HD ** -0.5)
    dec = jnp.exp(jnp.where(tmask2 > 0, gc_l - gc_r, -1e30))
    kns = bdiag(kn)
    m = -(_bmm_nt(kb, kns) * dec * strict2)
    if inv is None:
        inv = eye2 + m
        p = m
        for _ in range(5):
            p = _bmm(p, bdiag(p), P3)
            inv = inv + _bmm(inv, bdiag(p), P3)
        u = _bmm(inv, bdiag(vb), P3)
        w = _bmm(inv, bdiag(kbg), P3)
    else:
        u, w = _solve_given_inv(m, vb, kbg, inv, bd2)
    v_new = u - _bmm(w, s)
    k_tail = kn * jnp.exp(glast_l - gc_l)
    cdec = jnp.sum(exp_gt * jnp.exp(glast), axis=-1, keepdims=True)
    s_new = s * cdec + bd2 * _bmm_tn(k_tail, v_new)
    a = _bmm_nt(qs, kns) * dec
    o = _bmm(qs * eg, s) + _bmm(a, bdiag(v_new))
    return s_new, o, inv


def _gdn_finish(o_f, o_b, z, norm_g, bd):
    o = o_f + o_b
    ms = _head_sum(o * o, bd) * (1.0 / HD)
    return o * lax.rsqrt(ms + EPS) * norm_g * _silu(z)


def _gelu(x):
    return 0.5 * x * (1.0 + jnp.tanh(0.7978845608028654 * (x + 0.044715 * (x * x * x))))


def _sg_block(u0, u1, v0, v1, z0, z1, w, b, hmp, bdr):
    ts = u0.shape[0]
    nc = ts // RC
    g0, g1 = _gelu(v0), _gelu(v1)
    mu = (jnp.sum(g0, axis=-1, keepdims=True) + jnp.sum(g1, axis=-1, keepdims=True)) * (1.0 / BW)
    x0, x1 = g0 - mu, g1 - mu
    var = (jnp.sum(x0 * x0, axis=-1, keepdims=True) + jnp.sum(x1 * x1, axis=-1, keepdims=True)) * (1.0 / BW)
    rstd = lax.rsqrt(var + EPS)
    ys = []
    for p, (u, xc, z) in enumerate(((u0, x0, z0), (u1, x1, z1))):
        vn = (xc * rstd).reshape(nc, RC, PAIR_W)
        wp = jnp.concatenate([w[2 * p], w[2 * p + 1]], axis=1)
        mix = _bmm(jnp.broadcast_to(wp, (nc, RC, 2 * RC)), jnp.concatenate([vn, vn], axis=1) * bdr)
        bias = _dot_tn(b, hmp[p], precision=HI)
        s = (mix + bias).reshape(ts, PAIR_W)
        ys.append(_gelu(u) * s * _silu(z))
    return ys[0], ys[1]


def _make_shifts(t_ctx, n):
    def dn(x):
        t = lax.broadcasted_iota(jnp.int32, x.shape, 0)
        return jnp.where((t != 0) & (t != t_ctx), pltpu.roll(x, 1, axis=0), 0.0)

    def up(x):
        t = lax.broadcasted_iota(jnp.int32, x.shape, 0)
        return jnp.where((t != t_ctx - 1) & (t != n - 1), pltpu.roll(x, n - 1, axis=0), 0.0)

    @jax.custom_vjp
    def shift_dn(x):
        return dn(x)
    shift_dn.defvjp(lambda x: (dn(x), None), lambda _, g: (up(g),))

    @jax.custom_vjp
    def shift_up(x):
        return up(x)
    shift_up.defvjp(lambda x: (up(x), None), lambda _, g: (dn(g),))
    return shift_dn, shift_up


def _conv3(x, w, shift_dn, shift_up):
    return shift_dn(x) * w[0:1] + x * w[1:2] + shift_up(x) * w[2:3]


def inproj_fwd(x, shift_t, scale_t, g_pre, w_in, n_batch, sb):
    n = x.shape[0]

    def sel(i):
        return jnp.where(i % sb == 0, n_batch, i // sb)

    def body(x_ref, sh0, sh1, sc0, sc1, g_ref, w_ref, p_ref, h_ref):
        hs = []
        for k, (sh_ref, sc_ref) in enumerate(((sh0, sc0), (sh1, sc1))):
            xv = x_ref[k * TM:(k + 1) * TM, :]
            r = xv * lax.rsqrt(jnp.mean(xv * xv, axis=-1, keepdims=True) + EPS)
            hs.append(((r * g_ref[...]) * (1.0 + sc_ref[0]) + sh_ref[0]).astype(bf16))
        hb = jnp.concatenate(hs, axis=0)
        h_ref[...] = hb
        p_ref[...] = _dot(hb, w_ref[...])

    def mrow(k):
        return pl.BlockSpec((1, 1, D), lambda i: (sel(2 * i + k), 0, 0))

    return pl.pallas_call(
        body, name="inproj_fwd", grid=(n // TP,),
        in_specs=[pl.BlockSpec((TP, D), lambda i: (i, 0)), mrow(0), mrow(1), mrow(0), mrow(1),
                  _full((1, D)), _full((D, PW))],
        out_specs=[pl.BlockSpec((TP, PW), lambda i: (i, 0)), pl.BlockSpec((TP, D), lambda i: (i, 0))],
        out_shape=[_sds((n, PW)), _sds((n, D), bf16)],
        compiler_params=_cparams(("arbitrary",), VMEM_BIG),
    )(x, shift_t, shift_t, scale_t, scale_t, g_pre, w_in)


def outproj_fwd(ys, w_out, x, gate_t, g_post, n_batch, sb):
    n = x.shape[0]

    def sel(i):
        return jnp.where(i % sb == 0, n_batch, i // sb)

    def body(y0, y1, y2, y3, w_ref, x_ref, gt0, gt1, g_ref, xn_ref, o_ref):
        y = jnp.concatenate([y0[...], y1[...], y2[...], y3[...]], axis=1)
        o = _dot(y, w_ref[...])
        o_ref[...] = o
        nrm = o * lax.rsqrt(jnp.mean(o * o, axis=-1, keepdims=True) + EPS) * g_ref[...]
        for k, gt_ref in enumerate((gt0, gt1)):
            rows = slice(k * TM, (k + 1) * TM)
            xn_ref[rows, :] = x_ref[rows, :] + gt_ref[0] * nrm[rows]

    def mrow(k):
        return pl.BlockSpec((1, 1, D), lambda i: (sel(2 * i + k), 0, 0))

    yspec = pl.BlockSpec((TP, BW), lambda i: (i, 0))
    return pl.pallas_call(
        body, name="outproj_fwd", grid=(n // TP,),
        in_specs=[yspec, yspec, yspec, yspec, _full((D, D)), pl.BlockSpec((TP, D), lambda i: (i, 0)),
                  mrow(0), mrow(1), _full((1, D))],
        out_specs=[pl.BlockSpec((TP, D), lambda i: (i, 0)), pl.BlockSpec((TP, D), lambda i: (i, 0))],
        out_shape=[_sds((n, D)), _sds((n, D))],
        compiler_params=_cparams(("arbitrary",), VMEM_BIG),
    )(*ys, w_out, x, gate_t, gate_t, g_post)


def _row_onehot(r):
    return lax.broadcasted_iota(jnp.int32, (8, 1), 0) == r


def outproj_bwd(dxn, o, gate_t, g_post, ys, w_out, n_batch, sb):
    n = dxn.shape[0]

    def sel(i):
        return jnp.where(i % sb == 0, n_batch, i // sb)

    def body(dxn_ref, o_ref, gt0, gt1, g_ref, y0, y1, y2, y3, w_ref, dy_ref, dw_ref, dg_ref, dgate_ref):
        i = pl.program_id(0)

        @pl.when(i == 0)
        def _():
            dw_ref[...] = jnp.zeros_like(dw_ref)
            dg_ref[...] = jnp.zeros_like(dg_ref)
            dgate_ref[...] = jnp.zeros_like(dgate_ref)

        g = g_ref[...]
        dos = []
        for k, gt_ref in enumerate((gt0, gt1)):
            rows = slice(k * TM, (k + 1) * TM)
            ov = o_ref[rows, :]
            rstd = lax.rsqrt(jnp.mean(ov * ov, axis=-1, keepdims=True) + EPS)
            r = ov * rstd
            dx = dxn_ref[rows, :]
            dgate_ref[...] += jnp.where(_row_onehot(sel(2 * i + k)), jnp.sum(dx * (r * g), axis=0, keepdims=True), 0.0)
            dn = dx * gt_ref[0]
            dg_ref[...] += jnp.sum(dn * r, axis=0, keepdims=True)
            dr = dn * g
            dos.append((rstd * (dr - r * jnp.mean(dr * r, axis=-1, keepdims=True))).astype(bf16))
        dob = jnp.concatenate(dos, axis=0)
        dy_ref[...] = _dot_nt(dob, w_ref[...])
        y = jnp.concatenate([y0[...], y1[...], y2[...], y3[...]], axis=1)
        dw_ref[...] += _dot_tn(y, dob)

    def mrow(k):
        return pl.BlockSpec((1, 1, D), lambda i: (sel(2 * i + k), 0, 0))

    yspec = pl.BlockSpec((TP, BW), lambda i: (i, 0))
    row = pl.BlockSpec((TP, D), lambda i: (i, 0))
    return pl.pallas_call(
        body, name="outproj_bwd", grid=(n // TP,),
        in_specs=[row, row, mrow(0), mrow(1), _full((1, D)), yspec, yspec, yspec, yspec, _full((D, D))],
        out_specs=[row, _full((D, D)), _full((1, D)), _full((8, D))],
        out_shape=[_sds((n, D)), _sds((D, D)), _sds((1, D)), _sds((8, D))],
        compiler_params=_cparams(("arbitrary",), VMEM_BIG),
    )(dxn, o, gate_t, gate_t, g_post, *ys, w_out)


def inproj_bwd_x(dp, w_in, x, scale_t, g_pre, dxn, n_batch, sb, xchg=None):
    n = x.shape[0]

    def sel(i):
        return jnp.where(i % sb == 0, n_batch, i // sb)

    def body(dp_ref, w_ref, x_ref, sc0, sc1, g_ref, dxn_ref, dx_ref, dg_ref, dsh_ref, dsc_ref):
        i = pl.program_id(0)

        @pl.when(i == 0)
        def _():
            dg_ref[...] = jnp.zeros_like(dg_ref)
            dsh_ref[...] = jnp.zeros_like(dsh_ref)
            dsc_ref[...] = jnp.zeros_like(dsc_ref)

        dh_all = _dot_nt(dp_ref[...], w_ref[...])
        g = g_ref[...]
        for k, sc_ref in enumerate((sc0, sc1)):
            rows = slice(k * TM, (k + 1) * TM)
            dh = dh_all[rows]
            xv = x_ref[rows, :]
            rstd = lax.rsqrt(jnp.mean(xv * xv, axis=-1, keepdims=True) + EPS)
            r = xv * rstd
            hot = _row_onehot(sel(2 * i + k))
            dsh_ref[...] += jnp.where(hot, jnp.sum(dh, axis=0, keepdims=True), 0.0)
            dsc_ref[...] += jnp.where(hot, jnp.sum(dh * (r * g), axis=0, keepdims=True), 0.0)
            t = dh * (1.0 + sc_ref[0])
            dg_ref[...] += jnp.sum(t * r, axis=0, keepdims=True)
            dr = t * g
            dx_ref[rows, :] = dxn_ref[rows, :] + rstd * (dr - r * jnp.mean(dr * r, axis=-1, keepdims=True))

    def mrow(k):
        return pl.BlockSpec((1, 1, D), lambda i: (sel(2 * i + k), 0, 0))

    row = pl.BlockSpec((TP, D), lambda i: (i, 0))
    fused, x_in, x_out, x_shape, x_scratch = _with_exchange(body, 7, 4, 0, xchg, n // TP)
    return pl.pallas_call(
        fused, name="inproj_bwd_x" + ("" if xchg is None else "_" + xchg[0]), grid=(n // TP,),
        in_specs=[pl.BlockSpec((TP, PW), lambda i: (i, 0)), _full((D, PW)), row, mrow(0), mrow(1), _full((1, D)), row]
                 + x_in,
        out_specs=[row, _full((1, D)), _full((8, D)), _full((8, D))] + x_out,
        out_shape=[_sds((n, D)), _sds((1, D)), _sds((8, D)), _sds((8, D))] + x_shape,
        scratch_shapes=x_scratch,
        compiler_params=_cparams(("arbitrary",), VMEM_BIG),
    )(dp, w_in, x, scale_t, scale_t, g_pre, dxn, *([] if xchg is None else xchg[1]))


def dw_in(h, dp):
    n = h.shape[0]
    tk, tn = (1536 if n % 1536 == 0 else 512), 1024
    nk = n // tk

    def body(h_ref, dp_ref, o_ref, acc):
        k = pl.program_id(1)

        @pl.when(k == 0)
        def _():
            acc[...] = jnp.zeros_like(acc)
        acc[...] += _dot_tn(dp_ref[...], h_ref[...])

        @pl.when(k == nk - 1)
        def _():
            o_ref[...] = acc[...].astype(bf16)

    return pl.pallas_call(
        body, name="dw_in", grid=(PW // tn, nk),
        in_specs=[pl.BlockSpec((tk, D), lambda j, k: (k, 0)), pl.BlockSpec((tk, tn), lambda j, k: (k, j))],
        out_specs=pl.BlockSpec((tn, D), lambda j, k: (j, 0)),
        out_shape=_sds((PW, D), bf16),
        scratch_shapes=[pltpu.VMEM((tn, D), f32)],
        compiler_params=_cparams(("parallel", "arbitrary"), VMEM_BIG),
    )(h, dp)


def place_weights(slabs):
    n_ch, d, wc = slabs.shape

    def body(w_ref, o_ref):
        acc = jnp.pad(w_ref[0].astype(f32), ((0, 0), (0, PW - wc)))
        for k in range(1, n_ch):
            acc = acc + pltpu.roll(jnp.pad(w_ref[k].astype(f32), ((0, 0), (0, PW - wc))), wc * k, axis=1)
        o_ref[...] = acc.astype(bf16)

    return pl.pallas_call(
        body, name="place_weights", grid=(d // TM,),
        in_specs=[pl.BlockSpec((n_ch, TM, wc), lambda i: (0, i, 0))],
        out_specs=pl.BlockSpec((TM, PW), lambda i: (i, 0)),
        out_shape=_sds((d, PW), bf16),
        compiler_params=_cparams(("arbitrary",), VMEM_BIG),
    )(slabs)


def loss_head(xf, target, t_ctx):
    nb, s, _ = xf.shape
    jc = t_ctx // TM

    def body(x_ref, t_ref, dx_ref, l_ref):
        b, j = pl.program_id(0), pl.program_id(1)

        @pl.when((b == 0) & (j == 0))
        def _():
            l_ref[...] = jnp.zeros_like(l_ref)

        @pl.when(j < jc)
        def _():
            dx_ref[...] = jnp.zeros_like(dx_ref)

        @pl.when(j >= jc)
        def _():
            diff = x_ref[0] - t_ref[0]
            dx_ref[0] = diff * (1.0 / D)
            l_ref[...] += 0.5 * jnp.sum(diff * diff) * (1.0 / D)

    return pl.pallas_call(
        body, name="loss_head", grid=(nb, s // TM),
        in_specs=[pl.BlockSpec((1, TM, D), lambda b, j: (b, j, 0)),
                  pl.BlockSpec((1, TM, D), lambda b, j: (b, jnp.maximum(j - jc, 0), 0))],
        out_specs=[pl.BlockSpec((1, TM, D), lambda b, j: (b, j, 0)), _full((1, 128))],
        out_shape=[_sds((nb, s, D)), _sds((1, 128))],
        compiler_params=_cparams(("arbitrary", "arbitrary")),
    )(xf, target)


def _chunk_maps(n_ctx, n_lat):
    n = n_ctx + n_lat

    def cf(t):
        return t

    def cb(t):
        return jnp.where(t < n_ctx, n_ctx - 1 - t, n - 1 - t + n_ctx)
    return n, cf, cb


def ret_scan_fwd(p3, cos, sins, consts, t_ctx):
    nb, s, _ = p3.shape
    n, cf, cb = _chunk_maps(t_ctx // RC, (s - t_ctx) // RC)
    nz = 4 * nb

    def body(qf, kf, vf, qb, kb, vb, cosf, sinf, cosb, sinb, intra_r, qdec_r, kdec_r, cd_r, bd_r, bdr_r,
             of_ref, ob_ref, sall_ref, s_sc):
        @pl.when(pl.program_id(0) == 0)
        def _():
            s_sc[...] = jnp.zeros_like(s_sc)
        st = s_sc[...]
        sall_ref[0] = st
        s_new, o = _ret_step(st, _pairs(qf, qb, nb), _pairs(kf, kb, nb), _pairs(vf, vb, nb),
                             _pair_tables(cosf, cosb, nb), _pair_tables(sinf, sinb, nb), intra_r[...], qdec_r[...],
                             kdec_r[...], cd_r[...], bd_r[...], bdr_r[...])
        s_sc[...] = s_new
        _unpairs(o, of_ref, ob_ref, nb)

    def pspec(m, seg):
        return pl.BlockSpec((nb, RC, BW), lambda t: (0, m(t), seg))

    def tspec(m):
        return pl.BlockSpec((RC, PAIR_W), lambda t: (m(t), 0))

    return pl.pallas_call(
        body, name="ret_scan_fwd", grid=(n,),
        in_specs=[pspec(cf, 0), pspec(cf, 1), pspec(cf, 2), pspec(cb, 0), pspec(cb, 1), pspec(cb, 2),
                  tspec(cf), tspec(cf), tspec(cb), tspec(cb)] + [_full(c.shape) for c in consts],
        out_specs=[pl.BlockSpec((nb, RC, BW), lambda t: (0, cf(t), 0)),
                   pl.BlockSpec((nb, RC, BW), lambda t: (0, cb(t), 0)),
                   pl.BlockSpec((1, nz, PAIR_W, PAIR_W), lambda t: (t, 0, 0, 0))],
        out_shape=[_sds((nb, s, BW)), _sds((nb, s, BW)), _sds((n, nz, PAIR_W, PAIR_W))],
        scratch_shapes=[pltpu.VMEM((nz, PAIR_W, PAIR_W), f32)],
        compiler_params=_cparams(("arbitrary",)),
    )(p3, p3, p3, p3, p3, p3, cos, sins, cos, sins, *consts)


def ret_scan_bwd(p3, cos, sins, consts, s_all, do, t_ctx):
    nb, s, _ = p3.shape
    n, cf, cb = _chunk_maps(t_ctx // RC, (s - t_ctx) // RC)
    nz = 4 * nb

    def rf(t):
        return cf(n - 1 - t)

    def rb(t):
        return cb(n - 1 - t)

    def body(qf, kf, vf, qb, kb, vb, cosf, sinf, cosb, sinb, intra_r, qdec_r, kdec_r, cd_r, bd_r, bdr_r,
             sall_ref, dof, dob, dqf, dkf, dvf, dqb, dkb, dvb, ds_sc):
        @pl.when(pl.program_id(0) == 0)
        def _():
            ds_sc[...] = jnp.zeros_like(ds_sc)
        step = functools.partial(_ret_step, cos=_pair_tables(cosf, cosb, nb), sins=_pair_tables(sinf, sinb, nb),
                                 intra=intra_r[...], qdec=qdec_r[...], kdec=kdec_r[...], cd=cd_r[...], bd2=bd_r[...],
                                 bdr=bdr_r[...])
        _, vjp = jax.vjp(step, sall_ref[0], _pairs(qf, qb, nb), _pairs(kf, kb, nb), _pairs(vf, vb, nb))
        ds, dq, dk, dv = vjp((ds_sc[...], _pairs(dof, dob, nb)))
        ds_sc[...] = ds
        _unpairs(dq, dqf, dqb, nb)
        _unpairs(dk, dkf, dkb, nb)
        _unpairs(dv, dvf, dvb, nb)

    def pspec(m, seg):
        return pl.BlockSpec((nb, RC, BW), lambda t: (0, m(t), seg))

    def tspec(m):
        return pl.BlockSpec((RC, PAIR_W), lambda t: (m(t), 0))

    def ospec(m):
        return pl.BlockSpec((nb, RC, BW), lambda t: (0, m(t), 0))

    return pl.pallas_call(
        body, name="ret_scan_bwd", grid=(n,),
        in_specs=[pspec(rf, 0), pspec(rf, 1), pspec(rf, 2), pspec(rb, 0), pspec(rb, 1), pspec(rb, 2),
                  tspec(rf), tspec(rf), tspec(rb), tspec(rb)] + [_full(c.shape) for c in consts]
                 + [pl.BlockSpec((1, nz, PAIR_W, PAIR_W), lambda t: (n - 1 - t, 0, 0, 0)), ospec(rf), ospec(rb)],
        out_specs=[ospec(rf), ospec(rf), ospec(rf), ospec(rb), ospec(rb), ospec(rb)],
        out_shape=[_sds((nb, s, BW), bf16)] * 6,
        scratch_shapes=[pltpu.VMEM((nz, PAIR_W, PAIR_W), f32)],
        compiler_params=_cparams(("arbitrary",), VMEM_BIG),
    )(p3, p3, p3, p3, p3, p3, cos, sins, cos, sins, *consts, s_all, do, do)


def mix_finish_fwd(fn, name, o_f, o_b, p3, zseg, norm_g, bd):
    nb, s, _ = p3.shape

    def body(of_ref, ob_ref, z_ref, g_ref, bd_ref, y_ref):
        y_ref[0] = fn(of_ref[0], ob_ref[0], z_ref[0], g_ref[...], bd_ref[...]).astype(bf16)

    blk = pl.BlockSpec((1, TM, BW), lambda b, j: (b, j, 0))
    return pl.pallas_call(
        body, name=name, grid=(nb, s // TM),
        in_specs=[blk, blk, pl.BlockSpec((1, TM, BW), lambda b, j: (b, j, zseg)), _full((1, BW)), _full((BW, BW))],
        out_specs=blk, out_shape=_sds((nb, s, BW), bf16),
        compiler_params=_cparams(("arbitrary", "arbitrary")),
    )(o_f, o_b, p3, norm_g, bd)


def mix_finish_bwd(fn, name, o_f, o_b, p3, zseg, norm_g, bd, dy3, yseg):
    nb, s, _ = p3.shape

    def body(of_ref, ob_ref, z_ref, g_ref, bd_ref, dy_ref, do_ref, dz_ref, dg_ref):
        @pl.when((pl.program_id(0) == 0) & (pl.program_id(1) == 0))
        def _():
            dg_ref[...] = jnp.zeros_like(dg_ref)
        bdv = bd_ref[...]
        _, vjp = jax.vjp(lambda a, b, z, g: fn(a, b, z, g, bdv), of_ref[0], ob_ref[0], z_ref[0], g_ref[...])
        do, _, dz, dg = vjp(dy_ref[0])
        do_ref[0] = do
        dz_ref[0] = dz.astype(bf16)
        dg_ref[...] += dg

    blk = pl.BlockSpec((1, TM, BW), lambda b, j: (b, j, 0))
    return pl.pallas_call(
        body, name=name, grid=(nb, s // TM),
        in_specs=[blk, blk, pl.BlockSpec((1, TM, BW), lambda b, j: (b, j, zseg)), _full((1, BW)), _full((BW, BW)),
                  pl.BlockSpec((1, TM, BW), lambda b, j: (b, j, yseg))],
        out_specs=[blk, blk, _full((1, BW))],
        out_shape=[_sds((nb, s, BW)), _sds((nb, s, BW), bf16), _sds((1, BW))],
        compiler_params=_cparams(("arbitrary", "arbitrary")),
    )(o_f, o_b, p3, norm_g, bd, dy3)


def gdn_conv_fwd(p3, w, seg, t_ctx):
    nb, s, _ = p3.shape
    sd, su = _make_shifts(t_ctx, s)

    def body(x_ref, w_ref, o_ref):
        o_ref[0] = _silu(_conv3(x_ref[0], w_ref[...], sd, su))

    return pl.pallas_call(
        body, name="gdn_conv_fwd", grid=(nb, 2),
        in_specs=[pl.BlockSpec((1, s, 128), lambda b, j: (b, 0, 2 * seg + j)), pl.BlockSpec((3, 128), lambda b, j: (0, j))],
        out_specs=pl.BlockSpec((1, s, 128), lambda b, j: (b, 0, j)),
        out_shape=_sds((nb, s, BW)),
        compiler_params=_cparams(("arbitrary", "arbitrary")),
    )(p3, w)


def gdn_conv_bwd(p3, w, seg, d_f, d_b, t_ctx):
    nb, s, _ = p3.shape
    sd, su = _make_shifts(t_ctx, s)

    def body(x_ref, w_ref, df_ref, db_ref, dx_ref, dw_ref):
        @pl.when(pl.program_id(1) == 0)
        def _():
            dw_ref[...] = jnp.zeros_like(dw_ref)
        _, vjp = jax.vjp(lambda x, w_: _silu(_conv3(x, w_, sd, su)), x_ref[0], w_ref[...])
        dx, dw = vjp(df_ref[0] + db_ref[0])
        dx_ref[0] = dx.astype(bf16)
        dw_ref[...] += dw

    blk = pl.BlockSpec((1, s, 128), lambda j, b: (b, 0, j))
    return pl.pallas_call(
        body, name="gdn_conv_bwd", grid=(2, nb),
        in_specs=[pl.BlockSpec((1, s, 128), lambda j, b: (b, 0, 2 * seg + j)), pl.BlockSpec((3, 128), lambda j, b: (0, j)),
                  blk, blk],
        out_specs=[blk, pl.BlockSpec((3, 128), lambda j, b: (0, j))],
        out_shape=[_sds((nb, s, BW), bf16), _sds((3, BW))],
        compiler_params=_cparams(("arbitrary", "arbitrary"), VMEM_BIG),
    )(p3, w, d_f, d_b)


def _pairs(f_ref, b_ref, nb):
    return jnp.stack([r[b, :, PAIR_W * p:PAIR_W * (p + 1)] for b in range(nb) for r in (f_ref, b_ref) for p in range(2)])


def _pair_tables(f_ref, b_ref, nb):
    return jnp.stack([r[...] for _ in range(nb) for r in (f_ref, b_ref) for _ in range(2)])


def _gates(f_ref, b_ref, nb):
    return jnp.stack([r[b] for b in range(nb) for r in (f_ref, b_ref)])


def _unpairs(a, f_ref, b_ref, nb):
    for b in range(nb):
        for d, r in enumerate((f_ref, b_ref)):
            for p in range(2):
                r[b, :, PAIR_W * p:PAIR_W * (p + 1)] = a[4 * b + 2 * d + p].astype(r.dtype)


def _with_exchange(body, n_in, n_out, n_scratch, xchg, n_steps):
    if xchg is None:
        return body, [], [], [], []
    kind, arrs = xchg
    nx = len(arrs)

    def fused(*refs):
        ins, rest = refs[:n_in], refs[n_in:]
        srcs, rest = rest[:nx], rest[nx:]
        outs, rest = rest[:n_out], rest[n_out:]
        dsts, rest = rest[:nx], rest[nx:]
        scratch, sems = rest[:n_scratch], rest[n_scratch:]
        start, wait = _peer_exchange(kind, "chips", srcs, dsts, *sems)
        pl.when(pl.program_id(0) == 0)(start)
        body(*ins, *outs, *scratch)
        pl.when(pl.program_id(0) == n_steps - 1)(wait)

    any_ = pl.BlockSpec(memory_space=pl.ANY)
    return fused, [any_] * nx, [any_] * nx, _exchange_shapes(kind, "chips", arrs), _exchange_scratch("chips", nx)


def gdn_scan_fwd(cq, ck, cv, p3, alog, dtb, consts, t_ctx, xchg=None):
    nb, s, _ = p3.shape
    n, cf, cb = _chunk_maps(t_ctx // GC, (s - t_ctx) // GC)
    gblk = GATE_COL // 128

    nz = 4 * nb

    def body(qf, kf, vf, gf, qb, kb, vb, gb, al_ref, dt_ref, tm_r, tm2_r, st2_r, eg_r, eb_r, egt_r, dsel_r, eye_r, bd_r,
             of_ref, ob_ref, sall_ref, inv_ref, s_sc):
        @pl.when(pl.program_id(0) == 0)
        def _():
            s_sc[...] = jnp.zeros_like(s_sc)
        st = s_sc[...]
        sall_ref[0] = st
        s_new, o, inv = _gdn_step(st, _pairs(qf, qb, nb), _pairs(kf, kb, nb), _pairs(vf, vb, nb), _gates(gf, gb, nb),
                                  al_ref[...], dt_ref[...], tm_r[...], tm2_r[...], st2_r[...], eg_r[...], eb_r[...],
                                  egt_r[...], dsel_r[...], eye_r[...], bd_r[...])
        s_sc[...] = s_new
        inv_ref[0] = inv
        _unpairs(o, of_ref, ob_ref, nb)

    def cspec(m):
        return pl.BlockSpec((nb, GC, BW), lambda t: (0, m(t), 0))

    def gspec(m):
        return pl.BlockSpec((nb, GC, 128), lambda t: (0, m(t), gblk))

    fused, x_in, x_out, x_shape, x_scratch = _with_exchange(body, 10 + len(consts), 4, 1, xchg, n)
    return pl.pallas_call(
        fused, name="gdn_scan_fwd" + ("" if xchg is None else "_" + xchg[0]), grid=(n,),
        in_specs=[cspec(cf), cspec(cf), cspec(cf), gspec(cf), cspec(cb), cspec(cb), cspec(cb), gspec(cb),
                  _full((1, 128)), _full((1, 128))] + [_full(c.shape) for c in consts] + x_in,
        out_specs=[cspec(cf), cspec(cb), pl.BlockSpec((1, nz, PAIR_W, PAIR_W), lambda t: (t, 0, 0, 0)),
                   pl.BlockSpec((1, nz, GC, PAIR_W), lambda t: (t, 0, 0, 0))] + x_out,
        out_shape=[_sds((nb, s, BW)), _sds((nb, s, BW)), _sds((n, nz, PAIR_W, PAIR_W)), _sds((n, nz, GC, PAIR_W))]
                  + x_shape,
        scratch_shapes=[pltpu.VMEM((nz, PAIR_W, PAIR_W), f32)] + x_scratch,
        compiler_params=_cparams(("arbitrary",)),
    )(cq, ck, cv, p3, cq, ck, cv, p3, alog, dtb, *consts, *([] if xchg is None else xchg[1]))


def gdn_scan_bwd(cq, ck, cv, p3, alog, dtb, consts, s_all, inv_all, do, t_ctx, xchg=None):
    nb, s, _ = p3.shape
    n, cf, cb = _chunk_maps(t_ctx // GC, (s - t_ctx) // GC)
    gblk = GATE_COL // 128

    def rf(t):
        return cf(n - 1 - t)

    def rb(t):
        return cb(n - 1 - t)

    nz = 4 * nb

    def body(qf, kf, vf, gf, qb, kb, vb, gb, al_ref, dt_ref, tm_r, tm2_r, st2_r, eg_r, eb_r, egt_r, dsel_r, eye_r, bd_r,
             sall_ref, inv_ref, dof, dob, dqf, dkf, dvf, dgf, dqb, dkb, dvb, dgb, dal_ref, ddt_ref, ds_sc):
        @pl.when(pl.program_id(0) == 0)
        def _():
            dal_ref[...] = jnp.zeros_like(dal_ref)
            ddt_ref[...] = jnp.zeros_like(ddt_ref)
            ds_sc[...] = jnp.zeros_like(ds_sc)
        consts = dict(tmask=tm_r[...], tmask2=tm2_r[...], strict2=st2_r[...], exp_g=eg_r[...], exp_b=eb_r[...],
                      exp_gt=egt_r[...], dsel2=dsel_r[...], eye2=eye_r[...], bd2=bd_r[...], inv=inv_ref[0])

        def step(*a):
            return _gdn_step(*a, **consts)[:2]

        _, vjp = jax.vjp(step, sall_ref[0], _pairs(qf, qb, nb), _pairs(kf, kb, nb), _pairs(vf, vb, nb),
                         _gates(gf, gb, nb), al_ref[...], dt_ref[...])
        ds, dq, dk, dv, dg, dal, ddt = vjp((ds_sc[...], _pairs(dof, dob, nb)))
        ds_sc[...] = ds
        _unpairs(dq, dqf, dqb, nb)
        _unpairs(dk, dkf, dkb, nb)
        _unpairs(dv, dvf, dvb, nb)
        for b in range(nb):
            dgf[b] = dg[2 * b].astype(bf16)
            dgb[b] = dg[2 * b + 1].astype(bf16)
        dal_ref[...] += dal
        ddt_ref[...] += ddt

    def cspec(m):
        return pl.BlockSpec((nb, GC, BW), lambda t: (0, m(t), 0))

    def gspec(m):
        return pl.BlockSpec((nb, GC, 128), lambda t: (0, m(t), gblk))

    def gout(m):
        return pl.BlockSpec((nb, GC, 128), lambda t: (0, m(t), 0))

    fused, x_in, x_out, x_shape, x_scratch = _with_exchange(body, 14 + len(consts), 10, 1, xchg, n)
    return pl.pallas_call(
        fused, name="gdn_scan_bwd" + ("" if xchg is None else "_" + xchg[0]), grid=(n,),
        in_specs=[cspec(rf), cspec(rf), cspec(rf), gspec(rf), cspec(rb), cspec(rb), cspec(rb), gspec(rb),
                  _full((1, 128)), _full((1, 128))] + [_full(c.shape) for c in consts]
                 + [pl.BlockSpec((1, nz, PAIR_W, PAIR_W), lambda t: (n - 1 - t, 0, 0, 0)),
                    pl.BlockSpec((1, nz, GC, PAIR_W), lambda t: (n - 1 - t, 0, 0, 0)), cspec(rf), cspec(rb)] + x_in,
        out_specs=[cspec(rf), cspec(rf), cspec(rf), gout(rf), cspec(rb), cspec(rb), cspec(rb), gout(rb),
                   _full((1, 128)), _full((1, 128))] + x_out,
        out_shape=[_sds((nb, s, BW))] * 3 + [_sds((nb, s, 128), bf16)] + [_sds((nb, s, BW))] * 3 + [_sds((nb, s, 128), bf16)]
                  + [_sds((1, 128)), _sds((1, 128))] + x_shape,
        scratch_shapes=[pltpu.VMEM((nz, PAIR_W, PAIR_W), f32)] + x_scratch,
        compiler_params=_cparams(("arbitrary",), VMEM_BIG),
    )(cq, ck, cv, p3, cq, ck, cv, p3, alog, dtb, *consts, s_all, inv_all, do, do, *([] if xchg is None else xchg[1]))


def _sg_consts():
    hmp = np.zeros((2, NH, PAIR_W))
    for h in range(NH):
        hmp[h // 2, h, (h % 2) * HD:(h % 2 + 1) * HD] = 1.0
    bdr = (np.arange(2 * RC)[:, None] // RC == np.arange(PAIR_W)[None, :] // HD)
    return jnp.asarray(hmp, f32), jnp.asarray(bdr, f32)


def _sg_rows(s):
    return 6 * RC if s % (6 * RC) == 0 else 2 * RC


def _halves(ref):
    return ref[0, :, :PAIR_W], ref[0, :, PAIR_W:]


def sg_fwd(p3, w, b, hmp, bdr):
    nb, s, _ = p3.shape
    ts = _sg_rows(s)

    def body(u_ref, v_ref, z_ref, w_ref, b_ref, hm_ref, bdr_ref, y_ref):
        y0, y1 = _sg_block(*_halves(u_ref), *_halves(v_ref), *_halves(z_ref), w_ref[...], b_ref[...], hm_ref[...],
                           bdr_ref[...])
        y_ref[0, :, :PAIR_W] = y0.astype(bf16)
        y_ref[0, :, PAIR_W:] = y1.astype(bf16)

    def seg(k):
        return pl.BlockSpec((1, ts, BW), lambda bi, i: (bi, i, k))

    return pl.pallas_call(
        body, name="sg_fwd", grid=(nb, s // ts),
        in_specs=[seg(4), seg(5), seg(6), _full((NH, RC, RC)), _full((NH, RC)), _full(hmp.shape), _full(bdr.shape)],
        out_specs=pl.BlockSpec((1, ts, BW), lambda bi, i: (bi, i, 0)),
        out_shape=_sds((nb, s, BW), bf16),
        compiler_params=_cparams(("arbitrary", "arbitrary")),
    )(p3, p3, p3, w, b, hmp, bdr)


def sg_bwd(p3, w, b, hmp, bdr, dy3):
    nb, s, _ = p3.shape
    ts = _sg_rows(s)

    def body(u_ref, v_ref, z_ref, w_ref, b_ref, hm_ref, bdr_ref, dy_ref, du_ref, dv_ref, dz_ref, dw_ref, db_ref):
        @pl.when((pl.program_id(0) == 0) & (pl.program_id(1) == 0))
        def _():
            dw_ref[...] = jnp.zeros_like(dw_ref)
            db_ref[...] = jnp.zeros_like(db_ref)
        hm, bdr_v = hm_ref[...], bdr_ref[...]
        _, vjp = jax.vjp(lambda *a: _sg_block(*a, hm, bdr_v), *_halves(u_ref), *_halves(v_ref), *_halves(z_ref),
                         w_ref[...], b_ref[...])
        du0, du1, dv0, dv1, dz0, dz1, dw, db = vjp(_halves(dy_ref))
        for ref, a0, a1 in ((du_ref, du0, du1), (dv_ref, dv0, dv1), (dz_ref, dz0, dz1)):
            ref[0, :, :PAIR_W] = a0.astype(bf16)
            ref[0, :, PAIR_W:] = a1.astype(bf16)
        dw_ref[...] += dw
        db_ref[...] += db

    def seg(k):
        return pl.BlockSpec((1, ts, BW), lambda bi, i: (bi, i, k))

    blk = pl.BlockSpec((1, ts, BW), lambda bi, i: (bi, i, 0))
    return pl.pallas_call(
        body, name="sg_bwd", grid=(nb, s // ts),
        in_specs=[seg(4), seg(5), seg(6), _full((NH, RC, RC)), _full((NH, RC)), _full(hmp.shape), _full(bdr.shape),
                  seg(1)],
        out_specs=[blk, blk, blk, _full((NH, RC, RC)), _full((NH, RC))],
        out_shape=[_sds((nb, s, BW), bf16)] * 3 + [_sds((NH, RC, RC)), _sds((NH, RC))],
        compiler_params=_cparams(("arbitrary", "arbitrary"), VMEM_BIG),
    )(p3, p3, p3, w, b, hmp, bdr, dy3)


def _sc_fn(b, c, h, z, w, sd, su):
    return b * _conv3(c * h, w, sd, su) * _silu(z)


def sc_fwd(p3, w, t_ctx):
    nb, s, _ = p3.shape
    sd, su = _make_shifts(t_ctx, s)

    def body(b_ref, c_ref, h_ref, z_ref, w_ref, y_ref):
        y_ref[0] = _sc_fn(b_ref[0], c_ref[0], h_ref[0], z_ref[0], w_ref[...], sd, su).astype(bf16)

    def seg(k):
        return pl.BlockSpec((1, s, 128), lambda bi, j: (bi, 0, 2 * k + j))

    return pl.pallas_call(
        body, name="sc_fwd", grid=(nb, 2),
        in_specs=[seg(7), seg(8), seg(9), seg(10), pl.BlockSpec((3, 128), lambda bi, j: (0, j))],
        out_specs=pl.BlockSpec((1, s, 128), lambda bi, j: (bi, 0, j)),
        out_shape=_sds((nb, s, BW), bf16),
        compiler_params=_cparams(("arbitrary", "arbitrary"), VMEM_BIG),
    )(p3, p3, p3, p3, w)


def sc_bwd(p3, w, dy3, t_ctx):
    nb, s, _ = p3.shape
    sd, su = _make_shifts(t_ctx, s)

    def body(b_ref, c_ref, h_ref, z_ref, w_ref, dy_ref, db_ref, dc_ref, dh_ref, dz_ref, dw_ref):
        @pl.when(pl.program_id(1) == 0)
        def _():
            dw_ref[...] = jnp.zeros_like(dw_ref)
        _, vjp = jax.vjp(lambda b, c, h, z, w_: _sc_fn(b, c, h, z, w_, sd, su),
                         b_ref[0], c_ref[0], h_ref[0], z_ref[0], w_ref[...])
        db, dc, dh, dz, dw = vjp(dy_ref[0])
        db_ref[0] = db.astype(bf16)
        dc_ref[0] = dc.astype(bf16)
        dh_ref[0] = dh.astype(bf16)
        dz_ref[0] = dz.astype(bf16)
        dw_ref[...] += dw

    def seg(k):
        return pl.BlockSpec((1, s, 128), lambda j, bi: (bi, 0, 2 * k + j))

    blk = pl.BlockSpec((1, s, 128), lambda j, bi: (bi, 0, j))
    wspec = pl.BlockSpec((3, 128), lambda j, bi: (0, j))
    return pl.pallas_call(
        body, name="sc_bwd", grid=(2, nb),
        in_specs=[seg(7), seg(8), seg(9), seg(10), wspec, seg(2)],
        out_specs=[blk, blk, blk, blk, wspec],
        out_shape=[_sds((nb, s, BW), bf16)] * 4 + [_sds((3, BW))],
        compiler_params=_cparams(("arbitrary", "arbitrary"), VMEM_BIG),
    )(p3, p3, p3, p3, w, dy3)


def assemble_dp(pairs, singles_a, gdn_x, singles_b, gates):
    nb, s, _ = singles_a[0].shape
    flat = [a for pr in pairs for a in pr] + list(singles_a) + list(gdn_x) + list(singles_b) + list(gates)
    n_pairs, n_a, n_x, n_b = len(pairs), len(singles_a), len(gdn_x), len(singles_b)

    def body(*refs):
        out = refs[-1]
        ins = refs[:-1]
        col = 0
        for p in range(n_pairs):
            out[0, :, col:col + BW] = (ins[2 * p][0].astype(f32) + ins[2 * p + 1][0].astype(f32)).astype(bf16)
            col += BW
        k = 2 * n_pairs
        for _ in range(n_a + n_x + n_b):
            out[0, :, col:col + BW] = ins[k][0]
            col += BW
            k += 1
        out[0, :, col:col + 128] = (ins[k][0].astype(f32) + ins[k + 1][0].astype(f32)).astype(bf16)
        out[0, :, col + 128:] = jnp.zeros((TM, PW - col - 128), bf16)

    def spec(a):
        return pl.BlockSpec((1, TM, a.shape[-1]), lambda b, j: (b, j, 0))

    return pl.pallas_call(
        body, name="assemble_dp", grid=(nb, s // TM),
        in_specs=[spec(a) for a in flat],
        out_specs=pl.BlockSpec((1, TM, PW), lambda b, j: (b, j, 0)),
        out_shape=_sds((nb, s, PW), bf16),
        compiler_params=_cparams(("arbitrary", "arbitrary")),
    )(*flat)


def mod_fwd(c_rows, w_mod, b_cols):
    nl, _, wc = w_mod.shape
    nr = c_rows.shape[0]

    def body(c_ref, w_ref, b_ref, o_ref):
        o_ref[0] = _dot(_silu(c_ref[...]), w_ref[0], precision=HI) + b_ref[0]

    return pl.pallas_call(
        body, name="mod_fwd", grid=(nl,),
        in_specs=[_full((nr, D)), pl.BlockSpec((1, D, wc), lambda l: (l, 0, 0)), pl.BlockSpec((1, 1, wc), lambda l: (l, 0, 0))],
        out_specs=pl.BlockSpec((1, nr, wc), lambda l: (l, 0, 0)),
        out_shape=_sds((nl, nr, wc)),
        compiler_params=_cparams(("arbitrary",)),
    )(c_rows, w_mod, b_cols)


def mod_bwd(c_rows, w_mod, dm_cols, dm_full):
    nl, _, wc = w_mod.shape
    nr = c_rows.shape[0]

    def body(c_ref, w_ref, dmc_ref, dmf_ref, gw_ref, gb_ref, dcc_ref):
        @pl.when(pl.program_id(0) == 0)
        def _():
            dcc_ref[...] = jnp.zeros_like(dcc_ref)
        a = _silu(c_ref[...])
        dmc = dmc_ref[0]
        gw_ref[0] = _dot_tn(a, dmc, precision=HI)
        gb_ref[0] = jnp.sum(dmf_ref[0], axis=0, keepdims=True)
        dcc_ref[...] += _dot_nt(dmc[nr - 8:nr], w_ref[0], precision=HI)

    return pl.pallas_call(
        body, name="mod_bwd", grid=(nl,),
        in_specs=[_full((nr, D)), pl.BlockSpec((1, D, wc), lambda l: (l, 0, 0)),
                  pl.BlockSpec((1, nr, wc), lambda l: (l, 0, 0)), pl.BlockSpec((1, nr, 3 * D), lambda l: (l, 0, 0))],
        out_specs=[pl.BlockSpec((1, D, wc), lambda l: (l, 0, 0)), pl.BlockSpec((1, 1, 3 * D), lambda l: (l, 0, 0)),
                   _full((8, D))],
        out_shape=[_sds((nl, D, wc)), _sds((nl, 1, 3 * D)), _sds((8, D))],
        compiler_params=_cparams(("arbitrary",)),
    )(c_rows, w_mod, dm_cols, dm_full)


def cctx_grad(parts, c_ctx):
    def body(p_ref, c_ref, o_ref):
        tot = p_ref[0, 0:1, :]
        for k in (2, 4, 6):
            tot = tot + p_ref[k, 0:1, :]
        c = c_ref[...]
        sg = jax.nn.sigmoid(c)
        o_ref[...] = tot * (sg * (1.0 + c * (1.0 - sg)))

    return pl.pallas_call(body, name="cctx_grad", out_shape=_sds((1, D)))(parts, c_ctx)


def sum_lead(x, out_dtype=f32, tr=256, tc=None):
    k, r, c = x.shape
    tr = min(tr, r)
    tc = c if tc is None else tc
    assert r % tr == 0 and c % tc == 0

    def body(x_ref, o_ref):
        tot = x_ref[0].astype(f32)
        for i in range(1, k):
            tot = tot + x_ref[i].astype(f32)
        o_ref[...] = tot.astype(out_dtype)

    return pl.pallas_call(
        body, name="sum_lead", grid=(r // tr, c // tc),
        in_specs=[pl.BlockSpec((k, tr, tc), lambda i, j: (0, i, j))],
        out_specs=pl.BlockSpec((tr, tc), lambda i, j: (i, j)),
        out_shape=_sds((r, c), out_dtype),
        compiler_params=_cparams(("arbitrary", "arbitrary")),
    )(x)


def adamw(w, m, v, g1, g2=None, tr=256, block=None):
    if block is None:
        block = (1,) * (w.ndim - 2) + (min(tr, w.shape[-2]), w.shape[-1])
    assert len(block) == w.ndim and all(d % b == 0 for d, b in zip(w.shape, block))
    two = g2 is not None
    c1 = 1.0 / (1.0 - ADAM_B1 ** ADAM_STEP)
    c2 = 1.0 / (1.0 - ADAM_B2 ** ADAM_STEP)

    def body(*refs):
        w_ref, m_ref, v_ref, g_ref = refs[:4]
        g = g_ref[...]
        if two:
            g = g + refs[4][...]
        go_ref, d_ref, mo_ref, vo_ref = refs[-4:]
        mn = ADAM_B1 * m_ref[...] + (1.0 - ADAM_B1) * g
        vn = ADAM_B2 * v_ref[...] + (1.0 - ADAM_B2) * (g * g)
        go_ref[...] = g
        mo_ref[...] = mn
        vo_ref[...] = vn
        d_ref[...] = -ADAM_LR * ((mn * c1) / (jnp.sqrt(vn * c2) + ADAM_EPS) + ADAM_WD * w_ref[...])

    blk = pl.BlockSpec(block, lambda *i: i)
    grid = tuple(d // b for d, b in zip(w.shape, block))
    args = [w, m, v, g1] + ([g2] if two else [])
    return pl.pallas_call(
        body, name="adamw", grid=grid,
        in_specs=[blk] * len(args), out_specs=[blk] * 4, out_shape=[_sds(w.shape)] * 4,
        compiler_params=_cparams(("arbitrary",) * len(grid)),
    )(*args)


def _my_pos():
    return lax.axis_index("x"), lax.axis_index("y"), lax.axis_index("c")


GROUP_SIZE = {"devices": N_DEV, "chips": N_CHIPS, "cores": 2}


def _peer_exchange(kind, group, src_refs, dst_refs, send_sems, recv_sems, local_sems):
    mx, my, mc = _my_pos()
    n = GROUP_SIZE[group]
    if group == "devices":
        me = 4 * mx + 2 * my + mc
    elif group == "chips":
        me = 2 * mx + my
    else:
        me = mc

    def peer(k):
        if group == "devices":
            return (mx ^ (k >> 2), my ^ ((k >> 1) & 1), mc ^ (k & 1))
        if group == "chips":
            return (mx ^ (k >> 1), my ^ (k & 1), mc)
        return (mx, my, mc ^ k)

    def copies():
        local, sends, recvs = [], [], []
        for i, (src, dst) in enumerate(zip(src_refs, dst_refs)):
            def part(k):
                return src.at[k] if kind == "scatter" else src

            def slab(k):
                return dst if kind == "send" else dst.at[k]

            if kind != "send":
                local.append(pltpu.make_async_copy(part(me), dst.at[me], local_sems.at[i]))
            for k in range(1, n):
                sem = dict(send_sem=send_sems.at[i, k - 1], recv_sem=recv_sems.at[i, k - 1], device_id_type=MESH)
                sends.append(pltpu.make_async_remote_copy(src_ref=part(me ^ k), dst_ref=slab(me), device_id=peer(k), **sem))
                recvs.append(pltpu.make_async_remote_copy(src_ref=part(me ^ k), dst_ref=slab(me ^ k),
                                                          device_id=(mx, my, mc), **sem))
        return local, sends, recvs

    def start():
        local, sends, _ = copies()
        for cp in local + sends:
            cp.start()

    def wait():
        local, sends, recvs = copies()
        for cp in recvs:
            cp.wait_recv()
        for cp in sends:
            cp.wait_send()
        for cp in local:
            cp.wait()

    return start, wait


def _exchange_scratch(group, n):
    k = GROUP_SIZE[group] - 1
    return [pltpu.SemaphoreType.DMA((n, k)), pltpu.SemaphoreType.DMA((n, k)), pltpu.SemaphoreType.DMA((n,))]


def _exchange_shapes(kind, group, arrs):
    return [_sds(((GROUP_SIZE[group],) + a.shape) if kind == "gather" else a.shape, a.dtype) for a in arrs]


def exchange(name, parts):
    counts = [len(arrs) for _, _, arrs in parts]
    total = sum(counts)

    def body(*refs):
        srcs, dsts, sems = refs[:total], refs[total:2 * total], refs[2 * total:]
        ops, at = [], 0
        for j, (kind, group, arrs) in enumerate(parts):
            ops.append(_peer_exchange(kind, group, srcs[at:at + counts[j]], dsts[at:at + counts[j]], *sems[3 * j:3 * j + 3]))
            at += counts[j]
        for start, _ in ops:
            start()
        for _, wait in ops:
            wait()

    any_ = pl.BlockSpec(memory_space=pl.ANY)
    flat = [a for _, _, arrs in parts for a in arrs]
    outs = pl.pallas_call(
        body, name=name, out_shape=[sh for kind, group, arrs in parts for sh in _exchange_shapes(kind, group, arrs)],
        in_specs=[any_] * total, out_specs=[any_] * total,
        scratch_shapes=[sc for _, group, arrs in parts for sc in _exchange_scratch(group, len(arrs))],
    )(*flat)
    res, at = [], 0
    for cnt in counts:
        res.append(list(outs[at:at + cnt]))
        at += cnt
    return res


def gather8(x):
    return exchange("gather8", [("gather", "devices", [x])])[0][0]


PACK_ROWS = 64
SMALL = ("c_ctx", "b_mod", "g_pre", "g_post", "ret_norm_g", "sg_w", "sg_b", "sc_conv_w", "gdn_conv_w",
         "gdn_a_log", "gdn_dt_bias", "gdn_norm_g")


def _pack(arrs, width=D, mult=PACK_ROWS):
    rows = []
    for a in arrs:
        flat = a.reshape(-1)
        pad = (-flat.shape[0]) % width
        rows.append(jnp.pad(flat, (0, pad)).reshape(-1, width))
    out = jnp.concatenate(rows, axis=0)
    return jnp.pad(out, ((0, (-out.shape[0]) % mult), (0, 0)))


def _unpack(packed, shapes, width=D):
    outs, r = [], 0
    for shp in shapes:
        size = int(np.prod(shp))
        nr = -(-size // width)
        outs.append(packed[r:r + nr].reshape(-1)[:size].reshape(shp))
        r += nr
    return outs


def kernel(x, c, ctx, c_ctx, w_mod, b_mod, g_pre, g_post, w_in, w_out, ret_norm_g, sg_w, sg_b, sc_conv_w, gdn_conv_w, gdn_a_log, gdn_dt_bias, gdn_norm_g, loss_target, m_c_ctx, m_w_mod, m_b_mod, m_g_pre, m_g_post, m_w_in, m_w_out, m_ret_norm_g, m_sg_w, m_sg_b, m_sc_conv_w, m_gdn_conv_w, m_gdn_a_log, m_gdn_dt_bias, m_gdn_norm_g, v_c_ctx, v_w_mod, v_b_mod, v_g_pre, v_g_post, v_w_in, v_w_out, v_ret_norm_g, v_sg_w, v_sg_b, v_sc_conv_w, v_gdn_conv_w, v_gdn_a_log, v_gdn_dt_bias, v_gdn_norm_g):
    weights = dict(c_ctx=c_ctx, w_mod=w_mod, b_mod=b_mod, g_pre=g_pre, g_post=g_post, w_in=w_in, w_out=w_out,
                   ret_norm_g=ret_norm_g, sg_w=sg_w, sg_b=sg_b, sc_conv_w=sc_conv_w, gdn_conv_w=gdn_conv_w,
                   gdn_a_log=gdn_a_log, gdn_dt_bias=gdn_dt_bias, gdn_norm_g=gdn_norm_g)
    mom = dict(c_ctx=m_c_ctx, w_mod=m_w_mod, b_mod=m_b_mod, g_pre=m_g_pre, g_post=m_g_post, w_in=m_w_in,
               w_out=m_w_out, ret_norm_g=m_ret_norm_g, sg_w=m_sg_w, sg_b=m_sg_b, sc_conv_w=m_sc_conv_w,
               gdn_conv_w=m_gdn_conv_w, gdn_a_log=m_gdn_a_log, gdn_dt_bias=m_gdn_dt_bias, gdn_norm_g=m_gdn_norm_g)
    var = dict(c_ctx=v_c_ctx, w_mod=v_w_mod, b_mod=v_b_mod, g_pre=v_g_pre, g_post=v_g_post, w_in=v_w_in,
               w_out=v_w_out, ret_norm_g=v_ret_norm_g, sg_w=v_sg_w, sg_b=v_sg_b, sc_conv_w=v_sc_conv_w,
               gdn_conv_w=v_gdn_conv_w, gdn_a_log=v_gdn_a_log, gdn_dt_bias=v_gdn_dt_bias, gdn_norm_g=v_gdn_norm_g)

    nb, t_lat, _ = x.shape
    t_ctx = ctx.shape[1]
    s = t_ctx + t_lat
    n = nb * s
    sb = s // TM
    nl = w_in.shape[0]
    wc_in = w_in.shape[2]
    wc_mod = w_mod.shape[2]
    rows_out = w_out.shape[1]
    n_all = nb * N_DEV
    mx, my, mc = _my_pos()
    chip = 2 * mx + my
    dev = 2 * chip + mc

    sg_c = _sg_consts()
    bd = jnp.asarray(_block_diag())
    ret_c = _ret_consts(nb)
    gdn_c = _gdn_consts(nb)
    cos, sins = _rope_tables(t_lat, t_ctx)

    w_in_b, w_out_b = w_in.astype(bf16), w_out.astype(bf16)
    pre = _pack([c, sc_conv_w, gdn_conv_w], mult=8)
    (pre_all,), w0_parts = exchange("startup_gather", [("gather", "devices", [pre]), ("gather", "chips", [w_in_b[0]])])
    c_parts, scw_parts, gcw_parts = [], [], []
    for k in range(N_DEV):
        ck, sk, gk = _unpack(pre_all[k], [c.shape, sc_conv_w.shape, gdn_conv_w.shape])
        c_parts.append(ck)
        if k % 2 == 0:
            scw_parts.append(sk)
            gcw_parts.append(gk)
    c_all = jnp.concatenate(c_parts, axis=0)
    sc_w_full = jnp.concatenate(scw_parts, axis=-1)
    gdn_w_full = jnp.concatenate(gcw_parts, axis=-1)
    c_rows = jnp.concatenate([c_all, c_ctx[None, :], jnp.zeros((7, D), f32)], axis=0)

    b_cols = lax.dynamic_slice_in_dim(b_mod, chip * wc_mod, wc_mod, axis=1)[:, None, :]
    mod_part = mod_fwd(c_rows, w_mod, b_cols)
    mod_all = gather8(mod_part)
    mod = jnp.concatenate([mod_all[2 * k] for k in range(N_CHIPS)], axis=-1)
    my_rows = jnp.concatenate([lax.dynamic_slice_in_dim(mod, dev * nb, nb, axis=1), mod[:, n_all:n_all + 1]], axis=1)
    shift_t = my_rows[:, :, None, 0:D]
    scale_t = my_rows[:, :, None, D:2 * D]
    gate_t = my_rows[:, :, None, 2 * D:3 * D]

    w_in_full, w_out_full = [None] * nl, [None] * nl
    w_in_full[0] = place_weights(w0_parts[0])

    alog = jnp.pad(gdn_a_log.reshape(nl, 1, 8), ((0, 0), (0, 0), (0, 120)))
    dtb = jnp.pad(gdn_dt_bias.reshape(nl, 1, 8), ((0, 0), (0, 0), (0, 120)))
    gdn_ng = jnp.tile(gdn_norm_g, (1, NH))[:, None, :]
    ret_ng = ret_norm_g[:, None, :]

    xs = jnp.concatenate([ctx, x], axis=1).reshape(n, D)
    saved = []
    for l in range(nl):
        p, h = inproj_fwd(xs, shift_t[l], scale_t[l], g_pre[l][None, :], w_in_full[l], nb, sb)
        p3 = p.reshape(nb, s, PW)
        ro_f, ro_b, rs_all = ret_scan_fwd(p3, cos, sins, ret_c, t_ctx)
        y_ret = mix_finish_fwd(_ret_finish, "ret_finish_fwd", ro_f, ro_b, p3, 3, ret_ng[l], bd)
        y_sg = sg_fwd(p3, sg_w[l], sg_b[l], *sg_c)
        y_sc = sc_fwd(p3, sc_w_full[l], t_ctx)
        cq, ck, cv = [gdn_conv_fwd(p3, gdn_w_full[l][:, BW * i:BW * (i + 1)], 11 + i, t_ctx) for i in range(3)]
        riding = [w_out_b[l]] + ([w_in_b[l + 1]] if l + 1 < nl else [])
        go_f, go_b, *gs_all = gdn_scan_fwd(cq, ck, cv, p3, alog[l], dtb[l], gdn_c, t_ctx, ("gather", riding))
        w_out_full[l] = gs_all[2].reshape(D, D)
        if l + 1 < nl:
            w_in_full[l + 1] = place_weights(gs_all[3])
        gs_all = gs_all[:2]
        y_gdn = mix_finish_fwd(_gdn_finish, "gdn_finish_fwd", go_f, go_b, p3, 14, gdn_ng[l], bd)
        ys = [a.reshape(n, BW) for a in (y_ret, y_sg, y_sc, y_gdn)]
        x_new, o = outproj_fwd(ys, w_out_full[l], xs, gate_t[l], g_post[l][None, :], nb, sb)
        saved.append(dict(x=xs, h=h, p3=p3, ro=(ro_f, ro_b), rs=rs_all, c=(cq, ck, cv), go=(go_f, go_b), gs=gs_all,
                          ys=ys, o=o))
        xs = x_new

    dx3, loss_part = loss_head(xs.reshape(nb, s, D), loss_target, t_ctx)
    loss = lax.psum(loss_part[0, 0], ("x", "y", "c"))

    dxs = dx3.reshape(n, D)
    g_small = {k: [None] * nl for k in SMALL if k not in ("c_ctx", "b_mod")}
    dm_rows = [None] * nl
    slab_in = None
    got_in, got_out = [None] * nl, [None] * nl
    for l in reversed(range(nl)):
        sv = saved[l]
        p3 = sv["p3"]
        dy, gw_out, dg_post, dgate = outproj_bwd(dxs, sv["o"], gate_t[l], g_post[l][None, :], sv["ys"], w_out_full[l], nb, sb)
        dy3 = dy.reshape(nb, s, D)
        r_do, r_dz, d_rng = mix_finish_bwd(_ret_finish, "ret_finish_bwd", *sv["ro"], p3, 3, ret_ng[l], bd, dy3, 0)
        r_d = ret_scan_bwd(p3, cos, sins, ret_c, sv["rs"], r_do, t_ctx)
        s_du, s_dv, s_dz, d_sgw, d_sgb = sg_bwd(p3, sg_w[l], sg_b[l], *sg_c, dy3)
        c_db, c_dc, c_dh, c_dz, d_scw = sc_bwd(p3, sc_w_full[l], dy3, t_ctx)
        g_do, g_dz, d_gng = mix_finish_bwd(_gdn_finish, "gdn_finish_bwd", *sv["go"], p3, 14, gdn_ng[l], bd, dy3, 3)
        riding = [gw_out.reshape(N_CHIPS, rows_out, D).astype(bf16)] + ([] if slab_in is None else [slab_in])
        g_d = gdn_scan_bwd(*sv["c"], p3, alog[l], dtb[l], gdn_c, *sv["gs"], g_do, t_ctx, ("scatter", riding))
        got_out[l] = g_d[10]
        if slab_in is not None:
            got_in[l + 1] = g_d[11]
        gx, d_gcw = [], []
        for i in range(3):
            dxi, dwi = gdn_conv_bwd(p3, gdn_w_full[l][:, BW * i:BW * (i + 1)], 11 + i, g_d[i], g_d[4 + i], t_ctx)
            gx.append(dxi)
            d_gcw.append(dwi)
        dp3 = assemble_dp([(r_d[0], r_d[3]), (r_d[1], r_d[4]), (r_d[2], r_d[5])],
                          [r_dz, s_du, s_dv, s_dz, c_db, c_dc, c_dh, c_dz], gx, [g_dz], [g_d[3], g_d[7]])
        dp = dp3.reshape(n, PW)
        gw_in = dw_in(sv["h"], dp)
        slab_in = jnp.stack([gw_in[k * wc_in:(k + 1) * wc_in] for k in range(N_CHIPS)])
        dxs, dg_pre, dshift, dscale, *got = inproj_bwd_x(dp, w_in_full[l], sv["x"], scale_t[l], g_pre[l][None, :], dxs, nb, sb,
                                                         ("scatter", [slab_in]) if l == 0 else None)
        if l == 0:
            got_in[0] = got[0]
        g_small["g_pre"][l] = dg_pre[0]
        g_small["g_post"][l] = dg_post[0]
        g_small["ret_norm_g"][l] = d_rng[0]
        g_small["sg_w"][l] = d_sgw
        g_small["sg_b"][l] = d_sgb
        g_small["sc_conv_w"][l] = d_scw
        g_small["gdn_conv_w"][l] = jnp.concatenate(d_gcw, axis=-1)
        g_small["gdn_a_log"][l] = g_d[8][0, :8].reshape(2, NH)
        g_small["gdn_dt_bias"][l] = g_d[9][0, :8].reshape(2, NH)
        g_small["gdn_norm_g"][l] = d_gng[0].reshape(NH, HD)
        dm_rows[l] = jnp.concatenate([dshift, dscale, dgate], axis=-1)[:nb + 1]
    grad_x = dxs.reshape(nb, s, D)[:, t_ctx:, :]

    g_small = {k: jnp.stack(v) for k, v in g_small.items()}
    dm_rows = jnp.stack(dm_rows)
    names2 = [k for k in SMALL if k not in ("c_ctx", "b_mod")]
    pack_sum = _pack([g_small[k] for k in names2] + [dm_rows[:, nb:]])
    pack_own = _pack([dm_rows[:, :nb]], mult=8)
    rs = -(-pack_sum.shape[0] // (8 * N_DEV)) * 8
    slabs_sum = jnp.pad(pack_sum, ((0, N_DEV * rs - pack_sum.shape[0]), (0, 0))).reshape(N_DEV, rs, D)
    ((got_small,),) = exchange("tail_scatter", [("scatter", "devices", [slabs_sum])])
    my_slab = sum_lead(got_small, tr=rs)
    gin_mine = jnp.stack([sum_lead(a, tr=wc_in, tc=256) for a in got_in], axis=1)
    gout_mine = jnp.stack([sum_lead(a) for a in got_out])
    (all2,), (gin_sib, gout_sib) = exchange("tail_gather", [
        ("gather", "devices", [jnp.concatenate([my_slab, pack_own], axis=0)]), ("send", "cores", [gin_mine, gout_mine])])
    tot2 = all2[:, :rs].reshape(N_DEV * rs, D)
    outs2 = _unpack(tot2, [g_small[k].shape for k in names2] + [(nl, 1, 3 * D)])
    grads = dict(zip(names2, outs2[:-1]))
    dm_own = jnp.stack([_unpack(all2[k, rs:], [(nl, nb, 3 * D)])[0] for k in range(N_DEV)])
    dm_own = jnp.transpose(dm_own, (1, 0, 2, 3)).reshape(nl, n_all, 3 * D)
    dm_all = jnp.concatenate([dm_own, jnp.pad(outs2[-1], ((0, 0), (0, 7), (0, 0)))], axis=1)
    grads["gdn_norm_g"] = sum_lead(jnp.transpose(grads["gdn_norm_g"], (1, 0, 2)), tr=nl)
    for k in ("sc_conv_w", "gdn_conv_w"):
        wc = weights[k].shape[2]
        grads[k] = lax.dynamic_slice_in_dim(grads[k], chip * wc, wc, axis=2)

    dm_cols = lax.dynamic_slice_in_dim(dm_all, chip * wc_mod, wc_mod, axis=2)
    g_w_mod, g_b_mod, dcc_part = mod_bwd(c_rows, w_mod, dm_cols, dm_all)
    grads["b_mod"] = g_b_mod[:, 0, :]
    grads["c_ctx"] = cctx_grad(gather8(dcc_part), c_ctx[None, :])[0]

    res = {}
    w_in_t, m_in_t, v_in_t = [jnp.transpose(a, (2, 0, 1)) for a in (w_in, m_w_in, v_w_in)]
    res["w_in"] = [jnp.transpose(a, (1, 2, 0)) for a in
                   adamw(w_in_t, m_in_t, v_in_t, gin_mine, gin_sib, block=(wc_in // 4, nl, 256))]
    res["w_out"] = adamw(w_out, m_w_out, v_w_out, gout_mine, gout_sib)
    res["w_mod"] = adamw(w_mod, m_w_mod, v_w_mod, g_w_mod)
    shapes = [weights[k].shape for k in SMALL]
    small = adamw(_pack([weights[k] for k in SMALL]), _pack([mom[k] for k in SMALL]), _pack([var[k] for k in SMALL]),
                  _pack([grads[k].reshape(weights[k].shape) for k in SMALL]), tr=PACK_ROWS)
    small = [_unpack(a, shapes) for a in small]
    for i, k in enumerate(SMALL):
        res[k] = [small[j][i] for j in range(4)]

    order = ["c_ctx", "w_mod", "b_mod", "g_pre", "g_post", "w_in", "w_out", "ret_norm_g", "sg_w", "sg_b", "sc_conv_w",
             "gdn_conv_w", "gdn_a_log", "gdn_dt_bias", "gdn_norm_g"]
    return (loss, grad_x, *[res[k][0] for k in order], *[res[k][1] for k in order], *[res[k][2] for k in order],
            *[res[k][3] for k in order])
```

```python
import functools

import jax
import jax.numpy as jnp
import numpy as np
from jax import lax
from jax.experimental import pallas as pl
from jax.experimental.pallas import tpu as pltpu

f32 = jnp.float32
bf16 = jnp.bfloat16
HI = lax.Precision.HIGHEST
P3 = lax.Precision.HIGH
MESH = pl.DeviceIdType.MESH

EPS = 1e-6
D = 1024
NH = 4
HD = 64
BW = NH * HD
PAIR_W = 2 * HD
RC = 128
GC = 64
GRID_W = 64
ROPE_BASE = 10000.0
IN_W = 15 * BW + 16
PW = 4096
GATE_COL = 15 * BW
N_CHIPS = 4
N_DEV = 8
TM = 256
TP = 2 * TM
ADAM_LR, ADAM_B1, ADAM_B2, ADAM_EPS, ADAM_WD, ADAM_STEP = 0.001, 0.9, 0.999, 1e-08, 0.01, 10
LANE_HEAD = np.arange(BW) // HD
VMEM_BIG = 56 * 1024 * 1024


def _dot(a, b, precision=None):
    return jnp.dot(a, b, precision=precision, preferred_element_type=f32)


def _dot_nt(a, b, precision=None):
    return lax.dot_general(a, b, (((1,), (1,)), ((), ())), precision=precision, preferred_element_type=f32)


def _dot_tn(a, b, precision=None):
    return lax.dot_general(a, b, (((0,), (0,)), ((), ())), precision=precision, preferred_element_type=f32)


def _sds(shape, dtype=f32):
    return jax.ShapeDtypeStruct(shape, dtype)


def _cparams(sem=None, vmem=None):
    kw = {}
    if sem is not None:
        kw["dimension_semantics"] = sem
    if vmem is not None:
        kw["vmem_limit_bytes"] = vmem
    return pltpu.CompilerParams(**kw)


def _full(shape):
    n = len(shape)
    return pl.BlockSpec(shape, lambda *_: (0,) * n)


def _block_diag():
    return (LANE_HEAD[:, None] == LANE_HEAD[None, :]).astype(np.float32)


def _tau(c, d):
    return np.arange(c) if d == 0 else c - 1 - np.arange(c)


def _ret_consts(nb):
    lg = np.log(1.0 - 2.0 ** (-5.0 - np.arange(NH)))
    intra = np.zeros((2, 2, RC, 2 * RC)); qdec = np.zeros((2, 2, RC, PAIR_W)); kdec = np.zeros((2, 2, RC, PAIR_W))
    cd = np.zeros((2, 2, PAIR_W, PAIR_W))
    for d in range(2):
        t = _tau(RC, d)
        diff = t[:, None] - t[None, :]
        for p in range(2):
            lane_lg = lg[2 * p + np.arange(PAIR_W) // HD]
            for h in range(2):
                intra[d, p, :, h * RC:(h + 1) * RC] = np.where(diff >= 0, np.exp(np.maximum(diff, 0) * lg[2 * p + h]), 0.0)
            qdec[d, p] = np.exp((t[:, None] + 1.0) * lane_lg[None, :])
            kdec[d, p] = np.exp((RC - 1.0 - t[:, None]) * lane_lg[None, :])
            cd[d, p] = np.exp(RC * lane_lg)[:, None] * np.ones((1, PAIR_W))
    per_z = [np.tile(a.reshape((4,) + a.shape[2:]), (nb, 1, 1)) for a in (intra, qdec, kdec, cd)]
    bd2 = (np.arange(PAIR_W)[:, None] // HD == np.arange(PAIR_W)[None, :] // HD)
    bdr = (np.arange(2 * RC)[:, None] // RC == np.arange(PAIR_W)[None, :] // HD)
    return [jnp.asarray(a, f32) for a in per_z + [bd2, bdr]]


def _rope_tables(t_lat, t_ctx):
    nf = HD // 4
    inv = ROPE_BASE ** (-np.arange(nf) / nf)
    pos = np.arange(t_lat)
    ang_r = (pos // GRID_W)[:, None] * inv[None, :]
    ang_c = (pos % GRID_W)[:, None] * inv[None, :]
    ang = np.concatenate([ang_r, ang_r, ang_c, ang_c], axis=1)
    sign = np.concatenate([-np.ones(nf), np.ones(nf), -np.ones(nf), np.ones(nf)])
    cos = np.tile(np.cos(ang), (1, 2)); sins = np.tile(np.sin(ang) * sign, (1, 2))
    cos = np.concatenate([np.ones((t_ctx, PAIR_W)), cos]); sins = np.concatenate([np.zeros((t_ctx, PAIR_W)), sins])
    return jnp.asarray(cos, f32), jnp.asarray(sins, f32)


def _gdn_consts(nb):
    tmask = np.zeros((2, 2, GC, GC)); tmask2 = np.zeros((2, 2, GC, PAIR_W)); strict2 = np.zeros((2, 2, GC, PAIR_W))
    exp_g = np.zeros((2, 2, 128, PAIR_W)); exp_b = np.zeros((2, 2, 128, PAIR_W))
    for d in range(2):
        t = _tau(GC, d)
        tmask[d, :] = (t[:, None] >= t[None, :])
        tmask2[d, :] = np.tile(t[:, None] >= t[None, :], (1, 2))
        strict2[d, :] = np.tile(t[:, None] > t[None, :], (1, 2))
        for h in range(NH):
            exp_g[d, h // 2, 4 * d + h, (h % 2) * HD:(h % 2 + 1) * HD] = 1.0
            exp_b[d, h // 2, 8 + 4 * d + h, (h % 2) * HD:(h % 2 + 1) * HD] = 1.0
    exp_gt = np.transpose(exp_g, (0, 1, 3, 2))
    per_z = [np.tile(a.reshape((4,) + a.shape[2:]), (nb, 1, 1)) for a in (tmask, tmask2, strict2, exp_g, exp_b, exp_gt)]
    dsel2 = np.tile(np.eye(GC), (1, 2))
    eye2 = np.tile(np.eye(GC), (1, 2))
    bd2 = (np.arange(PAIR_W)[:, None] // HD == np.arange(PAIR_W)[None, :] // HD)
    return [jnp.asarray(a, f32) for a in per_z + [dsel2, eye2, bd2]]


def _swap16(x):
    lane = lax.broadcasted_iota(jnp.int32, x.shape, x.ndim - 1)
    n = x.shape[-1]
    return jnp.where(lane % 32 < 16, pltpu.roll(x, n - 16, axis=x.ndim - 1), pltpu.roll(x, 16, axis=x.ndim - 1))


@jax.custom_vjp
def _rot(x, cos, sins):
    return x * cos + _swap16(x) * sins


def _rot_fwd(x, cos, sins):
    return _rot(x, cos, sins), (cos, sins)


def _rot_bwd(res, g):
    cos, sins = res
    return g * cos + _swap16(g * sins), jnp.zeros_like(cos), jnp.zeros_like(sins)


_rot.defvjp(_rot_fwd, _rot_bwd)


def _silu(z):
    return z * jax.nn.sigmoid(z)


def _head_sum(x, bd):
    return _sel_r(x, bd)


def _ret_step(s, q, k, v, cos, sins, intra, qdec, kdec, cd, bd2, bdr):
    def bdiag(x):
        return jnp.concatenate([x, x], axis=1) * bdr

    qr = _rot(q, cos, sins)
    kr = _rot(k, cos, sins) * (HD ** -0.5)
    sc = _bmm_nt(qr, bdiag(kr)) * intra
    o = _bmm(qr * qdec, s) + _bmm(sc, bdiag(v))
    s_new = s * cd + bd2 * _bmm_tn(kr * kdec, v)
    return s_new, o


def _ret_finish(o_f, o_b, z, norm_g, bd):
    o = o_f + o_b
    mu = _head_sum(o, bd) * (1.0 / HD)
    xc = o - mu
    var = _head_sum(xc * xc, bd) * (1.0 / HD)
    return xc * lax.rsqrt(var + EPS) * norm_g * _silu(z)


def _softplus(x):
    return jnp.maximum(x, 0.0) + jnp.log(1.0 + jnp.exp(-jnp.abs(x)))


def _bmm(a, b, precision=None):
    return lax.dot_general(a, b, (((2,), (1,)), ((0,), (0,))), precision=precision, preferred_element_type=f32)


def _bmm_nt(a, b, precision=None):
    return lax.dot_general(a, b, (((2,), (2,)), ((0,), (0,))), precision=precision, preferred_element_type=f32)


def _bmm_tn(a, b, precision=None):
    return lax.dot_general(a, b, (((1,), (1,)), ((0,), (0,))), precision=precision, preferred_element_type=f32)


def _mm(a, b, mode):
    ca, cb = {"nn": (1, 0), "nt": (1, 1), "tn": (0, 0)}[mode]
    if a.ndim == 3:
        dims = (((ca + 1,), (cb + 1,)), ((0,), (0,)))
    else:
        dims = (((ca,), (cb,)), ((), ()))
    return lax.dot_general(a, b, dims, preferred_element_type=f32)


def _split(a):
    hi = a.astype(bf16)
    return hi, (a - hi.astype(f32)).astype(bf16)


def _sel2(a, e, mode, e_left):
    hi, lo = _split(a)
    eb = e.astype(bf16)
    if e_left:
        return _mm(eb, hi, mode) + _mm(eb, lo, mode)
    return _mm(hi, eb, mode) + _mm(lo, eb, mode)


@jax.custom_vjp
def _sel_r(a, e):
    return _sel2(a, e, "nn", False)


_sel_r.defvjp(lambda a, e: (_sel_r(a, e), e), lambda e, g: (_sel2(g, e, "nt", False), jnp.zeros_like(e)))


@jax.custom_vjp
def _sel_l(e, b):
    return _sel2(b, e, "nn", True)


_sel_l.defvjp(lambda e, b: (_sel_l(e, b), e), lambda e, g: (jnp.zeros_like(e), _sel2(g, e, "tn", True)))


def _bdiag(x, bd2):
    return jnp.concatenate([x, x], axis=1) * bd2


@jax.custom_vjp
def _solve_given_inv(m, vb, kbg, inv, bd2):
    return _bmm(inv, _bdiag(vb, bd2), P3), _bmm(inv, _bdiag(kbg, bd2), P3)


def _solve_fwd(m, vb, kbg, inv, bd2):
    u, w = _solve_given_inv(m, vb, kbg, inv, bd2)
    return (u, w), (inv, u, w, bd2)


def _solve_bwd(res, cts):
    inv, u, w, bd2 = res
    du, dw = cts
    c = inv.shape[1]
    t = jnp.swapaxes(_bdiag(inv, bd2), 1, 2)
    inv_t = t[:, :c] + t[:, c:]
    dvb = _bmm(inv_t, _bdiag(du, bd2), P3)
    dkbg = _bmm(inv_t, _bdiag(dw, bd2), P3)
    dm = _bmm_nt(dvb, _bdiag(u, bd2), P3) + _bmm_nt(dkbg, _bdiag(w, bd2), P3)
    return dm, dvb, dkbg, jnp.zeros_like(inv), jnp.zeros_like(bd2)


_solve_given_inv.defvjp(_solve_fwd, _solve_bwd)


def _gdn_step(s, q, k, v, gate, alog, dtb, tmask, tmask2, strict2, exp_g, exp_b, exp_gt, dsel2, eye2, bd2, inv=None):
    z, c, w_ = q.shape
    ne = gate.shape[0]

    def per_pair(a):
        return jnp.broadcast_to(a[:, None], (ne, z // ne) + a.shape[1:]).reshape((z,) + a.shape[1:])

    def rows(a):
        return a.reshape(z * c, w_)

    def bdiag(x):
        return _bdiag(x, bd2)

    g = per_pair(-jnp.exp(alog) * _softplus(gate + dtb))
    beta = per_pair(jax.nn.sigmoid(gate))
    gl = _sel_r(g, exp_g)
    gc_l = _sel_l(tmask, gl)
    glast_l = jnp.sum(gl, axis=1, keepdims=True)
    glast = jnp.sum(g, axis=1, keepdims=True)
    beta_l = _sel_r(beta, exp_b)
    gc_r = jnp.sum(gc_l * dsel2, axis=1, keepdims=True)
    qn = q * lax.rsqrt(_sel_r(rows(q * q), bd2).reshape(z, c, w_) + EPS)
    kn = k * lax.rsqrt(_sel_r(rows(k * k), bd2).reshape(z, c, w_) + EPS)
    eg = jnp.exp(gc_l)
    kb = kn * beta_l
    vb = v * beta_l
    kbg = kb * eg
    qs = qn * (HD ** -0.5)
    dec = jnp.exp(jnp.where(tmask2 > 0, gc_l - gc_r, -1e30))
    kns = bdiag(kn)
    m = -(_bmm_nt(kb, kns) * dec * strict2)
    if inv is None:
        inv = eye2 + m
        p = m
        for _ in range(5):
            p = _bmm(p, bdiag(p), P3)
            inv = inv + _bmm(inv, bdiag(p), P3)
        u = _bmm(inv, bdiag(vb), P3)
        w = _bmm(inv, bdiag(kbg), P3)
    else:
        u, w = _solve_given_inv(m, vb, kbg, inv, bd2)
    v_new = u - _bmm(w, s)
    k_tail = kn * jnp.exp(glast_l - gc_l)
    cdec = jnp.sum(exp_gt * jnp.exp(glast), axis=-1, keepdims=True)
    s_new = s * cdec + bd2 * _bmm_tn(k_tail, v_new)
    a = _bmm_nt(qs, kns) * dec
    o = _bmm(qs * eg, s) + _bmm(a, bdiag(v_new))
    return s_new, o, inv


def _gdn_finish(o_f, o_b, z, norm_g, bd):
    o = o_f + o_b
    ms = _head_sum(o * o, bd) * (1.0 / HD)
    return o * lax.rsqrt(ms + EPS) * norm_g * _silu(z)


def _gelu(x):
    return 0.5 * x * (1.0 + jnp.tanh(0.7978845608028654 * (x + 0.044715 * (x * x * x))))


def _sg_block(u0, u1, v0, v1, z0, z1, w, b, hmp, bdr):
    ts = u0.shape[0]
    nc = ts // RC
    g0, g1 = _gelu(v0), _gelu(v1)
    mu = (jnp.sum(g0, axis=-1, keepdims=True) + jnp.sum(g1, axis=-1, keepdims=True)) * (1.0 / BW)
    x0, x1 = g0 - mu, g1 - mu
    var = (jnp.sum(x0 * x0, axis=-1, keepdims=True) + jnp.sum(x1 * x1, axis=-1, keepdims=True)) * (1.0 / BW)
    rstd = lax.rsqrt(var + EPS)
    ys = []
    for p, (u, xc, z) in enumerate(((u0, x0, z0), (u1, x1, z1))):
        vn = (xc * rstd).reshape(nc, RC, PAIR_W)
        wp = jnp.concatenate([w[2 * p], w[2 * p + 1]], axis=1)
        mix = _bmm(jnp.broadcast_to(wp, (nc, RC, 2 * RC)), jnp.concatenate([vn, vn], axis=1) * bdr)
        bias = _dot_tn(b, hmp[p], precision=HI)
        s = (mix + bias).reshape(ts, PAIR_W)
        ys.append(_gelu(u) * s * _silu(z))
    return ys[0], ys[1]


def _make_shifts(t_ctx, n):
    def dn(x):
        t = lax.broadcasted_iota(jnp.int32, x.shape, 0)
        return jnp.where((t != 0) & (t != t_ctx), pltpu.roll(x, 1, axis=0), 0.0)

    def up(x):
        t = lax.broadcasted_iota(jnp.int32, x.shape, 0)
        return jnp.where((t != t_ctx - 1) & (t != n - 1), pltpu.roll(x, n - 1, axis=0), 0.0)

    @jax.custom_vjp
    def shift_dn(x):
        return dn(x)
    shift_dn.defvjp(lambda x: (dn(x), None), lambda _, g: (up(g),))

    @jax.custom_vjp
    def shift_up(x):
        return up(x)
    shift_up.defvjp(lambda x: (up(x), None), lambda _, g: (dn(g),))
    return shift_dn, shift_up


def _conv3(x, w, shift_dn, shift_up):
    return shift_dn(x) * w[0:1] + x * w[1:2] + shift_up(x) * w[2:3]


def inproj_fwd(x, shift_t, scale_t, g_pre, w_in, n_batch, sb):
    n = x.shape[0]

    def sel(i):
        return jnp.where(i % sb == 0, n_batch, i // sb)

    def body(x_ref, sh0, sh1, sc0, sc1, g_ref, w_ref, p_ref, h_ref):
        hs = []
        for k, (sh_ref, sc_ref) in enumerate(((sh0, sc0), (sh1, sc1))):
            xv = x_ref[k * TM:(k + 1) * TM, :]
            r = xv * lax.rsqrt(jnp.mean(xv * xv, axis=-1, keepdims=True) + EPS)
            hs.append(((r * g_ref[...]) * (1.0 + sc_ref[0]) + sh_ref[0]).astype(bf16))
        hb = jnp.concatenate(hs, axis=0)
        h_ref[...] = hb
        p_ref[...] = _dot(hb, w_ref[...])

    def mrow(k):
        return pl.BlockSpec((1, 1, D), lambda i: (sel(2 * i + k), 0, 0))

    return pl.pallas_call(
        body, name="inproj_fwd", grid=(n // TP,),
        in_specs=[pl.BlockSpec((TP, D), lambda i: (i, 0)), mrow(0), mrow(1), mrow(0), mrow(1),
                  _full((1, D)), _full((D, PW))],
        out_specs=[pl.BlockSpec((TP, PW), lambda i: (i, 0)), pl.BlockSpec((TP, D), lambda i: (i, 0))],
        out_shape=[_sds((n, PW)), _sds((n, D), bf16)],
        compiler_params=_cparams(("arbitrary",), VMEM_BIG),
    )(x, shift_t, shift_t, scale_t, scale_t, g_pre, w_in)


def outproj_fwd(ys, w_out, x, gate_t, g_post, n_batch, sb):
    n = x.shape[0]

    def sel(i):
        return jnp.where(i % sb == 0, n_batch, i // sb)

    def body(y0, y1, y2, y3, w_ref, x_ref, gt0, gt1, g_ref, xn_ref, o_ref):
        y = jnp.concatenate([y0[...], y1[...], y2[...], y3[...]], axis=1)
        o = _dot(y, w_ref[...])
        o_ref[...] = o
        nrm = o * lax.rsqrt(jnp.mean(o * o, axis=-1, keepdims=True) + EPS) * g_ref[...]
        for k, gt_ref in enumerate((gt0, gt1)):
            rows = slice(k * TM, (k + 1) * TM)
            xn_ref[rows, :] = x_ref[rows, :] + gt_ref[0] * nrm[rows]

    def mrow(k):
        return pl.BlockSpec((1, 1, D), lambda i: (sel(2 * i + k), 0, 0))

    yspec = pl.BlockSpec((TP, BW), lambda i: (i, 0))
    return pl.pallas_call(
        body, name="outproj_fwd", grid=(n // TP,),
        in_specs=[yspec, yspec, yspec, yspec, _full((D, D)), pl.BlockSpec((TP, D), lambda i: (i, 0)),
                  mrow(0), mrow(1), _full((1, D))],
        out_specs=[pl.BlockSpec((TP, D), lambda i: (i, 0)), pl.BlockSpec((TP, D), lambda i: (i, 0))],
        out_shape=[_sds((n, D)), _sds((n, D))],
        compiler_params=_cparams(("arbitrary",), VMEM_BIG),
    )(*ys, w_out, x, gate_t, gate_t, g_post)


def _row_onehot(r):
    return lax.broadcasted_iota(jnp.int32, (8, 1), 0) == r


def outproj_bwd(dxn, o, gate_t, g_post, ys, w_out, n_batch, sb):
    n = dxn.shape[0]

    def sel(i):
        return jnp.where(i % sb == 0, n_batch, i // sb)

    def body(dxn_ref, o_ref, gt0, gt1, g_ref, y0, y1, y2, y3, w_ref, dy_ref, dw_ref, dg_ref, dgate_ref):
        i = pl.program_id(0)

        @pl.when(i == 0)
        def _():
            dw_ref[...] = jnp.zeros_like(dw_ref)
            dg_ref[...] = jnp.zeros_like(dg_ref)
            dgate_ref[...] = jnp.zeros_like(dgate_ref)

        g = g_ref[...]
        dos = []
        for k, gt_ref in enumerate((gt0, gt1)):
            rows = slice(k * TM, (k + 1) * TM)
            ov = o_ref[rows, :]
            rstd = lax.rsqrt(jnp.mean(ov * ov, axis=-1, keepdims=True) + EPS)
            r = ov * rstd
            dx = dxn_ref[rows, :]
            dgate_ref[...] += jnp.where(_row_onehot(sel(2 * i + k)), jnp.sum(dx * (r * g), axis=0, keepdims=True), 0.0)
            dn = dx * gt_ref[0]
            dg_ref[...] += jnp.sum(dn * r, axis=0, keepdims=True)
            dr = dn * g
            dos.append((rstd * (dr - r * jnp.mean(dr * r, axis=-1, keepdims=True))).astype(bf16))
        dob = jnp.concatenate(dos, axis=0)
        dy_ref[...] = _dot_nt(dob, w_ref[...])
        y = jnp.concatenate([y0[...], y1[...], y2[...], y3[...]], axis=1)
        dw_ref[...] += _dot_tn(y, dob)

    def mrow(k):
        return pl.BlockSpec((1, 1, D), lambda i: (sel(2 * i + k), 0, 0))

    yspec = pl.BlockSpec((TP, BW), lambda i: (i, 0))
    row = pl.BlockSpec((TP, D), lambda i: (i, 0))
    return pl.pallas_call(
        body, name="outproj_bwd", grid=(n // TP,),
        in_specs=[row, row, mrow(0), mrow(1), _full((1, D)), yspec, yspec, yspec, yspec, _full((D, D))],
        out_specs=[row, _full((D, D)), _full((1, D)), _full((8, D))],
        out_shape=[_sds((n, D)), _sds((D, D)), _sds((1, D)), _sds((8, D))],
        compiler_params=_cparams(("arbitrary",), VMEM_BIG),
    )(dxn, o, gate_t, gate_t, g_post, *ys, w_out)


def inproj_bwd_x(dp, w_in, x, scale_t, g_pre, dxn, n_batch, sb, xchg=None):
    n = x.shape[0]

    def sel(i):
        return jnp.where(i % sb == 0, n_batch, i // sb)

    def body(dp_ref, w_ref, x_ref, sc0, sc1, g_ref, dxn_ref, dx_ref, dg_ref, dsh_ref, dsc_ref):
        i = pl.program_id(0)

        @pl.when(i == 0)
        def _():
            dg_ref[...] = jnp.zeros_like(dg_ref)
            dsh_ref[...] = jnp.zeros_like(dsh_ref)
            dsc_ref[...] = jnp.zeros_like(dsc_ref)

        dh_all = _dot_nt(dp_ref[...], w_ref[...])
        g = g_ref[...]
        for k, sc_ref in enumerate((sc0, sc1)):
            rows = slice(k * TM, (k + 1) * TM)
            dh = dh_all[rows]
            xv = x_ref[rows, :]
            rstd = lax.rsqrt(jnp.mean(xv * xv, axis=-1, keepdims=True) + EPS)
            r = xv * rstd
            hot = _row_onehot(sel(2 * i + k))
            dsh_ref[...] += jnp.where(hot, jnp.sum(dh, axis=0, keepdims=True), 0.0)
            dsc_ref[...] += jnp.where(hot, jnp.sum(dh * (r * g), axis=0, keepdims=True), 0.0)
            t = dh * (1.0 + sc_ref[0])
            dg_ref[...] += jnp.sum(t * r, axis=0, keepdims=True)
            dr = t * g
            dx_ref[rows, :] = dxn_ref[rows, :] + rstd * (dr - r * jnp.mean(dr * r, axis=-1, keepdims=True))

    def mrow(k):
        return pl.BlockSpec((1, 1, D), lambda i: (sel(2 * i + k), 0, 0))

    row = pl.BlockSpec((TP, D), lambda i: (i, 0))
    fused, x_in, x_out, x_shape, x_scratch = _with_exchange(body, 7, 4, 0, xchg, n // TP)
    return pl.pallas_call(
        fused, name="inproj_bwd_x" + ("" if xchg is None else "_" + xchg[0]), grid=(n // TP,),
        in_specs=[pl.BlockSpec((TP, PW), lambda i: (i, 0)), _full((D, PW)), row, mrow(0), mrow(1), _full((1, D)), row]
                 + x_in,
        out_specs=[row, _full((1, D)), _full((8, D)), _full((8, D))] + x_out,
        out_shape=[_sds((n, D)), _sds((1, D)), _sds((8, D)), _sds((8, D))] + x_shape,
        scratch_shapes=x_scratch,
        compiler_params=_cparams(("arbitrary",), VMEM_BIG),
    )(dp, w_in, x, scale_t, scale_t, g_pre, dxn, *([] if xchg is None else xchg[1]))


def dw_in(h, dp):
    n = h.shape[0]
    tk, tn = (1536 if n % 1536 == 0 else 512), 1024
    nk = n // tk

    def body(h_ref, dp_ref, o_ref, acc):
        k = pl.program_id(1)

        @pl.when(k == 0)
        def _():
            acc[...] = jnp.zeros_like(acc)
        acc[...] += _dot_tn(dp_ref[...], h_ref[...])

        @pl.when(k == nk - 1)
        def _():
            o_ref[...] = acc[...].astype(bf16)

    return pl.pallas_call(
        body, name="dw_in", grid=(PW // tn, nk),
        in_specs=[pl.BlockSpec((tk, D), lambda j, k: (k, 0)), pl.BlockSpec((tk, tn), lambda j, k: (k, j))],
        out_specs=pl.BlockSpec((tn, D), lambda j, k: (j, 0)),
        out_shape=_sds((PW, D), bf16),
        scratch_shapes=[pltpu.VMEM((tn, D), f32)],
        compiler_params=_cparams(("parallel", "arbitrary"), VMEM_BIG),
    )(h, dp)


def place_weights(slabs):
    n_ch, d, wc = slabs.shape

    def body(w_ref, o_ref):
        acc = jnp.pad(w_ref[0].astype(f32), ((0, 0), (0, PW - wc)))
        for k in range(1, n_ch):
            acc = acc + pltpu.roll(jnp.pad(w_ref[k].astype(f32), ((0, 0), (0, PW - wc))), wc * k, axis=1)
        o_ref[...] = acc.astype(bf16)

    return pl.pallas_call(
        body, name="place_weights", grid=(d // TM,),
        in_specs=[pl.BlockSpec((n_ch, TM, wc), lambda i: (0, i, 0))],
        out_specs=pl.BlockSpec((TM, PW), lambda i: (i, 0)),
        out_shape=_sds((d, PW), bf16),
        compiler_params=_cparams(("arbitrary",), VMEM_BIG),
    )(slabs)


def loss_head(xf, target, t_ctx):
    nb, s, _ = xf.shape
    jc = t_ctx // TM

    def body(x_ref, t_ref, dx_ref, l_ref):
        b, j = pl.program_id(0), pl.program_id(1)

        @pl.when((b == 0) & (j == 0))
        def _():
            l_ref[...] = jnp.zeros_like(l_ref)

        @pl.when(j < jc)
        def _():
            dx_ref[...] = jnp.zeros_like(dx_ref)

        @pl.when(j >= jc)
        def _():
            diff = x_ref[0] - t_ref[0]
            dx_ref[0] = diff * (1.0 / D)
            l_ref[...] += 0.5 * jnp.sum(diff * diff) * (1.0 / D)

    return pl.pallas_call(
        body, name="loss_head", grid=(nb, s // TM),
        in_specs=[pl.BlockSpec((1, TM, D), lambda b, j: (b, j, 0)),
                  pl.BlockSpec((1, TM, D), lambda b, j: (b, jnp.maximum(j - jc, 0), 0))],
        out_specs=[pl.BlockSpec((1, TM, D), lambda b, j: (b, j, 0)), _full((1, 128))],
        out_shape=[_sds((nb, s, D)), _sds((1, 128))],
        compiler_params=_cparams(("arbitrary", "arbitrary")),
    )(xf, target)


def _chunk_maps(n_ctx, n_lat):
    n = n_ctx + n_lat

    def cf(t):
        return t

    def cb(t):
        return jnp.where(t < n_ctx, n_ctx - 1 - t, n - 1 - t + n_ctx)
    return n, cf, cb


def ret_scan_fwd(p3, cos, sins, consts, t_ctx):
    nb, s, _ = p3.shape
    n, cf, cb = _chunk_maps(t_ctx // RC, (s - t_ctx) // RC)
    nz = 4 * nb

    def body(qf, kf, vf, qb, kb, vb, cosf, sinf, cosb, sinb, intra_r, qdec_r, kdec_r, cd_r, bd_r, bdr_r,
             of_ref, ob_ref, sall_ref, s_sc):
        @pl.when(pl.program_id(0) == 0)
        def _():
            s_sc[...] = jnp.zeros_like(s_sc)
        st = s_sc[...]
        sall_ref[0] = st
        s_new, o = _ret_step(st, _pairs(qf, qb, nb), _pairs(kf, kb, nb), _pairs(vf, vb, nb),
                             _pair_tables(cosf, cosb, nb), _pair_tables(sinf, sinb, nb), intra_r[...], qdec_r[...],
                             kdec_r[...], cd_r[...], bd_r[...], bdr_r[...])
        s_sc[...] = s_new
        _unpairs(o, of_ref, ob_ref, nb)

    def pspec(m, seg):
        return pl.BlockSpec((nb, RC, BW), lambda t: (0, m(t), seg))

    def tspec(m):
        return pl.BlockSpec((RC, PAIR_W), lambda t: (m(t), 0))

    return pl.pallas_call(
        body, name="ret_scan_fwd", grid=(n,),
        in_specs=[pspec(cf, 0), pspec(cf, 1), pspec(cf, 2), pspec(cb, 0), pspec(cb, 1), pspec(cb, 2),
                  tspec(cf), tspec(cf), tspec(cb), tspec(cb)] + [_full(c.shape) for c in consts],
        out_specs=[pl.BlockSpec((nb, RC, BW), lambda t: (0, cf(t), 0)),
                   pl.BlockSpec((nb, RC, BW), lambda t: (0, cb(t), 0)),
                   pl.BlockSpec((1, nz, PAIR_W, PAIR_W), lambda t: (t, 0, 0, 0))],
        out_shape=[_sds((nb, s, BW)), _sds((nb, s, BW)), _sds((n, nz, PAIR_W, PAIR_W))],
        scratch_shapes=[pltpu.VMEM((nz, PAIR_W, PAIR_W), f32)],
        compiler_params=_cparams(("arbitrary",)),
    )(p3, p3, p3, p3, p3, p3, cos, sins, cos, sins, *consts)


def ret_scan_bwd(p3, cos, sins, consts, s_all, do, t_ctx):
    nb, s, _ = p3.shape
    n, cf, cb = _chunk_maps(t_ctx // RC, (s - t_ctx) // RC)
    nz = 4 * nb

    def rf(t):
        return cf(n - 1 - t)

    def rb(t):
        return cb(n - 1 - t)

    def body(qf, kf, vf, qb, kb, vb, cosf, sinf, cosb, sinb, intra_r, qdec_r, kdec_r, cd_r, bd_r, bdr_r,
             sall_ref, dof, dob, dqf, dkf, dvf, dqb, dkb, dvb, ds_sc):
        @pl.when(pl.program_id(0) == 0)
        def _():
            ds_sc[...] = jnp.zeros_like(ds_sc)
        step = functools.partial(_ret_step, cos=_pair_tables(cosf, cosb, nb), sins=_pair_tables(sinf, sinb, nb),
                                 intra=intra_r[...], qdec=qdec_r[...], kdec=kdec_r[...], cd=cd_r[...], bd2=bd_r[...],
                                 bdr=bdr_r[...])
        _, vjp = jax.vjp(step, sall_ref[0], _pairs(qf, qb, nb), _pairs(kf, kb, nb), _pairs(vf, vb, nb))
        ds, dq, dk, dv = vjp((ds_sc[...], _pairs(dof, dob, nb)))
        ds_sc[...] = ds
        _unpairs(dq, dqf, dqb, nb)
        _unpairs(dk, dkf, dkb, nb)
        _unpairs(dv, dvf, dvb, nb)

    def pspec(m, seg):
        return pl.BlockSpec((nb, RC, BW), lambda t: (0, m(t), seg))

    def tspec(m):
        return pl.BlockSpec((RC, PAIR_W), lambda t: (m(t), 0))

    def ospec(m):
        return pl.BlockSpec((nb, RC, BW), lambda t: (0, m(t), 0))

    return pl.pallas_call(
        body, name="ret_scan_bwd", grid=(n,),
        in_specs=[pspec(rf, 0), pspec(rf, 1), pspec(rf, 2), pspec(rb, 0), pspec(rb, 1), pspec(rb, 2),
                  tspec(rf), tspec(rf), tspec(rb), tspec(rb)] + [_full(c.shape) for c in consts]
                 + [pl.BlockSpec((1, nz, PAIR_W, PAIR_W), lambda t: (n - 1 - t, 0, 0, 0)), ospec(rf), ospec(rb)],
        out_specs=[ospec(rf), ospec(rf), ospec(rf), ospec(rb), ospec(rb), ospec(rb)],
        out_shape=[_sds((nb, s, BW), bf16)] * 6,
        scratch_shapes=[pltpu.VMEM((nz, PAIR_W, PAIR_W), f32)],
        compiler_params=_cparams(("arbitrary",), VMEM_BIG),
    )(p3, p3, p3, p3, p3, p3, cos, sins, cos, sins, *consts, s_all, do, do)


def mix_finish_fwd(fn, name, o_f, o_b, p3, zseg, norm_g, bd):
    nb, s, _ = p3.shape

    def body(of_ref, ob_ref, z_ref, g_ref, bd_ref, y_ref):
        y_ref[0] = fn(of_ref[0], ob_ref[0], z_ref[0], g_ref[...], bd_ref[...]).astype(bf16)

    blk = pl.BlockSpec((1, TM, BW), lambda b, j: (b, j, 0))
    return pl.pallas_call(
        body, name=name, grid=(nb, s // TM),
        in_specs=[blk, blk, pl.BlockSpec((1, TM, BW), lambda b, j: (b, j, zseg)), _full((1, BW)), _full((BW, BW))],
        out_specs=blk, out_shape=_sds((nb, s, BW), bf16),
        compiler_params=_cparams(("arbitrary", "arbitrary")),
    )(o_f, o_b, p3, norm_g, bd)


def mix_finish_bwd(fn, name, o_f, o_b, p3, zseg, norm_g, bd, dy3, yseg):
    nb, s, _ = p3.shape

    def body(of_ref, ob_ref, z_ref, g_ref, bd_ref, dy_ref, do_ref, dz_ref, dg_ref):
        @pl.when((pl.program_id(0) == 0) & (pl.program_id(1) == 0))
        def _():
            dg_ref[...] = jnp.zeros_like(dg_ref)
        bdv = bd_ref[...]
        _, vjp = jax.vjp(lambda a, b, z, g: fn(a, b, z, g, bdv), of_ref[0], ob_ref[0], z_ref[0], g_ref[...])
        do, _, dz, dg = vjp(dy_ref[0])
        do_ref[0] = do
        dz_ref[0] = dz.astype(bf16)
        dg_ref[...] += dg

    blk = pl.BlockSpec((1, TM, BW), lambda b, j: (b, j, 0))
    return pl.pallas_call(
        body, name=name, grid=(nb, s // TM),
        in_specs=[blk, blk, pl.BlockSpec((1, TM, BW), lambda b, j: (b, j, zseg)), _full((1, BW)), _full((BW, BW)),
                  pl.BlockSpec((1, TM, BW), lambda b, j: (b, j, yseg))],
        out_specs=[blk, blk, _full((1, BW))],
        out_shape=[_sds((nb, s, BW)), _sds((nb, s, BW), bf16), _sds((1, BW))],
        compiler_params=_cparams(("arbitrary", "arbitrary")),
    )(o_f, o_b, p3, norm_g, bd, dy3)


GDN_QKV = 11 * BW // 128
N_QKV = 3 * BW // 128


def gdn_conv_fwd(p3, w, t_ctx):
    nb, s, _ = p3.shape
    sd, su = _make_shifts(t_ctx, s)

    def body(x_ref, w_ref, o_ref):
        o_ref[0] = _silu(_conv3(x_ref[0], w_ref[...], sd, su))

    return pl.pallas_call(
        body, name="gdn_conv_fwd", grid=(nb, N_QKV),
        in_specs=[pl.BlockSpec((1, s, 128), lambda b, j: (b, 0, GDN_QKV + j)), pl.BlockSpec((3, 128), lambda b, j: (0, j))],
        out_specs=pl.BlockSpec((1, s, 128), lambda b, j: (b, 0, j)),
        out_shape=_sds((nb, s, 3 * BW)),
        compiler_params=_cparams(("arbitrary", "arbitrary")),
    )(p3, w)


def gdn_conv_bwd(p3, w, d_f, d_b, t_ctx):
    nb, s, _ = p3.shape
    sd, su = _make_shifts(t_ctx, s)

    def body(x_ref, w_ref, df_ref, db_ref, dx_ref, dw_ref):
        @pl.when(pl.program_id(1) == 0)
        def _():
            dw_ref[...] = jnp.zeros_like(dw_ref)
        _, vjp = jax.vjp(lambda x, w_: _silu(_conv3(x, w_, sd, su)), x_ref[0], w_ref[...])
        dx, dw = vjp(df_ref[0] + db_ref[0])
        dx_ref[0] = dx.astype(bf16)
        dw_ref[...] += dw

    blk = pl.BlockSpec((1, s, 128), lambda j, b: (b, 0, j))
    return pl.pallas_call(
        body, name="gdn_conv_bwd", grid=(N_QKV, nb),
        in_specs=[pl.BlockSpec((1, s, 128), lambda j, b: (b, 0, GDN_QKV + j)), pl.BlockSpec((3, 128), lambda j, b: (0, j)),
                  blk, blk],
        out_specs=[blk, pl.BlockSpec((3, 128), lambda j, b: (0, j))],
        out_shape=[_sds((nb, s, 3 * BW), bf16), _sds((3, 3 * BW))],
        compiler_params=_cparams(("arbitrary", "arbitrary"), VMEM_BIG),
    )(p3, w, d_f, d_b)


def _pairs(f_ref, b_ref, nb):
    return jnp.stack([r[b, :, PAIR_W * p:PAIR_W * (p + 1)] for b in range(nb) for r in (f_ref, b_ref) for p in range(2)])


def _pair_tables(f_ref, b_ref, nb):
    return jnp.stack([r[...] for _ in range(nb) for r in (f_ref, b_ref) for _ in range(2)])


def _gates(f_ref, b_ref, nb):
    return jnp.stack([r[b] for b in range(nb) for r in (f_ref, b_ref)])


def _unpairs(a, f_ref, b_ref, nb, lane0=0):
    for b in range(nb):
        for d, r in enumerate((f_ref, b_ref)):
            for p in range(2):
                r[b, :, lane0 + PAIR_W * p:lane0 + PAIR_W * (p + 1)] = a[4 * b + 2 * d + p].astype(r.dtype)


def _with_exchange(body, n_in, n_out, n_scratch, xchg, n_steps):
    if xchg is None:
        return body, [], [], [], []
    kind, arrs = xchg
    nx = len(arrs)

    def fused(*refs):
        ins, rest = refs[:n_in], refs[n_in:]
        srcs, rest = rest[:nx], rest[nx:]
        outs, rest = rest[:n_out], rest[n_out:]
        dsts, rest = rest[:nx], rest[nx:]
        scratch, sems = rest[:n_scratch], rest[n_scratch:]
        start, wait = _peer_exchange(kind, "chips", srcs, dsts, *sems)
        pl.when(pl.program_id(0) == 0)(start)
        body(*ins, *outs, *scratch)
        pl.when(pl.program_id(0) == n_steps - 1)(wait)

    any_ = pl.BlockSpec(memory_space=pl.ANY)
    return fused, [any_] * nx, [any_] * nx, _exchange_shapes(kind, "chips", arrs), _exchange_scratch("chips", nx)


def gdn_scan_fwd(c3, p3, alog, dtb, consts, t_ctx, xchg=None):
    nb, s, _ = p3.shape
    n, cf, cb = _chunk_maps(t_ctx // GC, (s - t_ctx) // GC)
    gblk = GATE_COL // 128

    nz = 4 * nb

    def body(qf, kf, vf, gf, qb, kb, vb, gb, al_ref, dt_ref, tm_r, tm2_r, st2_r, eg_r, eb_r, egt_r, dsel_r, eye_r, bd_r,
             of_ref, ob_ref, sall_ref, inv_ref, s_sc):
        @pl.when(pl.program_id(0) == 0)
        def _():
            s_sc[...] = jnp.zeros_like(s_sc)
        st = s_sc[...]
        sall_ref[0] = st
        s_new, o, inv = _gdn_step(st, _pairs(qf, qb, nb), _pairs(kf, kb, nb), _pairs(vf, vb, nb), _gates(gf, gb, nb),
                                  al_ref[...], dt_ref[...], tm_r[...], tm2_r[...], st2_r[...], eg_r[...], eb_r[...],
                                  egt_r[...], dsel_r[...], eye_r[...], bd_r[...])
        s_sc[...] = s_new
        inv_ref[0] = inv
        _unpairs(o, of_ref, ob_ref, nb)

    def cspec(m, col=0):
        return pl.BlockSpec((nb, GC, BW), lambda t: (0, m(t), col))

    def gspec(m):
        return pl.BlockSpec((nb, GC, 128), lambda t: (0, m(t), gblk))

    fused, x_in, x_out, x_shape, x_scratch = _with_exchange(body, 10 + len(consts), 4, 1, xchg, n)
    return pl.pallas_call(
        fused, name="gdn_scan_fwd" + ("" if xchg is None else "_" + xchg[0]), grid=(n,),
        in_specs=[cspec(cf, 0), cspec(cf, 1), cspec(cf, 2), gspec(cf), cspec(cb, 0), cspec(cb, 1), cspec(cb, 2), gspec(cb),
                  _full((1, 128)), _full((1, 128))] + [_full(c.shape) for c in consts] + x_in,
        out_specs=[cspec(cf), cspec(cb), pl.BlockSpec((1, nz, PAIR_W, PAIR_W), lambda t: (t, 0, 0, 0)),
                   pl.BlockSpec((1, nz, GC, PAIR_W), lambda t: (t, 0, 0, 0))] + x_out,
        out_shape=[_sds((nb, s, BW)), _sds((nb, s, BW)), _sds((n, nz, PAIR_W, PAIR_W)), _sds((n, nz, GC, PAIR_W))]
                  + x_shape,
        scratch_shapes=[pltpu.VMEM((nz, PAIR_W, PAIR_W), f32)] + x_scratch,
        compiler_params=_cparams(("arbitrary",)),
    )(c3, c3, c3, p3, c3, c3, c3, p3, alog, dtb, *consts, *([] if xchg is None else xchg[1]))


def gdn_scan_bwd(c3, p3, alog, dtb, consts, s_all, inv_all, do, t_ctx, xchg=None):
    nb, s, _ = p3.shape
    n, cf, cb = _chunk_maps(t_ctx // GC, (s - t_ctx) // GC)
    gblk = GATE_COL // 128

    def rf(t):
        return cf(n - 1 - t)

    def rb(t):
        return cb(n - 1 - t)

    nz = 4 * nb

    def body(qf, kf, vf, gf, qb, kb, vb, gb, al_ref, dt_ref, tm_r, tm2_r, st2_r, eg_r, eb_r, egt_r, dsel_r, eye_r, bd_r,
             sall_ref, inv_ref, dof, dob, dcf, dgf, dcb, dgb, dal_ref, ddt_ref, ds_sc):
        @pl.when(pl.program_id(0) == 0)
        def _():
            dal_ref[...] = jnp.zeros_like(dal_ref)
            ddt_ref[...] = jnp.zeros_like(ddt_ref)
            ds_sc[...] = jnp.zeros_like(ds_sc)
        consts = dict(tmask=tm_r[...], tmask2=tm2_r[...], strict2=st2_r[...], exp_g=eg_r[...], exp_b=eb_r[...],
                      exp_gt=egt_r[...], dsel2=dsel_r[...], eye2=eye_r[...], bd2=bd_r[...], inv=inv_ref[0])

        def step(*a):
            return _gdn_step(*a, **consts)[:2]

        _, vjp = jax.vjp(step, sall_ref[0], _pairs(qf, qb, nb), _pairs(kf, kb, nb), _pairs(vf, vb, nb),
                         _gates(gf, gb, nb), al_ref[...], dt_ref[...])
        ds, dq, dk, dv, dg, dal, ddt = vjp((ds_sc[...], _pairs(dof, dob, nb)))
        ds_sc[...] = ds
        for i, a in enumerate((dq, dk, dv)):
            _unpairs(a, dcf, dcb, nb, BW * i)
        for b in range(nb):
            dgf[b] = dg[2 * b].astype(bf16)
            dgb[b] = dg[2 * b + 1].astype(bf16)
        dal_ref[...] += dal
        ddt_ref[...] += ddt

    def cspec(m, col=0):
        return pl.BlockSpec((nb, GC, BW), lambda t: (0, m(t), col))

    def gspec(m):
        return pl.BlockSpec((nb, GC, 128), lambda t: (0, m(t), gblk))

    def dcout(m):
        return pl.BlockSpec((nb, GC, 3 * BW), lambda t: (0, m(t), 0))

    def gout(m):
        return pl.BlockSpec((nb, GC, 128), lambda t: (0, m(t), 0))

    fused, x_in, x_out, x_shape, x_scratch = _with_exchange(body, 14 + len(consts), 6, 1, xchg, n)
    return pl.pallas_call(
        fused, name="gdn_scan_bwd" + ("" if xchg is None else "_" + xchg[0]), grid=(n,),
        in_specs=[cspec(rf, 0), cspec(rf, 1), cspec(rf, 2), gspec(rf), cspec(rb, 0), cspec(rb, 1), cspec(rb, 2), gspec(rb),
                  _full((1, 128)), _full((1, 128))] + [_full(c.shape) for c in consts]
                 + [pl.BlockSpec((1, nz, PAIR_W, PAIR_W), lambda t: (n - 1 - t, 0, 0, 0)),
                    pl.BlockSpec((1, nz, GC, PAIR_W), lambda t: (n - 1 - t, 0, 0, 0)), cspec(rf), cspec(rb)] + x_in,
        out_specs=[dcout(rf), gout(rf), dcout(rb), gout(rb), _full((1, 128)), _full((1, 128))] + x_out,
        out_shape=[_sds((nb, s, 3 * BW)), _sds((nb, s, 128), bf16), _sds((nb, s, 3 * BW)), _sds((nb, s, 128), bf16),
                   _sds((1, 128)), _sds((1, 128))] + x_shape,
        scratch_shapes=[pltpu.VMEM((nz, PAIR_W, PAIR_W), f32)] + x_scratch,
        compiler_params=_cparams(("arbitrary",), VMEM_BIG),
    )(c3, c3, c3, p3, c3, c3, c3, p3, alog, dtb, *consts, s_all, inv_all, do, do, *([] if xchg is None else xchg[1]))


def _sg_consts():
    hmp = np.zeros((2, NH, PAIR_W))
    for h in range(NH):
        hmp[h // 2, h, (h % 2) * HD:(h % 2 + 1) * HD] = 1.0
    bdr = (np.arange(2 * RC)[:, None] // RC == np.arange(PAIR_W)[None, :] // HD)
    return jnp.asarray(hmp, f32), jnp.asarray(bdr, f32)


def _sg_rows(s):
    return 6 * RC if s % (6 * RC) == 0 else 2 * RC


def _halves(ref):
    return ref[0, :, :PAIR_W], ref[0, :, PAIR_W:]


def sg_fwd(p3, w, b, hmp, bdr):
    nb, s, _ = p3.shape
    ts = _sg_rows(s)

    def body(u_ref, v_ref, z_ref, w_ref, b_ref, hm_ref, bdr_ref, y_ref):
        y0, y1 = _sg_block(*_halves(u_ref), *_halves(v_ref), *_halves(z_ref), w_ref[...], b_ref[...], hm_ref[...],
                           bdr_ref[...])
        y_ref[0, :, :PAIR_W] = y0.astype(bf16)
        y_ref[0, :, PAIR_W:] = y1.astype(bf16)

    def seg(k):
        return pl.BlockSpec((1, ts, BW), lambda bi, i: (bi, i, k))

    return pl.pallas_call(
        body, name="sg_fwd", grid=(nb, s // ts),
        in_specs=[seg(4), seg(5), seg(6), _full((NH, RC, RC)), _full((NH, RC)), _full(hmp.shape), _full(bdr.shape)],
        out_specs=pl.BlockSpec((1, ts, BW), lambda bi, i: (bi, i, 0)),
        out_shape=_sds((nb, s, BW), bf16),
        compiler_params=_cparams(("arbitrary", "arbitrary")),
    )(p3, p3, p3, w, b, hmp, bdr)


def sg_bwd(p3, w, b, hmp, bdr, dy3):
    nb, s, _ = p3.shape
    ts = _sg_rows(s)

    def body(u_ref, v_ref, z_ref, w_ref, b_ref, hm_ref, bdr_ref, dy_ref, du_ref, dv_ref, dz_ref, dw_ref, db_ref):
        @pl.when((pl.program_id(0) == 0) & (pl.program_id(1) == 0))
        def _():
            dw_ref[...] = jnp.zeros_like(dw_ref)
            db_ref[...] = jnp.zeros_like(db_ref)
        hm, bdr_v = hm_ref[...], bdr_ref[...]
        _, vjp = jax.vjp(lambda *a: _sg_block(*a, hm, bdr_v), *_halves(u_ref), *_halves(v_ref), *_halves(z_ref),
                         w_ref[...], b_ref[...])
        du0, du1, dv0, dv1, dz0, dz1, dw, db = vjp(_halves(dy_ref))
        for ref, a0, a1 in ((du_ref, du0, du1), (dv_ref, dv0, dv1), (dz_ref, dz0, dz1)):
            ref[0, :, :PAIR_W] = a0.astype(bf16)
            ref[0, :, PAIR_W:] = a1.astype(bf16)
        dw_ref[...] += dw
        db_ref[...] += db

    def seg(k):
        return pl.BlockSpec((1, ts, BW), lambda bi, i: (bi, i, k))

    blk = pl.BlockSpec((1, ts, BW), lambda bi, i: (bi, i, 0))
    return pl.pallas_call(
        body, name="sg_bwd", grid=(nb, s // ts),
        in_specs=[seg(4), seg(5), seg(6), _full((NH, RC, RC)), _full((NH, RC)), _full(hmp.shape), _full(bdr.shape),
                  seg(1)],
        out_specs=[blk, blk, blk, _full((NH, RC, RC)), _full((NH, RC))],
        out_shape=[_sds((nb, s, BW), bf16)] * 3 + [_sds((NH, RC, RC)), _sds((NH, RC))],
        compiler_params=_cparams(("arbitrary", "arbitrary"), VMEM_BIG),
    )(p3, p3, p3, w, b, hmp, bdr, dy3)


def _sc_fn(b, c, h, z, w, sd, su):
    return b * _conv3(c * h, w, sd, su) * _silu(z)


def sc_fwd(p3, w, t_ctx):
    nb, s, _ = p3.shape
    sd, su = _make_shifts(t_ctx, s)

    def body(b_ref, c_ref, h_ref, z_ref, w_ref, y_ref):
        y_ref[0] = _sc_fn(b_ref[0], c_ref[0], h_ref[0], z_ref[0], w_ref[...], sd, su).astype(bf16)

    def seg(k):
        return pl.BlockSpec((1, s, 128), lambda bi, j: (bi, 0, 2 * k + j))

    return pl.pallas_call(
        body, name="sc_fwd", grid=(nb, 2),
        in_specs=[seg(7), seg(8), seg(9), seg(10), pl.BlockSpec((3, 128), lambda bi, j: (0, j))],
        out_specs=pl.BlockSpec((1, s, 128), lambda bi, j: (bi, 0, j)),
        out_shape=_sds((nb, s, BW), bf16),
        compiler_params=_cparams(("arbitrary", "arbitrary"), VMEM_BIG),
    )(p3, p3, p3, p3, w)


def sc_bwd(p3, w, dy3, t_ctx):
    nb, s, _ = p3.shape
    sd, su = _make_shifts(t_ctx, s)

    def body(b_ref, c_ref, h_ref, z_ref, w_ref, dy_ref, db_ref, dc_ref, dh_ref, dz_ref, dw_ref):
        @pl.when(pl.program_id(1) == 0)
        def _():
            dw_ref[...] = jnp.zeros_like(dw_ref)
        _, vjp = jax.vjp(lambda b, c, h, z, w_: _sc_fn(b, c, h, z, w_, sd, su),
                         b_ref[0], c_ref[0], h_ref[0], z_ref[0], w_ref[...])
        db, dc, dh, dz, dw = vjp(dy_ref[0])
        db_ref[0] = db.astype(bf16)
        dc_ref[0] = dc.astype(bf16)
        dh_ref[0] = dh.astype(bf16)
        dz_ref[0] = dz.astype(bf16)
        dw_ref[...] += dw

    def seg(k):
        return pl.BlockSpec((1, s, 128), lambda j, bi: (bi, 0, 2 * k + j))

    blk = pl.BlockSpec((1, s, 128), lambda j, bi: (bi, 0, j))
    wspec = pl.BlockSpec((3, 128), lambda j, bi: (0, j))
    return pl.pallas_call(
        body, name="sc_bwd", grid=(2, nb),
        in_specs=[seg(7), seg(8), seg(9), seg(10), wspec, seg(2)],
        out_specs=[blk, blk, blk, blk, wspec],
        out_shape=[_sds((nb, s, BW), bf16)] * 4 + [_sds((3, BW))],
        compiler_params=_cparams(("arbitrary", "arbitrary"), VMEM_BIG),
    )(p3, p3, p3, p3, w, dy3)


def assemble_dp(pairs, singles_a, gdn_x, singles_b, gates):
    nb, s, _ = singles_a[0].shape
    flat = [a for pr in pairs for a in pr] + list(singles_a) + list(gdn_x) + list(singles_b) + list(gates)
    n_pairs, n_a, n_x, n_b = len(pairs), len(singles_a), len(gdn_x), len(singles_b)

    def body(*refs):
        out = refs[-1]
        ins = refs[:-1]
        col = 0
        for p in range(n_pairs):
            out[0, :, col:col + BW] = (ins[2 * p][0].astype(f32) + ins[2 * p + 1][0].astype(f32)).astype(bf16)
            col += BW
        k = 2 * n_pairs
        for _ in range(n_a + n_x + n_b):
            wk = ins[k].shape[-1]
            out[0, :, col:col + wk] = ins[k][0]
            col += wk
            k += 1
        out[0, :, col:col + 128] = (ins[k][0].astype(f32) + ins[k + 1][0].astype(f32)).astype(bf16)
        out[0, :, col + 128:] = jnp.zeros((TM, PW - col - 128), bf16)

    def spec(a):
        return pl.BlockSpec((1, TM, a.shape[-1]), lambda b, j: (b, j, 0))

    return pl.pallas_call(
        body, name="assemble_dp", grid=(nb, s // TM),
        in_specs=[spec(a) for a in flat],
        out_specs=pl.BlockSpec((1, TM, PW), lambda b, j: (b, j, 0)),
        out_shape=_sds((nb, s, PW), bf16),
        compiler_params=_cparams(("arbitrary", "arbitrary")),
    )(*flat)


def mod_fwd(c_rows, w_mod, b_cols):
    nl, _, wc = w_mod.shape
    nr = c_rows.shape[0]

    def body(c_ref, w_ref, b_ref, o_ref):
        o_ref[0] = _dot(_silu(c_ref[...]), w_ref[0], precision=HI) + b_ref[0]

    return pl.pallas_call(
        body, name="mod_fwd", grid=(nl,),
        in_specs=[_full((nr, D)), pl.BlockSpec((1, D, wc), lambda l: (l, 0, 0)), pl.BlockSpec((1, 1, wc), lambda l: (l, 0, 0))],
        out_specs=pl.BlockSpec((1, nr, wc), lambda l: (l, 0, 0)),
        out_shape=_sds((nl, nr, wc)),
        compiler_params=_cparams(("arbitrary",)),
    )(c_rows, w_mod, b_cols)


def mod_bwd(c_rows, w_mod, dm_cols, dm_full):
    nl, _, wc = w_mod.shape
    nr = c_rows.shape[0]

    def body(c_ref, w_ref, dmc_ref, dmf_ref, gw_ref, gb_ref, dcc_ref):
        @pl.when(pl.program_id(0) == 0)
        def _():
            dcc_ref[...] = jnp.zeros_like(dcc_ref)
        a = _silu(c_ref[...])
        dmc = dmc_ref[0]
        gw_ref[0] = _dot_tn(a, dmc, precision=HI)
        gb_ref[0] = jnp.sum(dmf_ref[0], axis=0, keepdims=True)
        dcc_ref[...] += _dot_nt(dmc[nr - 8:nr], w_ref[0], precision=HI)

    return pl.pallas_call(
        body, name="mod_bwd", grid=(nl,),
        in_specs=[_full((nr, D)), pl.BlockSpec((1, D, wc), lambda l: (l, 0, 0)),
                  pl.BlockSpec((1, nr, wc), lambda l: (l, 0, 0)), pl.BlockSpec((1, nr, 3 * D), lambda l: (l, 0, 0))],
        out_specs=[pl.BlockSpec((1, D, wc), lambda l: (l, 0, 0)), pl.BlockSpec((1, 1, 3 * D), lambda l: (l, 0, 0)),
                   _full((8, D))],
        out_shape=[_sds((nl, D, wc)), _sds((nl, 1, 3 * D)), _sds((8, D))],
        compiler_params=_cparams(("arbitrary",)),
    )(c_rows, w_mod, dm_cols, dm_full)


def cctx_grad(parts, c_ctx):
    def body(p_ref, c_ref, o_ref):
        tot = p_ref[0, 0:1, :]
        for k in (2, 4, 6):
            tot = tot + p_ref[k, 0:1, :]
        c = c_ref[...]
        sg = jax.nn.sigmoid(c)
        o_ref[...] = tot * (sg * (1.0 + c * (1.0 - sg)))

    return pl.pallas_call(body, name="cctx_grad", out_shape=_sds((1, D)))(parts, c_ctx)


def sum_lead(x, out_dtype=f32, tr=256, tc=None):
    k, r, c = x.shape
    tr = min(tr, r)
    tc = c if tc is None else tc
    assert r % tr == 0 and c % tc == 0

    def body(x_ref, o_ref):
        tot = x_ref[0].astype(f32)
        for i in range(1, k):
            tot = tot + x_ref[i].astype(f32)
        o_ref[...] = tot.astype(out_dtype)

    return pl.pallas_call(
        body, name="sum_lead", grid=(r // tr, c // tc),
        in_specs=[pl.BlockSpec((k, tr, tc), lambda i, j: (0, i, j))],
        out_specs=pl.BlockSpec((tr, tc), lambda i, j: (i, j)),
        out_shape=_sds((r, c), out_dtype),
        compiler_params=_cparams(("arbitrary", "arbitrary")),
    )(x)


def adamw(w, m, v, g1, g2=None, tr=256, block=None):
    if block is None:
        block = (1,) * (w.ndim - 2) + (min(tr, w.shape[-2]), w.shape[-1])
    assert len(block) == w.ndim and all(d % b == 0 for d, b in zip(w.shape, block))
    two = g2 is not None
    c1 = 1.0 / (1.0 - ADAM_B1 ** ADAM_STEP)
    c2 = 1.0 / (1.0 - ADAM_B2 ** ADAM_STEP)

    def body(*refs):
        w_ref, m_ref, v_ref, g_ref = refs[:4]
        g = g_ref[...]
        if two:
            g = g + refs[4][...]
        go_ref, d_ref, mo_ref, vo_ref = refs[-4:]
        mn = ADAM_B1 * m_ref[...] + (1.0 - ADAM_B1) * g
        vn = ADAM_B2 * v_ref[...] + (1.0 - ADAM_B2) * (g * g)
        go_ref[...] = g
        mo_ref[...] = mn
        vo_ref[...] = vn
        d_ref[...] = -ADAM_LR * ((mn * c1) / (jnp.sqrt(vn * c2) + ADAM_EPS) + ADAM_WD * w_ref[...])

    blk = pl.BlockSpec(block, lambda *i: i)
    grid = tuple(d // b for d, b in zip(w.shape, block))
    args = [w, m, v, g1] + ([g2] if two else [])
    return pl.pallas_call(
        body, name="adamw", grid=grid,
        in_specs=[blk] * len(args), out_specs=[blk] * 4, out_shape=[_sds(w.shape)] * 4,
        compiler_params=_cparams(("arbitrary",) * len(grid)),
    )(*args)


def _my_pos():
    return lax.axis_index("x"), lax.axis_index("y"), lax.axis_index("c")


GROUP_SIZE = {"devices": N_DEV, "chips": N_CHIPS, "cores": 2}


def _peer_exchange(kind, group, src_refs, dst_refs, send_sems, recv_sems, local_sems):
    mx, my, mc = _my_pos()
    n = GROUP_SIZE[group]
    if group == "devices":
        me = 4 * mx + 2 * my + mc
    elif group == "chips":
        me = 2 * mx + my
    else:
        me = mc

    def peer(k):
        if group == "devices":
            return (mx ^ (k >> 2), my ^ ((k >> 1) & 1), mc ^ (k & 1))
        if group == "chips":
            return (mx ^ (k >> 1), my ^ (k & 1), mc)
        return (mx, my, mc ^ k)

    def copies():
        local, sends, recvs = [], [], []
        for i, (src, dst) in enumerate(zip(src_refs, dst_refs)):
            def part(k):
                return src.at[k] if kind == "scatter" else src

            def slab(k):
                return dst if kind == "send" else dst.at[k]

            if kind != "send":
                local.append(pltpu.make_async_copy(part(me), dst.at[me], local_sems.at[i]))
            for k in range(1, n):
                sem = dict(send_sem=send_sems.at[i, k - 1], recv_sem=recv_sems.at[i, k - 1], device_id_type=MESH)
                sends.append(pltpu.make_async_remote_copy(src_ref=part(me ^ k), dst_ref=slab(me), device_id=peer(k), **sem))
                recvs.append(pltpu.make_async_remote_copy(src_ref=part(me ^ k), dst_ref=slab(me ^ k),
                                                          device_id=(mx, my, mc), **sem))
        return local, sends, recvs

    def start():
        local, sends, _ = copies()
        for cp in local + sends:
            cp.start()

    def wait():
        local, sends, recvs = copies()
        for cp in recvs:
            cp.wait_recv()
        for cp in sends:
            cp.wait_send()
        for cp in local:
            cp.wait()

    return start, wait


def _exchange_scratch(group, n):
    k = GROUP_SIZE[group] - 1
    return [pltpu.SemaphoreType.DMA((n, k)), pltpu.SemaphoreType.DMA((n, k)), pltpu.SemaphoreType.DMA((n,))]


def _exchange_shapes(kind, group, arrs):
    return [_sds(((GROUP_SIZE[group],) + a.shape) if kind == "gather" else a.shape, a.dtype) for a in arrs]


def exchange(name, parts):
    counts = [len(arrs) for _, _, arrs in parts]
    total = sum(counts)

    def body(*refs):
        srcs, dsts, sems = refs[:total], refs[total:2 * total], refs[2 * total:]
        ops, at = [], 0
        for j, (kind, group, arrs) in enumerate(parts):
            ops.append(_peer_exchange(kind, group, srcs[at:at + counts[j]], dsts[at:at + counts[j]], *sems[3 * j:3 * j + 3]))
            at += counts[j]
        for start, _ in ops:
            start()
        for _, wait in ops:
            wait()

    any_ = pl.BlockSpec(memory_space=pl.ANY)
    flat = [a for _, _, arrs in parts for a in arrs]
    outs = pl.pallas_call(
        body, name=name, out_shape=[sh for kind, group, arrs in parts for sh in _exchange_shapes(kind, group, arrs)],
        in_specs=[any_] * total, out_specs=[any_] * total,
        scratch_shapes=[sc for _, group, arrs in parts for sc in _exchange_scratch(group, len(arrs))],
    )(*flat)
    res, at = [], 0
    for cnt in counts:
        res.append(list(outs[at:at + cnt]))
        at += cnt
    return res


def gather8(x):
    return exchange("gather8", [("gather", "devices", [x])])[0][0]


PACK_ROWS = 64
SMALL = ("c_ctx", "b_mod", "g_pre", "g_post", "ret_norm_g", "sg_w", "sg_b", "sc_conv_w", "gdn_conv_w",
         "gdn_a_log", "gdn_dt_bias", "gdn_norm_g")


def _pack(arrs, width=D, mult=PACK_ROWS):
    rows = []
    for a in arrs:
        flat = a.reshape(-1)
        pad = (-flat.shape[0]) % width
        rows.append(jnp.pad(flat, (0, pad)).reshape(-1, width))
    out = jnp.concatenate(rows, axis=0)
    return jnp.pad(out, ((0, (-out.shape[0]) % mult), (0, 0)))


def _unpack(packed, shapes, width=D):
    outs, r = [], 0
    for shp in shapes:
        size = int(np.prod(shp))
        nr = -(-size // width)
        outs.append(packed[r:r + nr].reshape(-1)[:size].reshape(shp))
        r += nr
    return outs


def kernel(x, c, ctx, c_ctx, w_mod, b_mod, g_pre, g_post, w_in, w_out, ret_norm_g, sg_w, sg_b, sc_conv_w, gdn_conv_w, gdn_a_log, gdn_dt_bias, gdn_norm_g, loss_target, m_c_ctx, m_w_mod, m_b_mod, m_g_pre, m_g_post, m_w_in, m_w_out, m_ret_norm_g, m_sg_w, m_sg_b, m_sc_conv_w, m_gdn_conv_w, m_gdn_a_log, m_gdn_dt_bias, m_gdn_norm_g, v_c_ctx, v_w_mod, v_b_mod, v_g_pre, v_g_post, v_w_in, v_w_out, v_ret_norm_g, v_sg_w, v_sg_b, v_sc_conv_w, v_gdn_conv_w, v_gdn_a_log, v_gdn_dt_bias, v_gdn_norm_g):
    weights = dict(c_ctx=c_ctx, w_mod=w_mod, b_mod=b_mod, g_pre=g_pre, g_post=g_post, w_in=w_in, w_out=w_out,
                   ret_norm_g=ret_norm_g, sg_w=sg_w, sg_b=sg_b, sc_conv_w=sc_conv_w, gdn_conv_w=gdn_conv_w,
                   gdn_a_log=gdn_a_log, gdn_dt_bias=gdn_dt_bias, gdn_norm_g=gdn_norm_g)
    mom = dict(c_ctx=m_c_ctx, w_mod=m_w_mod, b_mod=m_b_mod, g_pre=m_g_pre, g_post=m_g_post, w_in=m_w_in,
               w_out=m_w_out, ret_norm_g=m_ret_norm_g, sg_w=m_sg_w, sg_b=m_sg_b, sc_conv_w=m_sc_conv_w,
               gdn_conv_w=m_gdn_conv_w, gdn_a_log=m_gdn_a_log, gdn_dt_bias=m_gdn_dt_bias, gdn_norm_g=m_gdn_norm_g)
    var = dict(c_ctx=v_c_ctx, w_mod=v_w_mod, b_mod=v_b_mod, g_pre=v_g_pre, g_post=v_g_post, w_in=v_w_in,
               w_out=v_w_out, ret_norm_g=v_ret_norm_g, sg_w=v_sg_w, sg_b=v_sg_b, sc_conv_w=v_sc_conv_w,
               gdn_conv_w=v_gdn_conv_w, gdn_a_log=v_gdn_a_log, gdn_dt_bias=v_gdn_dt_bias, gdn_norm_g=v_gdn_norm_g)

    nb, t_lat, _ = x.shape
    t_ctx = ctx.shape[1]
    s = t_ctx + t_lat
    n = nb * s
    sb = s // TM
    nl = w_in.shape[0]
    wc_in = w_in.shape[2]
    wc_mod = w_mod.shape[2]
    rows_out = w_out.shape[1]
    n_all = nb * N_DEV
    mx, my, mc = _my_pos()
    chip = 2 * mx + my
    dev = 2 * chip + mc

    sg_c = _sg_consts()
    bd = jnp.asarray(_block_diag())
    ret_c = _ret_consts(nb)
    gdn_c = _gdn_consts(nb)
    cos, sins = _rope_tables(t_lat, t_ctx)

    w_in_b, w_out_b = w_in.astype(bf16), w_out.astype(bf16)
    pre = _pack([c, sc_conv_w, gdn_conv_w], mult=8)
    (pre_all,), w0_parts = exchange("startup_gather", [("gather", "devices", [pre]), ("gather", "chips", [w_in_b[0]])])
    c_parts, scw_parts, gcw_parts = [], [], []
    for k in range(N_DEV):
        ck, sk, gk = _unpack(pre_all[k], [c.shape, sc_conv_w.shape, gdn_conv_w.shape])
        c_parts.append(ck)
        if k % 2 == 0:
            scw_parts.append(sk)
            gcw_parts.append(gk)
    c_all = jnp.concatenate(c_parts, axis=0)
    sc_w_full = jnp.concatenate(scw_parts, axis=-1)
    gdn_w_full = jnp.concatenate(gcw_parts, axis=-1)
    c_rows = jnp.concatenate([c_all, c_ctx[None, :], jnp.zeros((7, D), f32)], axis=0)

    b_cols = lax.dynamic_slice_in_dim(b_mod, chip * wc_mod, wc_mod, axis=1)[:, None, :]
    mod_part = mod_fwd(c_rows, w_mod, b_cols)
    mod_all = gather8(mod_part)
    mod = jnp.concatenate([mod_all[2 * k] for k in range(N_CHIPS)], axis=-1)
    my_rows = jnp.concatenate([lax.dynamic_slice_in_dim(mod, dev * nb, nb, axis=1), mod[:, n_all:n_all + 1]], axis=1)
    shift_t = my_rows[:, :, None, 0:D]
    scale_t = my_rows[:, :, None, D:2 * D]
    gate_t = my_rows[:, :, None, 2 * D:3 * D]

    w_in_full, w_out_full = [None] * nl, [None] * nl
    w_in_full[0] = place_weights(w0_parts[0])

    alog = jnp.pad(gdn_a_log.reshape(nl, 1, 8), ((0, 0), (0, 0), (0, 120)))
    dtb = jnp.pad(gdn_dt_bias.reshape(nl, 1, 8), ((0, 0), (0, 0), (0, 120)))
    gdn_ng = jnp.tile(gdn_norm_g, (1, NH))[:, None, :]
    ret_ng = ret_norm_g[:, None, :]

    xs = jnp.concatenate([ctx, x], axis=1).reshape(n, D)
    saved = []
    for l in range(nl):
        p, h = inproj_fwd(xs, shift_t[l], scale_t[l], g_pre[l][None, :], w_in_full[l], nb, sb)
        p3 = p.reshape(nb, s, PW)
        ro_f, ro_b, rs_all = ret_scan_fwd(p3, cos, sins, ret_c, t_ctx)
        y_ret = mix_finish_fwd(_ret_finish, "ret_finish_fwd", ro_f, ro_b, p3, 3, ret_ng[l], bd)
        y_sg = sg_fwd(p3, sg_w[l], sg_b[l], *sg_c)
        y_sc = sc_fwd(p3, sc_w_full[l], t_ctx)
        c3 = gdn_conv_fwd(p3, gdn_w_full[l], t_ctx)
        riding = [w_out_b[l]] + ([w_in_b[l + 1]] if l + 1 < nl else [])
        go_f, go_b, *gs_all = gdn_scan_fwd(c3, p3, alog[l], dtb[l], gdn_c, t_ctx, ("gather", riding))
        w_out_full[l] = gs_all[2].reshape(D, D)
        if l + 1 < nl:
            w_in_full[l + 1] = place_weights(gs_all[3])
        gs_all = gs_all[:2]
        y_gdn = mix_finish_fwd(_gdn_finish, "gdn_finish_fwd", go_f, go_b, p3, 14, gdn_ng[l], bd)
        ys = [a.reshape(n, BW) for a in (y_ret, y_sg, y_sc, y_gdn)]
        x_new, o = outproj_fwd(ys, w_out_full[l], xs, gate_t[l], g_post[l][None, :], nb, sb)
        saved.append(dict(x=xs, h=h, p3=p3, ro=(ro_f, ro_b), rs=rs_all, c=c3, go=(go_f, go_b), gs=gs_all,
                          ys=ys, o=o))
        xs = x_new

    dx3, loss_part = loss_head(xs.reshape(nb, s, D), loss_target, t_ctx)
    loss = lax.psum(loss_part[0, 0], ("x", "y", "c"))

    dxs = dx3.reshape(n, D)
    g_small = {k: [None] * nl for k in SMALL if k not in ("c_ctx", "b_mod")}
    dm_rows = [None] * nl
    slab_in = None
    got_in, got_out = [None] * nl, [None] * nl
    for l in reversed(range(nl)):
        sv = saved[l]
        p3 = sv["p3"]
        dy, gw_out, dg_post, dgate = outproj_bwd(dxs, sv["o"], gate_t[l], g_post[l][None, :], sv["ys"], w_out_full[l], nb, sb)
        dy3 = dy.reshape(nb, s, D)
        r_do, r_dz, d_rng = mix_finish_bwd(_ret_finish, "ret_finish_bwd", *sv["ro"], p3, 3, ret_ng[l], bd, dy3, 0)
        r_d = ret_scan_bwd(p3, cos, sins, ret_c, sv["rs"], r_do, t_ctx)
        s_du, s_dv, s_dz, d_sgw, d_sgb = sg_bwd(p3, sg_w[l], sg_b[l], *sg_c, dy3)
        c_db, c_dc, c_dh, c_dz, d_scw = sc_bwd(p3, sc_w_full[l], dy3, t_ctx)
        g_do, g_dz, d_gng = mix_finish_bwd(_gdn_finish, "gdn_finish_bwd", *sv["go"], p3, 14, gdn_ng[l], bd, dy3, 3)
        riding = [gw_out.reshape(N_CHIPS, rows_out, D).astype(bf16)] + ([] if slab_in is None else [slab_in])
        g_dcf, g_dgf, g_dcb, g_dgb, g_dal, g_ddt, *got = gdn_scan_bwd(
            sv["c"], p3, alog[l], dtb[l], gdn_c, *sv["gs"], g_do, t_ctx, ("scatter", riding))
        got_out[l] = got[0]
        if slab_in is not None:
            got_in[l + 1] = got[1]
        gx, d_gcw = gdn_conv_bwd(p3, gdn_w_full[l], g_dcf, g_dcb, t_ctx)
        dp3 = assemble_dp([(r_d[0], r_d[3]), (r_d[1], r_d[4]), (r_d[2], r_d[5])],
                          [r_dz, s_du, s_dv, s_dz, c_db, c_dc, c_dh, c_dz], [gx], [g_dz], [g_dgf, g_dgb])
        dp = dp3.reshape(n, PW)
        gw_in = dw_in(sv["h"], dp)
        slab_in = jnp.stack([gw_in[k * wc_in:(k + 1) * wc_in] for k in range(N_CHIPS)])
        dxs, dg_pre, dshift, dscale, *got = inproj_bwd_x(dp, w_in_full[l], sv["x"], scale_t[l], g_pre[l][None, :], dxs, nb, sb,
                                                         ("scatter", [slab_in]) if l == 0 else None)
        if l == 0:
            got_in[0] = got[0]
        g_small["g_pre"][l] = dg_pre[0]
        g_small["g_post"][l] = dg_post[0]
        g_small["ret_norm_g"][l] = d_rng[0]
        g_small["sg_w"][l] = d_sgw
        g_small["sg_b"][l] = d_sgb
        g_small["sc_conv_w"][l] = d_scw
        g_small["gdn_conv_w"][l] = d_gcw
        g_small["gdn_a_log"][l] = g_dal[0, :8].reshape(2, NH)
        g_small["gdn_dt_bias"][l] = g_ddt[0, :8].reshape(2, NH)
        g_small["gdn_norm_g"][l] = d_gng[0].reshape(NH, HD)
        dm_rows[l] = jnp.concatenate([dshift, dscale, dgate], axis=-1)[:nb + 1]
    grad_x = dxs.reshape(nb, s, D)[:, t_ctx:, :]

    g_small = {k: jnp.stack(v) for k, v in g_small.items()}
    dm_rows = jnp.stack(dm_rows)
    names2 = [k for k in SMALL if k not in ("c_ctx", "b_mod")]
    pack_sum = _pack([g_small[k] for k in names2] + [dm_rows[:, nb:]])
    pack_own = _pack([dm_rows[:, :nb]], mult=8)
    rs = -(-pack_sum.shape[0] // (8 * N_DEV)) * 8
    slabs_sum = jnp.pad(pack_sum, ((0, N_DEV * rs - pack_sum.shape[0]), (0, 0))).reshape(N_DEV, rs, D)
    ((got_small,),) = exchange("tail_scatter", [("scatter", "devices", [slabs_sum])])
    my_slab = sum_lead(got_small, tr=rs)
    gin_mine = jnp.stack([sum_lead(a, tr=wc_in, tc=256) for a in got_in], axis=1)
    gout_mine = jnp.stack([sum_lead(a) for a in got_out])
    (all2,), (gin_sib, gout_sib) = exchange("tail_gather", [
        ("gather", "devices", [jnp.concatenate([my_slab, pack_own], axis=0)]), ("send", "cores", [gin_mine, gout_mine])])
    tot2 = all2[:, :rs].reshape(N_DEV * rs, D)
    outs2 = _unpack(tot2, [g_small[k].shape for k in names2] + [(nl, 1, 3 * D)])
    grads = dict(zip(names2, outs2[:-1]))
    dm_own = jnp.stack([_unpack(all2[k, rs:], [(nl, nb, 3 * D)])[0] for k in range(N_DEV)])
    dm_own = jnp.transpose(dm_own, (1, 0, 2, 3)).reshape(nl, n_all, 3 * D)
    dm_all = jnp.concatenate([dm_own, jnp.pad(outs2[-1], ((0, 0), (0, 7), (0, 0)))], axis=1)
    grads["gdn_norm_g"] = sum_lead(jnp.transpose(grads["gdn_norm_g"], (1, 0, 2)), tr=nl)
    for k in ("sc_conv_w", "gdn_conv_w"):
        wc = weights[k].shape[2]
        grads[k] = lax.dynamic_slice_in_dim(grads[k], chip * wc, wc, axis=2)

    dm_cols = lax.dynamic_slice_in_dim(dm_all, chip * wc_mod, wc_mod, axis=2)
    g_w_mod, g_b_mod, dcc_part = mod_bwd(c_rows, w_mod, dm_cols, dm_all)
    grads["b_mod"] = g_b_mod[:, 0, :]
    grads["c_ctx"] = cctx_grad(gather8(dcc_part), c_ctx[None, :])[0]

    res = {}
    w_in_t, m_in_t, v_in_t = [jnp.transpose(a, (2, 0, 1)) for a in (w_in, m_w_in, v_w_in)]
    res["w_in"] = [jnp.transpose(a, (1, 2, 0)) for a in
                   adamw(w_in_t, m_in_t, v_in_t, gin_mine, gin_sib, block=(wc_in // 4, nl, 256))]
    res["w_out"] = adamw(w_out, m_w_out, v_w_out, gout_mine, gout_sib)
    res["w_mod"] = adamw(w_mod, m_w_mod, v_w_mod, g_w_mod)
    shapes = [weights[k].shape for k in SMALL]
    small = adamw(_pack([weights[k] for k in SMALL]), _pack([mom[k] for k in SMALL]), _pack([var[k] for k in SMALL]),
                  _pack([grads[k].reshape(weights[k].shape) for k in SMALL]), tr=PACK_ROWS)
    small = [_unpack(a, shapes) for a in small]
    for i, k in enumerate(SMALL):
        res[k] = [small[j][i] for j in range(4)]

    order = ["c_ctx", "w_mod", "b_mod", "g_pre", "g_post", "w_in", "w_out", "ret_norm_g", "sg_w", "sg_b", "sc_conv_w",
             "gdn_conv_w", "gdn_a_log", "gdn_dt_bias", "gdn_norm_g"]
    return (loss, grad_x, *[res[k][0] for k in order], *[res[k][1] for k in order], *[res[k][2] for k in order],
            *[res[k][3] for k in order])
```

```python
import functools

import jax
import jax.numpy as jnp
import numpy as np
from jax import lax
from jax.experimental import pallas as pl
from jax.experimental.pallas import tpu as pltpu

f32 = jnp.float32
bf16 = jnp.bfloat16
HI = lax.Precision.HIGHEST
P3 = lax.Precision.HIGH
MESH = pl.DeviceIdType.MESH

EPS = 1e-6
D = 1024
NH = 4
HD = 64
BW = NH * HD
PAIR_W = 2 * HD
RC = 128
GC = 64
GRID_W = 64
ROPE_BASE = 10000.0
IN_W = 15 * BW + 16
PW = 4096
GATE_COL = 15 * BW
N_CHIPS = 4
N_DEV = 8
TM = 256
TP = 2 * TM
ADAM_LR, ADAM_B1, ADAM_B2, ADAM_EPS, ADAM_WD, ADAM_STEP = 0.001, 0.9, 0.999, 1e-08, 0.01, 10
LANE_HEAD = np.arange(BW) // HD
VMEM_BIG = 56 * 1024 * 1024


def _dot(a, b, precision=None):
    return jnp.dot(a, b, precision=precision, preferred_element_type=f32)


def _dot_nt(a, b, precision=None):
    return lax.dot_general(a, b, (((1,), (1,)), ((), ())), precision=precision, preferred_element_type=f32)


def _dot_tn(a, b, precision=None):
    return lax.dot_general(a, b, (((0,), (0,)), ((), ())), precision=precision, preferred_element_type=f32)


def _sds(shape, dtype=f32):
    return jax.ShapeDtypeStruct(shape, dtype)


def _cparams(sem=None, vmem=None):
    kw = {}
    if sem is not None:
        kw["dimension_semantics"] = sem
    if vmem is not None:
        kw["vmem_limit_bytes"] = vmem
    return pltpu.CompilerParams(**kw)


def _full(shape):
    n = len(shape)
    return pl.BlockSpec(shape, lambda *_: (0,) * n)


def _block_diag():
    return (LANE_HEAD[:, None] == LANE_HEAD[None, :]).astype(np.float32)


def _tau(c, d):
    return np.arange(c) if d == 0 else c - 1 - np.arange(c)


def _ret_consts(nb):
    lg = np.log(1.0 - 2.0 ** (-5.0 - np.arange(NH)))
    intra = np.zeros((2, 2, RC, 2 * RC)); qdec = np.zeros((2, 2, RC, PAIR_W)); kdec = np.zeros((2, 2, RC, PAIR_W))
    cd = np.zeros((2, 2, PAIR_W, PAIR_W))
    for d in range(2):
        t = _tau(RC, d)
        diff = t[:, None] - t[None, :]
        for p in range(2):
            lane_lg = lg[2 * p + np.arange(PAIR_W) // HD]
            for h in range(2):
                intra[d, p, :, h * RC:(h + 1) * RC] = np.where(diff >= 0, np.exp(np.maximum(diff, 0) * lg[2 * p + h]), 0.0)
            qdec[d, p] = np.exp((t[:, None] + 1.0) * lane_lg[None, :])
            kdec[d, p] = np.exp((RC - 1.0 - t[:, None]) * lane_lg[None, :])
            cd[d, p] = np.exp(RC * lane_lg)[:, None] * np.ones((1, PAIR_W))
    per_z = [np.tile(a.reshape((4,) + a.shape[2:]), (nb, 1, 1)) for a in (intra, qdec, kdec, cd)]
    bd2 = (np.arange(PAIR_W)[:, None] // HD == np.arange(PAIR_W)[None, :] // HD)
    bdr = (np.arange(2 * RC)[:, None] // RC == np.arange(PAIR_W)[None, :] // HD)
    return [jnp.asarray(a, f32) for a in per_z + [bd2, bdr]]


def _rope_tables(t_lat, t_ctx):
    nf = HD // 4
    inv = ROPE_BASE ** (-np.arange(nf) / nf)
    pos = np.arange(t_lat)
    ang_r = (pos // GRID_W)[:, None] * inv[None, :]
    ang_c = (pos % GRID_W)[:, None] * inv[None, :]
    ang = np.concatenate([ang_r, ang_r, ang_c, ang_c], axis=1)
    sign = np.concatenate([-np.ones(nf), np.ones(nf), -np.ones(nf), np.ones(nf)])
    cos = np.tile(np.cos(ang), (1, 2)); sins = np.tile(np.sin(ang) * sign, (1, 2))
    cos = np.concatenate([np.ones((t_ctx, PAIR_W)), cos]); sins = np.concatenate([np.zeros((t_ctx, PAIR_W)), sins])
    return jnp.asarray(cos, f32), jnp.asarray(sins, f32)


def _gdn_consts(nb):
    tmask = np.zeros((2, 2, GC, GC)); tmask2 = np.zeros((2, 2, GC, PAIR_W)); strict2 = np.zeros((2, 2, GC, PAIR_W))
    exp_g = np.zeros((2, 2, 128, PAIR_W)); exp_b = np.zeros((2, 2, 128, PAIR_W))
    for d in range(2):
        t = _tau(GC, d)
        tmask[d, :] = (t[:, None] >= t[None, :])
        tmask2[d, :] = np.tile(t[:, None] >= t[None, :], (1, 2))
        strict2[d, :] = np.tile(t[:, None] > t[None, :], (1, 2))
        for h in range(NH):
            exp_g[d, h // 2, 4 * d + h, (h % 2) * HD:(h % 2 + 1) * HD] = 1.0
            exp_b[d, h // 2, 8 + 4 * d + h, (h % 2) * HD:(h % 2 + 1) * HD] = 1.0
    exp_gt = np.transpose(exp_g, (0, 1, 3, 2))
    per_z = [np.tile(a.reshape((4,) + a.shape[2:]), (nb, 1, 1)) for a in (tmask, tmask2, strict2, exp_g, exp_b, exp_gt)]
    dsel2 = np.tile(np.eye(GC), (1, 2))
    eye2 = np.tile(np.eye(GC), (1, 2))
    bd2 = (np.arange(PAIR_W)[:, None] // HD == np.arange(PAIR_W)[None, :] // HD)
    return [jnp.asarray(a, f32) for a in per_z + [dsel2, eye2, bd2]]


def _swap16(x):
    lane = lax.broadcasted_iota(jnp.int32, x.shape, x.ndim - 1)
    n = x.shape[-1]
    return jnp.where(lane % 32 < 16, pltpu.roll(x, n - 16, axis=x.ndim - 1), pltpu.roll(x, 16, axis=x.ndim - 1))


@jax.custom_vjp
def _rot(x, cos, sins):
    return x * cos + _swap16(x) * sins


def _rot_fwd(x, cos, sins):
    return _rot(x, cos, sins), (cos, sins)


def _rot_bwd(res, g):
    cos, sins = res
    return g * cos + _swap16(g * sins), jnp.zeros_like(cos), jnp.zeros_like(sins)


_rot.defvjp(_rot_fwd, _rot_bwd)


def _silu(z):
    return z * jax.nn.sigmoid(z)


def _head_sum(x, bd):
    return _sel_r(x, bd)


def _ret_step(s, q, k, v, cos, sins, intra, qdec, kdec, cd, bd2, bdr):
    def bdiag(x):
        return jnp.concatenate([x, x], axis=1) * bdr

    qr = _rot(q, cos, sins)
    kr = _rot(k, cos, sins) * (HD ** -0.5)
    sc = _bmm_nt(qr, bdiag(kr)) * intra
    o = _bmm(qr * qdec, s) + _bmm(sc, bdiag(v))
    s_new = s * cd + bd2 * _bmm_tn(kr * kdec, v)
    return s_new, o


def _ret_finish(o_f, o_b, z, norm_g, bd):
    o = o_f + o_b
    mu = _head_sum(o, bd) * (1.0 / HD)
    xc = o - mu
    var = _head_sum(xc * xc, bd) * (1.0 / HD)
    return xc * lax.rsqrt(var + EPS) * norm_g * _silu(z)


def _softplus(x):
    return jnp.maximum(x, 0.0) + jnp.log(1.0 + jnp.exp(-jnp.abs(x)))


def _bmm(a, b, precision=None):
    return lax.dot_general(a, b, (((2,), (1,)), ((0,), (0,))), precision=precision, preferred_element_type=f32)


def _bmm_nt(a, b, precision=None):
    return lax.dot_general(a, b, (((2,), (2,)), ((0,), (0,))), precision=precision, preferred_element_type=f32)


def _bmm_tn(a, b, precision=None):
    return lax.dot_general(a, b, (((1,), (1,)), ((0,), (0,))), precision=precision, preferred_element_type=f32)


def _mm(a, b, mode):
    ca, cb = {"nn": (1, 0), "nt": (1, 1), "tn": (0, 0)}[mode]
    if a.ndim == 3:
        dims = (((ca + 1,), (cb + 1,)), ((0,), (0,)))
    else:
        dims = (((ca,), (cb,)), ((), ()))
    return lax.dot_general(a, b, dims, preferred_element_type=f32)


def _split(a):
    hi = a.astype(bf16)
    return hi, (a - hi.astype(f32)).astype(bf16)


def _sel2(a, e, mode, e_left):
    hi, lo = _split(a)
    eb = e.astype(bf16)
    if e_left:
        return _mm(eb, hi, mode) + _mm(eb, lo, mode)
    return _mm(hi, eb, mode) + _mm(lo, eb, mode)


@jax.custom_vjp
def _sel_r(a, e):
    return _sel2(a, e, "nn", False)


_sel_r.defvjp(lambda a, e: (_sel_r(a, e), e), lambda e, g: (_sel2(g, e, "nt", False), jnp.zeros_like(e)))


@jax.custom_vjp
def _sel_l(e, b):
    return _sel2(b, e, "nn", True)


_sel_l.defvjp(lambda e, b: (_sel_l(e, b), e), lambda e, g: (jnp.zeros_like(e), _sel2(g, e, "tn", True)))


def _bdiag(x, bd2):
    return jnp.concatenate([x, x], axis=1) * bd2


@jax.custom_vjp
def _solve_given_inv(m, vb, kbg, inv, bd2):
    return _bmm(inv, _bdiag(vb, bd2), P3), _bmm(inv, _bdiag(kbg, bd2), P3)


def _solve_fwd(m, vb, kbg, inv, bd2):
    u, w = _solve_given_inv(m, vb, kbg, inv, bd2)
    return (u, w), (inv, u, w, bd2)


def _solve_bwd(res, cts):
    inv, u, w, bd2 = res
    du, dw = cts
    c = inv.shape[1]
    t = jnp.swapaxes(_bdiag(inv, bd2), 1, 2)
    inv_t = t[:, :c] + t[:, c:]
    dvb = _bmm(inv_t, _bdiag(du, bd2), P3)
    dkbg = _bmm(inv_t, _bdiag(dw, bd2), P3)
    dm = _bmm_nt(dvb, _bdiag(u, bd2), P3) + _bmm_nt(dkbg, _bdiag(w, bd2), P3)
    return dm, dvb, dkbg, jnp.zeros_like(inv), jnp.zeros_like(bd2)


_solve_given_inv.defvjp(_solve_fwd, _solve_bwd)


def _gdn_step(s, q, k, v, gate, alog, dtb, tmask, tmask2, strict2, exp_g, exp_b, exp_gt, dsel2, eye2, bd2, inv=None):
    z, c, w_ = q.shape
    ne = gate.shape[0]

    def per_pair(a):
        return jnp.broadcast_to(a[:, None], (ne, z // ne) + a.shape[1:]).reshape((z,) + a.shape[1:])

    def rows(a):
        return a.reshape(z * c, w_)

    def bdiag(x):
        return _bdiag(x, bd2)

    g = per_pair(-jnp.exp(alog) * _softplus(gate + dtb))
    beta = per_pair(jax.nn.sigmoid(gate))
    gl = _sel_r(g, exp_g)
    gc_l = _sel_l(tmask, gl)
    glast_l = jnp.sum(gl, axis=1, keepdims=True)
    glast = jnp.sum(g, axis=1, keepdims=True)
    beta_l = _sel_r(beta, exp_b)
    gc_r = jnp.sum(gc_l * dsel2, axis=1, keepdims=True)
    qn = q * lax.rsqrt(_sel_r(rows(q * q), bd2).reshape(z, c, w_) + EPS)
    kn = k * lax.rsqrt(_sel_r(rows(k * k), bd2).reshape(z, c, w_) + EPS)
    eg = jnp.exp(gc_l)
    kb = kn * beta_l
    vb = v * beta_l
    kbg = kb * eg
    qs = qn * (HD ** -0.5)
    dec = jnp.exp(jnp.where(tmask2 > 0, gc_l - gc_r, -1e30))
    kns = bdiag(kn)
    m = -(_bmm_nt(kb, kns) * dec * strict2)
    if inv is None:
        inv = eye2 + m
        p = _bmm(m, bdiag(m), P3)
        for _ in range(4):
            both = _bmm(jnp.concatenate([p, inv], axis=1), bdiag(p), P3)
            inv = inv + both[:, c:]
            p = both[:, :c]
        inv = inv + _bmm(inv, bdiag(p), P3)
        uw = _bmm(inv, jnp.concatenate([bdiag(vb), bdiag(kbg)], axis=2), P3)
        u, w = uw[:, :, :w_], uw[:, :, w_:]
    else:
        u, w = _solve_given_inv(m, vb, kbg, inv, bd2)
    v_new = u - _bmm(w, s)
    k_tail = kn * jnp.exp(glast_l - gc_l)
    cdec = jnp.sum(exp_gt * jnp.exp(glast), axis=-1, keepdims=True)
    s_new = s * cdec + bd2 * _bmm_tn(k_tail, v_new)
    a = _bmm_nt(qs, kns) * dec
    o = _bmm(qs * eg, s) + _bmm(a, bdiag(v_new))
    return s_new, o, inv


def _gdn_finish(o_f, o_b, z, norm_g, bd):
    o = o_f + o_b
    ms = _head_sum(o * o, bd) * (1.0 / HD)
    return o * lax.rsqrt(ms + EPS) * norm_g * _silu(z)


def _gelu(x):
    return 0.5 * x * (1.0 + jnp.tanh(0.7978845608028654 * (x + 0.044715 * (x * x * x))))


def _sg_block(u0, u1, v0, v1, z0, z1, w, b, hmp, bdr):
    ts = u0.shape[0]
    nc = ts // RC
    g0, g1 = _gelu(v0), _gelu(v1)
    mu = (jnp.sum(g0, axis=-1, keepdims=True) + jnp.sum(g1, axis=-1, keepdims=True)) * (1.0 / BW)
    x0, x1 = g0 - mu, g1 - mu
    var = (jnp.sum(x0 * x0, axis=-1, keepdims=True) + jnp.sum(x1 * x1, axis=-1, keepdims=True)) * (1.0 / BW)
    rstd = lax.rsqrt(var + EPS)
    ys = []
    for p, (u, xc, z) in enumerate(((u0, x0, z0), (u1, x1, z1))):
        vn = (xc * rstd).reshape(nc, RC, PAIR_W)
        wp = jnp.concatenate([w[2 * p], w[2 * p + 1]], axis=1)
        mix = _bmm(jnp.broadcast_to(wp, (nc, RC, 2 * RC)), jnp.concatenate([vn, vn], axis=1) * bdr)
        bias = _dot_tn(b, hmp[p], precision=HI)
        s = (mix + bias).reshape(ts, PAIR_W)
        ys.append(_gelu(u) * s * _silu(z))
    return ys[0], ys[1]


def _make_shifts(t_ctx, n):
    def dn(x):
        t = lax.broadcasted_iota(jnp.int32, x.shape, 0)
        return jnp.where((t != 0) & (t != t_ctx), pltpu.roll(x, 1, axis=0), 0.0)

    def up(x):
        t = lax.broadcasted_iota(jnp.int32, x.shape, 0)
        return jnp.where((t != t_ctx - 1) & (t != n - 1), pltpu.roll(x, n - 1, axis=0), 0.0)

    @jax.custom_vjp
    def shift_dn(x):
        return dn(x)
    shift_dn.defvjp(lambda x: (dn(x), None), lambda _, g: (up(g),))

    @jax.custom_vjp
    def shift_up(x):
        return up(x)
    shift_up.defvjp(lambda x: (up(x), None), lambda _, g: (dn(g),))
    return shift_dn, shift_up


def _conv3(x, w, shift_dn, shift_up):
    return shift_dn(x) * w[0:1] + x * w[1:2] + shift_up(x) * w[2:3]


def inproj_fwd(x, shift_t, scale_t, g_pre, w_in, n_batch, sb):
    n = x.shape[0]

    def sel(i):
        return jnp.where(i % sb == 0, n_batch, i // sb)

    def body(x_ref, sh0, sh1, sc0, sc1, g_ref, w_ref, p_ref, h_ref):
        hs = []
        for k, (sh_ref, sc_ref) in enumerate(((sh0, sc0), (sh1, sc1))):
            xv = x_ref[k * TM:(k + 1) * TM, :]
            r = xv * lax.rsqrt(jnp.mean(xv * xv, axis=-1, keepdims=True) + EPS)
            hs.append(((r * g_ref[...]) * (1.0 + sc_ref[0]) + sh_ref[0]).astype(bf16))
        hb = jnp.concatenate(hs, axis=0)
        h_ref[...] = hb
        p_ref[...] = _dot(hb, w_ref[...])

    def mrow(k):
        return pl.BlockSpec((1, 1, D), lambda i: (sel(2 * i + k), 0, 0))

    return pl.pallas_call(
        body, name="inproj_fwd", grid=(n // TP,),
        in_specs=[pl.BlockSpec((TP, D), lambda i: (i, 0)), mrow(0), mrow(1), mrow(0), mrow(1),
                  _full((1, D)), _full((D, PW))],
        out_specs=[pl.BlockSpec((TP, PW), lambda i: (i, 0)), pl.BlockSpec((TP, D), lambda i: (i, 0))],
        out_shape=[_sds((n, PW)), _sds((n, D), bf16)],
        compiler_params=_cparams(("arbitrary",), VMEM_BIG),
    )(x, shift_t, shift_t, scale_t, scale_t, g_pre, w_in)


def outproj_fwd(ys, w_out, x, gate_t, g_post, n_batch, sb):
    n = x.shape[0]

    def sel(i):
        return jnp.where(i % sb == 0, n_batch, i // sb)

    def body(y0, y1, y2, y3, w_ref, x_ref, gt0, gt1, g_ref, xn_ref, o_ref):
        y = jnp.concatenate([y0[...], y1[...], y2[...], y3[...]], axis=1)
        o = _dot(y, w_ref[...])
        o_ref[...] = o
        nrm = o * lax.rsqrt(jnp.mean(o * o, axis=-1, keepdims=True) + EPS) * g_ref[...]
        for k, gt_ref in enumerate((gt0, gt1)):
            rows = slice(k * TM, (k + 1) * TM)
            xn_ref[rows, :] = x_ref[rows, :] + gt_ref[0] * nrm[rows]

    def mrow(k):
        return pl.BlockSpec((1, 1, D), lambda i: (sel(2 * i + k), 0, 0))

    yspec = pl.BlockSpec((TP, BW), lambda i: (i, 0))
    return pl.pallas_call(
        body, name="outproj_fwd", grid=(n // TP,),
        in_specs=[yspec, yspec, yspec, yspec, _full((D, D)), pl.BlockSpec((TP, D), lambda i: (i, 0)),
                  mrow(0), mrow(1), _full((1, D))],
        out_specs=[pl.BlockSpec((TP, D), lambda i: (i, 0)), pl.BlockSpec((TP, D), lambda i: (i, 0))],
        out_shape=[_sds((n, D)), _sds((n, D))],
        compiler_params=_cparams(("arbitrary",), VMEM_BIG),
    )(*ys, w_out, x, gate_t, gate_t, g_post)


def _row_onehot(r):
    return lax.broadcasted_iota(jnp.int32, (8, 1), 0) == r


def outproj_bwd(dxn, o, gate_t, g_post, ys, w_out, n_batch, sb):
    n = dxn.shape[0]

    def sel(i):
        return jnp.where(i % sb == 0, n_batch, i // sb)

    def body(dxn_ref, o_ref, gt0, gt1, g_ref, y0, y1, y2, y3, w_ref, dy_ref, dw_ref, dg_ref, dgate_ref):
        i = pl.program_id(0)

        @pl.when(i == 0)
        def _():
            dw_ref[...] = jnp.zeros_like(dw_ref)
            dg_ref[...] = jnp.zeros_like(dg_ref)
            dgate_ref[...] = jnp.zeros_like(dgate_ref)

        g = g_ref[...]
        dos = []
        for k, gt_ref in enumerate((gt0, gt1)):
            rows = slice(k * TM, (k + 1) * TM)
            ov = o_ref[rows, :]
            rstd = lax.rsqrt(jnp.mean(ov * ov, axis=-1, keepdims=True) + EPS)
            r = ov * rstd
            dx = dxn_ref[rows, :]
            dgate_ref[...] += jnp.where(_row_onehot(sel(2 * i + k)), jnp.sum(dx * (r * g), axis=0, keepdims=True), 0.0)
            dn = dx * gt_ref[0]
            dg_ref[...] += jnp.sum(dn * r, axis=0, keepdims=True)
            dr = dn * g
            dos.append((rstd * (dr - r * jnp.mean(dr * r, axis=-1, keepdims=True))).astype(bf16))
        dob = jnp.concatenate(dos, axis=0)
        dy_ref[...] = _dot_nt(dob, w_ref[...])
        y = jnp.concatenate([y0[...], y1[...], y2[...], y3[...]], axis=1)
        dw_ref[...] += _dot_tn(y, dob)

    def mrow(k):
        return pl.BlockSpec((1, 1, D), lambda i: (sel(2 * i + k), 0, 0))

    yspec = pl.BlockSpec((TP, BW), lambda i: (i, 0))
    row = pl.BlockSpec((TP, D), lambda i: (i, 0))
    return pl.pallas_call(
        body, name="outproj_bwd", grid=(n // TP,),
        in_specs=[row, row, mrow(0), mrow(1), _full((1, D)), yspec, yspec, yspec, yspec, _full((D, D))],
        out_specs=[row, _full((D, D)), _full((1, D)), _full((8, D))],
        out_shape=[_sds((n, D)), _sds((D, D)), _sds((1, D)), _sds((8, D))],
        compiler_params=_cparams(("arbitrary",), VMEM_BIG),
    )(dxn, o, gate_t, gate_t, g_post, *ys, w_out)


def inproj_bwd_x(dp, w_in, x, scale_t, g_pre, dxn, n_batch, sb, xchg=None):
    n = x.shape[0]

    def sel(i):
        return jnp.where(i % sb == 0, n_batch, i // sb)

    def body(dp_ref, w_ref, x_ref, sc0, sc1, g_ref, dxn_ref, dx_ref, dg_ref, dsh_ref, dsc_ref):
        i = pl.program_id(0)

        @pl.when(i == 0)
        def _():
            dg_ref[...] = jnp.zeros_like(dg_ref)
            dsh_ref[...] = jnp.zeros_like(dsh_ref)
            dsc_ref[...] = jnp.zeros_like(dsc_ref)

        dh_all = _dot_nt(dp_ref[...], w_ref[...])
        g = g_ref[...]
        for k, sc_ref in enumerate((sc0, sc1)):
            rows = slice(k * TM, (k + 1) * TM)
            dh = dh_all[rows]
            xv = x_ref[rows, :]
            rstd = lax.rsqrt(jnp.mean(xv * xv, axis=-1, keepdims=True) + EPS)
            r = xv * rstd
            hot = _row_onehot(sel(2 * i + k))
            dsh_ref[...] += jnp.where(hot, jnp.sum(dh, axis=0, keepdims=True), 0.0)
            dsc_ref[...] += jnp.where(hot, jnp.sum(dh * (r * g), axis=0, keepdims=True), 0.0)
            t = dh * (1.0 + sc_ref[0])
            dg_ref[...] += jnp.sum(t * r, axis=0, keepdims=True)
            dr = t * g
            dx_ref[rows, :] = dxn_ref[rows, :] + rstd * (dr - r * jnp.mean(dr * r, axis=-1, keepdims=True))

    def mrow(k):
        return pl.BlockSpec((1, 1, D), lambda i: (sel(2 * i + k), 0, 0))

    row = pl.BlockSpec((TP, D), lambda i: (i, 0))
    fused, x_in, x_out, x_shape, x_scratch = _with_exchange(body, 7, 4, 0, xchg, n // TP)
    return pl.pallas_call(
        fused, name="inproj_bwd_x" + ("" if xchg is None else "_" + xchg[0]), grid=(n // TP,),
        in_specs=[pl.BlockSpec((TP, PW), lambda i: (i, 0)), _full((D, PW)), row, mrow(0), mrow(1), _full((1, D)), row]
                 + x_in,
        out_specs=[row, _full((1, D)), _full((8, D)), _full((8, D))] + x_out,
        out_shape=[_sds((n, D)), _sds((1, D)), _sds((8, D)), _sds((8, D))] + x_shape,
        scratch_shapes=x_scratch,
        compiler_params=_cparams(("arbitrary",), VMEM_BIG),
    )(dp, w_in, x, scale_t, scale_t, g_pre, dxn, *([] if xchg is None else xchg[1]))


def dw_in(h, dp):
    n = h.shape[0]
    tk, tn = (1536 if n % 1536 == 0 else 512), 1024
    nk = n // tk

    def body(h_ref, dp_ref, o_ref, acc):
        k = pl.program_id(1)

        @pl.when(k == 0)
        def _():
            acc[...] = jnp.zeros_like(acc)
        acc[...] += _dot_tn(dp_ref[...], h_ref[...])

        @pl.when(k == nk - 1)
        def _():
            o_ref[...] = acc[...].astype(bf16)

    return pl.pallas_call(
        body, name="dw_in", grid=(PW // tn, nk),
        in_specs=[pl.BlockSpec((tk, D), lambda j, k: (k, 0)), pl.BlockSpec((tk, tn), lambda j, k: (k, j))],
        out_specs=pl.BlockSpec((tn, D), lambda j, k: (j, 0)),
        out_shape=_sds((PW, D), bf16),
        scratch_shapes=[pltpu.VMEM((tn, D), f32)],
        compiler_params=_cparams(("parallel", "arbitrary"), VMEM_BIG),
    )(h, dp)


def place_weights(slabs):
    n_ch, d, wc = slabs.shape

    def body(w_ref, o_ref):
        acc = jnp.pad(w_ref[0].astype(f32), ((0, 0), (0, PW - wc)))
        for k in range(1, n_ch):
            acc = acc + pltpu.roll(jnp.pad(w_ref[k].astype(f32), ((0, 0), (0, PW - wc))), wc * k, axis=1)
        o_ref[...] = acc.astype(bf16)

    return pl.pallas_call(
        body, name="place_weights", grid=(d // TM,),
        in_specs=[pl.BlockSpec((n_ch, TM, wc), lambda i: (0, i, 0))],
        out_specs=pl.BlockSpec((TM, PW), lambda i: (i, 0)),
        out_shape=_sds((d, PW), bf16),
        compiler_params=_cparams(("arbitrary",), VMEM_BIG),
    )(slabs)


def loss_head(xf, target, t_ctx):
    nb, s, _ = xf.shape
    jc = t_ctx // TM

    def body(x_ref, t_ref, dx_ref, l_ref):
        b, j = pl.program_id(0), pl.program_id(1)

        @pl.when((b == 0) & (j == 0))
        def _():
            l_ref[...] = jnp.zeros_like(l_ref)

        @pl.when(j < jc)
        def _():
            dx_ref[...] = jnp.zeros_like(dx_ref)

        @pl.when(j >= jc)
        def _():
            diff = x_ref[0] - t_ref[0]
            dx_ref[0] = diff * (1.0 / D)
            l_ref[...] += 0.5 * jnp.sum(diff * diff) * (1.0 / D)

    return pl.pallas_call(
        body, name="loss_head", grid=(nb, s // TM),
        in_specs=[pl.BlockSpec((1, TM, D), lambda b, j: (b, j, 0)),
                  pl.BlockSpec((1, TM, D), lambda b, j: (b, jnp.maximum(j - jc, 0), 0))],
        out_specs=[pl.BlockSpec((1, TM, D), lambda b, j: (b, j, 0)), _full((1, 128))],
        out_shape=[_sds((nb, s, D)), _sds((1, 128))],
        compiler_params=_cparams(("arbitrary", "arbitrary")),
    )(xf, target)


def _chunk_maps(n_ctx, n_lat):
    n = n_ctx + n_lat

    def cf(t):
        return t

    def cb(t):
        return jnp.where(t < n_ctx, n_ctx - 1 - t, n - 1 - t + n_ctx)
    return n, cf, cb


def ret_scan_fwd(p3, cos, sins, consts, t_ctx):
    nb, s, _ = p3.shape
    n, cf, cb = _chunk_maps(t_ctx // RC, (s - t_ctx) // RC)
    nz = 4 * nb

    def body(qf, kf, vf, qb, kb, vb, cosf, sinf, cosb, sinb, intra_r, qdec_r, kdec_r, cd_r, bd_r, bdr_r,
             of_ref, ob_ref, sall_ref, s_sc):
        @pl.when(pl.program_id(0) == 0)
        def _():
            s_sc[...] = jnp.zeros_like(s_sc)
        st = s_sc[...]
        sall_ref[0] = st
        s_new, o = _ret_step(st, _pairs(qf, qb, nb), _pairs(kf, kb, nb), _pairs(vf, vb, nb),
                             _pair_tables(cosf, cosb, nb), _pair_tables(sinf, sinb, nb), intra_r[...], qdec_r[...],
                             kdec_r[...], cd_r[...], bd_r[...], bdr_r[...])
        s_sc[...] = s_new
        _unpairs(o, of_ref, ob_ref, nb)

    def pspec(m, seg):
        return pl.BlockSpec((nb, RC, BW), lambda t: (0, m(t), seg))

    def tspec(m):
        return pl.BlockSpec((RC, PAIR_W), lambda t: (m(t), 0))

    return pl.pallas_call(
        body, name="ret_scan_fwd", grid=(n,),
        in_specs=[pspec(cf, 0), pspec(cf, 1), pspec(cf, 2), pspec(cb, 0), pspec(cb, 1), pspec(cb, 2),
                  tspec(cf), tspec(cf), tspec(cb), tspec(cb)] + [_full(c.shape) for c in consts],
        out_specs=[pl.BlockSpec((nb, RC, BW), lambda t: (0, cf(t), 0)),
                   pl.BlockSpec((nb, RC, BW), lambda t: (0, cb(t), 0)),
                   pl.BlockSpec((1, nz, PAIR_W, PAIR_W), lambda t: (t, 0, 0, 0))],
        out_shape=[_sds((nb, s, BW)), _sds((nb, s, BW)), _sds((n, nz, PAIR_W, PAIR_W))],
        scratch_shapes=[pltpu.VMEM((nz, PAIR_W, PAIR_W), f32)],
        compiler_params=_cparams(("arbitrary",)),
    )(p3, p3, p3, p3, p3, p3, cos, sins, cos, sins, *consts)


def ret_scan_bwd(p3, cos, sins, consts, s_all, do, t_ctx):
    nb, s, _ = p3.shape
    n, cf, cb = _chunk_maps(t_ctx // RC, (s - t_ctx) // RC)
    nz = 4 * nb

    def rf(t):
        return cf(n - 1 - t)

    def rb(t):
        return cb(n - 1 - t)

    def body(qf, kf, vf, qb, kb, vb, cosf, sinf, cosb, sinb, intra_r, qdec_r, kdec_r, cd_r, bd_r, bdr_r,
             sall_ref, dof, dob, dqf, dkf, dvf, dqb, dkb, dvb, ds_sc):
        @pl.when(pl.program_id(0) == 0)
        def _():
            ds_sc[...] = jnp.zeros_like(ds_sc)
        step = functools.partial(_ret_step, cos=_pair_tables(cosf, cosb, nb), sins=_pair_tables(sinf, sinb, nb),
                                 intra=intra_r[...], qdec=qdec_r[...], kdec=kdec_r[...], cd=cd_r[...], bd2=bd_r[...],
                                 bdr=bdr_r[...])
        _, vjp = jax.vjp(step, sall_ref[0], _pairs(qf, qb, nb), _pairs(kf, kb, nb), _pairs(vf, vb, nb))
        ds, dq, dk, dv = vjp((ds_sc[...], _pairs(dof, dob, nb)))
        ds_sc[...] = ds
        _unpairs(dq, dqf, dqb, nb)
        _unpairs(dk, dkf, dkb, nb)
        _unpairs(dv, dvf, dvb, nb)

    def pspec(m, seg):
        return pl.BlockSpec((nb, RC, BW), lambda t: (0, m(t), seg))

    def tspec(m):
        return pl.BlockSpec((RC, PAIR_W), lambda t: (m(t), 0))

    def ospec(m):
        return pl.BlockSpec((nb, RC, BW), lambda t: (0, m(t), 0))

    return pl.pallas_call(
        body, name="ret_scan_bwd", grid=(n,),
        in_specs=[pspec(rf, 0), pspec(rf, 1), pspec(rf, 2), pspec(rb, 0), pspec(rb, 1), pspec(rb, 2),
                  tspec(rf), tspec(rf), tspec(rb), tspec(rb)] + [_full(c.shape) for c in consts]
                 + [pl.BlockSpec((1, nz, PAIR_W, PAIR_W), lambda t: (n - 1 - t, 0, 0, 0)), ospec(rf), ospec(rb)],
        out_specs=[ospec(rf), ospec(rf), ospec(rf), ospec(rb), ospec(rb), ospec(rb)],
        out_shape=[_sds((nb, s, BW), bf16)] * 6,
        scratch_shapes=[pltpu.VMEM((nz, PAIR_W, PAIR_W), f32)],
        compiler_params=_cparams(("arbitrary",), VMEM_BIG),
    )(p3, p3, p3, p3, p3, p3, cos, sins, cos, sins, *consts, s_all, do, do)


def mix_finish_fwd(fn, name, o_f, o_b, p3, zseg, norm_g, bd):
    nb, s, _ = p3.shape

    def body(of_ref, ob_ref, z_ref, g_ref, bd_ref, y_ref):
        y_ref[0] = fn(of_ref[0], ob_ref[0], z_ref[0], g_ref[...], bd_ref[...]).astype(bf16)

    blk = pl.BlockSpec((1, TM, BW), lambda b, j: (b, j, 0))
    return pl.pallas_call(
        body, name=name, grid=(nb, s // TM),
        in_specs=[blk, blk, pl.BlockSpec((1, TM, BW), lambda b, j: (b, j, zseg)), _full((1, BW)), _full((BW, BW))],
        out_specs=blk, out_shape=_sds((nb, s, BW), bf16),
        compiler_params=_cparams(("arbitrary", "arbitrary")),
    )(o_f, o_b, p3, norm_g, bd)


def mix_finish_bwd(fn, name, o_f, o_b, p3, zseg, norm_g, bd, dy3, yseg):
    nb, s, _ = p3.shape

    def body(of_ref, ob_ref, z_ref, g_ref, bd_ref, dy_ref, do_ref, dz_ref, dg_ref):
        @pl.when((pl.program_id(0) == 0) & (pl.program_id(1) == 0))
        def _():
            dg_ref[...] = jnp.zeros_like(dg_ref)
        bdv = bd_ref[...]
        _, vjp = jax.vjp(lambda a, b, z, g: fn(a, b, z, g, bdv), of_ref[0], ob_ref[0], z_ref[0], g_ref[...])
        do, _, dz, dg = vjp(dy_ref[0])
        do_ref[0] = do
        dz_ref[0] = dz.astype(bf16)
        dg_ref[...] += dg

    blk = pl.BlockSpec((1, TM, BW), lambda b, j: (b, j, 0))
    return pl.pallas_call(
        body, name=name, grid=(nb, s // TM),
        in_specs=[blk, blk, pl.BlockSpec((1, TM, BW), lambda b, j: (b, j, zseg)), _full((1, BW)), _full((BW, BW)),
                  pl.BlockSpec((1, TM, BW), lambda b, j: (b, j, yseg))],
        out_specs=[blk, blk, _full((1, BW))],
        out_shape=[_sds((nb, s, BW)), _sds((nb, s, BW), bf16), _sds((1, BW))],
        compiler_params=_cparams(("arbitrary", "arbitrary")),
    )(o_f, o_b, p3, norm_g, bd, dy3)


GDN_QKV = 11 * BW // 128
N_QKV = 3 * BW // 128


def gdn_conv_fwd(p3, w, t_ctx):
    nb, s, _ = p3.shape
    sd, su = _make_shifts(t_ctx, s)

    def body(x_ref, w_ref, o_ref):
        o_ref[0] = _silu(_conv3(x_ref[0], w_ref[...], sd, su))

    return pl.pallas_call(
        body, name="gdn_conv_fwd", grid=(nb, N_QKV),
        in_specs=[pl.BlockSpec((1, s, 128), lambda b, j: (b, 0, GDN_QKV + j)), pl.BlockSpec((3, 128), lambda b, j: (0, j))],
        out_specs=pl.BlockSpec((1, s, 128), lambda b, j: (b, 0, j)),
        out_shape=_sds((nb, s, 3 * BW)),
        compiler_params=_cparams(("arbitrary", "arbitrary")),
    )(p3, w)


def gdn_conv_bwd(p3, w, d_f, d_b, t_ctx):
    nb, s, _ = p3.shape
    sd, su = _make_shifts(t_ctx, s)

    def body(x_ref, w_ref, df_ref, db_ref, dx_ref, dw_ref):
        @pl.when(pl.program_id(1) == 0)
        def _():
            dw_ref[...] = jnp.zeros_like(dw_ref)
        _, vjp = jax.vjp(lambda x, w_: _silu(_conv3(x, w_, sd, su)), x_ref[0], w_ref[...])
        dx, dw = vjp(df_ref[0] + db_ref[0])
        dx_ref[0] = dx.astype(bf16)
        dw_ref[...] += dw

    blk = pl.BlockSpec((1, s, 128), lambda j, b: (b, 0, j))
    return pl.pallas_call(
        body, name="gdn_conv_bwd", grid=(N_QKV, nb),
        in_specs=[pl.BlockSpec((1, s, 128), lambda j, b: (b, 0, GDN_QKV + j)), pl.BlockSpec((3, 128), lambda j, b: (0, j)),
                  blk, blk],
        out_specs=[blk, pl.BlockSpec((3, 128), lambda j, b: (0, j))],
        out_shape=[_sds((nb, s, 3 * BW), bf16), _sds((3, 3 * BW))],
        compiler_params=_cparams(("arbitrary", "arbitrary"), VMEM_BIG),
    )(p3, w, d_f, d_b)


def _pairs(f_ref, b_ref, nb):
    return jnp.stack([r[b, :, PAIR_W * p:PAIR_W * (p + 1)] for b in range(nb) for r in (f_ref, b_ref) for p in range(2)])


def _pair_tables(f_ref, b_ref, nb):
    return jnp.stack([r[...] for _ in range(nb) for r in (f_ref, b_ref) for _ in range(2)])


def _gates(f_ref, b_ref, nb):
    return jnp.stack([r[b] for b in range(nb) for r in (f_ref, b_ref)])


def _unpairs(a, f_ref, b_ref, nb, lane0=0):
    for b in range(nb):
        for d, r in enumerate((f_ref, b_ref)):
            for p in range(2):
                r[b, :, lane0 + PAIR_W * p:lane0 + PAIR_W * (p + 1)] = a[4 * b + 2 * d + p].astype(r.dtype)


def _with_exchange(body, n_in, n_out, n_scratch, xchg, n_steps):
    if xchg is None:
        return body, [], [], [], []
    kind, arrs = xchg
    nx = len(arrs)

    def fused(*refs):
        ins, rest = refs[:n_in], refs[n_in:]
        srcs, rest = rest[:nx], rest[nx:]
        outs, rest = rest[:n_out], rest[n_out:]
        dsts, rest = rest[:nx], rest[nx:]
        scratch, sems = rest[:n_scratch], rest[n_scratch:]
        start, wait = _peer_exchange(kind, "chips", srcs, dsts, *sems)
        pl.when(pl.program_id(0) == 0)(start)
        body(*ins, *outs, *scratch)
        pl.when(pl.program_id(0) == n_steps - 1)(wait)

    any_ = pl.BlockSpec(memory_space=pl.ANY)
    return fused, [any_] * nx, [any_] * nx, _exchange_shapes(kind, "chips", arrs), _exchange_scratch("chips", nx)


def gdn_scan_fwd(c3, p3, alog, dtb, consts, t_ctx, xchg=None):
    nb, s, _ = p3.shape
    n, cf, cb = _chunk_maps(t_ctx // GC, (s - t_ctx) // GC)
    gblk = GATE_COL // 128

    nz = 4 * nb

    def body(qf, kf, vf, gf, qb, kb, vb, gb, al_ref, dt_ref, tm_r, tm2_r, st2_r, eg_r, eb_r, egt_r, dsel_r, eye_r, bd_r,
             of_ref, ob_ref, sall_ref, inv_ref, s_sc):
        @pl.when(pl.program_id(0) == 0)
        def _():
            s_sc[...] = jnp.zeros_like(s_sc)
        st = s_sc[...]
        sall_ref[0] = st
        s_new, o, inv = _gdn_step(st, _pairs(qf, qb, nb), _pairs(kf, kb, nb), _pairs(vf, vb, nb), _gates(gf, gb, nb),
                                  al_ref[...], dt_ref[...], tm_r[...], tm2_r[...], st2_r[...], eg_r[...], eb_r[...],
                                  egt_r[...], dsel_r[...], eye_r[...], bd_r[...])
        s_sc[...] = s_new
        inv_ref[0] = inv
        _unpairs(o, of_ref, ob_ref, nb)

    def cspec(m, col=0):
        return pl.BlockSpec((nb, GC, BW), lambda t: (0, m(t), col))

    def gspec(m):
        return pl.BlockSpec((nb, GC, 128), lambda t: (0, m(t), gblk))

    fused, x_in, x_out, x_shape, x_scratch = _with_exchange(body, 10 + len(consts), 4, 1, xchg, n)
    return pl.pallas_call(
        fused, name="gdn_scan_fwd" + ("" if xchg is None else "_" + xchg[0]), grid=(n,),
        in_specs=[cspec(cf, 0), cspec(cf, 1), cspec(cf, 2), gspec(cf), cspec(cb, 0), cspec(cb, 1), cspec(cb, 2), gspec(cb),
                  _full((1, 128)), _full((1, 128))] + [_full(c.shape) for c in consts] + x_in,
        out_specs=[cspec(cf), cspec(cb), pl.BlockSpec((1, nz, PAIR_W, PAIR_W), lambda t: (t, 0, 0, 0)),
                   pl.BlockSpec((1, nz, GC, PAIR_W), lambda t: (t, 0, 0, 0))] + x_out,
        out_shape=[_sds((nb, s, BW)), _sds((nb, s, BW)), _sds((n, nz, PAIR_W, PAIR_W)), _sds((n, nz, GC, PAIR_W))]
                  + x_shape,
        scratch_shapes=[pltpu.VMEM((nz, PAIR_W, PAIR_W), f32)] + x_scratch,
        compiler_params=_cparams(("arbitrary",)),
    )(c3, c3, c3, p3, c3, c3, c3, p3, alog, dtb, *consts, *([] if xchg is None else xchg[1]))


def gdn_scan_bwd(c3, p3, alog, dtb, consts, s_all, inv_all, do, t_ctx, xchg=None):
    nb, s, _ = p3.shape
    n, cf, cb = _chunk_maps(t_ctx // GC, (s - t_ctx) // GC)
    gblk = GATE_COL // 128

    def rf(t):
        return cf(n - 1 - t)

    def rb(t):
        return cb(n - 1 - t)

    nz = 4 * nb

    def body(qf, kf, vf, gf, qb, kb, vb, gb, al_ref, dt_ref, tm_r, tm2_r, st2_r, eg_r, eb_r, egt_r, dsel_r, eye_r, bd_r,
             sall_ref, inv_ref, dof, dob, dcf, dgf, dcb, dgb, dal_ref, ddt_ref, ds_sc):
        @pl.when(pl.program_id(0) == 0)
        def _():
            dal_ref[...] = jnp.zeros_like(dal_ref)
            ddt_ref[...] = jnp.zeros_like(ddt_ref)
            ds_sc[...] = jnp.zeros_like(ds_sc)
        consts = dict(tmask=tm_r[...], tmask2=tm2_r[...], strict2=st2_r[...], exp_g=eg_r[...], exp_b=eb_r[...],
                      exp_gt=egt_r[...], dsel2=dsel_r[...], eye2=eye_r[...], bd2=bd_r[...], inv=inv_ref[0])

        def step(*a):
            return _gdn_step(*a, **consts)[:2]

        _, vjp = jax.vjp(step, sall_ref[0], _pairs(qf, qb, nb), _pairs(kf, kb, nb), _pairs(vf, vb, nb),
                         _gates(gf, gb, nb), al_ref[...], dt_ref[...])
        ds, dq, dk, dv, dg, dal, ddt = vjp((ds_sc[...], _pairs(dof, dob, nb)))
        ds_sc[...] = ds
        for i, a in enumerate((dq, dk, dv)):
            _unpairs(a, dcf, dcb, nb, BW * i)
        for b in range(nb):
            dgf[b] = dg[2 * b].astype(bf16)
            dgb[b] = dg[2 * b + 1].astype(bf16)
        dal_ref[...] += dal
        ddt_ref[...] += ddt

    def cspec(m, col=0):
        return pl.BlockSpec((nb, GC, BW), lambda t: (0, m(t), col))

    def gspec(m):
        return pl.BlockSpec((nb, GC, 128), lambda t: (0, m(t), gblk))

    def dcout(m):
        return pl.BlockSpec((nb, GC, 3 * BW), lambda t: (0, m(t), 0))

    def gout(m):
        return pl.BlockSpec((nb, GC, 128), lambda t: (0, m(t), 0))

    fused, x_in, x_out, x_shape, x_scratch = _with_exchange(body, 14 + len(consts), 6, 1, xchg, n)
    return pl.pallas_call(
        fused, name="gdn_scan_bwd" + ("" if xchg is None else "_" + xchg[0]), grid=(n,),
        in_specs=[cspec(rf, 0), cspec(rf, 1), cspec(rf, 2), gspec(rf), cspec(rb, 0), cspec(rb, 1), cspec(rb, 2), gspec(rb),
                  _full((1, 128)), _full((1, 128))] + [_full(c.shape) for c in consts]
                 + [pl.BlockSpec((1, nz, PAIR_W, PAIR_W), lambda t: (n - 1 - t, 0, 0, 0)),
                    pl.BlockSpec((1, nz, GC, PAIR_W), lambda t: (n - 1 - t, 0, 0, 0)), cspec(rf), cspec(rb)] + x_in,
        out_specs=[dcout(rf), gout(rf), dcout(rb), gout(rb), _full((1, 128)), _full((1, 128))] + x_out,
        out_shape=[_sds((nb, s, 3 * BW)), _sds((nb, s, 128), bf16), _sds((nb, s, 3 * BW)), _sds((nb, s, 128), bf16),
                   _sds((1, 128)), _sds((1, 128))] + x_shape,
        scratch_shapes=[pltpu.VMEM((nz, PAIR_W, PAIR_W), f32)] + x_scratch,
        compiler_params=_cparams(("arbitrary",), VMEM_BIG),
    )(c3, c3, c3, p3, c3, c3, c3, p3, alog, dtb, *consts, s_all, inv_all, do, do, *([] if xchg is None else xchg[1]))


def _sg_consts():
    hmp = np.zeros((2, NH, PAIR_W))
    for h in range(NH):
        hmp[h // 2, h, (h % 2) * HD:(h % 2 + 1) * HD] = 1.0
    bdr = (np.arange(2 * RC)[:, None] // RC == np.arange(PAIR_W)[None, :] // HD)
    return jnp.asarray(hmp, f32), jnp.asarray(bdr, f32)


def _sg_rows(s):
    return 6 * RC if s % (6 * RC) == 0 else 2 * RC


def _halves(ref):
    return ref[0, :, :PAIR_W], ref[0, :, PAIR_W:]


def sg_fwd(p3, w, b, hmp, bdr):
    nb, s, _ = p3.shape
    ts = _sg_rows(s)

    def body(u_ref, v_ref, z_ref, w_ref, b_ref, hm_ref, bdr_ref, y_ref):
        y0, y1 = _sg_block(*_halves(u_ref), *_halves(v_ref), *_halves(z_ref), w_ref[...], b_ref[...], hm_ref[...],
                           bdr_ref[...])
        y_ref[0, :, :PAIR_W] = y0.astype(bf16)
        y_ref[0, :, PAIR_W:] = y1.astype(bf16)

    def seg(k):
        return pl.BlockSpec((1, ts, BW), lambda bi, i: (bi, i, k))

    return pl.pallas_call(
        body, name="sg_fwd", grid=(nb, s // ts),
        in_specs=[seg(4), seg(5), seg(6), _full((NH, RC, RC)), _full((NH, RC)), _full(hmp.shape), _full(bdr.shape)],
        out_specs=pl.BlockSpec((1, ts, BW), lambda bi, i: (bi, i, 0)),
        out_shape=_sds((nb, s, BW), bf16),
        compiler_params=_cparams(("arbitrary", "arbitrary")),
    )(p3, p3, p3, w, b, hmp, bdr)


def sg_bwd(p3, w, b, hmp, bdr, dy3):
    nb, s, _ = p3.shape
    ts = _sg_rows(s)

    def body(u_ref, v_ref, z_ref, w_ref, b_ref, hm_ref, bdr_ref, dy_ref, du_ref, dv_ref, dz_ref, dw_ref, db_ref):
        @pl.when((pl.program_id(0) == 0) & (pl.program_id(1) == 0))
        def _():
            dw_ref[...] = jnp.zeros_like(dw_ref)
            db_ref[...] = jnp.zeros_like(db_ref)
        hm, bdr_v = hm_ref[...], bdr_ref[...]
        _, vjp = jax.vjp(lambda *a: _sg_block(*a, hm, bdr_v), *_halves(u_ref), *_halves(v_ref), *_halves(z_ref),
                         w_ref[...], b_ref[...])
        du0, du1, dv0, dv1, dz0, dz1, dw, db = vjp(_halves(dy_ref))
        for ref, a0, a1 in ((du_ref, du0, du1), (dv_ref, dv0, dv1), (dz_ref, dz0, dz1)):
            ref[0, :, :PAIR_W] = a0.astype(bf16)
            ref[0, :, PAIR_W:] = a1.astype(bf16)
        dw_ref[...] += dw
        db_ref[...] += db

    def seg(k):
        return pl.BlockSpec((1, ts, BW), lambda bi, i: (bi, i, k))

    blk = pl.BlockSpec((1, ts, BW), lambda bi, i: (bi, i, 0))
    return pl.pallas_call(
        body, name="sg_bwd", grid=(nb, s // ts),
        in_specs=[seg(4), seg(5), seg(6), _full((NH, RC, RC)), _full((NH, RC)), _full(hmp.shape), _full(bdr.shape),
                  seg(1)],
        out_specs=[blk, blk, blk, _full((NH, RC, RC)), _full((NH, RC))],
        out_shape=[_sds((nb, s, BW), bf16)] * 3 + [_sds((NH, RC, RC)), _sds((NH, RC))],
        compiler_params=_cparams(("arbitrary", "arbitrary"), VMEM_BIG),
    )(p3, p3, p3, w, b, hmp, bdr, dy3)


def _sc_fn(b, c, h, z, w, sd, su):
    return b * _conv3(c * h, w, sd, su) * _silu(z)


def sc_fwd(p3, w, t_ctx):
    nb, s, _ = p3.shape
    sd, su = _make_shifts(t_ctx, s)

    def body(b_ref, c_ref, h_ref, z_ref, w_ref, y_ref):
        y_ref[0] = _sc_fn(b_ref[0], c_ref[0], h_ref[0], z_ref[0], w_ref[...], sd, su).astype(bf16)

    def seg(k):
        return pl.BlockSpec((1, s, 128), lambda bi, j: (bi, 0, 2 * k + j))

    return pl.pallas_call(
        body, name="sc_fwd", grid=(nb, 2),
        in_specs=[seg(7), seg(8), seg(9), seg(10), pl.BlockSpec((3, 128), lambda bi, j: (0, j))],
        out_specs=pl.BlockSpec((1, s, 128), lambda bi, j: (bi, 0, j)),
        out_shape=_sds((nb, s, BW), bf16),
        compiler_params=_cparams(("arbitrary", "arbitrary"), VMEM_BIG),
    )(p3, p3, p3, p3, w)


def sc_bwd(p3, w, dy3, t_ctx):
    nb, s, _ = p3.shape
    sd, su = _make_shifts(t_ctx, s)

    def body(b_ref, c_ref, h_ref, z_ref, w_ref, dy_ref, db_ref, dc_ref, dh_ref, dz_ref, dw_ref):
        @pl.when(pl.program_id(1) == 0)
        def _():
            dw_ref[...] = jnp.zeros_like(dw_ref)
        _, vjp = jax.vjp(lambda b, c, h, z, w_: _sc_fn(b, c, h, z, w_, sd, su),
                         b_ref[0], c_ref[0], h_ref[0], z_ref[0], w_ref[...])
        db, dc, dh, dz, dw = vjp(dy_ref[0])
        db_ref[0] = db.astype(bf16)
        dc_ref[0] = dc.astype(bf16)
        dh_ref[0] = dh.astype(bf16)
        dz_ref[0] = dz.astype(bf16)
        dw_ref[...] += dw

    def seg(k):
        return pl.BlockSpec((1, s, 128), lambda j, bi: (bi, 0, 2 * k + j))

    blk = pl.BlockSpec((1, s, 128), lambda j, bi: (bi, 0, j))
    wspec = pl.BlockSpec((3, 128), lambda j, bi: (0, j))
    return pl.pallas_call(
        body, name="sc_bwd", grid=(2, nb),
        in_specs=[seg(7), seg(8), seg(9), seg(10), wspec, seg(2)],
        out_specs=[blk, blk, blk, blk, wspec],
        out_shape=[_sds((nb, s, BW), bf16)] * 4 + [_sds((3, BW))],
        compiler_params=_cparams(("arbitrary", "arbitrary"), VMEM_BIG),
    )(p3, p3, p3, p3, w, dy3)


def assemble_dp(pairs, singles_a, gdn_x, singles_b, gates):
    nb, s, _ = singles_a[0].shape
    flat = [a for pr in pairs for a in pr] + list(singles_a) + list(gdn_x) + list(singles_b) + list(gates)
    n_pairs, n_a, n_x, n_b = len(pairs), len(singles_a), len(gdn_x), len(singles_b)

    def body(*refs):
        out = refs[-1]
        ins = refs[:-1]
        col = 0
        for p in range(n_pairs):
            out[0, :, col:col + BW] = (ins[2 * p][0].astype(f32) + ins[2 * p + 1][0].astype(f32)).astype(bf16)
            col += BW
        k = 2 * n_pairs
        for _ in range(n_a + n_x + n_b):
            wk = ins[k].shape[-1]
            out[0, :, col:col + wk] = ins[k][0]
            col += wk
            k += 1
        out[0, :, col:col + 128] = (ins[k][0].astype(f32) + ins[k + 1][0].astype(f32)).astype(bf16)
        out[0, :, col + 128:] = jnp.zeros((TM, PW - col - 128), bf16)

    def spec(a):
        return pl.BlockSpec((1, TM, a.shape[-1]), lambda b, j: (b, j, 0))

    return pl.pallas_call(
        body, name="assemble_dp", grid=(nb, s // TM),
        in_specs=[spec(a) for a in flat],
        out_specs=pl.BlockSpec((1, TM, PW), lambda b, j: (b, j, 0)),
        out_shape=_sds((nb, s, PW), bf16),
        compiler_params=_cparams(("arbitrary", "arbitrary")),
    )(*flat)


def mod_fwd(c_rows, w_mod, b_cols):
    nl, _, wc = w_mod.shape
    nr = c_rows.shape[0]

    def body(c_ref, w_ref, b_ref, o_ref):
        o_ref[0] = _dot(_silu(c_ref[...]), w_ref[0], precision=HI) + b_ref[0]

    return pl.pallas_call(
        body, name="mod_fwd", grid=(nl,),
        in_specs=[_full((nr, D)), pl.BlockSpec((1, D, wc), lambda l: (l, 0, 0)), pl.BlockSpec((1, 1, wc), lambda l: (l, 0, 0))],
        out_specs=pl.BlockSpec((1, nr, wc), lambda l: (l, 0, 0)),
        out_shape=_sds((nl, nr, wc)),
        compiler_params=_cparams(("arbitrary",)),
    )(c_rows, w_mod, b_cols)


def mod_bwd(c_rows, w_mod, dm_cols, dm_full):
    nl, _, wc = w_mod.shape
    nr = c_rows.shape[0]

    def body(c_ref, w_ref, dmc_ref, dmf_ref, gw_ref, gb_ref, dcc_ref):
        @pl.when(pl.program_id(0) == 0)
        def _():
            dcc_ref[...] = jnp.zeros_like(dcc_ref)
        a = _silu(c_ref[...])
        dmc = dmc_ref[0]
        gw_ref[0] = _dot_tn(a, dmc, precision=HI)
        gb_ref[0] = jnp.sum(dmf_ref[0], axis=0, keepdims=True)
        dcc_ref[...] += _dot_nt(dmc[nr - 8:nr], w_ref[0], precision=HI)

    return pl.pallas_call(
        body, name="mod_bwd", grid=(nl,),
        in_specs=[_full((nr, D)), pl.BlockSpec((1, D, wc), lambda l: (l, 0, 0)),
                  pl.BlockSpec((1, nr, wc), lambda l: (l, 0, 0)), pl.BlockSpec((1, nr, 3 * D), lambda l: (l, 0, 0))],
        out_specs=[pl.BlockSpec((1, D, wc), lambda l: (l, 0, 0)), pl.BlockSpec((1, 1, 3 * D), lambda l: (l, 0, 0)),
                   _full((8, D))],
        out_shape=[_sds((nl, D, wc)), _sds((nl, 1, 3 * D)), _sds((8, D))],
        compiler_params=_cparams(("arbitrary",)),
    )(c_rows, w_mod, dm_cols, dm_full)


def cctx_grad(parts, c_ctx):
    def body(p_ref, c_ref, o_ref):
        tot = p_ref[0, 0:1, :]
        for k in (2, 4, 6):
            tot = tot + p_ref[k, 0:1, :]
        c = c_ref[...]
        sg = jax.nn.sigmoid(c)
        o_ref[...] = tot * (sg * (1.0 + c * (1.0 - sg)))

    return pl.pallas_call(body, name="cctx_grad", out_shape=_sds((1, D)))(parts, c_ctx)


def sum_lead(x, out_dtype=f32, tr=256, tc=None):
    k, r, c = x.shape
    tr = min(tr, r)
    tc = c if tc is None else tc
    assert r % tr == 0 and c % tc == 0

    def body(x_ref, o_ref):
        tot = x_ref[0].astype(f32)
        for i in range(1, k):
            tot = tot + x_ref[i].astype(f32)
        o_ref[...] = tot.astype(out_dtype)

    return pl.pallas_call(
        body, name="sum_lead", grid=(r // tr, c // tc),
        in_specs=[pl.BlockSpec((k, tr, tc), lambda i, j: (0, i, j))],
        out_specs=pl.BlockSpec((tr, tc), lambda i, j: (i, j)),
        out_shape=_sds((r, c), out_dtype),
        compiler_params=_cparams(("arbitrary", "arbitrary")),
    )(x)


def adamw(w, m, v, g1, g2=None, tr=256, block=None):
    if block is None:
        block = (1,) * (w.ndim - 2) + (min(tr, w.shape[-2]), w.shape[-1])
    assert len(block) == w.ndim and all(d % b == 0 for d, b in zip(w.shape, block))
    two = g2 is not None
    c1 = 1.0 / (1.0 - ADAM_B1 ** ADAM_STEP)
    c2 = 1.0 / (1.0 - ADAM_B2 ** ADAM_STEP)

    def body(*refs):
        w_ref, m_ref, v_ref, g_ref = refs[:4]
        g = g_ref[...]
        if two:
            g = g + refs[4][...]
        go_ref, d_ref, mo_ref, vo_ref = refs[-4:]
        mn = ADAM_B1 * m_ref[...] + (1.0 - ADAM_B1) * g
        vn = ADAM_B2 * v_ref[...] + (1.0 - ADAM_B2) * (g * g)
        go_ref[...] = g
        mo_ref[...] = mn
        vo_ref[...] = vn
        d_ref[...] = -ADAM_LR * ((mn * c1) / (jnp.sqrt(vn * c2) + ADAM_EPS) + ADAM_WD * w_ref[...])

    blk = pl.BlockSpec(block, lambda *i: i)
    grid = tuple(d // b for d, b in zip(w.shape, block))
    args = [w, m, v, g1] + ([g2] if two else [])
    return pl.pallas_call(
        body, name="adamw", grid=grid,
        in_specs=[blk] * len(args), out_specs=[blk] * 4, out_shape=[_sds(w.shape)] * 4,
        compiler_params=_cparams(("arbitrary",) * len(grid)),
    )(*args)


def _my_pos():
    return lax.axis_index("x"), lax.axis_index("y"), lax.axis_index("c")


GROUP_SIZE = {"devices": N_DEV, "chips": N_CHIPS, "cores": 2}


def _peer_exchange(kind, group, src_refs, dst_refs, send_sems, recv_sems, local_sems):
    mx, my, mc = _my_pos()
    n = GROUP_SIZE[group]
    if group == "devices":
        me = 4 * mx + 2 * my + mc
    elif group == "chips":
        me = 2 * mx + my
    else:
        me = mc

    def peer(k):
        if group == "devices":
            return (mx ^ (k >> 2), my ^ ((k >> 1) & 1), mc ^ (k & 1))
        if group == "chips":
            return (mx ^ (k >> 1), my ^ (k & 1), mc)
        return (mx, my, mc ^ k)

    def copies():
        local, sends, recvs = [], [], []
        for i, (src, dst) in enumerate(zip(src_refs, dst_refs)):
            def part(k):
                return src.at[k] if kind == "scatter" else src

            def slab(k):
                return dst if kind == "send" else dst.at[k]

            if kind != "send":
                local.append(pltpu.make_async_copy(part(me), dst.at[me], local_sems.at[i]))
            for k in range(1, n):
                sem = dict(send_sem=send_sems.at[i, k - 1], recv_sem=recv_sems.at[i, k - 1], device_id_type=MESH)
                sends.append(pltpu.make_async_remote_copy(src_ref=part(me ^ k), dst_ref=slab(me), device_id=peer(k), **sem))
                recvs.append(pltpu.make_async_remote_copy(src_ref=part(me ^ k), dst_ref=slab(me ^ k),
                                                          device_id=(mx, my, mc), **sem))
        return local, sends, recvs

    def start():
        local, sends, _ = copies()
        for cp in local + sends:
            cp.start()

    def wait():
        local, sends, recvs = copies()
        for cp in recvs:
            cp.wait_recv()
        for cp in sends:
            cp.wait_send()
        for cp in local:
            cp.wait()

    return start, wait


def _exchange_scratch(group, n):
    k = GROUP_SIZE[group] - 1
    return [pltpu.SemaphoreType.DMA((n, k)), pltpu.SemaphoreType.DMA((n, k)), pltpu.SemaphoreType.DMA((n,))]


def _exchange_shapes(kind, group, arrs):
    return [_sds(((GROUP_SIZE[group],) + a.shape) if kind == "gather" else a.shape, a.dtype) for a in arrs]


def exchange(name, parts):
    counts = [len(arrs) for _, _, arrs in parts]
    total = sum(counts)

    def body(*refs):
        srcs, dsts, sems = refs[:total], refs[total:2 * total], refs[2 * total:]
        ops, at = [], 0
        for j, (kind, group, arrs) in enumerate(parts):
            ops.append(_peer_exchange(kind, group, srcs[at:at + counts[j]], dsts[at:at + counts[j]], *sems[3 * j:3 * j + 3]))
            at += counts[j]
        for start, _ in ops:
            start()
        for _, wait in ops:
            wait()

    any_ = pl.BlockSpec(memory_space=pl.ANY)
    flat = [a for _, _, arrs in parts for a in arrs]
    outs = pl.pallas_call(
        body, name=name, out_shape=[sh for kind, group, arrs in parts for sh in _exchange_shapes(kind, group, arrs)],
        in_specs=[any_] * total, out_specs=[any_] * total,
        scratch_shapes=[sc for _, group, arrs in parts for sc in _exchange_scratch(group, len(arrs))],
    )(*flat)
    res, at = [], 0
    for cnt in counts:
        res.append(list(outs[at:at + cnt]))
        at += cnt
    return res


def gather8(x):
    return exchange("gather8", [("gather", "devices", [x])])[0][0]


PACK_ROWS = 64
SMALL = ("c_ctx", "b_mod", "g_pre", "g_post", "ret_norm_g", "sg_w", "sg_b", "sc_conv_w", "gdn_conv_w",
         "gdn_a_log", "gdn_dt_bias", "gdn_norm_g")


def _pack(arrs, width=D, mult=PACK_ROWS):
    rows = []
    for a in arrs:
        flat = a.reshape(-1)
        pad = (-flat.shape[0]) % width
        rows.append(jnp.pad(flat, (0, pad)).reshape(-1, width))
    out = jnp.concatenate(rows, axis=0)
    return jnp.pad(out, ((0, (-out.shape[0]) % mult), (0, 0)))


def _unpack(packed, shapes, width=D):
    outs, r = [], 0
    for shp in shapes:
        size = int(np.prod(shp))
        nr = -(-size // width)
        outs.append(packed[r:r + nr].reshape(-1)[:size].reshape(shp))
        r += nr
    return outs


def kernel(x, c, ctx, c_ctx, w_mod, b_mod, g_pre, g_post, w_in, w_out, ret_norm_g, sg_w, sg_b, sc_conv_w, gdn_conv_w, gdn_a_log, gdn_dt_bias, gdn_norm_g, loss_target, m_c_ctx, m_w_mod, m_b_mod, m_g_pre, m_g_post, m_w_in, m_w_out, m_ret_norm_g, m_sg_w, m_sg_b, m_sc_conv_w, m_gdn_conv_w, m_gdn_a_log, m_gdn_dt_bias, m_gdn_norm_g, v_c_ctx, v_w_mod, v_b_mod, v_g_pre, v_g_post, v_w_in, v_w_out, v_ret_norm_g, v_sg_w, v_sg_b, v_sc_conv_w, v_gdn_conv_w, v_gdn_a_log, v_gdn_dt_bias, v_gdn_norm_g):
    weights = dict(c_ctx=c_ctx, w_mod=w_mod, b_mod=b_mod, g_pre=g_pre, g_post=g_post, w_in=w_in, w_out=w_out,
                   ret_norm_g=ret_norm_g, sg_w=sg_w, sg_b=sg_b, sc_conv_w=sc_conv_w, gdn_conv_w=gdn_conv_w,
                   gdn_a_log=gdn_a_log, gdn_dt_bias=gdn_dt_bias, gdn_norm_g=gdn_norm_g)
    mom = dict(c_ctx=m_c_ctx, w_mod=m_w_mod, b_mod=m_b_mod, g_pre=m_g_pre, g_post=m_g_post, w_in=m_w_in,
               w_out=m_w_out, ret_norm_g=m_ret_norm_g, sg_w=m_sg_w, sg_b=m_sg_b, sc_conv_w=m_sc_conv_w,
               gdn_conv_w=m_gdn_conv_w, gdn_a_log=m_gdn_a_log, gdn_dt_bias=m_gdn_dt_bias, gdn_norm_g=m_gdn_norm_g)
    var = dict(c_ctx=v_c_ctx, w_mod=v_w_mod, b_mod=v_b_mod, g_pre=v_g_pre, g_post=v_g_post, w_in=v_w_in,
               w_out=v_w_out, ret_norm_g=v_ret_norm_g, sg_w=v_sg_w, sg_b=v_sg_b, sc_conv_w=v_sc_conv_w,
               gdn_conv_w=v_gdn_conv_w, gdn_a_log=v_gdn_a_log, gdn_dt_bias=v_gdn_dt_bias, gdn_norm_g=v_gdn_norm_g)

    nb, t_lat, _ = x.shape
    t_ctx = ctx.shape[1]
    s = t_ctx + t_lat
    n = nb * s
    sb = s // TM
    nl = w_in.shape[0]
    wc_in = w_in.shape[2]
    wc_mod = w_mod.shape[2]
    rows_out = w_out.shape[1]
    n_all = nb * N_DEV
    mx, my, mc = _my_pos()
    chip = 2 * mx + my
    dev = 2 * chip + mc

    sg_c = _sg_consts()
    bd = jnp.asarray(_block_diag())
    ret_c = _ret_consts(nb)
    gdn_c = _gdn_consts(nb)
    cos, sins = _rope_tables(t_lat, t_ctx)

    w_in_b, w_out_b = w_in.astype(bf16), w_out.astype(bf16)
    pre = _pack([c, sc_conv_w, gdn_conv_w], mult=8)
    (pre_all,), w0_parts = exchange("startup_gather", [("gather", "devices", [pre]), ("gather", "chips", [w_in_b[0]])])
    c_parts, scw_parts, gcw_parts = [], [], []
    for k in range(N_DEV):
        ck, sk, gk = _unpack(pre_all[k], [c.shape, sc_conv_w.shape, gdn_conv_w.shape])
        c_parts.append(ck)
        if k % 2 == 0:
            scw_parts.append(sk)
            gcw_parts.append(gk)
    c_all = jnp.concatenate(c_parts, axis=0)
    sc_w_full = jnp.concatenate(scw_parts, axis=-1)
    gdn_w_full = jnp.concatenate(gcw_parts, axis=-1)
    c_rows = jnp.concatenate([c_all, c_ctx[None, :], jnp.zeros((7, D), f32)], axis=0)

    b_cols = lax.dynamic_slice_in_dim(b_mod, chip * wc_mod, wc_mod, axis=1)[:, None, :]
    mod_part = mod_fwd(c_rows, w_mod, b_cols)
    mod_all = gather8(mod_part)
    mod = jnp.concatenate([mod_all[2 * k] for k in range(N_CHIPS)], axis=-1)
    my_rows = jnp.concatenate([lax.dynamic_slice_in_dim(mod, dev * nb, nb, axis=1), mod[:, n_all:n_all + 1]], axis=1)
    shift_t = my_rows[:, :, None, 0:D]
    scale_t = my_rows[:, :, None, D:2 * D]
    gate_t = my_rows[:, :, None, 2 * D:3 * D]

    w_in_full, w_out_full = [None] * nl, [None] * nl
    w_in_full[0] = place_weights(w0_parts[0])

    alog = jnp.pad(gdn_a_log.reshape(nl, 1, 8), ((0, 0), (0, 0), (0, 120)))
    dtb = jnp.pad(gdn_dt_bias.reshape(nl, 1, 8), ((0, 0), (0, 0), (0, 120)))
    gdn_ng = jnp.tile(gdn_norm_g, (1, NH))[:, None, :]
    ret_ng = ret_norm_g[:, None, :]

    xs = jnp.concatenate([ctx, x], axis=1).reshape(n, D)
    saved = []
    for l in range(nl):
        p, h = inproj_fwd(xs, shift_t[l], scale_t[l], g_pre[l][None, :], w_in_full[l], nb, sb)
        p3 = p.reshape(nb, s, PW)
        ro_f, ro_b, rs_all = ret_scan_fwd(p3, cos, sins, ret_c, t_ctx)
        y_ret = mix_finish_fwd(_ret_finish, "ret_finish_fwd", ro_f, ro_b, p3, 3, ret_ng[l], bd)
        y_sg = sg_fwd(p3, sg_w[l], sg_b[l], *sg_c)
        y_sc = sc_fwd(p3, sc_w_full[l], t_ctx)
        c3 = gdn_conv_fwd(p3, gdn_w_full[l], t_ctx)
        riding = [w_out_b[l]] + ([w_in_b[l + 1]] if l + 1 < nl else [])
        go_f, go_b, *gs_all = gdn_scan_fwd(c3, p3, alog[l], dtb[l], gdn_c, t_ctx, ("gather", riding))
        w_out_full[l] = gs_all[2].reshape(D, D)
        if l + 1 < nl:
            w_in_full[l + 1] = place_weights(gs_all[3])
        gs_all = gs_all[:2]
        y_gdn = mix_finish_fwd(_gdn_finish, "gdn_finish_fwd", go_f, go_b, p3, 14, gdn_ng[l], bd)
        ys = [a.reshape(n, BW) for a in (y_ret, y_sg, y_sc, y_gdn)]
        x_new, o = outproj_fwd(ys, w_out_full[l], xs, gate_t[l], g_post[l][None, :], nb, sb)
        saved.append(dict(x=xs, h=h, p3=p3, ro=(ro_f, ro_b), rs=rs_all, c=c3, go=(go_f, go_b), gs=gs_all,
                          ys=ys, o=o))
        xs = x_new

    dx3, loss_part = loss_head(xs.reshape(nb, s, D), loss_target, t_ctx)
    loss = lax.psum(loss_part[0, 0], ("x", "y", "c"))

    dxs = dx3.reshape(n, D)
    g_small = {k: [None] * nl for k in SMALL if k not in ("c_ctx", "b_mod")}
    dm_rows = [None] * nl
    slab_in = None
    got_in, got_out = [None] * nl, [None] * nl
    for l in reversed(range(nl)):
        sv = saved[l]
        p3 = sv["p3"]
        dy, gw_out, dg_post, dgate = outproj_bwd(dxs, sv["o"], gate_t[l], g_post[l][None, :], sv["ys"], w_out_full[l], nb, sb)
        dy3 = dy.reshape(nb, s, D)
        r_do, r_dz, d_rng = mix_finish_bwd(_ret_finish, "ret_finish_bwd", *sv["ro"], p3, 3, ret_ng[l], bd, dy3, 0)
        r_d = ret_scan_bwd(p3, cos, sins, ret_c, sv["rs"], r_do, t_ctx)
        s_du, s_dv, s_dz, d_sgw, d_sgb = sg_bwd(p3, sg_w[l], sg_b[l], *sg_c, dy3)
        c_db, c_dc, c_dh, c_dz, d_scw = sc_bwd(p3, sc_w_full[l], dy3, t_ctx)
        g_do, g_dz, d_gng = mix_finish_bwd(_gdn_finish, "gdn_finish_bwd", *sv["go"], p3, 14, gdn_ng[l], bd, dy3, 3)
        riding = [gw_out.reshape(N_CHIPS, rows_out, D).astype(bf16)] + ([] if slab_in is None else [slab_in])
        g_dcf, g_dgf, g_dcb, g_dgb, g_dal, g_ddt, *got = gdn_scan_bwd(
            sv["c"], p3, alog[l], dtb[l], gdn_c, *sv["gs"], g_do, t_ctx, ("scatter", riding))
        got_out[l] = got[0]
        if slab_in is not None:
            got_in[l + 1] = got[1]
        gx, d_gcw = gdn_conv_bwd(p3, gdn_w_full[l], g_dcf, g_dcb, t_ctx)
        dp3 = assemble_dp([(r_d[0], r_d[3]), (r_d[1], r_d[4]), (r_d[2], r_d[5])],
                          [r_dz, s_du, s_dv, s_dz, c_db, c_dc, c_dh, c_dz], [gx], [g_dz], [g_dgf, g_dgb])
        dp = dp3.reshape(n, PW)
        gw_in = dw_in(sv["h"], dp)
        slab_in = jnp.stack([gw_in[k * wc_in:(k + 1) * wc_in] for k in range(N_CHIPS)])
        dxs, dg_pre, dshift, dscale, *got = inproj_bwd_x(dp, w_in_full[l], sv["x"], scale_t[l], g_pre[l][None, :], dxs, nb, sb,
                                                         ("scatter", [slab_in]) if l == 0 else None)
        if l == 0:
            got_in[0] = got[0]
        g_small["g_pre"][l] = dg_pre[0]
        g_small["g_post"][l] = dg_post[0]
        g_small["ret_norm_g"][l] = d_rng[0]
        g_small["sg_w"][l] = d_sgw
        g_small["sg_b"][l] = d_sgb
        g_small["sc_conv_w"][l] = d_scw
        g_small["gdn_conv_w"][l] = d_gcw
        g_small["gdn_a_log"][l] = g_dal[0, :8].reshape(2, NH)
        g_small["gdn_dt_bias"][l] = g_ddt[0, :8].reshape(2, NH)
        g_small["gdn_norm_g"][l] = d_gng[0].reshape(NH, HD)
        dm_rows[l] = jnp.concatenate([dshift, dscale, dgate], axis=-1)[:nb + 1]
    grad_x = dxs.reshape(nb, s, D)[:, t_ctx:, :]

    g_small = {k: jnp.stack(v) for k, v in g_small.items()}
    dm_rows = jnp.stack(dm_rows)
    names2 = [k for k in SMALL if k not in ("c_ctx", "b_mod")]
    pack_sum = _pack([g_small[k] for k in names2] + [dm_rows[:, nb:]])
    pack_own = _pack([dm_rows[:, :nb]], mult=8)
    rs = -(-pack_sum.shape[0] // (8 * N_DEV)) * 8
    slabs_sum = jnp.pad(pack_sum, ((0, N_DEV * rs - pack_sum.shape[0]), (0, 0))).reshape(N_DEV, rs, D)
    ((got_small,),) = exchange("tail_scatter", [("scatter", "devices", [slabs_sum])])
    my_slab = sum_lead(got_small, tr=rs)
    gin_mine = jnp.stack([sum_lead(a, tr=wc_in, tc=256) for a in got_in], axis=1)
    gout_mine = jnp.stack([sum_lead(a) for a in got_out])
    (all2,), (gin_sib, gout_sib) = exchange("tail_gather", [
        ("gather", "devices", [jnp.concatenate([my_slab, pack_own], axis=0)]), ("send", "cores", [gin_mine, gout_mine])])
    tot2 = all2[:, :rs].reshape(N_DEV * rs, D)
    outs2 = _unpack(tot2, [g_small[k].shape for k in names2] + [(nl, 1, 3 * D)])
    grads = dict(zip(names2, outs2[:-1]))
    dm_own = jnp.stack([_unpack(all2[k, rs:], [(nl, nb, 3 * D)])[0] for k in range(N_DEV)])
    dm_own = jnp.transpose(dm_own, (1, 0, 2, 3)).reshape(nl, n_all, 3 * D)
    dm_all = jnp.concatenate([dm_own, jnp.pad(outs2[-1], ((0, 0), (0, 7), (0, 0)))], axis=1)
    grads["gdn_norm_g"] = sum_lead(jnp.transpose(grads["gdn_norm_g"], (1, 0, 2)), tr=nl)
    for k in ("sc_conv_w", "gdn_conv_w"):
        wc = weights[k].shape[2]
        grads[k] = lax.dynamic_slice_in_dim(grads[k], chip * wc, wc, axis=2)

    dm_cols = lax.dynamic_slice_in_dim(dm_all, chip * wc_mod, wc_mod, axis=2)
    g_w_mod, g_b_mod, dcc_part = mod_bwd(c_rows, w_mod, dm_cols, dm_all)
    grads["b_mod"] = g_b_mod[:, 0, :]
    grads["c_ctx"] = cctx_grad(gather8(dcc_part), c_ctx[None, :])[0]

    res = {}
    w_in_t, m_in_t, v_in_t = [jnp.transpose(a, (2, 0, 1)) for a in (w_in, m_w_in, v_w_in)]
    res["w_in"] = [jnp.transpose(a, (1, 2, 0)) for a in
                   adamw(w_in_t, m_in_t, v_in_t, gin_mine, gin_sib, block=(wc_in // 4, nl, 256))]
    res["w_out"] = adamw(w_out, m_w_out, v_w_out, gout_mine, gout_sib)
    res["w_mod"] = adamw(w_mod, m_w_mod, v_w_mod, g_w_mod)
    shapes = [weights[k].shape for k in SMALL]
    small = adamw(_pack([weights[k] for k in SMALL]), _pack([mom[k] for k in SMALL]), _pack([var[k] for k in SMALL]),
                  _pack([grads[k].reshape(weights[k].shape) for k in SMALL]), tr=PACK_ROWS)
    small = [_unpack(a, shapes) for a in small]
    for i, k in enumerate(SMALL):
        res[k] = [small[j][i] for j in range(4)]

    order = ["c_ctx", "w_mod", "b_mod", "g_pre", "g_post", "w_in", "w_out", "ret_norm_g", "sg_w", "sg_b", "sc_conv_w",
             "gdn_conv_w", "gdn_a_log", "gdn_dt_bias", "gdn_norm_g"]
    return (loss, grad_x, *[res[k][0] for k in order], *[res[k][1] for k in order], *[res[k][2] for k in order],
            *[res[k][3] for k in order])
```

```python
import functools

import jax
import jax.numpy as jnp
import numpy as np
from jax import lax
from jax.experimental import pallas as pl
from jax.experimental.pallas import tpu as pltpu

f32 = jnp.float32
bf16 = jnp.bfloat16
HI = lax.Precision.HIGHEST
P3 = lax.Precision.HIGH
MESH = pl.DeviceIdType.MESH

EPS = 1e-6
D = 1024
NH = 4
HD = 64
BW = NH * HD
PAIR_W = 2 * HD
RC = 128
GC = 64
GRID_W = 64
ROPE_BASE = 10000.0
IN_W = 15 * BW + 16
PW = 4096
GATE_COL = 15 * BW
N_CHIPS = 4
N_DEV = 8
TM = 256
TP = 2 * TM
ADAM_LR, ADAM_B1, ADAM_B2, ADAM_EPS, ADAM_WD, ADAM_STEP = 0.001, 0.9, 0.999, 1e-08, 0.01, 10
LANE_HEAD = np.arange(BW) // HD
VMEM_BIG = 56 * 1024 * 1024


def _dot(a, b, precision=None):
    return jnp.dot(a, b, precision=precision, preferred_element_type=f32)


def _dot_nt(a, b, precision=None):
    return lax.dot_general(a, b, (((1,), (1,)), ((), ())), precision=precision, preferred_element_type=f32)


def _dot_tn(a, b, precision=None):
    return lax.dot_general(a, b, (((0,), (0,)), ((), ())), precision=precision, preferred_element_type=f32)


def _sds(shape, dtype=f32):
    return jax.ShapeDtypeStruct(shape, dtype)


def _cparams(sem=None, vmem=None):
    kw = {}
    if sem is not None:
        kw["dimension_semantics"] = sem
    if vmem is not None:
        kw["vmem_limit_bytes"] = vmem
    return pltpu.CompilerParams(**kw)


def _full(shape):
    n = len(shape)
    return pl.BlockSpec(shape, lambda *_: (0,) * n)


def _block_diag():
    return (LANE_HEAD[:, None] == LANE_HEAD[None, :]).astype(np.float32)


def _tau(c, d):
    return np.arange(c) if d == 0 else c - 1 - np.arange(c)


def _ret_consts(nb):
    lg = np.log(1.0 - 2.0 ** (-5.0 - np.arange(NH)))
    intra = np.zeros((2, 2, RC, 2 * RC)); qdec = np.zeros((2, 2, RC, PAIR_W)); kdec = np.zeros((2, 2, RC, PAIR_W))
    cd = np.zeros((2, 2, PAIR_W, PAIR_W))
    for d in range(2):
        t = _tau(RC, d)
        diff = t[:, None] - t[None, :]
        for p in range(2):
            lane_lg = lg[2 * p + np.arange(PAIR_W) // HD]
            for h in range(2):
                intra[d, p, :, h * RC:(h + 1) * RC] = np.where(diff >= 0, np.exp(np.maximum(diff, 0) * lg[2 * p + h]), 0.0)
            qdec[d, p] = np.exp((t[:, None] + 1.0) * lane_lg[None, :])
            kdec[d, p] = np.exp((RC - 1.0 - t[:, None]) * lane_lg[None, :])
            cd[d, p] = np.exp(RC * lane_lg)[:, None] * np.ones((1, PAIR_W))
    per_z = [np.tile(a.reshape((4,) + a.shape[2:]), (nb, 1, 1)) for a in (intra, qdec, kdec, cd)]
    bd2 = (np.arange(PAIR_W)[:, None] // HD == np.arange(PAIR_W)[None, :] // HD)
    bdr = (np.arange(2 * RC)[:, None] // RC == np.arange(PAIR_W)[None, :] // HD)
    return [jnp.asarray(a, f32) for a in per_z + [bd2, bdr]]


def _rope_tables(t_lat, t_ctx):
    nf = HD // 4
    inv = ROPE_BASE ** (-np.arange(nf) / nf)
    pos = np.arange(t_lat)
    ang_r = (pos // GRID_W)[:, None] * inv[None, :]
    ang_c = (pos % GRID_W)[:, None] * inv[None, :]
    ang = np.concatenate([ang_r, ang_r, ang_c, ang_c], axis=1)
    sign = np.concatenate([-np.ones(nf), np.ones(nf), -np.ones(nf), np.ones(nf)])
    cos = np.tile(np.cos(ang), (1, 2)); sins = np.tile(np.sin(ang) * sign, (1, 2))
    cos = np.concatenate([np.ones((t_ctx, PAIR_W)), cos]); sins = np.concatenate([np.zeros((t_ctx, PAIR_W)), sins])
    return jnp.asarray(cos, f32), jnp.asarray(sins, f32)


def _gdn_consts(nb):
    tmask = np.zeros((2, 2, GC, GC)); tmask2 = np.zeros((2, 2, GC, PAIR_W)); strict2 = np.zeros((2, 2, GC, PAIR_W))
    exp_g = np.zeros((2, 2, 128, PAIR_W)); exp_b = np.zeros((2, 2, 128, PAIR_W))
    for d in range(2):
        t = _tau(GC, d)
        tmask[d, :] = (t[:, None] >= t[None, :])
        tmask2[d, :] = np.tile(t[:, None] >= t[None, :], (1, 2))
        strict2[d, :] = np.tile(t[:, None] > t[None, :], (1, 2))
        for h in range(NH):
            exp_g[d, h // 2, 4 * d + h, (h % 2) * HD:(h % 2 + 1) * HD] = 1.0
            exp_b[d, h // 2, 8 + 4 * d + h, (h % 2) * HD:(h % 2 + 1) * HD] = 1.0
    exp_gt = np.transpose(exp_g, (0, 1, 3, 2))
    per_z = [np.tile(a.reshape((4,) + a.shape[2:]), (nb, 1, 1)) for a in (tmask, tmask2, strict2, exp_g, exp_b, exp_gt)]
    dsel2 = np.tile(np.eye(GC), (1, 2))
    eye2 = np.tile(np.eye(GC), (1, 2))
    bd2 = (np.arange(PAIR_W)[:, None] // HD == np.arange(PAIR_W)[None, :] // HD)
    return [jnp.asarray(a, f32) for a in per_z + [dsel2, eye2, bd2]]


def _swap16(x):
    lane = lax.broadcasted_iota(jnp.int32, x.shape, x.ndim - 1)
    n = x.shape[-1]
    return jnp.where(lane % 32 < 16, pltpu.roll(x, n - 16, axis=x.ndim - 1), pltpu.roll(x, 16, axis=x.ndim - 1))


@jax.custom_vjp
def _rot(x, cos, sins):
    return x * cos + _swap16(x) * sins


def _rot_fwd(x, cos, sins):
    return _rot(x, cos, sins), (cos, sins)


def _rot_bwd(res, g):
    cos, sins = res
    return g * cos + _swap16(g * sins), jnp.zeros_like(cos), jnp.zeros_like(sins)


_rot.defvjp(_rot_fwd, _rot_bwd)


def _silu(z):
    return z * jax.nn.sigmoid(z)


def _head_sum(x, bd):
    return _sel_r(x, bd)


def _ret_step(s, q, k, v, cos, sins, intra, qdec, kdec, cd, bd2, bdr):
    def bdiag(x):
        return jnp.concatenate([x, x], axis=1) * bdr

    qr = _rot(q, cos, sins)
    kr = _rot(k, cos, sins) * (HD ** -0.5)
    sc = _bmm_nt(qr, bdiag(kr)) * intra
    o = _bmm(qr * qdec, s) + _bmm(sc, bdiag(v))
    s_new = s * cd + bd2 * _bmm_tn(kr * kdec, v)
    return s_new, o


def _ret_finish(o_f, o_b, z, norm_g, bd):
    o = o_f + o_b
    mu = _head_sum(o, bd) * (1.0 / HD)
    xc = o - mu
    var = _head_sum(xc * xc, bd) * (1.0 / HD)
    return xc * lax.rsqrt(var + EPS) * norm_g * _silu(z)


def _softplus(x):
    return jnp.maximum(x, 0.0) + jnp.log(1.0 + jnp.exp(-jnp.abs(x)))


def _bmm(a, b, precision=None):
    return lax.dot_general(a, b, (((2,), (1,)), ((0,), (0,))), precision=precision, preferred_element_type=f32)


def _bmm_nt(a, b, precision=None):
    return lax.dot_general(a, b, (((2,), (2,)), ((0,), (0,))), precision=precision, preferred_element_type=f32)


def _bmm_tn(a, b, precision=None):
    return lax.dot_general(a, b, (((1,), (1,)), ((0,), (0,))), precision=precision, preferred_element_type=f32)


def _mm(a, b, mode):
    ca, cb = {"nn": (1, 0), "nt": (1, 1), "tn": (0, 0)}[mode]
    if a.ndim == 3:
        dims = (((ca + 1,), (cb + 1,)), ((0,), (0,)))
    else:
        dims = (((ca,), (cb,)), ((), ()))
    return lax.dot_general(a, b, dims, preferred_element_type=f32)


def _split(a):
    hi = a.astype(bf16)
    return hi, (a - hi.astype(f32)).astype(bf16)


def _sel2(a, e, mode, e_left):
    hi, lo = _split(a)
    eb = e.astype(bf16)
    if e_left:
        return _mm(eb, hi, mode) + _mm(eb, lo, mode)
    return _mm(hi, eb, mode) + _mm(lo, eb, mode)


@jax.custom_vjp
def _sel_r(a, e):
    return _sel2(a, e, "nn", False)


_sel_r.defvjp(lambda a, e: (_sel_r(a, e), e), lambda e, g: (_sel2(g, e, "nt", False), jnp.zeros_like(e)))


@jax.custom_vjp
def _sel_l(e, b):
    return _sel2(b, e, "nn", True)


_sel_l.defvjp(lambda e, b: (_sel_l(e, b), e), lambda e, g: (jnp.zeros_like(e), _sel2(g, e, "tn", True)))


def _bdiag(x, bd2):
    return jnp.concatenate([x, x], axis=1) * bd2


@jax.custom_vjp
def _solve_given_inv(m, vb, kbg, inv, bd2):
    w_ = vb.shape[-1]
    uw = _bmm(inv, jnp.concatenate([_bdiag(vb, bd2), _bdiag(kbg, bd2)], axis=2), P3)
    return uw[:, :, :w_], uw[:, :, w_:]


def _solve_fwd(m, vb, kbg, inv, bd2):
    u, w = _solve_given_inv(m, vb, kbg, inv, bd2)
    return (u, w), (inv, u, w, bd2)


def _solve_bwd(res, cts):
    inv, u, w, bd2 = res
    du, dw = cts
    c = inv.shape[1]
    t = jnp.swapaxes(_bdiag(inv, bd2), 1, 2)
    inv_t = t[:, :c] + t[:, c:]
    w_ = du.shape[-1]
    both = _bmm(inv_t, jnp.concatenate([_bdiag(du, bd2), _bdiag(dw, bd2)], axis=2), P3)
    dm = _bmm_nt(both, jnp.concatenate([_bdiag(u, bd2), _bdiag(w, bd2)], axis=2), P3)
    return dm, both[:, :, :w_], both[:, :, w_:], jnp.zeros_like(inv), jnp.zeros_like(bd2)


_solve_given_inv.defvjp(_solve_fwd, _solve_bwd)


def _gdn_step(s, q, k, v, gate, alog, dtb, tmask, tmask2, strict2, exp_g, exp_b, exp_gt, dsel2, eye2, bd2, inv=None):
    z, c, w_ = q.shape
    ne = gate.shape[0]

    def per_pair(a):
        return jnp.broadcast_to(a[:, None], (ne, z // ne) + a.shape[1:]).reshape((z,) + a.shape[1:])

    def rows(a):
        return a.reshape(z * c, w_)

    def bdiag(x):
        return _bdiag(x, bd2)

    g = per_pair(-jnp.exp(alog) * _softplus(gate + dtb))
    beta = per_pair(jax.nn.sigmoid(gate))
    gl = _sel_r(g, exp_g)
    gc_l = _sel_l(tmask, gl)
    glast_l = jnp.sum(gl, axis=1, keepdims=True)
    glast = jnp.sum(g, axis=1, keepdims=True)
    beta_l = _sel_r(beta, exp_b)
    gc_r = jnp.sum(gc_l * dsel2, axis=1, keepdims=True)
    qn = q * lax.rsqrt(_sel_r(rows(q * q), bd2).reshape(z, c, w_) + EPS)
    kn = k * lax.rsqrt(_sel_r(rows(k * k), bd2).reshape(z, c, w_) + EPS)
    eg = jnp.exp(gc_l)
    kb = kn * beta_l
    vb = v * beta_l
    kbg = kb * eg
    qs = qn * (HD ** -0.5)
    dec = jnp.exp(jnp.where(tmask2 > 0, gc_l - gc_r, -1e30))
    kns = bdiag(kn)
    m = -(_bmm_nt(kb, kns) * dec * strict2)
    if inv is None:
        inv = eye2 + m
        p = _bmm(m, bdiag(m), P3)
        for _ in range(4):
            both = _bmm(jnp.concatenate([p, inv], axis=1), bdiag(p), P3)
            inv = inv + both[:, c:]
            p = both[:, :c]
        inv = inv + _bmm(inv, bdiag(p), P3)
        uw = _bmm(inv, jnp.concatenate([bdiag(vb), bdiag(kbg)], axis=2), P3)
        u, w = uw[:, :, :w_], uw[:, :, w_:]
    else:
        u, w = _solve_given_inv(m, vb, kbg, inv, bd2)
    v_new = u - _bmm(w, s)
    k_tail = kn * jnp.exp(glast_l - gc_l)
    cdec = jnp.sum(exp_gt * jnp.exp(glast), axis=-1, keepdims=True)
    s_new = s * cdec + bd2 * _bmm_tn(k_tail, v_new)
    a = _bmm_nt(qs, kns) * dec
    o = _bmm(qs * eg, s) + _bmm(a, bdiag(v_new))
    return s_new, o, inv


def _gdn_finish(o_f, o_b, z, norm_g, bd):
    o = o_f + o_b
    ms = _head_sum(o * o, bd) * (1.0 / HD)
    return o * lax.rsqrt(ms + EPS) * norm_g * _silu(z)


def _gelu(x):
    return 0.5 * x * (1.0 + jnp.tanh(0.7978845608028654 * (x + 0.044715 * (x * x * x))))


def _sg_block(u0, u1, v0, v1, z0, z1, w, b, hmp, bdr):
    ts = u0.shape[0]
    nc = ts // RC
    g0, g1 = _gelu(v0), _gelu(v1)
    mu = (jnp.sum(g0, axis=-1, keepdims=True) + jnp.sum(g1, axis=-1, keepdims=True)) * (1.0 / BW)
    x0, x1 = g0 - mu, g1 - mu
    var = (jnp.sum(x0 * x0, axis=-1, keepdims=True) + jnp.sum(x1 * x1, axis=-1, keepdims=True)) * (1.0 / BW)
    rstd = lax.rsqrt(var + EPS)
    ys = []
    for p, (u, xc, z) in enumerate(((u0, x0, z0), (u1, x1, z1))):
        vn = (xc * rstd).reshape(nc, RC, PAIR_W)
        wp = jnp.concatenate([w[2 * p], w[2 * p + 1]], axis=1)
        mix = _bmm(jnp.broadcast_to(wp, (nc, RC, 2 * RC)), jnp.concatenate([vn, vn], axis=1) * bdr)
        bias = _dot_tn(b, hmp[p], precision=HI)
        s = (mix + bias).reshape(ts, PAIR_W)
        ys.append(_gelu(u) * s * _silu(z))
    return ys[0], ys[1]


def _make_shifts(t_ctx, n):
    def dn(x):
        t = lax.broadcasted_iota(jnp.int32, x.shape, 0)
        return jnp.where((t != 0) & (t != t_ctx), pltpu.roll(x, 1, axis=0), 0.0)

    def up(x):
        t = lax.broadcasted_iota(jnp.int32, x.shape, 0)
        return jnp.where((t != t_ctx - 1) & (t != n - 1), pltpu.roll(x, n - 1, axis=0), 0.0)

    @jax.custom_vjp
    def shift_dn(x):
        return dn(x)
    shift_dn.defvjp(lambda x: (dn(x), None), lambda _, g: (up(g),))

    @jax.custom_vjp
    def shift_up(x):
        return up(x)
    shift_up.defvjp(lambda x: (up(x), None), lambda _, g: (dn(g),))
    return shift_dn, shift_up


def _conv3(x, w, shift_dn, shift_up):
    return shift_dn(x) * w[0:1] + x * w[1:2] + shift_up(x) * w[2:3]


def inproj_fwd(x, shift_t, scale_t, g_pre, w_in, n_batch, sb):
    n = x.shape[0]

    def sel(i):
        return jnp.where(i % sb == 0, n_batch, i // sb)

    def body(x_ref, sh0, sh1, sc0, sc1, g_ref, w_ref, p_ref, h_ref):
        hs = []
        for k, (sh_ref, sc_ref) in enumerate(((sh0, sc0), (sh1, sc1))):
            xv = x_ref[k * TM:(k + 1) * TM, :]
            r = xv * lax.rsqrt(jnp.mean(xv * xv, axis=-1, keepdims=True) + EPS)
            hs.append(((r * g_ref[...]) * (1.0 + sc_ref[0]) + sh_ref[0]).astype(bf16))
        hb = jnp.concatenate(hs, axis=0)
        h_ref[...] = hb
        p_ref[...] = _dot(hb, w_ref[...])

    def mrow(k):
        return pl.BlockSpec((1, 1, D), lambda i: (sel(2 * i + k), 0, 0))

    return pl.pallas_call(
        body, name="inproj_fwd", grid=(n // TP,),
        in_specs=[pl.BlockSpec((TP, D), lambda i: (i, 0)), mrow(0), mrow(1), mrow(0), mrow(1),
                  _full((1, D)), _full((D, PW))],
        out_specs=[pl.BlockSpec((TP, PW), lambda i: (i, 0)), pl.BlockSpec((TP, D), lambda i: (i, 0))],
        out_shape=[_sds((n, PW)), _sds((n, D), bf16)],
        compiler_params=_cparams(("arbitrary",), VMEM_BIG),
    )(x, shift_t, shift_t, scale_t, scale_t, g_pre, w_in)


def outproj_fwd(ys, w_out, x, gate_t, g_post, n_batch, sb):
    n = x.shape[0]

    def sel(i):
        return jnp.where(i % sb == 0, n_batch, i // sb)

    def body(y0, y1, y2, y3, w_ref, x_ref, gt0, gt1, g_ref, xn_ref, o_ref):
        y = jnp.concatenate([y0[...], y1[...], y2[...], y3[...]], axis=1)
        o = _dot(y, w_ref[...])
        o_ref[...] = o
        nrm = o * lax.rsqrt(jnp.mean(o * o, axis=-1, keepdims=True) + EPS) * g_ref[...]
        for k, gt_ref in enumerate((gt0, gt1)):
            rows = slice(k * TM, (k + 1) * TM)
            xn_ref[rows, :] = x_ref[rows, :] + gt_ref[0] * nrm[rows]

    def mrow(k):
        return pl.BlockSpec((1, 1, D), lambda i: (sel(2 * i + k), 0, 0))

    yspec = pl.BlockSpec((TP, BW), lambda i: (i, 0))
    return pl.pallas_call(
        body, name="outproj_fwd", grid=(n // TP,),
        in_specs=[yspec, yspec, yspec, yspec, _full((D, D)), pl.BlockSpec((TP, D), lambda i: (i, 0)),
                  mrow(0), mrow(1), _full((1, D))],
        out_specs=[pl.BlockSpec((TP, D), lambda i: (i, 0)), pl.BlockSpec((TP, D), lambda i: (i, 0))],
        out_shape=[_sds((n, D)), _sds((n, D))],
        compiler_params=_cparams(("arbitrary",), VMEM_BIG),
    )(*ys, w_out, x, gate_t, gate_t, g_post)


def _row_onehot(r):
    return lax.broadcasted_iota(jnp.int32, (8, 1), 0) == r


def outproj_bwd(dxn, o, gate_t, g_post, ys, w_out, n_batch, sb):
    n = dxn.shape[0]

    def sel(i):
        return jnp.where(i % sb == 0, n_batch, i // sb)

    def body(dxn_ref, o_ref, gt0, gt1, g_ref, y0, y1, y2, y3, w_ref, dy_ref, dw_ref, dg_ref, dgate_ref):
        i = pl.program_id(0)

        @pl.when(i == 0)
        def _():
            dw_ref[...] = jnp.zeros_like(dw_ref)
            dg_ref[...] = jnp.zeros_like(dg_ref)
            dgate_ref[...] = jnp.zeros_like(dgate_ref)

        g = g_ref[...]
        dos = []
        for k, gt_ref in enumerate((gt0, gt1)):
            rows = slice(k * TM, (k + 1) * TM)
            ov = o_ref[rows, :]
            rstd = lax.rsqrt(jnp.mean(ov * ov, axis=-1, keepdims=True) + EPS)
            r = ov * rstd
            dx = dxn_ref[rows, :]
            dgate_ref[...] += jnp.where(_row_onehot(sel(2 * i + k)), jnp.sum(dx * (r * g), axis=0, keepdims=True), 0.0)
            dn = dx * gt_ref[0]
            dg_ref[...] += jnp.sum(dn * r, axis=0, keepdims=True)
            dr = dn * g
            dos.append((rstd * (dr - r * jnp.mean(dr * r, axis=-1, keepdims=True))).astype(bf16))
        dob = jnp.concatenate(dos, axis=0)
        dy_ref[...] = _dot_nt(dob, w_ref[...])
        y = jnp.concatenate([y0[...], y1[...], y2[...], y3[...]], axis=1)
        dw_ref[...] += _dot_tn(y, dob)

    def mrow(k):
        return pl.BlockSpec((1, 1, D), lambda i: (sel(2 * i + k), 0, 0))

    yspec = pl.BlockSpec((TP, BW), lambda i: (i, 0))
    row = pl.BlockSpec((TP, D), lambda i: (i, 0))
    return pl.pallas_call(
        body, name="outproj_bwd", grid=(n // TP,),
        in_specs=[row, row, mrow(0), mrow(1), _full((1, D)), yspec, yspec, yspec, yspec, _full((D, D))],
        out_specs=[row, _full((D, D)), _full((1, D)), _full((8, D))],
        out_shape=[_sds((n, D)), _sds((D, D)), _sds((1, D)), _sds((8, D))],
        compiler_params=_cparams(("arbitrary",), VMEM_BIG),
    )(dxn, o, gate_t, gate_t, g_post, *ys, w_out)


def inproj_bwd_x(dp, w_in, x, scale_t, g_pre, dxn, n_batch, sb, xchg=None):
    n = x.shape[0]

    def sel(i):
        return jnp.where(i % sb == 0, n_batch, i // sb)

    def body(dp_ref, w_ref, x_ref, sc0, sc1, g_ref, dxn_ref, dx_ref, dg_ref, dsh_ref, dsc_ref):
        i = pl.program_id(0)

        @pl.when(i == 0)
        def _():
            dg_ref[...] = jnp.zeros_like(dg_ref)
            dsh_ref[...] = jnp.zeros_like(dsh_ref)
            dsc_ref[...] = jnp.zeros_like(dsc_ref)

        dh_all = _dot_nt(dp_ref[...], w_ref[...])
        g = g_ref[...]
        for k, sc_ref in enumerate((sc0, sc1)):
            rows = slice(k * TM, (k + 1) * TM)
            dh = dh_all[rows]
            xv = x_ref[rows, :]
            rstd = lax.rsqrt(jnp.mean(xv * xv, axis=-1, keepdims=True) + EPS)
            r = xv * rstd
            hot = _row_onehot(sel(2 * i + k))
            dsh_ref[...] += jnp.where(hot, jnp.sum(dh, axis=0, keepdims=True), 0.0)
            dsc_ref[...] += jnp.where(hot, jnp.sum(dh * (r * g), axis=0, keepdims=True), 0.0)
            t = dh * (1.0 + sc_ref[0])
            dg_ref[...] += jnp.sum(t * r, axis=0, keepdims=True)
            dr = t * g
            dx_ref[rows, :] = dxn_ref[rows, :] + rstd * (dr - r * jnp.mean(dr * r, axis=-1, keepdims=True))

    def mrow(k):
        return pl.BlockSpec((1, 1, D), lambda i: (sel(2 * i + k), 0, 0))

    row = pl.BlockSpec((TP, D), lambda i: (i, 0))
    fused, x_in, x_out, x_shape, x_scratch = _with_exchange(body, 7, 4, 0, xchg, n // TP)
    return pl.pallas_call(
        fused, name="inproj_bwd_x" + ("" if xchg is None else "_" + xchg[0]), grid=(n // TP,),
        in_specs=[pl.BlockSpec((TP, PW), lambda i: (i, 0)), _full((D, PW)), row, mrow(0), mrow(1), _full((1, D)), row]
                 + x_in,
        out_specs=[row, _full((1, D)), _full((8, D)), _full((8, D))] + x_out,
        out_shape=[_sds((n, D)), _sds((1, D)), _sds((8, D)), _sds((8, D))] + x_shape,
        scratch_shapes=x_scratch,
        compiler_params=_cparams(("arbitrary",), VMEM_BIG),
    )(dp, w_in, x, scale_t, scale_t, g_pre, dxn, *([] if xchg is None else xchg[1]))


def dw_in(h, dp):
    n = h.shape[0]
    tk, tn = (1536 if n % 1536 == 0 else 512), 1024
    nk = n // tk

    def body(h_ref, dp_ref, o_ref, acc):
        k = pl.program_id(1)

        @pl.when(k == 0)
        def _():
            acc[...] = jnp.zeros_like(acc)
        acc[...] += _dot_tn(dp_ref[...], h_ref[...])

        @pl.when(k == nk - 1)
        def _():
            o_ref[...] = acc[...].astype(bf16)

    return pl.pallas_call(
        body, name="dw_in", grid=(PW // tn, nk),
        in_specs=[pl.BlockSpec((tk, D), lambda j, k: (k, 0)), pl.BlockSpec((tk, tn), lambda j, k: (k, j))],
        out_specs=pl.BlockSpec((tn, D), lambda j, k: (j, 0)),
        out_shape=_sds((PW, D), bf16),
        scratch_shapes=[pltpu.VMEM((tn, D), f32)],
        compiler_params=_cparams(("parallel", "arbitrary"), VMEM_BIG),
    )(h, dp)


def place_weights(slabs):
    n_ch, d, wc = slabs.shape

    def body(w_ref, o_ref):
        acc = jnp.pad(w_ref[0].astype(f32), ((0, 0), (0, PW - wc)))
        for k in range(1, n_ch):
            acc = acc + pltpu.roll(jnp.pad(w_ref[k].astype(f32), ((0, 0), (0, PW - wc))), wc * k, axis=1)
        o_ref[...] = acc.astype(bf16)

    return pl.pallas_call(
        body, name="place_weights", grid=(d // TM,),
        in_specs=[pl.BlockSpec((n_ch, TM, wc), lambda i: (0, i, 0))],
        out_specs=pl.BlockSpec((TM, PW), lambda i: (i, 0)),
        out_shape=_sds((d, PW), bf16),
        compiler_params=_cparams(("arbitrary",), VMEM_BIG),
    )(slabs)


def loss_head(xf, target, t_ctx):
    nb, s, _ = xf.shape
    jc = t_ctx // TM

    def body(x_ref, t_ref, dx_ref, l_ref):
        b, j = pl.program_id(0), pl.program_id(1)

        @pl.when((b == 0) & (j == 0))
        def _():
            l_ref[...] = jnp.zeros_like(l_ref)

        @pl.when(j < jc)
        def _():
            dx_ref[...] = jnp.zeros_like(dx_ref)

        @pl.when(j >= jc)
        def _():
            diff = x_ref[0] - t_ref[0]
            dx_ref[0] = diff * (1.0 / D)
            l_ref[...] += 0.5 * jnp.sum(diff * diff) * (1.0 / D)

    return pl.pallas_call(
        body, name="loss_head", grid=(nb, s // TM),
        in_specs=[pl.BlockSpec((1, TM, D), lambda b, j: (b, j, 0)),
                  pl.BlockSpec((1, TM, D), lambda b, j: (b, jnp.maximum(j - jc, 0), 0))],
        out_specs=[pl.BlockSpec((1, TM, D), lambda b, j: (b, j, 0)), _full((1, 128))],
        out_shape=[_sds((nb, s, D)), _sds((1, 128))],
        compiler_params=_cparams(("arbitrary", "arbitrary")),
    )(xf, target)


def _chunk_maps(n_ctx, n_lat):
    n = n_ctx + n_lat

    def cf(t):
        return t

    def cb(t):
        return jnp.where(t < n_ctx, n_ctx - 1 - t, n - 1 - t + n_ctx)
    return n, cf, cb


def ret_scan_fwd(p3, cos, sins, consts, t_ctx):
    nb, s, _ = p3.shape
    n, cf, cb = _chunk_maps(t_ctx // RC, (s - t_ctx) // RC)
    nz = 4 * nb

    def body(qf, kf, vf, qb, kb, vb, cosf, sinf, cosb, sinb, intra_r, qdec_r, kdec_r, cd_r, bd_r, bdr_r,
             of_ref, ob_ref, sall_ref, s_sc):
        @pl.when(pl.program_id(0) == 0)
        def _():
            s_sc[...] = jnp.zeros_like(s_sc)
        st = s_sc[...]
        sall_ref[0] = st
        s_new, o = _ret_step(st, _pairs(qf, qb, nb), _pairs(kf, kb, nb), _pairs(vf, vb, nb),
                             _pair_tables(cosf, cosb, nb), _pair_tables(sinf, sinb, nb), intra_r[...], qdec_r[...],
                             kdec_r[...], cd_r[...], bd_r[...], bdr_r[...])
        s_sc[...] = s_new
        _unpairs(o, of_ref, ob_ref, nb)

    def pspec(m, seg):
        return pl.BlockSpec((nb, RC, BW), lambda t: (0, m(t), seg))

    def tspec(m):
        return pl.BlockSpec((RC, PAIR_W), lambda t: (m(t), 0))

    return pl.pallas_call(
        body, name="ret_scan_fwd", grid=(n,),
        in_specs=[pspec(cf, 0), pspec(cf, 1), pspec(cf, 2), pspec(cb, 0), pspec(cb, 1), pspec(cb, 2),
                  tspec(cf), tspec(cf), tspec(cb), tspec(cb)] + [_full(c.shape) for c in consts],
        out_specs=[pl.BlockSpec((nb, RC, BW), lambda t: (0, cf(t), 0)),
                   pl.BlockSpec((nb, RC, BW), lambda t: (0, cb(t), 0)),
                   pl.BlockSpec((1, nz, PAIR_W, PAIR_W), lambda t: (t, 0, 0, 0))],
        out_shape=[_sds((nb, s, BW)), _sds((nb, s, BW)), _sds((n, nz, PAIR_W, PAIR_W))],
        scratch_shapes=[pltpu.VMEM((nz, PAIR_W, PAIR_W), f32)],
        compiler_params=_cparams(("arbitrary",)),
    )(p3, p3, p3, p3, p3, p3, cos, sins, cos, sins, *consts)


def ret_scan_bwd(p3, cos, sins, consts, s_all, do, t_ctx):
    nb, s, _ = p3.shape
    n, cf, cb = _chunk_maps(t_ctx // RC, (s - t_ctx) // RC)
    nz = 4 * nb

    def rf(t):
        return cf(n - 1 - t)

    def rb(t):
        return cb(n - 1 - t)

    def body(qf, kf, vf, qb, kb, vb, cosf, sinf, cosb, sinb, intra_r, qdec_r, kdec_r, cd_r, bd_r, bdr_r,
             sall_ref, dof, dob, dqf, dkf, dvf, dqb, dkb, dvb, ds_sc):
        @pl.when(pl.program_id(0) == 0)
        def _():
            ds_sc[...] = jnp.zeros_like(ds_sc)
        step = functools.partial(_ret_step, cos=_pair_tables(cosf, cosb, nb), sins=_pair_tables(sinf, sinb, nb),
                                 intra=intra_r[...], qdec=qdec_r[...], kdec=kdec_r[...], cd=cd_r[...], bd2=bd_r[...],
                                 bdr=bdr_r[...])
        _, vjp = jax.vjp(step, sall_ref[0], _pairs(qf, qb, nb), _pairs(kf, kb, nb), _pairs(vf, vb, nb))
        ds, dq, dk, dv = vjp((ds_sc[...], _pairs(dof, dob, nb)))
        ds_sc[...] = ds
        _unpairs(dq, dqf, dqb, nb)
        _unpairs(dk, dkf, dkb, nb)
        _unpairs(dv, dvf, dvb, nb)

    def pspec(m, seg):
        return pl.BlockSpec((nb, RC, BW), lambda t: (0, m(t), seg))

    def tspec(m):
        return pl.BlockSpec((RC, PAIR_W), lambda t: (m(t), 0))

    def ospec(m):
        return pl.BlockSpec((nb, RC, BW), lambda t: (0, m(t), 0))

    return pl.pallas_call(
        body, name="ret_scan_bwd", grid=(n,),
        in_specs=[pspec(rf, 0), pspec(rf, 1), pspec(rf, 2), pspec(rb, 0), pspec(rb, 1), pspec(rb, 2),
                  tspec(rf), tspec(rf), tspec(rb), tspec(rb)] + [_full(c.shape) for c in consts]
                 + [pl.BlockSpec((1, nz, PAIR_W, PAIR_W), lambda t: (n - 1 - t, 0, 0, 0)), ospec(rf), ospec(rb)],
        out_specs=[ospec(rf), ospec(rf), ospec(rf), ospec(rb), ospec(rb), ospec(rb)],
        out_shape=[_sds((nb, s, BW), bf16)] * 6,
        scratch_shapes=[pltpu.VMEM((nz, PAIR_W, PAIR_W), f32)],
        compiler_params=_cparams(("arbitrary",), VMEM_BIG),
    )(p3, p3, p3, p3, p3, p3, cos, sins, cos, sins, *consts, s_all, do, do)


def mix_finish_fwd(fn, name, o_f, o_b, p3, zseg, norm_g, bd):
    nb, s, _ = p3.shape

    def body(of_ref, ob_ref, z_ref, g_ref, bd_ref, y_ref):
        y_ref[0] = fn(of_ref[0], ob_ref[0], z_ref[0], g_ref[...], bd_ref[...]).astype(bf16)

    blk = pl.BlockSpec((1, TM, BW), lambda b, j: (b, j, 0))
    return pl.pallas_call(
        body, name=name, grid=(nb, s // TM),
        in_specs=[blk, blk, pl.BlockSpec((1, TM, BW), lambda b, j: (b, j, zseg)), _full((1, BW)), _full((BW, BW))],
        out_specs=blk, out_shape=_sds((nb, s, BW), bf16),
        compiler_params=_cparams(("arbitrary", "arbitrary")),
    )(o_f, o_b, p3, norm_g, bd)


def mix_finish_bwd(fn, name, o_f, o_b, p3, zseg, norm_g, bd, dy3, yseg):
    nb, s, _ = p3.shape

    def body(of_ref, ob_ref, z_ref, g_ref, bd_ref, dy_ref, do_ref, dz_ref, dg_ref):
        @pl.when((pl.program_id(0) == 0) & (pl.program_id(1) == 0))
        def _():
            dg_ref[...] = jnp.zeros_like(dg_ref)
        bdv = bd_ref[...]
        _, vjp = jax.vjp(lambda a, b, z, g: fn(a, b, z, g, bdv), of_ref[0], ob_ref[0], z_ref[0], g_ref[...])
        do, _, dz, dg = vjp(dy_ref[0])
        do_ref[0] = do
        dz_ref[0] = dz.astype(bf16)
        dg_ref[...] += dg

    blk = pl.BlockSpec((1, TM, BW), lambda b, j: (b, j, 0))
    return pl.pallas_call(
        body, name=name, grid=(nb, s // TM),
        in_specs=[blk, blk, pl.BlockSpec((1, TM, BW), lambda b, j: (b, j, zseg)), _full((1, BW)), _full((BW, BW)),
                  pl.BlockSpec((1, TM, BW), lambda b, j: (b, j, yseg))],
        out_specs=[blk, blk, _full((1, BW))],
        out_shape=[_sds((nb, s, BW)), _sds((nb, s, BW), bf16), _sds((1, BW))],
        compiler_params=_cparams(("arbitrary", "arbitrary")),
    )(o_f, o_b, p3, norm_g, bd, dy3)


GDN_QKV = 11 * BW // 128
N_QKV = 3 * BW // 128


def gdn_conv_fwd(p3, w, t_ctx):
    nb, s, _ = p3.shape
    sd, su = _make_shifts(t_ctx, s)

    def body(x_ref, w_ref, o_ref):
        o_ref[0] = _silu(_conv3(x_ref[0], w_ref[...], sd, su))

    return pl.pallas_call(
        body, name="gdn_conv_fwd", grid=(nb, N_QKV),
        in_specs=[pl.BlockSpec((1, s, 128), lambda b, j: (b, 0, GDN_QKV + j)), pl.BlockSpec((3, 128), lambda b, j: (0, j))],
        out_specs=pl.BlockSpec((1, s, 128), lambda b, j: (b, 0, j)),
        out_shape=_sds((nb, s, 3 * BW)),
        compiler_params=_cparams(("arbitrary", "arbitrary")),
    )(p3, w)


def gdn_conv_bwd(p3, w, d_f, d_b, t_ctx):
    nb, s, _ = p3.shape
    sd, su = _make_shifts(t_ctx, s)

    def body(x_ref, w_ref, df_ref, db_ref, dx_ref, dw_ref):
        @pl.when(pl.program_id(1) == 0)
        def _():
            dw_ref[...] = jnp.zeros_like(dw_ref)
        _, vjp = jax.vjp(lambda x, w_: _silu(_conv3(x, w_, sd, su)), x_ref[0], w_ref[...])
        dx, dw = vjp(df_ref[0] + db_ref[0])
        dx_ref[0] = dx.astype(bf16)
        dw_ref[...] += dw

    blk = pl.BlockSpec((1, s, 128), lambda j, b: (b, 0, j))
    return pl.pallas_call(
        body, name="gdn_conv_bwd", grid=(N_QKV, nb),
        in_specs=[pl.BlockSpec((1, s, 128), lambda j, b: (b, 0, GDN_QKV + j)), pl.BlockSpec((3, 128), lambda j, b: (0, j)),
                  blk, blk],
        out_specs=[blk, pl.BlockSpec((3, 128), lambda j, b: (0, j))],
        out_shape=[_sds((nb, s, 3 * BW), bf16), _sds((3, 3 * BW))],
        compiler_params=_cparams(("arbitrary", "arbitrary"), VMEM_BIG),
    )(p3, w, d_f, d_b)


def _pairs(f_ref, b_ref, nb):
    return jnp.stack([r[b, :, PAIR_W * p:PAIR_W * (p + 1)] for b in range(nb) for r in (f_ref, b_ref) for p in range(2)])


def _pair_tables(f_ref, b_ref, nb):
    return jnp.stack([r[...] for _ in range(nb) for r in (f_ref, b_ref) for _ in range(2)])


def _gates(f_ref, b_ref, nb):
    return jnp.stack([r[b] for b in range(nb) for r in (f_ref, b_ref)])


def _unpairs(a, f_ref, b_ref, nb, lane0=0):
    for b in range(nb):
        for d, r in enumerate((f_ref, b_ref)):
            for p in range(2):
                r[b, :, lane0 + PAIR_W * p:lane0 + PAIR_W * (p + 1)] = a[4 * b + 2 * d + p].astype(r.dtype)


def _with_exchange(body, n_in, n_out, n_scratch, xchg, n_steps):
    if xchg is None:
        return body, [], [], [], []
    kind, arrs = xchg
    nx = len(arrs)

    def fused(*refs):
        ins, rest = refs[:n_in], refs[n_in:]
        srcs, rest = rest[:nx], rest[nx:]
        outs, rest = rest[:n_out], rest[n_out:]
        dsts, rest = rest[:nx], rest[nx:]
        scratch, sems = rest[:n_scratch], rest[n_scratch:]
        start, wait = _peer_exchange(kind, "chips", srcs, dsts, *sems)
        pl.when(pl.program_id(0) == 0)(start)
        body(*ins, *outs, *scratch)
        pl.when(pl.program_id(0) == n_steps - 1)(wait)

    any_ = pl.BlockSpec(memory_space=pl.ANY)
    return fused, [any_] * nx, [any_] * nx, _exchange_shapes(kind, "chips", arrs), _exchange_scratch("chips", nx)


def gdn_scan_fwd(c3, p3, alog, dtb, consts, t_ctx, xchg=None):
    nb, s, _ = p3.shape
    n, cf, cb = _chunk_maps(t_ctx // GC, (s - t_ctx) // GC)
    gblk = GATE_COL // 128

    nz = 4 * nb

    def body(qf, kf, vf, gf, qb, kb, vb, gb, al_ref, dt_ref, tm_r, tm2_r, st2_r, eg_r, eb_r, egt_r, dsel_r, eye_r, bd_r,
             of_ref, ob_ref, sall_ref, inv_ref, s_sc):
        @pl.when(pl.program_id(0) == 0)
        def _():
            s_sc[...] = jnp.zeros_like(s_sc)
        st = s_sc[...]
        sall_ref[0] = st
        s_new, o, inv = _gdn_step(st, _pairs(qf, qb, nb), _pairs(kf, kb, nb), _pairs(vf, vb, nb), _gates(gf, gb, nb),
                                  al_ref[...], dt_ref[...], tm_r[...], tm2_r[...], st2_r[...], eg_r[...], eb_r[...],
                                  egt_r[...], dsel_r[...], eye_r[...], bd_r[...])
        s_sc[...] = s_new
        inv_ref[0] = inv
        _unpairs(o, of_ref, ob_ref, nb)

    def cspec(m, col=0):
        return pl.BlockSpec((nb, GC, BW), lambda t: (0, m(t), col))

    def gspec(m):
        return pl.BlockSpec((nb, GC, 128), lambda t: (0, m(t), gblk))

    fused, x_in, x_out, x_shape, x_scratch = _with_exchange(body, 10 + len(consts), 4, 1, xchg, n)
    return pl.pallas_call(
        fused, name="gdn_scan_fwd" + ("" if xchg is None else "_" + xchg[0]), grid=(n,),
        in_specs=[cspec(cf, 0), cspec(cf, 1), cspec(cf, 2), gspec(cf), cspec(cb, 0), cspec(cb, 1), cspec(cb, 2), gspec(cb),
                  _full((1, 128)), _full((1, 128))] + [_full(c.shape) for c in consts] + x_in,
        out_specs=[cspec(cf), cspec(cb), pl.BlockSpec((1, nz, PAIR_W, PAIR_W), lambda t: (t, 0, 0, 0)),
                   pl.BlockSpec((1, nz, GC, PAIR_W), lambda t: (t, 0, 0, 0))] + x_out,
        out_shape=[_sds((nb, s, BW)), _sds((nb, s, BW)), _sds((n, nz, PAIR_W, PAIR_W)), _sds((n, nz, GC, PAIR_W))]
                  + x_shape,
        scratch_shapes=[pltpu.VMEM((nz, PAIR_W, PAIR_W), f32)] + x_scratch,
        compiler_params=_cparams(("arbitrary",)),
    )(c3, c3, c3, p3, c3, c3, c3, p3, alog, dtb, *consts, *([] if xchg is None else xchg[1]))


def gdn_scan_bwd(c3, p3, alog, dtb, consts, s_all, inv_all, do, t_ctx, xchg=None):
    nb, s, _ = p3.shape
    n, cf, cb = _chunk_maps(t_ctx // GC, (s - t_ctx) // GC)
    gblk = GATE_COL // 128

    def rf(t):
        return cf(n - 1 - t)

    def rb(t):
        return cb(n - 1 - t)

    nz = 4 * nb

    def body(qf, kf, vf, gf, qb, kb, vb, gb, al_ref, dt_ref, tm_r, tm2_r, st2_r, eg_r, eb_r, egt_r, dsel_r, eye_r, bd_r,
             sall_ref, inv_ref, dof, dob, dcf, dgf, dcb, dgb, dal_ref, ddt_ref, ds_sc):
        @pl.when(pl.program_id(0) == 0)
        def _():
            dal_ref[...] = jnp.zeros_like(dal_ref)
            ddt_ref[...] = jnp.zeros_like(ddt_ref)
            ds_sc[...] = jnp.zeros_like(ds_sc)
        consts = dict(tmask=tm_r[...], tmask2=tm2_r[...], strict2=st2_r[...], exp_g=eg_r[...], exp_b=eb_r[...],
                      exp_gt=egt_r[...], dsel2=dsel_r[...], eye2=eye_r[...], bd2=bd_r[...], inv=inv_ref[0])

        def step(*a):
            return _gdn_step(*a, **consts)[:2]

        _, vjp = jax.vjp(step, sall_ref[0], _pairs(qf, qb, nb), _pairs(kf, kb, nb), _pairs(vf, vb, nb),
                         _gates(gf, gb, nb), al_ref[...], dt_ref[...])
        ds, dq, dk, dv, dg, dal, ddt = vjp((ds_sc[...], _pairs(dof, dob, nb)))
        ds_sc[...] = ds
        for i, a in enumerate((dq, dk, dv)):
            _unpairs(a, dcf, dcb, nb, BW * i)
        for b in range(nb):
            dgf[b] = dg[2 * b].astype(bf16)
            dgb[b] = dg[2 * b + 1].astype(bf16)
        dal_ref[...] += dal
        ddt_ref[...] += ddt

    def cspec(m, col=0):
        return pl.BlockSpec((nb, GC, BW), lambda t: (0, m(t), col))

    def gspec(m):
        return pl.BlockSpec((nb, GC, 128), lambda t: (0, m(t), gblk))

    def dcout(m):
        return pl.BlockSpec((nb, GC, 3 * BW), lambda t: (0, m(t), 0))

    def gout(m):
        return pl.BlockSpec((nb, GC, 128), lambda t: (0, m(t), 0))

    fused, x_in, x_out, x_shape, x_scratch = _with_exchange(body, 14 + len(consts), 6, 1, xchg, n)
    return pl.pallas_call(
        fused, name="gdn_scan_bwd" + ("" if xchg is None else "_" + xchg[0]), grid=(n,),
        in_specs=[cspec(rf, 0), cspec(rf, 1), cspec(rf, 2), gspec(rf), cspec(rb, 0), cspec(rb, 1), cspec(rb, 2), gspec(rb),
                  _full((1, 128)), _full((1, 128))] + [_full(c.shape) for c in consts]
                 + [pl.BlockSpec((1, nz, PAIR_W, PAIR_W), lambda t: (n - 1 - t, 0, 0, 0)),
                    pl.BlockSpec((1, nz, GC, PAIR_W), lambda t: (n - 1 - t, 0, 0, 0)), cspec(rf), cspec(rb)] + x_in,
        out_specs=[dcout(rf), gout(rf), dcout(rb), gout(rb), _full((1, 128)), _full((1, 128))] + x_out,
        out_shape=[_sds((nb, s, 3 * BW)), _sds((nb, s, 128), bf16), _sds((nb, s, 3 * BW)), _sds((nb, s, 128), bf16),
                   _sds((1, 128)), _sds((1, 128))] + x_shape,
        scratch_shapes=[pltpu.VMEM((nz, PAIR_W, PAIR_W), f32)] + x_scratch,
        compiler_params=_cparams(("arbitrary",), VMEM_BIG),
    )(c3, c3, c3, p3, c3, c3, c3, p3, alog, dtb, *consts, s_all, inv_all, do, do, *([] if xchg is None else xchg[1]))


def _sg_consts():
    hmp = np.zeros((2, NH, PAIR_W))
    for h in range(NH):
        hmp[h // 2, h, (h % 2) * HD:(h % 2 + 1) * HD] = 1.0
    bdr = (np.arange(2 * RC)[:, None] // RC == np.arange(PAIR_W)[None, :] // HD)
    return jnp.asarray(hmp, f32), jnp.asarray(bdr, f32)


def _sg_rows(s):
    return 6 * RC if s % (6 * RC) == 0 else 2 * RC


def _halves(ref):
    return ref[0, :, :PAIR_W], ref[0, :, PAIR_W:]


def sg_fwd(p3, w, b, hmp, bdr):
    nb, s, _ = p3.shape
    ts = _sg_rows(s)

    def body(u_ref, v_ref, z_ref, w_ref, b_ref, hm_ref, bdr_ref, y_ref):
        y0, y1 = _sg_block(*_halves(u_ref), *_halves(v_ref), *_halves(z_ref), w_ref[...], b_ref[...], hm_ref[...],
                           bdr_ref[...])
        y_ref[0, :, :PAIR_W] = y0.astype(bf16)
        y_ref[0, :, PAIR_W:] = y1.astype(bf16)

    def seg(k):
        return pl.BlockSpec((1, ts, BW), lambda bi, i: (bi, i, k))

    return pl.pallas_call(
        body, name="sg_fwd", grid=(nb, s // ts),
        in_specs=[seg(4), seg(5), seg(6), _full((NH, RC, RC)), _full((NH, RC)), _full(hmp.shape), _full(bdr.shape)],
        out_specs=pl.BlockSpec((1, ts, BW), lambda bi, i: (bi, i, 0)),
        out_shape=_sds((nb, s, BW), bf16),
        compiler_params=_cparams(("arbitrary", "arbitrary")),
    )(p3, p3, p3, w, b, hmp, bdr)


def sg_bwd(p3, w, b, hmp, bdr, dy3):
    nb, s, _ = p3.shape
    ts = _sg_rows(s)

    def body(u_ref, v_ref, z_ref, w_ref, b_ref, hm_ref, bdr_ref, dy_ref, du_ref, dv_ref, dz_ref, dw_ref, db_ref):
        @pl.when((pl.program_id(0) == 0) & (pl.program_id(1) == 0))
        def _():
            dw_ref[...] = jnp.zeros_like(dw_ref)
            db_ref[...] = jnp.zeros_like(db_ref)
        hm, bdr_v = hm_ref[...], bdr_ref[...]
        _, vjp = jax.vjp(lambda *a: _sg_block(*a, hm, bdr_v), *_halves(u_ref), *_halves(v_ref), *_halves(z_ref),
                         w_ref[...], b_ref[...])
        du0, du1, dv0, dv1, dz0, dz1, dw, db = vjp(_halves(dy_ref))
        for ref, a0, a1 in ((du_ref, du0, du1), (dv_ref, dv0, dv1), (dz_ref, dz0, dz1)):
            ref[0, :, :PAIR_W] = a0.astype(bf16)
            ref[0, :, PAIR_W:] = a1.astype(bf16)
        dw_ref[...] += dw
        db_ref[...] += db

    def seg(k):
        return pl.BlockSpec((1, ts, BW), lambda bi, i: (bi, i, k))

    blk = pl.BlockSpec((1, ts, BW), lambda bi, i: (bi, i, 0))
    return pl.pallas_call(
        body, name="sg_bwd", grid=(nb, s // ts),
        in_specs=[seg(4), seg(5), seg(6), _full((NH, RC, RC)), _full((NH, RC)), _full(hmp.shape), _full(bdr.shape),
                  seg(1)],
        out_specs=[blk, blk, blk, _full((NH, RC, RC)), _full((NH, RC))],
        out_shape=[_sds((nb, s, BW), bf16)] * 3 + [_sds((NH, RC, RC)), _sds((NH, RC))],
        compiler_params=_cparams(("arbitrary", "arbitrary"), VMEM_BIG),
    )(p3, p3, p3, w, b, hmp, bdr, dy3)


def _sc_fn(b, c, h, z, w, sd, su):
    return b * _conv3(c * h, w, sd, su) * _silu(z)


def sc_fwd(p3, w, t_ctx):
    nb, s, _ = p3.shape
    sd, su = _make_shifts(t_ctx, s)

    def body(b_ref, c_ref, h_ref, z_ref, w_ref, y_ref):
        y_ref[0] = _sc_fn(b_ref[0], c_ref[0], h_ref[0], z_ref[0], w_ref[...], sd, su).astype(bf16)

    def seg(k):
        return pl.BlockSpec((1, s, 128), lambda bi, j: (bi, 0, 2 * k + j))

    return pl.pallas_call(
        body, name="sc_fwd", grid=(nb, 2),
        in_specs=[seg(7), seg(8), seg(9), seg(10), pl.BlockSpec((3, 128), lambda bi, j: (0, j))],
        out_specs=pl.BlockSpec((1, s, 128), lambda bi, j: (bi, 0, j)),
        out_shape=_sds((nb, s, BW), bf16),
        compiler_params=_cparams(("arbitrary", "arbitrary"), VMEM_BIG),
    )(p3, p3, p3, p3, w)


def sc_bwd(p3, w, dy3, t_ctx):
    nb, s, _ = p3.shape
    sd, su = _make_shifts(t_ctx, s)

    def body(b_ref, c_ref, h_ref, z_ref, w_ref, dy_ref, db_ref, dc_ref, dh_ref, dz_ref, dw_ref):
        @pl.when(pl.program_id(1) == 0)
        def _():
            dw_ref[...] = jnp.zeros_like(dw_ref)
        _, vjp = jax.vjp(lambda b, c, h, z, w_: _sc_fn(b, c, h, z, w_, sd, su),
                         b_ref[0], c_ref[0], h_ref[0], z_ref[0], w_ref[...])
        db, dc, dh, dz, dw = vjp(dy_ref[0])
        db_ref[0] = db.astype(bf16)
        dc_ref[0] = dc.astype(bf16)
        dh_ref[0] = dh.astype(bf16)
        dz_ref[0] = dz.astype(bf16)
        dw_ref[...] += dw

    def seg(k):
        return pl.BlockSpec((1, s, 128), lambda j, bi: (bi, 0, 2 * k + j))

    blk = pl.BlockSpec((1, s, 128), lambda j, bi: (bi, 0, j))
    wspec = pl.BlockSpec((3, 128), lambda j, bi: (0, j))
    return pl.pallas_call(
        body, name="sc_bwd", grid=(2, nb),
        in_specs=[seg(7), seg(8), seg(9), seg(10), wspec, seg(2)],
        out_specs=[blk, blk, blk, blk, wspec],
        out_shape=[_sds((nb, s, BW), bf16)] * 4 + [_sds((3, BW))],
        compiler_params=_cparams(("arbitrary", "arbitrary"), VMEM_BIG),
    )(p3, p3, p3, p3, w, dy3)


def assemble_dp(pairs, singles_a, gdn_x, singles_b, gates):
    nb, s, _ = singles_a[0].shape
    flat = [a for pr in pairs for a in pr] + list(singles_a) + list(gdn_x) + list(singles_b) + list(gates)
    n_pairs, n_a, n_x, n_b = len(pairs), len(singles_a), len(gdn_x), len(singles_b)

    def body(*refs):
        out = refs[-1]
        ins = refs[:-1]
        col = 0
        for p in range(n_pairs):
            out[0, :, col:col + BW] = (ins[2 * p][0].astype(f32) + ins[2 * p + 1][0].astype(f32)).astype(bf16)
            col += BW
        k = 2 * n_pairs
        for _ in range(n_a + n_x + n_b):
            wk = ins[k].shape[-1]
            out[0, :, col:col + wk] = ins[k][0]
            col += wk
            k += 1
        out[0, :, col:col + 128] = (ins[k][0].astype(f32) + ins[k + 1][0].astype(f32)).astype(bf16)
        out[0, :, col + 128:] = jnp.zeros((TM, PW - col - 128), bf16)

    def spec(a):
        return pl.BlockSpec((1, TM, a.shape[-1]), lambda b, j: (b, j, 0))

    return pl.pallas_call(
        body, name="assemble_dp", grid=(nb, s // TM),
        in_specs=[spec(a) for a in flat],
        out_specs=pl.BlockSpec((1, TM, PW), lambda b, j: (b, j, 0)),
        out_shape=_sds((nb, s, PW), bf16),
        compiler_params=_cparams(("arbitrary", "arbitrary")),
    )(*flat)


def mod_fwd(c_rows, w_mod, b_cols):
    nl, _, wc = w_mod.shape
    nr = c_rows.shape[0]

    def body(c_ref, w_ref, b_ref, o_ref):
        o_ref[0] = _dot(_silu(c_ref[...]), w_ref[0], precision=HI) + b_ref[0]

    return pl.pallas_call(
        body, name="mod_fwd", grid=(nl,),
        in_specs=[_full((nr, D)), pl.BlockSpec((1, D, wc), lambda l: (l, 0, 0)), pl.BlockSpec((1, 1, wc), lambda l: (l, 0, 0))],
        out_specs=pl.BlockSpec((1, nr, wc), lambda l: (l, 0, 0)),
        out_shape=_sds((nl, nr, wc)),
        compiler_params=_cparams(("arbitrary",)),
    )(c_rows, w_mod, b_cols)


def mod_bwd(c_rows, w_mod, dm_cols, dm_full):
    nl, _, wc = w_mod.shape
    nr = c_rows.shape[0]

    def body(c_ref, w_ref, dmc_ref, dmf_ref, gw_ref, gb_ref, dcc_ref):
        @pl.when(pl.program_id(0) == 0)
        def _():
            dcc_ref[...] = jnp.zeros_like(dcc_ref)
        a = _silu(c_ref[...])
        dmc = dmc_ref[0]
        gw_ref[0] = _dot_tn(a, dmc, precision=HI)
        gb_ref[0] = jnp.sum(dmf_ref[0], axis=0, keepdims=True)
        dcc_ref[...] += _dot_nt(dmc[nr - 8:nr], w_ref[0], precision=HI)

    return pl.pallas_call(
        body, name="mod_bwd", grid=(nl,),
        in_specs=[_full((nr, D)), pl.BlockSpec((1, D, wc), lambda l: (l, 0, 0)),
                  pl.BlockSpec((1, nr, wc), lambda l: (l, 0, 0)), pl.BlockSpec((1, nr, 3 * D), lambda l: (l, 0, 0))],
        out_specs=[pl.BlockSpec((1, D, wc), lambda l: (l, 0, 0)), pl.BlockSpec((1, 1, 3 * D), lambda l: (l, 0, 0)),
                   _full((8, D))],
        out_shape=[_sds((nl, D, wc)), _sds((nl, 1, 3 * D)), _sds((8, D))],
        compiler_params=_cparams(("arbitrary",)),
    )(c_rows, w_mod, dm_cols, dm_full)


def cctx_grad(parts, c_ctx):
    def body(p_ref, c_ref, o_ref):
        tot = p_ref[0, 0:1, :]
        for k in (2, 4, 6):
            tot = tot + p_ref[k, 0:1, :]
        c = c_ref[...]
        sg = jax.nn.sigmoid(c)
        o_ref[...] = tot * (sg * (1.0 + c * (1.0 - sg)))

    return pl.pallas_call(body, name="cctx_grad", out_shape=_sds((1, D)))(parts, c_ctx)


def sum_lead(x, out_dtype=f32, tr=256, tc=None):
    k, r, c = x.shape
    tr = min(tr, r)
    tc = c if tc is None else tc
    assert r % tr == 0 and c % tc == 0

    def body(x_ref, o_ref):
        tot = x_ref[0].astype(f32)
        for i in range(1, k):
            tot = tot + x_ref[i].astype(f32)
        o_ref[...] = tot.astype(out_dtype)

    return pl.pallas_call(
        body, name="sum_lead", grid=(r // tr, c // tc),
        in_specs=[pl.BlockSpec((k, tr, tc), lambda i, j: (0, i, j))],
        out_specs=pl.BlockSpec((tr, tc), lambda i, j: (i, j)),
        out_shape=_sds((r, c), out_dtype),
        compiler_params=_cparams(("arbitrary", "arbitrary")),
    )(x)


def adamw(w, m, v, g1, g2=None, tr=256, block=None):
    if block is None:
        block = (1,) * (w.ndim - 2) + (min(tr, w.shape[-2]), w.shape[-1])
    assert len(block) == w.ndim and all(d % b == 0 for d, b in zip(w.shape, block))
    two = g2 is not None
    c1 = 1.0 / (1.0 - ADAM_B1 ** ADAM_STEP)
    c2 = 1.0 / (1.0 - ADAM_B2 ** ADAM_STEP)

    def body(*refs):
        w_ref, m_ref, v_ref, g_ref = refs[:4]
        g = g_ref[...]
        if two:
            g = g + refs[4][...]
        go_ref, d_ref, mo_ref, vo_ref = refs[-4:]
        mn = ADAM_B1 * m_ref[...] + (1.0 - ADAM_B1) * g
        vn = ADAM_B2 * v_ref[...] + (1.0 - ADAM_B2) * (g * g)
        go_ref[...] = g
        mo_ref[...] = mn
        vo_ref[...] = vn
        d_ref[...] = -ADAM_LR * ((mn * c1) / (jnp.sqrt(vn * c2) + ADAM_EPS) + ADAM_WD * w_ref[...])

    blk = pl.BlockSpec(block, lambda *i: i)
    grid = tuple(d // b for d, b in zip(w.shape, block))
    args = [w, m, v, g1] + ([g2] if two else [])
    return pl.pallas_call(
        body, name="adamw", grid=grid,
        in_specs=[blk] * len(args), out_specs=[blk] * 4, out_shape=[_sds(w.shape)] * 4,
        compiler_params=_cparams(("arbitrary",) * len(grid)),
    )(*args)


def _my_pos():
    return lax.axis_index("x"), lax.axis_index("y"), lax.axis_index("c")


GROUP_SIZE = {"devices": N_DEV, "chips": N_CHIPS, "cores": 2}


def _peer_exchange(kind, group, src_refs, dst_refs, send_sems, recv_sems, local_sems):
    mx, my, mc = _my_pos()
    n = GROUP_SIZE[group]
    if group == "devices":
        me = 4 * mx + 2 * my + mc
    elif group == "chips":
        me = 2 * mx + my
    else:
        me = mc

    def peer(k):
        if group == "devices":
            return (mx ^ (k >> 2), my ^ ((k >> 1) & 1), mc ^ (k & 1))
        if group == "chips":
            return (mx ^ (k >> 1), my ^ (k & 1), mc)
        return (mx, my, mc ^ k)

    def copies():
        local, sends, recvs = [], [], []
        for i, (src, dst) in enumerate(zip(src_refs, dst_refs)):
            def part(k):
                return src.at[k] if kind == "scatter" else src

            def slab(k):
                return dst if kind == "send" else dst.at[k]

            if kind != "send":
                local.append(pltpu.make_async_copy(part(me), dst.at[me], local_sems.at[i]))
            for k in range(1, n):
                sem = dict(send_sem=send_sems.at[i, k - 1], recv_sem=recv_sems.at[i, k - 1], device_id_type=MESH)
                sends.append(pltpu.make_async_remote_copy(src_ref=part(me ^ k), dst_ref=slab(me), device_id=peer(k), **sem))
                recvs.append(pltpu.make_async_remote_copy(src_ref=part(me ^ k), dst_ref=slab(me ^ k),
                                                          device_id=(mx, my, mc), **sem))
        return local, sends, recvs

    def start():
        local, sends, _ = copies()
        for cp in local + sends:
            cp.start()

    def wait():
        local, sends, recvs = copies()
        for cp in recvs:
            cp.wait_recv()
        for cp in sends:
            cp.wait_send()
        for cp in local:
            cp.wait()

    return start, wait


def _exchange_scratch(group, n):
    k = GROUP_SIZE[group] - 1
    return [pltpu.SemaphoreType.DMA((n, k)), pltpu.SemaphoreType.DMA((n, k)), pltpu.SemaphoreType.DMA((n,))]


def _exchange_shapes(kind, group, arrs):
    return [_sds(((GROUP_SIZE[group],) + a.shape) if kind == "gather" else a.shape, a.dtype) for a in arrs]


def exchange(name, parts):
    counts = [len(arrs) for _, _, arrs in parts]
    total = sum(counts)

    def body(*refs):
        srcs, dsts, sems = refs[:total], refs[total:2 * total], refs[2 * total:]
        ops, at = [], 0
        for j, (kind, group, arrs) in enumerate(parts):
            ops.append(_peer_exchange(kind, group, srcs[at:at + counts[j]], dsts[at:at + counts[j]], *sems[3 * j:3 * j + 3]))
            at += counts[j]
        for start, _ in ops:
            start()
        for _, wait in ops:
            wait()

    any_ = pl.BlockSpec(memory_space=pl.ANY)
    flat = [a for _, _, arrs in parts for a in arrs]
    outs = pl.pallas_call(
        body, name=name, out_shape=[sh for kind, group, arrs in parts for sh in _exchange_shapes(kind, group, arrs)],
        in_specs=[any_] * total, out_specs=[any_] * total,
        scratch_shapes=[sc for _, group, arrs in parts for sc in _exchange_scratch(group, len(arrs))],
    )(*flat)
    res, at = [], 0
    for cnt in counts:
        res.append(list(outs[at:at + cnt]))
        at += cnt
    return res


def gather8(x):
    return exchange("gather8", [("gather", "devices", [x])])[0][0]


PACK_ROWS = 64
SMALL = ("c_ctx", "b_mod", "g_pre", "g_post", "ret_norm_g", "sg_w", "sg_b", "sc_conv_w", "gdn_conv_w",
         "gdn_a_log", "gdn_dt_bias", "gdn_norm_g")


def _pack(arrs, width=D, mult=PACK_ROWS):
    rows = []
    for a in arrs:
        flat = a.reshape(-1)
        pad = (-flat.shape[0]) % width
        rows.append(jnp.pad(flat, (0, pad)).reshape(-1, width))
    out = jnp.concatenate(rows, axis=0)
    return jnp.pad(out, ((0, (-out.shape[0]) % mult), (0, 0)))


def _unpack(packed, shapes, width=D):
    outs, r = [], 0
    for shp in shapes:
        size = int(np.prod(shp))
        nr = -(-size // width)
        outs.append(packed[r:r + nr].reshape(-1)[:size].reshape(shp))
        r += nr
    return outs


def kernel(x, c, ctx, c_ctx, w_mod, b_mod, g_pre, g_post, w_in, w_out, ret_norm_g, sg_w, sg_b, sc_conv_w, gdn_conv_w, gdn_a_log, gdn_dt_bias, gdn_norm_g, loss_target, m_c_ctx, m_w_mod, m_b_mod, m_g_pre, m_g_post, m_w_in, m_w_out, m_ret_norm_g, m_sg_w, m_sg_b, m_sc_conv_w, m_gdn_conv_w, m_gdn_a_log, m_gdn_dt_bias, m_gdn_norm_g, v_c_ctx, v_w_mod, v_b_mod, v_g_pre, v_g_post, v_w_in, v_w_out, v_ret_norm_g, v_sg_w, v_sg_b, v_sc_conv_w, v_gdn_conv_w, v_gdn_a_log, v_gdn_dt_bias, v_gdn_norm_g):
    weights = dict(c_ctx=c_ctx, w_mod=w_mod, b_mod=b_mod, g_pre=g_pre, g_post=g_post, w_in=w_in, w_out=w_out,
                   ret_norm_g=ret_norm_g, sg_w=sg_w, sg_b=sg_b, sc_conv_w=sc_conv_w, gdn_conv_w=gdn_conv_w,
                   gdn_a_log=gdn_a_log, gdn_dt_bias=gdn_dt_bias, gdn_norm_g=gdn_norm_g)
    mom = dict(c_ctx=m_c_ctx, w_mod=m_w_mod, b_mod=m_b_mod, g_pre=m_g_pre, g_post=m_g_post, w_in=m_w_in,
               w_out=m_w_out, ret_norm_g=m_ret_norm_g, sg_w=m_sg_w, sg_b=m_sg_b, sc_conv_w=m_sc_conv_w,
               gdn_conv_w=m_gdn_conv_w, gdn_a_log=m_gdn_a_log, gdn_dt_bias=m_gdn_dt_bias, gdn_norm_g=m_gdn_norm_g)
    var = dict(c_ctx=v_c_ctx, w_mod=v_w_mod, b_mod=v_b_mod, g_pre=v_g_pre, g_post=v_g_post, w_in=v_w_in,
               w_out=v_w_out, ret_norm_g=v_ret_norm_g, sg_w=v_sg_w, sg_b=v_sg_b, sc_conv_w=v_sc_conv_w,
               gdn_conv_w=v_gdn_conv_w, gdn_a_log=v_gdn_a_log, gdn_dt_bias=v_gdn_dt_bias, gdn_norm_g=v_gdn_norm_g)

    nb, t_lat, _ = x.shape
    t_ctx = ctx.shape[1]
    s = t_ctx + t_lat
    n = nb * s
    sb = s // TM
    nl = w_in.shape[0]
    wc_in = w_in.shape[2]
    wc_mod = w_mod.shape[2]
    rows_out = w_out.shape[1]
    n_all = nb * N_DEV
    mx, my, mc = _my_pos()
    chip = 2 * mx + my
    dev = 2 * chip + mc

    sg_c = _sg_consts()
    bd = jnp.asarray(_block_diag())
    ret_c = _ret_consts(nb)
    gdn_c = _gdn_consts(nb)
    cos, sins = _rope_tables(t_lat, t_ctx)

    w_in_b, w_out_b = w_in.astype(bf16), w_out.astype(bf16)
    pre = _pack([c, sc_conv_w, gdn_conv_w], mult=8)
    (pre_all,), w0_parts = exchange("startup_gather", [("gather", "devices", [pre]), ("gather", "chips", [w_in_b[0]])])
    c_parts, scw_parts, gcw_parts = [], [], []
    for k in range(N_DEV):
        ck, sk, gk = _unpack(pre_all[k], [c.shape, sc_conv_w.shape, gdn_conv_w.shape])
        c_parts.append(ck)
        if k % 2 == 0:
            scw_parts.append(sk)
            gcw_parts.append(gk)
    c_all = jnp.concatenate(c_parts, axis=0)
    sc_w_full = jnp.concatenate(scw_parts, axis=-1)
    gdn_w_full = jnp.concatenate(gcw_parts, axis=-1)
    c_rows = jnp.concatenate([c_all, c_ctx[None, :], jnp.zeros((7, D), f32)], axis=0)

    b_cols = lax.dynamic_slice_in_dim(b_mod, chip * wc_mod, wc_mod, axis=1)[:, None, :]
    mod_part = mod_fwd(c_rows, w_mod, b_cols)
    mod_all = gather8(mod_part)
    mod = jnp.concatenate([mod_all[2 * k] for k in range(N_CHIPS)], axis=-1)
    my_rows = jnp.concatenate([lax.dynamic_slice_in_dim(mod, dev * nb, nb, axis=1), mod[:, n_all:n_all + 1]], axis=1)
    shift_t = my_rows[:, :, None, 0:D]
    scale_t = my_rows[:, :, None, D:2 * D]
    gate_t = my_rows[:, :, None, 2 * D:3 * D]

    w_in_full, w_out_full = [None] * nl, [None] * nl
    w_in_full[0] = place_weights(w0_parts[0])

    alog = jnp.pad(gdn_a_log.reshape(nl, 1, 8), ((0, 0), (0, 0), (0, 120)))
    dtb = jnp.pad(gdn_dt_bias.reshape(nl, 1, 8), ((0, 0), (0, 0), (0, 120)))
    gdn_ng = jnp.tile(gdn_norm_g, (1, NH))[:, None, :]
    ret_ng = ret_norm_g[:, None, :]

    xs = jnp.concatenate([ctx, x], axis=1).reshape(n, D)
    saved = []
    for l in range(nl):
        p, h = inproj_fwd(xs, shift_t[l], scale_t[l], g_pre[l][None, :], w_in_full[l], nb, sb)
        p3 = p.reshape(nb, s, PW)
        ro_f, ro_b, rs_all = ret_scan_fwd(p3, cos, sins, ret_c, t_ctx)
        y_ret = mix_finish_fwd(_ret_finish, "ret_finish_fwd", ro_f, ro_b, p3, 3, ret_ng[l], bd)
        y_sg = sg_fwd(p3, sg_w[l], sg_b[l], *sg_c)
        y_sc = sc_fwd(p3, sc_w_full[l], t_ctx)
        c3 = gdn_conv_fwd(p3, gdn_w_full[l], t_ctx)
        riding = [w_out_b[l]] + ([w_in_b[l + 1]] if l + 1 < nl else [])
        go_f, go_b, *gs_all = gdn_scan_fwd(c3, p3, alog[l], dtb[l], gdn_c, t_ctx, ("gather", riding))
        w_out_full[l] = gs_all[2].reshape(D, D)
        if l + 1 < nl:
            w_in_full[l + 1] = place_weights(gs_all[3])
        gs_all = gs_all[:2]
        y_gdn = mix_finish_fwd(_gdn_finish, "gdn_finish_fwd", go_f, go_b, p3, 14, gdn_ng[l], bd)
        ys = [a.reshape(n, BW) for a in (y_ret, y_sg, y_sc, y_gdn)]
        x_new, o = outproj_fwd(ys, w_out_full[l], xs, gate_t[l], g_post[l][None, :], nb, sb)
        saved.append(dict(x=xs, h=h, p3=p3, ro=(ro_f, ro_b), rs=rs_all, c=c3, go=(go_f, go_b), gs=gs_all,
                          ys=ys, o=o))
        xs = x_new

    dx3, loss_part = loss_head(xs.reshape(nb, s, D), loss_target, t_ctx)
    loss = lax.psum(loss_part[0, 0], ("x", "y", "c"))

    dxs = dx3.reshape(n, D)
    g_small = {k: [None] * nl for k in SMALL if k not in ("c_ctx", "b_mod")}
    dm_rows = [None] * nl
    slab_in = None
    got_in, got_out = [None] * nl, [None] * nl
    for l in reversed(range(nl)):
        sv = saved[l]
        p3 = sv["p3"]
        dy, gw_out, dg_post, dgate = outproj_bwd(dxs, sv["o"], gate_t[l], g_post[l][None, :], sv["ys"], w_out_full[l], nb, sb)
        dy3 = dy.reshape(nb, s, D)
        r_do, r_dz, d_rng = mix_finish_bwd(_ret_finish, "ret_finish_bwd", *sv["ro"], p3, 3, ret_ng[l], bd, dy3, 0)
        r_d = ret_scan_bwd(p3, cos, sins, ret_c, sv["rs"], r_do, t_ctx)
        s_du, s_dv, s_dz, d_sgw, d_sgb = sg_bwd(p3, sg_w[l], sg_b[l], *sg_c, dy3)
        c_db, c_dc, c_dh, c_dz, d_scw = sc_bwd(p3, sc_w_full[l], dy3, t_ctx)
        g_do, g_dz, d_gng = mix_finish_bwd(_gdn_finish, "gdn_finish_bwd", *sv["go"], p3, 14, gdn_ng[l], bd, dy3, 3)
        riding = [gw_out.reshape(N_CHIPS, rows_out, D).astype(bf16)] + ([] if slab_in is None else [slab_in])
        g_dcf, g_dgf, g_dcb, g_dgb, g_dal, g_ddt, *got = gdn_scan_bwd(
            sv["c"], p3, alog[l], dtb[l], gdn_c, *sv["gs"], g_do, t_ctx, ("scatter", riding))
        got_out[l] = got[0]
        if slab_in is not None:
            got_in[l + 1] = got[1]
        gx, d_gcw = gdn_conv_bwd(p3, gdn_w_full[l], g_dcf, g_dcb, t_ctx)
        dp3 = assemble_dp([(r_d[0], r_d[3]), (r_d[1], r_d[4]), (r_d[2], r_d[5])],
                          [r_dz, s_du, s_dv, s_dz, c_db, c_dc, c_dh, c_dz], [gx], [g_dz], [g_dgf, g_dgb])
        dp = dp3.reshape(n, PW)
        gw_in = dw_in(sv["h"], dp)
        slab_in = jnp.stack([gw_in[k * wc_in:(k + 1) * wc_in] for k in range(N_CHIPS)])
        dxs, dg_pre, dshift, dscale, *got = inproj_bwd_x(dp, w_in_full[l], sv["x"], scale_t[l], g_pre[l][None, :], dxs, nb, sb,
                                                         ("scatter", [slab_in]) if l == 0 else None)
        if l == 0:
            got_in[0] = got[0]
        g_small["g_pre"][l] = dg_pre[0]
        g_small["g_post"][l] = dg_post[0]
        g_small["ret_norm_g"][l] = d_rng[0]
        g_small["sg_w"][l] = d_sgw
        g_small["sg_b"][l] = d_sgb
        g_small["sc_conv_w"][l] = d_scw
        g_small["gdn_conv_w"][l] = d_gcw
        g_small["gdn_a_log"][l] = g_dal[0, :8].reshape(2, NH)
        g_small["gdn_dt_bias"][l] = g_ddt[0, :8].reshape(2, NH)
        g_small["gdn_norm_g"][l] = d_gng[0].reshape(NH, HD)
        dm_rows[l] = jnp.concatenate([dshift, dscale, dgate], axis=-1)[:nb + 1]
    grad_x = dxs.reshape(nb, s, D)[:, t_ctx:, :]

    g_small = {k: jnp.stack(v) for k, v in g_small.items()}
    dm_rows = jnp.stack(dm_rows)
    names2 = [k for k in SMALL if k not in ("c_ctx", "b_mod")]
    pack_sum = _pack([g_small[k] for k in names2] + [dm_rows[:, nb:]])
    pack_own = _pack([dm_rows[:, :nb]], mult=8)
    rs = -(-pack_sum.shape[0] // (8 * N_DEV)) * 8
    slabs_sum = jnp.pad(pack_sum, ((0, N_DEV * rs - pack_sum.shape[0]), (0, 0))).reshape(N_DEV, rs, D)
    ((got_small,),) = exchange("tail_scatter", [("scatter", "devices", [slabs_sum])])
    my_slab = sum_lead(got_small, tr=rs)
    gin_mine = jnp.stack([sum_lead(a, tr=wc_in, tc=256) for a in got_in], axis=1)
    gout_mine = jnp.stack([sum_lead(a) for a in got_out])
    (all2,), (gin_sib, gout_sib) = exchange("tail_gather", [
        ("gather", "devices", [jnp.concatenate([my_slab, pack_own], axis=0)]), ("send", "cores", [gin_mine, gout_mine])])
    tot2 = all2[:, :rs].reshape(N_DEV * rs, D)
    outs2 = _unpack(tot2, [g_small[k].shape for k in names2] + [(nl, 1, 3 * D)])
    grads = dict(zip(names2, outs2[:-1]))
    dm_own = jnp.stack([_unpack(all2[k, rs:], [(nl, nb, 3 * D)])[0] for k in range(N_DEV)])
    dm_own = jnp.transpose(dm_own, (1, 0, 2, 3)).reshape(nl, n_all, 3 * D)
    dm_all = jnp.concatenate([dm_own, jnp.pad(outs2[-1], ((0, 0), (0, 7), (0, 0)))], axis=1)
    grads["gdn_norm_g"] = sum_lead(jnp.transpose(grads["gdn_norm_g"], (1, 0, 2)), tr=nl)
    for k in ("sc_conv_w", "gdn_conv_w"):
        wc = weights[k].shape[2]
        grads[k] = lax.dynamic_slice_in_dim(grads[k], chip * wc, wc, axis=2)

    dm_cols = lax.dynamic_slice_in_dim(dm_all, chip * wc_mod, wc_mod, axis=2)
    g_w_mod, g_b_mod, dcc_part = mod_bwd(c_rows, w_mod, dm_cols, dm_all)
    grads["b_mod"] = g_b_mod[:, 0, :]
    grads["c_ctx"] = cctx_grad(gather8(dcc_part), c_ctx[None, :])[0]

    res = {}
    w_in_t, m_in_t, v_in_t = [jnp.transpose(a, (2, 0, 1)) for a in (w_in, m_w_in, v_w_in)]
    res["w_in"] = [jnp.transpose(a, (1, 2, 0)) for a in
                   adamw(w_in_t, m_in_t, v_in_t, gin_mine, gin_sib, block=(wc_in // 4, nl, 256))]
    res["w_out"] = adamw(w_out, m_w_out, v_w_out, gout_mine, gout_sib)
    res["w_mod"] = adamw(w_mod, m_w_mod, v_w_mod, g_w_mod)
    shapes = [weights[k].shape for k in SMALL]
    small = adamw(_pack([weights[k] for k in SMALL]), _pack([mom[k] for k in SMALL]), _pack([var[k] for k in SMALL]),
                  _pack([grads[k].reshape(weights[k].shape) for k in SMALL]), tr=PACK_ROWS)
    small = [_unpack(a, shapes) for a in small]
    for i, k in enumerate(SMALL):
        res[k] = [small[j][i] for j in range(4)]

    order = ["c_ctx", "w_mod", "b_mod", "g_pre", "g_post", "w_in", "w_out", "ret_norm_g", "sg_w", "sg_b", "sc_conv_w",
             "gdn_conv_w", "gdn_a_log", "gdn_dt_bias", "gdn_norm_g"]
    return (loss, grad_x, *[res[k][0] for k in order], *[res[k][1] for k in order], *[res[k][2] for k in order],
            *[res[k][3] for k in order])
```

```python
import functools

import jax
import jax.numpy as jnp
import numpy as np
from jax import lax
from jax.experimental import pallas as pl
from jax.experimental.pallas import tpu as pltpu

f32 = jnp.float32
bf16 = jnp.bfloat16
HI = lax.Precision.HIGHEST
P3 = lax.Precision.HIGH
MESH = pl.DeviceIdType.MESH

EPS = 1e-6
D = 1024
NH = 4
HD = 64
BW = NH * HD
PAIR_W = 2 * HD
RC = 128
GC = 64
GRID_W = 64
ROPE_BASE = 10000.0
IN_W = 15 * BW + 16
PW = 4096
GATE_COL = 15 * BW
N_CHIPS = 4
N_DEV = 8
TM = 256
TP = 2 * TM
ADAM_LR, ADAM_B1, ADAM_B2, ADAM_EPS, ADAM_WD, ADAM_STEP = 0.001, 0.9, 0.999, 1e-08, 0.01, 10
LANE_HEAD = np.arange(BW) // HD
VMEM_BIG = 56 * 1024 * 1024


def _dot(a, b, precision=None):
    return jnp.dot(a, b, precision=precision, preferred_element_type=f32)


def _dot_nt(a, b, precision=None):
    return lax.dot_general(a, b, (((1,), (1,)), ((), ())), precision=precision, preferred_element_type=f32)


def _dot_tn(a, b, precision=None):
    return lax.dot_general(a, b, (((0,), (0,)), ((), ())), precision=precision, preferred_element_type=f32)


def _sds(shape, dtype=f32):
    return jax.ShapeDtypeStruct(shape, dtype)


def _cparams(sem=None, vmem=None):
    kw = {}
    if sem is not None:
        kw["dimension_semantics"] = sem
    if vmem is not None:
        kw["vmem_limit_bytes"] = vmem
    return pltpu.CompilerParams(**kw)


def _full(shape):
    n = len(shape)
    return pl.BlockSpec(shape, lambda *_: (0,) * n)


def _block_diag():
    return (LANE_HEAD[:, None] == LANE_HEAD[None, :]).astype(np.float32)


def _tau(c, d):
    return np.arange(c) if d == 0 else c - 1 - np.arange(c)


def _ret_consts(nb):
    lg = np.log(1.0 - 2.0 ** (-5.0 - np.arange(NH)))
    intra = np.zeros((2, 2, RC, 2 * RC)); qdec = np.zeros((2, 2, RC, PAIR_W)); kdec = np.zeros((2, 2, RC, PAIR_W))
    cd = np.zeros((2, 2, PAIR_W, PAIR_W))
    for d in range(2):
        t = _tau(RC, d)
        diff = t[:, None] - t[None, :]
        for p in range(2):
            lane_lg = lg[2 * p + np.arange(PAIR_W) // HD]
            for h in range(2):
                intra[d, p, :, h * RC:(h + 1) * RC] = np.where(diff >= 0, np.exp(np.maximum(diff, 0) * lg[2 * p + h]), 0.0)
            qdec[d, p] = np.exp((t[:, None] + 1.0) * lane_lg[None, :])
            kdec[d, p] = np.exp((RC - 1.0 - t[:, None]) * lane_lg[None, :])
            cd[d, p] = np.exp(RC * lane_lg)[:, None] * np.ones((1, PAIR_W))
    per_z = [np.tile(a.reshape((4,) + a.shape[2:]), (nb, 1, 1)) for a in (intra, qdec, kdec, cd)]
    bd2 = (np.arange(PAIR_W)[:, None] // HD == np.arange(PAIR_W)[None, :] // HD)
    bdr = (np.arange(2 * RC)[:, None] // RC == np.arange(PAIR_W)[None, :] // HD)
    return [jnp.asarray(a, f32) for a in per_z + [bd2, bdr]]


def _rope_tables(t_lat, t_ctx):
    nf = HD // 4
    inv = ROPE_BASE ** (-np.arange(nf) / nf)
    pos = np.arange(t_lat)
    ang_r = (pos // GRID_W)[:, None] * inv[None, :]
    ang_c = (pos % GRID_W)[:, None] * inv[None, :]
    ang = np.concatenate([ang_r, ang_r, ang_c, ang_c], axis=1)
    sign = np.concatenate([-np.ones(nf), np.ones(nf), -np.ones(nf), np.ones(nf)])
    cos = np.tile(np.cos(ang), (1, 2)); sins = np.tile(np.sin(ang) * sign, (1, 2))
    cos = np.concatenate([np.ones((t_ctx, PAIR_W)), cos]); sins = np.concatenate([np.zeros((t_ctx, PAIR_W)), sins])
    return jnp.asarray(cos, f32), jnp.asarray(sins, f32)


def _gdn_consts(nb):
    tmask = np.zeros((2, 2, GC, GC)); tmask2 = np.zeros((2, 2, GC, PAIR_W)); strict2 = np.zeros((2, 2, GC, PAIR_W))
    exp_g = np.zeros((2, 2, 128, PAIR_W)); exp_b = np.zeros((2, 2, 128, PAIR_W))
    for d in range(2):
        t = _tau(GC, d)
        tmask[d, :] = (t[:, None] >= t[None, :])
        tmask2[d, :] = np.tile(t[:, None] >= t[None, :], (1, 2))
        strict2[d, :] = np.tile(t[:, None] > t[None, :], (1, 2))
        for h in range(NH):
            exp_g[d, h // 2, 4 * d + h, (h % 2) * HD:(h % 2 + 1) * HD] = 1.0
            exp_b[d, h // 2, 8 + 4 * d + h, (h % 2) * HD:(h % 2 + 1) * HD] = 1.0
    exp_gt = np.transpose(exp_g, (0, 1, 3, 2))
    per_z = [np.tile(a.reshape((4,) + a.shape[2:]), (nb, 1, 1)) for a in (tmask, tmask2, strict2, exp_g, exp_b, exp_gt)]
    dsel2 = np.tile(np.eye(GC), (1, 2))
    eye2 = np.tile(np.eye(GC), (1, 2))
    bd2 = (np.arange(PAIR_W)[:, None] // HD == np.arange(PAIR_W)[None, :] // HD)
    return [jnp.asarray(a, f32) for a in per_z + [dsel2, eye2, bd2]]


def _swap16(x):
    lane = lax.broadcasted_iota(jnp.int32, x.shape, x.ndim - 1)
    n = x.shape[-1]
    return jnp.where(lane % 32 < 16, pltpu.roll(x, n - 16, axis=x.ndim - 1), pltpu.roll(x, 16, axis=x.ndim - 1))


@jax.custom_vjp
def _rot(x, cos, sins):
    return x * cos + _swap16(x) * sins


def _rot_fwd(x, cos, sins):
    return _rot(x, cos, sins), (cos, sins)


def _rot_bwd(res, g):
    cos, sins = res
    return g * cos + _swap16(g * sins), jnp.zeros_like(cos), jnp.zeros_like(sins)


_rot.defvjp(_rot_fwd, _rot_bwd)


def _silu(z):
    return z * jax.nn.sigmoid(z)


def _head_sum(x, bd):
    return _sel_r(x, bd)


def _ret_step(s, q, k, v, cos, sins, intra, qdec, kdec, cd, bd2, bdr):
    def bdiag(x):
        return jnp.concatenate([x, x], axis=1) * bdr

    qr = _rot(q, cos, sins)
    kr = _rot(k, cos, sins) * (HD ** -0.5)
    sc = _bmm_nt(qr, bdiag(kr)) * intra
    o = _bmm(qr * qdec, s) + _bmm(sc, bdiag(v))
    s_new = s * cd + bd2 * _bmm_tn(kr * kdec, v)
    return s_new, o


def _ret_finish(o_f, o_b, z, norm_g, bd):
    o = o_f + o_b
    mu = _head_sum(o, bd) * (1.0 / HD)
    xc = o - mu
    var = _head_sum(xc * xc, bd) * (1.0 / HD)
    return xc * lax.rsqrt(var + EPS) * norm_g * _silu(z)


def _softplus(x):
    return jnp.maximum(x, 0.0) + jnp.log(1.0 + jnp.exp(-jnp.abs(x)))


def _bmm(a, b, precision=None):
    return lax.dot_general(a, b, (((2,), (1,)), ((0,), (0,))), precision=precision, preferred_element_type=f32)


def _bmm_nt(a, b, precision=None):
    return lax.dot_general(a, b, (((2,), (2,)), ((0,), (0,))), precision=precision, preferred_element_type=f32)


def _bmm_tn(a, b, precision=None):
    return lax.dot_general(a, b, (((1,), (1,)), ((0,), (0,))), precision=precision, preferred_element_type=f32)


def _mm(a, b, mode):
    ca, cb = {"nn": (1, 0), "nt": (1, 1), "tn": (0, 0)}[mode]
    if a.ndim == 3:
        dims = (((ca + 1,), (cb + 1,)), ((0,), (0,)))
    else:
        dims = (((ca,), (cb,)), ((), ()))
    return lax.dot_general(a, b, dims, preferred_element_type=f32)


def _split(a):
    hi = a.astype(bf16)
    return hi, (a - hi.astype(f32)).astype(bf16)


def _sel2(a, e, mode, e_left):
    hi, lo = _split(a)
    eb = e.astype(bf16)
    if e_left:
        return _mm(eb, hi, mode) + _mm(eb, lo, mode)
    return _mm(hi, eb, mode) + _mm(lo, eb, mode)


@jax.custom_vjp
def _sel_r(a, e):
    return _sel2(a, e, "nn", False)


_sel_r.defvjp(lambda a, e: (_sel_r(a, e), e), lambda e, g: (_sel2(g, e, "nt", False), jnp.zeros_like(e)))


@jax.custom_vjp
def _sel_l(e, b):
    return _sel2(b, e, "nn", True)


_sel_l.defvjp(lambda e, b: (_sel_l(e, b), e), lambda e, g: (jnp.zeros_like(e), _sel2(g, e, "tn", True)))


def _bdiag(x, bd2):
    return jnp.concatenate([x, x], axis=1) * bd2


@jax.custom_vjp
def _solve_given_inv(m, vb, kbg, inv, bd2):
    w_ = vb.shape[-1]
    uw = _bmm(inv, jnp.concatenate([_bdiag(vb, bd2), _bdiag(kbg, bd2)], axis=2), P3)
    return uw[:, :, :w_], uw[:, :, w_:]


def _solve_fwd(m, vb, kbg, inv, bd2):
    u, w = _solve_given_inv(m, vb, kbg, inv, bd2)
    return (u, w), (inv, u, w, bd2)


def _solve_bwd(res, cts):
    inv, u, w, bd2 = res
    du, dw = cts
    c = inv.shape[1]
    t = jnp.swapaxes(_bdiag(inv, bd2), 1, 2)
    inv_t = t[:, :c] + t[:, c:]
    w_ = du.shape[-1]
    both = _bmm(inv_t, jnp.concatenate([_bdiag(du, bd2), _bdiag(dw, bd2)], axis=2), P3)
    dm = _bmm_nt(both, jnp.concatenate([_bdiag(u, bd2), _bdiag(w, bd2)], axis=2), P3)
    return dm, both[:, :, :w_], both[:, :, w_:], jnp.zeros_like(inv), jnp.zeros_like(bd2)


_solve_given_inv.defvjp(_solve_fwd, _solve_bwd)


def _gdn_step(s, q, k, v, gate, alog, dtb, tmask, tmask2, strict2, exp_g, exp_b, exp_gt, dsel2, eye2, bd2, inv=None,
              narrow=False):
    z, c, w_ = q.shape
    ne = gate.shape[0]

    def per_pair(a):
        return jnp.broadcast_to(a[:, None], (ne, z // ne) + a.shape[1:]).reshape((z,) + a.shape[1:])

    def rows(a):
        return a.reshape(z * c, w_)

    def bdiag(x):
        return _bdiag(x, bd2)

    g = per_pair(-jnp.exp(alog) * _softplus(gate + dtb))
    beta = per_pair(jax.nn.sigmoid(gate))
    gl = _sel_r(g, exp_g)
    gc_l = _sel_l(tmask, gl)
    glast_l = jnp.sum(gl, axis=1, keepdims=True)
    glast = jnp.sum(g, axis=1, keepdims=True)
    beta_l = _sel_r(beta, exp_b)
    gc_r = jnp.sum(gc_l * dsel2, axis=1, keepdims=True)
    qn = q * lax.rsqrt(_sel_r(rows(q * q), bd2).reshape(z, c, w_) + EPS)
    kn = k * lax.rsqrt(_sel_r(rows(k * k), bd2).reshape(z, c, w_) + EPS)
    eg = jnp.exp(gc_l)
    kb = kn * beta_l
    vb = v * beta_l
    kbg = kb * eg
    qs = qn * (HD ** -0.5)
    dec = jnp.exp(jnp.where(tmask2 > 0, gc_l - gc_r, -1e30))
    kns = bdiag(kn)
    def nr(x):
        return x.astype(bf16) if narrow else x

    kns_n, s_n = nr(kns), nr(s)
    m = -(_bmm_nt(nr(kb), kns_n) * dec * strict2)
    if inv is None:
        inv = eye2 + m
        p = _bmm(m, bdiag(m), P3)
        for _ in range(4):
            both = _bmm(jnp.concatenate([p, inv], axis=1), bdiag(p), P3)
            inv = inv + both[:, c:]
            p = both[:, :c]
        inv = inv + _bmm(inv, bdiag(p), P3)
        uw = _bmm(inv, jnp.concatenate([bdiag(vb), bdiag(kbg)], axis=2), P3)
        u, w = uw[:, :, :w_], uw[:, :, w_:]
    else:
        u, w = _solve_given_inv(m, vb, kbg, inv, bd2)
    v_new = u - _bmm(nr(w), s_n)
    k_tail = kn * jnp.exp(glast_l - gc_l)
    cdec = jnp.sum(exp_gt * jnp.exp(glast), axis=-1, keepdims=True)
    v_n = nr(v_new)
    s_new = s * cdec + bd2 * _bmm_tn(nr(k_tail), v_n)
    a = _bmm_nt(nr(qs), kns_n) * dec
    o = _bmm(nr(qs * eg), s_n) + _bmm(nr(a), nr(bdiag(v_new)))
    return s_new, o, inv


def _gdn_finish(o_f, o_b, z, norm_g, bd):
    o = o_f + o_b
    ms = _head_sum(o * o, bd) * (1.0 / HD)
    return o * lax.rsqrt(ms + EPS) * norm_g * _silu(z)


def _gelu(x):
    return 0.5 * x * (1.0 + jnp.tanh(0.7978845608028654 * (x + 0.044715 * (x * x * x))))


def _sg_block(u0, u1, v0, v1, z0, z1, w, b, hmp, bdr):
    ts = u0.shape[0]
    nc = ts // RC
    g0, g1 = _gelu(v0), _gelu(v1)
    mu = (jnp.sum(g0, axis=-1, keepdims=True) + jnp.sum(g1, axis=-1, keepdims=True)) * (1.0 / BW)
    x0, x1 = g0 - mu, g1 - mu
    var = (jnp.sum(x0 * x0, axis=-1, keepdims=True) + jnp.sum(x1 * x1, axis=-1, keepdims=True)) * (1.0 / BW)
    rstd = lax.rsqrt(var + EPS)
    ys = []
    for p, (u, xc, z) in enumerate(((u0, x0, z0), (u1, x1, z1))):
        vn = (xc * rstd).reshape(nc, RC, PAIR_W)
        wp = jnp.concatenate([w[2 * p], w[2 * p + 1]], axis=1)
        mix = _bmm(jnp.broadcast_to(wp, (nc, RC, 2 * RC)), jnp.concatenate([vn, vn], axis=1) * bdr)
        bias = _dot_tn(b, hmp[p], precision=HI)
        s = (mix + bias).reshape(ts, PAIR_W)
        ys.append(_gelu(u) * s * _silu(z))
    return ys[0], ys[1]


def _make_shifts(t_ctx, n):
    def dn(x):
        t = lax.broadcasted_iota(jnp.int32, x.shape, 0)
        return jnp.where((t != 0) & (t != t_ctx), pltpu.roll(x, 1, axis=0), 0.0)

    def up(x):
        t = lax.broadcasted_iota(jnp.int32, x.shape, 0)
        return jnp.where((t != t_ctx - 1) & (t != n - 1), pltpu.roll(x, n - 1, axis=0), 0.0)

    @jax.custom_vjp
    def shift_dn(x):
        return dn(x)
    shift_dn.defvjp(lambda x: (dn(x), None), lambda _, g: (up(g),))

    @jax.custom_vjp
    def shift_up(x):
        return up(x)
    shift_up.defvjp(lambda x: (up(x), None), lambda _, g: (dn(g),))
    return shift_dn, shift_up


def _conv3(x, w, shift_dn, shift_up):
    return shift_dn(x) * w[0:1] + x * w[1:2] + shift_up(x) * w[2:3]


def inproj_fwd(x, shift_t, scale_t, g_pre, w_in, n_batch, sb):
    n = x.shape[0]

    def sel(i):
        return jnp.where(i % sb == 0, n_batch, i // sb)

    def body(x_ref, sh0, sh1, sc0, sc1, g_ref, w_ref, p_ref, h_ref):
        hs = []
        for k, (sh_ref, sc_ref) in enumerate(((sh0, sc0), (sh1, sc1))):
            xv = x_ref[k * TM:(k + 1) * TM, :]
            r = xv * lax.rsqrt(jnp.mean(xv * xv, axis=-1, keepdims=True) + EPS)
            hs.append(((r * g_ref[...]) * (1.0 + sc_ref[0]) + sh_ref[0]).astype(bf16))
        hb = jnp.concatenate(hs, axis=0)
        h_ref[...] = hb
        p_ref[...] = _dot(hb, w_ref[...])

    def mrow(k):
        return pl.BlockSpec((1, 1, D), lambda i: (sel(2 * i + k), 0, 0))

    return pl.pallas_call(
        body, name="inproj_fwd", grid=(n // TP,),
        in_specs=[pl.BlockSpec((TP, D), lambda i: (i, 0)), mrow(0), mrow(1), mrow(0), mrow(1),
                  _full((1, D)), _full((D, PW))],
        out_specs=[pl.BlockSpec((TP, PW), lambda i: (i, 0)), pl.BlockSpec((TP, D), lambda i: (i, 0))],
        out_shape=[_sds((n, PW)), _sds((n, D), bf16)],
        compiler_params=_cparams(("arbitrary",), VMEM_BIG),
    )(x, shift_t, shift_t, scale_t, scale_t, g_pre, w_in)


def outproj_fwd(ys, w_out, x, gate_t, g_post, n_batch, sb):
    n = x.shape[0]

    def sel(i):
        return jnp.where(i % sb == 0, n_batch, i // sb)

    def body(y0, y1, y2, y3, w_ref, x_ref, gt0, gt1, g_ref, xn_ref, o_ref):
        y = jnp.concatenate([y0[...], y1[...], y2[...], y3[...]], axis=1)
        o = _dot(y, w_ref[...])
        o_ref[...] = o
        nrm = o * lax.rsqrt(jnp.mean(o * o, axis=-1, keepdims=True) + EPS) * g_ref[...]
        for k, gt_ref in enumerate((gt0, gt1)):
            rows = slice(k * TM, (k + 1) * TM)
            xn_ref[rows, :] = x_ref[rows, :] + gt_ref[0] * nrm[rows]

    def mrow(k):
        return pl.BlockSpec((1, 1, D), lambda i: (sel(2 * i + k), 0, 0))

    yspec = pl.BlockSpec((TP, BW), lambda i: (i, 0))
    return pl.pallas_call(
        body, name="outproj_fwd", grid=(n // TP,),
        in_specs=[yspec, yspec, yspec, yspec, _full((D, D)), pl.BlockSpec((TP, D), lambda i: (i, 0)),
                  mrow(0), mrow(1), _full((1, D))],
        out_specs=[pl.BlockSpec((TP, D), lambda i: (i, 0)), pl.BlockSpec((TP, D), lambda i: (i, 0))],
        out_shape=[_sds((n, D)), _sds((n, D))],
        compiler_params=_cparams(("arbitrary",), VMEM_BIG),
    )(*ys, w_out, x, gate_t, gate_t, g_post)


def _row_onehot(r):
    return lax.broadcasted_iota(jnp.int32, (8, 1), 0) == r


def outproj_bwd(dxn, o, gate_t, g_post, ys, w_out, n_batch, sb):
    n = dxn.shape[0]

    def sel(i):
        return jnp.where(i % sb == 0, n_batch, i // sb)

    def body(dxn_ref, o_ref, gt0, gt1, g_ref, y0, y1, y2, y3, w_ref, dy_ref, dw_ref, dg_ref, dgate_ref):
        i = pl.program_id(0)

        @pl.when(i == 0)
        def _():
            dw_ref[...] = jnp.zeros_like(dw_ref)
            dg_ref[...] = jnp.zeros_like(dg_ref)
            dgate_ref[...] = jnp.zeros_like(dgate_ref)

        g = g_ref[...]
        dos = []
        for k, gt_ref in enumerate((gt0, gt1)):
            rows = slice(k * TM, (k + 1) * TM)
            ov = o_ref[rows, :]
            rstd = lax.rsqrt(jnp.mean(ov * ov, axis=-1, keepdims=True) + EPS)
            r = ov * rstd
            dx = dxn_ref[rows, :]
            dgate_ref[...] += jnp.where(_row_onehot(sel(2 * i + k)), jnp.sum(dx * (r * g), axis=0, keepdims=True), 0.0)
            dn = dx * gt_ref[0]
            dg_ref[...] += jnp.sum(dn * r, axis=0, keepdims=True)
            dr = dn * g
            dos.append((rstd * (dr - r * jnp.mean(dr * r, axis=-1, keepdims=True))).astype(bf16))
        dob = jnp.concatenate(dos, axis=0)
        dy_ref[...] = _dot_nt(dob, w_ref[...])
        y = jnp.concatenate([y0[...], y1[...], y2[...], y3[...]], axis=1)
        dw_ref[...] += _dot_tn(y, dob)

    def mrow(k):
        return pl.BlockSpec((1, 1, D), lambda i: (sel(2 * i + k), 0, 0))

    yspec = pl.BlockSpec((TP, BW), lambda i: (i, 0))
    row = pl.BlockSpec((TP, D), lambda i: (i, 0))
    return pl.pallas_call(
        body, name="outproj_bwd", grid=(n // TP,),
        in_specs=[row, row, mrow(0), mrow(1), _full((1, D)), yspec, yspec, yspec, yspec, _full((D, D))],
        out_specs=[row, _full((D, D)), _full((1, D)), _full((8, D))],
        out_shape=[_sds((n, D)), _sds((D, D)), _sds((1, D)), _sds((8, D))],
        compiler_params=_cparams(("arbitrary",), VMEM_BIG),
    )(dxn, o, gate_t, gate_t, g_post, *ys, w_out)


def inproj_bwd_x(dp, w_in, x, scale_t, g_pre, dxn, n_batch, sb, xchg=None):
    n = x.shape[0]

    def sel(i):
        return jnp.where(i % sb == 0, n_batch, i // sb)

    def body(dp_ref, w_ref, x_ref, sc0, sc1, g_ref, dxn_ref, dx_ref, dg_ref, dsh_ref, dsc_ref):
        i = pl.program_id(0)

        @pl.when(i == 0)
        def _():
            dg_ref[...] = jnp.zeros_like(dg_ref)
            dsh_ref[...] = jnp.zeros_like(dsh_ref)
            dsc_ref[...] = jnp.zeros_like(dsc_ref)

        dh_all = _dot_nt(dp_ref[...], w_ref[...])
        g = g_ref[...]
        for k, sc_ref in enumerate((sc0, sc1)):
            rows = slice(k * TM, (k + 1) * TM)
            dh = dh_all[rows]
            xv = x_ref[rows, :]
            rstd = lax.rsqrt(jnp.mean(xv * xv, axis=-1, keepdims=True) + EPS)
            r = xv * rstd
            hot = _row_onehot(sel(2 * i + k))
            dsh_ref[...] += jnp.where(hot, jnp.sum(dh, axis=0, keepdims=True), 0.0)
            dsc_ref[...] += jnp.where(hot, jnp.sum(dh * (r * g), axis=0, keepdims=True), 0.0)
            t = dh * (1.0 + sc_ref[0])
            dg_ref[...] += jnp.sum(t * r, axis=0, keepdims=True)
            dr = t * g
            dx_ref[rows, :] = dxn_ref[rows, :] + rstd * (dr - r * jnp.mean(dr * r, axis=-1, keepdims=True))

    def mrow(k):
        return pl.BlockSpec((1, 1, D), lambda i: (sel(2 * i + k), 0, 0))

    row = pl.BlockSpec((TP, D), lambda i: (i, 0))
    fused, x_in, x_out, x_shape, x_scratch = _with_exchange(body, 7, 4, 0, xchg, n // TP)
    return pl.pallas_call(
        fused, name="inproj_bwd_x" + ("" if xchg is None else "_" + xchg[0]), grid=(n // TP,),
        in_specs=[pl.BlockSpec((TP, PW), lambda i: (i, 0)), _full((D, PW)), row, mrow(0), mrow(1), _full((1, D)), row]
                 + x_in,
        out_specs=[row, _full((1, D)), _full((8, D)), _full((8, D))] + x_out,
        out_shape=[_sds((n, D)), _sds((1, D)), _sds((8, D)), _sds((8, D))] + x_shape,
        scratch_shapes=x_scratch,
        compiler_params=_cparams(("arbitrary",), VMEM_BIG),
    )(dp, w_in, x, scale_t, scale_t, g_pre, dxn, *([] if xchg is None else xchg[1]))


def dw_in(h, dp):
    n = h.shape[0]
    tk, tn = (1536 if n % 1536 == 0 else 512), 1024
    nk = n // tk

    def body(h_ref, dp_ref, o_ref, acc):
        k = pl.program_id(1)

        @pl.when(k == 0)
        def _():
            acc[...] = jnp.zeros_like(acc)
        acc[...] += _dot_tn(dp_ref[...], h_ref[...])

        @pl.when(k == nk - 1)
        def _():
            o_ref[...] = acc[...].astype(bf16)

    return pl.pallas_call(
        body, name="dw_in", grid=(PW // tn, nk),
        in_specs=[pl.BlockSpec((tk, D), lambda j, k: (k, 0)), pl.BlockSpec((tk, tn), lambda j, k: (k, j))],
        out_specs=pl.BlockSpec((tn, D), lambda j, k: (j, 0)),
        out_shape=_sds((PW, D), bf16),
        scratch_shapes=[pltpu.VMEM((tn, D), f32)],
        compiler_params=_cparams(("parallel", "arbitrary"), VMEM_BIG),
    )(h, dp)


def place_weights(slabs):
    n_ch, d, wc = slabs.shape

    def body(w_ref, o_ref):
        acc = jnp.pad(w_ref[0].astype(f32), ((0, 0), (0, PW - wc)))
        for k in range(1, n_ch):
            acc = acc + pltpu.roll(jnp.pad(w_ref[k].astype(f32), ((0, 0), (0, PW - wc))), wc * k, axis=1)
        o_ref[...] = acc.astype(bf16)

    return pl.pallas_call(
        body, name="place_weights", grid=(d // TM,),
        in_specs=[pl.BlockSpec((n_ch, TM, wc), lambda i: (0, i, 0))],
        out_specs=pl.BlockSpec((TM, PW), lambda i: (i, 0)),
        out_shape=_sds((d, PW), bf16),
        compiler_params=_cparams(("arbitrary",), VMEM_BIG),
    )(slabs)


def loss_head(xf, target, t_ctx):
    nb, s, _ = xf.shape
    jc = t_ctx // TM

    def body(x_ref, t_ref, dx_ref, l_ref):
        b, j = pl.program_id(0), pl.program_id(1)

        @pl.when((b == 0) & (j == 0))
        def _():
            l_ref[...] = jnp.zeros_like(l_ref)

        @pl.when(j < jc)
        def _():
            dx_ref[...] = jnp.zeros_like(dx_ref)

        @pl.when(j >= jc)
        def _():
            diff = x_ref[0] - t_ref[0]
            dx_ref[0] = diff * (1.0 / D)
            l_ref[...] += 0.5 * jnp.sum(diff * diff) * (1.0 / D)

    return pl.pallas_call(
        body, name="loss_head", grid=(nb, s // TM),
        in_specs=[pl.BlockSpec((1, TM, D), lambda b, j: (b, j, 0)),
                  pl.BlockSpec((1, TM, D), lambda b, j: (b, jnp.maximum(j - jc, 0), 0))],
        out_specs=[pl.BlockSpec((1, TM, D), lambda b, j: (b, j, 0)), _full((1, 128))],
        out_shape=[_sds((nb, s, D)), _sds((1, 128))],
        compiler_params=_cparams(("arbitrary", "arbitrary")),
    )(xf, target)


def _chunk_maps(n_ctx, n_lat):
    n = n_ctx + n_lat

    def cf(t):
        return t

    def cb(t):
        return jnp.where(t < n_ctx, n_ctx - 1 - t, n - 1 - t + n_ctx)
    return n, cf, cb


def ret_scan_fwd(p3, cos, sins, consts, t_ctx):
    nb, s, _ = p3.shape
    n, cf, cb = _chunk_maps(t_ctx // RC, (s - t_ctx) // RC)
    nz = 4 * nb

    def body(qf, kf, vf, qb, kb, vb, cosf, sinf, cosb, sinb, intra_r, qdec_r, kdec_r, cd_r, bd_r, bdr_r,
             of_ref, ob_ref, sall_ref, s_sc):
        @pl.when(pl.program_id(0) == 0)
        def _():
            s_sc[...] = jnp.zeros_like(s_sc)
        st = s_sc[...]
        sall_ref[0] = st
        s_new, o = _ret_step(st, _pairs(qf, qb, nb), _pairs(kf, kb, nb), _pairs(vf, vb, nb),
                             _pair_tables(cosf, cosb, nb), _pair_tables(sinf, sinb, nb), intra_r[...], qdec_r[...],
                             kdec_r[...], cd_r[...], bd_r[...], bdr_r[...])
        s_sc[...] = s_new
        _unpairs(o, of_ref, ob_ref, nb)

    def pspec(m, seg):
        return pl.BlockSpec((nb, RC, BW), lambda t: (0, m(t), seg))

    def tspec(m):
        return pl.BlockSpec((RC, PAIR_W), lambda t: (m(t), 0))

    return pl.pallas_call(
        body, name="ret_scan_fwd", grid=(n,),
        in_specs=[pspec(cf, 0), pspec(cf, 1), pspec(cf, 2), pspec(cb, 0), pspec(cb, 1), pspec(cb, 2),
                  tspec(cf), tspec(cf), tspec(cb), tspec(cb)] + [_full(c.shape) for c in consts],
        out_specs=[pl.BlockSpec((nb, RC, BW), lambda t: (0, cf(t), 0)),
                   pl.BlockSpec((nb, RC, BW), lambda t: (0, cb(t), 0)),
                   pl.BlockSpec((1, nz, PAIR_W, PAIR_W), lambda t: (t, 0, 0, 0))],
        out_shape=[_sds((nb, s, BW)), _sds((nb, s, BW)), _sds((n, nz, PAIR_W, PAIR_W))],
        scratch_shapes=[pltpu.VMEM((nz, PAIR_W, PAIR_W), f32)],
        compiler_params=_cparams(("arbitrary",)),
    )(p3, p3, p3, p3, p3, p3, cos, sins, cos, sins, *consts)


def ret_scan_bwd(p3, cos, sins, consts, s_all, do, t_ctx):
    nb, s, _ = p3.shape
    n, cf, cb = _chunk_maps(t_ctx // RC, (s - t_ctx) // RC)
    nz = 4 * nb

    def rf(t):
        return cf(n - 1 - t)

    def rb(t):
        return cb(n - 1 - t)

    def body(qf, kf, vf, qb, kb, vb, cosf, sinf, cosb, sinb, intra_r, qdec_r, kdec_r, cd_r, bd_r, bdr_r,
             sall_ref, dof, dob, dqf, dkf, dvf, dqb, dkb, dvb, ds_sc):
        @pl.when(pl.program_id(0) == 0)
        def _():
            ds_sc[...] = jnp.zeros_like(ds_sc)
        step = functools.partial(_ret_step, cos=_pair_tables(cosf, cosb, nb), sins=_pair_tables(sinf, sinb, nb),
                                 intra=intra_r[...], qdec=qdec_r[...], kdec=kdec_r[...], cd=cd_r[...], bd2=bd_r[...],
                                 bdr=bdr_r[...])
        _, vjp = jax.vjp(step, sall_ref[0], _pairs(qf, qb, nb), _pairs(kf, kb, nb), _pairs(vf, vb, nb))
        ds, dq, dk, dv = vjp((ds_sc[...], _pairs(dof, dob, nb)))
        ds_sc[...] = ds
        _unpairs(dq, dqf, dqb, nb)
        _unpairs(dk, dkf, dkb, nb)
        _unpairs(dv, dvf, dvb, nb)

    def pspec(m, seg):
        return pl.BlockSpec((nb, RC, BW), lambda t: (0, m(t), seg))

    def tspec(m):
        return pl.BlockSpec((RC, PAIR_W), lambda t: (m(t), 0))

    def ospec(m):
        return pl.BlockSpec((nb, RC, BW), lambda t: (0, m(t), 0))

    return pl.pallas_call(
        body, name="ret_scan_bwd", grid=(n,),
        in_specs=[pspec(rf, 0), pspec(rf, 1), pspec(rf, 2), pspec(rb, 0), pspec(rb, 1), pspec(rb, 2),
                  tspec(rf), tspec(rf), tspec(rb), tspec(rb)] + [_full(c.shape) for c in consts]
                 + [pl.BlockSpec((1, nz, PAIR_W, PAIR_W), lambda t: (n - 1 - t, 0, 0, 0)), ospec(rf), ospec(rb)],
        out_specs=[ospec(rf), ospec(rf), ospec(rf), ospec(rb), ospec(rb), ospec(rb)],
        out_shape=[_sds((nb, s, BW), bf16)] * 6,
        scratch_shapes=[pltpu.VMEM((nz, PAIR_W, PAIR_W), f32)],
        compiler_params=_cparams(("arbitrary",), VMEM_BIG),
    )(p3, p3, p3, p3, p3, p3, cos, sins, cos, sins, *consts, s_all, do, do)


def mix_finish_fwd(fn, name, o_f, o_b, p3, zseg, norm_g, bd):
    nb, s, _ = p3.shape

    def body(of_ref, ob_ref, z_ref, g_ref, bd_ref, y_ref):
        y_ref[0] = fn(of_ref[0], ob_ref[0], z_ref[0], g_ref[...], bd_ref[...]).astype(bf16)

    blk = pl.BlockSpec((1, TM, BW), lambda b, j: (b, j, 0))
    return pl.pallas_call(
        body, name=name, grid=(nb, s // TM),
        in_specs=[blk, blk, pl.BlockSpec((1, TM, BW), lambda b, j: (b, j, zseg)), _full((1, BW)), _full((BW, BW))],
        out_specs=blk, out_shape=_sds((nb, s, BW), bf16),
        compiler_params=_cparams(("arbitrary", "arbitrary")),
    )(o_f, o_b, p3, norm_g, bd)


def mix_finish_bwd(fn, name, o_f, o_b, p3, zseg, norm_g, bd, dy3, yseg):
    nb, s, _ = p3.shape

    def body(of_ref, ob_ref, z_ref, g_ref, bd_ref, dy_ref, do_ref, dz_ref, dg_ref):
        @pl.when((pl.program_id(0) == 0) & (pl.program_id(1) == 0))
        def _():
            dg_ref[...] = jnp.zeros_like(dg_ref)
        bdv = bd_ref[...]
        _, vjp = jax.vjp(lambda a, b, z, g: fn(a, b, z, g, bdv), of_ref[0], ob_ref[0], z_ref[0], g_ref[...])
        do, _, dz, dg = vjp(dy_ref[0])
        do_ref[0] = do
        dz_ref[0] = dz.astype(bf16)
        dg_ref[...] += dg

    blk = pl.BlockSpec((1, TM, BW), lambda b, j: (b, j, 0))
    return pl.pallas_call(
        body, name=name, grid=(nb, s // TM),
        in_specs=[blk, blk, pl.BlockSpec((1, TM, BW), lambda b, j: (b, j, zseg)), _full((1, BW)), _full((BW, BW)),
                  pl.BlockSpec((1, TM, BW), lambda b, j: (b, j, yseg))],
        out_specs=[blk, blk, _full((1, BW))],
        out_shape=[_sds((nb, s, BW)), _sds((nb, s, BW), bf16), _sds((1, BW))],
        compiler_params=_cparams(("arbitrary", "arbitrary")),
    )(o_f, o_b, p3, norm_g, bd, dy3)


GDN_QKV = 11 * BW // 128
N_QKV = 3 * BW // 128


def gdn_conv_fwd(p3, w, t_ctx):
    nb, s, _ = p3.shape
    sd, su = _make_shifts(t_ctx, s)

    def body(x_ref, w_ref, o_ref):
        o_ref[0] = _silu(_conv3(x_ref[0], w_ref[...], sd, su))

    return pl.pallas_call(
        body, name="gdn_conv_fwd", grid=(nb, N_QKV),
        in_specs=[pl.BlockSpec((1, s, 128), lambda b, j: (b, 0, GDN_QKV + j)), pl.BlockSpec((3, 128), lambda b, j: (0, j))],
        out_specs=pl.BlockSpec((1, s, 128), lambda b, j: (b, 0, j)),
        out_shape=_sds((nb, s, 3 * BW)),
        compiler_params=_cparams(("arbitrary", "arbitrary")),
    )(p3, w)


def gdn_conv_bwd(p3, w, d_f, d_b, t_ctx):
    nb, s, _ = p3.shape
    sd, su = _make_shifts(t_ctx, s)

    def body(x_ref, w_ref, df_ref, db_ref, dx_ref, dw_ref):
        @pl.when(pl.program_id(1) == 0)
        def _():
            dw_ref[...] = jnp.zeros_like(dw_ref)
        _, vjp = jax.vjp(lambda x, w_: _silu(_conv3(x, w_, sd, su)), x_ref[0], w_ref[...])
        dx, dw = vjp(df_ref[0] + db_ref[0])
        dx_ref[0] = dx.astype(bf16)
        dw_ref[...] += dw

    blk = pl.BlockSpec((1, s, 128), lambda j, b: (b, 0, j))
    return pl.pallas_call(
        body, name="gdn_conv_bwd", grid=(N_QKV, nb),
        in_specs=[pl.BlockSpec((1, s, 128), lambda j, b: (b, 0, GDN_QKV + j)), pl.BlockSpec((3, 128), lambda j, b: (0, j)),
                  blk, blk],
        out_specs=[blk, pl.BlockSpec((3, 128), lambda j, b: (0, j))],
        out_shape=[_sds((nb, s, 3 * BW), bf16), _sds((3, 3 * BW))],
        compiler_params=_cparams(("arbitrary", "arbitrary"), VMEM_BIG),
    )(p3, w, d_f, d_b)


def _pairs(f_ref, b_ref, nb):
    return jnp.stack([r[b, :, PAIR_W * p:PAIR_W * (p + 1)] for b in range(nb) for r in (f_ref, b_ref) for p in range(2)])


def _pair_tables(f_ref, b_ref, nb):
    return jnp.stack([r[...] for _ in range(nb) for r in (f_ref, b_ref) for _ in range(2)])


def _gates(f_ref, b_ref, nb):
    return jnp.stack([r[b] for b in range(nb) for r in (f_ref, b_ref)])


def _unpairs(a, f_ref, b_ref, nb, lane0=0):
    for b in range(nb):
        for d, r in enumerate((f_ref, b_ref)):
            for p in range(2):
                r[b, :, lane0 + PAIR_W * p:lane0 + PAIR_W * (p + 1)] = a[4 * b + 2 * d + p].astype(r.dtype)


def _with_exchange(body, n_in, n_out, n_scratch, xchg, n_steps):
    if xchg is None:
        return body, [], [], [], []
    kind, arrs = xchg
    nx = len(arrs)

    def fused(*refs):
        ins, rest = refs[:n_in], refs[n_in:]
        srcs, rest = rest[:nx], rest[nx:]
        outs, rest = rest[:n_out], rest[n_out:]
        dsts, rest = rest[:nx], rest[nx:]
        scratch, sems = rest[:n_scratch], rest[n_scratch:]
        start, wait = _peer_exchange(kind, "chips", srcs, dsts, *sems)
        pl.when(pl.program_id(0) == 0)(start)
        body(*ins, *outs, *scratch)
        pl.when(pl.program_id(0) == n_steps - 1)(wait)

    any_ = pl.BlockSpec(memory_space=pl.ANY)
    return fused, [any_] * nx, [any_] * nx, _exchange_shapes(kind, "chips", arrs), _exchange_scratch("chips", nx)


def gdn_scan_fwd(c3, p3, alog, dtb, consts, t_ctx, xchg=None):
    nb, s, _ = p3.shape
    n, cf, cb = _chunk_maps(t_ctx // GC, (s - t_ctx) // GC)
    gblk = GATE_COL // 128

    nz = 4 * nb

    def body(qf, kf, vf, gf, qb, kb, vb, gb, al_ref, dt_ref, tm_r, tm2_r, st2_r, eg_r, eb_r, egt_r, dsel_r, eye_r, bd_r,
             of_ref, ob_ref, sall_ref, inv_ref, s_sc):
        @pl.when(pl.program_id(0) == 0)
        def _():
            s_sc[...] = jnp.zeros_like(s_sc)
        st = s_sc[...]
        sall_ref[0] = st
        s_new, o, inv = _gdn_step(st, _pairs(qf, qb, nb), _pairs(kf, kb, nb), _pairs(vf, vb, nb), _gates(gf, gb, nb),
                                  al_ref[...], dt_ref[...], tm_r[...], tm2_r[...], st2_r[...], eg_r[...], eb_r[...],
                                  egt_r[...], dsel_r[...], eye_r[...], bd_r[...], narrow=True)
        s_sc[...] = s_new
        inv_ref[0] = inv
        _unpairs(o, of_ref, ob_ref, nb)

    def cspec(m, col=0):
        return pl.BlockSpec((nb, GC, BW), lambda t: (0, m(t), col))

    def gspec(m):
        return pl.BlockSpec((nb, GC, 128), lambda t: (0, m(t), gblk))

    fused, x_in, x_out, x_shape, x_scratch = _with_exchange(body, 10 + len(consts), 4, 1, xchg, n)
    return pl.pallas_call(
        fused, name="gdn_scan_fwd" + ("" if xchg is None else "_" + xchg[0]), grid=(n,),
        in_specs=[cspec(cf, 0), cspec(cf, 1), cspec(cf, 2), gspec(cf), cspec(cb, 0), cspec(cb, 1), cspec(cb, 2), gspec(cb),
                  _full((1, 128)), _full((1, 128))] + [_full(c.shape) for c in consts] + x_in,
        out_specs=[cspec(cf), cspec(cb), pl.BlockSpec((1, nz, PAIR_W, PAIR_W), lambda t: (t, 0, 0, 0)),
                   pl.BlockSpec((1, nz, GC, PAIR_W), lambda t: (t, 0, 0, 0))] + x_out,
        out_shape=[_sds((nb, s, BW)), _sds((nb, s, BW)), _sds((n, nz, PAIR_W, PAIR_W)), _sds((n, nz, GC, PAIR_W))]
                  + x_shape,
        scratch_shapes=[pltpu.VMEM((nz, PAIR_W, PAIR_W), f32)] + x_scratch,
        compiler_params=_cparams(("arbitrary",)),
    )(c3, c3, c3, p3, c3, c3, c3, p3, alog, dtb, *consts, *([] if xchg is None else xchg[1]))


def gdn_scan_bwd(c3, p3, alog, dtb, consts, s_all, inv_all, do, t_ctx, xchg=None):
    nb, s, _ = p3.shape
    n, cf, cb = _chunk_maps(t_ctx // GC, (s - t_ctx) // GC)
    gblk = GATE_COL // 128

    def rf(t):
        return cf(n - 1 - t)

    def rb(t):
        return cb(n - 1 - t)

    nz = 4 * nb

    def body(qf, kf, vf, gf, qb, kb, vb, gb, al_ref, dt_ref, tm_r, tm2_r, st2_r, eg_r, eb_r, egt_r, dsel_r, eye_r, bd_r,
             sall_ref, inv_ref, dof, dob, dcf, dgf, dcb, dgb, dal_ref, ddt_ref, ds_sc):
        @pl.when(pl.program_id(0) == 0)
        def _():
            dal_ref[...] = jnp.zeros_like(dal_ref)
            ddt_ref[...] = jnp.zeros_like(ddt_ref)
            ds_sc[...] = jnp.zeros_like(ds_sc)
        consts = dict(tmask=tm_r[...], tmask2=tm2_r[...], strict2=st2_r[...], exp_g=eg_r[...], exp_b=eb_r[...],
                      exp_gt=egt_r[...], dsel2=dsel_r[...], eye2=eye_r[...], bd2=bd_r[...], inv=inv_ref[0])

        def step(*a):
            return _gdn_step(*a, **consts)[:2]

        _, vjp = jax.vjp(step, sall_ref[0], _pairs(qf, qb, nb), _pairs(kf, kb, nb), _pairs(vf, vb, nb),
                         _gates(gf, gb, nb), al_ref[...], dt_ref[...])
        ds, dq, dk, dv, dg, dal, ddt = vjp((ds_sc[...], _pairs(dof, dob, nb)))
        ds_sc[...] = ds
        for i, a in enumerate((dq, dk, dv)):
            _unpairs(a, dcf, dcb, nb, BW * i)
        for b in range(nb):
            dgf[b] = dg[2 * b].astype(bf16)
            dgb[b] = dg[2 * b + 1].astype(bf16)
        dal_ref[...] += dal
        ddt_ref[...] += ddt

    def cspec(m, col=0):
        return pl.BlockSpec((nb, GC, BW), lambda t: (0, m(t), col))

    def gspec(m):
        return pl.BlockSpec((nb, GC, 128), lambda t: (0, m(t), gblk))

    def dcout(m):
        return pl.BlockSpec((nb, GC, 3 * BW), lambda t: (0, m(t), 0))

    def gout(m):
        return pl.BlockSpec((nb, GC, 128), lambda t: (0, m(t), 0))

    fused, x_in, x_out, x_shape, x_scratch = _with_exchange(body, 14 + len(consts), 6, 1, xchg, n)
    return pl.pallas_call(
        fused, name="gdn_scan_bwd" + ("" if xchg is None else "_" + xchg[0]), grid=(n,),
        in_specs=[cspec(rf, 0), cspec(rf, 1), cspec(rf, 2), gspec(rf), cspec(rb, 0), cspec(rb, 1), cspec(rb, 2), gspec(rb),
                  _full((1, 128)), _full((1, 128))] + [_full(c.shape) for c in consts]
                 + [pl.BlockSpec((1, nz, PAIR_W, PAIR_W), lambda t: (n - 1 - t, 0, 0, 0)),
                    pl.BlockSpec((1, nz, GC, PAIR_W), lambda t: (n - 1 - t, 0, 0, 0)), cspec(rf), cspec(rb)] + x_in,
        out_specs=[dcout(rf), gout(rf), dcout(rb), gout(rb), _full((1, 128)), _full((1, 128))] + x_out,
        out_shape=[_sds((nb, s, 3 * BW)), _sds((nb, s, 128), bf16), _sds((nb, s, 3 * BW)), _sds((nb, s, 128), bf16),
                   _sds((1, 128)), _sds((1, 128))] + x_shape,
        scratch_shapes=[pltpu.VMEM((nz, PAIR_W, PAIR_W), f32)] + x_scratch,
        compiler_params=_cparams(("arbitrary",), VMEM_BIG),
    )(c3, c3, c3, p3, c3, c3, c3, p3, alog, dtb, *consts, s_all, inv_all, do, do, *([] if xchg is None else xchg[1]))


def _sg_consts():
    hmp = np.zeros((2, NH, PAIR_W))
    for h in range(NH):
        hmp[h // 2, h, (h % 2) * HD:(h % 2 + 1) * HD] = 1.0
    bdr = (np.arange(2 * RC)[:, None] // RC == np.arange(PAIR_W)[None, :] // HD)
    return jnp.asarray(hmp, f32), jnp.asarray(bdr, f32)


def _sg_rows(s):
    return 6 * RC if s % (6 * RC) == 0 else 2 * RC


def _halves(ref):
    return ref[0, :, :PAIR_W], ref[0, :, PAIR_W:]


def sg_fwd(p3, w, b, hmp, bdr):
    nb, s, _ = p3.shape
    ts = _sg_rows(s)

    def body(u_ref, v_ref, z_ref, w_ref, b_ref, hm_ref, bdr_ref, y_ref):
        y0, y1 = _sg_block(*_halves(u_ref), *_halves(v_ref), *_halves(z_ref), w_ref[...], b_ref[...], hm_ref[...],
                           bdr_ref[...])
        y_ref[0, :, :PAIR_W] = y0.astype(bf16)
        y_ref[0, :, PAIR_W:] = y1.astype(bf16)

    def seg(k):
        return pl.BlockSpec((1, ts, BW), lambda bi, i: (bi, i, k))

    return pl.pallas_call(
        body, name="sg_fwd", grid=(nb, s // ts),
        in_specs=[seg(4), seg(5), seg(6), _full((NH, RC, RC)), _full((NH, RC)), _full(hmp.shape), _full(bdr.shape)],
        out_specs=pl.BlockSpec((1, ts, BW), lambda bi, i: (bi, i, 0)),
        out_shape=_sds((nb, s, BW), bf16),
        compiler_params=_cparams(("arbitrary", "arbitrary")),
    )(p3, p3, p3, w, b, hmp, bdr)


def sg_bwd(p3, w, b, hmp, bdr, dy3):
    nb, s, _ = p3.shape
    ts = _sg_rows(s)

    def body(u_ref, v_ref, z_ref, w_ref, b_ref, hm_ref, bdr_ref, dy_ref, du_ref, dv_ref, dz_ref, dw_ref, db_ref):
        @pl.when((pl.program_id(0) == 0) & (pl.program_id(1) == 0))
        def _():
            dw_ref[...] = jnp.zeros_like(dw_ref)
            db_ref[...] = jnp.zeros_like(db_ref)
        hm, bdr_v = hm_ref[...], bdr_ref[...]
        _, vjp = jax.vjp(lambda *a: _sg_block(*a, hm, bdr_v), *_halves(u_ref), *_halves(v_ref), *_halves(z_ref),
                         w_ref[...], b_ref[...])
        du0, du1, dv0, dv1, dz0, dz1, dw, db = vjp(_halves(dy_ref))
        for ref, a0, a1 in ((du_ref, du0, du1), (dv_ref, dv0, dv1), (dz_ref, dz0, dz1)):
            ref[0, :, :PAIR_W] = a0.astype(bf16)
            ref[0, :, PAIR_W:] = a1.astype(bf16)
        dw_ref[...] += dw
        db_ref[...] += db

    def seg(k):
        return pl.BlockSpec((1, ts, BW), lambda bi, i: (bi, i, k))

    blk = pl.BlockSpec((1, ts, BW), lambda bi, i: (bi, i, 0))
    return pl.pallas_call(
        body, name="sg_bwd", grid=(nb, s // ts),
        in_specs=[seg(4), seg(5), seg(6), _full((NH, RC, RC)), _full((NH, RC)), _full(hmp.shape), _full(bdr.shape),
                  seg(1)],
        out_specs=[blk, blk, blk, _full((NH, RC, RC)), _full((NH, RC))],
        out_shape=[_sds((nb, s, BW), bf16)] * 3 + [_sds((NH, RC, RC)), _sds((NH, RC))],
        compiler_params=_cparams(("arbitrary", "arbitrary"), VMEM_BIG),
    )(p3, p3, p3, w, b, hmp, bdr, dy3)


def _sc_fn(b, c, h, z, w, sd, su):
    return b * _conv3(c * h, w, sd, su) * _silu(z)


def sc_fwd(p3, w, t_ctx):
    nb, s, _ = p3.shape
    sd, su = _make_shifts(t_ctx, s)

    def body(b_ref, c_ref, h_ref, z_ref, w_ref, y_ref):
        y_ref[0] = _sc_fn(b_ref[0], c_ref[0], h_ref[0], z_ref[0], w_ref[...], sd, su).astype(bf16)

    def seg(k):
        return pl.BlockSpec((1, s, 128), lambda bi, j: (bi, 0, 2 * k + j))

    return pl.pallas_call(
        body, name="sc_fwd", grid=(nb, 2),
        in_specs=[seg(7), seg(8), seg(9), seg(10), pl.BlockSpec((3, 128), lambda bi, j: (0, j))],
        out_specs=pl.BlockSpec((1, s, 128), lambda bi, j: (bi, 0, j)),
        out_shape=_sds((nb, s, BW), bf16),
        compiler_params=_cparams(("arbitrary", "arbitrary"), VMEM_BIG),
    )(p3, p3, p3, p3, w)


def sc_bwd(p3, w, dy3, t_ctx):
    nb, s, _ = p3.shape
    sd, su = _make_shifts(t_ctx, s)

    def body(b_ref, c_ref, h_ref, z_ref, w_ref, dy_ref, db_ref, dc_ref, dh_ref, dz_ref, dw_ref):
        @pl.when(pl.program_id(1) == 0)
        def _():
            dw_ref[...] = jnp.zeros_like(dw_ref)
        _, vjp = jax.vjp(lambda b, c, h, z, w_: _sc_fn(b, c, h, z, w_, sd, su),
                         b_ref[0], c_ref[0], h_ref[0], z_ref[0], w_ref[...])
        db, dc, dh, dz, dw = vjp(dy_ref[0])
        db_ref[0] = db.astype(bf16)
        dc_ref[0] = dc.astype(bf16)
        dh_ref[0] = dh.astype(bf16)
        dz_ref[0] = dz.astype(bf16)
        dw_ref[...] += dw

    def seg(k):
        return pl.BlockSpec((1, s, 128), lambda j, bi: (bi, 0, 2 * k + j))

    blk = pl.BlockSpec((1, s, 128), lambda j, bi: (bi, 0, j))
    wspec = pl.BlockSpec((3, 128), lambda j, bi: (0, j))
    return pl.pallas_call(
        body, name="sc_bwd", grid=(2, nb),
        in_specs=[seg(7), seg(8), seg(9), seg(10), wspec, seg(2)],
        out_specs=[blk, blk, blk, blk, wspec],
        out_shape=[_sds((nb, s, BW), bf16)] * 4 + [_sds((3, BW))],
        compiler_params=_cparams(("arbitrary", "arbitrary"), VMEM_BIG),
    )(p3, p3, p3, p3, w, dy3)


def assemble_dp(pairs, singles_a, gdn_x, singles_b, gates):
    nb, s, _ = singles_a[0].shape
    flat = [a for pr in pairs for a in pr] + list(singles_a) + list(gdn_x) + list(singles_b) + list(gates)
    n_pairs, n_a, n_x, n_b = len(pairs), len(singles_a), len(gdn_x), len(singles_b)

    def body(*refs):
        out = refs[-1]
        ins = refs[:-1]
        col = 0
        for p in range(n_pairs):
            out[0, :, col:col + BW] = (ins[2 * p][0].astype(f32) + ins[2 * p + 1][0].astype(f32)).astype(bf16)
            col += BW
        k = 2 * n_pairs
        for _ in range(n_a + n_x + n_b):
            wk = ins[k].shape[-1]
            out[0, :, col:col + wk] = ins[k][0]
            col += wk
            k += 1
        out[0, :, col:col + 128] = (ins[k][0].astype(f32) + ins[k + 1][0].astype(f32)).astype(bf16)
        out[0, :, col + 128:] = jnp.zeros((TM, PW - col - 128), bf16)

    def spec(a):
        return pl.BlockSpec((1, TM, a.shape[-1]), lambda b, j: (b, j, 0))

    return pl.pallas_call(
        body, name="assemble_dp", grid=(nb, s // TM),
        in_specs=[spec(a) for a in flat],
        out_specs=pl.BlockSpec((1, TM, PW), lambda b, j: (b, j, 0)),
        out_shape=_sds((nb, s, PW), bf16),
        compiler_params=_cparams(("arbitrary", "arbitrary")),
    )(*flat)


def mod_fwd(c_rows, w_mod, b_cols):
    nl, _, wc = w_mod.shape
    nr = c_rows.shape[0]

    def body(c_ref, w_ref, b_ref, o_ref):
        o_ref[0] = _dot(_silu(c_ref[...]), w_ref[0], precision=HI) + b_ref[0]

    return pl.pallas_call(
        body, name="mod_fwd", grid=(nl,),
        in_specs=[_full((nr, D)), pl.BlockSpec((1, D, wc), lambda l: (l, 0, 0)), pl.BlockSpec((1, 1, wc), lambda l: (l, 0, 0))],
        out_specs=pl.BlockSpec((1, nr, wc), lambda l: (l, 0, 0)),
        out_shape=_sds((nl, nr, wc)),
        compiler_params=_cparams(("arbitrary",)),
    )(c_rows, w_mod, b_cols)


def mod_bwd(c_rows, w_mod, dm_cols, dm_full):
    nl, _, wc = w_mod.shape
    nr = c_rows.shape[0]

    def body(c_ref, w_ref, dmc_ref, dmf_ref, gw_ref, gb_ref, dcc_ref):
        @pl.when(pl.program_id(0) == 0)
        def _():
            dcc_ref[...] = jnp.zeros_like(dcc_ref)
        a = _silu(c_ref[...])
        dmc = dmc_ref[0]
        gw_ref[0] = _dot_tn(a, dmc, precision=HI)
        gb_ref[0] = jnp.sum(dmf_ref[0], axis=0, keepdims=True)
        dcc_ref[...] += _dot_nt(dmc[nr - 8:nr], w_ref[0], precision=HI)

    return pl.pallas_call(
        body, name="mod_bwd", grid=(nl,),
        in_specs=[_full((nr, D)), pl.BlockSpec((1, D, wc), lambda l: (l, 0, 0)),
                  pl.BlockSpec((1, nr, wc), lambda l: (l, 0, 0)), pl.BlockSpec((1, nr, 3 * D), lambda l: (l, 0, 0))],
        out_specs=[pl.BlockSpec((1, D, wc), lambda l: (l, 0, 0)), pl.BlockSpec((1, 1, 3 * D), lambda l: (l, 0, 0)),
                   _full((8, D))],
        out_shape=[_sds((nl, D, wc)), _sds((nl, 1, 3 * D)), _sds((8, D))],
        compiler_params=_cparams(("arbitrary",)),
    )(c_rows, w_mod, dm_cols, dm_full)


def cctx_grad(parts, c_ctx):
    def body(p_ref, c_ref, o_ref):
        tot = p_ref[0, 0:1, :]
        for k in (2, 4, 6):
            tot = tot + p_ref[k, 0:1, :]
        c = c_ref[...]
        sg = jax.nn.sigmoid(c)
        o_ref[...] = tot * (sg * (1.0 + c * (1.0 - sg)))

    return pl.pallas_call(body, name="cctx_grad", out_shape=_sds((1, D)))(parts, c_ctx)


def sum_lead(x, out_dtype=f32, tr=256, tc=None):
    k, r, c = x.shape
    tr = min(tr, r)
    tc = c if tc is None else tc
    assert r % tr == 0 and c % tc == 0

    def body(x_ref, o_ref):
        tot = x_ref[0].astype(f32)
        for i in range(1, k):
            tot = tot + x_ref[i].astype(f32)
        o_ref[...] = tot.astype(out_dtype)

    return pl.pallas_call(
        body, name="sum_lead", grid=(r // tr, c // tc),
        in_specs=[pl.BlockSpec((k, tr, tc), lambda i, j: (0, i, j))],
        out_specs=pl.BlockSpec((tr, tc), lambda i, j: (i, j)),
        out_shape=_sds((r, c), out_dtype),
        compiler_params=_cparams(("arbitrary", "arbitrary")),
    )(x)


def adamw(w, m, v, g1, g2=None, tr=256, block=None):
    if block is None:
        block = (1,) * (w.ndim - 2) + (min(tr, w.shape[-2]), w.shape[-1])
    assert len(block) == w.ndim and all(d % b == 0 for d, b in zip(w.shape, block))
    two = g2 is not None
    c1 = 1.0 / (1.0 - ADAM_B1 ** ADAM_STEP)
    c2 = 1.0 / (1.0 - ADAM_B2 ** ADAM_STEP)

    def body(*refs):
        w_ref, m_ref, v_ref, g_ref = refs[:4]
        g = g_ref[...]
        if two:
            g = g + refs[4][...]
        go_ref, d_ref, mo_ref, vo_ref = refs[-4:]
        mn = ADAM_B1 * m_ref[...] + (1.0 - ADAM_B1) * g
        vn = ADAM_B2 * v_ref[...] + (1.0 - ADAM_B2) * (g * g)
        go_ref[...] = g
        mo_ref[...] = mn
        vo_ref[...] = vn
        d_ref[...] = -ADAM_LR * ((mn * c1) / (jnp.sqrt(vn * c2) + ADAM_EPS) + ADAM_WD * w_ref[...])

    blk = pl.BlockSpec(block, lambda *i: i)
    grid = tuple(d // b for d, b in zip(w.shape, block))
    args = [w, m, v, g1] + ([g2] if two else [])
    return pl.pallas_call(
        body, name="adamw", grid=grid,
        in_specs=[blk] * len(args), out_specs=[blk] * 4, out_shape=[_sds(w.shape)] * 4,
        compiler_params=_cparams(("arbitrary",) * len(grid)),
    )(*args)


def _my_pos():
    return lax.axis_index("x"), lax.axis_index("y"), lax.axis_index("c")


GROUP_SIZE = {"devices": N_DEV, "chips": N_CHIPS, "cores": 2}


def _peer_exchange(kind, group, src_refs, dst_refs, send_sems, recv_sems, local_sems):
    mx, my, mc = _my_pos()
    n = GROUP_SIZE[group]
    if group == "devices":
        me = 4 * mx + 2 * my + mc
    elif group == "chips":
        me = 2 * mx + my
    else:
        me = mc

    def peer(k):
        if group == "devices":
            return (mx ^ (k >> 2), my ^ ((k >> 1) & 1), mc ^ (k & 1))
        if group == "chips":
            return (mx ^ (k >> 1), my ^ (k & 1), mc)
        return (mx, my, mc ^ k)

    def copies():
        local, sends, recvs = [], [], []
        for i, (src, dst) in enumerate(zip(src_refs, dst_refs)):
            def part(k):
                return src.at[k] if kind == "scatter" else src

            def slab(k):
                return dst if kind == "send" else dst.at[k]

            if kind != "send":
                local.append(pltpu.make_async_copy(part(me), dst.at[me], local_sems.at[i]))
            for k in range(1, n):
                sem = dict(send_sem=send_sems.at[i, k - 1], recv_sem=recv_sems.at[i, k - 1], device_id_type=MESH)
                sends.append(pltpu.make_async_remote_copy(src_ref=part(me ^ k), dst_ref=slab(me), device_id=peer(k), **sem))
                recvs.append(pltpu.make_async_remote_copy(src_ref=part(me ^ k), dst_ref=slab(me ^ k),
                                                          device_id=(mx, my, mc), **sem))
        return local, sends, recvs

    def start():
        local, sends, _ = copies()
        for cp in local + sends:
            cp.start()

    def wait():
        local, sends, recvs = copies()
        for cp in recvs:
            cp.wait_recv()
        for cp in sends:
            cp.wait_send()
        for cp in local:
            cp.wait()

    return start, wait


def _exchange_scratch(group, n):
    k = GROUP_SIZE[group] - 1
    return [pltpu.SemaphoreType.DMA((n, k)), pltpu.SemaphoreType.DMA((n, k)), pltpu.SemaphoreType.DMA((n,))]


def _exchange_shapes(kind, group, arrs):
    return [_sds(((GROUP_SIZE[group],) + a.shape) if kind == "gather" else a.shape, a.dtype) for a in arrs]


def exchange(name, parts):
    counts = [len(arrs) for _, _, arrs in parts]
    total = sum(counts)

    def body(*refs):
        srcs, dsts, sems = refs[:total], refs[total:2 * total], refs[2 * total:]
        ops, at = [], 0
        for j, (kind, group, arrs) in enumerate(parts):
            ops.append(_peer_exchange(kind, group, srcs[at:at + counts[j]], dsts[at:at + counts[j]], *sems[3 * j:3 * j + 3]))
            at += counts[j]
        for start, _ in ops:
            start()
        for _, wait in ops:
            wait()

    any_ = pl.BlockSpec(memory_space=pl.ANY)
    flat = [a for _, _, arrs in parts for a in arrs]
    outs = pl.pallas_call(
        body, name=name, out_shape=[sh for kind, group, arrs in parts for sh in _exchange_shapes(kind, group, arrs)],
        in_specs=[any_] * total, out_specs=[any_] * total,
        scratch_shapes=[sc for _, group, arrs in parts for sc in _exchange_scratch(group, len(arrs))],
    )(*flat)
    res, at = [], 0
    for cnt in counts:
        res.append(list(outs[at:at + cnt]))
        at += cnt
    return res


def gather8(x):
    return exchange("gather8", [("gather", "devices", [x])])[0][0]


PACK_ROWS = 64
SMALL = ("c_ctx", "b_mod", "g_pre", "g_post", "ret_norm_g", "sg_w", "sg_b", "sc_conv_w", "gdn_conv_w",
         "gdn_a_log", "gdn_dt_bias", "gdn_norm_g")


def _pack(arrs, width=D, mult=PACK_ROWS):
    rows = []
    for a in arrs:
        flat = a.reshape(-1)
        pad = (-flat.shape[0]) % width
        rows.append(jnp.pad(flat, (0, pad)).reshape(-1, width))
    out = jnp.concatenate(rows, axis=0)
    return jnp.pad(out, ((0, (-out.shape[0]) % mult), (0, 0)))


def _unpack(packed, shapes, width=D):
    outs, r = [], 0
    for shp in shapes:
        size = int(np.prod(shp))
        nr = -(-size // width)
        outs.append(packed[r:r + nr].reshape(-1)[:size].reshape(shp))
        r += nr
    return outs


def kernel(x, c, ctx, c_ctx, w_mod, b_mod, g_pre, g_post, w_in, w_out, ret_norm_g, sg_w, sg_b, sc_conv_w, gdn_conv_w, gdn_a_log, gdn_dt_bias, gdn_norm_g, loss_target, m_c_ctx, m_w_mod, m_b_mod, m_g_pre, m_g_post, m_w_in, m_w_out, m_ret_norm_g, m_sg_w, m_sg_b, m_sc_conv_w, m_gdn_conv_w, m_gdn_a_log, m_gdn_dt_bias, m_gdn_norm_g, v_c_ctx, v_w_mod, v_b_mod, v_g_pre, v_g_post, v_w_in, v_w_out, v_ret_norm_g, v_sg_w, v_sg_b, v_sc_conv_w, v_gdn_conv_w, v_gdn_a_log, v_gdn_dt_bias, v_gdn_norm_g):
    weights = dict(c_ctx=c_ctx, w_mod=w_mod, b_mod=b_mod, g_pre=g_pre, g_post=g_post, w_in=w_in, w_out=w_out,
                   ret_norm_g=ret_norm_g, sg_w=sg_w, sg_b=sg_b, sc_conv_w=sc_conv_w, gdn_conv_w=gdn_conv_w,
                   gdn_a_log=gdn_a_log, gdn_dt_bias=gdn_dt_bias, gdn_norm_g=gdn_norm_g)
    mom = dict(c_ctx=m_c_ctx, w_mod=m_w_mod, b_mod=m_b_mod, g_pre=m_g_pre, g_post=m_g_post, w_in=m_w_in,
               w_out=m_w_out, ret_norm_g=m_ret_norm_g, sg_w=m_sg_w, sg_b=m_sg_b, sc_conv_w=m_sc_conv_w,
               gdn_conv_w=m_gdn_conv_w, gdn_a_log=m_gdn_a_log, gdn_dt_bias=m_gdn_dt_bias, gdn_norm_g=m_gdn_norm_g)
    var = dict(c_ctx=v_c_ctx, w_mod=v_w_mod, b_mod=v_b_mod, g_pre=v_g_pre, g_post=v_g_post, w_in=v_w_in,
               w_out=v_w_out, ret_norm_g=v_ret_norm_g, sg_w=v_sg_w, sg_b=v_sg_b, sc_conv_w=v_sc_conv_w,
               gdn_conv_w=v_gdn_conv_w, gdn_a_log=v_gdn_a_log, gdn_dt_bias=v_gdn_dt_bias, gdn_norm_g=v_gdn_norm_g)

    nb, t_lat, _ = x.shape
    t_ctx = ctx.shape[1]
    s = t_ctx + t_lat
    n = nb * s
    sb = s // TM
    nl = w_in.shape[0]
    wc_in = w_in.shape[2]
    wc_mod = w_mod.shape[2]
    rows_out = w_out.shape[1]
    n_all = nb * N_DEV
    mx, my, mc = _my_pos()
    chip = 2 * mx + my
    dev = 2 * chip + mc

    sg_c = _sg_consts()
    bd = jnp.asarray(_block_diag())
    ret_c = _ret_consts(nb)
    gdn_c = _gdn_consts(nb)
    cos, sins = _rope_tables(t_lat, t_ctx)

    w_in_b, w_out_b = w_in.astype(bf16), w_out.astype(bf16)
    pre = _pack([c, sc_conv_w, gdn_conv_w], mult=8)
    (pre_all,), w0_parts = exchange("startup_gather", [("gather", "devices", [pre]), ("gather", "chips", [w_in_b[0]])])
    c_parts, scw_parts, gcw_parts = [], [], []
    for k in range(N_DEV):
        ck, sk, gk = _unpack(pre_all[k], [c.shape, sc_conv_w.shape, gdn_conv_w.shape])
        c_parts.append(ck)
        if k % 2 == 0:
            scw_parts.append(sk)
            gcw_parts.append(gk)
    c_all = jnp.concatenate(c_parts, axis=0)
    sc_w_full = jnp.concatenate(scw_parts, axis=-1)
    gdn_w_full = jnp.concatenate(gcw_parts, axis=-1)
    c_rows = jnp.concatenate([c_all, c_ctx[None, :], jnp.zeros((7, D), f32)], axis=0)

    b_cols = lax.dynamic_slice_in_dim(b_mod, chip * wc_mod, wc_mod, axis=1)[:, None, :]
    mod_part = mod_fwd(c_rows, w_mod, b_cols)
    mod_all = gather8(mod_part)
    mod = jnp.concatenate([mod_all[2 * k] for k in range(N_CHIPS)], axis=-1)
    my_rows = jnp.concatenate([lax.dynamic_slice_in_dim(mod, dev * nb, nb, axis=1), mod[:, n_all:n_all + 1]], axis=1)
    shift_t = my_rows[:, :, None, 0:D]
    scale_t = my_rows[:, :, None, D:2 * D]
    gate_t = my_rows[:, :, None, 2 * D:3 * D]

    w_in_full, w_out_full = [None] * nl, [None] * nl
    w_in_full[0] = place_weights(w0_parts[0])

    alog = jnp.pad(gdn_a_log.reshape(nl, 1, 8), ((0, 0), (0, 0), (0, 120)))
    dtb = jnp.pad(gdn_dt_bias.reshape(nl, 1, 8), ((0, 0), (0, 0), (0, 120)))
    gdn_ng = jnp.tile(gdn_norm_g, (1, NH))[:, None, :]
    ret_ng = ret_norm_g[:, None, :]

    xs = jnp.concatenate([ctx, x], axis=1).reshape(n, D)
    saved = []
    for l in range(nl):
        p, h = inproj_fwd(xs, shift_t[l], scale_t[l], g_pre[l][None, :], w_in_full[l], nb, sb)
        p3 = p.reshape(nb, s, PW)
        ro_f, ro_b, rs_all = ret_scan_fwd(p3, cos, sins, ret_c, t_ctx)
        y_ret = mix_finish_fwd(_ret_finish, "ret_finish_fwd", ro_f, ro_b, p3, 3, ret_ng[l], bd)
        y_sg = sg_fwd(p3, sg_w[l], sg_b[l], *sg_c)
        y_sc = sc_fwd(p3, sc_w_full[l], t_ctx)
        c3 = gdn_conv_fwd(p3, gdn_w_full[l], t_ctx)
        riding = [w_out_b[l]] + ([w_in_b[l + 1]] if l + 1 < nl else [])
        go_f, go_b, *gs_all = gdn_scan_fwd(c3, p3, alog[l], dtb[l], gdn_c, t_ctx, ("gather", riding))
        w_out_full[l] = gs_all[2].reshape(D, D)
        if l + 1 < nl:
            w_in_full[l + 1] = place_weights(gs_all[3])
        gs_all = gs_all[:2]
        y_gdn = mix_finish_fwd(_gdn_finish, "gdn_finish_fwd", go_f, go_b, p3, 14, gdn_ng[l], bd)
        ys = [a.reshape(n, BW) for a in (y_ret, y_sg, y_sc, y_gdn)]
        x_new, o = outproj_fwd(ys, w_out_full[l], xs, gate_t[l], g_post[l][None, :], nb, sb)
        saved.append(dict(x=xs, h=h, p3=p3, ro=(ro_f, ro_b), rs=rs_all, c=c3, go=(go_f, go_b), gs=gs_all,
                          ys=ys, o=o))
        xs = x_new

    dx3, loss_part = loss_head(xs.reshape(nb, s, D), loss_target, t_ctx)
    loss = lax.psum(loss_part[0, 0], ("x", "y", "c"))

    dxs = dx3.reshape(n, D)
    g_small = {k: [None] * nl for k in SMALL if k not in ("c_ctx", "b_mod")}
    dm_rows = [None] * nl
    slab_in = None
    got_in, got_out = [None] * nl, [None] * nl
    for l in reversed(range(nl)):
        sv = saved[l]
        p3 = sv["p3"]
        dy, gw_out, dg_post, dgate = outproj_bwd(dxs, sv["o"], gate_t[l], g_post[l][None, :], sv["ys"], w_out_full[l], nb, sb)
        dy3 = dy.reshape(nb, s, D)
        r_do, r_dz, d_rng = mix_finish_bwd(_ret_finish, "ret_finish_bwd", *sv["ro"], p3, 3, ret_ng[l], bd, dy3, 0)
        r_d = ret_scan_bwd(p3, cos, sins, ret_c, sv["rs"], r_do, t_ctx)
        s_du, s_dv, s_dz, d_sgw, d_sgb = sg_bwd(p3, sg_w[l], sg_b[l], *sg_c, dy3)
        c_db, c_dc, c_dh, c_dz, d_scw = sc_bwd(p3, sc_w_full[l], dy3, t_ctx)
        g_do, g_dz, d_gng = mix_finish_bwd(_gdn_finish, "gdn_finish_bwd", *sv["go"], p3, 14, gdn_ng[l], bd, dy3, 3)
        riding = [gw_out.reshape(N_CHIPS, rows_out, D).astype(bf16)] + ([] if slab_in is None else [slab_in])
        g_dcf, g_dgf, g_dcb, g_dgb, g_dal, g_ddt, *got = gdn_scan_bwd(
            sv["c"], p3, alog[l], dtb[l], gdn_c, *sv["gs"], g_do, t_ctx, ("scatter", riding))
        got_out[l] = got[0]
        if slab_in is not None:
            got_in[l + 1] = got[1]
        gx, d_gcw = gdn_conv_bwd(p3, gdn_w_full[l], g_dcf, g_dcb, t_ctx)
        dp3 = assemble_dp([(r_d[0], r_d[3]), (r_d[1], r_d[4]), (r_d[2], r_d[5])],
                          [r_dz, s_du, s_dv, s_dz, c_db, c_dc, c_dh, c_dz], [gx], [g_dz], [g_dgf, g_dgb])
        dp = dp3.reshape(n, PW)
        gw_in = dw_in(sv["h"], dp)
        slab_in = jnp.stack([gw_in[k * wc_in:(k + 1) * wc_in] for k in range(N_CHIPS)])
        dxs, dg_pre, dshift, dscale, *got = inproj_bwd_x(dp, w_in_full[l], sv["x"], scale_t[l], g_pre[l][None, :], dxs, nb, sb,
                                                         ("scatter", [slab_in]) if l == 0 else None)
        if l == 0:
            got_in[0] = got[0]
        g_small["g_pre"][l] = dg_pre[0]
        g_small["g_post"][l] = dg_post[0]
        g_small["ret_norm_g"][l] = d_rng[0]
        g_small["sg_w"][l] = d_sgw
        g_small["sg_b"][l] = d_sgb
        g_small["sc_conv_w"][l] = d_scw
        g_small["gdn_conv_w"][l] = d_gcw
        g_small["gdn_a_log"][l] = g_dal[0, :8].reshape(2, NH)
        g_small["gdn_dt_bias"][l] = g_ddt[0, :8].reshape(2, NH)
        g_small["gdn_norm_g"][l] = d_gng[0].reshape(NH, HD)
        dm_rows[l] = jnp.concatenate([dshift, dscale, dgate], axis=-1)[:nb + 1]
    grad_x = dxs.reshape(nb, s, D)[:, t_ctx:, :]

    g_small = {k: jnp.stack(v) for k, v in g_small.items()}
    dm_rows = jnp.stack(dm_rows)
    names2 = [k for k in SMALL if k not in ("c_ctx", "b_mod")]
    pack_sum = _pack([g_small[k] for k in names2] + [dm_rows[:, nb:]])
    pack_own = _pack([dm_rows[:, :nb]], mult=8)
    rs = -(-pack_sum.shape[0] // (8 * N_DEV)) * 8
    slabs_sum = jnp.pad(pack_sum, ((0, N_DEV * rs - pack_sum.shape[0]), (0, 0))).reshape(N_DEV, rs, D)
    ((got_small,),) = exchange("tail_scatter", [("scatter", "devices", [slabs_sum])])
    my_slab = sum_lead(got_small, tr=rs)
    gin_mine = jnp.stack([sum_lead(a, tr=wc_in, tc=256) for a in got_in], axis=1)
    gout_mine = jnp.stack([sum_lead(a) for a in got_out])
    (all2,), (gin_sib, gout_sib) = exchange("tail_gather", [
        ("gather", "devices", [jnp.concatenate([my_slab, pack_own], axis=0)]), ("send", "cores", [gin_mine, gout_mine])])
    tot2 = all2[:, :rs].reshape(N_DEV * rs, D)
    outs2 = _unpack(tot2, [g_small[k].shape for k in names2] + [(nl, 1, 3 * D)])
    grads = dict(zip(names2, outs2[:-1]))
    dm_own = jnp.stack([_unpack(all2[k, rs:], [(nl, nb, 3 * D)])[0] for k in range(N_DEV)])
    dm_own = jnp.transpose(dm_own, (1, 0, 2, 3)).reshape(nl, n_all, 3 * D)
    dm_all = jnp.concatenate([dm_own, jnp.pad(outs2[-1], ((0, 0), (0, 7), (0, 0)))], axis=1)
    grads["gdn_norm_g"] = sum_lead(jnp.transpose(grads["gdn_norm_g"], (1, 0, 2)), tr=nl)
    for k in ("sc_conv_w", "gdn_conv_w"):
        wc = weights[k].shape[2]
        grads[k] = lax.dynamic_slice_in_dim(grads[k], chip * wc, wc, axis=2)

    dm_cols = lax.dynamic_slice_in_dim(dm_all, chip * wc_mod, wc_mod, axis=2)
    g_w_mod, g_b_mod, dcc_part = mod_bwd(c_rows, w_mod, dm_cols, dm_all)
    grads["b_mod"] = g_b_mod[:, 0, :]
    grads["c_ctx"] = cctx_grad(gather8(dcc_part), c_ctx[None, :])[0]

    res = {}
    w_in_t, m_in_t, v_in_t = [jnp.transpose(a, (2, 0, 1)) for a in (w_in, m_w_in, v_w_in)]
    res["w_in"] = [jnp.transpose(a, (1, 2, 0)) for a in
                   adamw(w_in_t, m_in_t, v_in_t, gin_mine, gin_sib, block=(wc_in // 4, nl, 256))]
    res["w_out"] = adamw(w_out, m_w_out, v_w_out, gout_mine, gout_sib)
    res["w_mod"] = adamw(w_mod, m_w_mod, v_w_mod, g_w_mod)
    shapes = [weights[k].shape for k in SMALL]
    small = adamw(_pack([weights[k] for k in SMALL]), _pack([mom[k] for k in SMALL]), _pack([var[k] for k in SMALL]),
                  _pack([grads[k].reshape(weights[k].shape) for k in SMALL]), tr=PACK_ROWS)
    small = [_unpack(a, shapes) for a in small]
    for i, k in enumerate(SMALL):
        res[k] = [small[j][i] for j in range(4)]

    order = ["c_ctx", "w_mod", "b_mod", "g_pre", "g_post", "w_in", "w_out", "ret_norm_g", "sg_w", "sg_b", "sc_conv_w",
             "gdn_conv_w", "gdn_a_log", "gdn_dt_bias", "gdn_norm_g"]
    return (loss, grad_x, *[res[k][0] for k in order], *[res[k][1] for k in order], *[res[k][2] for k in order],
            *[res[k][3] for k in order])
```
